```python
import jax, jax.numpy as jnp
from jax import lax
import numpy as np

D_MODEL = 1024
BATCH = 32
SEQ = 2048
DEPTH = 2

N_A_LAYERS = DEPTH // 2
N_B_LAYERS = DEPTH - N_A_LAYERS

NORM_EPS = 1e-6

A_WINDOWS = (128, 512, 2048)
A_DILATIONS = (1, 4, 16)
A_GROUPS = 3
A_HEADS = 8
A_HEAD_DIM = 128
A_WIDTH = A_HEADS * A_HEAD_DIM
A_ROT_DIM = A_HEAD_DIM // 4
A_ROPE_THETA = 500000.0
BAND_BLOCK = 128
A_IN_WIDTH = A_GROUPS * 3 * A_WIDTH + A_WIDTH

B_HEADS = 16
B_NOPE = 64
B_ROPE = 32
B_QK_DIM = B_NOPE + B_ROPE
B_VDIM = 64
B_WIDTH = B_HEADS * B_VDIM
B_Q_LORA = 384
B_KV_LORA = 256
B_ROPE_THETA = 10000.0
B_IN_WIDTH = B_Q_LORA + B_WIDTH
ATTN_BLOCK = 128

kernel_name = "yoco_dilated_swa_mla_hybrid"


def rms_norm(x, g):
    xf = x.astype(jnp.float32)
    y = xf * lax.rsqrt(jnp.mean(xf * xf, axis=-1, keepdims=True) + NORM_EPS)
    return (y * g.astype(jnp.float32)).astype(x.dtype)


def rope(x, positions, theta):
    dim = x.shape[-1]
    half = dim // 2
    inv_freq = 1.0 / (theta ** (jnp.arange(half, dtype=jnp.float32) * (2.0 / dim)))
    ang = positions.astype(jnp.float32)[..., None] * inv_freq
    cos = jnp.cos(ang)[:, :, None, :]
    sin = jnp.sin(ang)[:, :, None, :]
    xf = x.astype(jnp.float32)
    x1, x2 = xf[..., :half], xf[..., half:]
    out = jnp.concatenate([x1 * cos - x2 * sin, x2 * cos + x1 * sin], axis=-1)
    return out.astype(x.dtype)


def partial_rope(x, positions):
    return jnp.concatenate(
        [rope(x[..., :A_ROT_DIM], positions, A_ROPE_THETA), x[..., A_ROT_DIM:]], axis=-1)


def dilated_window_attention(q, k, v, window, dilation):
    B, T, H, Dh = q.shape
    L = T // dilation
    w_sub = window // dilation
    Q = BAND_BLOCK
    nb = -(-L // Q)
    Lp = nb * Q

    def split(a):
        a = a.reshape(B, L, dilation, H, Dh).transpose(0, 2, 1, 3, 4)
        a = jnp.pad(a, ((0, 0), (0, 0), (0, Lp - L), (0, 0), (0, 0)))
        return a.reshape(B, dilation, nb, Q, H, Dh)

    def band(a):
        prev = jnp.pad(a, ((0, 0), (0, 0), (1, 0), (0, 0), (0, 0), (0, 0)))[:, :, :-1]
        return jnp.concatenate([prev, a], axis=3)

    qb = split(q)
    kb = band(split(k))
    vb = band(split(v))
    s = jnp.einsum('brnqhd,brnkhd->brnhqk', qb, kb,
                   preferred_element_type=jnp.float32) * (A_HEAD_DIM ** -0.5)
    qi = jnp.arange(Q)[:, None]
    ki = jnp.arange(2 * Q)[None, :]
    dist = Q + qi - ki
    key_pos = (jnp.arange(nb)[:, None, None] - 1) * Q + ki[None]
    allowed = (dist >= 0)[None] & (dist <= w_sub)[None] & (key_pos >= 0)
    s = jnp.where(allowed[None, None, :, None], s, -jnp.inf)
    m = jnp.max(s, axis=-1, keepdims=True)
    p = jnp.exp(s - m)
    l = jnp.sum(p, axis=-1)
    o = jnp.einsum('brnhqk,brnkhd->brnqhd', p, vb.astype(jnp.float32))
    l_q = l.transpose(0, 1, 2, 4, 3)
    o = o / l_q[..., None]
    lse = (m[..., 0] + jnp.log(l)).transpose(0, 1, 2, 4, 3)
    o = o.reshape(B, dilation, Lp, H, Dh)[:, :, :L].transpose(0, 2, 1, 3, 4).reshape(B, T, H, Dh)
    lse = lse.reshape(B, dilation, Lp, H)[:, :, :L].transpose(0, 2, 1, 3).reshape(B, T, H)
    return o, lse


def mixer_a(h, positions, w_in, w_out):
    B, T, _ = h.shape
    proj = h @ w_in
    qkv = proj[..., :A_GROUPS * 3 * A_WIDTH].reshape(B, T, A_GROUPS, 3, A_HEADS, A_HEAD_DIM)
    z = proj[..., A_GROUPS * 3 * A_WIDTH:]
    outs, lses = [], []
    for g in range(A_GROUPS):
        q = partial_rope(qkv[:, :, g, 0], positions)
        k = partial_rope(qkv[:, :, g, 1], positions)
        v = qkv[:, :, g, 2]
        o, lse = dilated_window_attention(q, k, v, A_WINDOWS[g], A_DILATIONS[g])
        outs.append(o)
        lses.append(lse)
    wts = jax.nn.softmax(jnp.stack(lses, axis=0), axis=0)
    o = jnp.einsum('gbth,gbthd->bthd', wts, jnp.stack(outs, axis=0))
    y = o.reshape(B, T, A_WIDTH).astype(h.dtype) * jax.nn.silu(z)
    return y @ w_out


def shared_latent_kv(h, positions, kv_norm, kv_w_down, kv_latent_norm, kv_w_up):
    B, T, _ = h.shape
    hn = rms_norm(h, kv_norm)
    ckr = hn @ kv_w_down
    c_kv = rms_norm(ckr[..., :B_KV_LORA], kv_latent_norm)
    k_rope = rope(ckr[..., B_KV_LORA:][:, :, None, :], positions, B_ROPE_THETA)
    kv = (c_kv @ kv_w_up).reshape(B, T, B_HEADS, B_NOPE + B_VDIM)
    k = jnp.concatenate(
        [kv[..., :B_NOPE], jnp.broadcast_to(k_rope, (B, T, B_HEADS, B_ROPE))], axis=-1)
    v = kv[..., B_NOPE:]
    return k, v


def causal_block_attention(q, k, v):
    B, T, H, Dq = q.shape
    Dv = v.shape[-1]
    nb = T // ATTN_BLOCK
    q_blocks = q.reshape(B, nb, ATTN_BLOCK, H, Dq).transpose(1, 0, 2, 3, 4)
    k_pos = jnp.arange(T)
    vf = v.astype(jnp.float32)

    def one_block(args):
        qb, idx = args
        s = jnp.einsum('bqhd,bkhd->bhqk', qb, k,
                       preferred_element_type=jnp.float32) * (B_QK_DIM ** -0.5)
        q_pos = idx * ATTN_BLOCK + jnp.arange(ATTN_BLOCK)
        mask = k_pos[None, :] <= q_pos[:, None]
        p = jax.nn.softmax(jnp.where(mask[None, None], s, -jnp.inf), axis=-1)
        return jnp.einsum('bhqk,bkhd->bqhd', p, vf)

    o = lax.map(one_block, (q_blocks, jnp.arange(nb)))
    return o.transpose(1, 0, 2, 3, 4).reshape(B, T, H, Dv)


def mixer_b(h, positions, k, v, w_in, q_norm, w_q_up, w_out):
    B, T, _ = h.shape
    proj = h @ w_in
    c_q = rms_norm(proj[..., :B_Q_LORA], q_norm)
    z = proj[..., B_Q_LORA:]
    q = (c_q @ w_q_up).reshape(B, T, B_HEADS, B_QK_DIM)
    q = jnp.concatenate([q[..., :B_NOPE], rope(q[..., B_NOPE:], positions, B_ROPE_THETA)], axis=-1)
    o = causal_block_attention(q, k, v)
    y = o.reshape(B, T, B_WIDTH).astype(h.dtype) * jax.nn.silu(z)
    return y @ w_out


def _fwd_setup_inputs(seed: int = 0) -> dict:
    key = jax.random.key(seed)
    ks = jax.random.split(key, 20)
    f32 = jnp.float32

    def w(k, shape, fan_in):
        return jax.random.normal(k, shape, f32) * (fan_in ** -0.5)

    def gain(k, shape):
        return 1.0 + 0.01 * jax.random.normal(k, shape, f32)

    x = jax.random.normal(ks[0], (BATCH, SEQ, D_MODEL), f32)
    offsets = jax.random.randint(ks[1], (BATCH, 1), 0, 4096, dtype=jnp.int32)
    positions = offsets + jnp.arange(SEQ, dtype=jnp.int32)[None, :]
    return {
        "x": x,
        "positions": positions,
        "a_pre_norm": gain(ks[2], (N_A_LAYERS, D_MODEL)),
        "a_w_in": w(ks[3], (N_A_LAYERS, D_MODEL, A_IN_WIDTH), D_MODEL),
        "a_w_out": w(ks[4], (N_A_LAYERS, A_WIDTH, D_MODEL), A_WIDTH),
        "a_post_norm": gain(ks[5], (N_A_LAYERS, D_MODEL)),
        "kv_norm": gain(ks[6], (D_MODEL,)),
        "kv_w_down": w(ks[7], (D_MODEL, B_KV_LORA + B_ROPE), D_MODEL),
        "kv_latent_norm": gain(ks[8], (B_KV_LORA,)),
        "kv_w_up": w(ks[9], (B_KV_LORA, B_HEADS * (B_NOPE + B_VDIM)), B_KV_LORA),
        "b_pre_norm": gain(ks[10], (N_B_LAYERS, D_MODEL)),
        "b_w_in": w(ks[11], (N_B_LAYERS, D_MODEL, B_IN_WIDTH), D_MODEL),
        "b_q_norm": gain(ks[12], (N_B_LAYERS, B_Q_LORA)),
        "b_w_q_up": w(ks[13], (N_B_LAYERS, B_Q_LORA, B_HEADS * B_QK_DIM), B_Q_LORA),
        "b_w_out": w(ks[14], (N_B_LAYERS, B_WIDTH, D_MODEL), B_WIDTH),
        "b_post_norm": gain(ks[15], (N_B_LAYERS, D_MODEL)),
    }


def _fwd_reference(x, positions, a_pre_norm, a_w_in, a_w_out, a_post_norm,
              kv_norm, kv_w_down, kv_latent_norm, kv_w_up,
              b_pre_norm, b_w_in, b_q_norm, b_w_q_up, b_w_out, b_post_norm):
    h = x
    k_shared = None
    v_shared = None
    for layer in range(DEPTH):
        if layer < N_A_LAYERS:
            i = layer
            y = mixer_a(rms_norm(h, a_pre_norm[i]), positions, a_w_in[i], a_w_out[i])
            h = h + rms_norm(y, a_post_norm[i])
        else:
            if layer == N_A_LAYERS:
                k_shared, v_shared = shared_latent_kv(
                    h, positions, kv_norm, kv_w_down, kv_latent_norm, kv_w_up)
            i = layer - N_A_LAYERS
            y = mixer_b(rms_norm(h, b_pre_norm[i]), positions, k_shared, v_shared,
                        b_w_in[i], b_q_norm[i], b_w_q_up[i], b_w_out[i])
            h = h + rms_norm(y, b_post_norm[i])
    return h


import jax as _jax
import jax.numpy as _jnp

TWIN_FORMAT = 'train_step'
FWD_PARAMS = ['x', 'positions', 'a_pre_norm', 'a_w_in', 'a_w_out', 'a_post_norm', 'kv_norm', 'kv_w_down', 'kv_latent_norm', 'kv_w_up', 'b_pre_norm', 'b_w_in', 'b_q_norm', 'b_w_q_up', 'b_w_out', 'b_post_norm']
TWIN_WEIGHTS = ['a_pre_norm', 'a_w_in', 'a_w_out', 'a_post_norm', 'kv_norm', 'kv_w_down', 'kv_latent_norm', 'kv_w_up', 'b_pre_norm', 'b_w_in', 'b_q_norm', 'b_w_q_up', 'b_w_out', 'b_post_norm']
TWIN_DIFF_INPUT = 'x'
TWIN_INPUTS = ['x', 'positions', 'a_pre_norm', 'a_w_in', 'a_w_out', 'a_post_norm', 'kv_norm', 'kv_w_down', 'kv_latent_norm', 'kv_w_up', 'b_pre_norm', 'b_w_in', 'b_q_norm', 'b_w_q_up', 'b_w_out', 'b_post_norm', 'loss_target', 'm_a_pre_norm', 'm_a_w_in', 'm_a_w_out', 'm_a_post_norm', 'm_kv_norm', 'm_kv_w_down', 'm_kv_latent_norm', 'm_kv_w_up', 'm_b_pre_norm', 'm_b_w_in', 'm_b_q_norm', 'm_b_w_q_up', 'm_b_w_out', 'm_b_post_norm', 'v_a_pre_norm', 'v_a_w_in', 'v_a_w_out', 'v_a_post_norm', 'v_kv_norm', 'v_kv_w_down', 'v_kv_latent_norm', 'v_kv_w_up', 'v_b_pre_norm', 'v_b_w_in', 'v_b_q_norm', 'v_b_w_q_up', 'v_b_w_out', 'v_b_post_norm']
TWIN_OUTPUTS = ['loss', 'grad_x', 'grad_a_pre_norm', 'grad_a_w_in', 'grad_a_w_out', 'grad_a_post_norm', 'grad_kv_norm', 'grad_kv_w_down', 'grad_kv_latent_norm', 'grad_kv_w_up', 'grad_b_pre_norm', 'grad_b_w_in', 'grad_b_q_norm', 'grad_b_w_q_up', 'grad_b_w_out', 'grad_b_post_norm', 'delta_a_pre_norm', 'delta_a_w_in', 'delta_a_w_out', 'delta_a_post_norm', 'delta_kv_norm', 'delta_kv_w_down', 'delta_kv_latent_norm', 'delta_kv_w_up', 'delta_b_pre_norm', 'delta_b_w_in', 'delta_b_q_norm', 'delta_b_w_q_up', 'delta_b_w_out', 'delta_b_post_norm', 'new_m_a_pre_norm', 'new_m_a_w_in', 'new_m_a_w_out', 'new_m_a_post_norm', 'new_m_kv_norm', 'new_m_kv_w_down', 'new_m_kv_latent_norm', 'new_m_kv_w_up', 'new_m_b_pre_norm', 'new_m_b_w_in', 'new_m_b_q_norm', 'new_m_b_w_q_up', 'new_m_b_w_out', 'new_m_b_post_norm', 'new_v_a_pre_norm', 'new_v_a_w_in', 'new_v_a_w_out', 'new_v_a_post_norm', 'new_v_kv_norm', 'new_v_kv_w_down', 'new_v_kv_latent_norm', 'new_v_kv_w_up', 'new_v_b_pre_norm', 'new_v_b_w_in', 'new_v_b_q_norm', 'new_v_b_w_q_up', 'new_v_b_w_out', 'new_v_b_post_norm']
TWIN_LEAF_KINDS = {'loss': 'loss', 'grad_x': 'grad_x', 'grad_a_pre_norm': 'grad_w', 'grad_a_w_in': 'grad_w', 'grad_a_w_out': 'grad_w', 'grad_a_post_norm': 'grad_w', 'grad_kv_norm': 'grad_w', 'grad_kv_w_down': 'grad_w', 'grad_kv_latent_norm': 'grad_w', 'grad_kv_w_up': 'grad_w', 'grad_b_pre_norm': 'grad_w', 'grad_b_w_in': 'grad_w', 'grad_b_q_norm': 'grad_w', 'grad_b_w_q_up': 'grad_w', 'grad_b_w_out': 'grad_w', 'grad_b_post_norm': 'grad_w', 'delta_a_pre_norm': 'delta_w', 'delta_a_w_in': 'delta_w', 'delta_a_w_out': 'delta_w', 'delta_a_post_norm': 'delta_w', 'delta_kv_norm': 'delta_w', 'delta_kv_w_down': 'delta_w', 'delta_kv_latent_norm': 'delta_w', 'delta_kv_w_up': 'delta_w', 'delta_b_pre_norm': 'delta_w', 'delta_b_w_in': 'delta_w', 'delta_b_q_norm': 'delta_w', 'delta_b_w_q_up': 'delta_w', 'delta_b_w_out': 'delta_w', 'delta_b_post_norm': 'delta_w', 'new_m_a_pre_norm': 'new_m', 'new_m_a_w_in': 'new_m', 'new_m_a_w_out': 'new_m', 'new_m_a_post_norm': 'new_m', 'new_m_kv_norm': 'new_m', 'new_m_kv_w_down': 'new_m', 'new_m_kv_latent_norm': 'new_m', 'new_m_kv_w_up': 'new_m', 'new_m_b_pre_norm': 'new_m', 'new_m_b_w_in': 'new_m', 'new_m_b_q_norm': 'new_m', 'new_m_b_w_q_up': 'new_m', 'new_m_b_w_out': 'new_m', 'new_m_b_post_norm': 'new_m', 'new_v_a_pre_norm': 'new_v', 'new_v_a_w_in': 'new_v', 'new_v_a_w_out': 'new_v', 'new_v_a_post_norm': 'new_v', 'new_v_kv_norm': 'new_v', 'new_v_kv_w_down': 'new_v', 'new_v_kv_latent_norm': 'new_v', 'new_v_kv_w_up': 'new_v', 'new_v_b_pre_norm': 'new_v', 'new_v_b_w_in': 'new_v', 'new_v_b_q_norm': 'new_v', 'new_v_b_w_q_up': 'new_v', 'new_v_b_w_out': 'new_v', 'new_v_b_post_norm': 'new_v'}


def _forward(args):
    return _fwd_reference(*[args[k] for k in FWD_PARAMS])


def _output_shape():
    out = _jax.eval_shape(lambda: _forward(_fwd_setup_inputs(0)))
    return out.shape, out.dtype

N_MICROBATCH = 1
ADAM_LR = 0.001
ADAM_B1 = 0.9
ADAM_B2 = 0.999
ADAM_EPS = 1e-08
ADAM_WD = 0.01
ADAM_STEP = 10
PER_EXAMPLE_BATCH_AXIS = {'x': 0, 'positions': 0, 'loss_target': 0}
SHARED_INPUTS = []
_WEIGHT_DTYPES = {'a_pre_norm': _jnp.float32, 'a_w_in': _jnp.float32, 'a_w_out': _jnp.float32, 'a_post_norm': _jnp.float32, 'kv_norm': _jnp.float32, 'kv_w_down': _jnp.float32, 'kv_latent_norm': _jnp.float32, 'kv_w_up': _jnp.float32, 'b_pre_norm': _jnp.float32, 'b_w_in': _jnp.float32, 'b_q_norm': _jnp.float32, 'b_w_q_up': _jnp.float32, 'b_w_out': _jnp.float32, 'b_post_norm': _jnp.float32}
MOMENT_SCALE = {'a_pre_norm': 1.426125e+00, 'a_w_in': 4.352063e-01, 'a_w_out': 8.092329e-01, 'a_post_norm': 6.387103e+01, 'kv_norm': 6.306979e-01, 'kv_w_down': 1.067295e+00, 'kv_latent_norm': 1.670404e+00, 'kv_w_up': 5.271810e-01, 'b_pre_norm': 6.789112e-01, 'b_w_in': 5.947353e-01, 'b_q_norm': 5.970333e-01, 'b_w_q_up': 3.572139e-01, 'b_w_out': 6.121417e-01, 'b_post_norm': 6.419853e+01}


def _to_microbatches(a, axis):
    t = _jnp.moveaxis(a, axis, 0)
    t = t.reshape((N_MICROBATCH, t.shape[0] // N_MICROBATCH) + t.shape[1:])
    return _jnp.moveaxis(t, 1, axis + 1)


def setup_inputs(seed: int = 0) -> dict:
    inp = _fwd_setup_inputs(seed)
    key = _jax.random.fold_in(_jax.random.key(seed), 7919)
    shape, _ = _output_shape()
    out = dict(inp)
    out["loss_target"] = _jax.random.normal(_jax.random.fold_in(key, 0), shape, _jnp.float32)
    for i, name in enumerate(TWIN_WEIGHTS):
        w = inp[name].astype(_jnp.float32)
        if MOMENT_SCALE is None:
            s = _jnp.sqrt(_jnp.mean(_jnp.square(w)) + 1e-30)
        else:
            s = MOMENT_SCALE[name]
        km, kv = _jax.random.split(_jax.random.fold_in(key, i + 1))
        out[name] = w
        out["m_" + name] = s * _jax.random.normal(km, w.shape, _jnp.float32)
        out["v_" + name] = (s * s) * _jax.random.uniform(kv, w.shape, _jnp.float32, 0.5, 1.5)
    if N_MICROBATCH > 1:
        for name, axis in PER_EXAMPLE_BATCH_AXIS.items():
            out[name] = _to_microbatches(out[name], axis)
    return {'x': out['x'], 'positions': out['positions'], 'a_pre_norm': out['a_pre_norm'], 'a_w_in': out['a_w_in'], 'a_w_out': out['a_w_out'], 'a_post_norm': out['a_post_norm'], 'kv_norm': out['kv_norm'], 'kv_w_down': out['kv_w_down'], 'kv_latent_norm': out['kv_latent_norm'], 'kv_w_up': out['kv_w_up'], 'b_pre_norm': out['b_pre_norm'], 'b_w_in': out['b_w_in'], 'b_q_norm': out['b_q_norm'], 'b_w_q_up': out['b_w_q_up'], 'b_w_out': out['b_w_out'], 'b_post_norm': out['b_post_norm'], 'loss_target': out['loss_target'], 'm_a_pre_norm': out['m_a_pre_norm'], 'm_a_w_in': out['m_a_w_in'], 'm_a_w_out': out['m_a_w_out'], 'm_a_post_norm': out['m_a_post_norm'], 'm_kv_norm': out['m_kv_norm'], 'm_kv_w_down': out['m_kv_w_down'], 'm_kv_latent_norm': out['m_kv_latent_norm'], 'm_kv_w_up': out['m_kv_w_up'], 'm_b_pre_norm': out['m_b_pre_norm'], 'm_b_w_in': out['m_b_w_in'], 'm_b_q_norm': out['m_b_q_norm'], 'm_b_w_q_up': out['m_b_w_q_up'], 'm_b_w_out': out['m_b_w_out'], 'm_b_post_norm': out['m_b_post_norm'], 'v_a_pre_norm': out['v_a_pre_norm'], 'v_a_w_in': out['v_a_w_in'], 'v_a_w_out': out['v_a_w_out'], 'v_a_post_norm': out['v_a_post_norm'], 'v_kv_norm': out['v_kv_norm'], 'v_kv_w_down': out['v_kv_w_down'], 'v_kv_latent_norm': out['v_kv_latent_norm'], 'v_kv_w_up': out['v_kv_w_up'], 'v_b_pre_norm': out['v_b_pre_norm'], 'v_b_w_in': out['v_b_w_in'], 'v_b_q_norm': out['v_b_q_norm'], 'v_b_w_q_up': out['v_b_w_q_up'], 'v_b_w_out': out['v_b_w_out'], 'v_b_post_norm': out['v_b_post_norm']}


def _loss(weights, diff, rest, loss_target):
    with _jax.named_scope("forward"):
        args = {**rest, TWIN_DIFF_INPUT: diff, **{k: w.astype(_WEIGHT_DTYPES[k]) for k, w in weights.items()}}
        y = _forward(args)
    with _jax.named_scope("loss_head"):
        err = _jnp.square(y.astype(_jnp.float32) - loss_target)
        return 0.5 * _jnp.sum(_jnp.mean(err, axis=-1)) if err.ndim else 0.5 * err


def _adamw(w, g, m, v):
    m = ADAM_B1 * m + (1.0 - ADAM_B1) * g
    v = ADAM_B2 * v + (1.0 - ADAM_B2) * _jnp.square(g)
    m_hat = m / (1.0 - ADAM_B1 ** ADAM_STEP)
    v_hat = v / (1.0 - ADAM_B2 ** ADAM_STEP)
    delta = -ADAM_LR * (m_hat / (_jnp.sqrt(v_hat) + ADAM_EPS) + ADAM_WD * w)
    return delta, m, v


def reference(x, positions, a_pre_norm, a_w_in, a_w_out, a_post_norm, kv_norm, kv_w_down, kv_latent_norm, kv_w_up, b_pre_norm, b_w_in, b_q_norm, b_w_q_up, b_w_out, b_post_norm, loss_target, m_a_pre_norm, m_a_w_in, m_a_w_out, m_a_post_norm, m_kv_norm, m_kv_w_down, m_kv_latent_norm, m_kv_w_up, m_b_pre_norm, m_b_w_in, m_b_q_norm, m_b_w_q_up, m_b_w_out, m_b_post_norm, v_a_pre_norm, v_a_w_in, v_a_w_out, v_a_post_norm, v_kv_norm, v_kv_w_down, v_kv_latent_norm, v_kv_w_up, v_b_pre_norm, v_b_w_in, v_b_q_norm, v_b_w_q_up, v_b_w_out, v_b_post_norm):
    given = dict(x=x, positions=positions, a_pre_norm=a_pre_norm, a_w_in=a_w_in, a_w_out=a_w_out, a_post_norm=a_post_norm, kv_norm=kv_norm, kv_w_down=kv_w_down, kv_latent_norm=kv_latent_norm, kv_w_up=kv_w_up, b_pre_norm=b_pre_norm, b_w_in=b_w_in, b_q_norm=b_q_norm, b_w_q_up=b_w_q_up, b_w_out=b_w_out, b_post_norm=b_post_norm, loss_target=loss_target, m_a_pre_norm=m_a_pre_norm, m_a_w_in=m_a_w_in, m_a_w_out=m_a_w_out, m_a_post_norm=m_a_post_norm, m_kv_norm=m_kv_norm, m_kv_w_down=m_kv_w_down, m_kv_latent_norm=m_kv_latent_norm, m_kv_w_up=m_kv_w_up, m_b_pre_norm=m_b_pre_norm, m_b_w_in=m_b_w_in, m_b_q_norm=m_b_q_norm, m_b_w_q_up=m_b_w_q_up, m_b_w_out=m_b_w_out, m_b_post_norm=m_b_post_norm, v_a_pre_norm=v_a_pre_norm, v_a_w_in=v_a_w_in, v_a_w_out=v_a_w_out, v_a_post_norm=v_a_post_norm, v_kv_norm=v_kv_norm, v_kv_w_down=v_kv_w_down, v_kv_latent_norm=v_kv_latent_norm, v_kv_w_up=v_kv_w_up, v_b_pre_norm=v_b_pre_norm, v_b_w_in=v_b_w_in, v_b_q_norm=v_b_q_norm, v_b_w_q_up=v_b_w_q_up, v_b_w_out=v_b_w_out, v_b_post_norm=v_b_post_norm)
    weights = {n: given[n] for n in TWIN_WEIGHTS}
    shared = {n: given[n] for n in SHARED_INPUTS}
    per_example = {n: given[n] for n in ['x', 'positions']}
    grad_fn = _jax.value_and_grad(_loss, argnums=(0, 1))

    def one_microbatch(ex, loss_target):
        ex = dict(ex)
        diff = ex.pop(TWIN_DIFF_INPUT)
        return grad_fn(weights, diff, {**shared, **ex}, loss_target)

    if N_MICROBATCH == 1:
        loss, (grad_w, grad_x) = one_microbatch(per_example, given["loss_target"])
    else:
        def body(carry, xs):
            loss_sum, grad_sum = carry
            l_k, (gw_k, gx_k) = one_microbatch(xs[0], xs[1])
            with _jax.named_scope("update"):
                return (loss_sum + l_k, _jax.tree.map(_jnp.add, grad_sum, gw_k)), gx_k

        init = (_jnp.zeros((), _jnp.float32), _jax.tree.map(_jnp.zeros_like, weights))
        (loss, grad_w), grad_x = _jax.lax.scan(body, init, (per_example, given["loss_target"]))
    with _jax.named_scope("update"):
        delta_w, new_m, new_v = {}, {}, {}
        for n in TWIN_WEIGHTS:
            delta_w[n], new_m[n], new_v[n] = _adamw(weights[n], grad_w[n], given["m_" + n], given["v_" + n])
    return (loss, grad_x, *[grad_w[n] for n in TWIN_WEIGHTS], *[delta_w[n] for n in TWIN_WEIGHTS],
            *[new_m[n] for n in TWIN_WEIGHTS], *[new_v[n] for n in TWIN_WEIGHTS])
```

```python
import jax
import jax.numpy as jnp
from jax import lax
from jax.experimental import pallas as pl
from jax.experimental.pallas import tpu as pltpu

F32 = jnp.float32
BF16 = jnp.bfloat16

N_DEV = 8
D_MODEL = 1024
NORM_EPS = 1e-6
A_GROUPS = 3
A_DILATIONS = (1, 4, 16)
A_HEADS = 8
A_HEAD_DIM = 128
A_WIDTH = 1024
A_ROT_DIM = 32
A_ROPE_THETA = 500000.0
A_QKV = A_GROUPS * 3 * A_WIDTH
B_HEADS = 16
B_NOPE = 64
B_ROPE = 32
B_QK_DIM = 96
B_VDIM = 64
B_Q_LORA = 384
B_KV_LORA = 256
B_ROPE_THETA = 10000.0
B_KPAD = B_HEADS * 128
ADAM_LR = 0.001
ADAM_B1 = 0.9
ADAM_B2 = 0.999
ADAM_EPS = 1e-08
ADAM_WD = 0.01
ADAM_STEP = 10

LANES = 128
BAND = 128
NEG = -1e30
VMEM_LIMIT = 56 * 1024 * 1024
MESH = pl.DeviceIdType.MESH


def _cparams(sem):
    return pltpu.CompilerParams(dimension_semantics=sem, vmem_limit_bytes=VMEM_LIMIT)


def _rowwise(fn, rows, bcast, outs, accs=(), *, tm, name):
    n = rows[0].shape[0]
    nr, nb, no = len(rows), len(bcast), len(outs)

    def body(*refs):
        res = fn(*[r[...] for r in refs[:nr + nb]])
        out_refs = refs[nr + nb:nr + nb + no]
        acc_refs = refs[nr + nb + no:]
        for r, v in zip(out_refs, res[:no]):
            r[...] = v.astype(r.dtype)
        if acc_refs:
            @pl.when(pl.program_id(0) == 0)
            def _():
                for r in acc_refs:
                    r[...] = jnp.zeros_like(r)
            for r, v in zip(acc_refs, res[no:]):
                r[...] += v.reshape(tm // 8, 8, v.shape[-1]).sum(axis=0)

    in_specs = [pl.BlockSpec((tm, a.shape[1]), lambda i: (i, 0)) for a in rows]
    in_specs += [pl.BlockSpec(a.shape, lambda i: (0, 0)) for a in bcast]
    out_specs = [pl.BlockSpec((tm, c), lambda i: (i, 0)) for c, _ in outs]
    out_specs += [pl.BlockSpec((8, c), lambda i: (0, 0)) for c in accs]
    out_shape = [jax.ShapeDtypeStruct((n, c), dt) for c, dt in outs]
    out_shape += [jax.ShapeDtypeStruct((8, c), F32) for c in accs]
    return pl.pallas_call(
        body, name=name, grid=(n // tm,), in_specs=in_specs, out_specs=out_specs, out_shape=out_shape,
        compiler_params=_cparams(("arbitrary",)))(*rows, *bcast)


def _matmul(a, b, *, out_dtype, tm, tn, tk=None, name, epilogue=None, extras=()):
    m, k = a.shape
    n = b.shape[1]
    tk = tk or k
    nk = k // tk
    ne = len(extras)

    def body(*refs):
        a_ref, b_ref = refs[:2]
        ex = refs[2:2 + ne]
        o_ref = refs[2 + ne]
        part = _dot(a_ref[...].astype(BF16), b_ref[...].astype(BF16))

        def finish(acc):
            if epilogue is None:
                o_ref[...] = acc.astype(o_ref.dtype)
            else:
                epilogue(acc, o_ref, pl.program_id(0), *[e[...] for e in ex])

        if nk == 1:
            finish(part)
        else:
            acc_ref = refs[-1]
            kk = pl.program_id(2)

            @pl.when(kk == 0)
            def _():
                acc_ref[...] = part

            @pl.when(kk > 0)
            def _():
                acc_ref[...] += part

            @pl.when(kk == nk - 1)
            def _():
                finish(acc_ref[...])

    in_specs = [pl.BlockSpec((tm, tk), lambda j, i, kk: (i, kk)),
                pl.BlockSpec((tk, tn), lambda j, i, kk: (kk, j))]
    in_specs += [pl.BlockSpec(bs, im) for _, bs, im in extras]
    return pl.pallas_call(
        body, name=name, grid=(n // tn, m // tm, nk), in_specs=in_specs,
        out_specs=pl.BlockSpec((tm, tn), lambda j, i, kk: (i, j)),
        out_shape=jax.ShapeDtypeStruct((m, n), out_dtype),
        scratch_shapes=[pltpu.VMEM((tm, tn), F32)] if nk > 1 else [],
        compiler_params=_cparams(("parallel", "parallel", "arbitrary")))(a, b, *[e[0] for e in extras])


def _add_epilogue(acc, o_ref, j, prev):
    o_ref[...] = (acc + prev).astype(o_ref.dtype)


def _rope(x, c, sp, sm):
    return x * c + pltpu.roll(x, 16, 1) * sp + pltpu.roll(x, LANES - 16, 1) * sm


def _rope_t(dy, c, sp, sm):
    return dy * c + pltpu.roll(dy * sp, LANES - 16, 1) + pltpu.roll(dy * sm, 16, 1)


def _rope_tables(positions, theta, rot_dim, lane0):
    half = rot_dim // 2
    inv_freq = 1.0 / (theta ** (jnp.arange(half, dtype=F32) * (2.0 / rot_dim)))
    ang = positions.astype(F32)[..., None] * inv_freq
    cos, sin = jnp.cos(ang), jnp.sin(ang)
    shape = positions.shape + (LANES,)
    c = jnp.ones(shape, F32).at[..., lane0:lane0 + half].set(cos).at[..., lane0 + half:lane0 + rot_dim].set(cos)
    sp = jnp.zeros(shape, F32).at[..., lane0 + half:lane0 + rot_dim].set(sin)
    sm = jnp.zeros(shape, F32).at[..., lane0:lane0 + half].set(-sin)
    return c, sp, sm


def _rms(x, g):
    xf = x.astype(F32)
    return xf * lax.rsqrt(jnp.mean(xf * xf, axis=-1, keepdims=True) + NORM_EPS) * g


def _rms_bwd(x, g, dy):
    xf = x.astype(F32)
    rstd = lax.rsqrt(jnp.mean(xf * xf, axis=-1, keepdims=True) + NORM_EPS)
    xhat = xf * rstd
    dxhat = dy * g
    dx = rstd * (dxhat - xhat * jnp.mean(dxhat * xhat, axis=-1, keepdims=True))
    return dx, dy * xhat


def _silu(z):
    return z * jax.nn.sigmoid(z)


def _silu_grad(z):
    s = jax.nn.sigmoid(z)
    return s * (1.0 + z * (1.0 - s))


def _dot_nt(a, b):
    return lax.dot_general(a, b, (((1,), (1,)), ((), ())), preferred_element_type=F32)


def _dot_tn(a, b):
    return lax.dot_general(a, b, (((0,), (0,)), ((), ())), preferred_element_type=F32)


def _dot(a, b):
    return jnp.dot(a, b, preferred_element_type=F32)


def _attn_a_fwd(qkv, state, g, d, first, last):
    bsz, t, _ = qkv.shape
    ln = t // d
    nb = ln // BAND
    scale = A_HEAD_DIM ** -0.5
    qv = qkv.reshape(bsz, ln, d * A_QKV)
    blk = (None, BAND, A_WIDTH)

    def col(s):
        return lambda b, r, l: (b, l, r * 9 + 3 * g + s)

    def colp(s):
        return lambda b, r, l: (b, jnp.maximum(l - 1, 0), r * 9 + 3 * g + s)

    st_spec = pl.BlockSpec(blk, lambda b, r, l: (b, l, r))
    n_state = 0 if first else 3

    def body(q_ref, kc_ref, kp_ref, vc_ref, vp_ref, *rest):
        st_in, outs = rest[:n_state], rest[n_state:]
        qi = lax.broadcasted_iota(jnp.int32, (BAND, BAND), 0)
        ki = lax.broadcasted_iota(jnp.int32, (BAND, BAND), 1)
        mask_c = ki <= qi
        mask_p = ki >= qi
        pen_p = jnp.where(pl.program_id(2) > 0, 0.0, NEG)
        for h in range(A_HEADS):
            hs = slice(h * A_HEAD_DIM, (h + 1) * A_HEAD_DIM)
            q = q_ref[:, hs]
            s_c = jnp.where(mask_c, _dot_nt(q, kc_ref[:, hs]) * scale, NEG)
            s_p = jnp.where(mask_p, _dot_nt(q, kp_ref[:, hs]) * scale, NEG) + pen_p
            m_new = jnp.maximum(jnp.max(s_c, axis=1, keepdims=True), jnp.max(s_p, axis=1, keepdims=True))
            if not first:
                m_old = st_in[1][:, hs][:, :1]
                m_new = jnp.maximum(m_new, m_old)
            p_c = jnp.exp(s_c - m_new)
            p_p = jnp.exp(s_p - m_new)
            l_new = jnp.sum(p_c, axis=1, keepdims=True) + jnp.sum(p_p, axis=1, keepdims=True)
            acc = _dot(p_c.astype(BF16), vc_ref[:, hs]) + _dot(p_p.astype(BF16), vp_ref[:, hs])
            if not first:
                alpha = jnp.exp(m_old - m_new)
                l_new = l_new + alpha * st_in[2][:, hs][:, :1]
                acc = acc + alpha * st_in[0][:, hs]
            if last:
                outs[0][:, hs] = acc / l_new
                outs[1][:, hs] = jnp.broadcast_to(m_new + jnp.log(l_new), (BAND, A_HEAD_DIM))
            else:
                outs[0][:, hs] = acc
                outs[1][:, hs] = jnp.broadcast_to(m_new, (BAND, A_HEAD_DIM))
                outs[2][:, hs] = jnp.broadcast_to(l_new, (BAND, A_HEAD_DIM))

    n_out = 2 if last else 3
    in_specs = [pl.BlockSpec(blk, col(0)), pl.BlockSpec(blk, col(1)), pl.BlockSpec(blk, colp(1)),
                pl.BlockSpec(blk, col(2)), pl.BlockSpec(blk, colp(2))] + [st_spec] * n_state
    st_args = [] if first else [s.reshape(bsz, ln, d * A_WIDTH) for s in state]
    res = pl.pallas_call(
        body, name=f"attn_a_fwd_g{g}", grid=(bsz, d, nb), in_specs=in_specs,
        out_specs=[st_spec] * n_out,
        out_shape=[jax.ShapeDtypeStruct((bsz, ln, d * A_WIDTH), F32)] * n_out,
        compiler_params=_cparams(("parallel", "parallel", "arbitrary")))(qv, qv, qv, qv, qv, *st_args)
    return [x.reshape(bsz, t, A_WIDTH) for x in res]


def _attn_a_bwd(qkv, do, lse, dsum, tabs, g, d):
    bsz, t, _ = qkv.shape
    ln = t // d
    nb = ln // BAND
    scale = A_HEAD_DIM ** -0.5
    qv = qkv.reshape(bsz, ln, d * A_QKV)
    blk = (None, BAND, A_WIDTH)

    def col(s, off):
        def im(b, r, l):
            return (b, jnp.clip(l + off, 0, nb - 1), r * 9 + 3 * g + s)
        return im

    def tok(off):
        def im(b, r, l):
            return (b, jnp.clip(l + off, 0, nb - 1), r)
        return im

    def body(q_ref, qn_ref, kc_ref, kp_ref, vc_ref, vp_ref, do_ref, don_ref, lse_ref, lsen_ref,
             ds_ref, dsn_ref, c_ref, sp_ref, sm_ref, out_ref):
        l_idx = pl.program_id(2)
        qi = lax.broadcasted_iota(jnp.int32, (BAND, BAND), 0)
        ki = lax.broadcasted_iota(jnp.int32, (BAND, BAND), 1)
        mask_c = ki <= qi
        mask_p = ki >= qi
        pen_p = jnp.where(l_idx > 0, 0.0, NEG)
        pen_n = jnp.where(l_idx < nb - 1, 0.0, NEG)
        c, sp, sm = c_ref[...], sp_ref[...], sm_ref[...]

        def block(q, k, v, do_b, lse_c, dsum_c, mask, pen):
            s = jnp.where(mask, _dot_nt(q, k) * scale, NEG) + pen
            p = jnp.exp(s - lse_c)
            dp = _dot_nt(do_b, v)
            return p, p * (dp - dsum_c) * scale

        for h in range(A_HEADS):
            hs = slice(h * A_HEAD_DIM, (h + 1) * A_HEAD_DIM)
            q, qn = q_ref[:, hs], qn_ref[:, hs]
            kc, kp, vc, vp = kc_ref[:, hs], kp_ref[:, hs], vc_ref[:, hs], vp_ref[:, hs]
            do_b, do_n = do_ref[:, hs], don_ref[:, hs]
            lse_c, lse_n = lse_ref[:, hs][:, :1], lsen_ref[:, hs][:, :1]
            ds_c, ds_n = ds_ref[:, hs][:, :1], dsn_ref[:, hs][:, :1]
            p, ds = block(q, kc, vc, do_b, lse_c, ds_c, mask_c, 0.0)
            dsb = ds.astype(BF16)
            dq = _dot(dsb, kc)
            dk = _dot_tn(dsb, q)
            dv = _dot_tn(p.astype(BF16), do_b)
            p, ds = block(q, kp, vp, do_b, lse_c, ds_c, mask_p, pen_p)
            dq = dq + _dot(ds.astype(BF16), kp)
            p, ds = block(qn, kc, vc, do_n, lse_n, ds_n, mask_p, pen_n)
            dk = dk + _dot_tn(ds.astype(BF16), qn)
            dv = dv + _dot_tn(p.astype(BF16), do_n)
            out_ref[:, h * A_HEAD_DIM:(h + 1) * A_HEAD_DIM] = _rope_t(dq, c, sp, sm).astype(BF16)
            out_ref[:, A_WIDTH + h * A_HEAD_DIM:A_WIDTH + (h + 1) * A_HEAD_DIM] = _rope_t(dk, c, sp, sm).astype(BF16)
            out_ref[:, 2 * A_WIDTH + h * A_HEAD_DIM:2 * A_WIDTH + (h + 1) * A_HEAD_DIM] = dv.astype(BF16)

    def view(a):
        return a.reshape(bsz, ln, d * a.shape[-1])

    tspec = pl.BlockSpec((None, BAND, LANES), tok(0))
    in_specs = [pl.BlockSpec(blk, col(0, 0)), pl.BlockSpec(blk, col(0, 1)),
                pl.BlockSpec(blk, col(1, 0)), pl.BlockSpec(blk, col(1, -1)),
                pl.BlockSpec(blk, col(2, 0)), pl.BlockSpec(blk, col(2, -1)),
                pl.BlockSpec(blk, tok(0)), pl.BlockSpec(blk, tok(1)),
                pl.BlockSpec(blk, tok(0)), pl.BlockSpec(blk, tok(1)),
                pl.BlockSpec(blk, tok(0)), pl.BlockSpec(blk, tok(1)),
                tspec, tspec, tspec]
    dov, lsev, dsv = view(do), view(lse), view(dsum)
    out = pl.pallas_call(
        body, name=f"attn_a_bwd_g{g}", grid=(bsz, d, nb), in_specs=in_specs,
        out_specs=pl.BlockSpec((None, BAND, 3 * A_WIDTH), tok(0)),
        out_shape=jax.ShapeDtypeStruct((bsz, ln, d * 3 * A_WIDTH), BF16),
        compiler_params=_cparams(("parallel", "parallel", "arbitrary")))(
            qv, qv, qv, qv, qv, qv, dov, dov, lsev, lsev, dsv, dsv, *[view(x) for x in tabs])
    return out.reshape(bsz, t, 3 * A_WIDTH)


B_TQ = 256
B_SCALE = B_QK_DIM ** -0.5


def _pair_masks(rows):
    lane = lax.broadcasted_iota(jnp.int32, (rows, LANES), 1)
    return lane < B_VDIM


def _causal_mask(qb, kb, tq, tk):
    row = qb * tq + lax.broadcasted_iota(jnp.int32, (tq, tk), 0)
    colp = kb * tk + lax.broadcasted_iota(jnp.int32, (tq, tk), 1)
    return colp <= row


def _b_specs(t, tq, tk):
    pair_tile = pl.BlockSpec((None, tq, 2 * LANES), lambda b, j, i: (b, i, j))
    pair_full = pl.BlockSpec((None, t, 2 * LANES), lambda b, j, i: (b, 0, j))
    one_tile = pl.BlockSpec((None, tk, LANES), lambda b, j, i: (b, i, j))
    one_full = pl.BlockSpec((None, t, LANES), lambda b, j, i: (b, 0, j))
    tab_tile = pl.BlockSpec((None, tq, LANES), lambda b, j, i: (b, i, 0))
    return pair_tile, pair_full, one_tile, one_full, tab_tile


def _attn_b_fwd(q, kpad, v):
    bsz, t, _ = q.shape
    tq = tk = B_TQ
    nq = t // tq

    def body(q_ref, k_ref, v_ref, o_ref, lse_ref):
        qblk = pl.program_id(2)
        first_half = _pair_masks(tq)
        o_pair, lse_pair = None, None
        for hh in range(2):
            ls = slice(hh * LANES, (hh + 1) * LANES)
            qh = q_ref[:, ls]

            def step(kb, carry):
                m_old, l_old, acc = carry
                start = pl.multiple_of(kb * tk, tk)
                k = k_ref[pl.ds(start, tk), ls]
                v = v_ref[pl.ds(start, tk), :]
                s = jnp.where(_causal_mask(qblk, kb, tq, tk), _dot_nt(qh, k) * B_SCALE, NEG)
                m_new = jnp.maximum(m_old, jnp.max(s, axis=1, keepdims=True))
                alpha = jnp.exp(m_old - m_new)
                p = jnp.exp(s - m_new)
                l_new = alpha * l_old + jnp.sum(p, axis=1, keepdims=True)
                return m_new, l_new, alpha * acc + _dot(p.astype(BF16), v)

            init = (jnp.full((tq, 1), NEG, F32), jnp.zeros((tq, 1), F32), jnp.zeros((tq, LANES), F32))
            m_f, l_f, acc = lax.fori_loop(0, qblk + 1, step, init)
            o_h = acc / l_f
            lse_h = jnp.broadcast_to(m_f + jnp.log(l_f), (tq, LANES))
            if hh == 0:
                o_pair, lse_pair = o_h, lse_h
            else:
                o_pair = jnp.where(first_half, o_pair, o_h)
                lse_pair = jnp.where(first_half, lse_pair, lse_h)
        o_ref[...] = o_pair
        lse_ref[...] = lse_pair

    pair_tile, pair_full, one_tile, one_full, _ = _b_specs(t, tq, tk)
    return pl.pallas_call(
        body, name="attn_b_fwd", grid=(bsz, B_HEADS // 2, nq),
        in_specs=[pair_tile, pair_full, one_full],
        out_specs=[one_tile, one_tile],
        out_shape=[jax.ShapeDtypeStruct((bsz, t, B_HEADS * B_VDIM), F32)] * 2,
        compiler_params=_cparams(("parallel", "parallel", "arbitrary")))(q, kpad, v)


def _attn_b_dq(q, kpad, v, do, o, lse, tabs):
    bsz, t, _ = q.shape
    tq = tk = B_TQ
    nq = t // tq

    def body(q_ref, k_ref, v_ref, do_ref, o_ref, lse_ref, c_ref, sp_ref, sm_ref, dq_ref, dsum_ref):
        qblk = pl.program_id(2)
        first_half = _pair_masks(tq)
        do_f = do_ref[...].astype(F32)
        prod = do_f * o_ref[...]
        d_a = jnp.sum(jnp.where(first_half, prod, 0.0), axis=1, keepdims=True)
        d_b = jnp.sum(jnp.where(first_half, 0.0, prod), axis=1, keepdims=True)
        dsum_ref[...] = jnp.where(first_half, d_a, d_b)
        lse = lse_ref[...]
        c, sp, sm = c_ref[...], sp_ref[...], sm_ref[...]
        heads = ((d_a, lse[:, :1], jnp.where(first_half, do_f, 0.0)),
                 (d_b, lse[:, B_VDIM:B_VDIM + 1], jnp.where(first_half, 0.0, do_f)))
        for hh, (d_h, lse_h, do_h) in enumerate(heads):
            ls = slice(hh * LANES, (hh + 1) * LANES)
            qh = q_ref[:, ls]
            do_h = do_h.astype(BF16)

            def step(kb, dq):
                start = pl.multiple_of(kb * tk, tk)
                k = k_ref[pl.ds(start, tk), ls]
                v = v_ref[pl.ds(start, tk), :]
                s = jnp.where(_causal_mask(qblk, kb, tq, tk), _dot_nt(qh, k) * B_SCALE, NEG)
                p = jnp.exp(s - lse_h)
                ds = p * (_dot_nt(do_h, v) - d_h) * B_SCALE
                return dq + _dot(ds.astype(BF16), k)

            dq = lax.fori_loop(0, qblk + 1, step, jnp.zeros((tq, LANES), F32))
            dq_ref[:, ls] = _rope_t(dq, c, sp, sm).astype(BF16)

    pair_tile, pair_full, one_tile, one_full, tab_tile = _b_specs(t, tq, tk)
    return pl.pallas_call(
        body, name="attn_b_dq", grid=(bsz, B_HEADS // 2, nq),
        in_specs=[pair_tile, pair_full, one_full, one_tile, one_tile, one_tile, tab_tile, tab_tile, tab_tile],
        out_specs=[pair_tile, one_tile],
        out_shape=[jax.ShapeDtypeStruct((bsz, t, B_KPAD), BF16),
                   jax.ShapeDtypeStruct((bsz, t, B_HEADS * B_VDIM), F32)],
        compiler_params=_cparams(("parallel", "parallel", "arbitrary")))(q, kpad, v, do, o, lse, *tabs)


def _attn_b_dkv(q, kpad, v, do, lse, dsum):
    bsz, t, _ = q.shape
    tq = tk = B_TQ
    nq = t // tq

    def body(q_ref, k_ref, v_ref, do_ref, lse_ref, dsum_ref, dk_ref, dv_ref):
        kblk = pl.program_id(2)
        first_half = _pair_masks(tq)
        v = v_ref[...]
        dv = jnp.zeros((tk, LANES), F32)
        for hh in range(2):
            ls = slice(hh * LANES, (hh + 1) * LANES)
            k = k_ref[:, ls]
            lane0 = hh * B_VDIM

            def step(qb, carry):
                dk, dv_c = carry
                start = pl.multiple_of(qb * tq, tq)
                qh = q_ref[pl.ds(start, tq), ls]
                do_f = do_ref[pl.ds(start, tq), :]
                do_h = jnp.where(first_half, do_f, jnp.zeros_like(do_f)) if hh == 0 else \
                    jnp.where(first_half, jnp.zeros_like(do_f), do_f)
                lse_h = lse_ref[pl.ds(start, tq), :][:, lane0:lane0 + 1]
                d_h = dsum_ref[pl.ds(start, tq), :][:, lane0:lane0 + 1]
                s = jnp.where(_causal_mask(qb, kblk, tq, tk), _dot_nt(qh, k) * B_SCALE, NEG)
                p = jnp.exp(s - lse_h)
                ds = p * (_dot_nt(do_h, v) - d_h) * B_SCALE
                return dk + _dot_tn(ds.astype(BF16), qh), dv_c + _dot_tn(p.astype(BF16), do_h)

            dk, dv = lax.fori_loop(kblk, nq, step, (jnp.zeros((tk, LANES), F32), dv))
            dk_ref[:, ls] = dk
        dv_ref[...] = dv

    pair_tile, pair_full, one_tile, one_full, _ = _b_specs(t, tq, tk)
    return pl.pallas_call(
        body, name="attn_b_dkv", grid=(bsz, B_HEADS // 2, nq),
        in_specs=[pair_full, pair_tile, one_tile, one_full, one_full, one_full],
        out_specs=[pair_tile, one_tile],
        out_shape=[jax.ShapeDtypeStruct((bsz, t, B_KPAD), F32),
                   jax.ShapeDtypeStruct((bsz, t, B_HEADS * B_VDIM), F32)],
        compiler_params=_cparams(("parallel", "parallel", "arbitrary")))(q, kpad, v, do, lse, dsum)


ANY = pl.BlockSpec(memory_space=pl.ANY)


def _all_gather(shard, name):
    def body(x_ref, out_ref, send_sems, recv_sems, local_sem):
        x, y, c = lax.axis_index("x"), lax.axis_index("y"), lax.axis_index("c")
        me, sibling = (x, y, c), (x, y, 1 - c)
        chips = [(1 - x, y), (x, 1 - y), (1 - x, 1 - y)]

        def rows(px, py, pc):
            return out_ref.at[4 * px + 2 * py + pc]

        def copy(k, block, to, src=None):
            return pltpu.make_async_remote_copy(
                src_ref=rows(*block) if src is None else src, dst_ref=rows(*block),
                send_sem=send_sems.at[k], recv_sem=recv_sems.at[k], device_id=to, device_id_type=MESH)

        mine = pltpu.make_async_copy(x_ref, rows(*me), local_sem)
        mine.start()
        first = [copy(0, me, sibling, src=x_ref)]
        first += [copy(1 + j, me, (*chip, c), src=x_ref) for j, chip in enumerate(chips)]
        for cp in first:
            cp.start()
        passed = [copy(4 + j, (*chip, c), sibling) for j, chip in enumerate(chips)]
        for j, chip in enumerate(chips):
            copy(1 + j, (*chip, c), me).wait_recv()
            passed[j].start()
        copy(0, sibling, me).wait_recv()
        for j, chip in enumerate(chips):
            copy(4 + j, (*chip, 1 - c), me).wait_recv()
        for cp in first + passed:
            cp.wait_send()
        mine.wait()

    return pl.pallas_call(
        body, name=name, in_specs=[ANY], out_specs=ANY,
        out_shape=jax.ShapeDtypeStruct((N_DEV,) + shard.shape, shard.dtype),
        scratch_shapes=[pltpu.SemaphoreType.DMA((7,)), pltpu.SemaphoreType.DMA((7,)), pltpu.SemaphoreType.DMA])(shard)


def _exchange_blocks(blocks, name):
    def body(g_ref, out_ref, send_sems, recv_sems, local_sem):
        x, y, c = lax.axis_index("x"), lax.axis_index("y"), lax.axis_index("c")
        me = 4 * x + 2 * y + c

        def peer(k):
            return (1 - x if k & 4 else x, 1 - y if k & 2 else y, 1 - c if k & 1 else c)

        def copy(k):
            px, py, pc = peer(k)
            return pltpu.make_async_remote_copy(
                src_ref=g_ref.at[4 * px + 2 * py + pc], dst_ref=out_ref.at[me],
                send_sem=send_sems.at[k - 1], recv_sem=recv_sems.at[k - 1], device_id=(px, py, pc), device_id_type=MESH)

        def arrival(k):
            px, py, pc = peer(k)
            slot = out_ref.at[4 * px + 2 * py + pc]
            return pltpu.make_async_remote_copy(
                src_ref=slot, dst_ref=slot, send_sem=send_sems.at[k - 1], recv_sem=recv_sems.at[k - 1],
                device_id=(px, py, pc), device_id_type=MESH)

        mine = pltpu.make_async_copy(g_ref.at[me], out_ref.at[me], local_sem)
        mine.start()
        sends = [copy(k) for k in range(1, N_DEV)]
        for cp in sends:
            cp.start()
        for k in range(1, N_DEV):
            arrival(k).wait_recv()
        for cp in sends:
            cp.wait_send()
        mine.wait()

    return pl.pallas_call(
        body, name=name, in_specs=[ANY], out_specs=ANY,
        out_shape=jax.ShapeDtypeStruct(blocks.shape, blocks.dtype),
        scratch_shapes=[pltpu.SemaphoreType.DMA((7,)), pltpu.SemaphoreType.DMA((7,)), pltpu.SemaphoreType.DMA])(blocks)


def _sum_slots(slots, *, tr, name):
    _, r, c = slots.shape

    def body(s_ref, o_ref):
        acc = s_ref[0].astype(F32)
        for s in range(1, N_DEV):
            acc = acc + s_ref[s].astype(F32)
        o_ref[...] = acc

    return pl.pallas_call(
        body, name=name, grid=(r // tr,),
        in_specs=[pl.BlockSpec((N_DEV, tr, c), lambda i: (0, i, 0))],
        out_specs=pl.BlockSpec((tr, c), lambda i: (i, 0)),
        out_shape=jax.ShapeDtypeStruct((r, c), F32),
        compiler_params=_cparams(("parallel",)))(slots)


def _adamw(w, g, m, v, name):
    shape = w.shape
    cols = shape[-1]
    args = [a.reshape(-1, cols) for a in (w, g, m, v)]
    rows = args[0].shape[0]
    tm = 256 if rows % 256 == 0 else rows

    def fn(w_t, g_t, m_t, v_t):
        m_n = ADAM_B1 * m_t + (1.0 - ADAM_B1) * g_t
        v_n = ADAM_B2 * v_t + (1.0 - ADAM_B2) * (g_t * g_t)
        m_hat = m_n / (1.0 - ADAM_B1 ** ADAM_STEP)
        v_hat = v_n / (1.0 - ADAM_B2 ** ADAM_STEP)
        delta = -ADAM_LR * (m_hat / (jnp.sqrt(v_hat) + ADAM_EPS) + ADAM_WD * w_t)
        return delta, m_n, v_n

    delta, m_n, v_n = _rowwise(fn, args, [], [(cols, F32)] * 3, tm=tm, name=name)
    return delta.reshape(shape), m_n.reshape(shape), v_n.reshape(shape)


def _heads_cat(fn, n_heads, width):
    return jnp.concatenate([fn(slice(h * width, (h + 1) * width)) for h in range(n_heads)], axis=1)


def _local_step(x, positions, target, w):
    bsz, t, _ = x.shape
    n = bsz * t
    tm = 256
    mm = 512
    x2 = x.reshape(n, D_MODEL)
    tgt2 = target.reshape(n, D_MODEL)
    tabs_a = _rope_tables(positions, A_ROPE_THETA, A_ROT_DIM, 0)
    tabs_b = _rope_tables(positions, B_ROPE_THETA, B_ROPE, B_NOPE)
    ta2 = [a.reshape(n, LANES) for a in tabs_a]
    tb2 = [a.reshape(n, LANES) for a in tabs_b]

    w_a_in = w["a_w_in"]
    w_qkv, w_z = w_a_in[:, :A_QKV], w_a_in[:, A_QKV:]
    w_a_out = w["a_w_out"]
    w_down = w["kv_w_down"]
    w_down_p = jnp.zeros((D_MODEL, 3 * LANES), BF16).at[:, :B_KV_LORA].set(w_down[:, :B_KV_LORA])
    w_down_p = w_down_p.at[:, B_KV_LORA + B_NOPE:B_KV_LORA + B_QK_DIM].set(w_down[:, B_KV_LORA:])
    wu = w["kv_w_up"].reshape(B_KV_LORA, B_HEADS, B_NOPE + B_VDIM)
    w_upk = jnp.pad(wu[:, :, :B_NOPE], ((0, 0), (0, 0), (0, LANES - B_NOPE))).reshape(B_KV_LORA, B_KPAD)
    w_upv = wu[:, :, B_NOPE:].reshape(B_KV_LORA, B_HEADS * B_VDIM)
    w_b_in = w["b_w_in"]
    w_q_p = jnp.pad(w["b_w_q_up"].reshape(B_Q_LORA, B_HEADS, B_QK_DIM),
                    ((0, 0), (0, 0), (0, LANES - B_QK_DIM))).reshape(B_Q_LORA, B_KPAD)
    w_b_out = w["b_w_out"]

    def tab_extras(tabs2, rows):
        return [(a, (rows, LANES), lambda j, i, kk: (i, 0)) for a in tabs2]

    (hn_a,) = _rowwise(lambda xt, g: (_rms(xt, g),), [x2], [w["a_pre_norm"]], [(D_MODEL, BF16)],
                       tm=tm, name="a_pre_norm")

    def qkv_epilogue(acc, o_ref, j, c, sp, sm):
        is_v = lax.rem(j, 3) == 2

        @pl.when(is_v)
        def _():
            o_ref[...] = acc.astype(BF16)

        @pl.when(jnp.logical_not(is_v))
        def _():
            for h in range(A_HEADS):
                hs = slice(h * A_HEAD_DIM, (h + 1) * A_HEAD_DIM)
                o_ref[:, hs] = _rope(acc[:, hs], c, sp, sm).astype(BF16)

    qkv = _matmul(hn_a, w_qkv, out_dtype=BF16, tm=mm, tn=A_WIDTH, name="a_qkv",
                  epilogue=qkv_epilogue, extras=tab_extras(ta2, mm))
    z_a = _matmul(hn_a, w_z, out_dtype=F32, tm=mm, tn=A_WIDTH, name="a_gate")
    qkv3 = qkv.reshape(bsz, t, A_QKV)
    state = None
    for g, d in enumerate(A_DILATIONS):
        state = _attn_a_fwd(qkv3, state, g, d, g == 0, g == A_GROUPS - 1)
    o_a, lse_a = state
    o_a2 = o_a.reshape(n, A_WIDTH)
    (y_a,) = _rowwise(lambda o, z: (o * _silu(z),), [o_a2, z_a], [], [(A_WIDTH, BF16)], tm=tm, name="a_gate_mul")
    out_a = _matmul(y_a, w_a_out, out_dtype=F32, tm=mm, tn=D_MODEL, name="a_out")

    def mid_fn(xt, out, g_post, g_kv, g_b):
        h1 = xt + _rms(out, g_post)
        return h1, _rms(h1, g_kv), _rms(h1, g_b)

    h1, hn_kv, hn_b = _rowwise(mid_fn, [x2, out_a], [w["a_post_norm"], w["kv_norm"], w["b_pre_norm"]],
                               [(D_MODEL, F32), (D_MODEL, BF16), (D_MODEL, BF16)], tm=tm, name="a_post_norm")

    ckr = _matmul(hn_kv, w_down_p, out_dtype=F32, tm=mm, tn=3 * LANES, name="kv_down")

    def latent_fn(ck, c, sp, sm, g):
        return _rms(ck[:, :B_KV_LORA], g), _rope(ck[:, B_KV_LORA:], c, sp, sm)

    c_kv, k_rope = _rowwise(latent_fn, [ckr, *tb2], [w["kv_latent_norm"]], [(B_KV_LORA, BF16), (LANES, F32)],
                            tm=tm, name="kv_latent_norm")

    def kpad_epilogue(acc, o_ref, j, kr):
        for h in range(acc.shape[1] // LANES):
            hs = slice(h * LANES, (h + 1) * LANES)
            o_ref[:, hs] = (acc[:, hs] + kr).astype(BF16)

    kpad = _matmul(c_kv, w_upk, out_dtype=BF16, tm=mm, tn=1024, name="kv_up_k", epilogue=kpad_epilogue,
                   extras=[(k_rope, (mm, LANES), lambda j, i, kk: (i, 0))])
    v_b = _matmul(c_kv, w_upv, out_dtype=BF16, tm=mm, tn=1024, name="kv_up_v")

    proj_b = _matmul(hn_b, w_b_in, out_dtype=F32, tm=mm, tn=w_b_in.shape[1], name="b_in")
    (c_q,) = _rowwise(lambda p, g: (_rms(p[:, :B_Q_LORA], g),), [proj_b], [w["b_q_norm"]], [(B_Q_LORA, BF16)],
                      tm=tm, name="b_q_norm")

    def q_epilogue(acc, o_ref, j, c, sp, sm):
        for h in range(acc.shape[1] // LANES):
            hs = slice(h * LANES, (h + 1) * LANES)
            o_ref[:, hs] = _rope(acc[:, hs], c, sp, sm).astype(BF16)

    q_b = _matmul(c_q, w_q_p, out_dtype=BF16, tm=mm, tn=1024, name="b_q_up", epilogue=q_epilogue,
                  extras=tab_extras(tb2, mm))
    q_b3, kpad3, v_b3 = q_b.reshape(bsz, t, B_KPAD), kpad.reshape(bsz, t, B_KPAD), v_b.reshape(bsz, t, -1)
    o_b, lse_b = _attn_b_fwd(q_b3, kpad3, v_b3)
    o_b2 = o_b.reshape(n, -1)
    (y_b,) = _rowwise(lambda o, p: (o * _silu(p[:, B_Q_LORA:]),), [o_b2, proj_b], [], [(D_MODEL, BF16)],
                      tm=tm, name="b_gate_mul")
    out_b = _matmul(y_b, w_b_out, out_dtype=F32, tm=mm, tn=D_MODEL, name="b_out")

    def head_fn(h1t, out, tgt, g):
        e = h1t + _rms(out, g) - tgt
        loss_row = 0.5 * jnp.mean(e * e, axis=-1, keepdims=True)
        dh2 = e * (1.0 / D_MODEL)
        d_out, dg = _rms_bwd(out, g, dh2)
        return dh2, d_out, dg, jnp.broadcast_to(loss_row * (1.0 / LANES), (loss_row.shape[0], LANES))

    dh2, d_out_b, dg_b_post, loss_acc = _rowwise(
        head_fn, [h1, out_b, tgt2], [w["b_post_norm"]], [(D_MODEL, F32), (D_MODEL, BF16)], [D_MODEL, LANES],
        tm=tm, name="loss_head")

    dy_b = _matmul(d_out_b, w_b_out.T, out_dtype=F32, tm=mm, tn=D_MODEL, name="b_out_dx")
    gw_b_out = _matmul(y_b.T, d_out_b, out_dtype=F32, tm=mm, tn=D_MODEL, tk=2048, name="b_out_dw")

    def gate_b_bwd(dy, o, p):
        z = p[:, B_Q_LORA:]
        return dy * _silu(z), dy * o * _silu_grad(z)

    do_b, dz_b = _rowwise(gate_b_bwd, [dy_b, o_b2, proj_b], [], [(D_MODEL, BF16), (D_MODEL, F32)],
                          tm=tm, name="b_gate_bwd")
    do_b3 = do_b.reshape(bsz, t, -1)
    dq_b, dsum_b = _attn_b_dq(q_b3, kpad3, v_b3, do_b3, o_b, lse_b, tabs_b)
    dk_b, dv_b = _attn_b_dkv(q_b3, kpad3, v_b3, do_b3, lse_b, dsum_b)
    dq_b2, dk_b2, dv_b2 = dq_b.reshape(n, B_KPAD), dk_b.reshape(n, B_KPAD), dv_b.reshape(n, -1)

    dk_bf, dv_bf = dk_b2.astype(BF16), dv_b2.astype(BF16)
    dc_kv = _matmul(dk_bf, w_upk.T, out_dtype=F32, tm=mm, tn=B_KV_LORA, name="kv_up_k_dx")
    dc_kv = _matmul(dv_bf, w_upv.T, out_dtype=F32, tm=mm, tn=B_KV_LORA, name="kv_up_v_dx",
                    epilogue=_add_epilogue, extras=[(dc_kv, (mm, B_KV_LORA), lambda j, i, kk: (i, j))])
    c_kv_t = c_kv.T
    gw_upk = _matmul(c_kv_t, dk_bf, out_dtype=F32, tm=B_KV_LORA, tn=1024, tk=2048, name="kv_up_k_dw")
    gw_upv = _matmul(c_kv_t, dv_bf, out_dtype=F32, tm=B_KV_LORA, tn=1024, tk=2048, name="kv_up_v_dw")

    def latent_bwd(ck, dck, dk, c, sp, sm, g):
        d1, dg = _rms_bwd(ck[:, :B_KV_LORA], g, dck)
        ksum = dk[:, :LANES]
        for h in range(1, B_HEADS):
            ksum = ksum + dk[:, h * LANES:(h + 1) * LANES]
        lane = lax.broadcasted_iota(jnp.int32, ksum.shape, 1)
        ksum = jnp.where((lane >= B_NOPE) & (lane < B_QK_DIM), ksum, 0.0)
        return jnp.concatenate([d1, _rope_t(ksum, c, sp, sm)], axis=1), dg

    dckr, dg_latent = _rowwise(latent_bwd, [ckr, dc_kv, dk_b2, *tb2], [w["kv_latent_norm"]],
                               [(3 * LANES, BF16)], [B_KV_LORA], tm=tm, name="kv_latent_bwd")
    dhn_kv = _matmul(dckr, w_down_p.T, out_dtype=F32, tm=mm, tn=D_MODEL, name="kv_down_dx")
    gw_down_p = _matmul(hn_kv.T, dckr, out_dtype=F32, tm=mm, tn=3 * LANES, tk=2048, name="kv_down_dw")

    dc_q = _matmul(dq_b2, w_q_p.T, out_dtype=F32, tm=mm, tn=B_Q_LORA, name="b_q_up_dx")
    gw_q_p = _matmul(c_q.T, dq_b2, out_dtype=F32, tm=B_Q_LORA, tn=1024, tk=2048, name="b_q_up_dw")

    def q_norm_bwd(p, dcq, dz, g):
        d1, dg = _rms_bwd(p[:, :B_Q_LORA], g, dcq)
        return jnp.concatenate([d1, dz], axis=1), dg

    dproj_b, dg_q_norm = _rowwise(q_norm_bwd, [proj_b, dc_q, dz_b], [w["b_q_norm"]],
                                  [(w_b_in.shape[1], BF16)], [B_Q_LORA], tm=tm, name="b_q_norm_bwd")
    dhn_b = _matmul(dproj_b, w_b_in.T, out_dtype=F32, tm=mm, tn=D_MODEL, name="b_in_dx")
    gw_b_in = _matmul(hn_b.T, dproj_b, out_dtype=F32, tm=mm, tn=w_b_in.shape[1], tk=2048, name="b_in_dw")

    def mid_bwd(h1t, dh2t, dkv, db, g_kv, g_b, g_post, out):
        dxa, ra = _rms_bwd(h1t, g_kv, dkv)
        dxb, rb = _rms_bwd(h1t, g_b, db)
        dh1 = dh2t + dxa + dxb
        d_out, rp = _rms_bwd(out, g_post, dh1)
        return dh1, d_out, ra, rb, rp

    def mid_bwd_fn(h1t, dh2t, dkv, db, out, g_kv, g_b, g_post):
        return mid_bwd(h1t, dh2t, dkv, db, g_kv, g_b, g_post, out)

    dh1, d_out_a, dg_kv, dg_b_pre, dg_a_post = _rowwise(
        mid_bwd_fn, [h1, dh2, dhn_kv, dhn_b, out_a], [w["kv_norm"], w["b_pre_norm"], w["a_post_norm"]],
        [(D_MODEL, F32), (D_MODEL, BF16)], [D_MODEL] * 3, tm=tm, name="mid_bwd")

    dy_a = _matmul(d_out_a, w_a_out.T, out_dtype=F32, tm=mm, tn=A_WIDTH, name="a_out_dx")
    gw_a_out = _matmul(y_a.T, d_out_a, out_dtype=F32, tm=mm, tn=D_MODEL, tk=2048, name="a_out_dw")

    def gate_a_bwd(dy, o, z):
        do = dy * _silu(z)
        prod = do * o
        dsum = _heads_cat(lambda hs: jnp.broadcast_to(jnp.sum(prod[:, hs], axis=1, keepdims=True),
                                                      (prod.shape[0], A_HEAD_DIM)), A_HEADS, A_HEAD_DIM)
        return do, dy * o * _silu_grad(z), dsum

    do_a, dz_a, dsum_a = _rowwise(gate_a_bwd, [dy_a, o_a2, z_a], [],
                                  [(A_WIDTH, BF16), (A_WIDTH, BF16), (A_WIDTH, F32)], tm=tm, name="a_gate_bwd")
    do_a3, dsum_a3 = do_a.reshape(bsz, t, A_WIDTH), dsum_a.reshape(bsz, t, A_WIDTH)
    hn_a_t = hn_a.T
    dhn_a = _matmul(dz_a, w_z.T, out_dtype=F32, tm=mm, tn=D_MODEL, name="a_gate_dx")
    gw_parts = []
    for g, d in enumerate(A_DILATIONS):
        dqkv = _attn_a_bwd(qkv3, do_a3, lse_a, dsum_a3, tabs_a, g, d).reshape(n, 3 * A_WIDTH)
        w_g_t = w_qkv[:, g * 3 * A_WIDTH:(g + 1) * 3 * A_WIDTH].T
        dhn_a = _matmul(dqkv, w_g_t, out_dtype=F32, tm=mm, tn=D_MODEL, name=f"a_qkv_dx_g{g}",
                        epilogue=_add_epilogue, extras=[(dhn_a, (mm, D_MODEL), lambda j, i, kk: (i, j))])
        gw_parts.append(_matmul(hn_a_t, dqkv, out_dtype=F32, tm=mm, tn=1024, tk=2048, name=f"a_qkv_dw_g{g}"))
    gw_parts.append(_matmul(hn_a_t, dz_a, out_dtype=F32, tm=mm, tn=1024, tk=2048, name="a_gate_dw"))
    gw_a_in = jnp.concatenate(gw_parts, axis=1)

    def first_bwd(xt, dhn, dh1t, g):
        dx, dg = _rms_bwd(xt, g, dhn)
        return dh1t + dx, dg

    grad_x, dg_a_pre = _rowwise(first_bwd, [x2, dhn_a, dh1], [w["a_pre_norm"]], [(D_MODEL, F32)], [D_MODEL],
                                tm=tm, name="a_pre_norm_bwd")

    gw_down = jnp.concatenate([gw_down_p[:, :B_KV_LORA], gw_down_p[:, B_KV_LORA + B_NOPE:B_KV_LORA + B_QK_DIM]], axis=1)
    gw_up = jnp.concatenate([gw_upk.reshape(B_KV_LORA, B_HEADS, LANES)[:, :, :B_NOPE],
                             gw_upv.reshape(B_KV_LORA, B_HEADS, B_VDIM)], axis=2).reshape(B_KV_LORA, -1)
    gw_q_up = gw_q_p.reshape(B_Q_LORA, B_HEADS, LANES)[:, :, :B_QK_DIM].reshape(B_Q_LORA, -1)
    grads = {"a_w_in": gw_a_in, "a_w_out": gw_a_out, "kv_w_down": gw_down, "kv_w_up": gw_up,
             "b_w_in": gw_b_in, "b_w_q_up": gw_q_up, "b_w_out": gw_b_out}
    gains = {"a_pre_norm": dg_a_pre, "a_post_norm": dg_a_post, "kv_norm": dg_kv, "kv_latent_norm": dg_latent,
             "b_pre_norm": dg_b_pre, "b_q_norm": dg_q_norm, "b_post_norm": dg_b_post}
    gains = {k: jnp.sum(a, axis=0) for k, a in gains.items()}
    return jnp.sum(loss_acc), grad_x.reshape(bsz, t, D_MODEL), grads, gains


WEIGHT_ORDER = ("a_pre_norm", "a_w_in", "a_w_out", "a_post_norm", "kv_norm", "kv_w_down", "kv_latent_norm",
                "kv_w_up", "b_pre_norm", "b_w_in", "b_q_norm", "b_w_q_up", "b_w_out", "b_post_norm")
MATRICES = (("a_w_in", 1024, 10240, 1), ("a_w_out", 1024, 1024, 0), ("kv_w_down", 1024, 288, 0),
            ("kv_w_up", 256, 2048, 1), ("b_w_in", 1024, 1408, 1), ("b_w_q_up", 384, 1536, 1),
            ("b_w_out", 1024, 1024, 0))
SHARDED_GAINS = ("a_pre_norm", "a_post_norm")
GAIN_WIDTHS = (("a_pre_norm", 1024), ("a_post_norm", 1024), ("kv_norm", 1024), ("kv_latent_norm", 256),
               ("b_pre_norm", 1024), ("b_q_norm", 384), ("b_post_norm", 1024))
GAIN_ROWS = 48


def _shard_rows(rows, cols):
    return rows * cols // (N_DEV * LANES)


def _whole_from_blocks(blocks, rows, cols, axis):
    if axis == 1:
        return blocks.reshape(N_DEV, rows, cols // N_DEV).transpose(1, 0, 2).reshape(rows, cols)
    return blocks.reshape(rows, cols)


def _blocks_from_whole(whole, rows, cols, axis):
    if axis == 1:
        whole = whole.reshape(rows, N_DEV, cols // N_DEV).transpose(1, 0, 2)
    return whole.reshape(N_DEV, -1, LANES)


def kernel(x, positions, a_pre_norm, a_w_in, a_w_out, a_post_norm, kv_norm, kv_w_down, kv_latent_norm, kv_w_up, b_pre_norm, b_w_in, b_q_norm, b_w_q_up, b_w_out, b_post_norm, loss_target, m_a_pre_norm, m_a_w_in, m_a_w_out, m_a_post_norm, m_kv_norm, m_kv_w_down, m_kv_latent_norm, m_kv_w_up, m_b_pre_norm, m_b_w_in, m_b_q_norm, m_b_w_q_up, m_b_w_out, m_b_post_norm, v_a_pre_norm, v_a_w_in, v_a_w_out, v_a_post_norm, v_kv_norm, v_kv_w_down, v_kv_latent_norm, v_kv_w_up, v_b_pre_norm, v_b_w_in, v_b_q_norm, v_b_w_q_up, v_b_w_out, v_b_post_norm):
    weights = dict(a_pre_norm=a_pre_norm, a_w_in=a_w_in, a_w_out=a_w_out, a_post_norm=a_post_norm, kv_norm=kv_norm,
                   kv_w_down=kv_w_down, kv_latent_norm=kv_latent_norm, kv_w_up=kv_w_up, b_pre_norm=b_pre_norm,
                   b_w_in=b_w_in, b_q_norm=b_q_norm, b_w_q_up=b_w_q_up, b_w_out=b_w_out, b_post_norm=b_post_norm)
    m_in = dict(a_pre_norm=m_a_pre_norm, a_w_in=m_a_w_in, a_w_out=m_a_w_out, a_post_norm=m_a_post_norm,
                kv_norm=m_kv_norm, kv_w_down=m_kv_w_down, kv_latent_norm=m_kv_latent_norm, kv_w_up=m_kv_w_up,
                b_pre_norm=m_b_pre_norm, b_w_in=m_b_w_in, b_q_norm=m_b_q_norm, b_w_q_up=m_b_w_q_up,
                b_w_out=m_b_w_out, b_post_norm=m_b_post_norm)
    v_in = dict(a_pre_norm=v_a_pre_norm, a_w_in=v_a_w_in, a_w_out=v_a_w_out, a_post_norm=v_a_post_norm,
                kv_norm=v_kv_norm, kv_w_down=v_kv_w_down, kv_latent_norm=v_kv_latent_norm, kv_w_up=v_kv_w_up,
                b_pre_norm=v_b_pre_norm, b_w_in=v_b_w_in, b_q_norm=v_b_q_norm, b_w_q_up=v_b_w_q_up,
                b_w_out=v_b_w_out, b_post_norm=v_b_post_norm)
    me = 4 * lax.axis_index("x") + 2 * lax.axis_index("y") + lax.axis_index("c")

    flat = jnp.concatenate([weights[name].astype(BF16).reshape(-1, LANES) for name, _, _, _ in MATRICES], axis=0)
    gathered = _all_gather(flat, "gather_weights")
    whole = {}
    off = 0
    for name, rows, cols, axis in MATRICES:
        nr = _shard_rows(rows, cols)
        whole[name] = _whole_from_blocks(gathered[:, off:off + nr], rows, cols, axis)
        off += nr
    gain_shard = jnp.concatenate([weights[name].reshape(1, LANES) for name in SHARDED_GAINS]
                                 + [jnp.zeros((8 - len(SHARDED_GAINS), LANES), F32)], axis=0)
    gain_blocks = _all_gather(gain_shard, "gather_gains")
    for i, name in enumerate(SHARDED_GAINS):
        whole[name] = gain_blocks[:, i, :].reshape(1, D_MODEL)
    for name in ("kv_norm", "kv_latent_norm", "b_pre_norm", "b_q_norm", "b_post_norm"):
        whole[name] = weights[name].reshape(1, -1)

    loss_part, grad_x, grads, gains = _local_step(x, positions, loss_target, whole)

    blocks = jnp.concatenate([_blocks_from_whole(grads[name], rows, cols, axis).astype(BF16)
                              for name, rows, cols, axis in MATRICES], axis=1)
    landed = _exchange_blocks(blocks, "scatter_grads")
    summed = _sum_slots(landed, tr=2512, name="sum_grads")
    grad_out = {}
    off = 0
    for name, rows, cols, axis in MATRICES:
        nr = _shard_rows(rows, cols)
        grad_out[name] = summed[off:off + nr].reshape(weights[name].shape)
        off += nr

    vec = jnp.concatenate([gains[name] for name, _ in GAIN_WIDTHS] + [jnp.full((LANES,), loss_part, F32)])
    vec = jnp.pad(vec, (0, GAIN_ROWS * LANES - vec.shape[0])).reshape(GAIN_ROWS, LANES)
    total = _sum_slots(_all_gather(vec, "gather_gain_grads"), tr=GAIN_ROWS, name="sum_gain_grads").reshape(-1)
    off = 0
    for name, width in GAIN_WIDTHS:
        g = total[off:off + width]
        if name in SHARDED_GAINS:
            g = lax.dynamic_slice(g, (me * LANES,), (LANES,))
        grad_out[name] = g.reshape(weights[name].shape)
        off += width
    loss = total[off]

    deltas, new_m, new_v = {}, {}, {}
    for name in WEIGHT_ORDER:
        deltas[name], new_m[name], new_v[name] = _adamw(weights[name], grad_out[name], m_in[name], v_in[name],
                                                        "adamw_" + name)
    return (loss, grad_x, *[grad_out[k] for k in WEIGHT_ORDER], *[deltas[k] for k in WEIGHT_ORDER],
            *[new_m[k] for k in WEIGHT_ORDER], *[new_v[k] for k in WEIGHT_ORDER])
```

```python
import jax
import jax.numpy as jnp
from jax import lax
from jax.experimental import pallas as pl
from jax.experimental.pallas import tpu as pltpu

F32 = jnp.float32
BF16 = jnp.bfloat16

N_DEV = 8
D_MODEL = 1024
NORM_EPS = 1e-6
A_GROUPS = 3
A_DILATIONS = (1, 4, 16)
A_HEADS = 8
A_HEAD_DIM = 128
A_WIDTH = 1024
A_ROT_DIM = 32
A_ROPE_THETA = 500000.0
A_QKV = A_GROUPS * 3 * A_WIDTH
B_HEADS = 16
B_NOPE = 64
B_ROPE = 32
B_QK_DIM = 96
B_VDIM = 64
B_Q_LORA = 384
B_KV_LORA = 256
B_ROPE_THETA = 10000.0
B_KPAD = B_HEADS * 128
ADAM_LR = 0.001
ADAM_B1 = 0.9
ADAM_B2 = 0.999
ADAM_EPS = 1e-08
ADAM_WD = 0.01
ADAM_STEP = 10

LANES = 128
BAND = 128
NEG = -1e30
VMEM_LIMIT = 56 * 1024 * 1024
MESH = pl.DeviceIdType.MESH


def _cparams(sem):
    return pltpu.CompilerParams(dimension_semantics=sem, vmem_limit_bytes=VMEM_LIMIT)


def _rowwise(fn, rows, bcast, outs, accs=(), *, tm, name):
    n = rows[0].shape[0]
    nr, nb, no = len(rows), len(bcast), len(outs)

    def body(*refs):
        res = fn(*[r[...] for r in refs[:nr + nb]])
        out_refs = refs[nr + nb:nr + nb + no]
        acc_refs = refs[nr + nb + no:]
        for r, v in zip(out_refs, res[:no]):
            r[...] = v.astype(r.dtype)
        if acc_refs:
            @pl.when(pl.program_id(0) == 0)
            def _():
                for r in acc_refs:
                    r[...] = jnp.zeros_like(r)
            for r, v in zip(acc_refs, res[no:]):
                r[...] += v.reshape(tm // 8, 8, v.shape[-1]).sum(axis=0)

    in_specs = [pl.BlockSpec((tm, a.shape[1]), lambda i: (i, 0)) for a in rows]
    in_specs += [pl.BlockSpec(a.shape, lambda i: (0, 0)) for a in bcast]
    out_specs = [pl.BlockSpec((tm, c), lambda i: (i, 0)) for c, _ in outs]
    out_specs += [pl.BlockSpec((8, c), lambda i: (0, 0)) for c in accs]
    out_shape = [jax.ShapeDtypeStruct((n, c), dt) for c, dt in outs]
    out_shape += [jax.ShapeDtypeStruct((8, c), F32) for c in accs]
    return pl.pallas_call(
        body, name=name, grid=(n // tm,), in_specs=in_specs, out_specs=out_specs, out_shape=out_shape,
        compiler_params=_cparams(("arbitrary",)))(*rows, *bcast)


def _matmul(a, b, *, out_dtype, tm, tn, tk=None, name, epilogue=None, extras=()):
    m, k = a.shape
    n = b.shape[1]
    tk = tk or k
    nk = k // tk
    ne = len(extras)

    def body(*refs):
        a_ref, b_ref = refs[:2]
        ex = refs[2:2 + ne]
        o_ref = refs[2 + ne]
        part = _dot(a_ref[...].astype(BF16), b_ref[...].astype(BF16))

        def finish(acc):
            if epilogue is None:
                o_ref[...] = acc.astype(o_ref.dtype)
            else:
                epilogue(acc, o_ref, pl.program_id(0), *[e[...] for e in ex])

        if nk == 1:
            finish(part)
        else:
            acc_ref = refs[-1]
            kk = pl.program_id(2)

            @pl.when(kk == 0)
            def _():
                acc_ref[...] = part

            @pl.when(kk > 0)
            def _():
                acc_ref[...] += part

            @pl.when(kk == nk - 1)
            def _():
                finish(acc_ref[...])

    in_specs = [pl.BlockSpec((tm, tk), lambda j, i, kk: (i, kk)),
                pl.BlockSpec((tk, tn), lambda j, i, kk: (kk, j))]
    in_specs += [pl.BlockSpec(bs, im) for _, bs, im in extras]
    return pl.pallas_call(
        body, name=name, grid=(n // tn, m // tm, nk), in_specs=in_specs,
        out_specs=pl.BlockSpec((tm, tn), lambda j, i, kk: (i, j)),
        out_shape=jax.ShapeDtypeStruct((m, n), out_dtype),
        scratch_shapes=[pltpu.VMEM((tm, tn), F32)] if nk > 1 else [],
        compiler_params=_cparams(("parallel", "parallel", "arbitrary")))(a, b, *[e[0] for e in extras])


def _add_epilogue(acc, o_ref, j, prev):
    o_ref[...] = (acc + prev).astype(o_ref.dtype)


def _rope(x, c, sp, sm):
    return x * c + pltpu.roll(x, 16, 1) * sp + pltpu.roll(x, LANES - 16, 1) * sm


def _rope_t(dy, c, sp, sm):
    return dy * c + pltpu.roll(dy * sp, LANES - 16, 1) + pltpu.roll(dy * sm, 16, 1)


def _rope_tables(positions, theta, rot_dim, lane0):
    half = rot_dim // 2
    inv_freq = 1.0 / (theta ** (jnp.arange(half, dtype=F32) * (2.0 / rot_dim)))
    ang = positions.astype(F32)[..., None] * inv_freq
    cos, sin = jnp.cos(ang), jnp.sin(ang)
    shape = positions.shape + (LANES,)
    c = jnp.ones(shape, F32).at[..., lane0:lane0 + half].set(cos).at[..., lane0 + half:lane0 + rot_dim].set(cos)
    sp = jnp.zeros(shape, F32).at[..., lane0 + half:lane0 + rot_dim].set(sin)
    sm = jnp.zeros(shape, F32).at[..., lane0:lane0 + half].set(-sin)
    return c, sp, sm


def _rms(x, g):
    xf = x.astype(F32)
    return xf * lax.rsqrt(jnp.mean(xf * xf, axis=-1, keepdims=True) + NORM_EPS) * g


def _rms_bwd(x, g, dy):
    xf = x.astype(F32)
    rstd = lax.rsqrt(jnp.mean(xf * xf, axis=-1, keepdims=True) + NORM_EPS)
    xhat = xf * rstd
    dxhat = dy * g
    dx = rstd * (dxhat - xhat * jnp.mean(dxhat * xhat, axis=-1, keepdims=True))
    return dx, dy * xhat


def _silu(z):
    return z * jax.nn.sigmoid(z)


def _silu_grad(z):
    s = jax.nn.sigmoid(z)
    return s * (1.0 + z * (1.0 - s))


def _dot_nt(a, b):
    return lax.dot_general(a, b, (((1,), (1,)), ((), ())), preferred_element_type=F32)


def _dot_tn(a, b):
    return lax.dot_general(a, b, (((0,), (0,)), ((), ())), preferred_element_type=F32)


def _dot(a, b):
    return jnp.dot(a, b, preferred_element_type=F32)


def _attn_a_fwd(qkv, state, g, d, first, last):
    bsz, t, _ = qkv.shape
    ln = t // d
    nb = ln // BAND
    scale = A_HEAD_DIM ** -0.5
    qv = qkv.reshape(bsz, ln, d * A_QKV)
    blk = (None, BAND, A_WIDTH)

    def col(s):
        return lambda b, r, l: (b, l, r * 9 + 3 * g + s)

    def colp(s):
        return lambda b, r, l: (b, jnp.maximum(l - 1, 0), r * 9 + 3 * g + s)

    st_spec = pl.BlockSpec(blk, lambda b, r, l: (b, l, r))
    n_state = 0 if first else 3

    def body(q_ref, kc_ref, kp_ref, vc_ref, vp_ref, *rest):
        st_in, outs = rest[:n_state], rest[n_state:]
        qi = lax.broadcasted_iota(jnp.int32, (BAND, BAND), 0)
        ki = lax.broadcasted_iota(jnp.int32, (BAND, BAND), 1)
        mask_c = ki <= qi
        mask_p = ki >= qi
        pen_p = jnp.where(pl.program_id(2) > 0, 0.0, NEG)
        for h in range(A_HEADS):
            hs = slice(h * A_HEAD_DIM, (h + 1) * A_HEAD_DIM)
            q = q_ref[:, hs]
            s_c = jnp.where(mask_c, _dot_nt(q, kc_ref[:, hs]) * scale, NEG)
            s_p = jnp.where(mask_p, _dot_nt(q, kp_ref[:, hs]) * scale, NEG) + pen_p
            m_new = jnp.maximum(jnp.max(s_c, axis=1, keepdims=True), jnp.max(s_p, axis=1, keepdims=True))
            if not first:
                m_old = st_in[1][:, hs][:, :1]
                m_new = jnp.maximum(m_new, m_old)
            p_c = jnp.exp(s_c - m_new)
            p_p = jnp.exp(s_p - m_new)
            l_new = jnp.sum(p_c, axis=1, keepdims=True) + jnp.sum(p_p, axis=1, keepdims=True)
            acc = _dot(p_c.astype(BF16), vc_ref[:, hs]) + _dot(p_p.astype(BF16), vp_ref[:, hs])
            if not first:
                alpha = jnp.exp(m_old - m_new)
                l_new = l_new + alpha * st_in[2][:, hs][:, :1]
                acc = acc + alpha * st_in[0][:, hs]
            if last:
                outs[0][:, hs] = acc / l_new
                outs[1][:, hs] = jnp.broadcast_to(m_new + jnp.log(l_new), (BAND, A_HEAD_DIM))
            else:
                outs[0][:, hs] = acc
                outs[1][:, hs] = jnp.broadcast_to(m_new, (BAND, A_HEAD_DIM))
                outs[2][:, hs] = jnp.broadcast_to(l_new, (BAND, A_HEAD_DIM))

    n_out = 2 if last else 3
    in_specs = [pl.BlockSpec(blk, col(0)), pl.BlockSpec(blk, col(1)), pl.BlockSpec(blk, colp(1)),
                pl.BlockSpec(blk, col(2)), pl.BlockSpec(blk, colp(2))] + [st_spec] * n_state
    st_args = [] if first else [s.reshape(bsz, ln, d * A_WIDTH) for s in state]
    res = pl.pallas_call(
        body, name=f"attn_a_fwd_g{g}", grid=(bsz, d, nb), in_specs=in_specs,
        out_specs=[st_spec] * n_out,
        out_shape=[jax.ShapeDtypeStruct((bsz, ln, d * A_WIDTH), F32)] * n_out,
        compiler_params=_cparams(("parallel", "parallel", "arbitrary")))(qv, qv, qv, qv, qv, *st_args)
    return [x.reshape(bsz, t, A_WIDTH) for x in res]


def _attn_a_bwd(qkv, do, lse, dsum, tabs, g, d):
    bsz, t, _ = qkv.shape
    ln = t // d
    nb = ln // BAND
    scale = A_HEAD_DIM ** -0.5
    qv = qkv.reshape(bsz, ln, d * A_QKV)
    blk = (None, BAND, A_WIDTH)

    def col(s, off):
        def im(b, r, l):
            return (b, jnp.clip(l + off, 0, nb - 1), r * 9 + 3 * g + s)
        return im

    def tok(off):
        def im(b, r, l):
            return (b, jnp.clip(l + off, 0, nb - 1), r)
        return im

    def body(q_ref, qn_ref, kc_ref, kp_ref, vc_ref, vp_ref, do_ref, don_ref, lse_ref, lsen_ref,
             ds_ref, dsn_ref, c_ref, sp_ref, sm_ref, out_ref):
        l_idx = pl.program_id(2)
        qi = lax.broadcasted_iota(jnp.int32, (BAND, BAND), 0)
        ki = lax.broadcasted_iota(jnp.int32, (BAND, BAND), 1)
        mask_c = ki <= qi
        mask_p = ki >= qi
        pen_p = jnp.where(l_idx > 0, 0.0, NEG)
        pen_n = jnp.where(l_idx < nb - 1, 0.0, NEG)
        c, sp, sm = c_ref[...], sp_ref[...], sm_ref[...]

        def block(q, k, v, do_b, lse_c, dsum_c, mask, pen):
            s = jnp.where(mask, _dot_nt(q, k) * scale, NEG) + pen
            p = jnp.exp(s - lse_c)
            dp = _dot_nt(do_b, v)
            return p, p * (dp - dsum_c) * scale

        for h in range(A_HEADS):
            hs = slice(h * A_HEAD_DIM, (h + 1) * A_HEAD_DIM)
            q, qn = q_ref[:, hs], qn_ref[:, hs]
            kc, kp, vc, vp = kc_ref[:, hs], kp_ref[:, hs], vc_ref[:, hs], vp_ref[:, hs]
            do_b, do_n = do_ref[:, hs], don_ref[:, hs]
            lse_c, lse_n = lse_ref[:, hs][:, :1], lsen_ref[:, hs][:, :1]
            ds_c, ds_n = ds_ref[:, hs][:, :1], dsn_ref[:, hs][:, :1]
            p, ds = block(q, kc, vc, do_b, lse_c, ds_c, mask_c, 0.0)
            dsb = ds.astype(BF16)
            dq = _dot(dsb, kc)
            dk = _dot_tn(dsb, q)
            dv = _dot_tn(p.astype(BF16), do_b)
            p, ds = block(q, kp, vp, do_b, lse_c, ds_c, mask_p, pen_p)
            dq = dq + _dot(ds.astype(BF16), kp)
            p, ds = block(qn, kc, vc, do_n, lse_n, ds_n, mask_p, pen_n)
            dk = dk + _dot_tn(ds.astype(BF16), qn)
            dv = dv + _dot_tn(p.astype(BF16), do_n)
            out_ref[:, h * A_HEAD_DIM:(h + 1) * A_HEAD_DIM] = _rope_t(dq, c, sp, sm).astype(BF16)
            out_ref[:, A_WIDTH + h * A_HEAD_DIM:A_WIDTH + (h + 1) * A_HEAD_DIM] = _rope_t(dk, c, sp, sm).astype(BF16)
            out_ref[:, 2 * A_WIDTH + h * A_HEAD_DIM:2 * A_WIDTH + (h + 1) * A_HEAD_DIM] = dv.astype(BF16)

    def view(a):
        return a.reshape(bsz, ln, d * a.shape[-1])

    tspec = pl.BlockSpec((None, BAND, LANES), tok(0))
    in_specs = [pl.BlockSpec(blk, col(0, 0)), pl.BlockSpec(blk, col(0, 1)),
                pl.BlockSpec(blk, col(1, 0)), pl.BlockSpec(blk, col(1, -1)),
                pl.BlockSpec(blk, col(2, 0)), pl.BlockSpec(blk, col(2, -1)),
                pl.BlockSpec(blk, tok(0)), pl.BlockSpec(blk, tok(1)),
                pl.BlockSpec(blk, tok(0)), pl.BlockSpec(blk, tok(1)),
                pl.BlockSpec(blk, tok(0)), pl.BlockSpec(blk, tok(1)),
                tspec, tspec, tspec]
    dov, lsev, dsv = view(do), view(lse), view(dsum)
    out = pl.pallas_call(
        body, name=f"attn_a_bwd_g{g}", grid=(bsz, d, nb), in_specs=in_specs,
        out_specs=pl.BlockSpec((None, BAND, 3 * A_WIDTH), tok(0)),
        out_shape=jax.ShapeDtypeStruct((bsz, ln, d * 3 * A_WIDTH), BF16),
        compiler_params=_cparams(("parallel", "parallel", "arbitrary")))(
            qv, qv, qv, qv, qv, qv, dov, dov, lsev, lsev, dsv, dsv, *[view(x) for x in tabs])
    return out.reshape(bsz, t, 3 * A_WIDTH)


B_TQ = 256
B_SCALE = B_QK_DIM ** -0.5
LOG2E = 1.4426950408889634
B_C2 = B_SCALE * LOG2E


def _b_row_tile(tq):
    return pl.BlockSpec((None, None, 8, tq), lambda b, j, i: (b, j, 0, i))


def _b_specs(t, tq, tk):
    pair_tile = pl.BlockSpec((None, tq, 2 * LANES), lambda b, j, i: (b, i, j))
    pair_full = pl.BlockSpec((None, t, 2 * LANES), lambda b, j, i: (b, 0, j))
    one_tile = pl.BlockSpec((None, tk, LANES), lambda b, j, i: (b, i, j))
    one_full = pl.BlockSpec((None, t, LANES), lambda b, j, i: (b, 0, j))
    tab_tile = pl.BlockSpec((None, tq, LANES), lambda b, j, i: (b, i, 0))
    return pair_tile, pair_full, one_tile, one_full, tab_tile


def _key_le_query(kb, qb, tk, tq):
    kpos = kb * tk + lax.broadcasted_iota(jnp.int32, (tk, tq), 0)
    qpos = qb * tq + lax.broadcasted_iota(jnp.int32, (tk, tq), 1)
    return kpos <= qpos


def _attn_b_fwd(q, kpad, vt1):
    bsz, t, _ = q.shape
    tq = tk = B_TQ
    nq = t // tq

    def body(q_ref, k_ref, vt_ref, o_ref, lse_ref):
        qblk = pl.program_id(2)
        qs = [q_ref[:, hh * LANES:(hh + 1) * LANES] for hh in range(2)]

        def make_step(masked):
            def step(kb, carry):
                start = pl.multiple_of(kb * tk, tk)
                ss = [_dot_nt(k_ref[pl.ds(start, tk), hh * LANES:(hh + 1) * LANES], qs[hh]) for hh in range(2)]
                ms, als, ps = [], [], []
                for hh in range(2):
                    s = ss[hh] * B_C2
                    if masked:
                        s = jnp.where(_key_le_query(kb, qblk, tk, tq), s, NEG)
                    m_new = jnp.maximum(carry[2 * hh], jnp.max(s, axis=0, keepdims=True))
                    als.append(jnp.exp2(carry[2 * hh] - m_new))
                    ps.append(jnp.exp2(s - m_new).astype(BF16))
                    ms.append(m_new)
                pv = [_dot(vt_ref[hh * LANES:(hh + 1) * LANES, pl.ds(start, tk)], ps[hh]) for hh in range(2)]
                return (ms[0], als[0] * carry[1] + pv[0], ms[1], als[1] * carry[3] + pv[1])
            return step

        init = (jnp.full((1, tq), NEG, F32), jnp.zeros((LANES, tq), F32)) * 2
        carry = lax.fori_loop(0, qblk, make_step(False), init)
        carry = make_step(True)(qblk, carry)
        ls = [carry[1][B_VDIM:B_VDIM + 1], carry[3][B_VDIM:B_VDIM + 1]]
        o_t = jnp.concatenate([carry[1][:B_VDIM] / ls[0], carry[3][:B_VDIM] / ls[1]], axis=0)
        o_ref[...] = o_t.T
        lse_ref[...] = jnp.concatenate([carry[0] + jnp.log2(ls[0]), carry[2] + jnp.log2(ls[1]),
                                        jnp.zeros((6, tq), F32)], axis=0)

    pair_tile, pair_full, one_tile, _, _ = _b_specs(t, tq, tk)
    vt_spec = pl.BlockSpec((None, 2 * LANES, t), lambda b, j, i: (b, j, 0))
    return pl.pallas_call(
        body, name="attn_b_fwd", grid=(bsz, B_HEADS // 2, nq),
        in_specs=[pair_tile, pair_full, vt_spec],
        out_specs=[one_tile, _b_row_tile(tq)],
        out_shape=[jax.ShapeDtypeStruct((bsz, t, B_HEADS * B_VDIM), F32),
                   jax.ShapeDtypeStruct((bsz, B_HEADS // 2, 8, t), F32)],
        compiler_params=_cparams(("parallel", "parallel", "arbitrary")))(q, kpad, vt1)


def _attn_b_dq(q, kpad, v, do, o, lse2, tabs):
    bsz, t, _ = q.shape
    tq = tk = B_TQ
    nq = t // tq

    def body(q_ref, k_ref, v_ref, do_ref, o_ref, lse_ref, c_ref, sp_ref, sm_ref, dq_ref, dsum_ref):
        qblk = pl.program_id(2)
        lane = lax.broadcasted_iota(jnp.int32, (tq, LANES), 1)
        do_b = do_ref[...]
        prod = do_b.astype(F32) * o_ref[...]
        hi = prod.astype(BF16)
        lo = (prod - hi.astype(F32)).astype(BF16)
        sel_lane = lax.broadcasted_iota(jnp.int32, (8, LANES), 1)
        sel_row = lax.broadcasted_iota(jnp.int32, (8, LANES), 0)
        sel = jnp.where((sel_lane < B_VDIM) == (sel_row == 0), 1.0, 0.0)
        sel = jnp.where(sel_row < 2, sel, 0.0).astype(BF16)
        dsum = _dot_nt(sel, hi) + _dot_nt(sel, lo)
        dsum_ref[...] = dsum
        lse = lse_ref[...]
        qs = [q_ref[:, hh * LANES:(hh + 1) * LANES] for hh in range(2)]
        zero = jnp.zeros_like(do_b)
        dos = [jnp.where(lane < B_VDIM, do_b, zero), jnp.where(lane < B_VDIM, zero, do_b)]

        def make_step(masked):
            def step(kb, carry):
                start = pl.multiple_of(kb * tk, tk)
                ks = [k_ref[pl.ds(start, tk), hh * LANES:(hh + 1) * LANES] for hh in range(2)]
                vb = v_ref[pl.ds(start, tk), :]
                ss = [_dot_nt(ks[hh], qs[hh]) for hh in range(2)]
                dps = [_dot_nt(vb, dos[hh]) for hh in range(2)]
                dss = []
                for hh in range(2):
                    s = ss[hh] * B_C2
                    if masked:
                        s = jnp.where(_key_le_query(kb, qblk, tk, tq), s, NEG)
                    p = jnp.exp2(s - lse[hh:hh + 1])
                    dss.append((p * (dps[hh] - dsum[hh:hh + 1]) * B_SCALE).astype(BF16))
                return tuple(carry[hh] + _dot_tn(ks[hh], dss[hh]) for hh in range(2))
            return step

        init = (jnp.zeros((LANES, tq), F32),) * 2
        carry = lax.fori_loop(0, qblk, make_step(False), init)
        carry = make_step(True)(qblk, carry)
        c, sp, sm = c_ref[...], sp_ref[...], sm_ref[...]
        for hh in range(2):
            dq_ref[:, hh * LANES:(hh + 1) * LANES] = _rope_t(carry[hh].T, c, sp, sm).astype(BF16)

    pair_tile, pair_full, one_tile, one_full, tab_tile = _b_specs(t, tq, tk)
    row_tile = _b_row_tile(tq)
    return pl.pallas_call(
        body, name="attn_b_dq", grid=(bsz, B_HEADS // 2, nq),
        in_specs=[pair_tile, pair_full, one_full, one_tile, one_tile, row_tile, tab_tile, tab_tile, tab_tile],
        out_specs=[pair_tile, row_tile],
        out_shape=[jax.ShapeDtypeStruct((bsz, t, B_KPAD), BF16),
                   jax.ShapeDtypeStruct((bsz, B_HEADS // 2, 8, t), F32)],
        compiler_params=_cparams(("parallel", "parallel", "arbitrary")))(q, kpad, v, do, o, lse2, *tabs)


def _attn_b_dkv(q, kpad, v, do, lse2, dsum):
    bsz, t, _ = q.shape
    tq = tk = B_TQ
    nq = t // tq

    def body(q_ref, k_ref, v_ref, do_ref, lse_ref, dsum_ref, dk_ref, dv_ref):
        kblk = pl.program_id(2)
        lane = lax.broadcasted_iota(jnp.int32, (tk, LANES), 1)
        ks = [k_ref[:, hh * LANES:(hh + 1) * LANES] for hh in range(2)]
        vb = v_ref[...]
        zero = jnp.zeros_like(vb)
        vs = [jnp.where(lane < B_VDIM, vb, zero), jnp.where(lane < B_VDIM, zero, vb)]

        def make_step(masked):
            def step(qb, carry):
                start = pl.multiple_of(qb * tq, tq)
                qs = [q_ref[pl.ds(start, tq), hh * LANES:(hh + 1) * LANES] for hh in range(2)]
                do_b = do_ref[pl.ds(start, tq), :]
                ss = [_dot_nt(ks[hh], qs[hh]) for hh in range(2)]
                dps = [_dot_nt(vs[hh], do_b) for hh in range(2)]
                pbs, dss = [], []
                for hh in range(2):
                    s = ss[hh] * B_C2
                    if masked:
                        s = jnp.where(_key_le_query(kblk, qb, tk, tq), s, NEG)
                    p = jnp.exp2(s - lse_ref[hh:hh + 1, pl.ds(start, tq)])
                    dss.append((p * (dps[hh] - dsum_ref[hh:hh + 1, pl.ds(start, tq)]) * B_SCALE).astype(BF16))
                    pbs.append(p.astype(BF16))
                return (carry[0] + _dot(dss[0], qs[0]), carry[1] + _dot(dss[1], qs[1]),
                        carry[2] + _dot(pbs[0], do_b), carry[3] + _dot(pbs[1], do_b))
            return step

        init = (jnp.zeros((tk, LANES), F32),) * 4
        carry = make_step(True)(kblk, init)
        carry = lax.fori_loop(kblk + 1, nq, make_step(False), carry)
        dk_ref[:, :LANES] = carry[0]
        dk_ref[:, LANES:] = carry[1]
        dv_ref[...] = jnp.where(lane < B_VDIM, carry[2], carry[3])

    pair_tile, pair_full, one_tile, one_full, _ = _b_specs(t, tq, tk)
    row_full = pl.BlockSpec((None, None, 8, t), lambda b, j, i: (b, j, 0, 0))
    return pl.pallas_call(
        body, name="attn_b_dkv", grid=(bsz, B_HEADS // 2, nq),
        in_specs=[pair_full, pair_tile, one_tile, one_full, row_full, row_full],
        out_specs=[pair_tile, one_tile],
        out_shape=[jax.ShapeDtypeStruct((bsz, t, B_KPAD), F32),
                   jax.ShapeDtypeStruct((bsz, t, B_HEADS * B_VDIM), F32)],
        compiler_params=_cparams(("parallel", "parallel", "arbitrary")))(q, kpad, v, do, lse2, dsum)


ANY = pl.BlockSpec(memory_space=pl.ANY)


def _all_gather(shard, name):
    def body(x_ref, out_ref, send_sems, recv_sems, local_sem):
        x, y, c = lax.axis_index("x"), lax.axis_index("y"), lax.axis_index("c")
        me, sibling = (x, y, c), (x, y, 1 - c)
        chips = [(1 - x, y), (x, 1 - y), (1 - x, 1 - y)]

        def rows(px, py, pc):
            return out_ref.at[4 * px + 2 * py + pc]

        def copy(k, block, to, src=None):
            return pltpu.make_async_remote_copy(
                src_ref=rows(*block) if src is None else src, dst_ref=rows(*block),
                send_sem=send_sems.at[k], recv_sem=recv_sems.at[k], device_id=to, device_id_type=MESH)

        mine = pltpu.make_async_copy(x_ref, rows(*me), local_sem)
        mine.start()
        first = [copy(0, me, sibling, src=x_ref)]
        first += [copy(1 + j, me, (*chip, c), src=x_ref) for j, chip in enumerate(chips)]
        for cp in first:
            cp.start()
        passed = [copy(4 + j, (*chip, c), sibling) for j, chip in enumerate(chips)]
        for j, chip in enumerate(chips):
            copy(1 + j, (*chip, c), me).wait_recv()
            passed[j].start()
        copy(0, sibling, me).wait_recv()
        for j, chip in enumerate(chips):
            copy(4 + j, (*chip, 1 - c), me).wait_recv()
        for cp in first + passed:
            cp.wait_send()
        mine.wait()

    return pl.pallas_call(
        body, name=name, in_specs=[ANY], out_specs=ANY,
        out_shape=jax.ShapeDtypeStruct((N_DEV,) + shard.shape, shard.dtype),
        scratch_shapes=[pltpu.SemaphoreType.DMA((7,)), pltpu.SemaphoreType.DMA((7,)), pltpu.SemaphoreType.DMA])(shard)


def _exchange_blocks(blocks, name):
    def body(g_ref, out_ref, send_sems, recv_sems, local_sem):
        x, y, c = lax.axis_index("x"), lax.axis_index("y"), lax.axis_index("c")
        me = 4 * x + 2 * y + c

        def peer(k):
            return (1 - x if k & 4 else x, 1 - y if k & 2 else y, 1 - c if k & 1 else c)

        def copy(k):
            px, py, pc = peer(k)
            return pltpu.make_async_remote_copy(
                src_ref=g_ref.at[4 * px + 2 * py + pc], dst_ref=out_ref.at[me],
                send_sem=send_sems.at[k - 1], recv_sem=recv_sems.at[k - 1], device_id=(px, py, pc), device_id_type=MESH)

        def arrival(k):
            px, py, pc = peer(k)
            slot = out_ref.at[4 * px + 2 * py + pc]
            return pltpu.make_async_remote_copy(
                src_ref=slot, dst_ref=slot, send_sem=send_sems.at[k - 1], recv_sem=recv_sems.at[k - 1],
                device_id=(px, py, pc), device_id_type=MESH)

        mine = pltpu.make_async_copy(g_ref.at[me], out_ref.at[me], local_sem)
        mine.start()
        sends = [copy(k) for k in range(1, N_DEV)]
        for cp in sends:
            cp.start()
        for k in range(1, N_DEV):
            arrival(k).wait_recv()
        for cp in sends:
            cp.wait_send()
        mine.wait()

    return pl.pallas_call(
        body, name=name, in_specs=[ANY], out_specs=ANY,
        out_shape=jax.ShapeDtypeStruct(blocks.shape, blocks.dtype),
        scratch_shapes=[pltpu.SemaphoreType.DMA((7,)), pltpu.SemaphoreType.DMA((7,)), pltpu.SemaphoreType.DMA])(blocks)


def _sum_slots(slots, *, tr, name):
    _, r, c = slots.shape

    def body(s_ref, o_ref):
        acc = s_ref[0].astype(F32)
        for s in range(1, N_DEV):
            acc = acc + s_ref[s].astype(F32)
        o_ref[...] = acc

    return pl.pallas_call(
        body, name=name, grid=(r // tr,),
        in_specs=[pl.BlockSpec((N_DEV, tr, c), lambda i: (0, i, 0))],
        out_specs=pl.BlockSpec((tr, c), lambda i: (i, 0)),
        out_shape=jax.ShapeDtypeStruct((r, c), F32),
        compiler_params=_cparams(("parallel",)))(slots)


def _adamw(w, g, m, v, name):
    shape = w.shape
    cols = shape[-1]
    args = [a.reshape(-1, cols) for a in (w, g, m, v)]
    rows = args[0].shape[0]
    tm = 256 if rows % 256 == 0 else rows

    def fn(w_t, g_t, m_t, v_t):
        m_n = ADAM_B1 * m_t + (1.0 - ADAM_B1) * g_t
        v_n = ADAM_B2 * v_t + (1.0 - ADAM_B2) * (g_t * g_t)
        m_hat = m_n / (1.0 - ADAM_B1 ** ADAM_STEP)
        v_hat = v_n / (1.0 - ADAM_B2 ** ADAM_STEP)
        delta = -ADAM_LR * (m_hat / (jnp.sqrt(v_hat) + ADAM_EPS) + ADAM_WD * w_t)
        return delta, m_n, v_n

    delta, m_n, v_n = _rowwise(fn, args, [], [(cols, F32)] * 3, tm=tm, name=name)
    return delta.reshape(shape), m_n.reshape(shape), v_n.reshape(shape)


def _heads_cat(fn, n_heads, width):
    return jnp.concatenate([fn(slice(h * width, (h + 1) * width)) for h in range(n_heads)], axis=1)


def _local_step(x, positions, target, w):
    bsz, t, _ = x.shape
    n = bsz * t
    tm = 256
    mm = 512
    x2 = x.reshape(n, D_MODEL)
    tgt2 = target.reshape(n, D_MODEL)
    tabs_a = _rope_tables(positions, A_ROPE_THETA, A_ROT_DIM, 0)
    tabs_b = _rope_tables(positions, B_ROPE_THETA, B_ROPE, B_NOPE)
    ta2 = [a.reshape(n, LANES) for a in tabs_a]
    tb2 = [a.reshape(n, LANES) for a in tabs_b]

    w_a_in = w["a_w_in"]
    w_qkv, w_z = w_a_in[:, :A_QKV], w_a_in[:, A_QKV:]
    w_a_out = w["a_w_out"]
    w_down = w["kv_w_down"]
    w_down_p = jnp.zeros((D_MODEL, 3 * LANES), BF16).at[:, :B_KV_LORA].set(w_down[:, :B_KV_LORA])
    w_down_p = w_down_p.at[:, B_KV_LORA + B_NOPE:B_KV_LORA + B_QK_DIM].set(w_down[:, B_KV_LORA:])
    wu = w["kv_w_up"].reshape(B_KV_LORA, B_HEADS, B_NOPE + B_VDIM)
    w_upk = jnp.pad(wu[:, :, :B_NOPE], ((0, 0), (0, 0), (0, LANES - B_NOPE))).reshape(B_KV_LORA, B_KPAD)
    w_upv = wu[:, :, B_NOPE:].reshape(B_KV_LORA, B_HEADS * B_VDIM)
    w_b_in = w["b_w_in"]
    w_q_p = jnp.pad(w["b_w_q_up"].reshape(B_Q_LORA, B_HEADS, B_QK_DIM),
                    ((0, 0), (0, 0), (0, LANES - B_QK_DIM))).reshape(B_Q_LORA, B_KPAD)
    w_b_out = w["b_w_out"]

    def tab_extras(tabs2, rows):
        return [(a, (rows, LANES), lambda j, i, kk: (i, 0)) for a in tabs2]

    (hn_a,) = _rowwise(lambda xt, g: (_rms(xt, g),), [x2], [w["a_pre_norm"]], [(D_MODEL, BF16)],
                       tm=tm, name="a_pre_norm")

    def qkv_epilogue(acc, o_ref, j, c, sp, sm):
        is_v = lax.rem(j, 3) == 2

        @pl.when(is_v)
        def _():
            o_ref[...] = acc.astype(BF16)

        @pl.when(jnp.logical_not(is_v))
        def _():
            for h in range(A_HEADS):
                hs = slice(h * A_HEAD_DIM, (h + 1) * A_HEAD_DIM)
                o_ref[:, hs] = _rope(acc[:, hs], c, sp, sm).astype(BF16)

    qkv = _matmul(hn_a, w_qkv, out_dtype=BF16, tm=mm, tn=A_WIDTH, name="a_qkv",
                  epilogue=qkv_epilogue, extras=tab_extras(ta2, mm))
    z_a = _matmul(hn_a, w_z, out_dtype=F32, tm=mm, tn=A_WIDTH, name="a_gate")
    qkv3 = qkv.reshape(bsz, t, A_QKV)
    state = None
    for g, d in enumerate(A_DILATIONS):
        state = _attn_a_fwd(qkv3, state, g, d, g == 0, g == A_GROUPS - 1)
    o_a, lse_a = state
    o_a2 = o_a.reshape(n, A_WIDTH)
    (y_a,) = _rowwise(lambda o, z: (o * _silu(z),), [o_a2, z_a], [], [(A_WIDTH, BF16)], tm=tm, name="a_gate_mul")
    out_a = _matmul(y_a, w_a_out, out_dtype=F32, tm=mm, tn=D_MODEL, name="a_out")

    def mid_fn(xt, out, g_post, g_kv, g_b):
        h1 = xt + _rms(out, g_post)
        return h1, _rms(h1, g_kv), _rms(h1, g_b)

    h1, hn_kv, hn_b = _rowwise(mid_fn, [x2, out_a], [w["a_post_norm"], w["kv_norm"], w["b_pre_norm"]],
                               [(D_MODEL, F32), (D_MODEL, BF16), (D_MODEL, BF16)], tm=tm, name="a_post_norm")

    ckr = _matmul(hn_kv, w_down_p, out_dtype=F32, tm=mm, tn=3 * LANES, name="kv_down")

    def latent_fn(ck, c, sp, sm, g):
        return _rms(ck[:, :B_KV_LORA], g), _rope(ck[:, B_KV_LORA:], c, sp, sm)

    c_kv, k_rope = _rowwise(latent_fn, [ckr, *tb2], [w["kv_latent_norm"]], [(B_KV_LORA, BF16), (LANES, F32)],
                            tm=tm, name="kv_latent_norm")

    def kpad_epilogue(acc, o_ref, j, kr):
        for h in range(acc.shape[1] // LANES):
            hs = slice(h * LANES, (h + 1) * LANES)
            o_ref[:, hs] = (acc[:, hs] + kr).astype(BF16)

    kpad = _matmul(c_kv, w_upk, out_dtype=BF16, tm=mm, tn=1024, name="kv_up_k", epilogue=kpad_epilogue,
                   extras=[(k_rope, (mm, LANES), lambda j, i, kk: (i, 0))])
    v_b = _matmul(c_kv, w_upv, out_dtype=BF16, tm=mm, tn=1024, name="kv_up_v")

    proj_b = _matmul(hn_b, w_b_in, out_dtype=F32, tm=mm, tn=w_b_in.shape[1], name="b_in")
    (c_q,) = _rowwise(lambda p, g: (_rms(p[:, :B_Q_LORA], g),), [proj_b], [w["b_q_norm"]], [(B_Q_LORA, BF16)],
                      tm=tm, name="b_q_norm")

    def q_epilogue(acc, o_ref, j, c, sp, sm):
        for h in range(acc.shape[1] // LANES):
            hs = slice(h * LANES, (h + 1) * LANES)
            o_ref[:, hs] = _rope(acc[:, hs], c, sp, sm).astype(BF16)

    q_b = _matmul(c_q, w_q_p, out_dtype=BF16, tm=mm, tn=1024, name="b_q_up", epilogue=q_epilogue,
                  extras=tab_extras(tb2, mm))
    q_b3, kpad3, v_b3 = q_b.reshape(bsz, t, B_KPAD), kpad.reshape(bsz, t, B_KPAD), v_b.reshape(bsz, t, -1)
    v4 = v_b3.reshape(bsz, t, B_HEADS, B_VDIM)
    vt1 = jnp.concatenate([v4, jnp.ones((bsz, t, B_HEADS, 1), BF16),
                           jnp.zeros((bsz, t, B_HEADS, LANES - B_VDIM - 1), BF16)], axis=3)
    vt1 = vt1.transpose(0, 2, 3, 1).reshape(bsz, B_KPAD, t)
    o_b, lse_b = _attn_b_fwd(q_b3, kpad3, vt1)
    o_b2 = o_b.reshape(n, -1)
    (y_b,) = _rowwise(lambda o, p: (o * _silu(p[:, B_Q_LORA:]),), [o_b2, proj_b], [], [(D_MODEL, BF16)],
                      tm=tm, name="b_gate_mul")
    out_b = _matmul(y_b, w_b_out, out_dtype=F32, tm=mm, tn=D_MODEL, name="b_out")

    def head_fn(h1t, out, tgt, g):
        e = h1t + _rms(out, g) - tgt
        loss_row = 0.5 * jnp.mean(e * e, axis=-1, keepdims=True)
        dh2 = e * (1.0 / D_MODEL)
        d_out, dg = _rms_bwd(out, g, dh2)
        return dh2, d_out, dg, jnp.broadcast_to(loss_row * (1.0 / LANES), (loss_row.shape[0], LANES))

    dh2, d_out_b, dg_b_post, loss_acc = _rowwise(
        head_fn, [h1, out_b, tgt2], [w["b_post_norm"]], [(D_MODEL, F32), (D_MODEL, BF16)], [D_MODEL, LANES],
        tm=tm, name="loss_head")

    dy_b = _matmul(d_out_b, w_b_out.T, out_dtype=F32, tm=mm, tn=D_MODEL, name="b_out_dx")
    gw_b_out = _matmul(y_b.T, d_out_b, out_dtype=F32, tm=mm, tn=D_MODEL, tk=2048, name="b_out_dw")

    def gate_b_bwd(dy, o, p):
        z = p[:, B_Q_LORA:]
        return dy * _silu(z), dy * o * _silu_grad(z)

    do_b, dz_b = _rowwise(gate_b_bwd, [dy_b, o_b2, proj_b], [], [(D_MODEL, BF16), (D_MODEL, F32)],
                          tm=tm, name="b_gate_bwd")
    do_b3 = do_b.reshape(bsz, t, -1)
    dq_b, dsum_b = _attn_b_dq(q_b3, kpad3, v_b3, do_b3, o_b, lse_b, tabs_b)
    dk_b, dv_b = _attn_b_dkv(q_b3, kpad3, v_b3, do_b3, lse_b, dsum_b)
    dq_b2, dk_b2, dv_b2 = dq_b.reshape(n, B_KPAD), dk_b.reshape(n, B_KPAD), dv_b.reshape(n, -1)

    dk_bf, dv_bf = dk_b2.astype(BF16), dv_b2.astype(BF16)
    dc_kv = _matmul(dk_bf, w_upk.T, out_dtype=F32, tm=mm, tn=B_KV_LORA, name="kv_up_k_dx")
    dc_kv = _matmul(dv_bf, w_upv.T, out_dtype=F32, tm=mm, tn=B_KV_LORA, name="kv_up_v_dx",
                    epilogue=_add_epilogue, extras=[(dc_kv, (mm, B_KV_LORA), lambda j, i, kk: (i, j))])
    c_kv_t = c_kv.T
    gw_upk = _matmul(c_kv_t, dk_bf, out_dtype=F32, tm=B_KV_LORA, tn=1024, tk=2048, name="kv_up_k_dw")
    gw_upv = _matmul(c_kv_t, dv_bf, out_dtype=F32, tm=B_KV_LORA, tn=1024, tk=2048, name="kv_up_v_dw")

    def latent_bwd(ck, dck, dk, c, sp, sm, g):
        d1, dg = _rms_bwd(ck[:, :B_KV_LORA], g, dck)
        ksum = dk[:, :LANES]
        for h in range(1, B_HEADS):
            ksum = ksum + dk[:, h * LANES:(h + 1) * LANES]
        lane = lax.broadcasted_iota(jnp.int32, ksum.shape, 1)
        ksum = jnp.where((lane >= B_NOPE) & (lane < B_QK_DIM), ksum, 0.0)
        return jnp.concatenate([d1, _rope_t(ksum, c, sp, sm)], axis=1), dg

    dckr, dg_latent = _rowwise(latent_bwd, [ckr, dc_kv, dk_b2, *tb2], [w["kv_latent_norm"]],
                               [(3 * LANES, BF16)], [B_KV_LORA], tm=tm, name="kv_latent_bwd")
    dhn_kv = _matmul(dckr, w_down_p.T, out_dtype=F32, tm=mm, tn=D_MODEL, name="kv_down_dx")
    gw_down_p = _matmul(hn_kv.T, dckr, out_dtype=F32, tm=mm, tn=3 * LANES, tk=2048, name="kv_down_dw")

    dc_q = _matmul(dq_b2, w_q_p.T, out_dtype=F32, tm=mm, tn=B_Q_LORA, name="b_q_up_dx")
    gw_q_p = _matmul(c_q.T, dq_b2, out_dtype=F32, tm=B_Q_LORA, tn=1024, tk=2048, name="b_q_up_dw")

    def q_norm_bwd(p, dcq, dz, g):
        d1, dg = _rms_bwd(p[:, :B_Q_LORA], g, dcq)
        return jnp.concatenate([d1, dz], axis=1), dg

    dproj_b, dg_q_norm = _rowwise(q_norm_bwd, [proj_b, dc_q, dz_b], [w["b_q_norm"]],
                                  [(w_b_in.shape[1], BF16)], [B_Q_LORA], tm=tm, name="b_q_norm_bwd")
    dhn_b = _matmul(dproj_b, w_b_in.T, out_dtype=F32, tm=mm, tn=D_MODEL, name="b_in_dx")
    gw_b_in = _matmul(hn_b.T, dproj_b, out_dtype=F32, tm=mm, tn=w_b_in.shape[1], tk=2048, name="b_in_dw")

    def mid_bwd(h1t, dh2t, dkv, db, g_kv, g_b, g_post, out):
        dxa, ra = _rms_bwd(h1t, g_kv, dkv)
        dxb, rb = _rms_bwd(h1t, g_b, db)
        dh1 = dh2t + dxa + dxb
        d_out, rp = _rms_bwd(out, g_post, dh1)
        return dh1, d_out, ra, rb, rp

    def mid_bwd_fn(h1t, dh2t, dkv, db, out, g_kv, g_b, g_post):
        return mid_bwd(h1t, dh2t, dkv, db, g_kv, g_b, g_post, out)

    dh1, d_out_a, dg_kv, dg_b_pre, dg_a_post = _rowwise(
        mid_bwd_fn, [h1, dh2, dhn_kv, dhn_b, out_a], [w["kv_norm"], w["b_pre_norm"], w["a_post_norm"]],
        [(D_MODEL, F32), (D_MODEL, BF16)], [D_MODEL] * 3, tm=tm, name="mid_bwd")

    dy_a = _matmul(d_out_a, w_a_out.T, out_dtype=F32, tm=mm, tn=A_WIDTH, name="a_out_dx")
    gw_a_out = _matmul(y_a.T, d_out_a, out_dtype=F32, tm=mm, tn=D_MODEL, tk=2048, name="a_out_dw")

    def gate_a_bwd(dy, o, z):
        do = dy * _silu(z)
        prod = do * o
        dsum = _heads_cat(lambda hs: jnp.broadcast_to(jnp.sum(prod[:, hs], axis=1, keepdims=True),
                                                      (prod.shape[0], A_HEAD_DIM)), A_HEADS, A_HEAD_DIM)
        return do, dy * o * _silu_grad(z), dsum

    do_a, dz_a, dsum_a = _rowwise(gate_a_bwd, [dy_a, o_a2, z_a], [],
                                  [(A_WIDTH, BF16), (A_WIDTH, BF16), (A_WIDTH, F32)], tm=tm, name="a_gate_bwd")
    do_a3, dsum_a3 = do_a.reshape(bsz, t, A_WIDTH), dsum_a.reshape(bsz, t, A_WIDTH)
    hn_a_t = hn_a.T
    dhn_a = _matmul(dz_a, w_z.T, out_dtype=F32, tm=mm, tn=D_MODEL, name="a_gate_dx")
    gw_parts = []
    for g, d in enumerate(A_DILATIONS):
        dqkv = _attn_a_bwd(qkv3, do_a3, lse_a, dsum_a3, tabs_a, g, d).reshape(n, 3 * A_WIDTH)
        w_g_t = w_qkv[:, g * 3 * A_WIDTH:(g + 1) * 3 * A_WIDTH].T
        dhn_a = _matmul(dqkv, w_g_t, out_dtype=F32, tm=mm, tn=D_MODEL, name=f"a_qkv_dx_g{g}",
                        epilogue=_add_epilogue, extras=[(dhn_a, (mm, D_MODEL), lambda j, i, kk: (i, j))])
        gw_parts.append(_matmul(hn_a_t, dqkv, out_dtype=F32, tm=mm, tn=1024, tk=2048, name=f"a_qkv_dw_g{g}"))
    gw_parts.append(_matmul(hn_a_t, dz_a, out_dtype=F32, tm=mm, tn=1024, tk=2048, name="a_gate_dw"))
    gw_a_in = jnp.concatenate(gw_parts, axis=1)

    def first_bwd(xt, dhn, dh1t, g):
        dx, dg = _rms_bwd(xt, g, dhn)
        return dh1t + dx, dg

    grad_x, dg_a_pre = _rowwise(first_bwd, [x2, dhn_a, dh1], [w["a_pre_norm"]], [(D_MODEL, F32)], [D_MODEL],
                                tm=tm, name="a_pre_norm_bwd")

    gw_down = jnp.concatenate([gw_down_p[:, :B_KV_LORA], gw_down_p[:, B_KV_LORA + B_NOPE:B_KV_LORA + B_QK_DIM]], axis=1)
    gw_up = jnp.concatenate([gw_upk.reshape(B_KV_LORA, B_HEADS, LANES)[:, :, :B_NOPE],
                             gw_upv.reshape(B_KV_LORA, B_HEADS, B_VDIM)], axis=2).reshape(B_KV_LORA, -1)
    gw_q_up = gw_q_p.reshape(B_Q_LORA, B_HEADS, LANES)[:, :, :B_QK_DIM].reshape(B_Q_LORA, -1)
    grads = {"a_w_in": gw_a_in, "a_w_out": gw_a_out, "kv_w_down": gw_down, "kv_w_up": gw_up,
             "b_w_in": gw_b_in, "b_w_q_up": gw_q_up, "b_w_out": gw_b_out}
    gains = {"a_pre_norm": dg_a_pre, "a_post_norm": dg_a_post, "kv_norm": dg_kv, "kv_latent_norm": dg_latent,
             "b_pre_norm": dg_b_pre, "b_q_norm": dg_q_norm, "b_post_norm": dg_b_post}
    gains = {k: jnp.sum(a, axis=0) for k, a in gains.items()}
    return jnp.sum(loss_acc), grad_x.reshape(bsz, t, D_MODEL), grads, gains


WEIGHT_ORDER = ("a_pre_norm", "a_w_in", "a_w_out", "a_post_norm", "kv_norm", "kv_w_down", "kv_latent_norm",
                "kv_w_up", "b_pre_norm", "b_w_in", "b_q_norm", "b_w_q_up", "b_w_out", "b_post_norm")
MATRICES = (("a_w_in", 1024, 10240, 1), ("a_w_out", 1024, 1024, 0), ("kv_w_down", 1024, 288, 0),
            ("kv_w_up", 256, 2048, 1), ("b_w_in", 1024, 1408, 1), ("b_w_q_up", 384, 1536, 1),
            ("b_w_out", 1024, 1024, 0))
SHARDED_GAINS = ("a_pre_norm", "a_post_norm")
GAIN_WIDTHS = (("a_pre_norm", 1024), ("a_post_norm", 1024), ("kv_norm", 1024), ("kv_latent_norm", 256),
               ("b_pre_norm", 1024), ("b_q_norm", 384), ("b_post_norm", 1024))
GAIN_ROWS = 48


def _shard_rows(rows, cols):
    return rows * cols // (N_DEV * LANES)


def _whole_from_blocks(blocks, rows, cols, axis):
    if axis == 1:
        return blocks.reshape(N_DEV, rows, cols // N_DEV).transpose(1, 0, 2).reshape(rows, cols)
    return blocks.reshape(rows, cols)


def _blocks_from_whole(whole, rows, cols, axis):
    if axis == 1:
        whole = whole.reshape(rows, N_DEV, cols // N_DEV).transpose(1, 0, 2)
    return whole.reshape(N_DEV, -1, LANES)


def kernel(x, positions, a_pre_norm, a_w_in, a_w_out, a_post_norm, kv_norm, kv_w_down, kv_latent_norm, kv_w_up, b_pre_norm, b_w_in, b_q_norm, b_w_q_up, b_w_out, b_post_norm, loss_target, m_a_pre_norm, m_a_w_in, m_a_w_out, m_a_post_norm, m_kv_norm, m_kv_w_down, m_kv_latent_norm, m_kv_w_up, m_b_pre_norm, m_b_w_in, m_b_q_norm, m_b_w_q_up, m_b_w_out, m_b_post_norm, v_a_pre_norm, v_a_w_in, v_a_w_out, v_a_post_norm, v_kv_norm, v_kv_w_down, v_kv_latent_norm, v_kv_w_up, v_b_pre_norm, v_b_w_in, v_b_q_norm, v_b_w_q_up, v_b_w_out, v_b_post_norm):
    weights = dict(a_pre_norm=a_pre_norm, a_w_in=a_w_in, a_w_out=a_w_out, a_post_norm=a_post_norm, kv_norm=kv_norm,
                   kv_w_down=kv_w_down, kv_latent_norm=kv_latent_norm, kv_w_up=kv_w_up, b_pre_norm=b_pre_norm,
                   b_w_in=b_w_in, b_q_norm=b_q_norm, b_w_q_up=b_w_q_up, b_w_out=b_w_out, b_post_norm=b_post_norm)
    m_in = dict(a_pre_norm=m_a_pre_norm, a_w_in=m_a_w_in, a_w_out=m_a_w_out, a_post_norm=m_a_post_norm,
                kv_norm=m_kv_norm, kv_w_down=m_kv_w_down, kv_latent_norm=m_kv_latent_norm, kv_w_up=m_kv_w_up,
                b_pre_norm=m_b_pre_norm, b_w_in=m_b_w_in, b_q_norm=m_b_q_norm, b_w_q_up=m_b_w_q_up,
                b_w_out=m_b_w_out, b_post_norm=m_b_post_norm)
    v_in = dict(a_pre_norm=v_a_pre_norm, a_w_in=v_a_w_in, a_w_out=v_a_w_out, a_post_norm=v_a_post_norm,
                kv_norm=v_kv_norm, kv_w_down=v_kv_w_down, kv_latent_norm=v_kv_latent_norm, kv_w_up=v_kv_w_up,
                b_pre_norm=v_b_pre_norm, b_w_in=v_b_w_in, b_q_norm=v_b_q_norm, b_w_q_up=v_b_w_q_up,
                b_w_out=v_b_w_out, b_post_norm=v_b_post_norm)
    me = 4 * lax.axis_index("x") + 2 * lax.axis_index("y") + lax.axis_index("c")

    flat = jnp.concatenate([weights[name].astype(BF16).reshape(-1, LANES) for name, _, _, _ in MATRICES], axis=0)
    gathered = _all_gather(flat, "gather_weights")
    whole = {}
    off = 0
    for name, rows, cols, axis in MATRICES:
        nr = _shard_rows(rows, cols)
        whole[name] = _whole_from_blocks(gathered[:, off:off + nr], rows, cols, axis)
        off += nr
    gain_shard = jnp.concatenate([weights[name].reshape(1, LANES) for name in SHARDED_GAINS]
                                 + [jnp.zeros((8 - len(SHARDED_GAINS), LANES), F32)], axis=0)
    gain_blocks = _all_gather(gain_shard, "gather_gains")
    for i, name in enumerate(SHARDED_GAINS):
        whole[name] = gain_blocks[:, i, :].reshape(1, D_MODEL)
    for name in ("kv_norm", "kv_latent_norm", "b_pre_norm", "b_q_norm", "b_post_norm"):
        whole[name] = weights[name].reshape(1, -1)

    loss_part, grad_x, grads, gains = _local_step(x, positions, loss_target, whole)

    blocks = jnp.concatenate([_blocks_from_whole(grads[name], rows, cols, axis).astype(BF16)
                              for name, rows, cols, axis in MATRICES], axis=1)
    landed = _exchange_blocks(blocks, "scatter_grads")
    summed = _sum_slots(landed, tr=2512, name="sum_grads")
    grad_out = {}
    off = 0
    for name, rows, cols, axis in MATRICES:
        nr = _shard_rows(rows, cols)
        grad_out[name] = summed[off:off + nr].reshape(weights[name].shape)
        off += nr

    vec = jnp.concatenate([gains[name] for name, _ in GAIN_WIDTHS] + [jnp.full((LANES,), loss_part, F32)])
    vec = jnp.pad(vec, (0, GAIN_ROWS * LANES - vec.shape[0])).reshape(GAIN_ROWS, LANES)
    total = _sum_slots(_all_gather(vec, "gather_gain_grads"), tr=GAIN_ROWS, name="sum_gain_grads").reshape(-1)
    off = 0
    for name, width in GAIN_WIDTHS:
        g = total[off:off + width]
        if name in SHARDED_GAINS:
            g = lax.dynamic_slice(g, (me * LANES,), (LANES,))
        grad_out[name] = g.reshape(weights[name].shape)
        off += width
    loss = total[off]

    deltas, new_m, new_v = {}, {}, {}
    for name in WEIGHT_ORDER:
        deltas[name], new_m[name], new_v[name] = _adamw(weights[name], grad_out[name], m_in[name], v_in[name],
                                                        "adamw_" + name)
    return (loss, grad_x, *[grad_out[k] for k in WEIGHT_ORDER], *[deltas[k] for k in WEIGHT_ORDER],
            *[new_m[k] for k in WEIGHT_ORDER], *[new_v[k] for k in WEIGHT_ORDER])
```

```python
import jax
import jax.numpy as jnp
from jax import lax
from jax.experimental import pallas as pl
from jax.experimental.pallas import tpu as pltpu

F32 = jnp.float32
BF16 = jnp.bfloat16

N_DEV = 8
D_MODEL = 1024
NORM_EPS = 1e-6
A_GROUPS = 3
A_DILATIONS = (1, 4, 16)
A_HEADS = 8
A_HEAD_DIM = 128
A_WIDTH = 1024
A_ROT_DIM = 32
A_ROPE_THETA = 500000.0
A_QKV = A_GROUPS * 3 * A_WIDTH
B_HEADS = 16
B_NOPE = 64
B_ROPE = 32
B_QK_DIM = 96
B_VDIM = 64
B_Q_LORA = 384
B_KV_LORA = 256
B_ROPE_THETA = 10000.0
B_KPAD = B_HEADS * 128
ADAM_LR = 0.001
ADAM_B1 = 0.9
ADAM_B2 = 0.999
ADAM_EPS = 1e-08
ADAM_WD = 0.01
ADAM_STEP = 10

LANES = 128
BAND = 128
NEG = -1e30
VMEM_LIMIT = 56 * 1024 * 1024
MESH = pl.DeviceIdType.MESH


def _cparams(sem):
    return pltpu.CompilerParams(dimension_semantics=sem, vmem_limit_bytes=VMEM_LIMIT)


def _rowwise(fn, rows, bcast, outs, accs=(), *, tm, name):
    n = rows[0].shape[0]
    nr, nb, no = len(rows), len(bcast), len(outs)

    def body(*refs):
        res = fn(*[r[...] for r in refs[:nr + nb]])
        out_refs = refs[nr + nb:nr + nb + no]
        acc_refs = refs[nr + nb + no:]
        for r, v in zip(out_refs, res[:no]):
            r[...] = v.astype(r.dtype)
        if acc_refs:
            @pl.when(pl.program_id(0) == 0)
            def _():
                for r in acc_refs:
                    r[...] = jnp.zeros_like(r)
            for r, v in zip(acc_refs, res[no:]):
                r[...] += v.reshape(tm // 8, 8, v.shape[-1]).sum(axis=0)

    in_specs = [pl.BlockSpec((tm, a.shape[1]), lambda i: (i, 0)) for a in rows]
    in_specs += [pl.BlockSpec(a.shape, lambda i: (0, 0)) for a in bcast]
    out_specs = [pl.BlockSpec((tm, c), lambda i: (i, 0)) for c, _ in outs]
    out_specs += [pl.BlockSpec((8, c), lambda i: (0, 0)) for c in accs]
    out_shape = [jax.ShapeDtypeStruct((n, c), dt) for c, dt in outs]
    out_shape += [jax.ShapeDtypeStruct((8, c), F32) for c in accs]
    return pl.pallas_call(
        body, name=name, grid=(n // tm,), in_specs=in_specs, out_specs=out_specs, out_shape=out_shape,
        compiler_params=_cparams(("arbitrary",)))(*rows, *bcast)


def _matmul(a, b, *, out_dtype, tm, tn, tk=None, name, epilogue=None, extras=()):
    m, k = a.shape
    n = b.shape[1]
    tk = tk or k
    nk = k // tk
    ne = len(extras)

    def body(*refs):
        a_ref, b_ref = refs[:2]
        ex = refs[2:2 + ne]
        o_ref = refs[2 + ne]
        part = _dot(a_ref[...].astype(BF16), b_ref[...].astype(BF16))

        def finish(acc):
            if epilogue is None:
                o_ref[...] = acc.astype(o_ref.dtype)
            else:
                epilogue(acc, o_ref, pl.program_id(0), *[e[...] for e in ex])

        if nk == 1:
            finish(part)
        else:
            acc_ref = refs[-1]
            kk = pl.program_id(2)

            @pl.when(kk == 0)
            def _():
                acc_ref[...] = part

            @pl.when(kk > 0)
            def _():
                acc_ref[...] += part

            @pl.when(kk == nk - 1)
            def _():
                finish(acc_ref[...])

    in_specs = [pl.BlockSpec((tm, tk), lambda j, i, kk: (i, kk)),
                pl.BlockSpec((tk, tn), lambda j, i, kk: (kk, j))]
    in_specs += [pl.BlockSpec(bs, im) for _, bs, im in extras]
    return pl.pallas_call(
        body, name=name, grid=(n // tn, m // tm, nk), in_specs=in_specs,
        out_specs=pl.BlockSpec((tm, tn), lambda j, i, kk: (i, j)),
        out_shape=jax.ShapeDtypeStruct((m, n), out_dtype),
        scratch_shapes=[pltpu.VMEM((tm, tn), F32)] if nk > 1 else [],
        compiler_params=_cparams(("parallel", "parallel", "arbitrary")))(a, b, *[e[0] for e in extras])


def _add_epilogue(acc, o_ref, j, prev):
    o_ref[...] = (acc + prev).astype(o_ref.dtype)


def _rope(x, c, sp, sm):
    return x * c + pltpu.roll(x, 16, 1) * sp + pltpu.roll(x, LANES - 16, 1) * sm


def _rope_t(dy, c, sp, sm):
    return dy * c + pltpu.roll(dy * sp, LANES - 16, 1) + pltpu.roll(dy * sm, 16, 1)


def _rope_tables(positions, theta, rot_dim, lane0):
    half = rot_dim // 2
    inv_freq = 1.0 / (theta ** (jnp.arange(half, dtype=F32) * (2.0 / rot_dim)))
    ang = positions.astype(F32)[..., None] * inv_freq
    cos, sin = jnp.cos(ang), jnp.sin(ang)
    shape = positions.shape + (LANES,)
    c = jnp.ones(shape, F32).at[..., lane0:lane0 + half].set(cos).at[..., lane0 + half:lane0 + rot_dim].set(cos)
    sp = jnp.zeros(shape, F32).at[..., lane0 + half:lane0 + rot_dim].set(sin)
    sm = jnp.zeros(shape, F32).at[..., lane0:lane0 + half].set(-sin)
    return c, sp, sm


def _rms(x, g):
    xf = x.astype(F32)
    return xf * lax.rsqrt(jnp.mean(xf * xf, axis=-1, keepdims=True) + NORM_EPS) * g


def _rms_bwd(x, g, dy):
    xf = x.astype(F32)
    rstd = lax.rsqrt(jnp.mean(xf * xf, axis=-1, keepdims=True) + NORM_EPS)
    xhat = xf * rstd
    dxhat = dy * g
    dx = rstd * (dxhat - xhat * jnp.mean(dxhat * xhat, axis=-1, keepdims=True))
    return dx, dy * xhat


def _silu(z):
    return z * jax.nn.sigmoid(z)


def _silu_grad(z):
    s = jax.nn.sigmoid(z)
    return s * (1.0 + z * (1.0 - s))


def _dot_nt(a, b):
    return lax.dot_general(a, b, (((1,), (1,)), ((), ())), preferred_element_type=F32)


def _dot_tn(a, b):
    return lax.dot_general(a, b, (((0,), (0,)), ((), ())), preferred_element_type=F32)


def _dot(a, b):
    return jnp.dot(a, b, preferred_element_type=F32)


A_SCALE = A_HEAD_DIM ** -0.5
LOG2E = 1.4426950408889634
A_C2 = A_SCALE * LOG2E
A_HEAD_GROUP = 4


def _attn_a_fwd(qkv, name):
    s_n, ln, _ = qkv.shape
    nb = ln // BAND
    blk = (None, BAND, A_WIDTH)

    def body(q_ref, kc_ref, kp_ref, vc_ref, vp_ref, o_ref, lse_ref):
        kpos = lax.broadcasted_iota(jnp.int32, (2 * BAND, BAND), 0)
        qpos = lax.broadcasted_iota(jnp.int32, (2 * BAND, BAND), 1) + BAND
        first_key = jnp.where(pl.program_id(1) > 0, 0, BAND)
        mask = (kpos <= qpos) & (kpos >= qpos - BAND) & (kpos >= first_key)
        rows = []
        for h0 in range(0, A_HEADS, A_HEAD_GROUP):
            hss = [slice(h * A_HEAD_DIM, (h + 1) * A_HEAD_DIM) for h in range(h0, h0 + A_HEAD_GROUP)]
            sts = [_dot_nt(jnp.concatenate([kp_ref[:, hs], kc_ref[:, hs]], axis=0), q_ref[:, hs]) for hs in hss]
            ps, ls = [], []
            for st in sts:
                st = jnp.where(mask, st * A_C2, NEG)
                m = jnp.max(st, axis=0, keepdims=True)
                p = jnp.exp2(st - m)
                l_row = jnp.sum(p, axis=0, keepdims=True)
                ps.append(p.astype(BF16))
                ls.append(l_row)
                rows.append(m + jnp.log2(l_row))
            ots = [_dot_tn(jnp.concatenate([vp_ref[:, hs], vc_ref[:, hs]], axis=0), p) for hs, p in zip(hss, ps)]
            for hs, o_t, l_row in zip(hss, ots, ls):
                o_ref[:, hs] = (o_t / l_row).T.astype(BF16)
        lse_ref[...] = jnp.concatenate(rows, axis=0)

    def col(c, off):
        return lambda s, l: (s, jnp.maximum(l + off, 0), c)

    return pl.pallas_call(
        body, name=name, grid=(s_n, nb),
        in_specs=[pl.BlockSpec(blk, col(0, 0)), pl.BlockSpec(blk, col(1, 0)), pl.BlockSpec(blk, col(1, -1)),
                  pl.BlockSpec(blk, col(2, 0)), pl.BlockSpec(blk, col(2, -1))],
        out_specs=[pl.BlockSpec(blk, lambda s, l: (s, l, 0)),
                   pl.BlockSpec((None, A_HEADS, BAND), lambda s, l: (s, 0, l))],
        out_shape=[jax.ShapeDtypeStruct((s_n, ln, A_WIDTH), BF16), jax.ShapeDtypeStruct((s_n, A_HEADS, ln), F32)],
        compiler_params=_cparams(("parallel", "arbitrary")))(qkv, qkv, qkv, qkv, qkv)


def _attn_a_bwd(qkv, do, lse2, dsum, tabs, name):
    s_n, ln, _ = qkv.shape
    nb = ln // BAND
    blk = (None, BAND, A_WIDTH)

    def body(q_ref, qn_ref, kc_ref, kp_ref, vc_ref, vp_ref, do_ref, don_ref, lse_ref, lsen_ref, ds_ref, dsn_ref,
             c_ref, sp_ref, sm_ref, out_ref):
        l_idx = pl.program_id(1)
        kpos = lax.broadcasted_iota(jnp.int32, (2 * BAND, BAND), 0)
        qpos = lax.broadcasted_iota(jnp.int32, (2 * BAND, BAND), 1) + BAND
        first_key = jnp.where(l_idx > 0, 0, BAND)
        mask_q = (kpos <= qpos) & (kpos >= qpos - BAND) & (kpos >= first_key)
        kpos2 = lax.broadcasted_iota(jnp.int32, (BAND, 2 * BAND), 0)
        qpos2 = lax.broadcasted_iota(jnp.int32, (BAND, 2 * BAND), 1)
        last_query = jnp.where(l_idx < nb - 1, 2 * BAND, BAND)
        mask_k = (kpos2 <= qpos2) & (kpos2 >= qpos2 - BAND) & (qpos2 < last_query)
        c, sp, sm = c_ref[...], sp_ref[...], sm_ref[...]
        lse_q, ds_q = lse_ref[...], ds_ref[...]
        lse_k = jnp.concatenate([lse_q, lsen_ref[...]], axis=1)
        ds_k = jnp.concatenate([ds_q, dsn_ref[...]], axis=1)
        for h in range(A_HEADS):
            hs = slice(h * A_HEAD_DIM, (h + 1) * A_HEAD_DIM)
            q, do_b, kc, vc = q_ref[:, hs], do_ref[:, hs], kc_ref[:, hs], vc_ref[:, hs]
            k2 = jnp.concatenate([kp_ref[:, hs], kc], axis=0)
            v2 = jnp.concatenate([vp_ref[:, hs], vc], axis=0)
            q2 = jnp.concatenate([q, qn_ref[:, hs]], axis=0)
            do2 = jnp.concatenate([do_b, don_ref[:, hs]], axis=0)
            st = _dot_nt(k2, q)
            dpt = _dot_nt(v2, do_b)
            st2 = _dot_nt(kc, q2)
            dpt2 = _dot_nt(vc, do2)
            p = jnp.exp2(jnp.where(mask_q, st * A_C2, NEG) - lse_q[h:h + 1])
            dst = (p * (dpt - ds_q[h:h + 1]) * A_SCALE).astype(BF16)
            dq = _dot_tn(dst, k2)
            p2 = jnp.exp2(jnp.where(mask_k, st2 * A_C2, NEG) - lse_k[h:h + 1])
            dst2 = (p2 * (dpt2 - ds_k[h:h + 1]) * A_SCALE).astype(BF16)
            dk = _dot(dst2, q2)
            dv = _dot(p2.astype(BF16), do2)
            out_ref[:, hs] = _rope_t(dq, c, sp, sm).astype(BF16)
            out_ref[:, A_WIDTH + h * A_HEAD_DIM:A_WIDTH + (h + 1) * A_HEAD_DIM] = _rope_t(dk, c, sp, sm).astype(BF16)
            out_ref[:, 2 * A_WIDTH + h * A_HEAD_DIM:2 * A_WIDTH + (h + 1) * A_HEAD_DIM] = dv.astype(BF16)

    def col(c, off):
        return lambda s, l: (s, jnp.clip(l + off, 0, nb - 1), c)

    def row(off):
        return pl.BlockSpec((None, A_HEADS, BAND), lambda s, l: (s, 0, jnp.clip(l + off, 0, nb - 1)))

    tspec = pl.BlockSpec((None, BAND, LANES), lambda s, l: (s, l, 0))
    in_specs = [pl.BlockSpec(blk, col(0, 0)), pl.BlockSpec(blk, col(0, 1)),
                pl.BlockSpec(blk, col(1, 0)), pl.BlockSpec(blk, col(1, -1)),
                pl.BlockSpec(blk, col(2, 0)), pl.BlockSpec(blk, col(2, -1)),
                pl.BlockSpec(blk, col(0, 0)), pl.BlockSpec(blk, col(0, 1)),
                row(0), row(1), row(0), row(1), tspec, tspec, tspec]
    return pl.pallas_call(
        body, name=name, grid=(s_n, nb), in_specs=in_specs,
        out_specs=pl.BlockSpec((None, BAND, 3 * A_WIDTH), lambda s, l: (s, l, 0)),
        out_shape=jax.ShapeDtypeStruct((s_n, ln, 3 * A_WIDTH), BF16),
        compiler_params=_cparams(("parallel", "arbitrary")))(
            qkv, qkv, qkv, qkv, qkv, qkv, do, do, lse2, lse2, dsum, dsum, *tabs)


B_TQ = 256
B_SCALE = B_QK_DIM ** -0.5
B_C2 = B_SCALE * LOG2E


def _b_row_tile(tq):
    return pl.BlockSpec((None, None, 8, tq), lambda b, j, i: (b, j, 0, i))


def _b_specs(t, tq, tk):
    pair_tile = pl.BlockSpec((None, tq, 2 * LANES), lambda b, j, i: (b, i, j))
    pair_full = pl.BlockSpec((None, t, 2 * LANES), lambda b, j, i: (b, 0, j))
    one_tile = pl.BlockSpec((None, tk, LANES), lambda b, j, i: (b, i, j))
    one_full = pl.BlockSpec((None, t, LANES), lambda b, j, i: (b, 0, j))
    tab_tile = pl.BlockSpec((None, tq, LANES), lambda b, j, i: (b, i, 0))
    return pair_tile, pair_full, one_tile, one_full, tab_tile


def _key_le_query(kb, qb, tk, tq):
    kpos = kb * tk + lax.broadcasted_iota(jnp.int32, (tk, tq), 0)
    qpos = qb * tq + lax.broadcasted_iota(jnp.int32, (tk, tq), 1)
    return kpos <= qpos


def _attn_b_fwd(q, kpad, vt1):
    bsz, t, _ = q.shape
    tq = tk = B_TQ
    nq = t // tq

    def body(q_ref, k_ref, vt_ref, o_ref, lse_ref):
        qblk = pl.program_id(2)
        qs = [q_ref[:, hh * LANES:(hh + 1) * LANES] for hh in range(2)]

        def make_step(masked):
            def step(kb, carry):
                start = pl.multiple_of(kb * tk, tk)
                ss = [_dot_nt(k_ref[pl.ds(start, tk), hh * LANES:(hh + 1) * LANES], qs[hh]) for hh in range(2)]
                ms, als, ps = [], [], []
                for hh in range(2):
                    s = ss[hh] * B_C2
                    if masked:
                        s = jnp.where(_key_le_query(kb, qblk, tk, tq), s, NEG)
                    m_new = jnp.maximum(carry[2 * hh], jnp.max(s, axis=0, keepdims=True))
                    als.append(jnp.exp2(carry[2 * hh] - m_new))
                    ps.append(jnp.exp2(s - m_new).astype(BF16))
                    ms.append(m_new)
                pv = [_dot(vt_ref[hh * LANES:(hh + 1) * LANES, pl.ds(start, tk)], ps[hh]) for hh in range(2)]
                return (ms[0], als[0] * carry[1] + pv[0], ms[1], als[1] * carry[3] + pv[1])
            return step

        init = (jnp.full((1, tq), NEG, F32), jnp.zeros((LANES, tq), F32)) * 2
        carry = lax.fori_loop(0, qblk, make_step(False), init)
        carry = make_step(True)(qblk, carry)
        ls = [carry[1][B_VDIM:B_VDIM + 1], carry[3][B_VDIM:B_VDIM + 1]]
        o_t = jnp.concatenate([carry[1][:B_VDIM] / ls[0], carry[3][:B_VDIM] / ls[1]], axis=0)
        o_ref[...] = o_t.T
        lse_ref[...] = jnp.concatenate([carry[0] + jnp.log2(ls[0]), carry[2] + jnp.log2(ls[1]),
                                        jnp.zeros((6, tq), F32)], axis=0)

    pair_tile, pair_full, one_tile, _, _ = _b_specs(t, tq, tk)
    vt_spec = pl.BlockSpec((None, 2 * LANES, t), lambda b, j, i: (b, j, 0))
    return pl.pallas_call(
        body, name="attn_b_fwd", grid=(bsz, B_HEADS // 2, nq),
        in_specs=[pair_tile, pair_full, vt_spec],
        out_specs=[one_tile, _b_row_tile(tq)],
        out_shape=[jax.ShapeDtypeStruct((bsz, t, B_HEADS * B_VDIM), F32),
                   jax.ShapeDtypeStruct((bsz, B_HEADS // 2, 8, t), F32)],
        compiler_params=_cparams(("parallel", "parallel", "arbitrary")))(q, kpad, vt1)


def _attn_b_dq(q, kpad, v, do, o, lse2, tabs):
    bsz, t, _ = q.shape
    tq = tk = B_TQ
    nq = t // tq

    def body(q_ref, k_ref, v_ref, do_ref, o_ref, lse_ref, c_ref, sp_ref, sm_ref, dq_ref, dsum_ref):
        qblk = pl.program_id(2)
        lane = lax.broadcasted_iota(jnp.int32, (tq, LANES), 1)
        do_b = do_ref[...]
        prod = do_b.astype(F32) * o_ref[...]
        hi = prod.astype(BF16)
        lo = (prod - hi.astype(F32)).astype(BF16)
        sel_lane = lax.broadcasted_iota(jnp.int32, (8, LANES), 1)
        sel_row = lax.broadcasted_iota(jnp.int32, (8, LANES), 0)
        sel = jnp.where((sel_lane < B_VDIM) == (sel_row == 0), 1.0, 0.0)
        sel = jnp.where(sel_row < 2, sel, 0.0).astype(BF16)
        dsum = _dot_nt(sel, hi) + _dot_nt(sel, lo)
        dsum_ref[...] = dsum
        lse = lse_ref[...]
        qs = [q_ref[:, hh * LANES:(hh + 1) * LANES] for hh in range(2)]
        zero = jnp.zeros_like(do_b)
        dos = [jnp.where(lane < B_VDIM, do_b, zero), jnp.where(lane < B_VDIM, zero, do_b)]

        def make_step(masked):
            def step(kb, carry):
                start = pl.multiple_of(kb * tk, tk)
                ks = [k_ref[pl.ds(start, tk), hh * LANES:(hh + 1) * LANES] for hh in range(2)]
                vb = v_ref[pl.ds(start, tk), :]
                ss = [_dot_nt(ks[hh], qs[hh]) for hh in range(2)]
                dps = [_dot_nt(vb, dos[hh]) for hh in range(2)]
                dss = []
                for hh in range(2):
                    s = ss[hh] * B_C2
                    if masked:
                        s = jnp.where(_key_le_query(kb, qblk, tk, tq), s, NEG)
                    p = jnp.exp2(s - lse[hh:hh + 1])
                    dss.append((p * (dps[hh] - dsum[hh:hh + 1]) * B_SCALE).astype(BF16))
                return tuple(carry[hh] + _dot_tn(ks[hh], dss[hh]) for hh in range(2))
            return step

        init = (jnp.zeros((LANES, tq), F32),) * 2
        carry = lax.fori_loop(0, qblk, make_step(False), init)
        carry = make_step(True)(qblk, carry)
        c, sp, sm = c_ref[...], sp_ref[...], sm_ref[...]
        for hh in range(2):
            dq_ref[:, hh * LANES:(hh + 1) * LANES] = _rope_t(carry[hh].T, c, sp, sm).astype(BF16)

    pair_tile, pair_full, one_tile, one_full, tab_tile = _b_specs(t, tq, tk)
    row_tile = _b_row_tile(tq)
    return pl.pallas_call(
        body, name="attn_b_dq", grid=(bsz, B_HEADS // 2, nq),
        in_specs=[pair_tile, pair_full, one_full, one_tile, one_tile, row_tile, tab_tile, tab_tile, tab_tile],
        out_specs=[pair_tile, row_tile],
        out_shape=[jax.ShapeDtypeStruct((bsz, t, B_KPAD), BF16),
                   jax.ShapeDtypeStruct((bsz, B_HEADS // 2, 8, t), F32)],
        compiler_params=_cparams(("parallel", "parallel", "arbitrary")))(q, kpad, v, do, o, lse2, *tabs)


def _attn_b_dkv(q, kpad, v, do, lse2, dsum):
    bsz, t, _ = q.shape
    tq = tk = B_TQ
    nq = t // tq

    def body(q_ref, k_ref, v_ref, do_ref, lse_ref, dsum_ref, dk_ref, dv_ref):
        kblk = pl.program_id(2)
        lane = lax.broadcasted_iota(jnp.int32, (tk, LANES), 1)
        ks = [k_ref[:, hh * LANES:(hh + 1) * LANES] for hh in range(2)]
        vb = v_ref[...]
        zero = jnp.zeros_like(vb)
        vs = [jnp.where(lane < B_VDIM, vb, zero), jnp.where(lane < B_VDIM, zero, vb)]

        def make_step(masked):
            def step(qb, carry):
                start = pl.multiple_of(qb * tq, tq)
                qs = [q_ref[pl.ds(start, tq), hh * LANES:(hh + 1) * LANES] for hh in range(2)]
                do_b = do_ref[pl.ds(start, tq), :]
                ss = [_dot_nt(ks[hh], qs[hh]) for hh in range(2)]
                dps = [_dot_nt(vs[hh], do_b) for hh in range(2)]
                pbs, dss = [], []
                for hh in range(2):
                    s = ss[hh] * B_C2
                    if masked:
                        s = jnp.where(_key_le_query(kblk, qb, tk, tq), s, NEG)
                    p = jnp.exp2(s - lse_ref[hh:hh + 1, pl.ds(start, tq)])
                    dss.append((p * (dps[hh] - dsum_ref[hh:hh + 1, pl.ds(start, tq)]) * B_SCALE).astype(BF16))
                    pbs.append(p.astype(BF16))
                return (carry[0] + _dot(dss[0], qs[0]), carry[1] + _dot(dss[1], qs[1]),
                        carry[2] + _dot(pbs[0], do_b), carry[3] + _dot(pbs[1], do_b))
            return step

        init = (jnp.zeros((tk, LANES), F32),) * 4
        carry = make_step(True)(kblk, init)
        carry = lax.fori_loop(kblk + 1, nq, make_step(False), carry)
        dk_ref[:, :LANES] = carry[0]
        dk_ref[:, LANES:] = carry[1]
        dv_ref[...] = jnp.where(lane < B_VDIM, carry[2], carry[3])

    pair_tile, pair_full, one_tile, one_full, _ = _b_specs(t, tq, tk)
    row_full = pl.BlockSpec((None, None, 8, t), lambda b, j, i: (b, j, 0, 0))
    return pl.pallas_call(
        body, name="attn_b_dkv", grid=(bsz, B_HEADS // 2, nq),
        in_specs=[pair_full, pair_tile, one_tile, one_full, row_full, row_full],
        out_specs=[pair_tile, one_tile],
        out_shape=[jax.ShapeDtypeStruct((bsz, t, B_KPAD), F32),
                   jax.ShapeDtypeStruct((bsz, t, B_HEADS * B_VDIM), F32)],
        compiler_params=_cparams(("parallel", "parallel", "arbitrary")))(q, kpad, v, do, lse2, dsum)


ANY = pl.BlockSpec(memory_space=pl.ANY)


def _all_gather(shard, name):
    def body(x_ref, out_ref, send_sems, recv_sems, local_sem):
        x, y, c = lax.axis_index("x"), lax.axis_index("y"), lax.axis_index("c")
        me, sibling = (x, y, c), (x, y, 1 - c)
        chips = [(1 - x, y), (x, 1 - y), (1 - x, 1 - y)]

        def rows(px, py, pc):
            return out_ref.at[4 * px + 2 * py + pc]

        def copy(k, block, to, src=None):
            return pltpu.make_async_remote_copy(
                src_ref=rows(*block) if src is None else src, dst_ref=rows(*block),
                send_sem=send_sems.at[k], recv_sem=recv_sems.at[k], device_id=to, device_id_type=MESH)

        mine = pltpu.make_async_copy(x_ref, rows(*me), local_sem)
        mine.start()
        first = [copy(0, me, sibling, src=x_ref)]
        first += [copy(1 + j, me, (*chip, c), src=x_ref) for j, chip in enumerate(chips)]
        for cp in first:
            cp.start()
        passed = [copy(4 + j, (*chip, c), sibling) for j, chip in enumerate(chips)]
        for j, chip in enumerate(chips):
            copy(1 + j, (*chip, c), me).wait_recv()
            passed[j].start()
        copy(0, sibling, me).wait_recv()
        for j, chip in enumerate(chips):
            copy(4 + j, (*chip, 1 - c), me).wait_recv()
        for cp in first + passed:
            cp.wait_send()
        mine.wait()

    return pl.pallas_call(
        body, name=name, in_specs=[ANY], out_specs=ANY,
        out_shape=jax.ShapeDtypeStruct((N_DEV,) + shard.shape, shard.dtype),
        scratch_shapes=[pltpu.SemaphoreType.DMA((7,)), pltpu.SemaphoreType.DMA((7,)), pltpu.SemaphoreType.DMA])(shard)


def _exchange_blocks(blocks, name):
    def body(g_ref, out_ref, send_sems, recv_sems, local_sem):
        x, y, c = lax.axis_index("x"), lax.axis_index("y"), lax.axis_index("c")
        me = 4 * x + 2 * y + c

        def peer(k):
            return (1 - x if k & 4 else x, 1 - y if k & 2 else y, 1 - c if k & 1 else c)

        def copy(k):
            px, py, pc = peer(k)
            return pltpu.make_async_remote_copy(
                src_ref=g_ref.at[4 * px + 2 * py + pc], dst_ref=out_ref.at[me],
                send_sem=send_sems.at[k - 1], recv_sem=recv_sems.at[k - 1], device_id=(px, py, pc), device_id_type=MESH)

        def arrival(k):
            px, py, pc = peer(k)
            slot = out_ref.at[4 * px + 2 * py + pc]
            return pltpu.make_async_remote_copy(
                src_ref=slot, dst_ref=slot, send_sem=send_sems.at[k - 1], recv_sem=recv_sems.at[k - 1],
                device_id=(px, py, pc), device_id_type=MESH)

        mine = pltpu.make_async_copy(g_ref.at[me], out_ref.at[me], local_sem)
        mine.start()
        sends = [copy(k) for k in range(1, N_DEV)]
        for cp in sends:
            cp.start()
        for k in range(1, N_DEV):
            arrival(k).wait_recv()
        for cp in sends:
            cp.wait_send()
        mine.wait()

    return pl.pallas_call(
        body, name=name, in_specs=[ANY], out_specs=ANY,
        out_shape=jax.ShapeDtypeStruct(blocks.shape, blocks.dtype),
        scratch_shapes=[pltpu.SemaphoreType.DMA((7,)), pltpu.SemaphoreType.DMA((7,)), pltpu.SemaphoreType.DMA])(blocks)


def _sum_slots(slots, *, tr, name):
    _, r, c = slots.shape

    def body(s_ref, o_ref):
        acc = s_ref[0].astype(F32)
        for s in range(1, N_DEV):
            acc = acc + s_ref[s].astype(F32)
        o_ref[...] = acc

    return pl.pallas_call(
        body, name=name, grid=(r // tr,),
        in_specs=[pl.BlockSpec((N_DEV, tr, c), lambda i: (0, i, 0))],
        out_specs=pl.BlockSpec((tr, c), lambda i: (i, 0)),
        out_shape=jax.ShapeDtypeStruct((r, c), F32),
        compiler_params=_cparams(("parallel",)))(slots)


def _adamw(w, g, m, v, name):
    shape = w.shape
    cols = shape[-1]
    args = [a.reshape(-1, cols) for a in (w, g, m, v)]
    rows = args[0].shape[0]
    tm = 256 if rows % 256 == 0 else rows

    def fn(w_t, g_t, m_t, v_t):
        m_n = ADAM_B1 * m_t + (1.0 - ADAM_B1) * g_t
        v_n = ADAM_B2 * v_t + (1.0 - ADAM_B2) * (g_t * g_t)
        m_hat = m_n / (1.0 - ADAM_B1 ** ADAM_STEP)
        v_hat = v_n / (1.0 - ADAM_B2 ** ADAM_STEP)
        delta = -ADAM_LR * (m_hat / (jnp.sqrt(v_hat) + ADAM_EPS) + ADAM_WD * w_t)
        return delta, m_n, v_n

    delta, m_n, v_n = _rowwise(fn, args, [], [(cols, F32)] * 3, tm=tm, name=name)
    return delta.reshape(shape), m_n.reshape(shape), v_n.reshape(shape)


def _local_step(x, positions, target, w):
    bsz, t, _ = x.shape
    n = bsz * t
    tm = 256
    mm = 512
    x2 = x.reshape(n, D_MODEL)
    tgt2 = target.reshape(n, D_MODEL)
    tabs_a = _rope_tables(positions, A_ROPE_THETA, A_ROT_DIM, 0)
    tabs_b = _rope_tables(positions, B_ROPE_THETA, B_ROPE, B_NOPE)
    ta2 = [a.reshape(n, LANES) for a in tabs_a]
    tb2 = [a.reshape(n, LANES) for a in tabs_b]

    w_a_in = w["a_w_in"]
    w_qkv, w_z = w_a_in[:, :A_QKV], w_a_in[:, A_QKV:]
    w_a_out = w["a_w_out"]
    w_down = w["kv_w_down"]
    w_down_p = jnp.zeros((D_MODEL, 3 * LANES), BF16).at[:, :B_KV_LORA].set(w_down[:, :B_KV_LORA])
    w_down_p = w_down_p.at[:, B_KV_LORA + B_NOPE:B_KV_LORA + B_QK_DIM].set(w_down[:, B_KV_LORA:])
    wu = w["kv_w_up"].reshape(B_KV_LORA, B_HEADS, B_NOPE + B_VDIM)
    w_upk = jnp.pad(wu[:, :, :B_NOPE], ((0, 0), (0, 0), (0, LANES - B_NOPE))).reshape(B_KV_LORA, B_KPAD)
    w_upv = wu[:, :, B_NOPE:].reshape(B_KV_LORA, B_HEADS * B_VDIM)
    w_b_in = w["b_w_in"]
    w_q_p = jnp.pad(w["b_w_q_up"].reshape(B_Q_LORA, B_HEADS, B_QK_DIM),
                    ((0, 0), (0, 0), (0, LANES - B_QK_DIM))).reshape(B_Q_LORA, B_KPAD)
    w_b_out = w["b_w_out"]

    def tab_extras(tabs2, rows):
        return [(a, (rows, LANES), lambda j, i, kk: (i, 0)) for a in tabs2]

    (hn_a,) = _rowwise(lambda xt, g: (_rms(xt, g),), [x2], [w["a_pre_norm"]], [(D_MODEL, BF16)],
                       tm=tm, name="a_pre_norm")

    def qkv_epilogue(acc, o_ref, j, c, sp, sm):
        @pl.when(j == 2)
        def _():
            o_ref[...] = acc.astype(BF16)

        @pl.when(j < 2)
        def _():
            for h in range(A_HEADS):
                hs = slice(h * A_HEAD_DIM, (h + 1) * A_HEAD_DIM)
                o_ref[:, hs] = _rope(acc[:, hs], c, sp, sm).astype(BF16)

    def to_group(a, d):
        if d == 1:
            return a
        return a.reshape(bsz, t // d, d, a.shape[-1]).transpose(0, 2, 1, 3).reshape(n, a.shape[-1])

    def from_group(a, d):
        if d == 1:
            return a
        return a.reshape(bsz, d, t // d, a.shape[-1]).transpose(0, 2, 1, 3).reshape(n, a.shape[-1])

    def rows_to_cols(r, d):
        return r.reshape(bsz, d, A_HEADS, t // d).transpose(0, 3, 1, 2).reshape(n, A_HEADS)

    def cols_to_rows(cc, d):
        return cc.reshape(bsz, t // d, d, A_HEADS).transpose(0, 2, 3, 1).reshape(bsz * d, A_HEADS, t // d)

    z_a = _matmul(hn_a, w_z, out_dtype=F32, tm=mm, tn=A_WIDTH, name="a_gate")
    hn_g, tabs_g, qkv_g, o_g, lse_g = [], [], [], [], []
    for g, d in enumerate(A_DILATIONS):
        hn_g.append(to_group(hn_a, d))
        tabs_g.append([to_group(a, d) for a in ta2])
        w_g = w_qkv[:, g * 3 * A_WIDTH:(g + 1) * 3 * A_WIDTH]
        qkv = _matmul(hn_g[g], w_g, out_dtype=BF16, tm=mm, tn=A_WIDTH, name=f"a_qkv_g{g}",
                      epilogue=qkv_epilogue, extras=tab_extras(tabs_g[g], mm))
        qkv_g.append(qkv.reshape(bsz * d, t // d, 3 * A_WIDTH))
        o, lse = _attn_a_fwd(qkv_g[g], f"attn_a_fwd_g{g}")
        o_g.append(from_group(o.reshape(n, A_WIDTH), d))
        lse_g.append(rows_to_cols(lse, d))

    def merge_fn(o0, o1, o2, l0, l1, l2, z):
        lmax = jnp.maximum(jnp.maximum(l0, l1), l2)
        e0, e1, e2 = jnp.exp2(l0 - lmax), jnp.exp2(l1 - lmax), jnp.exp2(l2 - lmax)
        den = e0 + e1 + e2
        w0, w1, w2 = e0 / den, e1 / den, e2 / den
        parts = []
        for h in range(A_HEADS):
            hs = slice(h * A_HEAD_DIM, (h + 1) * A_HEAD_DIM)
            parts.append(w0[:, h:h + 1] * o0[:, hs] + w1[:, h:h + 1] * o1[:, hs] + w2[:, h:h + 1] * o2[:, hs])
        o = jnp.concatenate(parts, axis=1)
        return o * _silu(z), o, lmax + jnp.log2(den)

    y_a, o_a2, lse_a = _rowwise(merge_fn, [*o_g, *lse_g, z_a], [],
                                [(A_WIDTH, BF16), (A_WIDTH, F32), (A_HEADS, F32)], tm=tm, name="a_merge_gate")
    out_a = _matmul(y_a, w_a_out, out_dtype=F32, tm=mm, tn=D_MODEL, name="a_out")

    def mid_fn(xt, out, g_post, g_kv, g_b):
        h1 = xt + _rms(out, g_post)
        return h1, _rms(h1, g_kv), _rms(h1, g_b)

    h1, hn_kv, hn_b = _rowwise(mid_fn, [x2, out_a], [w["a_post_norm"], w["kv_norm"], w["b_pre_norm"]],
                               [(D_MODEL, F32), (D_MODEL, BF16), (D_MODEL, BF16)], tm=tm, name="a_post_norm")

    ckr = _matmul(hn_kv, w_down_p, out_dtype=F32, tm=mm, tn=3 * LANES, name="kv_down")

    def latent_fn(ck, c, sp, sm, g):
        return _rms(ck[:, :B_KV_LORA], g), _rope(ck[:, B_KV_LORA:], c, sp, sm)

    c_kv, k_rope = _rowwise(latent_fn, [ckr, *tb2], [w["kv_latent_norm"]], [(B_KV_LORA, BF16), (LANES, F32)],
                            tm=tm, name="kv_latent_norm")

    def kpad_epilogue(acc, o_ref, j, kr):
        for h in range(acc.shape[1] // LANES):
            hs = slice(h * LANES, (h + 1) * LANES)
            o_ref[:, hs] = (acc[:, hs] + kr).astype(BF16)

    kpad = _matmul(c_kv, w_upk, out_dtype=BF16, tm=mm, tn=1024, name="kv_up_k", epilogue=kpad_epilogue,
                   extras=[(k_rope, (mm, LANES), lambda j, i, kk: (i, 0))])
    v_b = _matmul(c_kv, w_upv, out_dtype=BF16, tm=mm, tn=1024, name="kv_up_v")

    proj_b = _matmul(hn_b, w_b_in, out_dtype=F32, tm=mm, tn=w_b_in.shape[1], name="b_in")
    (c_q,) = _rowwise(lambda p, g: (_rms(p[:, :B_Q_LORA], g),), [proj_b], [w["b_q_norm"]], [(B_Q_LORA, BF16)],
                      tm=tm, name="b_q_norm")

    def q_epilogue(acc, o_ref, j, c, sp, sm):
        for h in range(acc.shape[1] // LANES):
            hs = slice(h * LANES, (h + 1) * LANES)
            o_ref[:, hs] = _rope(acc[:, hs], c, sp, sm).astype(BF16)

    q_b = _matmul(c_q, w_q_p, out_dtype=BF16, tm=mm, tn=1024, name="b_q_up", epilogue=q_epilogue,
                  extras=tab_extras(tb2, mm))
    q_b3, kpad3, v_b3 = q_b.reshape(bsz, t, B_KPAD), kpad.reshape(bsz, t, B_KPAD), v_b.reshape(bsz, t, -1)
    v4 = v_b3.reshape(bsz, t, B_HEADS, B_VDIM)
    vt1 = jnp.concatenate([v4, jnp.ones((bsz, t, B_HEADS, 1), BF16),
                           jnp.zeros((bsz, t, B_HEADS, LANES - B_VDIM - 1), BF16)], axis=3)
    vt1 = vt1.transpose(0, 2, 3, 1).reshape(bsz, B_KPAD, t)
    o_b, lse_b = _attn_b_fwd(q_b3, kpad3, vt1)
    o_b2 = o_b.reshape(n, -1)
    (y_b,) = _rowwise(lambda o, p: (o * _silu(p[:, B_Q_LORA:]),), [o_b2, proj_b], [], [(D_MODEL, BF16)],
                      tm=tm, name="b_gate_mul")
    out_b = _matmul(y_b, w_b_out, out_dtype=F32, tm=mm, tn=D_MODEL, name="b_out")

    def head_fn(h1t, out, tgt, g):
        e = h1t + _rms(out, g) - tgt
        loss_row = 0.5 * jnp.mean(e * e, axis=-1, keepdims=True)
        dh2 = e * (1.0 / D_MODEL)
        d_out, dg = _rms_bwd(out, g, dh2)
        return dh2, d_out, dg, jnp.broadcast_to(loss_row * (1.0 / LANES), (loss_row.shape[0], LANES))

    dh2, d_out_b, dg_b_post, loss_acc = _rowwise(
        head_fn, [h1, out_b, tgt2], [w["b_post_norm"]], [(D_MODEL, F32), (D_MODEL, BF16)], [D_MODEL, LANES],
        tm=tm, name="loss_head")

    dy_b = _matmul(d_out_b, w_b_out.T, out_dtype=F32, tm=mm, tn=D_MODEL, name="b_out_dx")
    gw_b_out = _matmul(y_b.T, d_out_b, out_dtype=F32, tm=mm, tn=D_MODEL, tk=2048, name="b_out_dw")

    def gate_b_bwd(dy, o, p):
        z = p[:, B_Q_LORA:]
        return dy * _silu(z), dy * o * _silu_grad(z)

    do_b, dz_b = _rowwise(gate_b_bwd, [dy_b, o_b2, proj_b], [], [(D_MODEL, BF16), (D_MODEL, F32)],
                          tm=tm, name="b_gate_bwd")
    do_b3 = do_b.reshape(bsz, t, -1)
    dq_b, dsum_b = _attn_b_dq(q_b3, kpad3, v_b3, do_b3, o_b, lse_b, tabs_b)
    dk_b, dv_b = _attn_b_dkv(q_b3, kpad3, v_b3, do_b3, lse_b, dsum_b)
    dq_b2, dk_b2, dv_b2 = dq_b.reshape(n, B_KPAD), dk_b.reshape(n, B_KPAD), dv_b.reshape(n, -1)

    dk_bf, dv_bf = dk_b2.astype(BF16), dv_b2.astype(BF16)
    dc_kv = _matmul(dk_bf, w_upk.T, out_dtype=F32, tm=mm, tn=B_KV_LORA, name="kv_up_k_dx")
    dc_kv = _matmul(dv_bf, w_upv.T, out_dtype=F32, tm=mm, tn=B_KV_LORA, name="kv_up_v_dx",
                    epilogue=_add_epilogue, extras=[(dc_kv, (mm, B_KV_LORA), lambda j, i, kk: (i, j))])
    c_kv_t = c_kv.T
    gw_upk = _matmul(c_kv_t, dk_bf, out_dtype=F32, tm=B_KV_LORA, tn=1024, tk=2048, name="kv_up_k_dw")
    gw_upv = _matmul(c_kv_t, dv_bf, out_dtype=F32, tm=B_KV_LORA, tn=1024, tk=2048, name="kv_up_v_dw")

    def latent_bwd(ck, dck, dk, c, sp, sm, g):
        d1, dg = _rms_bwd(ck[:, :B_KV_LORA], g, dck)
        ksum = dk[:, :LANES]
        for h in range(1, B_HEADS):
            ksum = ksum + dk[:, h * LANES:(h + 1) * LANES]
        lane = lax.broadcasted_iota(jnp.int32, ksum.shape, 1)
        ksum = jnp.where((lane >= B_NOPE) & (lane < B_QK_DIM), ksum, 0.0)
        return jnp.concatenate([d1, _rope_t(ksum, c, sp, sm)], axis=1), dg

    dckr, dg_latent = _rowwise(latent_bwd, [ckr, dc_kv, dk_b2, *tb2], [w["kv_latent_norm"]],
                               [(3 * LANES, BF16)], [B_KV_LORA], tm=tm, name="kv_latent_bwd")
    dhn_kv = _matmul(dckr, w_down_p.T, out_dtype=F32, tm=mm, tn=D_MODEL, name="kv_down_dx")
    gw_down_p = _matmul(hn_kv.T, dckr, out_dtype=F32, tm=mm, tn=3 * LANES, tk=2048, name="kv_down_dw")

    dc_q = _matmul(dq_b2, w_q_p.T, out_dtype=F32, tm=mm, tn=B_Q_LORA, name="b_q_up_dx")
    gw_q_p = _matmul(c_q.T, dq_b2, out_dtype=F32, tm=B_Q_LORA, tn=1024, tk=2048, name="b_q_up_dw")

    def q_norm_bwd(p, dcq, dz, g):
        d1, dg = _rms_bwd(p[:, :B_Q_LORA], g, dcq)
        return jnp.concatenate([d1, dz], axis=1), dg

    dproj_b, dg_q_norm = _rowwise(q_norm_bwd, [proj_b, dc_q, dz_b], [w["b_q_norm"]],
                                  [(w_b_in.shape[1], BF16)], [B_Q_LORA], tm=tm, name="b_q_norm_bwd")
    dhn_b = _matmul(dproj_b, w_b_in.T, out_dtype=F32, tm=mm, tn=D_MODEL, name="b_in_dx")
    gw_b_in = _matmul(hn_b.T, dproj_b, out_dtype=F32, tm=mm, tn=w_b_in.shape[1], tk=2048, name="b_in_dw")

    def mid_bwd(h1t, dh2t, dkv, db, g_kv, g_b, g_post, out):
        dxa, ra = _rms_bwd(h1t, g_kv, dkv)
        dxb, rb = _rms_bwd(h1t, g_b, db)
        dh1 = dh2t + dxa + dxb
        d_out, rp = _rms_bwd(out, g_post, dh1)
        return dh1, d_out, ra, rb, rp

    def mid_bwd_fn(h1t, dh2t, dkv, db, out, g_kv, g_b, g_post):
        return mid_bwd(h1t, dh2t, dkv, db, g_kv, g_b, g_post, out)

    dh1, d_out_a, dg_kv, dg_b_pre, dg_a_post = _rowwise(
        mid_bwd_fn, [h1, dh2, dhn_kv, dhn_b, out_a], [w["kv_norm"], w["b_pre_norm"], w["a_post_norm"]],
        [(D_MODEL, F32), (D_MODEL, BF16)], [D_MODEL] * 3, tm=tm, name="mid_bwd")

    dy_a = _matmul(d_out_a, w_a_out.T, out_dtype=F32, tm=mm, tn=A_WIDTH, name="a_out_dx")
    gw_a_out = _matmul(y_a.T, d_out_a, out_dtype=F32, tm=mm, tn=D_MODEL, tk=2048, name="a_out_dw")

    def gate_a_bwd(dy, o, z):
        do = dy * _silu(z)
        prod = do * o
        lane = lax.broadcasted_iota(jnp.int32, (prod.shape[0], A_HEADS), 1)
        dsum = jnp.zeros((prod.shape[0], A_HEADS), F32)
        for h in range(A_HEADS):
            col = jnp.sum(prod[:, h * A_HEAD_DIM:(h + 1) * A_HEAD_DIM], axis=1, keepdims=True)
            dsum = jnp.where(lane == h, col, dsum)
        return do, dy * o * _silu_grad(z), dsum

    do_a, dz_a, dsum_a = _rowwise(gate_a_bwd, [dy_a, o_a2, z_a], [],
                                  [(A_WIDTH, BF16), (A_WIDTH, BF16), (A_HEADS, F32)], tm=tm, name="a_gate_bwd")
    dhn_a = _matmul(dz_a, w_z.T, out_dtype=F32, tm=mm, tn=D_MODEL, name="a_gate_dx")
    gw_parts = []
    for g, d in enumerate(A_DILATIONS):
        s_n, ln = bsz * d, t // d
        dqkv = _attn_a_bwd(qkv_g[g], to_group(do_a, d).reshape(s_n, ln, A_WIDTH), cols_to_rows(lse_a, d),
                           cols_to_rows(dsum_a, d), [a.reshape(s_n, ln, LANES) for a in tabs_g[g]],
                           f"attn_a_bwd_g{g}").reshape(n, 3 * A_WIDTH)
        w_g_t = w_qkv[:, g * 3 * A_WIDTH:(g + 1) * 3 * A_WIDTH].T
        if d == 1:
            dhn_a = _matmul(dqkv, w_g_t, out_dtype=F32, tm=mm, tn=D_MODEL, name=f"a_qkv_dx_g{g}",
                            epilogue=_add_epilogue, extras=[(dhn_a, (mm, D_MODEL), lambda j, i, kk: (i, j))])
        else:
            dhn_a = dhn_a + from_group(_matmul(dqkv, w_g_t, out_dtype=F32, tm=mm, tn=D_MODEL,
                                               name=f"a_qkv_dx_g{g}"), d)
        gw_parts.append(_matmul(hn_g[g].T, dqkv, out_dtype=F32, tm=mm, tn=1024, tk=2048, name=f"a_qkv_dw_g{g}"))
    gw_parts.append(_matmul(hn_a.T, dz_a, out_dtype=F32, tm=mm, tn=1024, tk=2048, name="a_gate_dw"))
    gw_a_in = jnp.concatenate(gw_parts, axis=1)

    def first_bwd(xt, dhn, dh1t, g):
        dx, dg = _rms_bwd(xt, g, dhn)
        return dh1t + dx, dg

    grad_x, dg_a_pre = _rowwise(first_bwd, [x2, dhn_a, dh1], [w["a_pre_norm"]], [(D_MODEL, F32)], [D_MODEL],
                                tm=tm, name="a_pre_norm_bwd")

    gw_down = jnp.concatenate([gw_down_p[:, :B_KV_LORA], gw_down_p[:, B_KV_LORA + B_NOPE:B_KV_LORA + B_QK_DIM]], axis=1)
    gw_up = jnp.concatenate([gw_upk.reshape(B_KV_LORA, B_HEADS, LANES)[:, :, :B_NOPE],
                             gw_upv.reshape(B_KV_LORA, B_HEADS, B_VDIM)], axis=2).reshape(B_KV_LORA, -1)
    gw_q_up = gw_q_p.reshape(B_Q_LORA, B_HEADS, LANES)[:, :, :B_QK_DIM].reshape(B_Q_LORA, -1)
    grads = {"a_w_in": gw_a_in, "a_w_out": gw_a_out, "kv_w_down": gw_down, "kv_w_up": gw_up,
             "b_w_in": gw_b_in, "b_w_q_up": gw_q_up, "b_w_out": gw_b_out}
    gains = {"a_pre_norm": dg_a_pre, "a_post_norm": dg_a_post, "kv_norm": dg_kv, "kv_latent_norm": dg_latent,
             "b_pre_norm": dg_b_pre, "b_q_norm": dg_q_norm, "b_post_norm": dg_b_post}
    gains = {k: jnp.sum(a, axis=0) for k, a in gains.items()}
    return jnp.sum(loss_acc), grad_x.reshape(bsz, t, D_MODEL), grads, gains


WEIGHT_ORDER = ("a_pre_norm", "a_w_in", "a_w_out", "a_post_norm", "kv_norm", "kv_w_down", "kv_latent_norm",
                "kv_w_up", "b_pre_norm", "b_w_in", "b_q_norm", "b_w_q_up", "b_w_out", "b_post_norm")
MATRICES = (("a_w_in", 1024, 10240, 1), ("a_w_out", 1024, 1024, 0), ("kv_w_down", 1024, 288, 0),
            ("kv_w_up", 256, 2048, 1), ("b_w_in", 1024, 1408, 1), ("b_w_q_up", 384, 1536, 1),
            ("b_w_out", 1024, 1024, 0))
SHARDED_GAINS = ("a_pre_norm", "a_post_norm")
GAIN_WIDTHS = (("a_pre_norm", 1024), ("a_post_norm", 1024), ("kv_norm", 1024), ("kv_latent_norm", 256),
               ("b_pre_norm", 1024), ("b_q_norm", 384), ("b_post_norm", 1024))
GAIN_ROWS = 48


def _shard_rows(rows, cols):
    return rows * cols // (N_DEV * LANES)


def _whole_from_blocks(blocks, rows, cols, axis):
    if axis == 1:
        return blocks.reshape(N_DEV, rows, cols // N_DEV).transpose(1, 0, 2).reshape(rows, cols)
    return blocks.reshape(rows, cols)


def _blocks_from_whole(whole, rows, cols, axis):
    if axis == 1:
        whole = whole.reshape(rows, N_DEV, cols // N_DEV).transpose(1, 0, 2)
    return whole.reshape(N_DEV, -1, LANES)


def kernel(x, positions, a_pre_norm, a_w_in, a_w_out, a_post_norm, kv_norm, kv_w_down, kv_latent_norm, kv_w_up, b_pre_norm, b_w_in, b_q_norm, b_w_q_up, b_w_out, b_post_norm, loss_target, m_a_pre_norm, m_a_w_in, m_a_w_out, m_a_post_norm, m_kv_norm, m_kv_w_down, m_kv_latent_norm, m_kv_w_up, m_b_pre_norm, m_b_w_in, m_b_q_norm, m_b_w_q_up, m_b_w_out, m_b_post_norm, v_a_pre_norm, v_a_w_in, v_a_w_out, v_a_post_norm, v_kv_norm, v_kv_w_down, v_kv_latent_norm, v_kv_w_up, v_b_pre_norm, v_b_w_in, v_b_q_norm, v_b_w_q_up, v_b_w_out, v_b_post_norm):
    weights = dict(a_pre_norm=a_pre_norm, a_w_in=a_w_in, a_w_out=a_w_out, a_post_norm=a_post_norm, kv_norm=kv_norm,
                   kv_w_down=kv_w_down, kv_latent_norm=kv_latent_norm, kv_w_up=kv_w_up, b_pre_norm=b_pre_norm,
                   b_w_in=b_w_in, b_q_norm=b_q_norm, b_w_q_up=b_w_q_up, b_w_out=b_w_out, b_post_norm=b_post_norm)
    m_in = dict(a_pre_norm=m_a_pre_norm, a_w_in=m_a_w_in, a_w_out=m_a_w_out, a_post_norm=m_a_post_norm,
                kv_norm=m_kv_norm, kv_w_down=m_kv_w_down, kv_latent_norm=m_kv_latent_norm, kv_w_up=m_kv_w_up,
                b_pre_norm=m_b_pre_norm, b_w_in=m_b_w_in, b_q_norm=m_b_q_norm, b_w_q_up=m_b_w_q_up,
                b_w_out=m_b_w_out, b_post_norm=m_b_post_norm)
    v_in = dict(a_pre_norm=v_a_pre_norm, a_w_in=v_a_w_in, a_w_out=v_a_w_out, a_post_norm=v_a_post_norm,
                kv_norm=v_kv_norm, kv_w_down=v_kv_w_down, kv_latent_norm=v_kv_latent_norm, kv_w_up=v_kv_w_up,
                b_pre_norm=v_b_pre_norm, b_w_in=v_b_w_in, b_q_norm=v_b_q_norm, b_w_q_up=v_b_w_q_up,
                b_w_out=v_b_w_out, b_post_norm=v_b_post_norm)
    me = 4 * lax.axis_index("x") + 2 * lax.axis_index("y") + lax.axis_index("c")

    flat = jnp.concatenate([weights[name].astype(BF16).reshape(-1, LANES) for name, _, _, _ in MATRICES], axis=0)
    gathered = _all_gather(flat, "gather_weights")
    whole = {}
    off = 0
    for name, rows, cols, axis in MATRICES:
        nr = _shard_rows(rows, cols)
        whole[name] = _whole_from_blocks(gathered[:, off:off + nr], rows, cols, axis)
        off += nr
    gain_shard = jnp.concatenate([weights[name].reshape(1, LANES) for name in SHARDED_GAINS]
                                 + [jnp.zeros((8 - len(SHARDED_GAINS), LANES), F32)], axis=0)
    gain_blocks = _all_gather(gain_shard, "gather_gains")
    for i, name in enumerate(SHARDED_GAINS):
        whole[name] = gain_blocks[:, i, :].reshape(1, D_MODEL)
    for name in ("kv_norm", "kv_latent_norm", "b_pre_norm", "b_q_norm", "b_post_norm"):
        whole[name] = weights[name].reshape(1, -1)

    loss_part, grad_x, grads, gains = _local_step(x, positions, loss_target, whole)

    blocks = jnp.concatenate([_blocks_from_whole(grads[name], rows, cols, axis).astype(BF16)
                              for name, rows, cols, axis in MATRICES], axis=1)
    landed = _exchange_blocks(blocks, "scatter_grads")
    summed = _sum_slots(landed, tr=2512, name="sum_grads")
    grad_out = {}
    off = 0
    for name, rows, cols, axis in MATRICES:
        nr = _shard_rows(rows, cols)
        grad_out[name] = summed[off:off + nr].reshape(weights[name].shape)
        off += nr

    vec = jnp.concatenate([gains[name] for name, _ in GAIN_WIDTHS] + [jnp.full((LANES,), loss_part, F32)])
    vec = jnp.pad(vec, (0, GAIN_ROWS * LANES - vec.shape[0])).reshape(GAIN_ROWS, LANES)
    total = _sum_slots(_all_gather(vec, "gather_gain_grads"), tr=GAIN_ROWS, name="sum_gain_grads").reshape(-1)
    off = 0
    for name, width in GAIN_WIDTHS:
        g = total[off:off + width]
        if name in SHARDED_GAINS:
            g = lax.dynamic_slice(g, (me * LANES,), (LANES,))
        grad_out[name] = g.reshape(weights[name].shape)
        off += width
    loss = total[off]

    deltas, new_m, new_v = {}, {}, {}
    for name in WEIGHT_ORDER:
        deltas[name], new_m[name], new_v[name] = _adamw(weights[name], grad_out[name], m_in[name], v_in[name],
                                                        "adamw_" + name)
    return (loss, grad_x, *[grad_out[k] for k in WEIGHT_ORDER], *[deltas[k] for k in WEIGHT_ORDER],
            *[new_m[k] for k in WEIGHT_ORDER], *[new_v[k] for k in WEIGHT_ORDER])
```

```python
import jax
import jax.numpy as jnp
from jax import lax
from jax.experimental import pallas as pl
from jax.experimental.pallas import tpu as pltpu

F32 = jnp.float32
BF16 = jnp.bfloat16

N_DEV = 8
D_MODEL = 1024
NORM_EPS = 1e-6
A_GROUPS = 3
A_DILATIONS = (1, 4, 16)
A_HEADS = 8
A_HEAD_DIM = 128
A_WIDTH = 1024
A_ROT_DIM = 32
A_ROPE_THETA = 500000.0
A_QKV = A_GROUPS * 3 * A_WIDTH
B_HEADS = 16
B_NOPE = 64
B_ROPE = 32
B_QK_DIM = 96
B_VDIM = 64
B_Q_LORA = 384
B_KV_LORA = 256
B_ROPE_THETA = 10000.0
B_KPAD = B_HEADS * 128
ADAM_LR = 0.001
ADAM_B1 = 0.9
ADAM_B2 = 0.999
ADAM_EPS = 1e-08
ADAM_WD = 0.01
ADAM_STEP = 10

LANES = 128
BAND = 128
NEG = -1e30
VMEM_LIMIT = 56 * 1024 * 1024
MESH = pl.DeviceIdType.MESH


def _cparams(sem):
    return pltpu.CompilerParams(dimension_semantics=sem, vmem_limit_bytes=VMEM_LIMIT)


def _rowwise(fn, rows, bcast, outs, accs=(), *, tm, name):
    n = rows[0].shape[0]
    nr, nb, no = len(rows), len(bcast), len(outs)

    def body(*refs):
        res = fn(*[r[...] for r in refs[:nr + nb]])
        out_refs = refs[nr + nb:nr + nb + no]
        acc_refs = refs[nr + nb + no:]
        for r, v in zip(out_refs, res[:no]):
            r[...] = v.astype(r.dtype)
        if acc_refs:
            @pl.when(pl.program_id(0) == 0)
            def _():
                for r in acc_refs:
                    r[...] = jnp.zeros_like(r)
            for r, v in zip(acc_refs, res[no:]):
                r[...] += v.reshape(tm // 8, 8, v.shape[-1]).sum(axis=0)

    in_specs = [pl.BlockSpec((tm, a.shape[1]), lambda i: (i, 0)) for a in rows]
    in_specs += [pl.BlockSpec(a.shape, lambda i: (0, 0)) for a in bcast]
    out_specs = [pl.BlockSpec((tm, c), lambda i: (i, 0)) for c, _ in outs]
    out_specs += [pl.BlockSpec((8, c), lambda i: (0, 0)) for c in accs]
    out_shape = [jax.ShapeDtypeStruct((n, c), dt) for c, dt in outs]
    out_shape += [jax.ShapeDtypeStruct((8, c), F32) for c in accs]
    return pl.pallas_call(
        body, name=name, grid=(n // tm,), in_specs=in_specs, out_specs=out_specs, out_shape=out_shape,
        compiler_params=_cparams(("arbitrary",)))(*rows, *bcast)


def _matmul(a, b, *, out_dtype, tm, tn, tk=None, name, epilogue=None, extras=(), ta=False, tb=False):
    k, m = a.shape[::-1] if not ta else a.shape
    n = b.shape[0] if tb else b.shape[1]
    tk = tk or k
    nk = k // tk
    ne = len(extras)
    dot = _dot_tn if ta else (_dot_nt if tb else _dot)

    def body(*refs):
        a_ref, b_ref = refs[:2]
        ex = refs[2:2 + ne]
        o_ref = refs[2 + ne]
        part = dot(a_ref[...].astype(BF16), b_ref[...].astype(BF16))

        def finish(acc):
            if epilogue is None:
                o_ref[...] = acc.astype(o_ref.dtype)
            else:
                epilogue(acc, o_ref, pl.program_id(0), *[e[...] for e in ex])

        if nk == 1:
            finish(part)
        else:
            acc_ref = refs[-1]
            kk = pl.program_id(2)

            @pl.when(kk == 0)
            def _():
                acc_ref[...] = part

            @pl.when(kk > 0)
            def _():
                acc_ref[...] += part

            @pl.when(kk == nk - 1)
            def _():
                finish(acc_ref[...])

    a_spec = pl.BlockSpec((tk, tm), lambda j, i, kk: (kk, i)) if ta else pl.BlockSpec((tm, tk), lambda j, i, kk: (i, kk))
    b_spec = pl.BlockSpec((tn, tk), lambda j, i, kk: (j, kk)) if tb else pl.BlockSpec((tk, tn), lambda j, i, kk: (kk, j))
    in_specs = [a_spec, b_spec] + [pl.BlockSpec(bs, im) for _, bs, im in extras]
    return pl.pallas_call(
        body, name=name, grid=(n // tn, m // tm, nk), in_specs=in_specs,
        out_specs=pl.BlockSpec((tm, tn), lambda j, i, kk: (i, j)),
        out_shape=jax.ShapeDtypeStruct((m, n), out_dtype),
        scratch_shapes=[pltpu.VMEM((tm, tn), F32)] if nk > 1 else [],
        compiler_params=_cparams(("parallel", "parallel", "arbitrary")))(a, b, *[e[0] for e in extras])


def _add_epilogue(acc, o_ref, j, prev):
    o_ref[...] = (acc + prev).astype(o_ref.dtype)


def _rope(x, c, sp, sm):
    return x * c + pltpu.roll(x, 16, 1) * sp + pltpu.roll(x, LANES - 16, 1) * sm


def _rope_t(dy, c, sp, sm):
    return dy * c + pltpu.roll(dy * sp, LANES - 16, 1) + pltpu.roll(dy * sm, 16, 1)


def _rope_tables(positions, theta, rot_dim, lane0):
    half = rot_dim // 2
    inv_freq = 1.0 / (theta ** (jnp.arange(half, dtype=F32) * (2.0 / rot_dim)))
    ang = positions.astype(F32)[..., None] * inv_freq
    cos, sin = jnp.cos(ang), jnp.sin(ang)
    shape = positions.shape + (LANES,)
    c = jnp.ones(shape, F32).at[..., lane0:lane0 + half].set(cos).at[..., lane0 + half:lane0 + rot_dim].set(cos)
    sp = jnp.zeros(shape, F32).at[..., lane0 + half:lane0 + rot_dim].set(sin)
    sm = jnp.zeros(shape, F32).at[..., lane0:lane0 + half].set(-sin)
    return c, sp, sm


def _rms(x, g):
    xf = x.astype(F32)
    return xf * lax.rsqrt(jnp.mean(xf * xf, axis=-1, keepdims=True) + NORM_EPS) * g


def _rms_bwd(x, g, dy):
    xf = x.astype(F32)
    rstd = lax.rsqrt(jnp.mean(xf * xf, axis=-1, keepdims=True) + NORM_EPS)
    xhat = xf * rstd
    dxhat = dy * g
    dx = rstd * (dxhat - xhat * jnp.mean(dxhat * xhat, axis=-1, keepdims=True))
    return dx, dy * xhat


def _silu(z):
    return z * jax.nn.sigmoid(z)


def _silu_grad(z):
    s = jax.nn.sigmoid(z)
    return s * (1.0 + z * (1.0 - s))


def _dot_nt(a, b):
    return lax.dot_general(a, b, (((1,), (1,)), ((), ())), preferred_element_type=F32)


def _dot_tn(a, b):
    return lax.dot_general(a, b, (((0,), (0,)), ((), ())), preferred_element_type=F32)


def _dot(a, b):
    return jnp.dot(a, b, preferred_element_type=F32)


A_SCALE = A_HEAD_DIM ** -0.5
LOG2E = 1.4426950408889634
A_C2 = A_SCALE * LOG2E
A_HEAD_GROUP = 4


def _attn_a_fwd(qkv, name):
    s_n, ln, _ = qkv.shape
    nb = ln // BAND
    blk = (None, BAND, A_WIDTH)

    def body(q_ref, kc_ref, kp_ref, vc_ref, vp_ref, o_ref, lse_ref):
        kpos = lax.broadcasted_iota(jnp.int32, (2 * BAND, BAND), 0)
        qpos = lax.broadcasted_iota(jnp.int32, (2 * BAND, BAND), 1) + BAND
        first_key = jnp.where(pl.program_id(1) > 0, 0, BAND)
        mask = (kpos <= qpos) & (kpos >= qpos - BAND) & (kpos >= first_key)
        rows = []
        for h0 in range(0, A_HEADS, A_HEAD_GROUP):
            hss = [slice(h * A_HEAD_DIM, (h + 1) * A_HEAD_DIM) for h in range(h0, h0 + A_HEAD_GROUP)]
            sts = [_dot_nt(jnp.concatenate([kp_ref[:, hs], kc_ref[:, hs]], axis=0), q_ref[:, hs]) for hs in hss]
            ps, ls = [], []
            for st in sts:
                st = jnp.where(mask, st * A_C2, NEG)
                m = jnp.max(st, axis=0, keepdims=True)
                p = jnp.exp2(st - m)
                l_row = jnp.sum(p, axis=0, keepdims=True)
                ps.append(p.astype(BF16))
                ls.append(l_row)
                rows.append(m + jnp.log2(l_row))
            ots = [_dot_tn(jnp.concatenate([vp_ref[:, hs], vc_ref[:, hs]], axis=0), p) for hs, p in zip(hss, ps)]
            for hs, o_t, l_row in zip(hss, ots, ls):
                o_ref[:, hs] = (o_t / l_row).T.astype(BF16)
        lse_ref[...] = jnp.concatenate(rows, axis=0)

    def col(c, off):
        return lambda s, l: (s, jnp.maximum(l + off, 0), c)

    return pl.pallas_call(
        body, name=name, grid=(s_n, nb),
        in_specs=[pl.BlockSpec(blk, col(0, 0)), pl.BlockSpec(blk, col(1, 0)), pl.BlockSpec(blk, col(1, -1)),
                  pl.BlockSpec(blk, col(2, 0)), pl.BlockSpec(blk, col(2, -1))],
        out_specs=[pl.BlockSpec(blk, lambda s, l: (s, l, 0)),
                   pl.BlockSpec((None, A_HEADS, BAND), lambda s, l: (s, 0, l))],
        out_shape=[jax.ShapeDtypeStruct((s_n, ln, A_WIDTH), BF16), jax.ShapeDtypeStruct((s_n, A_HEADS, ln), F32)],
        compiler_params=_cparams(("parallel", "arbitrary")))(qkv, qkv, qkv, qkv, qkv)


def _attn_a_bwd(qkv, do, lse2, dsum, tabs, name):
    s_n, ln, _ = qkv.shape
    nb = ln // BAND
    blk = (None, BAND, A_WIDTH)

    def body(q_ref, qn_ref, kc_ref, kp_ref, vc_ref, vp_ref, do_ref, don_ref, lse_ref, lsen_ref, ds_ref, dsn_ref,
             c_ref, sp_ref, sm_ref, out_ref):
        l_idx = pl.program_id(1)
        kpos = lax.broadcasted_iota(jnp.int32, (2 * BAND, BAND), 0)
        qpos = lax.broadcasted_iota(jnp.int32, (2 * BAND, BAND), 1) + BAND
        first_key = jnp.where(l_idx > 0, 0, BAND)
        mask_q = (kpos <= qpos) & (kpos >= qpos - BAND) & (kpos >= first_key)
        kpos2 = lax.broadcasted_iota(jnp.int32, (BAND, 2 * BAND), 0)
        qpos2 = lax.broadcasted_iota(jnp.int32, (BAND, 2 * BAND), 1)
        last_query = jnp.where(l_idx < nb - 1, 2 * BAND, BAND)
        mask_k = (kpos2 <= qpos2) & (kpos2 >= qpos2 - BAND) & (qpos2 < last_query)
        c, sp, sm = c_ref[...], sp_ref[...], sm_ref[...]
        lse_q, ds_q = lse_ref[...], ds_ref[...]
        lse_k = jnp.concatenate([lse_q, lsen_ref[...]], axis=1)
        ds_k = jnp.concatenate([ds_q, dsn_ref[...]], axis=1)
        for h0 in range(0, A_HEADS, A_HEAD_GROUP):
            hl = list(range(h0, h0 + A_HEAD_GROUP))
            hss = [slice(h * A_HEAD_DIM, (h + 1) * A_HEAD_DIM) for h in hl]
            k2s = [jnp.concatenate([kp_ref[:, hs], kc_ref[:, hs]], axis=0) for hs in hss]
            v2s = [jnp.concatenate([vp_ref[:, hs], vc_ref[:, hs]], axis=0) for hs in hss]
            q2s = [jnp.concatenate([q_ref[:, hs], qn_ref[:, hs]], axis=0) for hs in hss]
            do2s = [jnp.concatenate([do_ref[:, hs], don_ref[:, hs]], axis=0) for hs in hss]
            sts = [_dot_nt(k2, q_ref[:, hs]) for k2, hs in zip(k2s, hss)]
            dpts = [_dot_nt(v2, do_ref[:, hs]) for v2, hs in zip(v2s, hss)]
            st2s = [_dot_nt(kc_ref[:, hs], q2) for q2, hs in zip(q2s, hss)]
            dpt2s = [_dot_nt(vc_ref[:, hs], do2) for do2, hs in zip(do2s, hss)]
            dsts, dst2s, p2s = [], [], []
            for i, h in enumerate(hl):
                p = jnp.exp2(jnp.where(mask_q, sts[i] * A_C2, NEG) - lse_q[h:h + 1])
                dsts.append((p * (dpts[i] - ds_q[h:h + 1]) * A_SCALE).astype(BF16))
                p2 = jnp.exp2(jnp.where(mask_k, st2s[i] * A_C2, NEG) - lse_k[h:h + 1])
                dst2s.append((p2 * (dpt2s[i] - ds_k[h:h + 1]) * A_SCALE).astype(BF16))
                p2s.append(p2.astype(BF16))
            dqs = [_dot_tn(dsts[i], k2s[i]) for i in range(A_HEAD_GROUP)]
            dks = [_dot(dst2s[i], q2s[i]) for i in range(A_HEAD_GROUP)]
            dvs = [_dot(p2s[i], do2s[i]) for i in range(A_HEAD_GROUP)]
            for i, h in enumerate(hl):
                out_ref[:, hss[i]] = _rope_t(dqs[i], c, sp, sm).astype(BF16)
                out_ref[:, A_WIDTH + h * A_HEAD_DIM:A_WIDTH + (h + 1) * A_HEAD_DIM] = _rope_t(dks[i], c, sp, sm).astype(BF16)
                out_ref[:, 2 * A_WIDTH + h * A_HEAD_DIM:2 * A_WIDTH + (h + 1) * A_HEAD_DIM] = dvs[i].astype(BF16)

    def col(c, off):
        return lambda s, l: (s, jnp.clip(l + off, 0, nb - 1), c)

    def row(off):
        return pl.BlockSpec((None, A_HEADS, BAND), lambda s, l: (s, 0, jnp.clip(l + off, 0, nb - 1)))

    tspec = pl.BlockSpec((None, BAND, LANES), lambda s, l: (s, l, 0))
    in_specs = [pl.BlockSpec(blk, col(0, 0)), pl.BlockSpec(blk, col(0, 1)),
                pl.BlockSpec(blk, col(1, 0)), pl.BlockSpec(blk, col(1, -1)),
                pl.BlockSpec(blk, col(2, 0)), pl.BlockSpec(blk, col(2, -1)),
                pl.BlockSpec(blk, col(0, 0)), pl.BlockSpec(blk, col(0, 1)),
                row(0), row(1), row(0), row(1), tspec, tspec, tspec]
    return pl.pallas_call(
        body, name=name, grid=(s_n, nb), in_specs=in_specs,
        out_specs=pl.BlockSpec((None, BAND, 3 * A_WIDTH), lambda s, l: (s, l, 0)),
        out_shape=jax.ShapeDtypeStruct((s_n, ln, 3 * A_WIDTH), BF16),
        compiler_params=_cparams(("parallel", "arbitrary")))(
            qkv, qkv, qkv, qkv, qkv, qkv, do, do, lse2, lse2, dsum, dsum, *tabs)


B_TQ = 256
B_SCALE = B_QK_DIM ** -0.5
B_C2 = B_SCALE * LOG2E


def _b_row_tile(tq):
    return pl.BlockSpec((None, None, 8, tq), lambda b, j, i: (b, j, 0, i))


def _b_specs(t, tq, tk):
    pair_tile = pl.BlockSpec((None, tq, 2 * LANES), lambda b, j, i: (b, i, j))
    pair_full = pl.BlockSpec((None, t, 2 * LANES), lambda b, j, i: (b, 0, j))
    one_tile = pl.BlockSpec((None, tk, LANES), lambda b, j, i: (b, i, j))
    one_full = pl.BlockSpec((None, t, LANES), lambda b, j, i: (b, 0, j))
    tab_tile = pl.BlockSpec((None, tq, LANES), lambda b, j, i: (b, i, 0))
    return pair_tile, pair_full, one_tile, one_full, tab_tile


def _key_le_query(kb, qb, tk, tq):
    kpos = kb * tk + lax.broadcasted_iota(jnp.int32, (tk, tq), 0)
    qpos = qb * tq + lax.broadcasted_iota(jnp.int32, (tk, tq), 1)
    return kpos <= qpos


def _attn_b_fwd(q, kpad, vt1):
    bsz, t, _ = q.shape
    tq = tk = B_TQ
    nq = t // tq

    def body(q_ref, k_ref, vt_ref, o_ref, lse_ref):
        qblk = pl.program_id(2)
        qs = [q_ref[:, hh * LANES:(hh + 1) * LANES] for hh in range(2)]

        def scores(kb):
            start = pl.multiple_of(kb * tk, tk)
            return [_dot_nt(k_ref[pl.ds(start, tk), hh * LANES:(hh + 1) * LANES], qs[hh]) for hh in range(2)]

        def pv(kb, ps):
            start = pl.multiple_of(kb * tk, tk)
            return [_dot(vt_ref[hh * LANES:(hh + 1) * LANES, pl.ds(start, tk)], ps[hh]) for hh in range(2)]

        def softmax(ss, ms, accs, kb, masked):
            out_m, out_acc, out_p = [], [], []
            for hh in range(2):
                s = ss[hh] * B_C2
                if masked:
                    s = jnp.where(_key_le_query(kb, qblk, tk, tq), s, NEG)
                m_new = jnp.maximum(ms[hh], jnp.max(s, axis=0, keepdims=True))
                out_acc.append(jnp.exp2(ms[hh] - m_new) * accs[hh])
                out_p.append(jnp.exp2(s - m_new).astype(BF16))
                out_m.append(m_new)
            return out_m, out_acc, out_p

        def step(kb, carry):
            ss, ps, ms, accs = carry
            pvs = pv(jnp.maximum(kb - 1, 0), ps)
            ss_next = scores(kb + 1)
            accs = [accs[hh] + pvs[hh] for hh in range(2)]
            ms, accs, ps = softmax(ss, ms, accs, kb, False)
            return (ss_next, ps, ms, accs)

        init = (scores(0), [jnp.zeros((tk, tq), BF16)] * 2, [jnp.full((1, tq), NEG, F32)] * 2,
                [jnp.zeros((LANES, tq), F32)] * 2)
        ss, ps, ms, accs = lax.fori_loop(0, qblk, step, init)
        pvs = pv(jnp.maximum(qblk - 1, 0), ps)
        accs = [accs[hh] + pvs[hh] for hh in range(2)]
        ms, accs, ps = softmax(ss, ms, accs, qblk, True)
        pvs = pv(qblk, ps)
        accs = [accs[hh] + pvs[hh] for hh in range(2)]
        ls = [accs[hh][B_VDIM:B_VDIM + 1] for hh in range(2)]
        o_t = jnp.concatenate([accs[0][:B_VDIM] / ls[0], accs[1][:B_VDIM] / ls[1]], axis=0)
        o_ref[...] = o_t.T
        lse_ref[...] = jnp.concatenate([ms[0] + jnp.log2(ls[0]), ms[1] + jnp.log2(ls[1]),
                                        jnp.zeros((6, tq), F32)], axis=0)

    pair_tile, pair_full, one_tile, _, _ = _b_specs(t, tq, tk)
    vt_spec = pl.BlockSpec((None, 2 * LANES, t), lambda b, j, i: (b, j, 0))
    return pl.pallas_call(
        body, name="attn_b_fwd", grid=(bsz, B_HEADS // 2, nq),
        in_specs=[pair_tile, pair_full, vt_spec],
        out_specs=[one_tile, _b_row_tile(tq)],
        out_shape=[jax.ShapeDtypeStruct((bsz, t, B_HEADS * B_VDIM), F32),
                   jax.ShapeDtypeStruct((bsz, B_HEADS // 2, 8, t), F32)],
        compiler_params=_cparams(("parallel", "parallel", "arbitrary")))(q, kpad, vt1)


def _attn_b_dq(q, kpad, v, do, o, lse2, tabs):
    bsz, t, _ = q.shape
    tq = tk = B_TQ
    nq = t // tq

    def body(q_ref, k_ref, v_ref, do_ref, o_ref, lse_ref, c_ref, sp_ref, sm_ref, dq_ref, dsum_ref):
        qblk = pl.program_id(2)
        lane = lax.broadcasted_iota(jnp.int32, (tq, LANES), 1)
        do_b = do_ref[...]
        prod = do_b.astype(F32) * o_ref[...]
        hi = prod.astype(BF16)
        lo = (prod - hi.astype(F32)).astype(BF16)
        sel_lane = lax.broadcasted_iota(jnp.int32, (8, LANES), 1)
        sel_row = lax.broadcasted_iota(jnp.int32, (8, LANES), 0)
        sel = jnp.where((sel_lane < B_VDIM) == (sel_row == 0), 1.0, 0.0)
        sel = jnp.where(sel_row < 2, sel, 0.0).astype(BF16)
        dsum = _dot_nt(sel, hi) + _dot_nt(sel, lo)
        dsum_ref[...] = dsum
        lse = lse_ref[...]
        qs = [q_ref[:, hh * LANES:(hh + 1) * LANES] for hh in range(2)]
        zero = jnp.zeros_like(do_b)
        dos = [jnp.where(lane < B_VDIM, do_b, zero), jnp.where(lane < B_VDIM, zero, do_b)]

        def kblock(kb, hh):
            start = pl.multiple_of(kb * tk, tk)
            return k_ref[pl.ds(start, tk), hh * LANES:(hh + 1) * LANES]

        def dots(kb):
            start = pl.multiple_of(kb * tk, tk)
            vb = v_ref[pl.ds(start, tk), :]
            return ([_dot_nt(kblock(kb, hh), qs[hh]) for hh in range(2)], [_dot_nt(vb, dos[hh]) for hh in range(2)])

        def ds_of(ss, dps, kb, masked):
            out = []
            for hh in range(2):
                s = ss[hh] * B_C2
                if masked:
                    s = jnp.where(_key_le_query(kb, qblk, tk, tq), s, NEG)
                p = jnp.exp2(s - lse[hh:hh + 1])
                out.append((p * (dps[hh] - dsum[hh:hh + 1]) * B_SCALE).astype(BF16))
            return out

        def accum(acc, kb, dss):
            return [acc[hh] + _dot_tn(kblock(kb, hh), dss[hh]) for hh in range(2)]

        def step(kb, carry):
            dss, acc = carry
            ss, dps = dots(kb)
            acc = accum(acc, jnp.maximum(kb - 1, 0), dss)
            return (ds_of(ss, dps, kb, False), acc)

        init = ([jnp.zeros((tk, tq), BF16)] * 2, [jnp.zeros((LANES, tq), F32)] * 2)
        dss, acc = lax.fori_loop(0, qblk, step, init)
        ss, dps = dots(qblk)
        acc = accum(acc, jnp.maximum(qblk - 1, 0), dss)
        acc = accum(acc, qblk, ds_of(ss, dps, qblk, True))
        c, sp, sm = c_ref[...], sp_ref[...], sm_ref[...]
        for hh in range(2):
            dq_ref[:, hh * LANES:(hh + 1) * LANES] = _rope_t(acc[hh].T, c, sp, sm).astype(BF16)

    pair_tile, pair_full, one_tile, one_full, tab_tile = _b_specs(t, tq, tk)
    row_tile = _b_row_tile(tq)
    return pl.pallas_call(
        body, name="attn_b_dq", grid=(bsz, B_HEADS // 2, nq),
        in_specs=[pair_tile, pair_full, one_full, one_tile, one_tile, row_tile, tab_tile, tab_tile, tab_tile],
        out_specs=[pair_tile, row_tile],
        out_shape=[jax.ShapeDtypeStruct((bsz, t, B_KPAD), BF16),
                   jax.ShapeDtypeStruct((bsz, B_HEADS // 2, 8, t), F32)],
        compiler_params=_cparams(("parallel", "parallel", "arbitrary")))(q, kpad, v, do, o, lse2, *tabs)


def _attn_b_dkv(q, kpad, v, do, lse2, dsum):
    bsz, t, _ = q.shape
    tq = tk = B_TQ
    nq = t // tq

    def body(q_ref, k_ref, v_ref, do_ref, lse_ref, dsum_ref, dk_ref, dv_ref):
        kblk = pl.program_id(2)
        lane = lax.broadcasted_iota(jnp.int32, (tk, LANES), 1)
        ks = [k_ref[:, hh * LANES:(hh + 1) * LANES] for hh in range(2)]
        vb = v_ref[...]
        zero = jnp.zeros_like(vb)
        vs = [jnp.where(lane < B_VDIM, vb, zero), jnp.where(lane < B_VDIM, zero, vb)]

        def make_step(masked):
            def step(qb, carry):
                start = pl.multiple_of(qb * tq, tq)
                qs = [q_ref[pl.ds(start, tq), hh * LANES:(hh + 1) * LANES] for hh in range(2)]
                do_b = do_ref[pl.ds(start, tq), :]
                ss = [_dot_nt(ks[hh], qs[hh]) for hh in range(2)]
                dps = [_dot_nt(vs[hh], do_b) for hh in range(2)]
                pbs, dss = [], []
                for hh in range(2):
                    s = ss[hh] * B_C2
                    if masked:
                        s = jnp.where(_key_le_query(kblk, qb, tk, tq), s, NEG)
                    p = jnp.exp2(s - lse_ref[hh:hh + 1, pl.ds(start, tq)])
                    dss.append((p * (dps[hh] - dsum_ref[hh:hh + 1, pl.ds(start, tq)]) * B_SCALE).astype(BF16))
                    pbs.append(p.astype(BF16))
                return (carry[0] + _dot(dss[0], qs[0]), carry[1] + _dot(dss[1], qs[1]),
                        carry[2] + _dot(pbs[0], do_b), carry[3] + _dot(pbs[1], do_b))
            return step

        init = (jnp.zeros((tk, LANES), F32),) * 4
        carry = make_step(True)(kblk, init)
        carry = lax.fori_loop(kblk + 1, nq, make_step(False), carry)
        dk_ref[:, :LANES] = carry[0].astype(BF16)
        dk_ref[:, LANES:] = carry[1].astype(BF16)
        dv_ref[...] = jnp.where(lane < B_VDIM, carry[2], carry[3]).astype(BF16)

    pair_tile, pair_full, one_tile, one_full, _ = _b_specs(t, tq, tk)
    row_full = pl.BlockSpec((None, None, 8, t), lambda b, j, i: (b, j, 0, 0))
    return pl.pallas_call(
        body, name="attn_b_dkv", grid=(bsz, B_HEADS // 2, nq),
        in_specs=[pair_full, pair_tile, one_tile, one_full, row_full, row_full],
        out_specs=[pair_tile, one_tile],
        out_shape=[jax.ShapeDtypeStruct((bsz, t, B_KPAD), BF16),
                   jax.ShapeDtypeStruct((bsz, t, B_HEADS * B_VDIM), BF16)],
        compiler_params=_cparams(("parallel", "parallel", "arbitrary")))(q, kpad, v, do, lse2, dsum)


ANY = pl.BlockSpec(memory_space=pl.ANY)


def _all_gather(shard, name):
    def body(x_ref, out_ref, send_sems, recv_sems, local_sem):
        x, y, c = lax.axis_index("x"), lax.axis_index("y"), lax.axis_index("c")
        me, sibling = (x, y, c), (x, y, 1 - c)
        chips = [(1 - x, y), (x, 1 - y), (1 - x, 1 - y)]

        def rows(px, py, pc):
            return out_ref.at[4 * px + 2 * py + pc]

        def copy(k, block, to, src=None):
            return pltpu.make_async_remote_copy(
                src_ref=rows(*block) if src is None else src, dst_ref=rows(*block),
                send_sem=send_sems.at[k], recv_sem=recv_sems.at[k], device_id=to, device_id_type=MESH)

        mine = pltpu.make_async_copy(x_ref, rows(*me), local_sem)
        mine.start()
        first = [copy(0, me, sibling, src=x_ref)]
        first += [copy(1 + j, me, (*chip, c), src=x_ref) for j, chip in enumerate(chips)]
        for cp in first:
            cp.start()
        passed = [copy(4 + j, (*chip, c), sibling) for j, chip in enumerate(chips)]
        for j, chip in enumerate(chips):
            copy(1 + j, (*chip, c), me).wait_recv()
            passed[j].start()
        copy(0, sibling, me).wait_recv()
        for j, chip in enumerate(chips):
            copy(4 + j, (*chip, 1 - c), me).wait_recv()
        for cp in first + passed:
            cp.wait_send()
        mine.wait()

    return pl.pallas_call(
        body, name=name, in_specs=[ANY], out_specs=ANY,
        out_shape=jax.ShapeDtypeStruct((N_DEV,) + shard.shape, shard.dtype),
        scratch_shapes=[pltpu.SemaphoreType.DMA((7,)), pltpu.SemaphoreType.DMA((7,)), pltpu.SemaphoreType.DMA])(shard)


def _exchange_blocks(blocks, name):
    def body(g_ref, out_ref, send_sems, recv_sems, local_sem):
        x, y, c = lax.axis_index("x"), lax.axis_index("y"), lax.axis_index("c")
        me = 4 * x + 2 * y + c

        def peer(k):
            return (1 - x if k & 4 else x, 1 - y if k & 2 else y, 1 - c if k & 1 else c)

        def copy(k):
            px, py, pc = peer(k)
            return pltpu.make_async_remote_copy(
                src_ref=g_ref.at[4 * px + 2 * py + pc], dst_ref=out_ref.at[me],
                send_sem=send_sems.at[k - 1], recv_sem=recv_sems.at[k - 1], device_id=(px, py, pc), device_id_type=MESH)

        def arrival(k):
            px, py, pc = peer(k)
            slot = out_ref.at[4 * px + 2 * py + pc]
            return pltpu.make_async_remote_copy(
                src_ref=slot, dst_ref=slot, send_sem=send_sems.at[k - 1], recv_sem=recv_sems.at[k - 1],
                device_id=(px, py, pc), device_id_type=MESH)

        mine = pltpu.make_async_copy(g_ref.at[me], out_ref.at[me], local_sem)
        mine.start()
        sends = [copy(k) for k in range(1, N_DEV)]
        for cp in sends:
            cp.start()
        for k in range(1, N_DEV):
            arrival(k).wait_recv()
        for cp in sends:
            cp.wait_send()
        mine.wait()

    return pl.pallas_call(
        body, name=name, in_specs=[ANY], out_specs=ANY,
        out_shape=jax.ShapeDtypeStruct(blocks.shape, blocks.dtype),
        scratch_shapes=[pltpu.SemaphoreType.DMA((7,)), pltpu.SemaphoreType.DMA((7,)), pltpu.SemaphoreType.DMA])(blocks)


def _sum_slots(slots, *, tr, name):
    _, r, c = slots.shape

    def body(s_ref, o_ref):
        acc = s_ref[0].astype(F32)
        for s in range(1, N_DEV):
            acc = acc + s_ref[s].astype(F32)
        o_ref[...] = acc

    return pl.pallas_call(
        body, name=name, grid=(r // tr,),
        in_specs=[pl.BlockSpec((N_DEV, tr, c), lambda i: (0, i, 0))],
        out_specs=pl.BlockSpec((tr, c), lambda i: (i, 0)),
        out_shape=jax.ShapeDtypeStruct((r, c), F32),
        compiler_params=_cparams(("parallel",)))(slots)


def _adamw(w, g, m, v, name):
    shape = w.shape
    cols = shape[-1]
    args = [a.reshape(-1, cols) for a in (w, g, m, v)]
    rows = args[0].shape[0]
    tm = 256 if rows % 256 == 0 else rows

    def fn(w_t, g_t, m_t, v_t):
        m_n = ADAM_B1 * m_t + (1.0 - ADAM_B1) * g_t
        v_n = ADAM_B2 * v_t + (1.0 - ADAM_B2) * (g_t * g_t)
        m_hat = m_n / (1.0 - ADAM_B1 ** ADAM_STEP)
        v_hat = v_n / (1.0 - ADAM_B2 ** ADAM_STEP)
        delta = -ADAM_LR * (m_hat / (jnp.sqrt(v_hat) + ADAM_EPS) + ADAM_WD * w_t)
        return delta, m_n, v_n

    delta, m_n, v_n = _rowwise(fn, args, [], [(cols, F32)] * 3, tm=tm, name=name)
    return delta.reshape(shape), m_n.reshape(shape), v_n.reshape(shape)


def _local_step(x, positions, target, w):
    bsz, t, _ = x.shape
    n = bsz * t
    tm = 256
    mm = 512
    x2 = x.reshape(n, D_MODEL)
    tgt2 = target.reshape(n, D_MODEL)
    tabs_a = _rope_tables(positions, A_ROPE_THETA, A_ROT_DIM, 0)
    tabs_b = _rope_tables(positions, B_ROPE_THETA, B_ROPE, B_NOPE)
    ta2 = [a.reshape(n, LANES) for a in tabs_a]
    tb2 = [a.reshape(n, LANES) for a in tabs_b]

    w_a_in = w["a_w_in"]
    w_qkv, w_z = w_a_in[:, :A_QKV], w_a_in[:, A_QKV:]
    w_a_out = w["a_w_out"]
    w_down = w["kv_w_down"]
    w_down_p = jnp.zeros((D_MODEL, 3 * LANES), BF16).at[:, :B_KV_LORA].set(w_down[:, :B_KV_LORA])
    w_down_p = w_down_p.at[:, B_KV_LORA + B_NOPE:B_KV_LORA + B_QK_DIM].set(w_down[:, B_KV_LORA:])
    wu = w["kv_w_up"].reshape(B_KV_LORA, B_HEADS, B_NOPE + B_VDIM)
    w_upk = jnp.pad(wu[:, :, :B_NOPE], ((0, 0), (0, 0), (0, LANES - B_NOPE))).reshape(B_KV_LORA, B_KPAD)
    w_upv = wu[:, :, B_NOPE:].reshape(B_KV_LORA, B_HEADS * B_VDIM)
    w_b_in = w["b_w_in"]
    w_q_p = jnp.pad(w["b_w_q_up"].reshape(B_Q_LORA, B_HEADS, B_QK_DIM),
                    ((0, 0), (0, 0), (0, LANES - B_QK_DIM))).reshape(B_Q_LORA, B_KPAD)
    w_b_out = w["b_w_out"]

    def tab_extras(tabs2, rows):
        return [(a, (rows, LANES), lambda j, i, kk: (i, 0)) for a in tabs2]

    (hn_a,) = _rowwise(lambda xt, g: (_rms(xt, g),), [x2], [w["a_pre_norm"]], [(D_MODEL, BF16)],
                       tm=tm, name="a_pre_norm")

    def qkv_epilogue(acc, o_ref, j, c, sp, sm):
        @pl.when(j == 2)
        def _():
            o_ref[...] = acc.astype(BF16)

        @pl.when(j < 2)
        def _():
            for h in range(A_HEADS):
                hs = slice(h * A_HEAD_DIM, (h + 1) * A_HEAD_DIM)
                o_ref[:, hs] = _rope(acc[:, hs], c, sp, sm).astype(BF16)

    def to_group(a, d):
        if d == 1:
            return a
        return a.reshape(bsz, t // d, d, a.shape[-1]).transpose(0, 2, 1, 3).reshape(n, a.shape[-1])

    def from_group(a, d):
        if d == 1:
            return a
        return a.reshape(bsz, d, t // d, a.shape[-1]).transpose(0, 2, 1, 3).reshape(n, a.shape[-1])

    def rows_to_cols(r, d):
        return r.reshape(bsz, d, A_HEADS, t // d).transpose(0, 3, 1, 2).reshape(n, A_HEADS)

    def cols_to_rows(cc, d):
        return cc.reshape(bsz, t // d, d, A_HEADS).transpose(0, 2, 3, 1).reshape(bsz * d, A_HEADS, t // d)

    z_a = _matmul(hn_a, w_z, out_dtype=F32, tm=mm, tn=A_WIDTH, name="a_gate")
    hn_g, tabs_g, qkv_g, o_g, lse_g = [], [], [], [], []
    for g, d in enumerate(A_DILATIONS):
        hn_g.append(to_group(hn_a, d))
        tabs_g.append([to_group(a, d) for a in ta2])
        w_g = w_qkv[:, g * 3 * A_WIDTH:(g + 1) * 3 * A_WIDTH]
        qkv = _matmul(hn_g[g], w_g, out_dtype=BF16, tm=mm, tn=A_WIDTH, name=f"a_qkv_g{g}",
                      epilogue=qkv_epilogue, extras=tab_extras(tabs_g[g], mm))
        qkv_g.append(qkv.reshape(bsz * d, t // d, 3 * A_WIDTH))
        o, lse = _attn_a_fwd(qkv_g[g], f"attn_a_fwd_g{g}")
        o_g.append(from_group(o.reshape(n, A_WIDTH), d))
        lse_g.append(rows_to_cols(lse, d))

    def merge_fn(o0, o1, o2, l0, l1, l2, z):
        lmax = jnp.maximum(jnp.maximum(l0, l1), l2)
        e0, e1, e2 = jnp.exp2(l0 - lmax), jnp.exp2(l1 - lmax), jnp.exp2(l2 - lmax)
        den = e0 + e1 + e2
        w0, w1, w2 = e0 / den, e1 / den, e2 / den
        parts = []
        for h in range(A_HEADS):
            hs = slice(h * A_HEAD_DIM, (h + 1) * A_HEAD_DIM)
            parts.append(w0[:, h:h + 1] * o0[:, hs] + w1[:, h:h + 1] * o1[:, hs] + w2[:, h:h + 1] * o2[:, hs])
        o = jnp.concatenate(parts, axis=1)
        return o * _silu(z), o, lmax + jnp.log2(den)

    y_a, o_a2, lse_a = _rowwise(merge_fn, [*o_g, *lse_g, z_a], [],
                                [(A_WIDTH, BF16), (A_WIDTH, F32), (A_HEADS, F32)], tm=tm, name="a_merge_gate")
    out_a = _matmul(y_a, w_a_out, out_dtype=F32, tm=mm, tn=D_MODEL, name="a_out")

    def mid_fn(xt, out, g_post, g_kv, g_b):
        h1 = xt + _rms(out, g_post)
        return h1, _rms(h1, g_kv), _rms(h1, g_b)

    h1, hn_kv, hn_b = _rowwise(mid_fn, [x2, out_a], [w["a_post_norm"], w["kv_norm"], w["b_pre_norm"]],
                               [(D_MODEL, F32), (D_MODEL, BF16), (D_MODEL, BF16)], tm=tm, name="a_post_norm")

    ckr = _matmul(hn_kv, w_down_p, out_dtype=F32, tm=mm, tn=3 * LANES, name="kv_down")

    def latent_fn(ck, c, sp, sm, g):
        return _rms(ck[:, :B_KV_LORA], g), _rope(ck[:, B_KV_LORA:], c, sp, sm)

    c_kv, k_rope = _rowwise(latent_fn, [ckr, *tb2], [w["kv_latent_norm"]], [(B_KV_LORA, BF16), (LANES, F32)],
                            tm=tm, name="kv_latent_norm")

    def kpad_epilogue(acc, o_ref, j, kr):
        for h in range(acc.shape[1] // LANES):
            hs = slice(h * LANES, (h + 1) * LANES)
            o_ref[:, hs] = (acc[:, hs] + kr).astype(BF16)

    kpad = _matmul(c_kv, w_upk, out_dtype=BF16, tm=mm, tn=1024, name="kv_up_k", epilogue=kpad_epilogue,
                   extras=[(k_rope, (mm, LANES), lambda j, i, kk: (i, 0))])
    v_b = _matmul(c_kv, w_upv, out_dtype=BF16, tm=mm, tn=1024, name="kv_up_v")

    proj_b = _matmul(hn_b, w_b_in, out_dtype=F32, tm=mm, tn=w_b_in.shape[1], name="b_in")
    (c_q,) = _rowwise(lambda p, g: (_rms(p[:, :B_Q_LORA], g),), [proj_b], [w["b_q_norm"]], [(B_Q_LORA, BF16)],
                      tm=tm, name="b_q_norm")

    def q_epilogue(acc, o_ref, j, c, sp, sm):
        for h in range(acc.shape[1] // LANES):
            hs = slice(h * LANES, (h + 1) * LANES)
            o_ref[:, hs] = _rope(acc[:, hs], c, sp, sm).astype(BF16)

    q_b = _matmul(c_q, w_q_p, out_dtype=BF16, tm=mm, tn=1024, name="b_q_up", epilogue=q_epilogue,
                  extras=tab_extras(tb2, mm))
    q_b3, kpad3, v_b3 = q_b.reshape(bsz, t, B_KPAD), kpad.reshape(bsz, t, B_KPAD), v_b.reshape(bsz, t, -1)
    v4 = v_b3.reshape(bsz, t, B_HEADS, B_VDIM)
    vt1 = jnp.concatenate([v4, jnp.ones((bsz, t, B_HEADS, 1), BF16),
                           jnp.zeros((bsz, t, B_HEADS, LANES - B_VDIM - 1), BF16)], axis=3)
    vt1 = vt1.transpose(0, 2, 3, 1).reshape(bsz, B_KPAD, t)
    o_b, lse_b = _attn_b_fwd(q_b3, kpad3, vt1)
    o_b2 = o_b.reshape(n, -1)
    (y_b,) = _rowwise(lambda o, p: (o * _silu(p[:, B_Q_LORA:]),), [o_b2, proj_b], [], [(D_MODEL, BF16)],
                      tm=tm, name="b_gate_mul")
    out_b = _matmul(y_b, w_b_out, out_dtype=F32, tm=mm, tn=D_MODEL, name="b_out")

    def head_fn(h1t, out, tgt, g):
        e = h1t + _rms(out, g) - tgt
        loss_row = 0.5 * jnp.mean(e * e, axis=-1, keepdims=True)
        dh2 = e * (1.0 / D_MODEL)
        d_out, dg = _rms_bwd(out, g, dh2)
        return dh2, d_out, dg, jnp.broadcast_to(loss_row * (1.0 / LANES), (loss_row.shape[0], LANES))

    dh2, d_out_b, dg_b_post, loss_acc = _rowwise(
        head_fn, [h1, out_b, tgt2], [w["b_post_norm"]], [(D_MODEL, F32), (D_MODEL, BF16)], [D_MODEL, LANES],
        tm=tm, name="loss_head")

    dy_b = _matmul(d_out_b, w_b_out, tb=True, out_dtype=F32, tm=mm, tn=D_MODEL, name="b_out_dx")
    gw_b_out = _matmul(y_b, d_out_b, ta=True, out_dtype=F32, tm=mm, tn=D_MODEL, tk=2048, name="b_out_dw")

    def gate_b_bwd(dy, o, p):
        z = p[:, B_Q_LORA:]
        return dy * _silu(z), dy * o * _silu_grad(z)

    do_b, dz_b = _rowwise(gate_b_bwd, [dy_b, o_b2, proj_b], [], [(D_MODEL, BF16), (D_MODEL, F32)],
                          tm=tm, name="b_gate_bwd")
    do_b3 = do_b.reshape(bsz, t, -1)
    dq_b, dsum_b = _attn_b_dq(q_b3, kpad3, v_b3, do_b3, o_b, lse_b, tabs_b)
    dk_b, dv_b = _attn_b_dkv(q_b3, kpad3, v_b3, do_b3, lse_b, dsum_b)
    dq_b2, dk_b2, dv_b2 = dq_b.reshape(n, B_KPAD), dk_b.reshape(n, B_KPAD), dv_b.reshape(n, -1)

    dc_kv = _matmul(dk_b2, w_upk, tb=True, out_dtype=F32, tm=mm, tn=B_KV_LORA, name="kv_up_k_dx")
    dc_kv = _matmul(dv_b2, w_upv, tb=True, out_dtype=F32, tm=mm, tn=B_KV_LORA, name="kv_up_v_dx",
                    epilogue=_add_epilogue, extras=[(dc_kv, (mm, B_KV_LORA), lambda j, i, kk: (i, j))])
    gw_upk = _matmul(c_kv, dk_b2, ta=True, out_dtype=F32, tm=B_KV_LORA, tn=1024, tk=2048, name="kv_up_k_dw")
    gw_upv = _matmul(c_kv, dv_b2, ta=True, out_dtype=F32, tm=B_KV_LORA, tn=1024, tk=2048, name="kv_up_v_dw")

    def latent_bwd(ck, dck, dk, c, sp, sm, g):
        d1, dg = _rms_bwd(ck[:, :B_KV_LORA], g, dck)
        ksum = dk[:, :LANES].astype(F32)
        for h in range(1, B_HEADS):
            ksum = ksum + dk[:, h * LANES:(h + 1) * LANES].astype(F32)
        lane = lax.broadcasted_iota(jnp.int32, ksum.shape, 1)
        ksum = jnp.where((lane >= B_NOPE) & (lane < B_QK_DIM), ksum, 0.0)
        return jnp.concatenate([d1, _rope_t(ksum, c, sp, sm)], axis=1), dg

    dckr, dg_latent = _rowwise(latent_bwd, [ckr, dc_kv, dk_b2, *tb2], [w["kv_latent_norm"]],
                               [(3 * LANES, BF16)], [B_KV_LORA], tm=tm, name="kv_latent_bwd")
    dhn_kv = _matmul(dckr, w_down_p, tb=True, out_dtype=F32, tm=mm, tn=D_MODEL, name="kv_down_dx")
    gw_down_p = _matmul(hn_kv, dckr, ta=True, out_dtype=F32, tm=mm, tn=3 * LANES, tk=2048, name="kv_down_dw")

    dc_q = _matmul(dq_b2, w_q_p, tb=True, out_dtype=F32, tm=mm, tn=B_Q_LORA, name="b_q_up_dx")
    gw_q_p = _matmul(c_q, dq_b2, ta=True, out_dtype=F32, tm=B_Q_LORA, tn=1024, tk=2048, name="b_q_up_dw")

    def q_norm_bwd(p, dcq, dz, g):
        d1, dg = _rms_bwd(p[:, :B_Q_LORA], g, dcq)
        return jnp.concatenate([d1, dz], axis=1), dg

    dproj_b, dg_q_norm = _rowwise(q_norm_bwd, [proj_b, dc_q, dz_b], [w["b_q_norm"]],
                                  [(w_b_in.shape[1], BF16)], [B_Q_LORA], tm=tm, name="b_q_norm_bwd")
    dhn_b = _matmul(dproj_b, w_b_in, tb=True, out_dtype=F32, tm=mm, tn=D_MODEL, name="b_in_dx")
    gw_b_in = _matmul(hn_b, dproj_b, ta=True, out_dtype=F32, tm=mm, tn=w_b_in.shape[1], tk=2048, name="b_in_dw")

    def mid_bwd(h1t, dh2t, dkv, db, g_kv, g_b, g_post, out):
        dxa, ra = _rms_bwd(h1t, g_kv, dkv)
        dxb, rb = _rms_bwd(h1t, g_b, db)
        dh1 = dh2t + dxa + dxb
        d_out, rp = _rms_bwd(out, g_post, dh1)
        return dh1, d_out, ra, rb, rp

    def mid_bwd_fn(h1t, dh2t, dkv, db, out, g_kv, g_b, g_post):
        return mid_bwd(h1t, dh2t, dkv, db, g_kv, g_b, g_post, out)

    dh1, d_out_a, dg_kv, dg_b_pre, dg_a_post = _rowwise(
        mid_bwd_fn, [h1, dh2, dhn_kv, dhn_b, out_a], [w["kv_norm"], w["b_pre_norm"], w["a_post_norm"]],
        [(D_MODEL, F32), (D_MODEL, BF16)], [D_MODEL] * 3, tm=tm, name="mid_bwd")

    dy_a = _matmul(d_out_a, w_a_out, tb=True, out_dtype=F32, tm=mm, tn=A_WIDTH, name="a_out_dx")
    gw_a_out = _matmul(y_a, d_out_a, ta=True, out_dtype=F32, tm=mm, tn=D_MODEL, tk=2048, name="a_out_dw")

    def gate_a_bwd(dy, o, z):
        do = dy * _silu(z)
        prod = do * o
        lane = lax.broadcasted_iota(jnp.int32, (prod.shape[0], A_HEADS), 1)
        dsum = jnp.zeros((prod.shape[0], A_HEADS), F32)
        for h in range(A_HEADS):
            col = jnp.sum(prod[:, h * A_HEAD_DIM:(h + 1) * A_HEAD_DIM], axis=1, keepdims=True)
            dsum = jnp.where(lane == h, col, dsum)
        return do, dy * o * _silu_grad(z), dsum

    do_a, dz_a, dsum_a = _rowwise(gate_a_bwd, [dy_a, o_a2, z_a], [],
                                  [(A_WIDTH, BF16), (A_WIDTH, BF16), (A_HEADS, F32)], tm=tm, name="a_gate_bwd")
    dhn_a = _matmul(dz_a, w_z, tb=True, out_dtype=F32, tm=mm, tn=D_MODEL, name="a_gate_dx")
    gw_parts = []
    for g, d in enumerate(A_DILATIONS):
        s_n, ln = bsz * d, t // d
        dqkv = _attn_a_bwd(qkv_g[g], to_group(do_a, d).reshape(s_n, ln, A_WIDTH), cols_to_rows(lse_a, d),
                           cols_to_rows(dsum_a, d), [a.reshape(s_n, ln, LANES) for a in tabs_g[g]],
                           f"attn_a_bwd_g{g}").reshape(n, 3 * A_WIDTH)
        w_g = w_qkv[:, g * 3 * A_WIDTH:(g + 1) * 3 * A_WIDTH]
        if d == 1:
            dhn_a = _matmul(dqkv, w_g, tb=True, out_dtype=F32, tm=mm, tn=D_MODEL, name=f"a_qkv_dx_g{g}",
                            epilogue=_add_epilogue, extras=[(dhn_a, (mm, D_MODEL), lambda j, i, kk: (i, j))])
        else:
            dhn_a = dhn_a + from_group(_matmul(dqkv, w_g, tb=True, out_dtype=F32, tm=mm, tn=D_MODEL,
                                               name=f"a_qkv_dx_g{g}"), d)
        gw_parts.append(_matmul(hn_g[g], dqkv, ta=True, out_dtype=F32, tm=mm, tn=1024, tk=2048,
                                name=f"a_qkv_dw_g{g}"))
    gw_parts.append(_matmul(hn_a, dz_a, ta=True, out_dtype=F32, tm=mm, tn=1024, tk=2048, name="a_gate_dw"))
    gw_a_in = jnp.concatenate(gw_parts, axis=1)

    def first_bwd(xt, dhn, dh1t, g):
        dx, dg = _rms_bwd(xt, g, dhn)
        return dh1t + dx, dg

    grad_x, dg_a_pre = _rowwise(first_bwd, [x2, dhn_a, dh1], [w["a_pre_norm"]], [(D_MODEL, F32)], [D_MODEL],
                                tm=tm, name="a_pre_norm_bwd")

    gw_down = jnp.concatenate([gw_down_p[:, :B_KV_LORA], gw_down_p[:, B_KV_LORA + B_NOPE:B_KV_LORA + B_QK_DIM]], axis=1)
    gw_up = jnp.concatenate([gw_upk.reshape(B_KV_LORA, B_HEADS, LANES)[:, :, :B_NOPE],
                             gw_upv.reshape(B_KV_LORA, B_HEADS, B_VDIM)], axis=2).reshape(B_KV_LORA, -1)
    gw_q_up = gw_q_p.reshape(B_Q_LORA, B_HEADS, LANES)[:, :, :B_QK_DIM].reshape(B_Q_LORA, -1)
    grads = {"a_w_in": gw_a_in, "a_w_out": gw_a_out, "kv_w_down": gw_down, "kv_w_up": gw_up,
             "b_w_in": gw_b_in, "b_w_q_up": gw_q_up, "b_w_out": gw_b_out}
    gains = {"a_pre_norm": dg_a_pre, "a_post_norm": dg_a_post, "kv_norm": dg_kv, "kv_latent_norm": dg_latent,
             "b_pre_norm": dg_b_pre, "b_q_norm": dg_q_norm, "b_post_norm": dg_b_post}
    gains = {k: jnp.sum(a, axis=0) for k, a in gains.items()}
    return jnp.sum(loss_acc), grad_x.reshape(bsz, t, D_MODEL), grads, gains


WEIGHT_ORDER = ("a_pre_norm", "a_w_in", "a_w_out", "a_post_norm", "kv_norm", "kv_w_down", "kv_latent_norm",
                "kv_w_up", "b_pre_norm", "b_w_in", "b_q_norm", "b_w_q_up", "b_w_out", "b_post_norm")
MATRICES = (("a_w_in", 1024, 10240, 1), ("a_w_out", 1024, 1024, 0), ("kv_w_down", 1024, 288, 0),
            ("kv_w_up", 256, 2048, 1), ("b_w_in", 1024, 1408, 1), ("b_w_q_up", 384, 1536, 1),
            ("b_w_out", 1024, 1024, 0))
SHARDED_GAINS = ("a_pre_norm", "a_post_norm")
GAIN_WIDTHS = (("a_pre_norm", 1024), ("a_post_norm", 1024), ("kv_norm", 1024), ("kv_latent_norm", 256),
               ("b_pre_norm", 1024), ("b_q_norm", 384), ("b_post_norm", 1024))
GAIN_ROWS = 48


def _shard_rows(rows, cols):
    return rows * cols // (N_DEV * LANES)


def _whole_from_blocks(blocks, rows, cols, axis):
    if axis == 1:
        return blocks.reshape(N_DEV, rows, cols // N_DEV).transpose(1, 0, 2).reshape(rows, cols)
    return blocks.reshape(rows, cols)


def _blocks_from_whole(whole, rows, cols, axis):
    if axis == 1:
        whole = whole.reshape(rows, N_DEV, cols // N_DEV).transpose(1, 0, 2)
    return whole.reshape(N_DEV, -1, LANES)


def kernel(x, positions, a_pre_norm, a_w_in, a_w_out, a_post_norm, kv_norm, kv_w_down, kv_latent_norm, kv_w_up, b_pre_norm, b_w_in, b_q_norm, b_w_q_up, b_w_out, b_post_norm, loss_target, m_a_pre_norm, m_a_w_in, m_a_w_out, m_a_post_norm, m_kv_norm, m_kv_w_down, m_kv_latent_norm, m_kv_w_up, m_b_pre_norm, m_b_w_in, m_b_q_norm, m_b_w_q_up, m_b_w_out, m_b_post_norm, v_a_pre_norm, v_a_w_in, v_a_w_out, v_a_post_norm, v_kv_norm, v_kv_w_down, v_kv_latent_norm, v_kv_w_up, v_b_pre_norm, v_b_w_in, v_b_q_norm, v_b_w_q_up, v_b_w_out, v_b_post_norm):
    weights = dict(a_pre_norm=a_pre_norm, a_w_in=a_w_in, a_w_out=a_w_out, a_post_norm=a_post_norm, kv_norm=kv_norm,
                   kv_w_down=kv_w_down, kv_latent_norm=kv_latent_norm, kv_w_up=kv_w_up, b_pre_norm=b_pre_norm,
                   b_w_in=b_w_in, b_q_norm=b_q_norm, b_w_q_up=b_w_q_up, b_w_out=b_w_out, b_post_norm=b_post_norm)
    m_in = dict(a_pre_norm=m_a_pre_norm, a_w_in=m_a_w_in, a_w_out=m_a_w_out, a_post_norm=m_a_post_norm,
                kv_norm=m_kv_norm, kv_w_down=m_kv_w_down, kv_latent_norm=m_kv_latent_norm, kv_w_up=m_kv_w_up,
                b_pre_norm=m_b_pre_norm, b_w_in=m_b_w_in, b_q_norm=m_b_q_norm, b_w_q_up=m_b_w_q_up,
                b_w_out=m_b_w_out, b_post_norm=m_b_post_norm)
    v_in = dict(a_pre_norm=v_a_pre_norm, a_w_in=v_a_w_in, a_w_out=v_a_w_out, a_post_norm=v_a_post_norm,
                kv_norm=v_kv_norm, kv_w_down=v_kv_w_down, kv_latent_norm=v_kv_latent_norm, kv_w_up=v_kv_w_up,
                b_pre_norm=v_b_pre_norm, b_w_in=v_b_w_in, b_q_norm=v_b_q_norm, b_w_q_up=v_b_w_q_up,
                b_w_out=v_b_w_out, b_post_norm=v_b_post_norm)
    me = 4 * lax.axis_index("x") + 2 * lax.axis_index("y") + lax.axis_index("c")

    flat = jnp.concatenate([weights[name].astype(BF16).reshape(-1, LANES) for name, _, _, _ in MATRICES], axis=0)
    gathered = _all_gather(flat, "gather_weights")
    whole = {}
    off = 0
    for name, rows, cols, axis in MATRICES:
        nr = _shard_rows(rows, cols)
        whole[name] = _whole_from_blocks(gathered[:, off:off + nr], rows, cols, axis)
        off += nr
    gain_shard = jnp.concatenate([weights[name].reshape(1, LANES) for name in SHARDED_GAINS]
                                 + [jnp.zeros((8 - len(SHARDED_GAINS), LANES), F32)], axis=0)
    gain_blocks = _all_gather(gain_shard, "gather_gains")
    for i, name in enumerate(SHARDED_GAINS):
        whole[name] = gain_blocks[:, i, :].reshape(1, D_MODEL)
    for name in ("kv_norm", "kv_latent_norm", "b_pre_norm", "b_q_norm", "b_post_norm"):
        whole[name] = weights[name].reshape(1, -1)

    loss_part, grad_x, grads, gains = _local_step(x, positions, loss_target, whole)

    blocks = jnp.concatenate([_blocks_from_whole(grads[name], rows, cols, axis).astype(BF16)
                              for name, rows, cols, axis in MATRICES], axis=1)
    landed = _exchange_blocks(blocks, "scatter_grads")
    summed = _sum_slots(landed, tr=2512, name="sum_grads")
    grad_out = {}
    off = 0
    for name, rows, cols, axis in MATRICES:
        nr = _shard_rows(rows, cols)
        grad_out[name] = summed[off:off + nr].reshape(weights[name].shape)
        off += nr

    vec = jnp.concatenate([gains[name] for name, _ in GAIN_WIDTHS] + [jnp.full((LANES,), loss_part, F32)])
    vec = jnp.pad(vec, (0, GAIN_ROWS * LANES - vec.shape[0])).reshape(GAIN_ROWS, LANES)
    total = _sum_slots(_all_gather(vec, "gather_gain_grads"), tr=GAIN_ROWS, name="sum_gain_grads").reshape(-1)
    off = 0
    for name, width in GAIN_WIDTHS:
        g = total[off:off + width]
        if name in SHARDED_GAINS:
            g = lax.dynamic_slice(g, (me * LANES,), (LANES,))
        grad_out[name] = g.reshape(weights[name].shape)
        off += width
    loss = total[off]

    deltas, new_m, new_v = {}, {}, {}
    for name in WEIGHT_ORDER:
        deltas[name], new_m[name], new_v[name] = _adamw(weights[name], grad_out[name], m_in[name], v_in[name],
                                                        "adamw_" + name)
    return (loss, grad_x, *[grad_out[k] for k in WEIGHT_ORDER], *[deltas[k] for k in WEIGHT_ORDER],
            *[new_m[k] for k in WEIGHT_ORDER], *[new_v[k] for k in WEIGHT_ORDER])
```

```python
import jax
import jax.numpy as jnp
from jax import lax
from jax.experimental import pallas as pl
from jax.experimental.pallas import tpu as pltpu

F32 = jnp.float32
BF16 = jnp.bfloat16

N_DEV = 8
D_MODEL = 1024
NORM_EPS = 1e-6
A_GROUPS = 3
A_DILATIONS = (1, 4, 16)
A_HEADS = 8
A_HEAD_DIM = 128
A_WIDTH = 1024
A_ROT_DIM = 32
A_ROPE_THETA = 500000.0
A_QKV = A_GROUPS * 3 * A_WIDTH
B_HEADS = 16
B_NOPE = 64
B_ROPE = 32
B_QK_DIM = 96
B_VDIM = 64
B_Q_LORA = 384
B_KV_LORA = 256
B_ROPE_THETA = 10000.0
B_KPAD = B_HEADS * 128
ADAM_LR = 0.001
ADAM_B1 = 0.9
ADAM_B2 = 0.999
ADAM_EPS = 1e-08
ADAM_WD = 0.01
ADAM_STEP = 10

LANES = 128
BAND = 128
EPILOGUE_ROWS = 128
NEG = -1e30
VMEM_LIMIT = 56 * 1024 * 1024
MESH = pl.DeviceIdType.MESH


def _cparams(sem):
    return pltpu.CompilerParams(dimension_semantics=sem, vmem_limit_bytes=VMEM_LIMIT)


def _rowwise(fn, rows, bcast, outs, accs=(), *, tm, name):
    n = rows[0].shape[0]
    nr, nb, no = len(rows), len(bcast), len(outs)

    def body(*refs):
        res = fn(*[r[...] for r in refs[:nr + nb]])
        out_refs = refs[nr + nb:nr + nb + no]
        acc_refs = refs[nr + nb + no:]
        for r, v in zip(out_refs, res[:no]):
            r[...] = v.astype(r.dtype)
        if acc_refs:
            @pl.when(pl.program_id(0) == 0)
            def _():
                for r in acc_refs:
                    r[...] = jnp.zeros_like(r)
            for r, v in zip(acc_refs, res[no:]):
                r[...] += v.reshape(tm // 8, 8, v.shape[-1]).sum(axis=0)

    in_specs = [pl.BlockSpec((tm, a.shape[1]), lambda i: (i, 0)) for a in rows]
    in_specs += [pl.BlockSpec(a.shape, lambda i: (0, 0)) for a in bcast]
    out_specs = [pl.BlockSpec((tm, c), lambda i: (i, 0)) for c, _ in outs]
    out_specs += [pl.BlockSpec((8, c), lambda i: (0, 0)) for c in accs]
    out_shape = [jax.ShapeDtypeStruct((n, c), dt) for c, dt in outs]
    out_shape += [jax.ShapeDtypeStruct((8, c), F32) for c in accs]
    return pl.pallas_call(
        body, name=name, grid=(n // tm,), in_specs=in_specs, out_specs=out_specs, out_shape=out_shape,
        compiler_params=_cparams(("arbitrary",)))(*rows, *bcast)


def _matmul(a, b, *, out_dtype, tm, tn, tk=None, name, epilogue=None, extras=(), ta=False, tb=False):
    k, m = a.shape[::-1] if not ta else a.shape
    n = b.shape[0] if tb else b.shape[1]
    tk = tk or k
    nk = k // tk
    ne = len(extras)
    dot = _dot_tn if ta else (_dot_nt if tb else _dot)
    assert epilogue is None or (nk == 1 and not ta)

    def body(*refs):
        a_ref, b_ref = refs[:2]
        ex = refs[2:2 + ne]
        o_ref = refs[2 + ne]
        if epilogue is not None:
            b_tile = b_ref[...].astype(BF16)
            for r0 in range(0, tm, EPILOGUE_ROWS):
                rows = slice(r0, r0 + EPILOGUE_ROWS)
                epilogue(dot(a_ref[rows, :].astype(BF16), b_tile), o_ref, rows, *ex)
            return
        part = dot(a_ref[...].astype(BF16), b_ref[...].astype(BF16))
        if nk == 1:
            o_ref[...] = part.astype(o_ref.dtype)
        else:
            acc_ref = refs[-1]
            kk = pl.program_id(2)

            @pl.when(kk == 0)
            def _():
                acc_ref[...] = part

            @pl.when(kk > 0)
            def _():
                acc_ref[...] += part

            @pl.when(kk == nk - 1)
            def _():
                o_ref[...] = acc_ref[...].astype(o_ref.dtype)

    a_spec = pl.BlockSpec((tk, tm), lambda j, i, kk: (kk, i)) if ta else pl.BlockSpec((tm, tk), lambda j, i, kk: (i, kk))
    b_spec = pl.BlockSpec((tn, tk), lambda j, i, kk: (j, kk)) if tb else pl.BlockSpec((tk, tn), lambda j, i, kk: (kk, j))
    in_specs = [a_spec, b_spec] + [pl.BlockSpec(bs, im) for _, bs, im in extras]
    return pl.pallas_call(
        body, name=name, grid=(n // tn, m // tm, nk), in_specs=in_specs,
        out_specs=pl.BlockSpec((tm, tn), lambda j, i, kk: (i, j)),
        out_shape=jax.ShapeDtypeStruct((m, n), out_dtype),
        scratch_shapes=[pltpu.VMEM((tm, tn), F32)] if nk > 1 else [],
        compiler_params=_cparams(("parallel", "parallel", "arbitrary")))(a, b, *[e[0] for e in extras])


def _add_epilogue(acc, o_ref, rows, prev_ref):
    o_ref[rows, :] = (acc + prev_ref[rows, :]).astype(o_ref.dtype)


def _rope(x, c, sp, sm):
    return x * c + pltpu.roll(x, 16, 1) * sp + pltpu.roll(x, LANES - 16, 1) * sm


def _rope_t(dy, c, sp, sm):
    return dy * c + pltpu.roll(dy * sp, LANES - 16, 1) + pltpu.roll(dy * sm, 16, 1)


def _rope_tables(positions, theta, rot_dim, lane0):
    half = rot_dim // 2
    inv_freq = 1.0 / (theta ** (jnp.arange(half, dtype=F32) * (2.0 / rot_dim)))
    ang = positions.astype(F32)[..., None] * inv_freq
    cos, sin = jnp.cos(ang), jnp.sin(ang)
    def lanes(first, second, fill):
        pad = lambda width: jnp.full(positions.shape + (width,), fill, F32)
        return jnp.concatenate([pad(lane0), first, second, pad(LANES - lane0 - rot_dim)], axis=-1)

    zero = jnp.zeros_like(sin)
    return lanes(cos, cos, 1.0), lanes(zero, sin, 0.0), lanes(-sin, zero, 0.0)


def _rms(x, g):
    xf = x.astype(F32)
    return xf * lax.rsqrt(jnp.mean(xf * xf, axis=-1, keepdims=True) + NORM_EPS) * g


def _rms_bwd(x, g, dy):
    xf = x.astype(F32)
    rstd = lax.rsqrt(jnp.mean(xf * xf, axis=-1, keepdims=True) + NORM_EPS)
    xhat = xf * rstd
    dxhat = dy * g
    dx = rstd * (dxhat - xhat * jnp.mean(dxhat * xhat, axis=-1, keepdims=True))
    return dx, dy * xhat


def _silu(z):
    return z * jax.nn.sigmoid(z)


def _silu_grad(z):
    s = jax.nn.sigmoid(z)
    return s * (1.0 + z * (1.0 - s))


def _dot_nt(a, b):
    return lax.dot_general(a, b, (((1,), (1,)), ((), ())), preferred_element_type=F32)


def _dot_tn(a, b):
    return lax.dot_general(a, b, (((0,), (0,)), ((), ())), preferred_element_type=F32)


def _dot(a, b):
    return jnp.dot(a, b, preferred_element_type=F32)


A_SCALE = A_HEAD_DIM ** -0.5
LOG2E = 1.4426950408889634
A_C2 = A_SCALE * LOG2E
A_HEAD_GROUP = 4


def _attn_a_fwd(qk, v, name):
    s_n, ln, _ = qk.shape
    nb = ln // BAND
    blk = (None, BAND, A_WIDTH)

    def body(q_ref, kc_ref, kp_ref, vc_ref, vp_ref, o_ref, lse_ref):
        kpos = lax.broadcasted_iota(jnp.int32, (2 * BAND, BAND), 0)
        qpos = lax.broadcasted_iota(jnp.int32, (2 * BAND, BAND), 1) + BAND
        first_key = jnp.where(pl.program_id(1) > 0, 0, BAND)
        mask = (kpos <= qpos) & (kpos >= qpos - BAND) & (kpos >= first_key)
        rows = []
        for h0 in range(0, A_HEADS, A_HEAD_GROUP):
            hss = [slice(h * A_HEAD_DIM, (h + 1) * A_HEAD_DIM) for h in range(h0, h0 + A_HEAD_GROUP)]
            sts = [_dot_nt(jnp.concatenate([kp_ref[:, hs], kc_ref[:, hs]], axis=0), q_ref[:, hs]) for hs in hss]
            ps, ls = [], []
            for st in sts:
                st = jnp.where(mask, st * A_C2, NEG)
                m = jnp.max(st, axis=0, keepdims=True)
                p = jnp.exp2(st - m)
                l_row = jnp.sum(p, axis=0, keepdims=True)
                ps.append(p.astype(BF16))
                ls.append(l_row)
                rows.append(m + jnp.log2(l_row))
            ots = [_dot_tn(jnp.concatenate([vp_ref[:, hs], vc_ref[:, hs]], axis=0), p) for hs, p in zip(hss, ps)]
            for hs, o_t, l_row in zip(hss, ots, ls):
                o_ref[:, hs] = (o_t / l_row).T.astype(BF16)
        lse_ref[...] = jnp.concatenate(rows, axis=0)

    def col(c, off):
        return lambda s, l: (s, jnp.maximum(l + off, 0), c)

    return pl.pallas_call(
        body, name=name, grid=(s_n, nb),
        in_specs=[pl.BlockSpec(blk, col(0, 0)), pl.BlockSpec(blk, col(1, 0)), pl.BlockSpec(blk, col(1, -1)),
                  pl.BlockSpec(blk, col(0, 0)), pl.BlockSpec(blk, col(0, -1))],
        out_specs=[pl.BlockSpec(blk, lambda s, l: (s, l, 0)),
                   pl.BlockSpec((None, A_HEADS, BAND), lambda s, l: (s, 0, l))],
        out_shape=[jax.ShapeDtypeStruct((s_n, ln, A_WIDTH), BF16), jax.ShapeDtypeStruct((s_n, A_HEADS, ln), F32)],
        compiler_params=_cparams(("parallel", "arbitrary")))(qk, qk, qk, v, v)


def _attn_a_bwd(qk, v, do, lse2, dsum, tabs, name):
    s_n, ln, _ = qk.shape
    nb = ln // BAND
    blk = (None, BAND, A_WIDTH)

    def body(q_ref, qn_ref, kc_ref, kp_ref, vc_ref, vp_ref, do_ref, don_ref, lse_ref, lsen_ref, ds_ref, dsn_ref,
             c_ref, sp_ref, sm_ref, out_ref):
        l_idx = pl.program_id(1)
        kpos = lax.broadcasted_iota(jnp.int32, (2 * BAND, BAND), 0)
        qpos = lax.broadcasted_iota(jnp.int32, (2 * BAND, BAND), 1) + BAND
        first_key = jnp.where(l_idx > 0, 0, BAND)
        mask_q = (kpos <= qpos) & (kpos >= qpos - BAND) & (kpos >= first_key)
        kpos2 = lax.broadcasted_iota(jnp.int32, (BAND, 2 * BAND), 0)
        qpos2 = lax.broadcasted_iota(jnp.int32, (BAND, 2 * BAND), 1)
        last_query = jnp.where(l_idx < nb - 1, 2 * BAND, BAND)
        mask_k = (kpos2 <= qpos2) & (kpos2 >= qpos2 - BAND) & (qpos2 < last_query)
        c, sp, sm = c_ref[...], sp_ref[...], sm_ref[...]
        lse_q, ds_q = lse_ref[...], ds_ref[...]
        lse_k = jnp.concatenate([lse_q, lsen_ref[...]], axis=1)
        ds_k = jnp.concatenate([ds_q, dsn_ref[...]], axis=1)
        for h0 in range(0, A_HEADS, A_HEAD_GROUP):
            hl = list(range(h0, h0 + A_HEAD_GROUP))
            hss = [slice(h * A_HEAD_DIM, (h + 1) * A_HEAD_DIM) for h in hl]
            k2s = [jnp.concatenate([kp_ref[:, hs], kc_ref[:, hs]], axis=0) for hs in hss]
            v2s = [jnp.concatenate([vp_ref[:, hs], vc_ref[:, hs]], axis=0) for hs in hss]
            q2s = [jnp.concatenate([q_ref[:, hs], qn_ref[:, hs]], axis=0) for hs in hss]
            do2s = [jnp.concatenate([do_ref[:, hs], don_ref[:, hs]], axis=0) for hs in hss]
            sts = [_dot_nt(k2, q_ref[:, hs]) for k2, hs in zip(k2s, hss)]
            dpts = [_dot_nt(v2, do_ref[:, hs]) for v2, hs in zip(v2s, hss)]
            st2s = [_dot_nt(kc_ref[:, hs], q2) for q2, hs in zip(q2s, hss)]
            dpt2s = [_dot_nt(vc_ref[:, hs], do2) for do2, hs in zip(do2s, hss)]
            dsts, dst2s, p2s = [], [], []
            for i, h in enumerate(hl):
                p = jnp.exp2(jnp.where(mask_q, sts[i] * A_C2, NEG) - lse_q[h:h + 1])
                dsts.append((p * (dpts[i] - ds_q[h:h + 1]) * A_SCALE).astype(BF16))
                p2 = jnp.exp2(jnp.where(mask_k, st2s[i] * A_C2, NEG) - lse_k[h:h + 1])
                dst2s.append((p2 * (dpt2s[i] - ds_k[h:h + 1]) * A_SCALE).astype(BF16))
                p2s.append(p2.astype(BF16))
            dqs = [_dot_tn(dsts[i], k2s[i]) for i in range(A_HEAD_GROUP)]
            dks = [_dot(dst2s[i], q2s[i]) for i in range(A_HEAD_GROUP)]
            dvs = [_dot(p2s[i], do2s[i]) for i in range(A_HEAD_GROUP)]
            for i, h in enumerate(hl):
                out_ref[:, hss[i]] = _rope_t(dqs[i], c, sp, sm).astype(BF16)
                out_ref[:, A_WIDTH + h * A_HEAD_DIM:A_WIDTH + (h + 1) * A_HEAD_DIM] = _rope_t(dks[i], c, sp, sm).astype(BF16)
                out_ref[:, 2 * A_WIDTH + h * A_HEAD_DIM:2 * A_WIDTH + (h + 1) * A_HEAD_DIM] = dvs[i].astype(BF16)

    def col(c, off):
        return lambda s, l: (s, jnp.clip(l + off, 0, nb - 1), c)

    def row(off):
        return pl.BlockSpec((None, A_HEADS, BAND), lambda s, l: (s, 0, jnp.clip(l + off, 0, nb - 1)))

    tspec = pl.BlockSpec((None, BAND, LANES), lambda s, l: (s, l, 0))
    in_specs = [pl.BlockSpec(blk, col(0, 0)), pl.BlockSpec(blk, col(0, 1)),
                pl.BlockSpec(blk, col(1, 0)), pl.BlockSpec(blk, col(1, -1)),
                pl.BlockSpec(blk, col(0, 0)), pl.BlockSpec(blk, col(0, -1)),
                pl.BlockSpec(blk, col(0, 0)), pl.BlockSpec(blk, col(0, 1)),
                row(0), row(1), row(0), row(1), tspec, tspec, tspec]
    return pl.pallas_call(
        body, name=name, grid=(s_n, nb), in_specs=in_specs,
        out_specs=pl.BlockSpec((None, BAND, 3 * A_WIDTH), lambda s, l: (s, l, 0)),
        out_shape=jax.ShapeDtypeStruct((s_n, ln, 3 * A_WIDTH), BF16),
        compiler_params=_cparams(("parallel", "arbitrary")))(
            qk, qk, qk, qk, v, v, do, do, lse2, lse2, dsum, dsum, *tabs)


B_TQ = 256
B_SCALE = B_QK_DIM ** -0.5
B_C2 = B_SCALE * LOG2E


def _b_row_tile(tq):
    return pl.BlockSpec((None, None, 8, tq), lambda b, j, i: (b, j, 0, i))


def _b_specs(t, tq, tk):
    pair_tile = pl.BlockSpec((None, tq, 2 * LANES), lambda b, j, i: (b, i, j))
    pair_full = pl.BlockSpec((None, t, 2 * LANES), lambda b, j, i: (b, 0, j))
    one_tile = pl.BlockSpec((None, tk, LANES), lambda b, j, i: (b, i, j))
    one_full = pl.BlockSpec((None, t, LANES), lambda b, j, i: (b, 0, j))
    tab_tile = pl.BlockSpec((None, tq, LANES), lambda b, j, i: (b, i, 0))
    return pair_tile, pair_full, one_tile, one_full, tab_tile


def _key_le_query(kb, qb, tk, tq):
    kpos = kb * tk + lax.broadcasted_iota(jnp.int32, (tk, tq), 0)
    qpos = qb * tq + lax.broadcasted_iota(jnp.int32, (tk, tq), 1)
    return kpos <= qpos


def _attn_b_fwd(q, kpad, vt1):
    bsz, t, _ = q.shape
    tq = tk = B_TQ
    nq = t // tq

    def body(q_ref, k_ref, vt_ref, o_ref, lse_ref):
        qblk = pl.program_id(2)
        qs = [q_ref[:, hh * LANES:(hh + 1) * LANES] for hh in range(2)]

        def scores(kb):
            start = pl.multiple_of(kb * tk, tk)
            return [_dot_nt(k_ref[pl.ds(start, tk), hh * LANES:(hh + 1) * LANES], qs[hh]) for hh in range(2)]

        def pv(kb, ps):
            start = pl.multiple_of(kb * tk, tk)
            return [_dot(vt_ref[hh * LANES:(hh + 1) * LANES, pl.ds(start, tk)], ps[hh]) for hh in range(2)]

        def softmax(ss, ms, accs, kb, masked):
            out_m, out_acc, out_p = [], [], []
            for hh in range(2):
                s = ss[hh] * B_C2
                if masked:
                    s = jnp.where(_key_le_query(kb, qblk, tk, tq), s, NEG)
                m_new = jnp.maximum(ms[hh], jnp.max(s, axis=0, keepdims=True))
                out_acc.append(jnp.exp2(ms[hh] - m_new) * accs[hh])
                out_p.append(jnp.exp2(s - m_new).astype(BF16))
                out_m.append(m_new)
            return out_m, out_acc, out_p

        def step(kb, carry):
            ss, ps, ms, accs = carry
            pvs = pv(jnp.maximum(kb - 1, 0), ps)
            ss_next = scores(kb + 1)
            accs = [accs[hh] + pvs[hh] for hh in range(2)]
            ms, accs, ps = softmax(ss, ms, accs, kb, False)
            return (ss_next, ps, ms, accs)

        init = (scores(0), [jnp.zeros((tk, tq), BF16)] * 2, [jnp.full((1, tq), NEG, F32)] * 2,
                [jnp.zeros((LANES, tq), F32)] * 2)
        ss, ps, ms, accs = lax.fori_loop(0, qblk, step, init)
        pvs = pv(jnp.maximum(qblk - 1, 0), ps)
        accs = [accs[hh] + pvs[hh] for hh in range(2)]
        ms, accs, ps = softmax(ss, ms, accs, qblk, True)
        pvs = pv(qblk, ps)
        accs = [accs[hh] + pvs[hh] for hh in range(2)]
        ls = [accs[hh][B_VDIM:B_VDIM + 1] for hh in range(2)]
        o_t = jnp.concatenate([accs[0][:B_VDIM] / ls[0], accs[1][:B_VDIM] / ls[1]], axis=0)
        o_ref[...] = o_t.T
        lse_ref[...] = jnp.concatenate([ms[0] + jnp.log2(ls[0]), ms[1] + jnp.log2(ls[1]),
                                        jnp.zeros((6, tq), F32)], axis=0)

    pair_tile, pair_full, one_tile, _, _ = _b_specs(t, tq, tk)
    vt_spec = pl.BlockSpec((None, 2 * LANES, t), lambda b, j, i: (b, j, 0))
    return pl.pallas_call(
        body, name="attn_b_fwd", grid=(bsz, B_HEADS // 2, nq),
        in_specs=[pair_tile, pair_full, vt_spec],
        out_specs=[one_tile, _b_row_tile(tq)],
        out_shape=[jax.ShapeDtypeStruct((bsz, t, B_HEADS * B_VDIM), F32),
                   jax.ShapeDtypeStruct((bsz, B_HEADS // 2, 8, t), F32)],
        compiler_params=_cparams(("parallel", "parallel", "arbitrary")))(q, kpad, vt1)


def _attn_b_dq(q, kpad, v, do, o, lse2, tabs):
    bsz, t, _ = q.shape
    tq = tk = B_TQ
    nq = t // tq

    def body(q_ref, k_ref, v_ref, do_ref, o_ref, lse_ref, c_ref, sp_ref, sm_ref, dq_ref, dsum_ref):
        qblk = pl.program_id(2)
        lane = lax.broadcasted_iota(jnp.int32, (tq, LANES), 1)
        do_b = do_ref[...]
        prod = do_b.astype(F32) * o_ref[...]
        hi = prod.astype(BF16)
        lo = (prod - hi.astype(F32)).astype(BF16)
        sel_lane = lax.broadcasted_iota(jnp.int32, (8, LANES), 1)
        sel_row = lax.broadcasted_iota(jnp.int32, (8, LANES), 0)
        sel = jnp.where((sel_lane < B_VDIM) == (sel_row == 0), 1.0, 0.0)
        sel = jnp.where(sel_row < 2, sel, 0.0).astype(BF16)
        dsum = _dot_nt(sel, hi) + _dot_nt(sel, lo)
        dsum_ref[...] = dsum
        lse = lse_ref[...]
        qs = [q_ref[:, hh * LANES:(hh + 1) * LANES] for hh in range(2)]
        zero = jnp.zeros_like(do_b)
        dos = [jnp.where(lane < B_VDIM, do_b, zero), jnp.where(lane < B_VDIM, zero, do_b)]

        def kblock(kb, hh):
            start = pl.multiple_of(kb * tk, tk)
            return k_ref[pl.ds(start, tk), hh * LANES:(hh + 1) * LANES]

        def dots(kb):
            start = pl.multiple_of(kb * tk, tk)
            vb = v_ref[pl.ds(start, tk), :]
            return ([_dot_nt(kblock(kb, hh), qs[hh]) for hh in range(2)], [_dot_nt(vb, dos[hh]) for hh in range(2)])

        def ds_of(ss, dps, kb, masked):
            out = []
            for hh in range(2):
                s = ss[hh] * B_C2
                if masked:
                    s = jnp.where(_key_le_query(kb, qblk, tk, tq), s, NEG)
                p = jnp.exp2(s - lse[hh:hh + 1])
                out.append((p * (dps[hh] - dsum[hh:hh + 1]) * B_SCALE).astype(BF16))
            return out

        def accum(acc, kb, dss):
            return [acc[hh] + _dot_tn(kblock(kb, hh), dss[hh]) for hh in range(2)]

        def step(kb, carry):
            dss, acc = carry
            ss, dps = dots(kb)
            acc = accum(acc, jnp.maximum(kb - 1, 0), dss)
            return (ds_of(ss, dps, kb, False), acc)

        init = ([jnp.zeros((tk, tq), BF16)] * 2, [jnp.zeros((LANES, tq), F32)] * 2)
        dss, acc = lax.fori_loop(0, qblk, step, init)
        ss, dps = dots(qblk)
        acc = accum(acc, jnp.maximum(qblk - 1, 0), dss)
        acc = accum(acc, qblk, ds_of(ss, dps, qblk, True))
        c, sp, sm = c_ref[...], sp_ref[...], sm_ref[...]
        for hh in range(2):
            dq_ref[:, hh * LANES:(hh + 1) * LANES] = _rope_t(acc[hh].T, c, sp, sm).astype(BF16)

    pair_tile, pair_full, one_tile, one_full, tab_tile = _b_specs(t, tq, tk)
    row_tile = _b_row_tile(tq)
    return pl.pallas_call(
        body, name="attn_b_dq", grid=(bsz, B_HEADS // 2, nq),
        in_specs=[pair_tile, pair_full, one_full, one_tile, one_tile, row_tile, tab_tile, tab_tile, tab_tile],
        out_specs=[pair_tile, row_tile],
        out_shape=[jax.ShapeDtypeStruct((bsz, t, B_KPAD), BF16),
                   jax.ShapeDtypeStruct((bsz, B_HEADS // 2, 8, t), F32)],
        compiler_params=_cparams(("parallel", "parallel", "arbitrary")))(q, kpad, v, do, o, lse2, *tabs)


def _attn_b_dkv(q, kpad, v, do, lse2, dsum):
    bsz, t, _ = q.shape
    tq = tk = B_TQ
    nq = t // tq

    def body(q_ref, k_ref, v_ref, do_ref, lse_ref, dsum_ref, dk_ref, dv_ref):
        kblk = pl.program_id(2)
        lane = lax.broadcasted_iota(jnp.int32, (tk, LANES), 1)
        ks = [k_ref[:, hh * LANES:(hh + 1) * LANES] for hh in range(2)]
        vb = v_ref[...]
        zero = jnp.zeros_like(vb)
        vs = [jnp.where(lane < B_VDIM, vb, zero), jnp.where(lane < B_VDIM, zero, vb)]

        def make_step(masked):
            def step(qb, carry):
                start = pl.multiple_of(qb * tq, tq)
                qs = [q_ref[pl.ds(start, tq), hh * LANES:(hh + 1) * LANES] for hh in range(2)]
                do_b = do_ref[pl.ds(start, tq), :]
                ss = [_dot_nt(ks[hh], qs[hh]) for hh in range(2)]
                dps = [_dot_nt(vs[hh], do_b) for hh in range(2)]
                pbs, dss = [], []
                for hh in range(2):
                    s = ss[hh] * B_C2
                    if masked:
                        s = jnp.where(_key_le_query(kblk, qb, tk, tq), s, NEG)
                    p = jnp.exp2(s - lse_ref[hh:hh + 1, pl.ds(start, tq)])
                    dss.append((p * (dps[hh] - dsum_ref[hh:hh + 1, pl.ds(start, tq)]) * B_SCALE).astype(BF16))
                    pbs.append(p.astype(BF16))
                return (carry[0] + _dot(dss[0], qs[0]), carry[1] + _dot(dss[1], qs[1]),
                        carry[2] + _dot(pbs[0], do_b), carry[3] + _dot(pbs[1], do_b))
            return step

        init = (jnp.zeros((tk, LANES), F32),) * 4
        carry = make_step(True)(kblk, init)
        carry = lax.fori_loop(kblk + 1, nq, make_step(False), carry)
        dk_ref[:, :LANES] = carry[0].astype(BF16)
        dk_ref[:, LANES:] = carry[1].astype(BF16)
        dv_ref[...] = jnp.where(lane < B_VDIM, carry[2], carry[3]).astype(BF16)

    pair_tile, pair_full, one_tile, one_full, _ = _b_specs(t, tq, tk)
    row_full = pl.BlockSpec((None, None, 8, t), lambda b, j, i: (b, j, 0, 0))
    return pl.pallas_call(
        body, name="attn_b_dkv", grid=(bsz, B_HEADS // 2, nq),
        in_specs=[pair_full, pair_tile, one_tile, one_full, row_full, row_full],
        out_specs=[pair_tile, one_tile],
        out_shape=[jax.ShapeDtypeStruct((bsz, t, B_KPAD), BF16),
                   jax.ShapeDtypeStruct((bsz, t, B_HEADS * B_VDIM), BF16)],
        compiler_params=_cparams(("parallel", "parallel", "arbitrary")))(q, kpad, v, do, lse2, dsum)


ANY = pl.BlockSpec(memory_space=pl.ANY)


def _all_gather(shard, name):
    def body(x_ref, out_ref, send_sems, recv_sems, local_sem):
        x, y, c = lax.axis_index("x"), lax.axis_index("y"), lax.axis_index("c")
        me, sibling = (x, y, c), (x, y, 1 - c)
        chips = [(1 - x, y), (x, 1 - y), (1 - x, 1 - y)]

        def rows(px, py, pc):
            return out_ref.at[4 * px + 2 * py + pc]

        def copy(k, block, to, src=None):
            return pltpu.make_async_remote_copy(
                src_ref=rows(*block) if src is None else src, dst_ref=rows(*block),
                send_sem=send_sems.at[k], recv_sem=recv_sems.at[k], device_id=to, device_id_type=MESH)

        mine = pltpu.make_async_copy(x_ref, rows(*me), local_sem)
        mine.start()
        first = [copy(0, me, sibling, src=x_ref)]
        first += [copy(1 + j, me, (*chip, c), src=x_ref) for j, chip in enumerate(chips)]
        for cp in first:
            cp.start()
        passed = [copy(4 + j, (*chip, c), sibling) for j, chip in enumerate(chips)]
        for j, chip in enumerate(chips):
            copy(1 + j, (*chip, c), me).wait_recv()
            passed[j].start()
        copy(0, sibling, me).wait_recv()
        for j, chip in enumerate(chips):
            copy(4 + j, (*chip, 1 - c), me).wait_recv()
        for cp in first + passed:
            cp.wait_send()
        mine.wait()

    return pl.pallas_call(
        body, name=name, in_specs=[ANY], out_specs=ANY,
        out_shape=jax.ShapeDtypeStruct((N_DEV,) + shard.shape, shard.dtype),
        scratch_shapes=[pltpu.SemaphoreType.DMA((7,)), pltpu.SemaphoreType.DMA((7,)), pltpu.SemaphoreType.DMA])(shard)


N_CHIPS = 4


def _exchange_sibling(blocks, name):
    def body(g_ref, kept_ref, got_ref, send_sems, recv_sems, local_sems):
        x, y, c = lax.axis_index("x"), lax.axis_index("y"), lax.axis_index("c")
        sibling = (x, y, 1 - c)
        keeps = [pltpu.make_async_copy(g_ref.at[2 * q + c], kept_ref.at[q], local_sems.at[q]) for q in range(N_CHIPS)]
        sends = [pltpu.make_async_remote_copy(
            src_ref=g_ref.at[2 * q + (1 - c)], dst_ref=got_ref.at[q], send_sem=send_sems.at[q],
            recv_sem=recv_sems.at[q], device_id=sibling, device_id_type=MESH) for q in range(N_CHIPS)]
        for cp in keeps + sends:
            cp.start()
        for cp in sends:
            cp.wait_recv()
        for cp in sends:
            cp.wait_send()
        for cp in keeps:
            cp.wait()

    half = jax.ShapeDtypeStruct((N_CHIPS,) + blocks.shape[1:], blocks.dtype)
    return pl.pallas_call(
        body, name=name, in_specs=[ANY], out_specs=[ANY, ANY], out_shape=[half, half],
        scratch_shapes=[pltpu.SemaphoreType.DMA((N_CHIPS,)), pltpu.SemaphoreType.DMA((N_CHIPS,)),
                        pltpu.SemaphoreType.DMA((N_CHIPS,))])(blocks)


def _exchange_chips(parts, name):
    def body(p_ref, out_ref, send_sems, recv_sems, local_sem):
        x, y, c = lax.axis_index("x"), lax.axis_index("y"), lax.axis_index("c")
        me = 2 * x + y

        def peer(k):
            return (1 - x if k & 2 else x, 1 - y if k & 1 else y)

        def copy(k):
            px, py = peer(k)
            return pltpu.make_async_remote_copy(
                src_ref=p_ref.at[2 * px + py], dst_ref=out_ref.at[me], send_sem=send_sems.at[k - 1],
                recv_sem=recv_sems.at[k - 1], device_id=(px, py, c), device_id_type=MESH)

        def arrival(k):
            px, py = peer(k)
            slot = out_ref.at[2 * px + py]
            return pltpu.make_async_remote_copy(
                src_ref=slot, dst_ref=slot, send_sem=send_sems.at[k - 1], recv_sem=recv_sems.at[k - 1],
                device_id=(px, py, c), device_id_type=MESH)

        mine = pltpu.make_async_copy(p_ref.at[me], out_ref.at[me], local_sem)
        mine.start()
        sends = [copy(k) for k in range(1, N_CHIPS)]
        for cp in sends:
            cp.start()
        for k in range(1, N_CHIPS):
            arrival(k).wait_recv()
        for cp in sends:
            cp.wait_send()
        mine.wait()

    return pl.pallas_call(
        body, name=name, in_specs=[ANY], out_specs=ANY,
        out_shape=jax.ShapeDtypeStruct(parts.shape, parts.dtype),
        scratch_shapes=[pltpu.SemaphoreType.DMA((N_CHIPS - 1,)), pltpu.SemaphoreType.DMA((N_CHIPS - 1,)),
                        pltpu.SemaphoreType.DMA])(parts)


def _add_pairs(a, b, *, tr, name):
    q, r, c = a.shape

    def body(a_ref, b_ref, o_ref):
        o_ref[...] = (a_ref[...].astype(F32) + b_ref[...].astype(F32)).astype(o_ref.dtype)

    spec = pl.BlockSpec((q, tr, c), lambda i: (0, i, 0))
    return pl.pallas_call(
        body, name=name, grid=(r // tr,), in_specs=[spec, spec], out_specs=spec,
        out_shape=jax.ShapeDtypeStruct(a.shape, BF16), compiler_params=_cparams(("parallel",)))(a, b)


def _sum_slots(slots, *, tr, name):
    n_slots, r, c = slots.shape

    def body(s_ref, o_ref):
        acc = s_ref[0].astype(F32)
        for s in range(1, n_slots):
            acc = acc + s_ref[s].astype(F32)
        o_ref[...] = acc

    return pl.pallas_call(
        body, name=name, grid=(r // tr,),
        in_specs=[pl.BlockSpec((n_slots, tr, c), lambda i: (0, i, 0))],
        out_specs=pl.BlockSpec((tr, c), lambda i: (i, 0)),
        out_shape=jax.ShapeDtypeStruct((r, c), F32),
        compiler_params=_cparams(("parallel",)))(slots)


def _adamw(w, g, m, v, name):
    shape = w.shape
    cols = shape[-1]
    args = [a.reshape(-1, cols) for a in (w, g, m, v)]
    rows = args[0].shape[0]
    tm = 256 if rows % 256 == 0 else rows

    def fn(w_t, g_t, m_t, v_t):
        m_n = ADAM_B1 * m_t + (1.0 - ADAM_B1) * g_t
        v_n = ADAM_B2 * v_t + (1.0 - ADAM_B2) * (g_t * g_t)
        m_hat = m_n / (1.0 - ADAM_B1 ** ADAM_STEP)
        v_hat = v_n / (1.0 - ADAM_B2 ** ADAM_STEP)
        delta = -ADAM_LR * (m_hat / (jnp.sqrt(v_hat) + ADAM_EPS) + ADAM_WD * w_t)
        return delta, m_n, v_n

    delta, m_n, v_n = _rowwise(fn, args, [], [(cols, F32)] * 3, tm=tm, name=name)
    return delta.reshape(shape), m_n.reshape(shape), v_n.reshape(shape)


def _local_step(x, positions, target, w):
    bsz, t, _ = x.shape
    n = bsz * t
    tm = 256
    mm = 512
    x2 = x.reshape(n, D_MODEL)
    tgt2 = target.reshape(n, D_MODEL)
    tabs_a = _rope_tables(positions, A_ROPE_THETA, A_ROT_DIM, 0)
    tabs_b = _rope_tables(positions, B_ROPE_THETA, B_ROPE, B_NOPE)
    ta2 = [a.reshape(n, LANES) for a in tabs_a]
    tb2 = [a.reshape(n, LANES) for a in tabs_b]

    w_a_in = w["a_w_in"]
    w_qkv, w_z = w_a_in[:, :A_QKV], w_a_in[:, A_QKV:]
    w_a_out = w["a_w_out"]
    w_down = w["kv_w_down"]
    w_down_p = jnp.zeros((D_MODEL, 3 * LANES), BF16).at[:, :B_KV_LORA].set(w_down[:, :B_KV_LORA])
    w_down_p = w_down_p.at[:, B_KV_LORA + B_NOPE:B_KV_LORA + B_QK_DIM].set(w_down[:, B_KV_LORA:])
    wu = w["kv_w_up"].reshape(B_KV_LORA, B_HEADS, B_NOPE + B_VDIM)
    w_upk = jnp.pad(wu[:, :, :B_NOPE], ((0, 0), (0, 0), (0, LANES - B_NOPE))).reshape(B_KV_LORA, B_KPAD)
    w_upv = wu[:, :, B_NOPE:].reshape(B_KV_LORA, B_HEADS * B_VDIM)
    w_b_in = w["b_w_in"]
    w_q_p = jnp.pad(w["b_w_q_up"].reshape(B_Q_LORA, B_HEADS, B_QK_DIM),
                    ((0, 0), (0, 0), (0, LANES - B_QK_DIM))).reshape(B_Q_LORA, B_KPAD)
    w_b_out = w["b_w_out"]

    def tab_extras(tabs2, rows):
        return [(a, (rows, LANES), lambda j, i, kk: (i, 0)) for a in tabs2]

    (hn_a,) = _rowwise(lambda xt, g: (_rms(xt, g),), [x2], [w["a_pre_norm"]], [(D_MODEL, BF16)],
                       tm=tm, name="a_pre_norm")

    def rope_epilogue(acc, o_ref, rows, c_ref, sp_ref, sm_ref):
        c, sp, sm = c_ref[rows, :], sp_ref[rows, :], sm_ref[rows, :]
        for h in range(acc.shape[1] // LANES):
            hs = slice(h * LANES, (h + 1) * LANES)
            o_ref[rows, hs] = _rope(acc[:, hs], c, sp, sm).astype(BF16)

    def to_group(a, d):
        if d == 1:
            return a
        return a.reshape(bsz, t // d, d, a.shape[-1]).transpose(0, 2, 1, 3).reshape(n, a.shape[-1])

    def from_group(a, d):
        if d == 1:
            return a
        return a.reshape(bsz, d, t // d, a.shape[-1]).transpose(0, 2, 1, 3).reshape(n, a.shape[-1])

    def rows_to_cols(r, d):
        return r.reshape(bsz, d, A_HEADS, t // d).transpose(0, 3, 1, 2).reshape(n, A_HEADS)

    def cols_to_rows(cc, d):
        return cc.reshape(bsz, t // d, d, A_HEADS).transpose(0, 2, 3, 1).reshape(bsz * d, A_HEADS, t // d)

    z_a = _matmul(hn_a, w_z, out_dtype=F32, tm=mm, tn=A_WIDTH, name="a_gate")
    hn_g, tabs_g, qk_g, v_g, o_g, lse_g = [], [], [], [], [], []
    for g, d in enumerate(A_DILATIONS):
        hn_g.append(to_group(hn_a, d))
        tabs_g.append([to_group(a, d) for a in ta2])
        w_g = w_qkv[:, g * 3 * A_WIDTH:(g + 1) * 3 * A_WIDTH]
        qk = _matmul(hn_g[g], w_g[:, :2 * A_WIDTH], out_dtype=BF16, tm=mm, tn=A_WIDTH, name=f"a_qk_g{g}",
                     epilogue=rope_epilogue, extras=tab_extras(tabs_g[g], mm))
        v = _matmul(hn_g[g], w_g[:, 2 * A_WIDTH:], out_dtype=BF16, tm=mm, tn=A_WIDTH, name=f"a_v_g{g}")
        qk_g.append(qk.reshape(bsz * d, t // d, 2 * A_WIDTH))
        v_g.append(v.reshape(bsz * d, t // d, A_WIDTH))
        o, lse = _attn_a_fwd(qk_g[g], v_g[g], f"attn_a_fwd_g{g}")
        o_g.append(from_group(o.reshape(n, A_WIDTH), d))
        lse_g.append(rows_to_cols(lse, d))

    def merge_fn(o0, o1, o2, l0, l1, l2, z):
        lmax = jnp.maximum(jnp.maximum(l0, l1), l2)
        e0, e1, e2 = jnp.exp2(l0 - lmax), jnp.exp2(l1 - lmax), jnp.exp2(l2 - lmax)
        den = e0 + e1 + e2
        w0, w1, w2 = e0 / den, e1 / den, e2 / den
        parts = []
        for h in range(A_HEADS):
            hs = slice(h * A_HEAD_DIM, (h + 1) * A_HEAD_DIM)
            parts.append(w0[:, h:h + 1] * o0[:, hs] + w1[:, h:h + 1] * o1[:, hs] + w2[:, h:h + 1] * o2[:, hs])
        o = jnp.concatenate(parts, axis=1)
        return o * _silu(z), o, lmax + jnp.log2(den)

    y_a, o_a2, lse_a = _rowwise(merge_fn, [*o_g, *lse_g, z_a], [],
                                [(A_WIDTH, BF16), (A_WIDTH, F32), (A_HEADS, F32)], tm=tm, name="a_merge_gate")
    out_a = _matmul(y_a, w_a_out, out_dtype=F32, tm=mm, tn=D_MODEL, name="a_out")

    def mid_fn(xt, out, g_post, g_kv, g_b):
        h1 = xt + _rms(out, g_post)
        return h1, _rms(h1, g_kv), _rms(h1, g_b)

    h1, hn_kv, hn_b = _rowwise(mid_fn, [x2, out_a], [w["a_post_norm"], w["kv_norm"], w["b_pre_norm"]],
                               [(D_MODEL, F32), (D_MODEL, BF16), (D_MODEL, BF16)], tm=tm, name="a_post_norm")

    ckr = _matmul(hn_kv, w_down_p, out_dtype=F32, tm=mm, tn=3 * LANES, name="kv_down")

    def latent_fn(ck, c, sp, sm, g):
        return _rms(ck[:, :B_KV_LORA], g), _rope(ck[:, B_KV_LORA:], c, sp, sm)

    c_kv, k_rope = _rowwise(latent_fn, [ckr, *tb2], [w["kv_latent_norm"]], [(B_KV_LORA, BF16), (LANES, F32)],
                            tm=tm, name="kv_latent_norm")

    def kpad_epilogue(acc, o_ref, rows, kr_ref):
        kr = kr_ref[rows, :]
        for h in range(acc.shape[1] // LANES):
            hs = slice(h * LANES, (h + 1) * LANES)
            o_ref[rows, hs] = (acc[:, hs] + kr).astype(BF16)

    kpad = _matmul(c_kv, w_upk, out_dtype=BF16, tm=mm, tn=1024, name="kv_up_k", epilogue=kpad_epilogue,
                   extras=[(k_rope, (mm, LANES), lambda j, i, kk: (i, 0))])
    v_b = _matmul(c_kv, w_upv, out_dtype=BF16, tm=mm, tn=1024, name="kv_up_v")

    proj_b = _matmul(hn_b, w_b_in, out_dtype=F32, tm=mm, tn=w_b_in.shape[1], name="b_in")
    (c_q,) = _rowwise(lambda p, g: (_rms(p[:, :B_Q_LORA], g),), [proj_b], [w["b_q_norm"]], [(B_Q_LORA, BF16)],
                      tm=tm, name="b_q_norm")

    q_b = _matmul(c_q, w_q_p, out_dtype=BF16, tm=mm, tn=1024, name="b_q_up", epilogue=rope_epilogue,
                  extras=tab_extras(tb2, mm))
    q_b3, kpad3, v_b3 = q_b.reshape(bsz, t, B_KPAD), kpad.reshape(bsz, t, B_KPAD), v_b.reshape(bsz, t, -1)
    v4 = v_b3.reshape(bsz, t, B_HEADS, B_VDIM)
    vt1 = jnp.concatenate([v4, jnp.ones((bsz, t, B_HEADS, 1), BF16),
                           jnp.zeros((bsz, t, B_HEADS, LANES - B_VDIM - 1), BF16)], axis=3)
    vt1 = vt1.transpose(0, 2, 3, 1).reshape(bsz, B_KPAD, t)
    o_b, lse_b = _attn_b_fwd(q_b3, kpad3, vt1)
    o_b2 = o_b.reshape(n, -1)
    (y_b,) = _rowwise(lambda o, p: (o * _silu(p[:, B_Q_LORA:]),), [o_b2, proj_b], [], [(D_MODEL, BF16)],
                      tm=tm, name="b_gate_mul")
    out_b = _matmul(y_b, w_b_out, out_dtype=F32, tm=mm, tn=D_MODEL, name="b_out")

    def head_fn(h1t, out, tgt, g):
        e = h1t + _rms(out, g) - tgt
        loss_row = 0.5 * jnp.mean(e * e, axis=-1, keepdims=True)
        dh2 = e * (1.0 / D_MODEL)
        d_out, dg = _rms_bwd(out, g, dh2)
        return dh2, d_out, dg, jnp.broadcast_to(loss_row * (1.0 / LANES), (loss_row.shape[0], LANES))

    dh2, d_out_b, dg_b_post, loss_acc = _rowwise(
        head_fn, [h1, out_b, tgt2], [w["b_post_norm"]], [(D_MODEL, F32), (D_MODEL, BF16)], [D_MODEL, LANES],
        tm=tm, name="loss_head")

    dy_b = _matmul(d_out_b, w_b_out, tb=True, out_dtype=F32, tm=mm, tn=D_MODEL, name="b_out_dx")
    gw_b_out = _matmul(y_b, d_out_b, ta=True, out_dtype=F32, tm=mm, tn=D_MODEL, tk=2048, name="b_out_dw")

    def gate_b_bwd(dy, o, p):
        z = p[:, B_Q_LORA:]
        return dy * _silu(z), dy * o * _silu_grad(z)

    do_b, dz_b = _rowwise(gate_b_bwd, [dy_b, o_b2, proj_b], [], [(D_MODEL, BF16), (D_MODEL, F32)],
                          tm=tm, name="b_gate_bwd")
    do_b3 = do_b.reshape(bsz, t, -1)
    dq_b, dsum_b = _attn_b_dq(q_b3, kpad3, v_b3, do_b3, o_b, lse_b, tabs_b)
    dk_b, dv_b = _attn_b_dkv(q_b3, kpad3, v_b3, do_b3, lse_b, dsum_b)
    dq_b2, dk_b2, dv_b2 = dq_b.reshape(n, B_KPAD), dk_b.reshape(n, B_KPAD), dv_b.reshape(n, -1)

    dc_kv = _matmul(dk_b2, w_upk, tb=True, out_dtype=F32, tm=mm, tn=B_KV_LORA, name="kv_up_k_dx")
    dc_kv = _matmul(dv_b2, w_upv, tb=True, out_dtype=F32, tm=mm, tn=B_KV_LORA, name="kv_up_v_dx",
                    epilogue=_add_epilogue, extras=[(dc_kv, (mm, B_KV_LORA), lambda j, i, kk: (i, j))])
    gw_upk = _matmul(c_kv, dk_b2, ta=True, out_dtype=F32, tm=B_KV_LORA, tn=1024, tk=2048, name="kv_up_k_dw")
    gw_upv = _matmul(c_kv, dv_b2, ta=True, out_dtype=F32, tm=B_KV_LORA, tn=1024, tk=2048, name="kv_up_v_dw")

    def latent_bwd(ck, dck, dk, c, sp, sm, g):
        d1, dg = _rms_bwd(ck[:, :B_KV_LORA], g, dck)
        ksum = dk[:, :LANES].astype(F32)
        for h in range(1, B_HEADS):
            ksum = ksum + dk[:, h * LANES:(h + 1) * LANES].astype(F32)
        lane = lax.broadcasted_iota(jnp.int32, ksum.shape, 1)
        ksum = jnp.where((lane >= B_NOPE) & (lane < B_QK_DIM), ksum, 0.0)
        return jnp.concatenate([d1, _rope_t(ksum, c, sp, sm)], axis=1), dg

    dckr, dg_latent = _rowwise(latent_bwd, [ckr, dc_kv, dk_b2, *tb2], [w["kv_latent_norm"]],
                               [(3 * LANES, BF16)], [B_KV_LORA], tm=tm, name="kv_latent_bwd")
    dhn_kv = _matmul(dckr, w_down_p, tb=True, out_dtype=F32, tm=mm, tn=D_MODEL, name="kv_down_dx")
    gw_down_p = _matmul(hn_kv, dckr, ta=True, out_dtype=F32, tm=mm, tn=3 * LANES, tk=2048, name="kv_down_dw")

    dc_q = _matmul(dq_b2, w_q_p, tb=True, out_dtype=F32, tm=mm, tn=B_Q_LORA, name="b_q_up_dx")
    gw_q_p = _matmul(c_q, dq_b2, ta=True, out_dtype=F32, tm=B_Q_LORA, tn=1024, tk=2048, name="b_q_up_dw")

    def q_norm_bwd(p, dcq, dz, g):
        d1, dg = _rms_bwd(p[:, :B_Q_LORA], g, dcq)
        return jnp.concatenate([d1, dz], axis=1), dg

    dproj_b, dg_q_norm = _rowwise(q_norm_bwd, [proj_b, dc_q, dz_b], [w["b_q_norm"]],
                                  [(w_b_in.shape[1], BF16)], [B_Q_LORA], tm=tm, name="b_q_norm_bwd")
    dhn_b = _matmul(dproj_b, w_b_in, tb=True, out_dtype=F32, tm=mm, tn=D_MODEL, name="b_in_dx")
    gw_b_in = _matmul(hn_b, dproj_b, ta=True, out_dtype=F32, tm=mm, tn=w_b_in.shape[1], tk=2048, name="b_in_dw")

    def mid_bwd(h1t, dh2t, dkv, db, g_kv, g_b, g_post, out):
        dxa, ra = _rms_bwd(h1t, g_kv, dkv)
        dxb, rb = _rms_bwd(h1t, g_b, db)
        dh1 = dh2t + dxa + dxb
        d_out, rp = _rms_bwd(out, g_post, dh1)
        return dh1, d_out, ra, rb, rp

    def mid_bwd_fn(h1t, dh2t, dkv, db, out, g_kv, g_b, g_post):
        return mid_bwd(h1t, dh2t, dkv, db, g_kv, g_b, g_post, out)

    dh1, d_out_a, dg_kv, dg_b_pre, dg_a_post = _rowwise(
        mid_bwd_fn, [h1, dh2, dhn_kv, dhn_b, out_a], [w["kv_norm"], w["b_pre_norm"], w["a_post_norm"]],
        [(D_MODEL, F32), (D_MODEL, BF16)], [D_MODEL] * 3, tm=tm, name="mid_bwd")

    dy_a = _matmul(d_out_a, w_a_out, tb=True, out_dtype=F32, tm=mm, tn=A_WIDTH, name="a_out_dx")
    gw_a_out = _matmul(y_a, d_out_a, ta=True, out_dtype=F32, tm=mm, tn=D_MODEL, tk=2048, name="a_out_dw")

    def gate_a_bwd(dy, o, z):
        do = dy * _silu(z)
        prod = do * o
        lane = lax.broadcasted_iota(jnp.int32, (prod.shape[0], A_HEADS), 1)
        dsum = jnp.zeros((prod.shape[0], A_HEADS), F32)
        for h in range(A_HEADS):
            col = jnp.sum(prod[:, h * A_HEAD_DIM:(h + 1) * A_HEAD_DIM], axis=1, keepdims=True)
            dsum = jnp.where(lane == h, col, dsum)
        return do, dy * o * _silu_grad(z), dsum

    do_a, dz_a, dsum_a = _rowwise(gate_a_bwd, [dy_a, o_a2, z_a], [],
                                  [(A_WIDTH, BF16), (A_WIDTH, BF16), (A_HEADS, F32)], tm=tm, name="a_gate_bwd")
    dhn_a = _matmul(dz_a, w_z, tb=True, out_dtype=F32, tm=mm, tn=D_MODEL, name="a_gate_dx")
    gw_parts = []
    for g, d in enumerate(A_DILATIONS):
        s_n, ln = bsz * d, t // d
        dqkv = _attn_a_bwd(qk_g[g], v_g[g], to_group(do_a, d).reshape(s_n, ln, A_WIDTH), cols_to_rows(lse_a, d),
                           cols_to_rows(dsum_a, d), [a.reshape(s_n, ln, LANES) for a in tabs_g[g]],
                           f"attn_a_bwd_g{g}").reshape(n, 3 * A_WIDTH)
        w_g = w_qkv[:, g * 3 * A_WIDTH:(g + 1) * 3 * A_WIDTH]
        if d == 1:
            dhn_a = _matmul(dqkv, w_g, tb=True, out_dtype=F32, tm=mm, tn=D_MODEL, name=f"a_qkv_dx_g{g}",
                            epilogue=_add_epilogue, extras=[(dhn_a, (mm, D_MODEL), lambda j, i, kk: (i, j))])
        else:
            dhn_a = dhn_a + from_group(_matmul(dqkv, w_g, tb=True, out_dtype=F32, tm=mm, tn=D_MODEL,
                                               name=f"a_qkv_dx_g{g}"), d)
        gw_parts.append(_matmul(hn_g[g], dqkv, ta=True, out_dtype=F32, tm=mm, tn=1024, tk=2048,
                                name=f"a_qkv_dw_g{g}"))
    gw_parts.append(_matmul(hn_a, dz_a, ta=True, out_dtype=F32, tm=mm, tn=1024, tk=2048, name="a_gate_dw"))
    gw_a_in = jnp.concatenate(gw_parts, axis=1)

    def first_bwd(xt, dhn, dh1t, g):
        dx, dg = _rms_bwd(xt, g, dhn)
        return dh1t + dx, dg

    grad_x, dg_a_pre = _rowwise(first_bwd, [x2, dhn_a, dh1], [w["a_pre_norm"]], [(D_MODEL, F32)], [D_MODEL],
                                tm=tm, name="a_pre_norm_bwd")

    gw_down = jnp.concatenate([gw_down_p[:, :B_KV_LORA], gw_down_p[:, B_KV_LORA + B_NOPE:B_KV_LORA + B_QK_DIM]], axis=1)
    gw_up = jnp.concatenate([gw_upk.reshape(B_KV_LORA, B_HEADS, LANES)[:, :, :B_NOPE],
                             gw_upv.reshape(B_KV_LORA, B_HEADS, B_VDIM)], axis=2).reshape(B_KV_LORA, -1)
    gw_q_up = gw_q_p.reshape(B_Q_LORA, B_HEADS, LANES)[:, :, :B_QK_DIM].reshape(B_Q_LORA, -1)
    grads = {"a_w_in": gw_a_in, "a_w_out": gw_a_out, "kv_w_down": gw_down, "kv_w_up": gw_up,
             "b_w_in": gw_b_in, "b_w_q_up": gw_q_up, "b_w_out": gw_b_out}
    gains = {"a_pre_norm": dg_a_pre, "a_post_norm": dg_a_post, "kv_norm": dg_kv, "kv_latent_norm": dg_latent,
             "b_pre_norm": dg_b_pre, "b_q_norm": dg_q_norm, "b_post_norm": dg_b_post}
    gains = {k: jnp.sum(a, axis=0) for k, a in gains.items()}
    return jnp.sum(loss_acc), grad_x.reshape(bsz, t, D_MODEL), grads, gains


WEIGHT_ORDER = ("a_pre_norm", "a_w_in", "a_w_out", "a_post_norm", "kv_norm", "kv_w_down", "kv_latent_norm",
                "kv_w_up", "b_pre_norm", "b_w_in", "b_q_norm", "b_w_q_up", "b_w_out", "b_post_norm")
MATRICES = (("a_w_in", 1024, 10240, 1), ("a_w_out", 1024, 1024, 0), ("kv_w_down", 1024, 288, 0),
            ("kv_w_up", 256, 2048, 1), ("b_w_in", 1024, 1408, 1), ("b_w_q_up", 384, 1536, 1),
            ("b_w_out", 1024, 1024, 0))
SHARDED_GAINS = ("a_pre_norm", "a_post_norm")
GAIN_WIDTHS = (("a_pre_norm", 1024), ("a_post_norm", 1024), ("kv_norm", 1024), ("kv_latent_norm", 256),
               ("b_pre_norm", 1024), ("b_q_norm", 384), ("b_post_norm", 1024))
GAIN_ROWS = 48


def _shard_rows(rows, cols):
    return rows * cols // (N_DEV * LANES)


def _whole_from_blocks(blocks, rows, cols, axis):
    if axis == 1:
        return blocks.reshape(N_DEV, rows, cols // N_DEV).transpose(1, 0, 2).reshape(rows, cols)
    return blocks.reshape(rows, cols)


def _blocks_from_whole(whole, rows, cols, axis):
    if axis == 1:
        whole = whole.reshape(rows, N_DEV, cols // N_DEV).transpose(1, 0, 2)
    return whole.reshape(N_DEV, -1, LANES)


def kernel(x, positions, a_pre_norm, a_w_in, a_w_out, a_post_norm, kv_norm, kv_w_down, kv_latent_norm, kv_w_up, b_pre_norm, b_w_in, b_q_norm, b_w_q_up, b_w_out, b_post_norm, loss_target, m_a_pre_norm, m_a_w_in, m_a_w_out, m_a_post_norm, m_kv_norm, m_kv_w_down, m_kv_latent_norm, m_kv_w_up, m_b_pre_norm, m_b_w_in, m_b_q_norm, m_b_w_q_up, m_b_w_out, m_b_post_norm, v_a_pre_norm, v_a_w_in, v_a_w_out, v_a_post_norm, v_kv_norm, v_kv_w_down, v_kv_latent_norm, v_kv_w_up, v_b_pre_norm, v_b_w_in, v_b_q_norm, v_b_w_q_up, v_b_w_out, v_b_post_norm):
    weights = dict(a_pre_norm=a_pre_norm, a_w_in=a_w_in, a_w_out=a_w_out, a_post_norm=a_post_norm, kv_norm=kv_norm,
                   kv_w_down=kv_w_down, kv_latent_norm=kv_latent_norm, kv_w_up=kv_w_up, b_pre_norm=b_pre_norm,
                   b_w_in=b_w_in, b_q_norm=b_q_norm, b_w_q_up=b_w_q_up, b_w_out=b_w_out, b_post_norm=b_post_norm)
    m_in = dict(a_pre_norm=m_a_pre_norm, a_w_in=m_a_w_in, a_w_out=m_a_w_out, a_post_norm=m_a_post_norm,
                kv_norm=m_kv_norm, kv_w_down=m_kv_w_down, kv_latent_norm=m_kv_latent_norm, kv_w_up=m_kv_w_up,
                b_pre_norm=m_b_pre_norm, b_w_in=m_b_w_in, b_q_norm=m_b_q_norm, b_w_q_up=m_b_w_q_up,
                b_w_out=m_b_w_out, b_post_norm=m_b_post_norm)
    v_in = dict(a_pre_norm=v_a_pre_norm, a_w_in=v_a_w_in, a_w_out=v_a_w_out, a_post_norm=v_a_post_norm,
                kv_norm=v_kv_norm, kv_w_down=v_kv_w_down, kv_latent_norm=v_kv_latent_norm, kv_w_up=v_kv_w_up,
                b_pre_norm=v_b_pre_norm, b_w_in=v_b_w_in, b_q_norm=v_b_q_norm, b_w_q_up=v_b_w_q_up,
                b_w_out=v_b_w_out, b_post_norm=v_b_post_norm)
    me = 4 * lax.axis_index("x") + 2 * lax.axis_index("y") + lax.axis_index("c")

    flat = jnp.concatenate([weights[name].astype(BF16).reshape(-1, LANES) for name, _, _, _ in MATRICES], axis=0)
    gathered = _all_gather(flat, "gather_weights")
    whole = {}
    off = 0
    for name, rows, cols, axis in MATRICES:
        nr = _shard_rows(rows, cols)
        whole[name] = _whole_from_blocks(gathered[:, off:off + nr], rows, cols, axis)
        off += nr
    gain_shard = jnp.concatenate([weights[name].reshape(1, LANES) for name in SHARDED_GAINS]
                                 + [jnp.zeros((8 - len(SHARDED_GAINS), LANES), F32)], axis=0)
    gain_blocks = _all_gather(gain_shard, "gather_gains")
    for i, name in enumerate(SHARDED_GAINS):
        whole[name] = gain_blocks[:, i, :].reshape(1, D_MODEL)
    for name in ("kv_norm", "kv_latent_norm", "b_pre_norm", "b_q_norm", "b_post_norm"):
        whole[name] = weights[name].reshape(1, -1)

    loss_part, grad_x, grads, gains = _local_step(x, positions, loss_target, whole)

    blocks = jnp.concatenate([_blocks_from_whole(grads[name], rows, cols, axis).astype(BF16)
                              for name, rows, cols, axis in MATRICES], axis=1)
    kept, got = _exchange_sibling(blocks, "scatter_grads_core")
    landed = _exchange_chips(_add_pairs(kept, got, tr=2512, name="add_core_grads"), "scatter_grads_chip")
    summed = _sum_slots(landed, tr=2512, name="sum_grads")
    grad_out = {}
    off = 0
    for name, rows, cols, axis in MATRICES:
        nr = _shard_rows(rows, cols)
        grad_out[name] = summed[off:off + nr].reshape(weights[name].shape)
        off += nr

    vec = jnp.concatenate([gains[name] for name, _ in GAIN_WIDTHS] + [jnp.full((LANES,), loss_part, F32)])
    vec = jnp.pad(vec, (0, GAIN_ROWS * LANES - vec.shape[0])).reshape(GAIN_ROWS, LANES)
    total = _sum_slots(_all_gather(vec, "gather_gain_grads"), tr=GAIN_ROWS, name="sum_gain_grads").reshape(-1)
    off = 0
    for name, width in GAIN_WIDTHS:
        g = total[off:off + width]
        if name in SHARDED_GAINS:
            g = lax.dynamic_slice(g, (me * LANES,), (LANES,))
        grad_out[name] = g.reshape(weights[name].shape)
        off += width
    loss = total[off]

    deltas, new_m, new_v = {}, {}, {}
    for name in WEIGHT_ORDER:
        deltas[name], new_m[name], new_v[name] = _adamw(weights[name], grad_out[name], m_in[name], v_in[name],
                                                        "adamw_" + name)
    return (loss, grad_x, *[grad_out[k] for k in WEIGHT_ORDER], *[deltas[k] for k in WEIGHT_ORDER],
            *[new_m[k] for k in WEIGHT_ORDER], *[new_v[k] for k in WEIGHT_ORDER])
```

```python
import jax
import jax.numpy as jnp
from jax import lax
from jax.experimental import pallas as pl
from jax.experimental.pallas import tpu as pltpu

F32 = jnp.float32
BF16 = jnp.bfloat16

N_DEV = 8
D_MODEL = 1024
NORM_EPS = 1e-6
A_GROUPS = 3
A_DILATIONS = (1, 4, 16)
A_HEADS = 8
A_HEAD_DIM = 128
A_WIDTH = 1024
A_ROT_DIM = 32
A_ROPE_THETA = 500000.0
A_QKV = A_GROUPS * 3 * A_WIDTH
B_HEADS = 16
B_NOPE = 64
B_ROPE = 32
B_QK_DIM = 96
B_VDIM = 64
B_Q_LORA = 384
B_KV_LORA = 256
B_ROPE_THETA = 10000.0
B_KPAD = B_HEADS * 128
ADAM_LR = 0.001
ADAM_B1 = 0.9
ADAM_B2 = 0.999
ADAM_EPS = 1e-08
ADAM_WD = 0.01
ADAM_STEP = 10

LANES = 128
BAND = 128
EPILOGUE_ROWS = 128
NEG = -1e30
VMEM_LIMIT = 56 * 1024 * 1024
MESH = pl.DeviceIdType.MESH


def _cparams(sem):
    return pltpu.CompilerParams(dimension_semantics=sem, vmem_limit_bytes=VMEM_LIMIT)


def _rowwise(fn, rows, bcast, outs, accs=(), *, tm, name):
    n = rows[0].shape[0]
    nr, nb, no = len(rows), len(bcast), len(outs)

    def body(*refs):
        res = fn(*[r[...] for r in refs[:nr + nb]])
        out_refs = refs[nr + nb:nr + nb + no]
        acc_refs = refs[nr + nb + no:]
        for r, v in zip(out_refs, res[:no]):
            r[...] = v.astype(r.dtype)
        if acc_refs:
            @pl.when(pl.program_id(0) == 0)
            def _():
                for r in acc_refs:
                    r[...] = jnp.zeros_like(r)
            for r, v in zip(acc_refs, res[no:]):
                r[...] += v.reshape(tm // 8, 8, v.shape[-1]).sum(axis=0)

    in_specs = [pl.BlockSpec((tm, a.shape[1]), lambda i: (i, 0)) for a in rows]
    in_specs += [pl.BlockSpec(a.shape, lambda i: (0, 0)) for a in bcast]
    out_specs = [pl.BlockSpec((tm, c), lambda i: (i, 0)) for c, _ in outs]
    out_specs += [pl.BlockSpec((8, c), lambda i: (0, 0)) for c in accs]
    out_shape = [jax.ShapeDtypeStruct((n, c), dt) for c, dt in outs]
    out_shape += [jax.ShapeDtypeStruct((8, c), F32) for c in accs]
    return pl.pallas_call(
        body, name=name, grid=(n // tm,), in_specs=in_specs, out_specs=out_specs, out_shape=out_shape,
        compiler_params=_cparams(("arbitrary",)))(*rows, *bcast)


def _matmul(a, b, *, out_dtype, tm, tn, tk=None, name, epilogue=None, extras=(), ta=False, tb=False,
            epilogue_rows=None):
    epilogue_rows = epilogue_rows or tm
    k, m = a.shape[::-1] if not ta else a.shape
    n = b.shape[0] if tb else b.shape[1]
    tk = tk or k
    nk = k // tk
    ne = len(extras)
    dot = _dot_tn if ta else (_dot_nt if tb else _dot)
    assert epilogue is None or (nk == 1 and not ta)

    def body(*refs):
        a_ref, b_ref = refs[:2]
        ex = refs[2:2 + ne]
        o_ref = refs[2 + ne]
        if epilogue is not None:
            b_tile = b_ref[...].astype(BF16)
            for r0 in range(0, tm, epilogue_rows):
                rows = slice(r0, r0 + epilogue_rows)
                epilogue(dot(a_ref[rows, :].astype(BF16), b_tile), o_ref, rows, *ex)
            return
        part = dot(a_ref[...].astype(BF16), b_ref[...].astype(BF16))
        if nk == 1:
            o_ref[...] = part.astype(o_ref.dtype)
        else:
            acc_ref = refs[-1]
            kk = pl.program_id(2)

            @pl.when(kk == 0)
            def _():
                acc_ref[...] = part

            @pl.when(kk > 0)
            def _():
                acc_ref[...] += part

            @pl.when(kk == nk - 1)
            def _():
                o_ref[...] = acc_ref[...].astype(o_ref.dtype)

    a_spec = pl.BlockSpec((tk, tm), lambda j, i, kk: (kk, i)) if ta else pl.BlockSpec((tm, tk), lambda j, i, kk: (i, kk))
    b_spec = pl.BlockSpec((tn, tk), lambda j, i, kk: (j, kk)) if tb else pl.BlockSpec((tk, tn), lambda j, i, kk: (kk, j))
    in_specs = [a_spec, b_spec] + [pl.BlockSpec(bs, im) for _, bs, im in extras]
    return pl.pallas_call(
        body, name=name, grid=(n // tn, m // tm, nk), in_specs=in_specs,
        out_specs=pl.BlockSpec((tm, tn), lambda j, i, kk: (i, j)),
        out_shape=jax.ShapeDtypeStruct((m, n), out_dtype),
        scratch_shapes=[pltpu.VMEM((tm, tn), F32)] if nk > 1 else [],
        compiler_params=_cparams(("parallel", "parallel", "arbitrary")))(a, b, *[e[0] for e in extras])


def _add_epilogue(acc, o_ref, rows, prev_ref):
    o_ref[rows, :] = (acc + prev_ref[rows, :]).astype(o_ref.dtype)


def _rope(x, c, sp, sm):
    return x * c + pltpu.roll(x, 16, 1) * sp + pltpu.roll(x, LANES - 16, 1) * sm


def _rope_t(dy, c, sp, sm):
    return dy * c + pltpu.roll(dy * sp, LANES - 16, 1) + pltpu.roll(dy * sm, 16, 1)


def _rope_tables(positions, theta, rot_dim, lane0):
    half = rot_dim // 2
    inv_freq = 1.0 / (theta ** (jnp.arange(half, dtype=F32) * (2.0 / rot_dim)))
    ang = positions.astype(F32)[..., None] * inv_freq
    cos, sin = jnp.cos(ang), jnp.sin(ang)
    def lanes(first, second, fill):
        pad = lambda width: jnp.full(positions.shape + (width,), fill, F32)
        return jnp.concatenate([pad(lane0), first, second, pad(LANES - lane0 - rot_dim)], axis=-1)

    zero = jnp.zeros_like(sin)
    return lanes(cos, cos, 1.0), lanes(zero, sin, 0.0), lanes(-sin, zero, 0.0)


def _rms(x, g):
    xf = x.astype(F32)
    return xf * lax.rsqrt(jnp.mean(xf * xf, axis=-1, keepdims=True) + NORM_EPS) * g


def _rms_bwd(x, g, dy):
    xf = x.astype(F32)
    rstd = lax.rsqrt(jnp.mean(xf * xf, axis=-1, keepdims=True) + NORM_EPS)
    xhat = xf * rstd
    dxhat = dy * g
    dx = rstd * (dxhat - xhat * jnp.mean(dxhat * xhat, axis=-1, keepdims=True))
    return dx, dy * xhat


def _silu(z):
    return z * jax.nn.sigmoid(z)


def _silu_grad(z):
    s = jax.nn.sigmoid(z)
    return s * (1.0 + z * (1.0 - s))


def _dot_nt(a, b):
    return lax.dot_general(a, b, (((1,), (1,)), ((), ())), preferred_element_type=F32)


def _dot_tn(a, b):
    return lax.dot_general(a, b, (((0,), (0,)), ((), ())), preferred_element_type=F32)


def _dot(a, b):
    return jnp.dot(a, b, preferred_element_type=F32)


A_SCALE = A_HEAD_DIM ** -0.5
LOG2E = 1.4426950408889634
A_C2 = A_SCALE * LOG2E
A_HEAD_GROUP = 4


def _attn_a_fwd(qk, v, name):
    s_n, ln, _ = qk.shape
    nb = ln // BAND
    blk = (None, BAND, A_WIDTH)

    def body(q_ref, kc_ref, kp_ref, vc_ref, vp_ref, o_ref, lse_ref):
        kpos = lax.broadcasted_iota(jnp.int32, (2 * BAND, BAND), 0)
        qpos = lax.broadcasted_iota(jnp.int32, (2 * BAND, BAND), 1) + BAND
        first_key = jnp.where(pl.program_id(1) > 0, 0, BAND)
        mask = (kpos <= qpos) & (kpos >= qpos - BAND) & (kpos >= first_key)
        rows = []
        for h0 in range(0, A_HEADS, A_HEAD_GROUP):
            hss = [slice(h * A_HEAD_DIM, (h + 1) * A_HEAD_DIM) for h in range(h0, h0 + A_HEAD_GROUP)]
            sts = [_dot_nt(jnp.concatenate([kp_ref[:, hs], kc_ref[:, hs]], axis=0), q_ref[:, hs]) for hs in hss]
            ps, ls = [], []
            for st in sts:
                st = jnp.where(mask, st * A_C2, NEG)
                m = jnp.max(st, axis=0, keepdims=True)
                p = jnp.exp2(st - m)
                l_row = jnp.sum(p, axis=0, keepdims=True)
                ps.append(p.astype(BF16))
                ls.append(l_row)
                rows.append(m + jnp.log2(l_row))
            ots = [_dot_tn(jnp.concatenate([vp_ref[:, hs], vc_ref[:, hs]], axis=0), p) for hs, p in zip(hss, ps)]
            for hs, o_t, l_row in zip(hss, ots, ls):
                o_ref[:, hs] = (o_t / l_row).T.astype(BF16)
        lse_ref[...] = jnp.concatenate(rows, axis=0)

    def col(c, off):
        return lambda s, l: (s, jnp.maximum(l + off, 0), c)

    return pl.pallas_call(
        body, name=name, grid=(s_n, nb),
        in_specs=[pl.BlockSpec(blk, col(0, 0)), pl.BlockSpec(blk, col(1, 0)), pl.BlockSpec(blk, col(1, -1)),
                  pl.BlockSpec(blk, col(0, 0)), pl.BlockSpec(blk, col(0, -1))],
        out_specs=[pl.BlockSpec(blk, lambda s, l: (s, l, 0)),
                   pl.BlockSpec((None, A_HEADS, BAND), lambda s, l: (s, 0, l))],
        out_shape=[jax.ShapeDtypeStruct((s_n, ln, A_WIDTH), BF16), jax.ShapeDtypeStruct((s_n, A_HEADS, ln), F32)],
        compiler_params=_cparams(("parallel", "arbitrary")))(qk, qk, qk, v, v)


def _attn_a_bwd(qk, v, do, lse2, dsum, tabs, name):
    s_n, ln, _ = qk.shape
    nb = ln // BAND
    blk = (None, BAND, A_WIDTH)

    def body(q_ref, qn_ref, kc_ref, kp_ref, vc_ref, vp_ref, do_ref, don_ref, lse_ref, lsen_ref, ds_ref, dsn_ref,
             c_ref, sp_ref, sm_ref, out_ref):
        l_idx = pl.program_id(1)
        kpos = lax.broadcasted_iota(jnp.int32, (2 * BAND, BAND), 0)
        qpos = lax.broadcasted_iota(jnp.int32, (2 * BAND, BAND), 1) + BAND
        first_key = jnp.where(l_idx > 0, 0, BAND)
        mask_q = (kpos <= qpos) & (kpos >= qpos - BAND) & (kpos >= first_key)
        kpos2 = lax.broadcasted_iota(jnp.int32, (BAND, 2 * BAND), 0)
        qpos2 = lax.broadcasted_iota(jnp.int32, (BAND, 2 * BAND), 1)
        last_query = jnp.where(l_idx < nb - 1, 2 * BAND, BAND)
        mask_k = (kpos2 <= qpos2) & (kpos2 >= qpos2 - BAND) & (qpos2 < last_query)
        c, sp, sm = c_ref[...], sp_ref[...], sm_ref[...]
        lse_q, ds_q = lse_ref[...], ds_ref[...]
        lse_k = jnp.concatenate([lse_q, lsen_ref[...]], axis=1)
        ds_k = jnp.concatenate([ds_q, dsn_ref[...]], axis=1)
        for h0 in range(0, A_HEADS, A_HEAD_GROUP):
            hl = list(range(h0, h0 + A_HEAD_GROUP))
            hss = [slice(h * A_HEAD_DIM, (h + 1) * A_HEAD_DIM) for h in hl]
            k2s = [jnp.concatenate([kp_ref[:, hs], kc_ref[:, hs]], axis=0) for hs in hss]
            v2s = [jnp.concatenate([vp_ref[:, hs], vc_ref[:, hs]], axis=0) for hs in hss]
            q2s = [jnp.concatenate([q_ref[:, hs], qn_ref[:, hs]], axis=0) for hs in hss]
            do2s = [jnp.concatenate([do_ref[:, hs], don_ref[:, hs]], axis=0) for hs in hss]
            sts = [_dot_nt(k2, q_ref[:, hs]) for k2, hs in zip(k2s, hss)]
            dpts = [_dot_nt(v2, do_ref[:, hs]) for v2, hs in zip(v2s, hss)]
            st2s = [_dot_nt(kc_ref[:, hs], q2) for q2, hs in zip(q2s, hss)]
            dpt2s = [_dot_nt(vc_ref[:, hs], do2) for do2, hs in zip(do2s, hss)]
            dsts, dst2s, p2s = [], [], []
            for i, h in enumerate(hl):
                p = jnp.exp2(jnp.where(mask_q, sts[i] * A_C2, NEG) - lse_q[h:h + 1])
                dsts.append((p * (dpts[i] - ds_q[h:h + 1]) * A_SCALE).astype(BF16))
                p2 = jnp.exp2(jnp.where(mask_k, st2s[i] * A_C2, NEG) - lse_k[h:h + 1])
                dst2s.append((p2 * (dpt2s[i] - ds_k[h:h + 1]) * A_SCALE).astype(BF16))
                p2s.append(p2.astype(BF16))
            dqs = [_dot_tn(dsts[i], k2s[i]) for i in range(A_HEAD_GROUP)]
            dks = [_dot(dst2s[i], q2s[i]) for i in range(A_HEAD_GROUP)]
            dvs = [_dot(p2s[i], do2s[i]) for i in range(A_HEAD_GROUP)]
            for i, h in enumerate(hl):
                out_ref[:, hss[i]] = _rope_t(dqs[i], c, sp, sm).astype(BF16)
                out_ref[:, A_WIDTH + h * A_HEAD_DIM:A_WIDTH + (h + 1) * A_HEAD_DIM] = _rope_t(dks[i], c, sp, sm).astype(BF16)
                out_ref[:, 2 * A_WIDTH + h * A_HEAD_DIM:2 * A_WIDTH + (h + 1) * A_HEAD_DIM] = dvs[i].astype(BF16)

    def col(c, off):
        return lambda s, l: (s, jnp.clip(l + off, 0, nb - 1), c)

    def row(off):
        return pl.BlockSpec((None, A_HEADS, BAND), lambda s, l: (s, 0, jnp.clip(l + off, 0, nb - 1)))

    tspec = pl.BlockSpec((None, BAND, LANES), lambda s, l: (s, l, 0))
    in_specs = [pl.BlockSpec(blk, col(0, 0)), pl.BlockSpec(blk, col(0, 1)),
                pl.BlockSpec(blk, col(1, 0)), pl.BlockSpec(blk, col(1, -1)),
                pl.BlockSpec(blk, col(0, 0)), pl.BlockSpec(blk, col(0, -1)),
                pl.BlockSpec(blk, col(0, 0)), pl.BlockSpec(blk, col(0, 1)),
                row(0), row(1), row(0), row(1), tspec, tspec, tspec]
    return pl.pallas_call(
        body, name=name, grid=(s_n, nb), in_specs=in_specs,
        out_specs=pl.BlockSpec((None, BAND, 3 * A_WIDTH), lambda s, l: (s, l, 0)),
        out_shape=jax.ShapeDtypeStruct((s_n, ln, 3 * A_WIDTH), BF16),
        compiler_params=_cparams(("parallel", "arbitrary")))(
            qk, qk, qk, qk, v, v, do, do, lse2, lse2, dsum, dsum, *tabs)


B_TQ = 256
B_SCALE = B_QK_DIM ** -0.5
B_C2 = B_SCALE * LOG2E


def _b_row_tile(tq):
    return pl.BlockSpec((None, None, 8, tq), lambda b, j, i: (b, j, 0, i))


def _b_specs(t, tq, tk):
    pair_tile = pl.BlockSpec((None, tq, 2 * LANES), lambda b, j, i: (b, i, j))
    pair_full = pl.BlockSpec((None, t, 2 * LANES), lambda b, j, i: (b, 0, j))
    one_tile = pl.BlockSpec((None, tk, LANES), lambda b, j, i: (b, i, j))
    one_full = pl.BlockSpec((None, t, LANES), lambda b, j, i: (b, 0, j))
    tab_tile = pl.BlockSpec((None, tq, LANES), lambda b, j, i: (b, i, 0))
    return pair_tile, pair_full, one_tile, one_full, tab_tile


def _key_le_query(kb, qb, tk, tq):
    kpos = kb * tk + lax.broadcasted_iota(jnp.int32, (tk, tq), 0)
    qpos = qb * tq + lax.broadcasted_iota(jnp.int32, (tk, tq), 1)
    return kpos <= qpos


def _attn_b_fwd(q, kpad, vt1):
    bsz, t, _ = q.shape
    tq = tk = B_TQ
    nq = t // tq

    def body(q_ref, k_ref, vt_ref, o_ref, lse_ref):
        qblk = pl.program_id(2)
        qs = [q_ref[:, hh * LANES:(hh + 1) * LANES] for hh in range(2)]

        def scores(kb):
            start = pl.multiple_of(kb * tk, tk)
            return [_dot_nt(k_ref[pl.ds(start, tk), hh * LANES:(hh + 1) * LANES], qs[hh]) for hh in range(2)]

        def pv(kb, ps):
            start = pl.multiple_of(kb * tk, tk)
            return [_dot(vt_ref[hh * LANES:(hh + 1) * LANES, pl.ds(start, tk)], ps[hh]) for hh in range(2)]

        def softmax(ss, ms, accs, kb, masked):
            out_m, out_acc, out_p = [], [], []
            for hh in range(2):
                s = ss[hh] * B_C2
                if masked:
                    s = jnp.where(_key_le_query(kb, qblk, tk, tq), s, NEG)
                m_new = jnp.maximum(ms[hh], jnp.max(s, axis=0, keepdims=True))
                out_acc.append(jnp.exp2(ms[hh] - m_new) * accs[hh])
                out_p.append(jnp.exp2(s - m_new).astype(BF16))
                out_m.append(m_new)
            return out_m, out_acc, out_p

        def step(kb, carry):
            ss, ps, ms, accs = carry
            pvs = pv(jnp.maximum(kb - 1, 0), ps)
            ss_next = scores(kb + 1)
            accs = [accs[hh] + pvs[hh] for hh in range(2)]
            ms, accs, ps = softmax(ss, ms, accs, kb, False)
            return (ss_next, ps, ms, accs)

        init = (scores(0), [jnp.zeros((tk, tq), BF16)] * 2, [jnp.full((1, tq), NEG, F32)] * 2,
                [jnp.zeros((LANES, tq), F32)] * 2)
        ss, ps, ms, accs = lax.fori_loop(0, qblk, step, init)
        pvs = pv(jnp.maximum(qblk - 1, 0), ps)
        accs = [accs[hh] + pvs[hh] for hh in range(2)]
        ms, accs, ps = softmax(ss, ms, accs, qblk, True)
        pvs = pv(qblk, ps)
        accs = [accs[hh] + pvs[hh] for hh in range(2)]
        ls = [accs[hh][B_VDIM:B_VDIM + 1] for hh in range(2)]
        o_t = jnp.concatenate([accs[0][:B_VDIM] / ls[0], accs[1][:B_VDIM] / ls[1]], axis=0)
        o_ref[...] = o_t.T
        lse_ref[...] = jnp.concatenate([ms[0] + jnp.log2(ls[0]), ms[1] + jnp.log2(ls[1]),
                                        jnp.zeros((6, tq), F32)], axis=0)

    pair_tile, pair_full, one_tile, _, _ = _b_specs(t, tq, tk)
    vt_spec = pl.BlockSpec((None, 2 * LANES, t), lambda b, j, i: (b, j, 0))
    return pl.pallas_call(
        body, name="attn_b_fwd", grid=(bsz, B_HEADS // 2, nq),
        in_specs=[pair_tile, pair_full, vt_spec],
        out_specs=[one_tile, _b_row_tile(tq)],
        out_shape=[jax.ShapeDtypeStruct((bsz, t, B_HEADS * B_VDIM), F32),
                   jax.ShapeDtypeStruct((bsz, B_HEADS // 2, 8, t), F32)],
        compiler_params=_cparams(("parallel", "parallel", "arbitrary")))(q, kpad, vt1)


def _attn_b_dq(q, kpad, v, do, o, lse2, tabs):
    bsz, t, _ = q.shape
    tq = tk = B_TQ
    nq = t // tq

    def body(q_ref, k_ref, v_ref, do_ref, o_ref, lse_ref, c_ref, sp_ref, sm_ref, dq_ref, dsum_ref):
        qblk = pl.program_id(2)
        lane = lax.broadcasted_iota(jnp.int32, (tq, LANES), 1)
        do_b = do_ref[...]
        prod = do_b.astype(F32) * o_ref[...]
        hi = prod.astype(BF16)
        lo = (prod - hi.astype(F32)).astype(BF16)
        sel_lane = lax.broadcasted_iota(jnp.int32, (8, LANES), 1)
        sel_row = lax.broadcasted_iota(jnp.int32, (8, LANES), 0)
        sel = jnp.where((sel_lane < B_VDIM) == (sel_row == 0), 1.0, 0.0)
        sel = jnp.where(sel_row < 2, sel, 0.0).astype(BF16)
        dsum = _dot_nt(sel, hi) + _dot_nt(sel, lo)
        dsum_ref[...] = dsum
        lse = lse_ref[...]
        qs = [q_ref[:, hh * LANES:(hh + 1) * LANES] for hh in range(2)]
        zero = jnp.zeros_like(do_b)
        dos = [jnp.where(lane < B_VDIM, do_b, zero), jnp.where(lane < B_VDIM, zero, do_b)]

        def kblock(kb, hh):
            start = pl.multiple_of(kb * tk, tk)
            return k_ref[pl.ds(start, tk), hh * LANES:(hh + 1) * LANES]

        def dots(kb):
            start = pl.multiple_of(kb * tk, tk)
            vb = v_ref[pl.ds(start, tk), :]
            return ([_dot_nt(kblock(kb, hh), qs[hh]) for hh in range(2)], [_dot_nt(vb, dos[hh]) for hh in range(2)])

        def ds_of(ss, dps, kb, masked):
            out = []
            for hh in range(2):
                s = ss[hh] * B_C2
                if masked:
                    s = jnp.where(_key_le_query(kb, qblk, tk, tq), s, NEG)
                p = jnp.exp2(s - lse[hh:hh + 1])
                out.append((p * (dps[hh] - dsum[hh:hh + 1]) * B_SCALE).astype(BF16))
            return out

        def accum(acc, kb, dss):
            return [acc[hh] + _dot_tn(kblock(kb, hh), dss[hh]) for hh in range(2)]

        def step(kb, carry):
            dss, acc = carry
            ss, dps = dots(kb)
            acc = accum(acc, jnp.maximum(kb - 1, 0), dss)
            return (ds_of(ss, dps, kb, False), acc)

        init = ([jnp.zeros((tk, tq), BF16)] * 2, [jnp.zeros((LANES, tq), F32)] * 2)
        dss, acc = lax.fori_loop(0, qblk, step, init)
        ss, dps = dots(qblk)
        acc = accum(acc, jnp.maximum(qblk - 1, 0), dss)
        acc = accum(acc, qblk, ds_of(ss, dps, qblk, True))
        c, sp, sm = c_ref[...], sp_ref[...], sm_ref[...]
        for hh in range(2):
            dq_ref[:, hh * LANES:(hh + 1) * LANES] = _rope_t(acc[hh].T, c, sp, sm).astype(BF16)

    pair_tile, pair_full, one_tile, one_full, tab_tile = _b_specs(t, tq, tk)
    row_tile = _b_row_tile(tq)
    return pl.pallas_call(
        body, name="attn_b_dq", grid=(bsz, B_HEADS // 2, nq),
        in_specs=[pair_tile, pair_full, one_full, one_tile, one_tile, row_tile, tab_tile, tab_tile, tab_tile],
        out_specs=[pair_tile, row_tile],
        out_shape=[jax.ShapeDtypeStruct((bsz, t, B_KPAD), BF16),
                   jax.ShapeDtypeStruct((bsz, B_HEADS // 2, 8, t), F32)],
        compiler_params=_cparams(("parallel", "parallel", "arbitrary")))(q, kpad, v, do, o, lse2, *tabs)


def _attn_b_dkv(q, kpad, v, do, lse2, dsum):
    bsz, t, _ = q.shape
    tq = tk = B_TQ
    nq = t // tq

    def body(q_ref, k_ref, v_ref, do_ref, lse_ref, dsum_ref, dk_ref, dv_ref):
        kblk = pl.program_id(2)
        lane = lax.broadcasted_iota(jnp.int32, (tk, LANES), 1)
        ks = [k_ref[:, hh * LANES:(hh + 1) * LANES] for hh in range(2)]
        vb = v_ref[...]
        zero = jnp.zeros_like(vb)
        vs = [jnp.where(lane < B_VDIM, vb, zero), jnp.where(lane < B_VDIM, zero, vb)]

        def make_step(masked):
            def step(qb, carry):
                start = pl.multiple_of(qb * tq, tq)
                qs = [q_ref[pl.ds(start, tq), hh * LANES:(hh + 1) * LANES] for hh in range(2)]
                do_b = do_ref[pl.ds(start, tq), :]
                ss = [_dot_nt(ks[hh], qs[hh]) for hh in range(2)]
                dps = [_dot_nt(vs[hh], do_b) for hh in range(2)]
                pbs, dss = [], []
                for hh in range(2):
                    s = ss[hh] * B_C2
                    if masked:
                        s = jnp.where(_key_le_query(kblk, qb, tk, tq), s, NEG)
                    p = jnp.exp2(s - lse_ref[hh:hh + 1, pl.ds(start, tq)])
                    dss.append((p * (dps[hh] - dsum_ref[hh:hh + 1, pl.ds(start, tq)]) * B_SCALE).astype(BF16))
                    pbs.append(p.astype(BF16))
                return (carry[0] + _dot(dss[0], qs[0]), carry[1] + _dot(dss[1], qs[1]),
                        carry[2] + _dot(pbs[0], do_b), carry[3] + _dot(pbs[1], do_b))
            return step

        init = (jnp.zeros((tk, LANES), F32),) * 4
        carry = make_step(True)(kblk, init)
        carry = lax.fori_loop(kblk + 1, nq, make_step(False), carry)
        dk_ref[:, :LANES] = carry[0].astype(BF16)
        dk_ref[:, LANES:] = carry[1].astype(BF16)
        dv_ref[...] = jnp.where(lane < B_VDIM, carry[2], carry[3]).astype(BF16)

    pair_tile, pair_full, one_tile, one_full, _ = _b_specs(t, tq, tk)
    row_full = pl.BlockSpec((None, None, 8, t), lambda b, j, i: (b, j, 0, 0))
    return pl.pallas_call(
        body, name="attn_b_dkv", grid=(bsz, B_HEADS // 2, nq),
        in_specs=[pair_full, pair_tile, one_tile, one_full, row_full, row_full],
        out_specs=[pair_tile, one_tile],
        out_shape=[jax.ShapeDtypeStruct((bsz, t, B_KPAD), BF16),
                   jax.ShapeDtypeStruct((bsz, t, B_HEADS * B_VDIM), BF16)],
        compiler_params=_cparams(("parallel", "parallel", "arbitrary")))(q, kpad, v, do, lse2, dsum)


ANY = pl.BlockSpec(memory_space=pl.ANY)


def _all_gather(shard, name):
    def body(x_ref, out_ref, send_sems, recv_sems, local_sem):
        x, y, c = lax.axis_index("x"), lax.axis_index("y"), lax.axis_index("c")
        me, sibling = (x, y, c), (x, y, 1 - c)
        chips = [(1 - x, y), (x, 1 - y), (1 - x, 1 - y)]

        def rows(px, py, pc):
            return out_ref.at[4 * px + 2 * py + pc]

        def copy(k, block, to, src=None):
            return pltpu.make_async_remote_copy(
                src_ref=rows(*block) if src is None else src, dst_ref=rows(*block),
                send_sem=send_sems.at[k], recv_sem=recv_sems.at[k], device_id=to, device_id_type=MESH)

        mine = pltpu.make_async_copy(x_ref, rows(*me), local_sem)
        mine.start()
        first = [copy(0, me, sibling, src=x_ref)]
        first += [copy(1 + j, me, (*chip, c), src=x_ref) for j, chip in enumerate(chips)]
        for cp in first:
            cp.start()
        passed = [copy(4 + j, (*chip, c), sibling) for j, chip in enumerate(chips)]
        for j, chip in enumerate(chips):
            copy(1 + j, (*chip, c), me).wait_recv()
            passed[j].start()
        copy(0, sibling, me).wait_recv()
        for j, chip in enumerate(chips):
            copy(4 + j, (*chip, 1 - c), me).wait_recv()
        for cp in first + passed:
            cp.wait_send()
        mine.wait()

    return pl.pallas_call(
        body, name=name, in_specs=[ANY], out_specs=ANY,
        out_shape=jax.ShapeDtypeStruct((N_DEV,) + shard.shape, shard.dtype),
        scratch_shapes=[pltpu.SemaphoreType.DMA((7,)), pltpu.SemaphoreType.DMA((7,)), pltpu.SemaphoreType.DMA])(shard)


N_CHIPS = 4


def _exchange_sibling(blocks, name):
    def body(g_ref, got_ref, send_sems, recv_sems):
        x, y, c = lax.axis_index("x"), lax.axis_index("y"), lax.axis_index("c")
        sends = [pltpu.make_async_remote_copy(
            src_ref=g_ref.at[2 * q + (1 - c)], dst_ref=got_ref.at[q], send_sem=send_sems.at[q],
            recv_sem=recv_sems.at[q], device_id=(x, y, 1 - c), device_id_type=MESH) for q in range(N_CHIPS)]
        for cp in sends:
            cp.start()
        for cp in sends:
            cp.wait_recv()
        for cp in sends:
            cp.wait_send()

    return pl.pallas_call(
        body, name=name, in_specs=[ANY], out_specs=ANY,
        out_shape=jax.ShapeDtypeStruct((N_CHIPS,) + blocks.shape[1:], blocks.dtype),
        scratch_shapes=[pltpu.SemaphoreType.DMA((N_CHIPS,)), pltpu.SemaphoreType.DMA((N_CHIPS,))])(blocks)


def _exchange_chips(parts, name):
    def body(p_ref, out_ref, send_sems, recv_sems, local_sem):
        x, y, c = lax.axis_index("x"), lax.axis_index("y"), lax.axis_index("c")
        me = 2 * x + y

        def peer(k):
            return (1 - x if k & 2 else x, 1 - y if k & 1 else y)

        def copy(k):
            px, py = peer(k)
            return pltpu.make_async_remote_copy(
                src_ref=p_ref.at[2 * px + py], dst_ref=out_ref.at[me], send_sem=send_sems.at[k - 1],
                recv_sem=recv_sems.at[k - 1], device_id=(px, py, c), device_id_type=MESH)

        def arrival(k):
            px, py = peer(k)
            slot = out_ref.at[2 * px + py]
            return pltpu.make_async_remote_copy(
                src_ref=slot, dst_ref=slot, send_sem=send_sems.at[k - 1], recv_sem=recv_sems.at[k - 1],
                device_id=(px, py, c), device_id_type=MESH)

        mine = pltpu.make_async_copy(p_ref.at[me], out_ref.at[me], local_sem)
        mine.start()
        sends = [copy(k) for k in range(1, N_CHIPS)]
        for cp in sends:
            cp.start()
        for k in range(1, N_CHIPS):
            arrival(k).wait_recv()
        for cp in sends:
            cp.wait_send()
        mine.wait()

    return pl.pallas_call(
        body, name=name, in_specs=[ANY], out_specs=ANY,
        out_shape=jax.ShapeDtypeStruct(parts.shape, parts.dtype),
        scratch_shapes=[pltpu.SemaphoreType.DMA((N_CHIPS - 1,)), pltpu.SemaphoreType.DMA((N_CHIPS - 1,)),
                        pltpu.SemaphoreType.DMA])(parts)


def _add_pairs(blocks, got, core, *, tr, name):
    q, r, c = got.shape

    def body(core_ref, a_ref, b_ref, o_ref):
        o_ref[...] = (a_ref[...].astype(F32) + b_ref[...].astype(F32)).astype(o_ref.dtype)

    spec = pl.BlockSpec((q, tr, c), lambda i, core_ref: (0, i, 0))
    mine = pl.BlockSpec((q, None, tr, c), lambda i, core_ref: (0, core_ref[0], i, 0))
    return pl.pallas_call(
        body, name=name,
        grid_spec=pltpu.PrefetchScalarGridSpec(num_scalar_prefetch=1, grid=(r // tr,), in_specs=[mine, spec],
                                               out_specs=spec),
        out_shape=jax.ShapeDtypeStruct(got.shape, BF16), compiler_params=_cparams(("parallel",)))(
            core, blocks.reshape(q, 2, r, c), got)


def _sum_slots(slots, *, tr, name):
    n_slots, r, c = slots.shape

    def body(s_ref, o_ref):
        acc = s_ref[0].astype(F32)
        for s in range(1, n_slots):
            acc = acc + s_ref[s].astype(F32)
        o_ref[...] = acc

    return pl.pallas_call(
        body, name=name, grid=(r // tr,),
        in_specs=[pl.BlockSpec((n_slots, tr, c), lambda i: (0, i, 0))],
        out_specs=pl.BlockSpec((tr, c), lambda i: (i, 0)),
        out_shape=jax.ShapeDtypeStruct((r, c), F32),
        compiler_params=_cparams(("parallel",)))(slots)


def _adamw(w, g, m, v, name):
    shape = w.shape
    cols = shape[-1]
    args = [a.reshape(-1, cols) for a in (w, g, m, v)]
    rows = args[0].shape[0]
    tm = 256 if rows % 256 == 0 else rows

    def fn(w_t, g_t, m_t, v_t):
        m_n = ADAM_B1 * m_t + (1.0 - ADAM_B1) * g_t
        v_n = ADAM_B2 * v_t + (1.0 - ADAM_B2) * (g_t * g_t)
        m_hat = m_n / (1.0 - ADAM_B1 ** ADAM_STEP)
        v_hat = v_n / (1.0 - ADAM_B2 ** ADAM_STEP)
        delta = -ADAM_LR * (m_hat / (jnp.sqrt(v_hat) + ADAM_EPS) + ADAM_WD * w_t)
        return delta, m_n, v_n

    delta, m_n, v_n = _rowwise(fn, args, [], [(cols, F32)] * 3, tm=tm, name=name)
    return delta.reshape(shape), m_n.reshape(shape), v_n.reshape(shape)


def _local_step(x, positions, target, w):
    bsz, t, _ = x.shape
    n = bsz * t
    tm = 256
    mm = 512
    x2 = x.reshape(n, D_MODEL)
    tgt2 = target.reshape(n, D_MODEL)
    tabs_a = _rope_tables(positions, A_ROPE_THETA, A_ROT_DIM, 0)
    tabs_b = _rope_tables(positions, B_ROPE_THETA, B_ROPE, B_NOPE)
    ta2 = [a.reshape(n, LANES) for a in tabs_a]
    tb2 = [a.reshape(n, LANES) for a in tabs_b]

    w_a_in = w["a_w_in"]
    w_qkv, w_z = w_a_in[:, :A_QKV], w_a_in[:, A_QKV:]
    w_a_out = w["a_w_out"]
    w_down = w["kv_w_down"]
    w_down_p = jnp.zeros((D_MODEL, 3 * LANES), BF16).at[:, :B_KV_LORA].set(w_down[:, :B_KV_LORA])
    w_down_p = w_down_p.at[:, B_KV_LORA + B_NOPE:B_KV_LORA + B_QK_DIM].set(w_down[:, B_KV_LORA:])
    wu = w["kv_w_up"].reshape(B_KV_LORA, B_HEADS, B_NOPE + B_VDIM)
    w_upk = jnp.pad(wu[:, :, :B_NOPE], ((0, 0), (0, 0), (0, LANES - B_NOPE))).reshape(B_KV_LORA, B_KPAD)
    w_upv = wu[:, :, B_NOPE:].reshape(B_KV_LORA, B_HEADS * B_VDIM)
    w_b_in = w["b_w_in"]
    w_q_p = jnp.pad(w["b_w_q_up"].reshape(B_Q_LORA, B_HEADS, B_QK_DIM),
                    ((0, 0), (0, 0), (0, LANES - B_QK_DIM))).reshape(B_Q_LORA, B_KPAD)
    w_b_out = w["b_w_out"]

    def tab_extras(tabs2, rows):
        return [(a, (rows, LANES), lambda j, i, kk: (i, 0)) for a in tabs2]

    (hn_a,) = _rowwise(lambda xt, g: (_rms(xt, g),), [x2], [w["a_pre_norm"]], [(D_MODEL, BF16)],
                       tm=tm, name="a_pre_norm")

    def rope_epilogue(acc, o_ref, rows, c_ref, sp_ref, sm_ref):
        c, sp, sm = c_ref[rows, :], sp_ref[rows, :], sm_ref[rows, :]
        for h in range(acc.shape[1] // LANES):
            hs = slice(h * LANES, (h + 1) * LANES)
            o_ref[rows, hs] = _rope(acc[:, hs], c, sp, sm).astype(BF16)

    def to_group(a, d):
        if d == 1:
            return a
        return a.reshape(bsz, t // d, d, a.shape[-1]).transpose(0, 2, 1, 3).reshape(n, a.shape[-1])

    def from_group(a, d):
        if d == 1:
            return a
        return a.reshape(bsz, d, t // d, a.shape[-1]).transpose(0, 2, 1, 3).reshape(n, a.shape[-1])

    def rows_to_cols(r, d):
        return r.reshape(bsz, d, A_HEADS, t // d).transpose(0, 3, 1, 2).reshape(n, A_HEADS)

    def cols_to_rows(cc, d):
        return cc.reshape(bsz, t // d, d, A_HEADS).transpose(0, 2, 3, 1).reshape(bsz * d, A_HEADS, t // d)

    z_a = _matmul(hn_a, w_z, out_dtype=F32, tm=mm, tn=A_WIDTH, name="a_gate")
    hn_g, tabs_g, qk_g, v_g, o_g, lse_g = [], [], [], [], [], []
    for g, d in enumerate(A_DILATIONS):
        hn_g.append(to_group(hn_a, d))
        tabs_g.append([to_group(a, d) for a in ta2])
        w_g = w_qkv[:, g * 3 * A_WIDTH:(g + 1) * 3 * A_WIDTH]
        qk = _matmul(hn_g[g], w_g[:, :2 * A_WIDTH], out_dtype=BF16, tm=mm, tn=A_WIDTH, name=f"a_qk_g{g}",
                     epilogue=rope_epilogue, epilogue_rows=EPILOGUE_ROWS, extras=tab_extras(tabs_g[g], mm))
        v = _matmul(hn_g[g], w_g[:, 2 * A_WIDTH:], out_dtype=BF16, tm=mm, tn=A_WIDTH, name=f"a_v_g{g}")
        qk_g.append(qk.reshape(bsz * d, t // d, 2 * A_WIDTH))
        v_g.append(v.reshape(bsz * d, t // d, A_WIDTH))
        o, lse = _attn_a_fwd(qk_g[g], v_g[g], f"attn_a_fwd_g{g}")
        o_g.append(from_group(o.reshape(n, A_WIDTH), d))
        lse_g.append(rows_to_cols(lse, d))

    def merge_fn(o0, o1, o2, l0, l1, l2, z):
        lmax = jnp.maximum(jnp.maximum(l0, l1), l2)
        e0, e1, e2 = jnp.exp2(l0 - lmax), jnp.exp2(l1 - lmax), jnp.exp2(l2 - lmax)
        den = e0 + e1 + e2
        w0, w1, w2 = e0 / den, e1 / den, e2 / den
        parts = []
        for h in range(A_HEADS):
            hs = slice(h * A_HEAD_DIM, (h + 1) * A_HEAD_DIM)
            parts.append(w0[:, h:h + 1] * o0[:, hs] + w1[:, h:h + 1] * o1[:, hs] + w2[:, h:h + 1] * o2[:, hs])
        o = jnp.concatenate(parts, axis=1)
        return o * _silu(z), o, lmax + jnp.log2(den)

    y_a, o_a2, lse_a = _rowwise(merge_fn, [*o_g, *lse_g, z_a], [],
                                [(A_WIDTH, BF16), (A_WIDTH, F32), (A_HEADS, F32)], tm=tm, name="a_merge_gate")
    out_a = _matmul(y_a, w_a_out, out_dtype=F32, tm=mm, tn=D_MODEL, name="a_out")

    def mid_fn(xt, out, g_post, g_kv, g_b):
        h1 = xt + _rms(out, g_post)
        return h1, _rms(h1, g_kv), _rms(h1, g_b)

    h1, hn_kv, hn_b = _rowwise(mid_fn, [x2, out_a], [w["a_post_norm"], w["kv_norm"], w["b_pre_norm"]],
                               [(D_MODEL, F32), (D_MODEL, BF16), (D_MODEL, BF16)], tm=tm, name="a_post_norm")

    ckr = _matmul(hn_kv, w_down_p, out_dtype=F32, tm=mm, tn=3 * LANES, name="kv_down")

    def latent_fn(ck, c, sp, sm, g):
        return _rms(ck[:, :B_KV_LORA], g), _rope(ck[:, B_KV_LORA:], c, sp, sm)

    c_kv, k_rope = _rowwise(latent_fn, [ckr, *tb2], [w["kv_latent_norm"]], [(B_KV_LORA, BF16), (LANES, F32)],
                            tm=tm, name="kv_latent_norm")

    def kpad_epilogue(acc, o_ref, rows, kr_ref):
        kr = kr_ref[rows, :]
        for h in range(acc.shape[1] // LANES):
            hs = slice(h * LANES, (h + 1) * LANES)
            o_ref[rows, hs] = (acc[:, hs] + kr).astype(BF16)

    kpad = _matmul(c_kv, w_upk, out_dtype=BF16, tm=mm, tn=1024, name="kv_up_k", epilogue=kpad_epilogue,
                   epilogue_rows=EPILOGUE_ROWS,
                   extras=[(k_rope, (mm, LANES), lambda j, i, kk: (i, 0))])
    v_b = _matmul(c_kv, w_upv, out_dtype=BF16, tm=mm, tn=1024, name="kv_up_v")

    proj_b = _matmul(hn_b, w_b_in, out_dtype=F32, tm=mm, tn=w_b_in.shape[1], name="b_in")
    (c_q,) = _rowwise(lambda p, g: (_rms(p[:, :B_Q_LORA], g),), [proj_b], [w["b_q_norm"]], [(B_Q_LORA, BF16)],
                      tm=tm, name="b_q_norm")

    q_b = _matmul(c_q, w_q_p, out_dtype=BF16, tm=mm, tn=1024, name="b_q_up", epilogue=rope_epilogue,
                  epilogue_rows=EPILOGUE_ROWS,
                  extras=tab_extras(tb2, mm))
    q_b3, kpad3, v_b3 = q_b.reshape(bsz, t, B_KPAD), kpad.reshape(bsz, t, B_KPAD), v_b.reshape(bsz, t, -1)
    v4 = v_b3.reshape(bsz, t, B_HEADS, B_VDIM)
    vt1 = jnp.concatenate([v4, jnp.ones((bsz, t, B_HEADS, 1), BF16),
                           jnp.zeros((bsz, t, B_HEADS, LANES - B_VDIM - 1), BF16)], axis=3)
    vt1 = vt1.transpose(0, 2, 3, 1).reshape(bsz, B_KPAD, t)
    o_b, lse_b = _attn_b_fwd(q_b3, kpad3, vt1)
    o_b2 = o_b.reshape(n, -1)
    (y_b,) = _rowwise(lambda o, p: (o * _silu(p[:, B_Q_LORA:]),), [o_b2, proj_b], [], [(D_MODEL, BF16)],
                      tm=tm, name="b_gate_mul")
    out_b = _matmul(y_b, w_b_out, out_dtype=F32, tm=mm, tn=D_MODEL, name="b_out")

    def head_fn(h1t, out, tgt, g):
        e = h1t + _rms(out, g) - tgt
        loss_row = 0.5 * jnp.mean(e * e, axis=-1, keepdims=True)
        dh2 = e * (1.0 / D_MODEL)
        d_out, dg = _rms_bwd(out, g, dh2)
        return dh2, d_out, dg, jnp.broadcast_to(loss_row * (1.0 / LANES), (loss_row.shape[0], LANES))

    dh2, d_out_b, dg_b_post, loss_acc = _rowwise(
        head_fn, [h1, out_b, tgt2], [w["b_post_norm"]], [(D_MODEL, F32), (D_MODEL, BF16)], [D_MODEL, LANES],
        tm=tm, name="loss_head")

    dy_b = _matmul(d_out_b, w_b_out, tb=True, out_dtype=F32, tm=mm, tn=D_MODEL, name="b_out_dx")
    gw_b_out = _matmul(y_b, d_out_b, ta=True, out_dtype=F32, tm=mm, tn=D_MODEL, tk=2048, name="b_out_dw")

    def gate_b_bwd(dy, o, p):
        z = p[:, B_Q_LORA:]
        return dy * _silu(z), dy * o * _silu_grad(z)

    do_b, dz_b = _rowwise(gate_b_bwd, [dy_b, o_b2, proj_b], [], [(D_MODEL, BF16), (D_MODEL, F32)],
                          tm=tm, name="b_gate_bwd")
    do_b3 = do_b.reshape(bsz, t, -1)
    dq_b, dsum_b = _attn_b_dq(q_b3, kpad3, v_b3, do_b3, o_b, lse_b, tabs_b)
    dk_b, dv_b = _attn_b_dkv(q_b3, kpad3, v_b3, do_b3, lse_b, dsum_b)
    dq_b2, dk_b2, dv_b2 = dq_b.reshape(n, B_KPAD), dk_b.reshape(n, B_KPAD), dv_b.reshape(n, -1)

    dc_kv = _matmul(dk_b2, w_upk, tb=True, out_dtype=F32, tm=mm, tn=B_KV_LORA, name="kv_up_k_dx")
    dc_kv = _matmul(dv_b2, w_upv, tb=True, out_dtype=F32, tm=mm, tn=B_KV_LORA, name="kv_up_v_dx",
                    epilogue=_add_epilogue, extras=[(dc_kv, (mm, B_KV_LORA), lambda j, i, kk: (i, j))])
    gw_upk = _matmul(c_kv, dk_b2, ta=True, out_dtype=F32, tm=B_KV_LORA, tn=1024, tk=2048, name="kv_up_k_dw")
    gw_upv = _matmul(c_kv, dv_b2, ta=True, out_dtype=F32, tm=B_KV_LORA, tn=1024, tk=2048, name="kv_up_v_dw")

    def latent_bwd(ck, dck, dk, c, sp, sm, g):
        d1, dg = _rms_bwd(ck[:, :B_KV_LORA], g, dck)
        ksum = dk[:, :LANES].astype(F32)
        for h in range(1, B_HEADS):
            ksum = ksum + dk[:, h * LANES:(h + 1) * LANES].astype(F32)
        lane = lax.broadcasted_iota(jnp.int32, ksum.shape, 1)
        ksum = jnp.where((lane >= B_NOPE) & (lane < B_QK_DIM), ksum, 0.0)
        return jnp.concatenate([d1, _rope_t(ksum, c, sp, sm)], axis=1), dg

    dckr, dg_latent = _rowwise(latent_bwd, [ckr, dc_kv, dk_b2, *tb2], [w["kv_latent_norm"]],
                               [(3 * LANES, BF16)], [B_KV_LORA], tm=tm, name="kv_latent_bwd")
    dhn_kv = _matmul(dckr, w_down_p, tb=True, out_dtype=F32, tm=mm, tn=D_MODEL, name="kv_down_dx")
    gw_down_p = _matmul(hn_kv, dckr, ta=True, out_dtype=F32, tm=mm, tn=3 * LANES, tk=2048, name="kv_down_dw")

    dc_q = _matmul(dq_b2, w_q_p, tb=True, out_dtype=F32, tm=mm, tn=B_Q_LORA, name="b_q_up_dx")
    gw_q_p = _matmul(c_q, dq_b2, ta=True, out_dtype=F32, tm=B_Q_LORA, tn=1024, tk=2048, name="b_q_up_dw")

    def q_norm_bwd(p, dcq, dz, g):
        d1, dg = _rms_bwd(p[:, :B_Q_LORA], g, dcq)
        return jnp.concatenate([d1, dz], axis=1), dg

    dproj_b, dg_q_norm = _rowwise(q_norm_bwd, [proj_b, dc_q, dz_b], [w["b_q_norm"]],
                                  [(w_b_in.shape[1], BF16)], [B_Q_LORA], tm=tm, name="b_q_norm_bwd")
    dhn_b = _matmul(dproj_b, w_b_in, tb=True, out_dtype=F32, tm=mm, tn=D_MODEL, name="b_in_dx")
    gw_b_in = _matmul(hn_b, dproj_b, ta=True, out_dtype=F32, tm=mm, tn=w_b_in.shape[1], tk=2048, name="b_in_dw")

    def mid_bwd(h1t, dh2t, dkv, db, g_kv, g_b, g_post, out):
        dxa, ra = _rms_bwd(h1t, g_kv, dkv)
        dxb, rb = _rms_bwd(h1t, g_b, db)
        dh1 = dh2t + dxa + dxb
        d_out, rp = _rms_bwd(out, g_post, dh1)
        return dh1, d_out, ra, rb, rp

    def mid_bwd_fn(h1t, dh2t, dkv, db, out, g_kv, g_b, g_post):
        return mid_bwd(h1t, dh2t, dkv, db, g_kv, g_b, g_post, out)

    dh1, d_out_a, dg_kv, dg_b_pre, dg_a_post = _rowwise(
        mid_bwd_fn, [h1, dh2, dhn_kv, dhn_b, out_a], [w["kv_norm"], w["b_pre_norm"], w["a_post_norm"]],
        [(D_MODEL, F32), (D_MODEL, BF16)], [D_MODEL] * 3, tm=tm, name="mid_bwd")

    dy_a = _matmul(d_out_a, w_a_out, tb=True, out_dtype=F32, tm=mm, tn=A_WIDTH, name="a_out_dx")
    gw_a_out = _matmul(y_a, d_out_a, ta=True, out_dtype=F32, tm=mm, tn=D_MODEL, tk=2048, name="a_out_dw")

    def gate_a_bwd(dy, o, z):
        do = dy * _silu(z)
        prod = do * o
        lane = lax.broadcasted_iota(jnp.int32, (prod.shape[0], A_HEADS), 1)
        dsum = jnp.zeros((prod.shape[0], A_HEADS), F32)
        for h in range(A_HEADS):
            col = jnp.sum(prod[:, h * A_HEAD_DIM:(h + 1) * A_HEAD_DIM], axis=1, keepdims=True)
            dsum = jnp.where(lane == h, col, dsum)
        return do, dy * o * _silu_grad(z), dsum

    do_a, dz_a, dsum_a = _rowwise(gate_a_bwd, [dy_a, o_a2, z_a], [],
                                  [(A_WIDTH, BF16), (A_WIDTH, BF16), (A_HEADS, F32)], tm=tm, name="a_gate_bwd")
    dhn_a = _matmul(dz_a, w_z, tb=True, out_dtype=F32, tm=mm, tn=D_MODEL, name="a_gate_dx")
    gw_parts = []
    for g, d in enumerate(A_DILATIONS):
        s_n, ln = bsz * d, t // d
        dqkv = _attn_a_bwd(qk_g[g], v_g[g], to_group(do_a, d).reshape(s_n, ln, A_WIDTH), cols_to_rows(lse_a, d),
                           cols_to_rows(dsum_a, d), [a.reshape(s_n, ln, LANES) for a in tabs_g[g]],
                           f"attn_a_bwd_g{g}").reshape(n, 3 * A_WIDTH)
        w_g = w_qkv[:, g * 3 * A_WIDTH:(g + 1) * 3 * A_WIDTH]
        if d == 1:
            dhn_a = _matmul(dqkv, w_g, tb=True, out_dtype=F32, tm=mm, tn=D_MODEL, name=f"a_qkv_dx_g{g}",
                            epilogue=_add_epilogue, extras=[(dhn_a, (mm, D_MODEL), lambda j, i, kk: (i, j))])
        else:
            dhn_a = dhn_a + from_group(_matmul(dqkv, w_g, tb=True, out_dtype=F32, tm=mm, tn=D_MODEL,
                                               name=f"a_qkv_dx_g{g}"), d)
        gw_parts.append(_matmul(hn_g[g], dqkv, ta=True, out_dtype=F32, tm=mm, tn=1024, tk=2048,
                                name=f"a_qkv_dw_g{g}"))
    gw_parts.append(_matmul(hn_a, dz_a, ta=True, out_dtype=F32, tm=mm, tn=1024, tk=2048, name="a_gate_dw"))
    gw_a_in = jnp.concatenate(gw_parts, axis=1)

    def first_bwd(xt, dhn, dh1t, g):
        dx, dg = _rms_bwd(xt, g, dhn)
        return dh1t + dx, dg

    grad_x, dg_a_pre = _rowwise(first_bwd, [x2, dhn_a, dh1], [w["a_pre_norm"]], [(D_MODEL, F32)], [D_MODEL],
                                tm=tm, name="a_pre_norm_bwd")

    gw_down = jnp.concatenate([gw_down_p[:, :B_KV_LORA], gw_down_p[:, B_KV_LORA + B_NOPE:B_KV_LORA + B_QK_DIM]], axis=1)
    gw_up = jnp.concatenate([gw_upk.reshape(B_KV_LORA, B_HEADS, LANES)[:, :, :B_NOPE],
                             gw_upv.reshape(B_KV_LORA, B_HEADS, B_VDIM)], axis=2).reshape(B_KV_LORA, -1)
    gw_q_up = gw_q_p.reshape(B_Q_LORA, B_HEADS, LANES)[:, :, :B_QK_DIM].reshape(B_Q_LORA, -1)
    grads = {"a_w_in": gw_a_in, "a_w_out": gw_a_out, "kv_w_down": gw_down, "kv_w_up": gw_up,
             "b_w_in": gw_b_in, "b_w_q_up": gw_q_up, "b_w_out": gw_b_out}
    gains = {"a_pre_norm": dg_a_pre, "a_post_norm": dg_a_post, "kv_norm": dg_kv, "kv_latent_norm": dg_latent,
             "b_pre_norm": dg_b_pre, "b_q_norm": dg_q_norm, "b_post_norm": dg_b_post}
    gains = {k: jnp.sum(a, axis=0) for k, a in gains.items()}
    return jnp.sum(loss_acc), grad_x.reshape(bsz, t, D_MODEL), grads, gains


WEIGHT_ORDER = ("a_pre_norm", "a_w_in", "a_w_out", "a_post_norm", "kv_norm", "kv_w_down", "kv_latent_norm",
                "kv_w_up", "b_pre_norm", "b_w_in", "b_q_norm", "b_w_q_up", "b_w_out", "b_post_norm")
MATRICES = (("a_w_in", 1024, 10240, 1), ("a_w_out", 1024, 1024, 0), ("kv_w_down", 1024, 288, 0),
            ("kv_w_up", 256, 2048, 1), ("b_w_in", 1024, 1408, 1), ("b_w_q_up", 384, 1536, 1),
            ("b_w_out", 1024, 1024, 0))
SHARDED_GAINS = ("a_pre_norm", "a_post_norm")
GAIN_WIDTHS = (("a_pre_norm", 1024), ("a_post_norm", 1024), ("kv_norm", 1024), ("kv_latent_norm", 256),
               ("b_pre_norm", 1024), ("b_q_norm", 384), ("b_post_norm", 1024))
GAIN_ROWS = 48


def _shard_rows(rows, cols):
    return rows * cols // (N_DEV * LANES)


def _whole_from_blocks(blocks, rows, cols, axis):
    if axis == 1:
        return blocks.reshape(N_DEV, rows, cols // N_DEV).transpose(1, 0, 2).reshape(rows, cols)
    return blocks.reshape(rows, cols)


def _blocks_from_whole(whole, rows, cols, axis):
    if axis == 1:
        whole = whole.reshape(rows, N_DEV, cols // N_DEV).transpose(1, 0, 2)
    return whole.reshape(N_DEV, -1, LANES)


def kernel(x, positions, a_pre_norm, a_w_in, a_w_out, a_post_norm, kv_norm, kv_w_down, kv_latent_norm, kv_w_up, b_pre_norm, b_w_in, b_q_norm, b_w_q_up, b_w_out, b_post_norm, loss_target, m_a_pre_norm, m_a_w_in, m_a_w_out, m_a_post_norm, m_kv_norm, m_kv_w_down, m_kv_latent_norm, m_kv_w_up, m_b_pre_norm, m_b_w_in, m_b_q_norm, m_b_w_q_up, m_b_w_out, m_b_post_norm, v_a_pre_norm, v_a_w_in, v_a_w_out, v_a_post_norm, v_kv_norm, v_kv_w_down, v_kv_latent_norm, v_kv_w_up, v_b_pre_norm, v_b_w_in, v_b_q_norm, v_b_w_q_up, v_b_w_out, v_b_post_norm):
    weights = dict(a_pre_norm=a_pre_norm, a_w_in=a_w_in, a_w_out=a_w_out, a_post_norm=a_post_norm, kv_norm=kv_norm,
                   kv_w_down=kv_w_down, kv_latent_norm=kv_latent_norm, kv_w_up=kv_w_up, b_pre_norm=b_pre_norm,
                   b_w_in=b_w_in, b_q_norm=b_q_norm, b_w_q_up=b_w_q_up, b_w_out=b_w_out, b_post_norm=b_post_norm)
    m_in = dict(a_pre_norm=m_a_pre_norm, a_w_in=m_a_w_in, a_w_out=m_a_w_out, a_post_norm=m_a_post_norm,
                kv_norm=m_kv_norm, kv_w_down=m_kv_w_down, kv_latent_norm=m_kv_latent_norm, kv_w_up=m_kv_w_up,
                b_pre_norm=m_b_pre_norm, b_w_in=m_b_w_in, b_q_norm=m_b_q_norm, b_w_q_up=m_b_w_q_up,
                b_w_out=m_b_w_out, b_post_norm=m_b_post_norm)
    v_in = dict(a_pre_norm=v_a_pre_norm, a_w_in=v_a_w_in, a_w_out=v_a_w_out, a_post_norm=v_a_post_norm,
                kv_norm=v_kv_norm, kv_w_down=v_kv_w_down, kv_latent_norm=v_kv_latent_norm, kv_w_up=v_kv_w_up,
                b_pre_norm=v_b_pre_norm, b_w_in=v_b_w_in, b_q_norm=v_b_q_norm, b_w_q_up=v_b_w_q_up,
                b_w_out=v_b_w_out, b_post_norm=v_b_post_norm)
    me = 4 * lax.axis_index("x") + 2 * lax.axis_index("y") + lax.axis_index("c")

    flat = jnp.concatenate([weights[name].astype(BF16).reshape(-1, LANES) for name, _, _, _ in MATRICES], axis=0)
    gathered = _all_gather(flat, "gather_weights")
    whole = {}
    off = 0
    for name, rows, cols, axis in MATRICES:
        nr = _shard_rows(rows, cols)
        whole[name] = _whole_from_blocks(gathered[:, off:off + nr], rows, cols, axis)
        off += nr
    gain_shard = jnp.concatenate([weights[name].reshape(1, LANES) for name in SHARDED_GAINS]
                                 + [jnp.zeros((8 - len(SHARDED_GAINS), LANES), F32)], axis=0)
    gain_blocks = _all_gather(gain_shard, "gather_gains")
    for i, name in enumerate(SHARDED_GAINS):
        whole[name] = gain_blocks[:, i, :].reshape(1, D_MODEL)
    for name in ("kv_norm", "kv_latent_norm", "b_pre_norm", "b_q_norm", "b_post_norm"):
        whole[name] = weights[name].reshape(1, -1)

    loss_part, grad_x, grads, gains = _local_step(x, positions, loss_target, whole)

    blocks = jnp.concatenate([_blocks_from_whole(grads[name], rows, cols, axis).astype(BF16)
                              for name, rows, cols, axis in MATRICES], axis=1)
    got = _exchange_sibling(blocks, "scatter_grads_core")
    core = lax.axis_index("c").astype(jnp.int32).reshape(1)
    landed = _exchange_chips(_add_pairs(blocks, got, core, tr=2512, name="add_core_grads"), "scatter_grads_chip")
    summed = _sum_slots(landed, tr=2512, name="sum_grads")
    grad_out = {}
    off = 0
    for name, rows, cols, axis in MATRICES:
        nr = _shard_rows(rows, cols)
        grad_out[name] = summed[off:off + nr].reshape(weights[name].shape)
        off += nr

    vec = jnp.concatenate([gains[name] for name, _ in GAIN_WIDTHS] + [jnp.full((LANES,), loss_part, F32)])
    vec = jnp.pad(vec, (0, GAIN_ROWS * LANES - vec.shape[0])).reshape(GAIN_ROWS, LANES)
    total = _sum_slots(_all_gather(vec, "gather_gain_grads"), tr=GAIN_ROWS, name="sum_gain_grads").reshape(-1)
    off = 0
    for name, width in GAIN_WIDTHS:
        g = total[off:off + width]
        if name in SHARDED_GAINS:
            g = lax.dynamic_slice(g, (me * LANES,), (LANES,))
        grad_out[name] = g.reshape(weights[name].shape)
        off += width
    loss = total[off]

    deltas, new_m, new_v = {}, {}, {}
    for name in WEIGHT_ORDER:
        deltas[name], new_m[name], new_v[name] = _adamw(weights[name], grad_out[name], m_in[name], v_in[name],
                                                        "adamw_" + name)
    return (loss, grad_x, *[grad_out[k] for k in WEIGHT_ORDER], *[deltas[k] for k in WEIGHT_ORDER],
            *[new_m[k] for k in WEIGHT_ORDER], *[new_v[k] for k in WEIGHT_ORDER])
```

```python
import jax
import jax.numpy as jnp
from jax import lax
from jax.experimental import pallas as pl
from jax.experimental.pallas import tpu as pltpu

F32 = jnp.float32
BF16 = jnp.bfloat16

N_DEV = 8
D_MODEL = 1024
NORM_EPS = 1e-6
A_GROUPS = 3
A_DILATIONS = (1, 4, 16)
A_HEADS = 8
A_HEAD_DIM = 128
A_WIDTH = 1024
A_ROT_DIM = 32
A_ROPE_THETA = 500000.0
A_QKV = A_GROUPS * 3 * A_WIDTH
B_HEADS = 16
B_NOPE = 64
B_ROPE = 32
B_QK_DIM = 96
B_VDIM = 64
B_Q_LORA = 384
B_KV_LORA = 256
B_ROPE_THETA = 10000.0
B_KPAD = B_HEADS * 128
ADAM_LR = 0.001
ADAM_B1 = 0.9
ADAM_B2 = 0.999
ADAM_EPS = 1e-08
ADAM_WD = 0.01
ADAM_STEP = 10

LANES = 128
BAND = 128
EPILOGUE_ROWS = 128
NEG = -1e30
VMEM_LIMIT = 56 * 1024 * 1024
MESH = pl.DeviceIdType.MESH


def _cparams(sem):
    return pltpu.CompilerParams(dimension_semantics=sem, vmem_limit_bytes=VMEM_LIMIT)


def _rowwise(fn, rows, bcast, outs, accs=(), *, tm, name):
    n = rows[0].shape[0]
    nr, nb, no = len(rows), len(bcast), len(outs)

    def body(*refs):
        res = fn(*[r[...] for r in refs[:nr + nb]])
        out_refs = refs[nr + nb:nr + nb + no]
        acc_refs = refs[nr + nb + no:]
        for r, v in zip(out_refs, res[:no]):
            r[...] = v.astype(r.dtype)
        if acc_refs:
            @pl.when(pl.program_id(0) == 0)
            def _():
                for r in acc_refs:
                    r[...] = jnp.zeros_like(r)
            for r, v in zip(acc_refs, res[no:]):
                r[...] += v.reshape(tm // 8, 8, v.shape[-1]).sum(axis=0)

    in_specs = [pl.BlockSpec((tm, a.shape[1]), lambda i: (i, 0)) for a in rows]
    in_specs += [pl.BlockSpec(a.shape, lambda i: (0, 0)) for a in bcast]
    out_specs = [pl.BlockSpec((tm, c), lambda i: (i, 0)) for c, _ in outs]
    out_specs += [pl.BlockSpec((8, c), lambda i: (0, 0)) for c in accs]
    out_shape = [jax.ShapeDtypeStruct((n, c), dt) for c, dt in outs]
    out_shape += [jax.ShapeDtypeStruct((8, c), F32) for c in accs]
    return pl.pallas_call(
        body, name=name, grid=(n // tm,), in_specs=in_specs, out_specs=out_specs, out_shape=out_shape,
        compiler_params=_cparams(("arbitrary",)))(*rows, *bcast)


def _matmul(a, b, *, out_dtype, tm, tn, tk=None, name, epilogue=None, extras=(), ta=False, tb=False,
            epilogue_rows=None):
    epilogue_rows = epilogue_rows or tm
    k, m = a.shape[::-1] if not ta else a.shape
    n = b.shape[0] if tb else b.shape[1]
    tk = tk or k
    nk = k // tk
    ne = len(extras)
    dot = _dot_tn if ta else (_dot_nt if tb else _dot)
    assert epilogue is None or (nk == 1 and not ta)

    def body(*refs):
        a_ref, b_ref = refs[:2]
        ex = refs[2:2 + ne]
        o_ref = refs[2 + ne]
        if epilogue is not None:
            b_tile = b_ref[...].astype(BF16)
            for r0 in range(0, tm, epilogue_rows):
                rows = slice(r0, r0 + epilogue_rows)
                epilogue(dot(a_ref[rows, :].astype(BF16), b_tile), o_ref, rows, *ex)
            return
        part = dot(a_ref[...].astype(BF16), b_ref[...].astype(BF16))
        if nk == 1:
            o_ref[...] = part.astype(o_ref.dtype)
        else:
            acc_ref = refs[-1]
            kk = pl.program_id(2)

            @pl.when(kk == 0)
            def _():
                acc_ref[...] = part

            @pl.when(kk > 0)
            def _():
                acc_ref[...] += part

            @pl.when(kk == nk - 1)
            def _():
                o_ref[...] = acc_ref[...].astype(o_ref.dtype)

    a_spec = pl.BlockSpec((tk, tm), lambda j, i, kk: (kk, i)) if ta else pl.BlockSpec((tm, tk), lambda j, i, kk: (i, kk))
    b_spec = pl.BlockSpec((tn, tk), lambda j, i, kk: (j, kk)) if tb else pl.BlockSpec((tk, tn), lambda j, i, kk: (kk, j))
    in_specs = [a_spec, b_spec] + [pl.BlockSpec(bs, im) for _, bs, im in extras]
    return pl.pallas_call(
        body, name=name, grid=(n // tn, m // tm, nk), in_specs=in_specs,
        out_specs=pl.BlockSpec((tm, tn), lambda j, i, kk: (i, j)),
        out_shape=jax.ShapeDtypeStruct((m, n), out_dtype),
        scratch_shapes=[pltpu.VMEM((tm, tn), F32)] if nk > 1 else [],
        compiler_params=_cparams(("parallel", "parallel", "arbitrary")))(a, b, *[e[0] for e in extras])


def _add_epilogue(acc, o_ref, rows, prev_ref):
    o_ref[rows, :] = (acc + prev_ref[rows, :]).astype(o_ref.dtype)


def _rope(x, c, sp, sm):
    return x * c + pltpu.roll(x, 16, 1) * sp + pltpu.roll(x, LANES - 16, 1) * sm


def _rope_t(dy, c, sp, sm):
    return dy * c + pltpu.roll(dy * sp, LANES - 16, 1) + pltpu.roll(dy * sm, 16, 1)


def _rope_tables(positions, theta, rot_dim, lane0):
    half = rot_dim // 2
    inv_freq = 1.0 / (theta ** (jnp.arange(half, dtype=F32) * (2.0 / rot_dim)))
    freq = jnp.concatenate([jnp.zeros((lane0,), F32), inv_freq, inv_freq,
                            jnp.zeros((LANES - lane0 - rot_dim,), F32)])
    ang = positions.astype(F32)[:, None] * freq
    cos, sin = jnp.cos(ang), jnp.sin(ang)
    lane = jnp.arange(LANES) - lane0
    first, second = (lane >= 0) & (lane < half), (lane >= half) & (lane < rot_dim)
    return (jnp.where(first | second, cos, 1.0), jnp.where(second, sin, 0.0), jnp.where(first, -sin, 0.0))


def _rms(x, g):
    xf = x.astype(F32)
    return xf * lax.rsqrt(jnp.mean(xf * xf, axis=-1, keepdims=True) + NORM_EPS) * g


def _rms_bwd(x, g, dy):
    xf = x.astype(F32)
    rstd = lax.rsqrt(jnp.mean(xf * xf, axis=-1, keepdims=True) + NORM_EPS)
    xhat = xf * rstd
    dxhat = dy * g
    dx = rstd * (dxhat - xhat * jnp.mean(dxhat * xhat, axis=-1, keepdims=True))
    return dx, dy * xhat


def _silu(z):
    return z * jax.nn.sigmoid(z)


def _silu_grad(z):
    s = jax.nn.sigmoid(z)
    return s * (1.0 + z * (1.0 - s))


def _dot_nt(a, b):
    return lax.dot_general(a, b, (((1,), (1,)), ((), ())), preferred_element_type=F32)


def _dot_tn(a, b):
    return lax.dot_general(a, b, (((0,), (0,)), ((), ())), preferred_element_type=F32)


def _dot(a, b):
    return jnp.dot(a, b, preferred_element_type=F32)


A_SCALE = A_HEAD_DIM ** -0.5
LOG2E = 1.4426950408889634
A_C2 = A_SCALE * LOG2E
A_HEAD_GROUP = 4


def _attn_a_fwd(qk, v, name):
    s_n, ln, _ = qk.shape
    nb = ln // BAND
    blk = (None, BAND, A_WIDTH)

    def body(q_ref, kc_ref, kp_ref, vc_ref, vp_ref, o_ref, lse_ref):
        kpos = lax.broadcasted_iota(jnp.int32, (2 * BAND, BAND), 0)
        qpos = lax.broadcasted_iota(jnp.int32, (2 * BAND, BAND), 1) + BAND
        first_key = jnp.where(pl.program_id(1) > 0, 0, BAND)
        mask = (kpos <= qpos) & (kpos >= qpos - BAND) & (kpos >= first_key)
        rows = []
        for h0 in range(0, A_HEADS, A_HEAD_GROUP):
            hss = [slice(h * A_HEAD_DIM, (h + 1) * A_HEAD_DIM) for h in range(h0, h0 + A_HEAD_GROUP)]
            sts = [_dot_nt(jnp.concatenate([kp_ref[:, hs], kc_ref[:, hs]], axis=0), q_ref[:, hs]) for hs in hss]
            ps, ls = [], []
            for st in sts:
                st = jnp.where(mask, st * A_C2, NEG)
                m = jnp.max(st, axis=0, keepdims=True)
                p = jnp.exp2(st - m)
                l_row = jnp.sum(p, axis=0, keepdims=True)
                ps.append(p.astype(BF16))
                ls.append(l_row)
                rows.append(m + jnp.log2(l_row))
            ots = [_dot_tn(jnp.concatenate([vp_ref[:, hs], vc_ref[:, hs]], axis=0), p) for hs, p in zip(hss, ps)]
            for hs, o_t, l_row in zip(hss, ots, ls):
                o_ref[:, hs] = (o_t / l_row).T.astype(BF16)
        lse_ref[...] = jnp.concatenate(rows, axis=0)

    def col(c, off):
        return lambda s, l: (s, jnp.maximum(l + off, 0), c)

    return pl.pallas_call(
        body, name=name, grid=(s_n, nb),
        in_specs=[pl.BlockSpec(blk, col(0, 0)), pl.BlockSpec(blk, col(1, 0)), pl.BlockSpec(blk, col(1, -1)),
                  pl.BlockSpec(blk, col(0, 0)), pl.BlockSpec(blk, col(0, -1))],
        out_specs=[pl.BlockSpec(blk, lambda s, l: (s, l, 0)),
                   pl.BlockSpec((None, A_HEADS, BAND), lambda s, l: (s, 0, l))],
        out_shape=[jax.ShapeDtypeStruct((s_n, ln, A_WIDTH), BF16), jax.ShapeDtypeStruct((s_n, A_HEADS, ln), F32)],
        compiler_params=_cparams(("parallel", "arbitrary")))(qk, qk, qk, v, v)


def _attn_a_bwd(qk, v, do, lse2, dsum, tabs, name):
    s_n, ln, _ = qk.shape
    nb = ln // BAND
    blk = (None, BAND, A_WIDTH)

    def body(q_ref, qn_ref, kc_ref, kp_ref, vc_ref, vp_ref, do_ref, don_ref, lse_ref, lsen_ref, ds_ref, dsn_ref,
             c_ref, sp_ref, sm_ref, out_ref):
        l_idx = pl.program_id(1)
        kpos = lax.broadcasted_iota(jnp.int32, (2 * BAND, BAND), 0)
        qpos = lax.broadcasted_iota(jnp.int32, (2 * BAND, BAND), 1) + BAND
        first_key = jnp.where(l_idx > 0, 0, BAND)
        mask_q = (kpos <= qpos) & (kpos >= qpos - BAND) & (kpos >= first_key)
        kpos2 = lax.broadcasted_iota(jnp.int32, (BAND, 2 * BAND), 0)
        qpos2 = lax.broadcasted_iota(jnp.int32, (BAND, 2 * BAND), 1)
        last_query = jnp.where(l_idx < nb - 1, 2 * BAND, BAND)
        mask_k = (kpos2 <= qpos2) & (kpos2 >= qpos2 - BAND) & (qpos2 < last_query)
        c, sp, sm = c_ref[...], sp_ref[...], sm_ref[...]
        lse_q, ds_q = lse_ref[...], ds_ref[...]
        lse_k = jnp.concatenate([lse_q, lsen_ref[...]], axis=1)
        ds_k = jnp.concatenate([ds_q, dsn_ref[...]], axis=1)
        for h0 in range(0, A_HEADS, A_HEAD_GROUP):
            hl = list(range(h0, h0 + A_HEAD_GROUP))
            hss = [slice(h * A_HEAD_DIM, (h + 1) * A_HEAD_DIM) for h in hl]
            k2s = [jnp.concatenate([kp_ref[:, hs], kc_ref[:, hs]], axis=0) for hs in hss]
            v2s = [jnp.concatenate([vp_ref[:, hs], vc_ref[:, hs]], axis=0) for hs in hss]
            q2s = [jnp.concatenate([q_ref[:, hs], qn_ref[:, hs]], axis=0) for hs in hss]
            do2s = [jnp.concatenate([do_ref[:, hs], don_ref[:, hs]], axis=0) for hs in hss]
            sts = [_dot_nt(k2, q_ref[:, hs]) for k2, hs in zip(k2s, hss)]
            dpts = [_dot_nt(v2, do_ref[:, hs]) for v2, hs in zip(v2s, hss)]
            st2s = [_dot_nt(kc_ref[:, hs], q2) for q2, hs in zip(q2s, hss)]
            dpt2s = [_dot_nt(vc_ref[:, hs], do2) for do2, hs in zip(do2s, hss)]
            dsts, dst2s, p2s = [], [], []
            for i, h in enumerate(hl):
                p = jnp.exp2(jnp.where(mask_q, sts[i] * A_C2, NEG) - lse_q[h:h + 1])
                dsts.append((p * (dpts[i] - ds_q[h:h + 1]) * A_SCALE).astype(BF16))
                p2 = jnp.exp2(jnp.where(mask_k, st2s[i] * A_C2, NEG) - lse_k[h:h + 1])
                dst2s.append((p2 * (dpt2s[i] - ds_k[h:h + 1]) * A_SCALE).astype(BF16))
                p2s.append(p2.astype(BF16))
            dqs = [_dot_tn(dsts[i], k2s[i]) for i in range(A_HEAD_GROUP)]
            dks = [_dot(dst2s[i], q2s[i]) for i in range(A_HEAD_GROUP)]
            dvs = [_dot(p2s[i], do2s[i]) for i in range(A_HEAD_GROUP)]
            for i, h in enumerate(hl):
                out_ref[:, hss[i]] = _rope_t(dqs[i], c, sp, sm).astype(BF16)
                out_ref[:, A_WIDTH + h * A_HEAD_DIM:A_WIDTH + (h + 1) * A_HEAD_DIM] = _rope_t(dks[i], c, sp, sm).astype(BF16)
                out_ref[:, 2 * A_WIDTH + h * A_HEAD_DIM:2 * A_WIDTH + (h + 1) * A_HEAD_DIM] = dvs[i].astype(BF16)

    def col(c, off):
        return lambda s, l: (s, jnp.clip(l + off, 0, nb - 1), c)

    def row(off):
        return pl.BlockSpec((None, A_HEADS, BAND), lambda s, l: (s, 0, jnp.clip(l + off, 0, nb - 1)))

    tspec = pl.BlockSpec((None, BAND, LANES), lambda s, l: (s, l, 0))
    in_specs = [pl.BlockSpec(blk, col(0, 0)), pl.BlockSpec(blk, col(0, 1)),
                pl.BlockSpec(blk, col(1, 0)), pl.BlockSpec(blk, col(1, -1)),
                pl.BlockSpec(blk, col(0, 0)), pl.BlockSpec(blk, col(0, -1)),
                pl.BlockSpec(blk, col(0, 0)), pl.BlockSpec(blk, col(0, 1)),
                row(0), row(1), row(0), row(1), tspec, tspec, tspec]
    return pl.pallas_call(
        body, name=name, grid=(s_n, nb), in_specs=in_specs,
        out_specs=pl.BlockSpec((None, BAND, 3 * A_WIDTH), lambda s, l: (s, l, 0)),
        out_shape=jax.ShapeDtypeStruct((s_n, ln, 3 * A_WIDTH), BF16),
        compiler_params=_cparams(("parallel", "arbitrary")))(
            qk, qk, qk, qk, v, v, do, do, lse2, lse2, dsum, dsum, *tabs)


B_TQ = 256
B_SCALE = B_QK_DIM ** -0.5
B_C2 = B_SCALE * LOG2E


def _b_row_tile(tq):
    return pl.BlockSpec((None, None, 8, tq), lambda b, j, i: (b, j, 0, i))


def _b_specs(t, tq, tk):
    pair_tile = pl.BlockSpec((None, tq, 2 * LANES), lambda b, j, i: (b, i, j))
    pair_full = pl.BlockSpec((None, t, 2 * LANES), lambda b, j, i: (b, 0, j))
    one_tile = pl.BlockSpec((None, tk, LANES), lambda b, j, i: (b, i, j))
    one_full = pl.BlockSpec((None, t, LANES), lambda b, j, i: (b, 0, j))
    tab_tile = pl.BlockSpec((None, tq, LANES), lambda b, j, i: (b, i, 0))
    return pair_tile, pair_full, one_tile, one_full, tab_tile


def _key_le_query(kb, qb, tk, tq):
    kpos = kb * tk + lax.broadcasted_iota(jnp.int32, (tk, tq), 0)
    qpos = qb * tq + lax.broadcasted_iota(jnp.int32, (tk, tq), 1)
    return kpos <= qpos


def _attn_b_fwd(q, kpad, vt1):
    bsz, t, _ = q.shape
    tq = tk = B_TQ
    nq = t // tq

    def body(q_ref, k_ref, vt_ref, o_ref, lse_ref):
        qblk = pl.program_id(2)
        qs = [q_ref[:, hh * LANES:(hh + 1) * LANES] for hh in range(2)]

        def scores(kb):
            start = pl.multiple_of(kb * tk, tk)
            return [_dot_nt(k_ref[pl.ds(start, tk), hh * LANES:(hh + 1) * LANES], qs[hh]) for hh in range(2)]

        def pv(kb, ps):
            start = pl.multiple_of(kb * tk, tk)
            return [_dot(vt_ref[hh * LANES:(hh + 1) * LANES, pl.ds(start, tk)], ps[hh]) for hh in range(2)]

        def softmax(ss, ms, accs, kb, masked):
            out_m, out_acc, out_p = [], [], []
            for hh in range(2):
                s = ss[hh] * B_C2
                if masked:
                    s = jnp.where(_key_le_query(kb, qblk, tk, tq), s, NEG)
                m_new = jnp.maximum(ms[hh], jnp.max(s, axis=0, keepdims=True))
                out_acc.append(jnp.exp2(ms[hh] - m_new) * accs[hh])
                out_p.append(jnp.exp2(s - m_new).astype(BF16))
                out_m.append(m_new)
            return out_m, out_acc, out_p

        def step(kb, carry):
            ss, ps, ms, accs = carry
            pvs = pv(jnp.maximum(kb - 1, 0), ps)
            ss_next = scores(kb + 1)
            accs = [accs[hh] + pvs[hh] for hh in range(2)]
            ms, accs, ps = softmax(ss, ms, accs, kb, False)
            return (ss_next, ps, ms, accs)

        init = (scores(0), [jnp.zeros((tk, tq), BF16)] * 2, [jnp.full((1, tq), NEG, F32)] * 2,
                [jnp.zeros((LANES, tq), F32)] * 2)
        ss, ps, ms, accs = lax.fori_loop(0, qblk, step, init)
        pvs = pv(jnp.maximum(qblk - 1, 0), ps)
        accs = [accs[hh] + pvs[hh] for hh in range(2)]
        ms, accs, ps = softmax(ss, ms, accs, qblk, True)
        pvs = pv(qblk, ps)
        accs = [accs[hh] + pvs[hh] for hh in range(2)]
        ls = [accs[hh][B_VDIM:B_VDIM + 1] for hh in range(2)]
        o_t = jnp.concatenate([accs[0][:B_VDIM] / ls[0], accs[1][:B_VDIM] / ls[1]], axis=0)
        o_ref[...] = o_t.T
        lse_ref[...] = jnp.concatenate([ms[0] + jnp.log2(ls[0]), ms[1] + jnp.log2(ls[1]),
                                        jnp.zeros((6, tq), F32)], axis=0)

    pair_tile, pair_full, one_tile, _, _ = _b_specs(t, tq, tk)
    vt_spec = pl.BlockSpec((None, 2 * LANES, t), lambda b, j, i: (b, j, 0))
    return pl.pallas_call(
        body, name="attn_b_fwd", grid=(bsz, B_HEADS // 2, nq),
        in_specs=[pair_tile, pair_full, vt_spec],
        out_specs=[one_tile, _b_row_tile(tq)],
        out_shape=[jax.ShapeDtypeStruct((bsz, t, B_HEADS * B_VDIM), F32),
                   jax.ShapeDtypeStruct((bsz, B_HEADS // 2, 8, t), F32)],
        compiler_params=_cparams(("parallel", "parallel", "arbitrary")))(q, kpad, vt1)


def _attn_b_bwd(q, kpad, v, do, o, lse2, tabs):
    bsz, t, _ = q.shape
    tq = tk = B_TQ
    nq = t // tq

    def body(q_ref, k_ref, v_ref, do_ref, o_ref, lse_ref, c_ref, sp_ref, sm_ref, dq_ref, dk_ref, dv_ref,
             dqt_scr, dsum_scr):
        lane = lax.broadcasted_iota(jnp.int32, (tk, LANES), 1)
        sel_lane = lax.broadcasted_iota(jnp.int32, (8, LANES), 1)
        sel_row = lax.broadcasted_iota(jnp.int32, (8, LANES), 0)
        sel = jnp.where((sel_lane < B_VDIM) == (sel_row == 0), 1.0, 0.0)
        sel = jnp.where(sel_row < 2, sel, 0.0).astype(BF16)

        def rows(blk):
            return pl.ds(pl.multiple_of(blk * tq, tq), tq)

        def dsum_step(qb, carry):
            prod = do_ref[rows(qb), :].astype(F32) * o_ref[rows(qb), :]
            hi = prod.astype(BF16)
            lo = (prod - hi.astype(F32)).astype(BF16)
            dsum_scr[:, rows(qb)] = _dot_nt(sel, hi) + _dot_nt(sel, lo)
            return carry

        lax.fori_loop(0, nq, dsum_step, 0)
        dqt_scr[...] = jnp.zeros_like(dqt_scr)

        def kv_step(kb, carry):
            ks = [k_ref[rows(kb), hh * LANES:(hh + 1) * LANES] for hh in range(2)]
            vb = v_ref[rows(kb), :]
            zero = jnp.zeros_like(vb)
            vs = [jnp.where(lane < B_VDIM, vb, zero), jnp.where(lane < B_VDIM, zero, vb)]

            def make_step(masked):
                def step(qb, acc):
                    qs = [q_ref[rows(qb), hh * LANES:(hh + 1) * LANES] for hh in range(2)]
                    do_b = do_ref[rows(qb), :]
                    ss = [_dot_nt(ks[hh], qs[hh]) for hh in range(2)]
                    dps = [_dot_nt(vs[hh], do_b) for hh in range(2)]
                    pbs, dss = [], []
                    for hh in range(2):
                        s = ss[hh] * B_C2
                        if masked:
                            s = jnp.where(_key_le_query(kb, qb, tk, tq), s, NEG)
                        p = jnp.exp2(s - lse_ref[hh:hh + 1, rows(qb)])
                        dss.append((p * (dps[hh] - dsum_scr[hh:hh + 1, rows(qb)]) * B_SCALE).astype(BF16))
                        pbs.append(p.astype(BF16))
                    for hh in range(2):
                        dqt_scr[hh, :, rows(qb)] += _dot_tn(ks[hh], dss[hh])
                    return (acc[0] + _dot(dss[0], qs[0]), acc[1] + _dot(dss[1], qs[1]),
                            acc[2] + _dot(pbs[0], do_b), acc[3] + _dot(pbs[1], do_b))
                return step

            acc = make_step(True)(kb, (jnp.zeros((tk, LANES), F32),) * 4)
            acc = lax.fori_loop(kb + 1, nq, make_step(False), acc)
            dk_ref[rows(kb), :LANES] = acc[0].astype(BF16)
            dk_ref[rows(kb), LANES:] = acc[1].astype(BF16)
            dv_ref[rows(kb), :] = jnp.where(lane < B_VDIM, acc[2], acc[3]).astype(BF16)
            return carry

        lax.fori_loop(0, nq, kv_step, 0)

        def dq_step(qb, carry):
            c, sp, sm = c_ref[rows(qb), :], sp_ref[rows(qb), :], sm_ref[rows(qb), :]
            for hh in range(2):
                dq_ref[rows(qb), hh * LANES:(hh + 1) * LANES] = _rope_t(dqt_scr[hh, :, rows(qb)].T, c, sp, sm).astype(BF16)
            return carry

        lax.fori_loop(0, nq, dq_step, 0)

    pair_full = pl.BlockSpec((None, t, 2 * LANES), lambda b, j: (b, 0, j))
    one_full = pl.BlockSpec((None, t, LANES), lambda b, j: (b, 0, j))
    row_full = pl.BlockSpec((None, None, 8, t), lambda b, j: (b, j, 0, 0))
    tab_full = pl.BlockSpec((None, t, LANES), lambda b, j: (b, 0, 0))
    return pl.pallas_call(
        body, name="attn_b_bwd", grid=(bsz, B_HEADS // 2),
        in_specs=[pair_full, pair_full, one_full, one_full, one_full, row_full, tab_full, tab_full, tab_full],
        out_specs=[pair_full, pair_full, one_full],
        out_shape=[jax.ShapeDtypeStruct((bsz, t, B_KPAD), BF16), jax.ShapeDtypeStruct((bsz, t, B_KPAD), BF16),
                   jax.ShapeDtypeStruct((bsz, t, B_HEADS * B_VDIM), BF16)],
        scratch_shapes=[pltpu.VMEM((2, LANES, t), F32), pltpu.VMEM((8, t), F32)],
        compiler_params=_cparams(("parallel", "parallel")))(q, kpad, v, do, o, lse2, *tabs)


def _attn_b_dq(q, kpad, v, do, o, lse2, tabs):
    bsz, t, _ = q.shape
    tq = tk = B_TQ
    nq = t // tq

    def body(q_ref, k_ref, v_ref, do_ref, o_ref, lse_ref, c_ref, sp_ref, sm_ref, dq_ref, dsum_ref):
        qblk = pl.program_id(2)
        lane = lax.broadcasted_iota(jnp.int32, (tq, LANES), 1)
        do_b = do_ref[...]
        prod = do_b.astype(F32) * o_ref[...]
        hi = prod.astype(BF16)
        lo = (prod - hi.astype(F32)).astype(BF16)
        sel_lane = lax.broadcasted_iota(jnp.int32, (8, LANES), 1)
        sel_row = lax.broadcasted_iota(jnp.int32, (8, LANES), 0)
        sel = jnp.where((sel_lane < B_VDIM) == (sel_row == 0), 1.0, 0.0)
        sel = jnp.where(sel_row < 2, sel, 0.0).astype(BF16)
        dsum = _dot_nt(sel, hi) + _dot_nt(sel, lo)
        dsum_ref[...] = dsum
        lse = lse_ref[...]
        qs = [q_ref[:, hh * LANES:(hh + 1) * LANES] for hh in range(2)]
        zero = jnp.zeros_like(do_b)
        dos = [jnp.where(lane < B_VDIM, do_b, zero), jnp.where(lane < B_VDIM, zero, do_b)]

        def kblock(kb, hh):
            start = pl.multiple_of(kb * tk, tk)
            return k_ref[pl.ds(start, tk), hh * LANES:(hh + 1) * LANES]

        def dots(kb):
            start = pl.multiple_of(kb * tk, tk)
            vb = v_ref[pl.ds(start, tk), :]
            return ([_dot_nt(kblock(kb, hh), qs[hh]) for hh in range(2)], [_dot_nt(vb, dos[hh]) for hh in range(2)])

        def ds_of(ss, dps, kb, masked):
            out = []
            for hh in range(2):
                s = ss[hh] * B_C2
                if masked:
                    s = jnp.where(_key_le_query(kb, qblk, tk, tq), s, NEG)
                p = jnp.exp2(s - lse[hh:hh + 1])
                out.append((p * (dps[hh] - dsum[hh:hh + 1]) * B_SCALE).astype(BF16))
            return out

        def accum(acc, kb, dss):
            return [acc[hh] + _dot_tn(kblock(kb, hh), dss[hh]) for hh in range(2)]

        def step(kb, carry):
            dss, acc = carry
            ss, dps = dots(kb)
            acc = accum(acc, jnp.maximum(kb - 1, 0), dss)
            return (ds_of(ss, dps, kb, False), acc)

        init = ([jnp.zeros((tk, tq), BF16)] * 2, [jnp.zeros((LANES, tq), F32)] * 2)
        dss, acc = lax.fori_loop(0, qblk, step, init)
        ss, dps = dots(qblk)
        acc = accum(acc, jnp.maximum(qblk - 1, 0), dss)
        acc = accum(acc, qblk, ds_of(ss, dps, qblk, True))
        c, sp, sm = c_ref[...], sp_ref[...], sm_ref[...]
        for hh in range(2):
            dq_ref[:, hh * LANES:(hh + 1) * LANES] = _rope_t(acc[hh].T, c, sp, sm).astype(BF16)

    pair_tile, pair_full, one_tile, one_full, tab_tile = _b_specs(t, tq, tk)
    row_tile = _b_row_tile(tq)
    return pl.pallas_call(
        body, name="attn_b_dq", grid=(bsz, B_HEADS // 2, nq),
        in_specs=[pair_tile, pair_full, one_full, one_tile, one_tile, row_tile, tab_tile, tab_tile, tab_tile],
        out_specs=[pair_tile, row_tile],
        out_shape=[jax.ShapeDtypeStruct((bsz, t, B_KPAD), BF16),
                   jax.ShapeDtypeStruct((bsz, B_HEADS // 2, 8, t), F32)],
        compiler_params=_cparams(("parallel", "parallel", "arbitrary")))(q, kpad, v, do, o, lse2, *tabs)


def _attn_b_dkv(q, kpad, v, do, lse2, dsum):
    bsz, t, _ = q.shape
    tq = tk = B_TQ
    nq = t // tq

    def body(q_ref, k_ref, v_ref, do_ref, lse_ref, dsum_ref, dk_ref, dv_ref):
        kblk = pl.program_id(2)
        lane = lax.broadcasted_iota(jnp.int32, (tk, LANES), 1)
        ks = [k_ref[:, hh * LANES:(hh + 1) * LANES] for hh in range(2)]
        vb = v_ref[...]
        zero = jnp.zeros_like(vb)
        vs = [jnp.where(lane < B_VDIM, vb, zero), jnp.where(lane < B_VDIM, zero, vb)]

        def make_step(masked):
            def step(qb, carry):
                start = pl.multiple_of(qb * tq, tq)
                qs = [q_ref[pl.ds(start, tq), hh * LANES:(hh + 1) * LANES] for hh in range(2)]
                do_b = do_ref[pl.ds(start, tq), :]
                ss = [_dot_nt(ks[hh], qs[hh]) for hh in range(2)]
                dps = [_dot_nt(vs[hh], do_b) for hh in range(2)]
                pbs, dss = [], []
                for hh in range(2):
                    s = ss[hh] * B_C2
                    if masked:
                        s = jnp.where(_key_le_query(kblk, qb, tk, tq), s, NEG)
                    p = jnp.exp2(s - lse_ref[hh:hh + 1, pl.ds(start, tq)])
                    dss.append((p * (dps[hh] - dsum_ref[hh:hh + 1, pl.ds(start, tq)]) * B_SCALE).astype(BF16))
                    pbs.append(p.astype(BF16))
                return (carry[0] + _dot(dss[0], qs[0]), carry[1] + _dot(dss[1], qs[1]),
                        carry[2] + _dot(pbs[0], do_b), carry[3] + _dot(pbs[1], do_b))
            return step

        init = (jnp.zeros((tk, LANES), F32),) * 4
        carry = make_step(True)(kblk, init)
        carry = lax.fori_loop(kblk + 1, nq, make_step(False), carry)
        dk_ref[:, :LANES] = carry[0].astype(BF16)
        dk_ref[:, LANES:] = carry[1].astype(BF16)
        dv_ref[...] = jnp.where(lane < B_VDIM, carry[2], carry[3]).astype(BF16)

    pair_tile, pair_full, one_tile, one_full, _ = _b_specs(t, tq, tk)
    row_full = pl.BlockSpec((None, None, 8, t), lambda b, j, i: (b, j, 0, 0))
    return pl.pallas_call(
        body, name="attn_b_dkv", grid=(bsz, B_HEADS // 2, nq),
        in_specs=[pair_full, pair_tile, one_tile, one_full, row_full, row_full],
        out_specs=[pair_tile, one_tile],
        out_shape=[jax.ShapeDtypeStruct((bsz, t, B_KPAD), BF16),
                   jax.ShapeDtypeStruct((bsz, t, B_HEADS * B_VDIM), BF16)],
        compiler_params=_cparams(("parallel", "parallel", "arbitrary")))(q, kpad, v, do, lse2, dsum)


ANY = pl.BlockSpec(memory_space=pl.ANY)


def _all_gather(shard, name):
    def body(x_ref, out_ref, send_sems, recv_sems, local_sem):
        x, y, c = lax.axis_index("x"), lax.axis_index("y"), lax.axis_index("c")
        me, sibling = (x, y, c), (x, y, 1 - c)
        chips = [(1 - x, y), (x, 1 - y), (1 - x, 1 - y)]

        def rows(px, py, pc):
            return out_ref.at[4 * px + 2 * py + pc]

        def copy(k, block, to, src=None):
            return pltpu.make_async_remote_copy(
                src_ref=rows(*block) if src is None else src, dst_ref=rows(*block),
                send_sem=send_sems.at[k], recv_sem=recv_sems.at[k], device_id=to, device_id_type=MESH)

        mine = pltpu.make_async_copy(x_ref, rows(*me), local_sem)
        mine.start()
        first = [copy(0, me, sibling, src=x_ref)]
        first += [copy(1 + j, me, (*chip, c), src=x_ref) for j, chip in enumerate(chips)]
        for cp in first:
            cp.start()
        passed = [copy(4 + j, (*chip, c), sibling) for j, chip in enumerate(chips)]
        for j, chip in enumerate(chips):
            copy(1 + j, (*chip, c), me).wait_recv()
            passed[j].start()
        copy(0, sibling, me).wait_recv()
        for j, chip in enumerate(chips):
            copy(4 + j, (*chip, 1 - c), me).wait_recv()
        for cp in first + passed:
            cp.wait_send()
        mine.wait()

    return pl.pallas_call(
        body, name=name, in_specs=[ANY], out_specs=ANY,
        out_shape=jax.ShapeDtypeStruct((N_DEV,) + shard.shape, shard.dtype),
        scratch_shapes=[pltpu.SemaphoreType.DMA((7,)), pltpu.SemaphoreType.DMA((7,)), pltpu.SemaphoreType.DMA])(shard)


N_CHIPS = 4


def _exchange_sibling(blocks, name):
    def body(g_ref, got_ref, send_sems, recv_sems):
        x, y, c = lax.axis_index("x"), lax.axis_index("y"), lax.axis_index("c")
        sends = [pltpu.make_async_remote_copy(
            src_ref=g_ref.at[2 * q + (1 - c)], dst_ref=got_ref.at[q], send_sem=send_sems.at[q],
            recv_sem=recv_sems.at[q], device_id=(x, y, 1 - c), device_id_type=MESH) for q in range(N_CHIPS)]
        for cp in sends:
            cp.start()
        for cp in sends:
            cp.wait_recv()
        for cp in sends:
            cp.wait_send()

    return pl.pallas_call(
        body, name=name, in_specs=[ANY], out_specs=ANY,
        out_shape=jax.ShapeDtypeStruct((N_CHIPS,) + blocks.shape[1:], blocks.dtype),
        scratch_shapes=[pltpu.SemaphoreType.DMA((N_CHIPS,)), pltpu.SemaphoreType.DMA((N_CHIPS,))])(blocks)


def _exchange_chips(parts, name):
    def body(p_ref, out_ref, send_sems, recv_sems, local_sem):
        x, y, c = lax.axis_index("x"), lax.axis_index("y"), lax.axis_index("c")
        me = 2 * x + y

        def peer(k):
            return (1 - x if k & 2 else x, 1 - y if k & 1 else y)

        def copy(k):
            px, py = peer(k)
            return pltpu.make_async_remote_copy(
                src_ref=p_ref.at[2 * px + py], dst_ref=out_ref.at[me], send_sem=send_sems.at[k - 1],
                recv_sem=recv_sems.at[k - 1], device_id=(px, py, c), device_id_type=MESH)

        def arrival(k):
            px, py = peer(k)
            slot = out_ref.at[2 * px + py]
            return pltpu.make_async_remote_copy(
                src_ref=slot, dst_ref=slot, send_sem=send_sems.at[k - 1], recv_sem=recv_sems.at[k - 1],
                device_id=(px, py, c), device_id_type=MESH)

        mine = pltpu.make_async_copy(p_ref.at[me], out_ref.at[me], local_sem)
        mine.start()
        sends = [copy(k) for k in range(1, N_CHIPS)]
        for cp in sends:
            cp.start()
        for k in range(1, N_CHIPS):
            arrival(k).wait_recv()
        for cp in sends:
            cp.wait_send()
        mine.wait()

    return pl.pallas_call(
        body, name=name, in_specs=[ANY], out_specs=ANY,
        out_shape=jax.ShapeDtypeStruct(parts.shape, parts.dtype),
        scratch_shapes=[pltpu.SemaphoreType.DMA((N_CHIPS - 1,)), pltpu.SemaphoreType.DMA((N_CHIPS - 1,)),
                        pltpu.SemaphoreType.DMA])(parts)


def _add_pairs(blocks, got, core, *, tr, name):
    q, r, c = got.shape

    def body(core_ref, a_ref, b_ref, o_ref):
        o_ref[...] = (a_ref[...].astype(F32) + b_ref[...].astype(F32)).astype(o_ref.dtype)

    spec = pl.BlockSpec((q, tr, c), lambda i, core_ref: (0, i, 0))
    mine = pl.BlockSpec((q, None, tr, c), lambda i, core_ref: (0, core_ref[0], i, 0))
    return pl.pallas_call(
        body, name=name,
        grid_spec=pltpu.PrefetchScalarGridSpec(num_scalar_prefetch=1, grid=(r // tr,), in_specs=[mine, spec],
                                               out_specs=spec),
        out_shape=jax.ShapeDtypeStruct(got.shape, BF16), compiler_params=_cparams(("parallel",)))(
            core, blocks.reshape(q, 2, r, c), got)


def _sum_slots(slots, *, tr, name):
    n_slots, r, c = slots.shape

    def body(s_ref, o_ref):
        acc = s_ref[0].astype(F32)
        for s in range(1, n_slots):
            acc = acc + s_ref[s].astype(F32)
        o_ref[...] = acc

    return pl.pallas_call(
        body, name=name, grid=(r // tr,),
        in_specs=[pl.BlockSpec((n_slots, tr, c), lambda i: (0, i, 0))],
        out_specs=pl.BlockSpec((tr, c), lambda i: (i, 0)),
        out_shape=jax.ShapeDtypeStruct((r, c), F32),
        compiler_params=_cparams(("parallel",)))(slots)


def _adamw(w, g, m, v, name):
    shape = w.shape
    cols = shape[-1]
    args = [a.reshape(-1, cols) for a in (w, g, m, v)]
    rows = args[0].shape[0]
    tm = 256 if rows % 256 == 0 else rows

    def fn(w_t, g_t, m_t, v_t):
        m_n = ADAM_B1 * m_t + (1.0 - ADAM_B1) * g_t
        v_n = ADAM_B2 * v_t + (1.0 - ADAM_B2) * (g_t * g_t)
        m_hat = m_n / (1.0 - ADAM_B1 ** ADAM_STEP)
        v_hat = v_n / (1.0 - ADAM_B2 ** ADAM_STEP)
        delta = -ADAM_LR * (m_hat / (jnp.sqrt(v_hat) + ADAM_EPS) + ADAM_WD * w_t)
        return delta, m_n, v_n

    delta, m_n, v_n = _rowwise(fn, args, [], [(cols, F32)] * 3, tm=tm, name=name)
    return delta.reshape(shape), m_n.reshape(shape), v_n.reshape(shape)


def _local_step(x, positions, target, w):
    bsz, t, _ = x.shape
    n = bsz * t
    tm = 256
    mm = 512
    x2 = x.reshape(n, D_MODEL)
    tgt2 = target.reshape(n, D_MODEL)
    pos = positions.reshape(n)
    tb2 = _rope_tables(pos, B_ROPE_THETA, B_ROPE, B_NOPE)
    tabs_b = [a.reshape(bsz, t, LANES) for a in tb2]

    w_a_in = w["a_w_in"]
    w_qkv, w_z = w_a_in[:, :A_QKV], w_a_in[:, A_QKV:]
    w_a_out = w["a_w_out"]
    w_down = w["kv_w_down"]
    w_down_p = jnp.zeros((D_MODEL, 3 * LANES), BF16).at[:, :B_KV_LORA].set(w_down[:, :B_KV_LORA])
    w_down_p = w_down_p.at[:, B_KV_LORA + B_NOPE:B_KV_LORA + B_QK_DIM].set(w_down[:, B_KV_LORA:])
    wu = w["kv_w_up"].reshape(B_KV_LORA, B_HEADS, B_NOPE + B_VDIM)
    w_upk = jnp.pad(wu[:, :, :B_NOPE], ((0, 0), (0, 0), (0, LANES - B_NOPE))).reshape(B_KV_LORA, B_KPAD)
    w_upv = wu[:, :, B_NOPE:].reshape(B_KV_LORA, B_HEADS * B_VDIM)
    w_b_in = w["b_w_in"]
    w_q_p = jnp.pad(w["b_w_q_up"].reshape(B_Q_LORA, B_HEADS, B_QK_DIM),
                    ((0, 0), (0, 0), (0, LANES - B_QK_DIM))).reshape(B_Q_LORA, B_KPAD)
    w_b_out = w["b_w_out"]

    def tab_extras(tabs2, rows):
        return [(a, (rows, LANES), lambda j, i, kk: (i, 0)) for a in tabs2]

    (hn_a,) = _rowwise(lambda xt, g: (_rms(xt, g),), [x2], [w["a_pre_norm"]], [(D_MODEL, BF16)],
                       tm=tm, name="a_pre_norm")

    def rope_epilogue(acc, o_ref, rows, c_ref, sp_ref, sm_ref):
        c, sp, sm = c_ref[rows, :], sp_ref[rows, :], sm_ref[rows, :]
        for h in range(acc.shape[1] // LANES):
            hs = slice(h * LANES, (h + 1) * LANES)
            o_ref[rows, hs] = _rope(acc[:, hs], c, sp, sm).astype(BF16)

    def to_group(a, d):
        if d == 1:
            return a
        return a.reshape(bsz, t // d, d, a.shape[-1]).transpose(0, 2, 1, 3).reshape(n, a.shape[-1])

    def from_group(a, d):
        if d == 1:
            return a
        return a.reshape(bsz, d, t // d, a.shape[-1]).transpose(0, 2, 1, 3).reshape(n, a.shape[-1])

    def rows_to_cols(r, d):
        return r.reshape(bsz, d, A_HEADS, t // d).transpose(0, 3, 1, 2).reshape(n, A_HEADS)

    def cols_to_rows(cc, d):
        return cc.reshape(bsz, t // d, d, A_HEADS).transpose(0, 2, 3, 1).reshape(bsz * d, A_HEADS, t // d)

    z_a = _matmul(hn_a, w_z, out_dtype=F32, tm=mm, tn=A_WIDTH, name="a_gate")
    hn_g, tabs_g, qk_g, v_g, o_g, lse_g = [], [], [], [], [], []
    for g, d in enumerate(A_DILATIONS):
        hn_g.append(to_group(hn_a, d))
        tabs_g.append(_rope_tables(to_group(pos.reshape(n, 1), d).reshape(n), A_ROPE_THETA, A_ROT_DIM, 0))
        w_g = w_qkv[:, g * 3 * A_WIDTH:(g + 1) * 3 * A_WIDTH]
        qk = _matmul(hn_g[g], w_g[:, :2 * A_WIDTH], out_dtype=BF16, tm=mm, tn=A_WIDTH, name=f"a_qk_g{g}",
                     epilogue=rope_epilogue, epilogue_rows=EPILOGUE_ROWS, extras=tab_extras(tabs_g[g], mm))
        v = _matmul(hn_g[g], w_g[:, 2 * A_WIDTH:], out_dtype=BF16, tm=mm, tn=A_WIDTH, name=f"a_v_g{g}")
        qk_g.append(qk.reshape(bsz * d, t // d, 2 * A_WIDTH))
        v_g.append(v.reshape(bsz * d, t // d, A_WIDTH))
        o, lse = _attn_a_fwd(qk_g[g], v_g[g], f"attn_a_fwd_g{g}")
        o_g.append(from_group(o.reshape(n, A_WIDTH), d))
        lse_g.append(rows_to_cols(lse, d))

    def merge_fn(o0, o1, o2, l0, l1, l2, z):
        lmax = jnp.maximum(jnp.maximum(l0, l1), l2)
        e0, e1, e2 = jnp.exp2(l0 - lmax), jnp.exp2(l1 - lmax), jnp.exp2(l2 - lmax)
        den = e0 + e1 + e2
        w0, w1, w2 = e0 / den, e1 / den, e2 / den
        parts = []
        for h in range(A_HEADS):
            hs = slice(h * A_HEAD_DIM, (h + 1) * A_HEAD_DIM)
            parts.append(w0[:, h:h + 1] * o0[:, hs] + w1[:, h:h + 1] * o1[:, hs] + w2[:, h:h + 1] * o2[:, hs])
        o = jnp.concatenate(parts, axis=1)
        return o * _silu(z), o, lmax + jnp.log2(den)

    y_a, o_a2, lse_a = _rowwise(merge_fn, [*o_g, *lse_g, z_a], [],
                                [(A_WIDTH, BF16), (A_WIDTH, F32), (A_HEADS, F32)], tm=tm, name="a_merge_gate")
    out_a = _matmul(y_a, w_a_out, out_dtype=F32, tm=mm, tn=D_MODEL, name="a_out")

    def mid_fn(xt, out, g_post, g_kv, g_b):
        h1 = xt + _rms(out, g_post)
        return h1, _rms(h1, g_kv), _rms(h1, g_b)

    h1, hn_kv, hn_b = _rowwise(mid_fn, [x2, out_a], [w["a_post_norm"], w["kv_norm"], w["b_pre_norm"]],
                               [(D_MODEL, F32), (D_MODEL, BF16), (D_MODEL, BF16)], tm=tm, name="a_post_norm")

    ckr = _matmul(hn_kv, w_down_p, out_dtype=F32, tm=mm, tn=3 * LANES, name="kv_down")

    def latent_fn(ck, c, sp, sm, g):
        return _rms(ck[:, :B_KV_LORA], g), _rope(ck[:, B_KV_LORA:], c, sp, sm)

    c_kv, k_rope = _rowwise(latent_fn, [ckr, *tb2], [w["kv_latent_norm"]], [(B_KV_LORA, BF16), (LANES, F32)],
                            tm=tm, name="kv_latent_norm")

    def kpad_epilogue(acc, o_ref, rows, kr_ref):
        kr = kr_ref[rows, :]
        for h in range(acc.shape[1] // LANES):
            hs = slice(h * LANES, (h + 1) * LANES)
            o_ref[rows, hs] = (acc[:, hs] + kr).astype(BF16)

    kpad = _matmul(c_kv, w_upk, out_dtype=BF16, tm=mm, tn=1024, name="kv_up_k", epilogue=kpad_epilogue,
                   epilogue_rows=EPILOGUE_ROWS,
                   extras=[(k_rope, (mm, LANES), lambda j, i, kk: (i, 0))])
    v_b = _matmul(c_kv, w_upv, out_dtype=BF16, tm=mm, tn=1024, name="kv_up_v")

    proj_b = _matmul(hn_b, w_b_in, out_dtype=F32, tm=mm, tn=w_b_in.shape[1], name="b_in")
    (c_q,) = _rowwise(lambda p, g: (_rms(p[:, :B_Q_LORA], g),), [proj_b], [w["b_q_norm"]], [(B_Q_LORA, BF16)],
                      tm=tm, name="b_q_norm")

    q_b = _matmul(c_q, w_q_p, out_dtype=BF16, tm=mm, tn=1024, name="b_q_up", epilogue=rope_epilogue,
                  epilogue_rows=EPILOGUE_ROWS,
                  extras=tab_extras(tb2, mm))
    q_b3, kpad3, v_b3 = q_b.reshape(bsz, t, B_KPAD), kpad.reshape(bsz, t, B_KPAD), v_b.reshape(bsz, t, -1)
    v4 = v_b3.reshape(bsz, t, B_HEADS, B_VDIM)
    vt1 = jnp.concatenate([v4, jnp.ones((bsz, t, B_HEADS, 1), BF16),
                           jnp.zeros((bsz, t, B_HEADS, LANES - B_VDIM - 1), BF16)], axis=3)
    vt1 = vt1.transpose(0, 2, 3, 1).reshape(bsz, B_KPAD, t)
    o_b, lse_b = _attn_b_fwd(q_b3, kpad3, vt1)
    o_b2 = o_b.reshape(n, -1)
    (y_b,) = _rowwise(lambda o, p: (o * _silu(p[:, B_Q_LORA:]),), [o_b2, proj_b], [], [(D_MODEL, BF16)],
                      tm=tm, name="b_gate_mul")
    out_b = _matmul(y_b, w_b_out, out_dtype=F32, tm=mm, tn=D_MODEL, name="b_out")

    def head_fn(h1t, out, tgt, g):
        e = h1t + _rms(out, g) - tgt
        loss_row = 0.5 * jnp.mean(e * e, axis=-1, keepdims=True)
        dh2 = e * (1.0 / D_MODEL)
        d_out, dg = _rms_bwd(out, g, dh2)
        return dh2, d_out, dg, jnp.broadcast_to(loss_row * (1.0 / LANES), (loss_row.shape[0], LANES))

    dh2, d_out_b, dg_b_post, loss_acc = _rowwise(
        head_fn, [h1, out_b, tgt2], [w["b_post_norm"]], [(D_MODEL, F32), (D_MODEL, BF16)], [D_MODEL, LANES],
        tm=tm, name="loss_head")

    dy_b = _matmul(d_out_b, w_b_out, tb=True, out_dtype=F32, tm=mm, tn=D_MODEL, name="b_out_dx")
    gw_b_out = _matmul(y_b, d_out_b, ta=True, out_dtype=F32, tm=mm, tn=D_MODEL, tk=2048, name="b_out_dw")

    def gate_b_bwd(dy, o, p):
        z = p[:, B_Q_LORA:]
        return dy * _silu(z), dy * o * _silu_grad(z)

    do_b, dz_b = _rowwise(gate_b_bwd, [dy_b, o_b2, proj_b], [], [(D_MODEL, BF16), (D_MODEL, F32)],
                          tm=tm, name="b_gate_bwd")
    do_b3 = do_b.reshape(bsz, t, -1)
    dq_b, dk_b, dv_b = _attn_b_bwd(q_b3, kpad3, v_b3, do_b3, o_b, lse_b, tabs_b)
    dq_b2, dk_b2, dv_b2 = dq_b.reshape(n, B_KPAD), dk_b.reshape(n, B_KPAD), dv_b.reshape(n, -1)

    dc_kv = _matmul(dk_b2, w_upk, tb=True, out_dtype=F32, tm=mm, tn=B_KV_LORA, name="kv_up_k_dx")
    dc_kv = _matmul(dv_b2, w_upv, tb=True, out_dtype=F32, tm=mm, tn=B_KV_LORA, name="kv_up_v_dx",
                    epilogue=_add_epilogue, extras=[(dc_kv, (mm, B_KV_LORA), lambda j, i, kk: (i, j))])
    gw_upk = _matmul(c_kv, dk_b2, ta=True, out_dtype=F32, tm=B_KV_LORA, tn=1024, tk=2048, name="kv_up_k_dw")
    gw_upv = _matmul(c_kv, dv_b2, ta=True, out_dtype=F32, tm=B_KV_LORA, tn=1024, tk=2048, name="kv_up_v_dw")

    def latent_bwd(ck, dck, dk, c, sp, sm, g):
        d1, dg = _rms_bwd(ck[:, :B_KV_LORA], g, dck)
        ksum = dk[:, :LANES].astype(F32)
        for h in range(1, B_HEADS):
            ksum = ksum + dk[:, h * LANES:(h + 1) * LANES].astype(F32)
        lane = lax.broadcasted_iota(jnp.int32, ksum.shape, 1)
        ksum = jnp.where((lane >= B_NOPE) & (lane < B_QK_DIM), ksum, 0.0)
        return jnp.concatenate([d1, _rope_t(ksum, c, sp, sm)], axis=1), dg

    dckr, dg_latent = _rowwise(latent_bwd, [ckr, dc_kv, dk_b2, *tb2], [w["kv_latent_norm"]],
                               [(3 * LANES, BF16)], [B_KV_LORA], tm=tm, name="kv_latent_bwd")
    dhn_kv = _matmul(dckr, w_down_p, tb=True, out_dtype=F32, tm=mm, tn=D_MODEL, name="kv_down_dx")
    gw_down_p = _matmul(hn_kv, dckr, ta=True, out_dtype=F32, tm=mm, tn=3 * LANES, tk=2048, name="kv_down_dw")

    dc_q = _matmul(dq_b2, w_q_p, tb=True, out_dtype=F32, tm=mm, tn=B_Q_LORA, name="b_q_up_dx")
    gw_q_p = _matmul(c_q, dq_b2, ta=True, out_dtype=F32, tm=B_Q_LORA, tn=1024, tk=2048, name="b_q_up_dw")

    def q_norm_bwd(p, dcq, dz, g):
        d1, dg = _rms_bwd(p[:, :B_Q_LORA], g, dcq)
        return jnp.concatenate([d1, dz], axis=1), dg

    dproj_b, dg_q_norm = _rowwise(q_norm_bwd, [proj_b, dc_q, dz_b], [w["b_q_norm"]],
                                  [(w_b_in.shape[1], BF16)], [B_Q_LORA], tm=tm, name="b_q_norm_bwd")
    dhn_b = _matmul(dproj_b, w_b_in, tb=True, out_dtype=F32, tm=mm, tn=D_MODEL, name="b_in_dx")
    gw_b_in = _matmul(hn_b, dproj_b, ta=True, out_dtype=F32, tm=mm, tn=w_b_in.shape[1], tk=2048, name="b_in_dw")

    def mid_bwd(h1t, dh2t, dkv, db, g_kv, g_b, g_post, out):
        dxa, ra = _rms_bwd(h1t, g_kv, dkv)
        dxb, rb = _rms_bwd(h1t, g_b, db)
        dh1 = dh2t + dxa + dxb
        d_out, rp = _rms_bwd(out, g_post, dh1)
        return dh1, d_out, ra, rb, rp

    def mid_bwd_fn(h1t, dh2t, dkv, db, out, g_kv, g_b, g_post):
        return mid_bwd(h1t, dh2t, dkv, db, g_kv, g_b, g_post, out)

    dh1, d_out_a, dg_kv, dg_b_pre, dg_a_post = _rowwise(
        mid_bwd_fn, [h1, dh2, dhn_kv, dhn_b, out_a], [w["kv_norm"], w["b_pre_norm"], w["a_post_norm"]],
        [(D_MODEL, F32), (D_MODEL, BF16)], [D_MODEL] * 3, tm=tm, name="mid_bwd")

    dy_a = _matmul(d_out_a, w_a_out, tb=True, out_dtype=F32, tm=mm, tn=A_WIDTH, name="a_out_dx")
    gw_a_out = _matmul(y_a, d_out_a, ta=True, out_dtype=F32, tm=mm, tn=D_MODEL, tk=2048, name="a_out_dw")

    def gate_a_bwd(dy, o, z):
        do = dy * _silu(z)
        prod = do * o
        lane = lax.broadcasted_iota(jnp.int32, (prod.shape[0], A_HEADS), 1)
        dsum = jnp.zeros((prod.shape[0], A_HEADS), F32)
        for h in range(A_HEADS):
            col = jnp.sum(prod[:, h * A_HEAD_DIM:(h + 1) * A_HEAD_DIM], axis=1, keepdims=True)
            dsum = jnp.where(lane == h, col, dsum)
        return do, dy * o * _silu_grad(z), dsum

    do_a, dz_a, dsum_a = _rowwise(gate_a_bwd, [dy_a, o_a2, z_a], [],
                                  [(A_WIDTH, BF16), (A_WIDTH, BF16), (A_HEADS, F32)], tm=tm, name="a_gate_bwd")
    dhn_a = _matmul(dz_a, w_z, tb=True, out_dtype=F32, tm=mm, tn=D_MODEL, name="a_gate_dx")
    gw_parts = []
    for g, d in enumerate(A_DILATIONS):
        s_n, ln = bsz * d, t // d
        dqkv = _attn_a_bwd(qk_g[g], v_g[g], to_group(do_a, d).reshape(s_n, ln, A_WIDTH), cols_to_rows(lse_a, d),
                           cols_to_rows(dsum_a, d), [a.reshape(s_n, ln, LANES) for a in tabs_g[g]],
                           f"attn_a_bwd_g{g}").reshape(n, 3 * A_WIDTH)
        w_g = w_qkv[:, g * 3 * A_WIDTH:(g + 1) * 3 * A_WIDTH]
        if d == 1:
            dhn_a = _matmul(dqkv, w_g, tb=True, out_dtype=F32, tm=mm, tn=D_MODEL, name=f"a_qkv_dx_g{g}",
                            epilogue=_add_epilogue, extras=[(dhn_a, (mm, D_MODEL), lambda j, i, kk: (i, j))])
        else:
            dhn_a = dhn_a + from_group(_matmul(dqkv, w_g, tb=True, out_dtype=F32, tm=mm, tn=D_MODEL,
                                               name=f"a_qkv_dx_g{g}"), d)
        gw_parts.append(_matmul(hn_g[g], dqkv, ta=True, out_dtype=F32, tm=mm, tn=1024, tk=2048,
                                name=f"a_qkv_dw_g{g}"))
    gw_parts.append(_matmul(hn_a, dz_a, ta=True, out_dtype=F32, tm=mm, tn=1024, tk=2048, name="a_gate_dw"))
    gw_a_in = jnp.concatenate(gw_parts, axis=1)

    def first_bwd(xt, dhn, dh1t, g):
        dx, dg = _rms_bwd(xt, g, dhn)
        return dh1t + dx, dg

    grad_x, dg_a_pre = _rowwise(first_bwd, [x2, dhn_a, dh1], [w["a_pre_norm"]], [(D_MODEL, F32)], [D_MODEL],
                                tm=tm, name="a_pre_norm_bwd")

    gw_down = jnp.concatenate([gw_down_p[:, :B_KV_LORA], gw_down_p[:, B_KV_LORA + B_NOPE:B_KV_LORA + B_QK_DIM]], axis=1)
    gw_up = jnp.concatenate([gw_upk.reshape(B_KV_LORA, B_HEADS, LANES)[:, :, :B_NOPE],
                             gw_upv.reshape(B_KV_LORA, B_HEADS, B_VDIM)], axis=2).reshape(B_KV_LORA, -1)
    gw_q_up = gw_q_p.reshape(B_Q_LORA, B_HEADS, LANES)[:, :, :B_QK_DIM].reshape(B_Q_LORA, -1)
    grads = {"a_w_in": gw_a_in, "a_w_out": gw_a_out, "kv_w_down": gw_down, "kv_w_up": gw_up,
             "b_w_in": gw_b_in, "b_w_q_up": gw_q_up, "b_w_out": gw_b_out}
    gains = {"a_pre_norm": dg_a_pre, "a_post_norm": dg_a_post, "kv_norm": dg_kv, "kv_latent_norm": dg_latent,
             "b_pre_norm": dg_b_pre, "b_q_norm": dg_q_norm, "b_post_norm": dg_b_post}
    gains = {k: jnp.sum(a, axis=0) for k, a in gains.items()}
    return jnp.sum(loss_acc), grad_x.reshape(bsz, t, D_MODEL), grads, gains


WEIGHT_ORDER = ("a_pre_norm", "a_w_in", "a_w_out", "a_post_norm", "kv_norm", "kv_w_down", "kv_latent_norm",
                "kv_w_up", "b_pre_norm", "b_w_in", "b_q_norm", "b_w_q_up", "b_w_out", "b_post_norm")
MATRICES = (("a_w_in", 1024, 10240, 1), ("a_w_out", 1024, 1024, 0), ("kv_w_down", 1024, 288, 0),
            ("kv_w_up", 256, 2048, 1), ("b_w_in", 1024, 1408, 1), ("b_w_q_up", 384, 1536, 1),
            ("b_w_out", 1024, 1024, 0))
SHARDED_GAINS = ("a_pre_norm", "a_post_norm")
GAIN_WIDTHS = (("a_pre_norm", 1024), ("a_post_norm", 1024), ("kv_norm", 1024), ("kv_latent_norm", 256),
               ("b_pre_norm", 1024), ("b_q_norm", 384), ("b_post_norm", 1024))
GAIN_ROWS = 48


def _shard_rows(rows, cols):
    return rows * cols // (N_DEV * LANES)


def _whole_from_blocks(blocks, rows, cols, axis):
    if axis == 1:
        return blocks.reshape(N_DEV, rows, cols // N_DEV).transpose(1, 0, 2).reshape(rows, cols)
    return blocks.reshape(rows, cols)


def _blocks_from_whole(whole, rows, cols, axis):
    if axis == 1:
        whole = whole.reshape(rows, N_DEV, cols // N_DEV).transpose(1, 0, 2)
    return whole.reshape(N_DEV, -1, LANES)


def kernel(x, positions, a_pre_norm, a_w_in, a_w_out, a_post_norm, kv_norm, kv_w_down, kv_latent_norm, kv_w_up, b_pre_norm, b_w_in, b_q_norm, b_w_q_up, b_w_out, b_post_norm, loss_target, m_a_pre_norm, m_a_w_in, m_a_w_out, m_a_post_norm, m_kv_norm, m_kv_w_down, m_kv_latent_norm, m_kv_w_up, m_b_pre_norm, m_b_w_in, m_b_q_norm, m_b_w_q_up, m_b_w_out, m_b_post_norm, v_a_pre_norm, v_a_w_in, v_a_w_out, v_a_post_norm, v_kv_norm, v_kv_w_down, v_kv_latent_norm, v_kv_w_up, v_b_pre_norm, v_b_w_in, v_b_q_norm, v_b_w_q_up, v_b_w_out, v_b_post_norm):
    weights = dict(a_pre_norm=a_pre_norm, a_w_in=a_w_in, a_w_out=a_w_out, a_post_norm=a_post_norm, kv_norm=kv_norm,
                   kv_w_down=kv_w_down, kv_latent_norm=kv_latent_norm, kv_w_up=kv_w_up, b_pre_norm=b_pre_norm,
                   b_w_in=b_w_in, b_q_norm=b_q_norm, b_w_q_up=b_w_q_up, b_w_out=b_w_out, b_post_norm=b_post_norm)
    m_in = dict(a_pre_norm=m_a_pre_norm, a_w_in=m_a_w_in, a_w_out=m_a_w_out, a_post_norm=m_a_post_norm,
                kv_norm=m_kv_norm, kv_w_down=m_kv_w_down, kv_latent_norm=m_kv_latent_norm, kv_w_up=m_kv_w_up,
                b_pre_norm=m_b_pre_norm, b_w_in=m_b_w_in, b_q_norm=m_b_q_norm, b_w_q_up=m_b_w_q_up,
                b_w_out=m_b_w_out, b_post_norm=m_b_post_norm)
    v_in = dict(a_pre_norm=v_a_pre_norm, a_w_in=v_a_w_in, a_w_out=v_a_w_out, a_post_norm=v_a_post_norm,
                kv_norm=v_kv_norm, kv_w_down=v_kv_w_down, kv_latent_norm=v_kv_latent_norm, kv_w_up=v_kv_w_up,
                b_pre_norm=v_b_pre_norm, b_w_in=v_b_w_in, b_q_norm=v_b_q_norm, b_w_q_up=v_b_w_q_up,
                b_w_out=v_b_w_out, b_post_norm=v_b_post_norm)
    me = 4 * lax.axis_index("x") + 2 * lax.axis_index("y") + lax.axis_index("c")

    flat = jnp.concatenate([weights[name].astype(BF16).reshape(-1, LANES) for name, _, _, _ in MATRICES], axis=0)
    gathered = _all_gather(flat, "gather_weights")
    whole = {}
    off = 0
    for name, rows, cols, axis in MATRICES:
        nr = _shard_rows(rows, cols)
        whole[name] = _whole_from_blocks(gathered[:, off:off + nr], rows, cols, axis)
        off += nr
    gain_shard = jnp.concatenate([weights[name].reshape(1, LANES) for name in SHARDED_GAINS]
                                 + [jnp.zeros((8 - len(SHARDED_GAINS), LANES), F32)], axis=0)
    gain_blocks = _all_gather(gain_shard, "gather_gains")
    for i, name in enumerate(SHARDED_GAINS):
        whole[name] = gain_blocks[:, i, :].reshape(1, D_MODEL)
    for name in ("kv_norm", "kv_latent_norm", "b_pre_norm", "b_q_norm", "b_post_norm"):
        whole[name] = weights[name].reshape(1, -1)

    loss_part, grad_x, grads, gains = _local_step(x, positions, loss_target, whole)

    blocks = jnp.concatenate([_blocks_from_whole(grads[name], rows, cols, axis).astype(BF16)
                              for name, rows, cols, axis in MATRICES], axis=1)
    got = _exchange_sibling(blocks, "scatter_grads_core")
    core = lax.axis_index("c").astype(jnp.int32).reshape(1)
    landed = _exchange_chips(_add_pairs(blocks, got, core, tr=2512, name="add_core_grads"), "scatter_grads_chip")
    summed = _sum_slots(landed, tr=2512, name="sum_grads")
    grad_out = {}
    off = 0
    for name, rows, cols, axis in MATRICES:
        nr = _shard_rows(rows, cols)
        grad_out[name] = summed[off:off + nr].reshape(weights[name].shape)
        off += nr

    vec = jnp.concatenate([gains[name] for name, _ in GAIN_WIDTHS] + [jnp.full((LANES,), loss_part, F32)])
    vec = jnp.pad(vec, (0, GAIN_ROWS * LANES - vec.shape[0])).reshape(GAIN_ROWS, LANES)
    total = _sum_slots(_all_gather(vec, "gather_gain_grads"), tr=GAIN_ROWS, name="sum_gain_grads").reshape(-1)
    off = 0
    for name, width in GAIN_WIDTHS:
        g = total[off:off + width]
        if name in SHARDED_GAINS:
            g = lax.dynamic_slice(g, (me * LANES,), (LANES,))
        grad_out[name] = g.reshape(weights[name].shape)
        off += width
    loss = total[off]

    deltas, new_m, new_v = {}, {}, {}
    for name in WEIGHT_ORDER:
        deltas[name], new_m[name], new_v[name] = _adamw(weights[name], grad_out[name], m_in[name], v_in[name],
                                                        "adamw_" + name)
    return (loss, grad_x, *[grad_out[k] for k in WEIGHT_ORDER], *[deltas[k] for k in WEIGHT_ORDER],
            *[new_m[k] for k in WEIGHT_ORDER], *[new_v[k] for k in WEIGHT_ORDER])
```

```python
import jax
import jax.numpy as jnp
from jax import lax
from jax.experimental import pallas as pl
from jax.experimental.pallas import tpu as pltpu

F32 = jnp.float32
BF16 = jnp.bfloat16

N_DEV = 8
D_MODEL = 1024
NORM_EPS = 1e-6
A_GROUPS = 3
A_DILATIONS = (1, 4, 16)
A_HEADS = 8
A_HEAD_DIM = 128
A_WIDTH = 1024
A_ROT_DIM = 32
A_ROPE_THETA = 500000.0
A_QKV = A_GROUPS * 3 * A_WIDTH
B_HEADS = 16
B_NOPE = 64
B_ROPE = 32
B_QK_DIM = 96
B_VDIM = 64
B_Q_LORA = 384
B_KV_LORA = 256
B_ROPE_THETA = 10000.0
B_KPAD = B_HEADS * 128
ADAM_LR = 0.001
ADAM_B1 = 0.9
ADAM_B2 = 0.999
ADAM_EPS = 1e-08
ADAM_WD = 0.01
ADAM_STEP = 10

LANES = 128
BAND = 128
EPILOGUE_ROWS = 128
NEG = -1e30
VMEM_LIMIT = 56 * 1024 * 1024
MESH = pl.DeviceIdType.MESH


def _cparams(sem):
    return pltpu.CompilerParams(dimension_semantics=sem, vmem_limit_bytes=VMEM_LIMIT)


def _rowwise(fn, rows, bcast, outs, accs=(), *, tm, name):
    n = rows[0].shape[0]
    nr, nb, no = len(rows), len(bcast), len(outs)

    def body(*refs):
        res = fn(*[r[...] for r in refs[:nr + nb]])
        out_refs = refs[nr + nb:nr + nb + no]
        acc_refs = refs[nr + nb + no:]
        for r, v in zip(out_refs, res[:no]):
            r[...] = v.astype(r.dtype)
        if acc_refs:
            @pl.when(pl.program_id(0) == 0)
            def _():
                for r in acc_refs:
                    r[...] = jnp.zeros_like(r)
            for r, v in zip(acc_refs, res[no:]):
                r[...] += v.reshape(tm // 8, 8, v.shape[-1]).sum(axis=0)

    in_specs = [pl.BlockSpec((tm, a.shape[1]), lambda i: (i, 0)) for a in rows]
    in_specs += [pl.BlockSpec(a.shape, lambda i: (0, 0)) for a in bcast]
    out_specs = [pl.BlockSpec((tm, c), lambda i: (i, 0)) for c, _ in outs]
    out_specs += [pl.BlockSpec((8, c), lambda i: (0, 0)) for c in accs]
    out_shape = [jax.ShapeDtypeStruct((n, c), dt) for c, dt in outs]
    out_shape += [jax.ShapeDtypeStruct((8, c), F32) for c in accs]
    return pl.pallas_call(
        body, name=name, grid=(n // tm,), in_specs=in_specs, out_specs=out_specs, out_shape=out_shape,
        compiler_params=_cparams(("arbitrary",)))(*rows, *bcast)


def _matmul(a, b, *, out_dtype, tm, tn, tk=None, name, epilogue=None, extras=(), ta=False, tb=False,
            epilogue_rows=None):
    epilogue_rows = epilogue_rows or tm
    k, m = a.shape[::-1] if not ta else a.shape
    n = b.shape[0] if tb else b.shape[1]
    tk = tk or k
    nk = k // tk
    ne = len(extras)
    dot = _dot_tn if ta else (_dot_nt if tb else _dot)
    assert epilogue is None or (nk == 1 and not ta)

    def body(*refs):
        a_ref, b_ref = refs[:2]
        ex = refs[2:2 + ne]
        o_ref = refs[2 + ne]
        if epilogue is not None:
            b_tile = b_ref[...].astype(BF16)
            for r0 in range(0, tm, epilogue_rows):
                rows = slice(r0, r0 + epilogue_rows)
                epilogue(dot(a_ref[rows, :].astype(BF16), b_tile), o_ref, rows, *ex)
            return
        part = dot(a_ref[...].astype(BF16), b_ref[...].astype(BF16))
        if nk == 1:
            o_ref[...] = part.astype(o_ref.dtype)
        else:
            acc_ref = refs[-1]
            kk = pl.program_id(2)

            @pl.when(kk == 0)
            def _():
                acc_ref[...] = part

            @pl.when(kk > 0)
            def _():
                acc_ref[...] += part

            @pl.when(kk == nk - 1)
            def _():
                o_ref[...] = acc_ref[...].astype(o_ref.dtype)

    a_spec = pl.BlockSpec((tk, tm), lambda j, i, kk: (kk, i)) if ta else pl.BlockSpec((tm, tk), lambda j, i, kk: (i, kk))
    b_spec = pl.BlockSpec((tn, tk), lambda j, i, kk: (j, kk)) if tb else pl.BlockSpec((tk, tn), lambda j, i, kk: (kk, j))
    in_specs = [a_spec, b_spec] + [pl.BlockSpec(bs, im) for _, bs, im in extras]
    return pl.pallas_call(
        body, name=name, grid=(n // tn, m // tm, nk), in_specs=in_specs,
        out_specs=pl.BlockSpec((tm, tn), lambda j, i, kk: (i, j)),
        out_shape=jax.ShapeDtypeStruct((m, n), out_dtype),
        scratch_shapes=[pltpu.VMEM((tm, tn), F32)] if nk > 1 else [],
        compiler_params=_cparams(("parallel", "parallel", "arbitrary")))(a, b, *[e[0] for e in extras])


def _add_epilogue(acc, o_ref, rows, prev_ref):
    o_ref[rows, :] = (acc + prev_ref[rows, :]).astype(o_ref.dtype)


def _rope(x, c, sp, sm):
    return x * c + pltpu.roll(x, 16, 1) * sp + pltpu.roll(x, LANES - 16, 1) * sm


def _rope_t(dy, c, sp, sm):
    return dy * c + pltpu.roll(dy * sp, LANES - 16, 1) + pltpu.roll(dy * sm, 16, 1)


def _rope_tables(positions, theta, rot_dim, lane0, name):
    n = positions.shape[0]
    half = rot_dim // 2
    inv_freq = 1.0 / (theta ** (jnp.arange(half, dtype=F32) * (2.0 / rot_dim)))
    freq = jnp.concatenate([jnp.zeros((lane0,), F32), inv_freq, inv_freq,
                            jnp.zeros((LANES - lane0 - rot_dim,), F32)]).reshape(1, LANES)

    def fn(p, f):
        ang = p * f
        cos, sin = jnp.cos(ang), jnp.sin(ang)
        lane = lax.broadcasted_iota(jnp.int32, ang.shape, 1) - lane0
        first, second = (lane >= 0) & (lane < half), (lane >= half) & (lane < rot_dim)
        return jnp.where(first | second, cos, 1.0), jnp.where(second, sin, 0.0), jnp.where(first, -sin, 0.0)

    return _rowwise(fn, [positions.astype(F32).reshape(n, 1)], [freq], [(LANES, F32)] * 3, tm=512, name=name)


def _value_heads_t(c_kv, w_vt, bsz, t):
    n, k = c_kv.shape
    tm, tn = 512, 1024

    def body(a_ref, b_ref, o_ref):
        acc = _dot(a_ref[...], b_ref[...])
        lane = lax.broadcasted_iota(jnp.int32, (1, tn), 1)
        acc = acc + jnp.where(lax.rem(lane, LANES) == B_VDIM, 1.0, 0.0)
        o_ref[...] = acc.T.astype(BF16)

    per_seq = t // tm
    return pl.pallas_call(
        body, name="kv_up_v_t", grid=(w_vt.shape[1] // tn, n // tm),
        in_specs=[pl.BlockSpec((tm, k), lambda j, i: (i, 0)), pl.BlockSpec((k, tn), lambda j, i: (0, j))],
        out_specs=pl.BlockSpec((None, tn, tm), lambda j, i: (i // per_seq, j, lax.rem(i, per_seq))),
        out_shape=jax.ShapeDtypeStruct((bsz, w_vt.shape[1], t), BF16),
        compiler_params=_cparams(("parallel", "parallel")))(c_kv, w_vt)


def _rms(x, g):
    xf = x.astype(F32)
    return xf * lax.rsqrt(jnp.mean(xf * xf, axis=-1, keepdims=True) + NORM_EPS) * g


def _rms_bwd(x, g, dy):
    xf = x.astype(F32)
    rstd = lax.rsqrt(jnp.mean(xf * xf, axis=-1, keepdims=True) + NORM_EPS)
    xhat = xf * rstd
    dxhat = dy * g
    dx = rstd * (dxhat - xhat * jnp.mean(dxhat * xhat, axis=-1, keepdims=True))
    return dx, dy * xhat


def _silu(z):
    return z * jax.nn.sigmoid(z)


def _silu_grad(z):
    s = jax.nn.sigmoid(z)
    return s * (1.0 + z * (1.0 - s))


def _dot_nt(a, b):
    return lax.dot_general(a, b, (((1,), (1,)), ((), ())), preferred_element_type=F32)


def _dot_tn(a, b):
    return lax.dot_general(a, b, (((0,), (0,)), ((), ())), preferred_element_type=F32)


def _dot(a, b):
    return jnp.dot(a, b, preferred_element_type=F32)


A_SCALE = A_HEAD_DIM ** -0.5
LOG2E = 1.4426950408889634
A_C2 = A_SCALE * LOG2E
A_HEAD_GROUP = 4


def _attn_a_fwd(qk, v, name):
    s_n, ln, _ = qk.shape
    nb = ln // BAND
    blk = (None, BAND, A_WIDTH)

    def body(q_ref, kc_ref, kp_ref, vc_ref, vp_ref, o_ref, lse_ref):
        kpos = lax.broadcasted_iota(jnp.int32, (2 * BAND, BAND), 0)
        qpos = lax.broadcasted_iota(jnp.int32, (2 * BAND, BAND), 1) + BAND
        first_key = jnp.where(pl.program_id(1) > 0, 0, BAND)
        mask = (kpos <= qpos) & (kpos >= qpos - BAND) & (kpos >= first_key)
        rows = []
        for h0 in range(0, A_HEADS, A_HEAD_GROUP):
            hss = [slice(h * A_HEAD_DIM, (h + 1) * A_HEAD_DIM) for h in range(h0, h0 + A_HEAD_GROUP)]
            sts = [_dot_nt(jnp.concatenate([kp_ref[:, hs], kc_ref[:, hs]], axis=0), q_ref[:, hs]) for hs in hss]
            ps, ls = [], []
            for st in sts:
                st = jnp.where(mask, st * A_C2, NEG)
                m = jnp.max(st, axis=0, keepdims=True)
                p = jnp.exp2(st - m)
                l_row = jnp.sum(p, axis=0, keepdims=True)
                ps.append(p.astype(BF16))
                ls.append(l_row)
                rows.append(m + jnp.log2(l_row))
            ots = [_dot_tn(jnp.concatenate([vp_ref[:, hs], vc_ref[:, hs]], axis=0), p) for hs, p in zip(hss, ps)]
            for hs, o_t, l_row in zip(hss, ots, ls):
                o_ref[:, hs] = (o_t / l_row).T.astype(BF16)
        lse_ref[...] = jnp.concatenate(rows, axis=0)

    def col(c, off):
        return lambda s, l: (s, jnp.maximum(l + off, 0), c)

    return pl.pallas_call(
        body, name=name, grid=(s_n, nb),
        in_specs=[pl.BlockSpec(blk, col(0, 0)), pl.BlockSpec(blk, col(1, 0)), pl.BlockSpec(blk, col(1, -1)),
                  pl.BlockSpec(blk, col(0, 0)), pl.BlockSpec(blk, col(0, -1))],
        out_specs=[pl.BlockSpec(blk, lambda s, l: (s, l, 0)),
                   pl.BlockSpec((None, A_HEADS, BAND), lambda s, l: (s, 0, l))],
        out_shape=[jax.ShapeDtypeStruct((s_n, ln, A_WIDTH), BF16), jax.ShapeDtypeStruct((s_n, A_HEADS, ln), F32)],
        compiler_params=_cparams(("parallel", "arbitrary")))(qk, qk, qk, v, v)


def _attn_a_bwd(qk, v, do, lse2, dsum, tabs, name):
    s_n, ln, _ = qk.shape
    nb = ln // BAND
    blk = (None, BAND, A_WIDTH)

    def body(q_ref, qn_ref, kc_ref, kp_ref, vc_ref, vp_ref, do_ref, don_ref, lse_ref, lsen_ref, ds_ref, dsn_ref,
             c_ref, sp_ref, sm_ref, out_ref):
        l_idx = pl.program_id(1)
        kpos = lax.broadcasted_iota(jnp.int32, (2 * BAND, BAND), 0)
        qpos = lax.broadcasted_iota(jnp.int32, (2 * BAND, BAND), 1) + BAND
        first_key = jnp.where(l_idx > 0, 0, BAND)
        mask_q = (kpos <= qpos) & (kpos >= qpos - BAND) & (kpos >= first_key)
        kpos2 = lax.broadcasted_iota(jnp.int32, (BAND, 2 * BAND), 0)
        qpos2 = lax.broadcasted_iota(jnp.int32, (BAND, 2 * BAND), 1)
        last_query = jnp.where(l_idx < nb - 1, 2 * BAND, BAND)
        mask_k = (kpos2 <= qpos2) & (kpos2 >= qpos2 - BAND) & (qpos2 < last_query)
        c, sp, sm = c_ref[...], sp_ref[...], sm_ref[...]
        lse_q, ds_q = lse_ref[...], ds_ref[...]
        lse_k = jnp.concatenate([lse_q, lsen_ref[...]], axis=1)
        ds_k = jnp.concatenate([ds_q, dsn_ref[...]], axis=1)
        for h0 in range(0, A_HEADS, A_HEAD_GROUP):
            hl = list(range(h0, h0 + A_HEAD_GROUP))
            hss = [slice(h * A_HEAD_DIM, (h + 1) * A_HEAD_DIM) for h in hl]
            k2s = [jnp.concatenate([kp_ref[:, hs], kc_ref[:, hs]], axis=0) for hs in hss]
            v2s = [jnp.concatenate([vp_ref[:, hs], vc_ref[:, hs]], axis=0) for hs in hss]
            q2s = [jnp.concatenate([q_ref[:, hs], qn_ref[:, hs]], axis=0) for hs in hss]
            do2s = [jnp.concatenate([do_ref[:, hs], don_ref[:, hs]], axis=0) for hs in hss]
            sts = [_dot_nt(k2, q_ref[:, hs]) for k2, hs in zip(k2s, hss)]
            dpts = [_dot_nt(v2, do_ref[:, hs]) for v2, hs in zip(v2s, hss)]
            st2s = [_dot_nt(kc_ref[:, hs], q2) for q2, hs in zip(q2s, hss)]
            dpt2s = [_dot_nt(vc_ref[:, hs], do2) for do2, hs in zip(do2s, hss)]
            dsts, dst2s, p2s = [], [], []
            for i, h in enumerate(hl):
                p = jnp.exp2(jnp.where(mask_q, sts[i] * A_C2, NEG) - lse_q[h:h + 1])
                dsts.append((p * (dpts[i] - ds_q[h:h + 1]) * A_SCALE).astype(BF16))
                p2 = jnp.exp2(jnp.where(mask_k, st2s[i] * A_C2, NEG) - lse_k[h:h + 1])
                dst2s.append((p2 * (dpt2s[i] - ds_k[h:h + 1]) * A_SCALE).astype(BF16))
                p2s.append(p2.astype(BF16))
            dqs = [_dot_tn(dsts[i], k2s[i]) for i in range(A_HEAD_GROUP)]
            dks = [_dot(dst2s[i], q2s[i]) for i in range(A_HEAD_GROUP)]
            dvs = [_dot(p2s[i], do2s[i]) for i in range(A_HEAD_GROUP)]
            for i, h in enumerate(hl):
                out_ref[:, hss[i]] = _rope_t(dqs[i], c, sp, sm).astype(BF16)
                out_ref[:, A_WIDTH + h * A_HEAD_DIM:A_WIDTH + (h + 1) * A_HEAD_DIM] = _rope_t(dks[i], c, sp, sm).astype(BF16)
                out_ref[:, 2 * A_WIDTH + h * A_HEAD_DIM:2 * A_WIDTH + (h + 1) * A_HEAD_DIM] = dvs[i].astype(BF16)

    def col(c, off):
        return lambda s, l: (s, jnp.clip(l + off, 0, nb - 1), c)

    def row(off):
        return pl.BlockSpec((None, A_HEADS, BAND), lambda s, l: (s, 0, jnp.clip(l + off, 0, nb - 1)))

    tspec = pl.BlockSpec((None, BAND, LANES), lambda s, l: (s, l, 0))
    in_specs = [pl.BlockSpec(blk, col(0, 0)), pl.BlockSpec(blk, col(0, 1)),
                pl.BlockSpec(blk, col(1, 0)), pl.BlockSpec(blk, col(1, -1)),
                pl.BlockSpec(blk, col(0, 0)), pl.BlockSpec(blk, col(0, -1)),
                pl.BlockSpec(blk, col(0, 0)), pl.BlockSpec(blk, col(0, 1)),
                row(0), row(1), row(0), row(1), tspec, tspec, tspec]
    return pl.pallas_call(
        body, name=name, grid=(s_n, nb), in_specs=in_specs,
        out_specs=pl.BlockSpec((None, BAND, 3 * A_WIDTH), lambda s, l: (s, l, 0)),
        out_shape=jax.ShapeDtypeStruct((s_n, ln, 3 * A_WIDTH), BF16),
        compiler_params=_cparams(("parallel", "arbitrary")))(
            qk, qk, qk, qk, v, v, do, do, lse2, lse2, dsum, dsum, *tabs)


B_TQ = 256
B_SCALE = B_QK_DIM ** -0.5
B_C2 = B_SCALE * LOG2E


def _b_row_tile(tq):
    return pl.BlockSpec((None, None, 8, tq), lambda b, j, i: (b, j, 0, i))


def _b_specs(t, tq, tk):
    pair_tile = pl.BlockSpec((None, tq, 2 * LANES), lambda b, j, i: (b, i, j))
    pair_full = pl.BlockSpec((None, t, 2 * LANES), lambda b, j, i: (b, 0, j))
    one_tile = pl.BlockSpec((None, tk, LANES), lambda b, j, i: (b, i, j))
    one_full = pl.BlockSpec((None, t, LANES), lambda b, j, i: (b, 0, j))
    tab_tile = pl.BlockSpec((None, tq, LANES), lambda b, j, i: (b, i, 0))
    return pair_tile, pair_full, one_tile, one_full, tab_tile


def _key_le_query(kb, qb, tk, tq):
    kpos = kb * tk + lax.broadcasted_iota(jnp.int32, (tk, tq), 0)
    qpos = qb * tq + lax.broadcasted_iota(jnp.int32, (tk, tq), 1)
    return kpos <= qpos


def _attn_b_fwd(q, kpad, vt1):
    bsz, t, _ = q.shape
    tq = tk = B_TQ
    nq = t // tq

    def body(q_ref, k_ref, vt_ref, o_ref, lse_ref):
        qblk = pl.program_id(2)
        qs = [q_ref[:, hh * LANES:(hh + 1) * LANES] for hh in range(2)]

        def scores(kb):
            start = pl.multiple_of(kb * tk, tk)
            return [_dot_nt(k_ref[pl.ds(start, tk), hh * LANES:(hh + 1) * LANES], qs[hh]) for hh in range(2)]

        def pv(kb, ps):
            start = pl.multiple_of(kb * tk, tk)
            return [_dot(vt_ref[hh * LANES:(hh + 1) * LANES, pl.ds(start, tk)], ps[hh]) for hh in range(2)]

        def softmax(ss, ms, accs, kb, masked):
            out_m, out_acc, out_p = [], [], []
            for hh in range(2):
                s = ss[hh] * B_C2
                if masked:
                    s = jnp.where(_key_le_query(kb, qblk, tk, tq), s, NEG)
                m_new = jnp.maximum(ms[hh], jnp.max(s, axis=0, keepdims=True))
                out_acc.append(jnp.exp2(ms[hh] - m_new) * accs[hh])
                out_p.append(jnp.exp2(s - m_new).astype(BF16))
                out_m.append(m_new)
            return out_m, out_acc, out_p

        def step(kb, carry):
            ss, ps, ms, accs = carry
            pvs = pv(jnp.maximum(kb - 1, 0), ps)
            ss_next = scores(kb + 1)
            accs = [accs[hh] + pvs[hh] for hh in range(2)]
            ms, accs, ps = softmax(ss, ms, accs, kb, False)
            return (ss_next, ps, ms, accs)

        init = (scores(0), [jnp.zeros((tk, tq), BF16)] * 2, [jnp.full((1, tq), NEG, F32)] * 2,
                [jnp.zeros((LANES, tq), F32)] * 2)
        ss, ps, ms, accs = lax.fori_loop(0, qblk, step, init)
        pvs = pv(jnp.maximum(qblk - 1, 0), ps)
        accs = [accs[hh] + pvs[hh] for hh in range(2)]
        ms, accs, ps = softmax(ss, ms, accs, qblk, True)
        pvs = pv(qblk, ps)
        accs = [accs[hh] + pvs[hh] for hh in range(2)]
        ls = [accs[hh][B_VDIM:B_VDIM + 1] for hh in range(2)]
        o_t = jnp.concatenate([accs[0][:B_VDIM] / ls[0], accs[1][:B_VDIM] / ls[1]], axis=0)
        o_ref[...] = o_t.T
        lse_ref[...] = jnp.concatenate([ms[0] + jnp.log2(ls[0]), ms[1] + jnp.log2(ls[1]),
                                        jnp.zeros((6, tq), F32)], axis=0)

    pair_tile, pair_full, one_tile, _, _ = _b_specs(t, tq, tk)
    vt_spec = pl.BlockSpec((None, 2 * LANES, t), lambda b, j, i: (b, j, 0))
    return pl.pallas_call(
        body, name="attn_b_fwd", grid=(bsz, B_HEADS // 2, nq),
        in_specs=[pair_tile, pair_full, vt_spec],
        out_specs=[one_tile, _b_row_tile(tq)],
        out_shape=[jax.ShapeDtypeStruct((bsz, t, B_HEADS * B_VDIM), F32),
                   jax.ShapeDtypeStruct((bsz, B_HEADS // 2, 8, t), F32)],
        compiler_params=_cparams(("parallel", "parallel", "arbitrary")))(q, kpad, vt1)


def _attn_b_bwd(q, kpad, v, do, o, lse2, tabs):
    bsz, t, _ = q.shape
    tq = tk = B_TQ
    nq = t // tq

    def body(q_ref, k_ref, v_ref, do_ref, o_ref, lse_ref, c_ref, sp_ref, sm_ref, dq_ref, dk_ref, dv_ref,
             dqt_scr, dsum_scr):
        lane = lax.broadcasted_iota(jnp.int32, (tk, LANES), 1)
        sel_lane = lax.broadcasted_iota(jnp.int32, (8, LANES), 1)
        sel_row = lax.broadcasted_iota(jnp.int32, (8, LANES), 0)
        sel = jnp.where((sel_lane < B_VDIM) == (sel_row == 0), 1.0, 0.0)
        sel = jnp.where(sel_row < 2, sel, 0.0).astype(BF16)

        def rows(blk):
            return pl.ds(pl.multiple_of(blk * tq, tq), tq)

        def dsum_step(qb, carry):
            prod = do_ref[rows(qb), :].astype(F32) * o_ref[rows(qb), :]
            hi = prod.astype(BF16)
            lo = (prod - hi.astype(F32)).astype(BF16)
            dsum_scr[:, rows(qb)] = _dot_nt(sel, hi) + _dot_nt(sel, lo)
            return carry

        lax.fori_loop(0, nq, dsum_step, 0)
        dqt_scr[...] = jnp.zeros_like(dqt_scr)

        def kv_step(kb, carry):
            ks = [k_ref[rows(kb), hh * LANES:(hh + 1) * LANES] for hh in range(2)]
            vb = v_ref[rows(kb), :]
            zero = jnp.zeros_like(vb)
            vs = [jnp.where(lane < B_VDIM, vb, zero), jnp.where(lane < B_VDIM, zero, vb)]

            def make_step(masked):
                def step(qb, acc):
                    qs = [q_ref[rows(qb), hh * LANES:(hh + 1) * LANES] for hh in range(2)]
                    do_b = do_ref[rows(qb), :]
                    ss = [_dot_nt(ks[hh], qs[hh]) for hh in range(2)]
                    dps = [_dot_nt(vs[hh], do_b) for hh in range(2)]
                    pbs, dss = [], []
                    for hh in range(2):
                        s = ss[hh] * B_C2
                        if masked:
                            s = jnp.where(_key_le_query(kb, qb, tk, tq), s, NEG)
                        p = jnp.exp2(s - lse_ref[hh:hh + 1, rows(qb)])
                        dss.append((p * (dps[hh] - dsum_scr[hh:hh + 1, rows(qb)]) * B_SCALE).astype(BF16))
                        pbs.append(p.astype(BF16))
                    for hh in range(2):
                        dqt_scr[hh, :, rows(qb)] += _dot_tn(ks[hh], dss[hh])
                    return (acc[0] + _dot(dss[0], qs[0]), acc[1] + _dot(dss[1], qs[1]),
                            acc[2] + _dot(pbs[0], do_b), acc[3] + _dot(pbs[1], do_b))
                return step

            acc = make_step(True)(kb, (jnp.zeros((tk, LANES), F32),) * 4)
            acc = lax.fori_loop(kb + 1, nq, make_step(False), acc)
            dk_ref[rows(kb), :LANES] = acc[0].astype(BF16)
            dk_ref[rows(kb), LANES:] = acc[1].astype(BF16)
            dv_ref[rows(kb), :] = jnp.where(lane < B_VDIM, acc[2], acc[3]).astype(BF16)
            return carry

        lax.fori_loop(0, nq, kv_step, 0)

        def dq_step(qb, carry):
            c, sp, sm = c_ref[rows(qb), :], sp_ref[rows(qb), :], sm_ref[rows(qb), :]
            for hh in range(2):
                dq_ref[rows(qb), hh * LANES:(hh + 1) * LANES] = _rope_t(dqt_scr[hh, :, rows(qb)].T, c, sp, sm).astype(BF16)
            return carry

        lax.fori_loop(0, nq, dq_step, 0)

    pair_full = pl.BlockSpec((None, t, 2 * LANES), lambda b, j: (b, 0, j))
    one_full = pl.BlockSpec((None, t, LANES), lambda b, j: (b, 0, j))
    row_full = pl.BlockSpec((None, None, 8, t), lambda b, j: (b, j, 0, 0))
    tab_full = pl.BlockSpec((None, t, LANES), lambda b, j: (b, 0, 0))
    return pl.pallas_call(
        body, name="attn_b_bwd", grid=(bsz, B_HEADS // 2),
        in_specs=[pair_full, pair_full, one_full, one_full, one_full, row_full, tab_full, tab_full, tab_full],
        out_specs=[pair_full, pair_full, one_full],
        out_shape=[jax.ShapeDtypeStruct((bsz, t, B_KPAD), BF16), jax.ShapeDtypeStruct((bsz, t, B_KPAD), BF16),
                   jax.ShapeDtypeStruct((bsz, t, B_HEADS * B_VDIM), BF16)],
        scratch_shapes=[pltpu.VMEM((2, LANES, t), F32), pltpu.VMEM((8, t), F32)],
        compiler_params=_cparams(("parallel", "parallel")))(q, kpad, v, do, o, lse2, *tabs)


def _attn_b_dq(q, kpad, v, do, o, lse2, tabs):
    bsz, t, _ = q.shape
    tq = tk = B_TQ
    nq = t // tq

    def body(q_ref, k_ref, v_ref, do_ref, o_ref, lse_ref, c_ref, sp_ref, sm_ref, dq_ref, dsum_ref):
        qblk = pl.program_id(2)
        lane = lax.broadcasted_iota(jnp.int32, (tq, LANES), 1)
        do_b = do_ref[...]
        prod = do_b.astype(F32) * o_ref[...]
        hi = prod.astype(BF16)
        lo = (prod - hi.astype(F32)).astype(BF16)
        sel_lane = lax.broadcasted_iota(jnp.int32, (8, LANES), 1)
        sel_row = lax.broadcasted_iota(jnp.int32, (8, LANES), 0)
        sel = jnp.where((sel_lane < B_VDIM) == (sel_row == 0), 1.0, 0.0)
        sel = jnp.where(sel_row < 2, sel, 0.0).astype(BF16)
        dsum = _dot_nt(sel, hi) + _dot_nt(sel, lo)
        dsum_ref[...] = dsum
        lse = lse_ref[...]
        qs = [q_ref[:, hh * LANES:(hh + 1) * LANES] for hh in range(2)]
        zero = jnp.zeros_like(do_b)
        dos = [jnp.where(lane < B_VDIM, do_b, zero), jnp.where(lane < B_VDIM, zero, do_b)]

        def kblock(kb, hh):
            start = pl.multiple_of(kb * tk, tk)
            return k_ref[pl.ds(start, tk), hh * LANES:(hh + 1) * LANES]

        def dots(kb):
            start = pl.multiple_of(kb * tk, tk)
            vb = v_ref[pl.ds(start, tk), :]
            return ([_dot_nt(kblock(kb, hh), qs[hh]) for hh in range(2)], [_dot_nt(vb, dos[hh]) for hh in range(2)])

        def ds_of(ss, dps, kb, masked):
            out = []
            for hh in range(2):
                s = ss[hh] * B_C2
                if masked:
                    s = jnp.where(_key_le_query(kb, qblk, tk, tq), s, NEG)
                p = jnp.exp2(s - lse[hh:hh + 1])
                out.append((p * (dps[hh] - dsum[hh:hh + 1]) * B_SCALE).astype(BF16))
            return out

        def accum(acc, kb, dss):
            return [acc[hh] + _dot_tn(kblock(kb, hh), dss[hh]) for hh in range(2)]

        def step(kb, carry):
            dss, acc = carry
            ss, dps = dots(kb)
            acc = accum(acc, jnp.maximum(kb - 1, 0), dss)
            return (ds_of(ss, dps, kb, False), acc)

        init = ([jnp.zeros((tk, tq), BF16)] * 2, [jnp.zeros((LANES, tq), F32)] * 2)
        dss, acc = lax.fori_loop(0, qblk, step, init)
        ss, dps = dots(qblk)
        acc = accum(acc, jnp.maximum(qblk - 1, 0), dss)
        acc = accum(acc, qblk, ds_of(ss, dps, qblk, True))
        c, sp, sm = c_ref[...], sp_ref[...], sm_ref[...]
        for hh in range(2):
            dq_ref[:, hh * LANES:(hh + 1) * LANES] = _rope_t(acc[hh].T, c, sp, sm).astype(BF16)

    pair_tile, pair_full, one_tile, one_full, tab_tile = _b_specs(t, tq, tk)
    row_tile = _b_row_tile(tq)
    return pl.pallas_call(
        body, name="attn_b_dq", grid=(bsz, B_HEADS // 2, nq),
        in_specs=[pair_tile, pair_full, one_full, one_tile, one_tile, row_tile, tab_tile, tab_tile, tab_tile],
        out_specs=[pair_tile, row_tile],
        out_shape=[jax.ShapeDtypeStruct((bsz, t, B_KPAD), BF16),
                   jax.ShapeDtypeStruct((bsz, B_HEADS // 2, 8, t), F32)],
        compiler_params=_cparams(("parallel", "parallel", "arbitrary")))(q, kpad, v, do, o, lse2, *tabs)


def _attn_b_dkv(q, kpad, v, do, lse2, dsum):
    bsz, t, _ = q.shape
    tq = tk = B_TQ
    nq = t // tq

    def body(q_ref, k_ref, v_ref, do_ref, lse_ref, dsum_ref, dk_ref, dv_ref):
        kblk = pl.program_id(2)
        lane = lax.broadcasted_iota(jnp.int32, (tk, LANES), 1)
        ks = [k_ref[:, hh * LANES:(hh + 1) * LANES] for hh in range(2)]
        vb = v_ref[...]
        zero = jnp.zeros_like(vb)
        vs = [jnp.where(lane < B_VDIM, vb, zero), jnp.where(lane < B_VDIM, zero, vb)]

        def make_step(masked):
            def step(qb, carry):
                start = pl.multiple_of(qb * tq, tq)
                qs = [q_ref[pl.ds(start, tq), hh * LANES:(hh + 1) * LANES] for hh in range(2)]
                do_b = do_ref[pl.ds(start, tq), :]
                ss = [_dot_nt(ks[hh], qs[hh]) for hh in range(2)]
                dps = [_dot_nt(vs[hh], do_b) for hh in range(2)]
                pbs, dss = [], []
                for hh in range(2):
                    s = ss[hh] * B_C2
                    if masked:
                        s = jnp.where(_key_le_query(kblk, qb, tk, tq), s, NEG)
                    p = jnp.exp2(s - lse_ref[hh:hh + 1, pl.ds(start, tq)])
                    dss.append((p * (dps[hh] - dsum_ref[hh:hh + 1, pl.ds(start, tq)]) * B_SCALE).astype(BF16))
                    pbs.append(p.astype(BF16))
                return (carry[0] + _dot(dss[0], qs[0]), carry[1] + _dot(dss[1], qs[1]),
                        carry[2] + _dot(pbs[0], do_b), carry[3] + _dot(pbs[1], do_b))
            return step

        init = (jnp.zeros((tk, LANES), F32),) * 4
        carry = make_step(True)(kblk, init)
        carry = lax.fori_loop(kblk + 1, nq, make_step(False), carry)
        dk_ref[:, :LANES] = carry[0].astype(BF16)
        dk_ref[:, LANES:] = carry[1].astype(BF16)
        dv_ref[...] = jnp.where(lane < B_VDIM, carry[2], carry[3]).astype(BF16)

    pair_tile, pair_full, one_tile, one_full, _ = _b_specs(t, tq, tk)
    row_full = pl.BlockSpec((None, None, 8, t), lambda b, j, i: (b, j, 0, 0))
    return pl.pallas_call(
        body, name="attn_b_dkv", grid=(bsz, B_HEADS // 2, nq),
        in_specs=[pair_full, pair_tile, one_tile, one_full, row_full, row_full],
        out_specs=[pair_tile, one_tile],
        out_shape=[jax.ShapeDtypeStruct((bsz, t, B_KPAD), BF16),
                   jax.ShapeDtypeStruct((bsz, t, B_HEADS * B_VDIM), BF16)],
        compiler_params=_cparams(("parallel", "parallel", "arbitrary")))(q, kpad, v, do, lse2, dsum)


ANY = pl.BlockSpec(memory_space=pl.ANY)


def _all_gather(shard, name):
    def body(x_ref, out_ref, send_sems, recv_sems, local_sem):
        x, y, c = lax.axis_index("x"), lax.axis_index("y"), lax.axis_index("c")
        me, sibling = (x, y, c), (x, y, 1 - c)
        chips = [(1 - x, y), (x, 1 - y), (1 - x, 1 - y)]

        def rows(px, py, pc):
            return out_ref.at[4 * px + 2 * py + pc]

        def copy(k, block, to, src=None):
            return pltpu.make_async_remote_copy(
                src_ref=rows(*block) if src is None else src, dst_ref=rows(*block),
                send_sem=send_sems.at[k], recv_sem=recv_sems.at[k], device_id=to, device_id_type=MESH)

        mine = pltpu.make_async_copy(x_ref, rows(*me), local_sem)
        mine.start()
        first = [copy(0, me, sibling, src=x_ref)]
        first += [copy(1 + j, me, (*chip, c), src=x_ref) for j, chip in enumerate(chips)]
        for cp in first:
            cp.start()
        passed = [copy(4 + j, (*chip, c), sibling) for j, chip in enumerate(chips)]
        for j, chip in enumerate(chips):
            copy(1 + j, (*chip, c), me).wait_recv()
            passed[j].start()
        copy(0, sibling, me).wait_recv()
        for j, chip in enumerate(chips):
            copy(4 + j, (*chip, 1 - c), me).wait_recv()
        for cp in first + passed:
            cp.wait_send()
        mine.wait()

    return pl.pallas_call(
        body, name=name, in_specs=[ANY], out_specs=ANY,
        out_shape=jax.ShapeDtypeStruct((N_DEV,) + shard.shape, shard.dtype),
        scratch_shapes=[pltpu.SemaphoreType.DMA((7,)), pltpu.SemaphoreType.DMA((7,)), pltpu.SemaphoreType.DMA])(shard)


N_CHIPS = 4


def _exchange_sibling(blocks, name):
    def body(g_ref, got_ref, send_sems, recv_sems):
        x, y, c = lax.axis_index("x"), lax.axis_index("y"), lax.axis_index("c")
        sends = [pltpu.make_async_remote_copy(
            src_ref=g_ref.at[2 * q + (1 - c)], dst_ref=got_ref.at[q], send_sem=send_sems.at[q],
            recv_sem=recv_sems.at[q], device_id=(x, y, 1 - c), device_id_type=MESH) for q in range(N_CHIPS)]
        for cp in sends:
            cp.start()
        for cp in sends:
            cp.wait_recv()
        for cp in sends:
            cp.wait_send()

    return pl.pallas_call(
        body, name=name, in_specs=[ANY], out_specs=ANY,
        out_shape=jax.ShapeDtypeStruct((N_CHIPS,) + blocks.shape[1:], blocks.dtype),
        scratch_shapes=[pltpu.SemaphoreType.DMA((N_CHIPS,)), pltpu.SemaphoreType.DMA((N_CHIPS,))])(blocks)


def _exchange_chips(parts, name):
    def body(p_ref, out_ref, send_sems, recv_sems, local_sem):
        x, y, c = lax.axis_index("x"), lax.axis_index("y"), lax.axis_index("c")
        me = 2 * x + y

        def peer(k):
            return (1 - x if k & 2 else x, 1 - y if k & 1 else y)

        def copy(k):
            px, py = peer(k)
            return pltpu.make_async_remote_copy(
                src_ref=p_ref.at[2 * px + py], dst_ref=out_ref.at[me], send_sem=send_sems.at[k - 1],
                recv_sem=recv_sems.at[k - 1], device_id=(px, py, c), device_id_type=MESH)

        def arrival(k):
            px, py = peer(k)
            slot = out_ref.at[2 * px + py]
            return pltpu.make_async_remote_copy(
                src_ref=slot, dst_ref=slot, send_sem=send_sems.at[k - 1], recv_sem=recv_sems.at[k - 1],
                device_id=(px, py, c), device_id_type=MESH)

        mine = pltpu.make_async_copy(p_ref.at[me], out_ref.at[me], local_sem)
        mine.start()
        sends = [copy(k) for k in range(1, N_CHIPS)]
        for cp in sends:
            cp.start()
        for k in range(1, N_CHIPS):
            arrival(k).wait_recv()
        for cp in sends:
            cp.wait_send()
        mine.wait()

    return pl.pallas_call(
        body, name=name, in_specs=[ANY], out_specs=ANY,
        out_shape=jax.ShapeDtypeStruct(parts.shape, parts.dtype),
        scratch_shapes=[pltpu.SemaphoreType.DMA((N_CHIPS - 1,)), pltpu.SemaphoreType.DMA((N_CHIPS - 1,)),
                        pltpu.SemaphoreType.DMA])(parts)


def _add_pairs(blocks, got, core, *, tr, name):
    q, r, c = got.shape

    def body(core_ref, a_ref, b_ref, o_ref):
        o_ref[...] = (a_ref[...].astype(F32) + b_ref[...].astype(F32)).astype(o_ref.dtype)

    spec = pl.BlockSpec((q, tr, c), lambda i, core_ref: (0, i, 0))
    mine = pl.BlockSpec((q, None, tr, c), lambda i, core_ref: (0, core_ref[0], i, 0))
    return pl.pallas_call(
        body, name=name,
        grid_spec=pltpu.PrefetchScalarGridSpec(num_scalar_prefetch=1, grid=(r // tr,), in_specs=[mine, spec],
                                               out_specs=spec),
        out_shape=jax.ShapeDtypeStruct(got.shape, BF16), compiler_params=_cparams(("parallel",)))(
            core, blocks.reshape(q, 2, r, c), got)


def _sum_slots(slots, *, tr, name):
    n_slots, r, c = slots.shape

    def body(s_ref, o_ref):
        acc = s_ref[0].astype(F32)
        for s in range(1, n_slots):
            acc = acc + s_ref[s].astype(F32)
        o_ref[...] = acc

    return pl.pallas_call(
        body, name=name, grid=(r // tr,),
        in_specs=[pl.BlockSpec((n_slots, tr, c), lambda i: (0, i, 0))],
        out_specs=pl.BlockSpec((tr, c), lambda i: (i, 0)),
        out_shape=jax.ShapeDtypeStruct((r, c), F32),
        compiler_params=_cparams(("parallel",)))(slots)


def _adamw(w, g, m, v, name):
    shape = w.shape
    cols = shape[-1]
    args = [a.reshape(-1, cols) for a in (w, g, m, v)]
    rows = args[0].shape[0]
    tm = 256 if rows % 256 == 0 else rows

    def fn(w_t, g_t, m_t, v_t):
        m_n = ADAM_B1 * m_t + (1.0 - ADAM_B1) * g_t
        v_n = ADAM_B2 * v_t + (1.0 - ADAM_B2) * (g_t * g_t)
        m_hat = m_n / (1.0 - ADAM_B1 ** ADAM_STEP)
        v_hat = v_n / (1.0 - ADAM_B2 ** ADAM_STEP)
        delta = -ADAM_LR * (m_hat / (jnp.sqrt(v_hat) + ADAM_EPS) + ADAM_WD * w_t)
        return delta, m_n, v_n

    delta, m_n, v_n = _rowwise(fn, args, [], [(cols, F32)] * 3, tm=tm, name=name)
    return delta.reshape(shape), m_n.reshape(shape), v_n.reshape(shape)


def _local_step(x, positions, target, w):
    bsz, t, _ = x.shape
    n = bsz * t
    tm = 256
    mm = 512
    x2 = x.reshape(n, D_MODEL)
    tgt2 = target.reshape(n, D_MODEL)
    pos = positions.reshape(n)
    tb2 = _rope_tables(pos, B_ROPE_THETA, B_ROPE, B_NOPE, "rope_tables_b")
    tabs_b = [a.reshape(bsz, t, LANES) for a in tb2]

    w_a_in = w["a_w_in"]
    w_qkv, w_z = w_a_in[:, :A_QKV], w_a_in[:, A_QKV:]
    w_a_out = w["a_w_out"]
    w_down = w["kv_w_down"]
    w_down_p = jnp.zeros((D_MODEL, 3 * LANES), BF16).at[:, :B_KV_LORA].set(w_down[:, :B_KV_LORA])
    w_down_p = w_down_p.at[:, B_KV_LORA + B_NOPE:B_KV_LORA + B_QK_DIM].set(w_down[:, B_KV_LORA:])
    wu = w["kv_w_up"].reshape(B_KV_LORA, B_HEADS, B_NOPE + B_VDIM)
    w_upk = jnp.pad(wu[:, :, :B_NOPE], ((0, 0), (0, 0), (0, LANES - B_NOPE))).reshape(B_KV_LORA, B_KPAD)
    w_upv = wu[:, :, B_NOPE:].reshape(B_KV_LORA, B_HEADS * B_VDIM)
    w_b_in = w["b_w_in"]
    w_q_p = jnp.pad(w["b_w_q_up"].reshape(B_Q_LORA, B_HEADS, B_QK_DIM),
                    ((0, 0), (0, 0), (0, LANES - B_QK_DIM))).reshape(B_Q_LORA, B_KPAD)
    w_b_out = w["b_w_out"]

    def tab_extras(tabs2, rows):
        return [(a, (rows, LANES), lambda j, i, kk: (i, 0)) for a in tabs2]

    (hn_a,) = _rowwise(lambda xt, g: (_rms(xt, g),), [x2], [w["a_pre_norm"]], [(D_MODEL, BF16)],
                       tm=tm, name="a_pre_norm")

    def rope_epilogue(acc, o_ref, rows, c_ref, sp_ref, sm_ref):
        c, sp, sm = c_ref[rows, :], sp_ref[rows, :], sm_ref[rows, :]
        for h in range(acc.shape[1] // LANES):
            hs = slice(h * LANES, (h + 1) * LANES)
            o_ref[rows, hs] = _rope(acc[:, hs], c, sp, sm).astype(BF16)

    def to_group(a, d):
        if d == 1:
            return a
        return a.reshape(bsz, t // d, d, a.shape[-1]).transpose(0, 2, 1, 3).reshape(n, a.shape[-1])

    def from_group(a, d):
        if d == 1:
            return a
        return a.reshape(bsz, d, t // d, a.shape[-1]).transpose(0, 2, 1, 3).reshape(n, a.shape[-1])

    def rows_to_cols(r, d):
        return r.reshape(bsz, d, A_HEADS, t // d).transpose(0, 3, 1, 2).reshape(n, A_HEADS)

    def cols_to_rows(cc, d):
        return cc.reshape(bsz, t // d, d, A_HEADS).transpose(0, 2, 3, 1).reshape(bsz * d, A_HEADS, t // d)

    z_a = _matmul(hn_a, w_z, out_dtype=F32, tm=mm, tn=A_WIDTH, name="a_gate")
    hn_g, tabs_g, qk_g, v_g, o_g, lse_g = [], [], [], [], [], []
    for g, d in enumerate(A_DILATIONS):
        hn_g.append(to_group(hn_a, d))
        tabs_g.append(_rope_tables(to_group(pos.reshape(n, 1), d).reshape(n), A_ROPE_THETA, A_ROT_DIM, 0,
                                   f"rope_tables_a_g{g}"))
        w_g = w_qkv[:, g * 3 * A_WIDTH:(g + 1) * 3 * A_WIDTH]
        qk = _matmul(hn_g[g], w_g[:, :2 * A_WIDTH], out_dtype=BF16, tm=mm, tn=A_WIDTH, name=f"a_qk_g{g}",
                     epilogue=rope_epilogue, epilogue_rows=EPILOGUE_ROWS, extras=tab_extras(tabs_g[g], mm))
        v = _matmul(hn_g[g], w_g[:, 2 * A_WIDTH:], out_dtype=BF16, tm=mm, tn=A_WIDTH, name=f"a_v_g{g}")
        qk_g.append(qk.reshape(bsz * d, t // d, 2 * A_WIDTH))
        v_g.append(v.reshape(bsz * d, t // d, A_WIDTH))
        o, lse = _attn_a_fwd(qk_g[g], v_g[g], f"attn_a_fwd_g{g}")
        o_g.append(from_group(o.reshape(n, A_WIDTH), d))
        lse_g.append(rows_to_cols(lse, d))

    def merge_fn(o0, o1, o2, l0, l1, l2, z):
        lmax = jnp.maximum(jnp.maximum(l0, l1), l2)
        e0, e1, e2 = jnp.exp2(l0 - lmax), jnp.exp2(l1 - lmax), jnp.exp2(l2 - lmax)
        den = e0 + e1 + e2
        w0, w1, w2 = e0 / den, e1 / den, e2 / den
        parts = []
        for h in range(A_HEADS):
            hs = slice(h * A_HEAD_DIM, (h + 1) * A_HEAD_DIM)
            parts.append(w0[:, h:h + 1] * o0[:, hs] + w1[:, h:h + 1] * o1[:, hs] + w2[:, h:h + 1] * o2[:, hs])
        o = jnp.concatenate(parts, axis=1)
        return o * _silu(z), o, lmax + jnp.log2(den)

    y_a, o_a2, lse_a = _rowwise(merge_fn, [*o_g, *lse_g, z_a], [],
                                [(A_WIDTH, BF16), (A_WIDTH, F32), (A_HEADS, F32)], tm=tm, name="a_merge_gate")
    out_a = _matmul(y_a, w_a_out, out_dtype=F32, tm=mm, tn=D_MODEL, name="a_out")

    def mid_fn(xt, out, g_post, g_kv, g_b):
        h1 = xt + _rms(out, g_post)
        return h1, _rms(h1, g_kv), _rms(h1, g_b)

    h1, hn_kv, hn_b = _rowwise(mid_fn, [x2, out_a], [w["a_post_norm"], w["kv_norm"], w["b_pre_norm"]],
                               [(D_MODEL, F32), (D_MODEL, BF16), (D_MODEL, BF16)], tm=tm, name="a_post_norm")

    ckr = _matmul(hn_kv, w_down_p, out_dtype=F32, tm=mm, tn=3 * LANES, name="kv_down")

    def latent_fn(ck, c, sp, sm, g):
        return _rms(ck[:, :B_KV_LORA], g), _rope(ck[:, B_KV_LORA:], c, sp, sm)

    c_kv, k_rope = _rowwise(latent_fn, [ckr, *tb2], [w["kv_latent_norm"]], [(B_KV_LORA, BF16), (LANES, F32)],
                            tm=tm, name="kv_latent_norm")

    def kpad_epilogue(acc, o_ref, rows, kr_ref):
        kr = kr_ref[rows, :]
        for h in range(acc.shape[1] // LANES):
            hs = slice(h * LANES, (h + 1) * LANES)
            o_ref[rows, hs] = (acc[:, hs] + kr).astype(BF16)

    kpad = _matmul(c_kv, w_upk, out_dtype=BF16, tm=mm, tn=1024, name="kv_up_k", epilogue=kpad_epilogue,
                   epilogue_rows=EPILOGUE_ROWS,
                   extras=[(k_rope, (mm, LANES), lambda j, i, kk: (i, 0))])
    v_b = _matmul(c_kv, w_upv, out_dtype=BF16, tm=mm, tn=1024, name="kv_up_v")

    proj_b = _matmul(hn_b, w_b_in, out_dtype=F32, tm=mm, tn=w_b_in.shape[1], name="b_in")
    (c_q,) = _rowwise(lambda p, g: (_rms(p[:, :B_Q_LORA], g),), [proj_b], [w["b_q_norm"]], [(B_Q_LORA, BF16)],
                      tm=tm, name="b_q_norm")

    q_b = _matmul(c_q, w_q_p, out_dtype=BF16, tm=mm, tn=1024, name="b_q_up", epilogue=rope_epilogue,
                  epilogue_rows=EPILOGUE_ROWS,
                  extras=tab_extras(tb2, mm))
    q_b3, kpad3, v_b3 = q_b.reshape(bsz, t, B_KPAD), kpad.reshape(bsz, t, B_KPAD), v_b.reshape(bsz, t, -1)
    w_vt = jnp.pad(wu[:, :, B_NOPE:], ((0, 0), (0, 0), (0, LANES - B_VDIM))).reshape(B_KV_LORA, B_KPAD)
    vt1 = _value_heads_t(c_kv, w_vt, bsz, t)
    o_b, lse_b = _attn_b_fwd(q_b3, kpad3, vt1)
    o_b2 = o_b.reshape(n, -1)
    (y_b,) = _rowwise(lambda o, p: (o * _silu(p[:, B_Q_LORA:]),), [o_b2, proj_b], [], [(D_MODEL, BF16)],
                      tm=tm, name="b_gate_mul")
    out_b = _matmul(y_b, w_b_out, out_dtype=F32, tm=mm, tn=D_MODEL, name="b_out")

    def head_fn(h1t, out, tgt, g):
        e = h1t + _rms(out, g) - tgt
        loss_row = 0.5 * jnp.mean(e * e, axis=-1, keepdims=True)
        dh2 = e * (1.0 / D_MODEL)
        d_out, dg = _rms_bwd(out, g, dh2)
        return dh2, d_out, dg, jnp.broadcast_to(loss_row * (1.0 / LANES), (loss_row.shape[0], LANES))

    dh2, d_out_b, dg_b_post, loss_acc = _rowwise(
        head_fn, [h1, out_b, tgt2], [w["b_post_norm"]], [(D_MODEL, F32), (D_MODEL, BF16)], [D_MODEL, LANES],
        tm=tm, name="loss_head")

    dy_b = _matmul(d_out_b, w_b_out, tb=True, out_dtype=F32, tm=mm, tn=D_MODEL, name="b_out_dx")
    gw_b_out = _matmul(y_b, d_out_b, ta=True, out_dtype=F32, tm=mm, tn=D_MODEL, tk=2048, name="b_out_dw")

    def gate_b_bwd(dy, o, p):
        z = p[:, B_Q_LORA:]
        return dy * _silu(z), dy * o * _silu_grad(z)

    do_b, dz_b = _rowwise(gate_b_bwd, [dy_b, o_b2, proj_b], [], [(D_MODEL, BF16), (D_MODEL, F32)],
                          tm=tm, name="b_gate_bwd")
    do_b3 = do_b.reshape(bsz, t, -1)
    dq_b, dk_b, dv_b = _attn_b_bwd(q_b3, kpad3, v_b3, do_b3, o_b, lse_b, tabs_b)
    dq_b2, dk_b2, dv_b2 = dq_b.reshape(n, B_KPAD), dk_b.reshape(n, B_KPAD), dv_b.reshape(n, -1)

    dc_kv = _matmul(dk_b2, w_upk, tb=True, out_dtype=F32, tm=mm, tn=B_KV_LORA, name="kv_up_k_dx")
    dc_kv = _matmul(dv_b2, w_upv, tb=True, out_dtype=F32, tm=mm, tn=B_KV_LORA, name="kv_up_v_dx",
                    epilogue=_add_epilogue, extras=[(dc_kv, (mm, B_KV_LORA), lambda j, i, kk: (i, j))])
    gw_upk = _matmul(c_kv, dk_b2, ta=True, out_dtype=F32, tm=B_KV_LORA, tn=1024, tk=2048, name="kv_up_k_dw")
    gw_upv = _matmul(c_kv, dv_b2, ta=True, out_dtype=F32, tm=B_KV_LORA, tn=1024, tk=2048, name="kv_up_v_dw")

    def latent_bwd(ck, dck, dk, c, sp, sm, g):
        d1, dg = _rms_bwd(ck[:, :B_KV_LORA], g, dck)
        ksum = dk[:, :LANES].astype(F32)
        for h in range(1, B_HEADS):
            ksum = ksum + dk[:, h * LANES:(h + 1) * LANES].astype(F32)
        lane = lax.broadcasted_iota(jnp.int32, ksum.shape, 1)
        ksum = jnp.where((lane >= B_NOPE) & (lane < B_QK_DIM), ksum, 0.0)
        return jnp.concatenate([d1, _rope_t(ksum, c, sp, sm)], axis=1), dg

    dckr, dg_latent = _rowwise(latent_bwd, [ckr, dc_kv, dk_b2, *tb2], [w["kv_latent_norm"]],
                               [(3 * LANES, BF16)], [B_KV_LORA], tm=tm, name="kv_latent_bwd")
    dhn_kv = _matmul(dckr, w_down_p, tb=True, out_dtype=F32, tm=mm, tn=D_MODEL, name="kv_down_dx")
    gw_down_p = _matmul(hn_kv, dckr, ta=True, out_dtype=F32, tm=mm, tn=3 * LANES, tk=2048, name="kv_down_dw")

    dc_q = _matmul(dq_b2, w_q_p, tb=True, out_dtype=F32, tm=mm, tn=B_Q_LORA, name="b_q_up_dx")
    gw_q_p = _matmul(c_q, dq_b2, ta=True, out_dtype=F32, tm=B_Q_LORA, tn=1024, tk=2048, name="b_q_up_dw")

    def q_norm_bwd(p, dcq, dz, g):
        d1, dg = _rms_bwd(p[:, :B_Q_LORA], g, dcq)
        return jnp.concatenate([d1, dz], axis=1), dg

    dproj_b, dg_q_norm = _rowwise(q_norm_bwd, [proj_b, dc_q, dz_b], [w["b_q_norm"]],
                                  [(w_b_in.shape[1], BF16)], [B_Q_LORA], tm=tm, name="b_q_norm_bwd")
    dhn_b = _matmul(dproj_b, w_b_in, tb=True, out_dtype=F32, tm=mm, tn=D_MODEL, name="b_in_dx")
    gw_b_in = _matmul(hn_b, dproj_b, ta=True, out_dtype=F32, tm=mm, tn=w_b_in.shape[1], tk=2048, name="b_in_dw")

    def mid_bwd(h1t, dh2t, dkv, db, g_kv, g_b, g_post, out):
        dxa, ra = _rms_bwd(h1t, g_kv, dkv)
        dxb, rb = _rms_bwd(h1t, g_b, db)
        dh1 = dh2t + dxa + dxb
        d_out, rp = _rms_bwd(out, g_post, dh1)
        return dh1, d_out, ra, rb, rp

    def mid_bwd_fn(h1t, dh2t, dkv, db, out, g_kv, g_b, g_post):
        return mid_bwd(h1t, dh2t, dkv, db, g_kv, g_b, g_post, out)

    dh1, d_out_a, dg_kv, dg_b_pre, dg_a_post = _rowwise(
        mid_bwd_fn, [h1, dh2, dhn_kv, dhn_b, out_a], [w["kv_norm"], w["b_pre_norm"], w["a_post_norm"]],
        [(D_MODEL, F32), (D_MODEL, BF16)], [D_MODEL] * 3, tm=tm, name="mid_bwd")

    dy_a = _matmul(d_out_a, w_a_out, tb=True, out_dtype=F32, tm=mm, tn=A_WIDTH, name="a_out_dx")
    gw_a_out = _matmul(y_a, d_out_a, ta=True, out_dtype=F32, tm=mm, tn=D_MODEL, tk=2048, name="a_out_dw")

    def gate_a_bwd(dy, o, z):
        do = dy * _silu(z)
        prod = do * o
        lane = lax.broadcasted_iota(jnp.int32, (prod.shape[0], A_HEADS), 1)
        dsum = jnp.zeros((prod.shape[0], A_HEADS), F32)
        for h in range(A_HEADS):
            col = jnp.sum(prod[:, h * A_HEAD_DIM:(h + 1) * A_HEAD_DIM], axis=1, keepdims=True)
            dsum = jnp.where(lane == h, col, dsum)
        return do, dy * o * _silu_grad(z), dsum

    do_a, dz_a, dsum_a = _rowwise(gate_a_bwd, [dy_a, o_a2, z_a], [],
                                  [(A_WIDTH, BF16), (A_WIDTH, BF16), (A_HEADS, F32)], tm=tm, name="a_gate_bwd")
    dhn_a = _matmul(dz_a, w_z, tb=True, out_dtype=F32, tm=mm, tn=D_MODEL, name="a_gate_dx")
    gw_parts = []
    for g, d in enumerate(A_DILATIONS):
        s_n, ln = bsz * d, t // d
        dqkv = _attn_a_bwd(qk_g[g], v_g[g], to_group(do_a, d).reshape(s_n, ln, A_WIDTH), cols_to_rows(lse_a, d),
                           cols_to_rows(dsum_a, d), [a.reshape(s_n, ln, LANES) for a in tabs_g[g]],
                           f"attn_a_bwd_g{g}").reshape(n, 3 * A_WIDTH)
        w_g = w_qkv[:, g * 3 * A_WIDTH:(g + 1) * 3 * A_WIDTH]
        if d == 1:
            dhn_a = _matmul(dqkv, w_g, tb=True, out_dtype=F32, tm=mm, tn=D_MODEL, name=f"a_qkv_dx_g{g}",
                            epilogue=_add_epilogue, extras=[(dhn_a, (mm, D_MODEL), lambda j, i, kk: (i, j))])
        else:
            dhn_a = dhn_a + from_group(_matmul(dqkv, w_g, tb=True, out_dtype=F32, tm=mm, tn=D_MODEL,
                                               name=f"a_qkv_dx_g{g}"), d)
        gw_parts.append(_matmul(hn_g[g], dqkv, ta=True, out_dtype=F32, tm=mm, tn=1024, tk=2048,
                                name=f"a_qkv_dw_g{g}"))
    gw_parts.append(_matmul(hn_a, dz_a, ta=True, out_dtype=F32, tm=mm, tn=1024, tk=2048, name="a_gate_dw"))
    gw_a_in = jnp.concatenate(gw_parts, axis=1)

    def first_bwd(xt, dhn, dh1t, g):
        dx, dg = _rms_bwd(xt, g, dhn)
        return dh1t + dx, dg

    grad_x, dg_a_pre = _rowwise(first_bwd, [x2, dhn_a, dh1], [w["a_pre_norm"]], [(D_MODEL, F32)], [D_MODEL],
                                tm=tm, name="a_pre_norm_bwd")

    gw_down = jnp.concatenate([gw_down_p[:, :B_KV_LORA], gw_down_p[:, B_KV_LORA + B_NOPE:B_KV_LORA + B_QK_DIM]], axis=1)
    gw_up = jnp.concatenate([gw_upk.reshape(B_KV_LORA, B_HEADS, LANES)[:, :, :B_NOPE],
                             gw_upv.reshape(B_KV_LORA, B_HEADS, B_VDIM)], axis=2).reshape(B_KV_LORA, -1)
    gw_q_up = gw_q_p.reshape(B_Q_LORA, B_HEADS, LANES)[:, :, :B_QK_DIM].reshape(B_Q_LORA, -1)
    grads = {"a_w_in": gw_a_in, "a_w_out": gw_a_out, "kv_w_down": gw_down, "kv_w_up": gw_up,
             "b_w_in": gw_b_in, "b_w_q_up": gw_q_up, "b_w_out": gw_b_out}
    gains = {"a_pre_norm": dg_a_pre, "a_post_norm": dg_a_post, "kv_norm": dg_kv, "kv_latent_norm": dg_latent,
             "b_pre_norm": dg_b_pre, "b_q_norm": dg_q_norm, "b_post_norm": dg_b_post}
    gains = {k: jnp.sum(a, axis=0) for k, a in gains.items()}
    return jnp.sum(loss_acc), grad_x.reshape(bsz, t, D_MODEL), grads, gains


WEIGHT_ORDER = ("a_pre_norm", "a_w_in", "a_w_out", "a_post_norm", "kv_norm", "kv_w_down", "kv_latent_norm",
                "kv_w_up", "b_pre_norm", "b_w_in", "b_q_norm", "b_w_q_up", "b_w_out", "b_post_norm")
MATRICES = (("a_w_in", 1024, 10240, 1), ("a_w_out", 1024, 1024, 0), ("kv_w_down", 1024, 288, 0),
            ("kv_w_up", 256, 2048, 1), ("b_w_in", 1024, 1408, 1), ("b_w_q_up", 384, 1536, 1),
            ("b_w_out", 1024, 1024, 0))
SHARDED_GAINS = ("a_pre_norm", "a_post_norm")
GAIN_WIDTHS = (("a_pre_norm", 1024), ("a_post_norm", 1024), ("kv_norm", 1024), ("kv_latent_norm", 256),
               ("b_pre_norm", 1024), ("b_q_norm", 384), ("b_post_norm", 1024))
GAIN_ROWS = 48


def _shard_rows(rows, cols):
    return rows * cols // (N_DEV * LANES)


def _whole_from_blocks(blocks, rows, cols, axis):
    if axis == 1:
        return blocks.reshape(N_DEV, rows, cols // N_DEV).transpose(1, 0, 2).reshape(rows, cols)
    return blocks.reshape(rows, cols)


def _blocks_from_whole(whole, rows, cols, axis):
    if axis == 1:
        whole = whole.reshape(rows, N_DEV, cols // N_DEV).transpose(1, 0, 2)
    return whole.reshape(N_DEV, -1, LANES)


def kernel(x, positions, a_pre_norm, a_w_in, a_w_out, a_post_norm, kv_norm, kv_w_down, kv_latent_norm, kv_w_up, b_pre_norm, b_w_in, b_q_norm, b_w_q_up, b_w_out, b_post_norm, loss_target, m_a_pre_norm, m_a_w_in, m_a_w_out, m_a_post_norm, m_kv_norm, m_kv_w_down, m_kv_latent_norm, m_kv_w_up, m_b_pre_norm, m_b_w_in, m_b_q_norm, m_b_w_q_up, m_b_w_out, m_b_post_norm, v_a_pre_norm, v_a_w_in, v_a_w_out, v_a_post_norm, v_kv_norm, v_kv_w_down, v_kv_latent_norm, v_kv_w_up, v_b_pre_norm, v_b_w_in, v_b_q_norm, v_b_w_q_up, v_b_w_out, v_b_post_norm):
    weights = dict(a_pre_norm=a_pre_norm, a_w_in=a_w_in, a_w_out=a_w_out, a_post_norm=a_post_norm, kv_norm=kv_norm,
                   kv_w_down=kv_w_down, kv_latent_norm=kv_latent_norm, kv_w_up=kv_w_up, b_pre_norm=b_pre_norm,
                   b_w_in=b_w_in, b_q_norm=b_q_norm, b_w_q_up=b_w_q_up, b_w_out=b_w_out, b_post_norm=b_post_norm)
    m_in = dict(a_pre_norm=m_a_pre_norm, a_w_in=m_a_w_in, a_w_out=m_a_w_out, a_post_norm=m_a_post_norm,
                kv_norm=m_kv_norm, kv_w_down=m_kv_w_down, kv_latent_norm=m_kv_latent_norm, kv_w_up=m_kv_w_up,
                b_pre_norm=m_b_pre_norm, b_w_in=m_b_w_in, b_q_norm=m_b_q_norm, b_w_q_up=m_b_w_q_up,
                b_w_out=m_b_w_out, b_post_norm=m_b_post_norm)
    v_in = dict(a_pre_norm=v_a_pre_norm, a_w_in=v_a_w_in, a_w_out=v_a_w_out, a_post_norm=v_a_post_norm,
                kv_norm=v_kv_norm, kv_w_down=v_kv_w_down, kv_latent_norm=v_kv_latent_norm, kv_w_up=v_kv_w_up,
                b_pre_norm=v_b_pre_norm, b_w_in=v_b_w_in, b_q_norm=v_b_q_norm, b_w_q_up=v_b_w_q_up,
                b_w_out=v_b_w_out, b_post_norm=v_b_post_norm)
    me = 4 * lax.axis_index("x") + 2 * lax.axis_index("y") + lax.axis_index("c")

    flat = jnp.concatenate([weights[name].astype(BF16).reshape(-1, LANES) for name, _, _, _ in MATRICES], axis=0)
    gathered = _all_gather(flat, "gather_weights")
    whole = {}
    off = 0
    for name, rows, cols, axis in MATRICES:
        nr = _shard_rows(rows, cols)
        whole[name] = _whole_from_blocks(gathered[:, off:off + nr], rows, cols, axis)
        off += nr
    gain_shard = jnp.concatenate([weights[name].reshape(1, LANES) for name in SHARDED_GAINS]
                                 + [jnp.zeros((8 - len(SHARDED_GAINS), LANES), F32)], axis=0)
    gain_blocks = _all_gather(gain_shard, "gather_gains")
    for i, name in enumerate(SHARDED_GAINS):
        whole[name] = gain_blocks[:, i, :].reshape(1, D_MODEL)
    for name in ("kv_norm", "kv_latent_norm", "b_pre_norm", "b_q_norm", "b_post_norm"):
        whole[name] = weights[name].reshape(1, -1)

    loss_part, grad_x, grads, gains = _local_step(x, positions, loss_target, whole)

    blocks = jnp.concatenate([_blocks_from_whole(grads[name], rows, cols, axis).astype(BF16)
                              for name, rows, cols, axis in MATRICES], axis=1)
    got = _exchange_sibling(blocks, "scatter_grads_core")
    core = lax.axis_index("c").astype(jnp.int32).reshape(1)
    landed = _exchange_chips(_add_pairs(blocks, got, core, tr=2512, name="add_core_grads"), "scatter_grads_chip")
    summed = _sum_slots(landed, tr=2512, name="sum_grads")
    grad_out = {}
    off = 0
    for name, rows, cols, axis in MATRICES:
        nr = _shard_rows(rows, cols)
        grad_out[name] = summed[off:off + nr].reshape(weights[name].shape)
        off += nr

    vec = jnp.concatenate([gains[name] for name, _ in GAIN_WIDTHS] + [jnp.full((LANES,), loss_part, F32)])
    vec = jnp.pad(vec, (0, GAIN_ROWS * LANES - vec.shape[0])).reshape(GAIN_ROWS, LANES)
    total = _sum_slots(_all_gather(vec, "gather_gain_grads"), tr=GAIN_ROWS, name="sum_gain_grads").reshape(-1)
    off = 0
    for name, width in GAIN_WIDTHS:
        g = total[off:off + width]
        if name in SHARDED_GAINS:
            g = lax.dynamic_slice(g, (me * LANES,), (LANES,))
        grad_out[name] = g.reshape(weights[name].shape)
        off += width
    loss = total[off]

    deltas, new_m, new_v = {}, {}, {}
    for name in WEIGHT_ORDER:
        deltas[name], new_m[name], new_v[name] = _adamw(weights[name], grad_out[name], m_in[name], v_in[name],
                                                        "adamw_" + name)
    return (loss, grad_x, *[grad_out[k] for k in WEIGHT_ORDER], *[deltas[k] for k in WEIGHT_ORDER],
            *[new_m[k] for k in WEIGHT_ORDER], *[new_v[k] for k in WEIGHT_ORDER])
```

```python
import jax
import jax.numpy as jnp
from jax import lax
from jax.experimental import pallas as pl
from jax.experimental.pallas import tpu as pltpu

F32 = jnp.float32
BF16 = jnp.bfloat16

N_DEV = 8
D_MODEL = 1024
NORM_EPS = 1e-6
A_GROUPS = 3
A_DILATIONS = (1, 4, 16)
A_HEADS = 8
A_HEAD_DIM = 128
A_WIDTH = 1024
A_ROT_DIM = 32
A_ROPE_THETA = 500000.0
A_QKV = A_GROUPS * 3 * A_WIDTH
B_HEADS = 16
B_NOPE = 64
B_ROPE = 32
B_QK_DIM = 96
B_VDIM = 64
B_Q_LORA = 384
B_KV_LORA = 256
B_ROPE_THETA = 10000.0
B_KPAD = B_HEADS * 128
ADAM_LR = 0.001
ADAM_B1 = 0.9
ADAM_B2 = 0.999
ADAM_EPS = 1e-08
ADAM_WD = 0.01
ADAM_STEP = 10

LANES = 128
BAND = 128
EPILOGUE_ROWS = 128
NEG = -1e30
VMEM_LIMIT = 56 * 1024 * 1024
MESH = pl.DeviceIdType.MESH


def _cparams(sem):
    return pltpu.CompilerParams(dimension_semantics=sem, vmem_limit_bytes=VMEM_LIMIT)


def _rowwise(fn, rows, bcast, outs, accs=(), *, tm, name):
    n = rows[0].shape[0]
    nr, nb, no = len(rows), len(bcast), len(outs)

    def body(*refs):
        res = fn(*[r[...] for r in refs[:nr + nb]])
        out_refs = refs[nr + nb:nr + nb + no]
        acc_refs = refs[nr + nb + no:]
        for r, v in zip(out_refs, res[:no]):
            r[...] = v.astype(r.dtype)
        if acc_refs:
            @pl.when(pl.program_id(0) == 0)
            def _():
                for r in acc_refs:
                    r[...] = jnp.zeros_like(r)
            for r, v in zip(acc_refs, res[no:]):
                r[...] += v.reshape(tm // 8, 8, v.shape[-1]).sum(axis=0)

    in_specs = [pl.BlockSpec((tm, a.shape[1]), lambda i: (i, 0)) for a in rows]
    in_specs += [pl.BlockSpec(a.shape, lambda i: (0, 0)) for a in bcast]
    out_specs = [pl.BlockSpec((tm, c), lambda i: (i, 0)) for c, _ in outs]
    out_specs += [pl.BlockSpec((8, c), lambda i: (0, 0)) for c in accs]
    out_shape = [jax.ShapeDtypeStruct((n, c), dt) for c, dt in outs]
    out_shape += [jax.ShapeDtypeStruct((8, c), F32) for c in accs]
    return pl.pallas_call(
        body, name=name, grid=(n // tm,), in_specs=in_specs, out_specs=out_specs, out_shape=out_shape,
        compiler_params=_cparams(("arbitrary",)))(*rows, *bcast)


def _matmul(a, b, *, out_dtype, tm, tn, tk=None, name, epilogue=None, extras=(), ta=False, tb=False,
            epilogue_rows=None):
    epilogue_rows = epilogue_rows or tm
    k, m = a.shape[::-1] if not ta else a.shape
    n = b.shape[0] if tb else b.shape[1]
    tk = tk or k
    nk = k // tk
    ne = len(extras)
    dot = _dot_tn if ta else (_dot_nt if tb else _dot)
    assert epilogue is None or (nk == 1 and not ta)

    def body(*refs):
        a_ref, b_ref = refs[:2]
        ex = refs[2:2 + ne]
        o_ref = refs[2 + ne]
        if epilogue is not None:
            b_tile = b_ref[...].astype(BF16)
            for r0 in range(0, tm, epilogue_rows):
                rows = slice(r0, r0 + epilogue_rows)
                epilogue(dot(a_ref[rows, :].astype(BF16), b_tile), o_ref, rows, *ex)
            return
        part = dot(a_ref[...].astype(BF16), b_ref[...].astype(BF16))
        if nk == 1:
            o_ref[...] = part.astype(o_ref.dtype)
        else:
            acc_ref = refs[-1]
            kk = pl.program_id(2)

            @pl.when(kk == 0)
            def _():
                acc_ref[...] = part

            @pl.when(kk > 0)
            def _():
                acc_ref[...] += part

            @pl.when(kk == nk - 1)
            def _():
                o_ref[...] = acc_ref[...].astype(o_ref.dtype)

    a_spec = pl.BlockSpec((tk, tm), lambda j, i, kk: (kk, i)) if ta else pl.BlockSpec((tm, tk), lambda j, i, kk: (i, kk))
    b_spec = pl.BlockSpec((tn, tk), lambda j, i, kk: (j, kk)) if tb else pl.BlockSpec((tk, tn), lambda j, i, kk: (kk, j))
    in_specs = [a_spec, b_spec] + [pl.BlockSpec(bs, im) for _, bs, im in extras]
    return pl.pallas_call(
        body, name=name, grid=(n // tn, m // tm, nk), in_specs=in_specs,
        out_specs=pl.BlockSpec((tm, tn), lambda j, i, kk: (i, j)),
        out_shape=jax.ShapeDtypeStruct((m, n), out_dtype),
        scratch_shapes=[pltpu.VMEM((tm, tn), F32)] if nk > 1 else [],
        compiler_params=_cparams(("parallel", "parallel", "arbitrary")))(a, b, *[e[0] for e in extras])


def _add_epilogue(acc, o_ref, rows, prev_ref):
    o_ref[rows, :] = (acc + prev_ref[rows, :]).astype(o_ref.dtype)


def _rope(x, c, sp, sm):
    return x * c + pltpu.roll(x, 16, 1) * sp + pltpu.roll(x, LANES - 16, 1) * sm


def _rope_t(dy, c, sp, sm):
    return dy * c + pltpu.roll(dy * sp, LANES - 16, 1) + pltpu.roll(dy * sm, 16, 1)


def _rope_tables(positions, theta, rot_dim, lane0, name):
    n = positions.shape[0]
    half = rot_dim // 2
    inv_freq = 1.0 / (theta ** (jnp.arange(half, dtype=F32) * (2.0 / rot_dim)))
    freq = jnp.concatenate([jnp.zeros((lane0,), F32), inv_freq, inv_freq,
                            jnp.zeros((LANES - lane0 - rot_dim,), F32)]).reshape(1, LANES)

    def fn(p, f):
        ang = p * f
        cos, sin = jnp.cos(ang), jnp.sin(ang)
        lane = lax.broadcasted_iota(jnp.int32, ang.shape, 1) - lane0
        first, second = (lane >= 0) & (lane < half), (lane >= half) & (lane < rot_dim)
        return jnp.where(first | second, cos, 1.0), jnp.where(second, sin, 0.0), jnp.where(first, -sin, 0.0)

    return _rowwise(fn, [positions.astype(F32).reshape(n, 1)], [freq], [(LANES, F32)] * 3, tm=512, name=name)


def _value_heads_t(c_kv, w_vt, bsz, t):
    n, k = c_kv.shape
    tm, tn = 512, 1024

    def body(a_ref, b_ref, o_ref):
        acc = _dot(a_ref[...], b_ref[...])
        lane = lax.broadcasted_iota(jnp.int32, (1, tn), 1)
        acc = acc + jnp.where(lax.rem(lane, LANES) == B_VDIM, 1.0, 0.0)
        o_ref[...] = acc.T.astype(BF16)

    per_seq = t // tm
    return pl.pallas_call(
        body, name="kv_up_v_t", grid=(w_vt.shape[1] // tn, n // tm),
        in_specs=[pl.BlockSpec((tm, k), lambda j, i: (i, 0)), pl.BlockSpec((k, tn), lambda j, i: (0, j))],
        out_specs=pl.BlockSpec((None, tn, tm), lambda j, i: (i // per_seq, j, lax.rem(i, per_seq))),
        out_shape=jax.ShapeDtypeStruct((bsz, w_vt.shape[1], t), BF16),
        compiler_params=_cparams(("parallel", "parallel")))(c_kv, w_vt)


def _rms(x, g):
    xf = x.astype(F32)
    return xf * lax.rsqrt(jnp.mean(xf * xf, axis=-1, keepdims=True) + NORM_EPS) * g


def _rms_bwd(x, g, dy):
    xf = x.astype(F32)
    rstd = lax.rsqrt(jnp.mean(xf * xf, axis=-1, keepdims=True) + NORM_EPS)
    xhat = xf * rstd
    dxhat = dy * g
    dx = rstd * (dxhat - xhat * jnp.mean(dxhat * xhat, axis=-1, keepdims=True))
    return dx, dy * xhat


def _silu(z):
    return z * jax.nn.sigmoid(z)


def _silu_grad(z):
    s = jax.nn.sigmoid(z)
    return s * (1.0 + z * (1.0 - s))


def _dot_nt(a, b):
    return lax.dot_general(a, b, (((1,), (1,)), ((), ())), preferred_element_type=F32)


def _dot_tn(a, b):
    return lax.dot_general(a, b, (((0,), (0,)), ((), ())), preferred_element_type=F32)


def _dot(a, b):
    return jnp.dot(a, b, preferred_element_type=F32)


A_SCALE = A_HEAD_DIM ** -0.5
LOG2E = 1.4426950408889634
A_C2 = A_SCALE * LOG2E
A_HEAD_GROUP = 4


def _attn_a_fwd(qk, v, name):
    s_n, ln, _ = qk.shape
    nb = ln // BAND
    blk = (None, BAND, A_WIDTH)

    def body(q_ref, kc_ref, kp_ref, vc_ref, vp_ref, o_ref, lse_ref):
        kpos = lax.broadcasted_iota(jnp.int32, (2 * BAND, BAND), 0)
        qpos = lax.broadcasted_iota(jnp.int32, (2 * BAND, BAND), 1) + BAND
        first_key = jnp.where(pl.program_id(1) > 0, 0, BAND)
        mask = (kpos <= qpos) & (kpos >= qpos - BAND) & (kpos >= first_key)
        rows = []
        for h0 in range(0, A_HEADS, A_HEAD_GROUP):
            hss = [slice(h * A_HEAD_DIM, (h + 1) * A_HEAD_DIM) for h in range(h0, h0 + A_HEAD_GROUP)]
            sts = [_dot_nt(jnp.concatenate([kp_ref[:, hs], kc_ref[:, hs]], axis=0), q_ref[:, hs]) for hs in hss]
            ps, ls = [], []
            for st in sts:
                st = jnp.where(mask, st * A_C2, NEG)
                m = jnp.max(st, axis=0, keepdims=True)
                p = jnp.exp2(st - m)
                l_row = jnp.sum(p, axis=0, keepdims=True)
                ps.append(p.astype(BF16))
                ls.append(l_row)
                rows.append(m + jnp.log2(l_row))
            ots = [_dot_tn(jnp.concatenate([vp_ref[:, hs], vc_ref[:, hs]], axis=0), p) for hs, p in zip(hss, ps)]
            for hs, o_t, l_row in zip(hss, ots, ls):
                o_ref[:, hs] = (o_t / l_row).T.astype(BF16)
        lse_ref[...] = jnp.concatenate(rows, axis=0)

    def col(c, off):
        return lambda s, l: (s, jnp.maximum(l + off, 0), c)

    return pl.pallas_call(
        body, name=name, grid=(s_n, nb),
        in_specs=[pl.BlockSpec(blk, col(0, 0)), pl.BlockSpec(blk, col(1, 0)), pl.BlockSpec(blk, col(1, -1)),
                  pl.BlockSpec(blk, col(0, 0)), pl.BlockSpec(blk, col(0, -1))],
        out_specs=[pl.BlockSpec(blk, lambda s, l: (s, l, 0)),
                   pl.BlockSpec((None, A_HEADS, BAND), lambda s, l: (s, 0, l))],
        out_shape=[jax.ShapeDtypeStruct((s_n, ln, A_WIDTH), BF16), jax.ShapeDtypeStruct((s_n, A_HEADS, ln), F32)],
        compiler_params=_cparams(("parallel", "arbitrary")))(qk, qk, qk, v, v)


def _attn_a_bwd(qk, v, do, lse2, dsum, tabs, name):
    s_n, ln, _ = qk.shape
    nb = ln // BAND
    blk = (None, BAND, A_WIDTH)

    def body(q_ref, qn_ref, kc_ref, kp_ref, vc_ref, vp_ref, do_ref, don_ref, lse_ref, lsen_ref, ds_ref, dsn_ref,
             c_ref, sp_ref, sm_ref, out_ref):
        l_idx = pl.program_id(1)
        kpos = lax.broadcasted_iota(jnp.int32, (2 * BAND, BAND), 0)
        qpos = lax.broadcasted_iota(jnp.int32, (2 * BAND, BAND), 1) + BAND
        first_key = jnp.where(l_idx > 0, 0, BAND)
        mask_q = (kpos <= qpos) & (kpos >= qpos - BAND) & (kpos >= first_key)
        kpos2 = lax.broadcasted_iota(jnp.int32, (BAND, 2 * BAND), 0)
        qpos2 = lax.broadcasted_iota(jnp.int32, (BAND, 2 * BAND), 1)
        last_query = jnp.where(l_idx < nb - 1, 2 * BAND, BAND)
        mask_k = (kpos2 <= qpos2) & (kpos2 >= qpos2 - BAND) & (qpos2 < last_query)
        c, sp, sm = c_ref[...], sp_ref[...], sm_ref[...]
        lse_q, ds_q = lse_ref[...], ds_ref[...]
        lse_k = jnp.concatenate([lse_q, lsen_ref[...]], axis=1)
        ds_k = jnp.concatenate([ds_q, dsn_ref[...]], axis=1)
        for h0 in range(0, A_HEADS, A_HEAD_GROUP):
            hl = list(range(h0, h0 + A_HEAD_GROUP))
            hss = [slice(h * A_HEAD_DIM, (h + 1) * A_HEAD_DIM) for h in hl]
            k2s = [jnp.concatenate([kp_ref[:, hs], kc_ref[:, hs]], axis=0) for hs in hss]
            v2s = [jnp.concatenate([vp_ref[:, hs], vc_ref[:, hs]], axis=0) for hs in hss]
            q2s = [jnp.concatenate([q_ref[:, hs], qn_ref[:, hs]], axis=0) for hs in hss]
            do2s = [jnp.concatenate([do_ref[:, hs], don_ref[:, hs]], axis=0) for hs in hss]
            sts = [_dot_nt(k2, q_ref[:, hs]) for k2, hs in zip(k2s, hss)]
            dpts = [_dot_nt(v2, do_ref[:, hs]) for v2, hs in zip(v2s, hss)]
            st2s = [_dot_nt(kc_ref[:, hs], q2) for q2, hs in zip(q2s, hss)]
            dpt2s = [_dot_nt(vc_ref[:, hs], do2) for do2, hs in zip(do2s, hss)]
            dsts, dst2s, p2s = [], [], []
            for i, h in enumerate(hl):
                p = jnp.exp2(jnp.where(mask_q, sts[i] * A_C2, NEG) - lse_q[h:h + 1])
                dsts.append((p * (dpts[i] - ds_q[h:h + 1]) * A_SCALE).astype(BF16))
                p2 = jnp.exp2(jnp.where(mask_k, st2s[i] * A_C2, NEG) - lse_k[h:h + 1])
                dst2s.append((p2 * (dpt2s[i] - ds_k[h:h + 1]) * A_SCALE).astype(BF16))
                p2s.append(p2.astype(BF16))
            dqs = [_dot_tn(dsts[i], k2s[i]) for i in range(A_HEAD_GROUP)]
            dks = [_dot(dst2s[i], q2s[i]) for i in range(A_HEAD_GROUP)]
            dvs = [_dot(p2s[i], do2s[i]) for i in range(A_HEAD_GROUP)]
            for i, h in enumerate(hl):
                out_ref[:, hss[i]] = _rope_t(dqs[i], c, sp, sm).astype(BF16)
                out_ref[:, A_WIDTH + h * A_HEAD_DIM:A_WIDTH + (h + 1) * A_HEAD_DIM] = _rope_t(dks[i], c, sp, sm).astype(BF16)
                out_ref[:, 2 * A_WIDTH + h * A_HEAD_DIM:2 * A_WIDTH + (h + 1) * A_HEAD_DIM] = dvs[i].astype(BF16)

    def col(c, off):
        return lambda s, l: (s, jnp.clip(l + off, 0, nb - 1), c)

    def row(off):
        return pl.BlockSpec((None, A_HEADS, BAND), lambda s, l: (s, 0, jnp.clip(l + off, 0, nb - 1)))

    tspec = pl.BlockSpec((None, BAND, LANES), lambda s, l: (s, l, 0))
    in_specs = [pl.BlockSpec(blk, col(0, 0)), pl.BlockSpec(blk, col(0, 1)),
                pl.BlockSpec(blk, col(1, 0)), pl.BlockSpec(blk, col(1, -1)),
                pl.BlockSpec(blk, col(0, 0)), pl.BlockSpec(blk, col(0, -1)),
                pl.BlockSpec(blk, col(0, 0)), pl.BlockSpec(blk, col(0, 1)),
                row(0), row(1), row(0), row(1), tspec, tspec, tspec]
    return pl.pallas_call(
        body, name=name, grid=(s_n, nb), in_specs=in_specs,
        out_specs=pl.BlockSpec((None, BAND, 3 * A_WIDTH), lambda s, l: (s, l, 0)),
        out_shape=jax.ShapeDtypeStruct((s_n, ln, 3 * A_WIDTH), BF16),
        compiler_params=_cparams(("parallel", "arbitrary")))(
            qk, qk, qk, qk, v, v, do, do, lse2, lse2, dsum, dsum, *tabs)


B_TQ = 256
B_SCALE = B_QK_DIM ** -0.5
B_C2 = B_SCALE * LOG2E


def _b_row_tile(tq):
    return pl.BlockSpec((None, None, 8, tq), lambda b, j, i: (b, j, 0, i))


def _b_specs(t, tq, tk):
    pair_tile = pl.BlockSpec((None, tq, 2 * LANES), lambda b, j, i: (b, i, j))
    pair_full = pl.BlockSpec((None, t, 2 * LANES), lambda b, j, i: (b, 0, j))
    one_tile = pl.BlockSpec((None, tk, LANES), lambda b, j, i: (b, i, j))
    one_full = pl.BlockSpec((None, t, LANES), lambda b, j, i: (b, 0, j))
    tab_tile = pl.BlockSpec((None, tq, LANES), lambda b, j, i: (b, i, 0))
    return pair_tile, pair_full, one_tile, one_full, tab_tile


def _key_le_query(kb, qb, tk, tq):
    kpos = kb * tk + lax.broadcasted_iota(jnp.int32, (tk, tq), 0)
    qpos = qb * tq + lax.broadcasted_iota(jnp.int32, (tk, tq), 1)
    return kpos <= qpos


def _attn_b_fwd(q, kpad, vt1):
    bsz, t, _ = q.shape
    tq = tk = B_TQ
    nq = t // tq

    def body(q_ref, k_ref, vt_ref, o_ref, lse_ref):
        qblk = pl.program_id(2)
        qs = [q_ref[:, hh * LANES:(hh + 1) * LANES] for hh in range(2)]

        def scores(kb):
            start = pl.multiple_of(kb * tk, tk)
            return [_dot_nt(k_ref[pl.ds(start, tk), hh * LANES:(hh + 1) * LANES], qs[hh]) for hh in range(2)]

        def pv(kb, ps):
            start = pl.multiple_of(kb * tk, tk)
            return [_dot(vt_ref[hh * LANES:(hh + 1) * LANES, pl.ds(start, tk)], ps[hh]) for hh in range(2)]

        def softmax(ss, ms, accs, kb, masked):
            out_m, out_acc, out_p = [], [], []
            for hh in range(2):
                s = ss[hh] * B_C2
                if masked:
                    s = jnp.where(_key_le_query(kb, qblk, tk, tq), s, NEG)
                m_new = jnp.maximum(ms[hh], jnp.max(s, axis=0, keepdims=True))
                out_acc.append(jnp.exp2(ms[hh] - m_new) * accs[hh])
                out_p.append(jnp.exp2(s - m_new).astype(BF16))
                out_m.append(m_new)
            return out_m, out_acc, out_p

        def step(kb, carry):
            ss, ps, ms, accs = carry
            pvs = pv(jnp.maximum(kb - 1, 0), ps)
            ss_next = scores(kb + 1)
            accs = [accs[hh] + pvs[hh] for hh in range(2)]
            ms, accs, ps = softmax(ss, ms, accs, kb, False)
            return (ss_next, ps, ms, accs)

        init = (scores(0), [jnp.zeros((tk, tq), BF16)] * 2, [jnp.full((1, tq), NEG, F32)] * 2,
                [jnp.zeros((LANES, tq), F32)] * 2)
        ss, ps, ms, accs = lax.fori_loop(0, qblk, step, init)
        pvs = pv(jnp.maximum(qblk - 1, 0), ps)
        accs = [accs[hh] + pvs[hh] for hh in range(2)]
        ms, accs, ps = softmax(ss, ms, accs, qblk, True)
        pvs = pv(qblk, ps)
        accs = [accs[hh] + pvs[hh] for hh in range(2)]
        ls = [accs[hh][B_VDIM:B_VDIM + 1] for hh in range(2)]
        o_t = jnp.concatenate([accs[0][:B_VDIM] / ls[0], accs[1][:B_VDIM] / ls[1]], axis=0)
        o_ref[...] = o_t.T
        lse_ref[...] = jnp.concatenate([ms[0] + jnp.log2(ls[0]), ms[1] + jnp.log2(ls[1]),
                                        jnp.zeros((6, tq), F32)], axis=0)

    pair_tile, pair_full, one_tile, _, _ = _b_specs(t, tq, tk)
    vt_spec = pl.BlockSpec((None, 2 * LANES, t), lambda b, j, i: (b, j, 0))
    return pl.pallas_call(
        body, name="attn_b_fwd", grid=(bsz, B_HEADS // 2, nq),
        in_specs=[pair_tile, pair_full, vt_spec],
        out_specs=[one_tile, _b_row_tile(tq)],
        out_shape=[jax.ShapeDtypeStruct((bsz, t, B_HEADS * B_VDIM), F32),
                   jax.ShapeDtypeStruct((bsz, B_HEADS // 2, 8, t), F32)],
        compiler_params=_cparams(("parallel", "parallel", "arbitrary")))(q, kpad, vt1)


def _attn_b_bwd(q, kpad, v, do, o, lse2, tabs):
    bsz, t, _ = q.shape
    tq = tk = B_TQ
    nq = t // tq

    def body(q_ref, k_ref, v_ref, do_ref, o_ref, lse_ref, c_ref, sp_ref, sm_ref, dq_ref, dk_ref, dv_ref,
             dqt_scr, dsum_scr):
        lane = lax.broadcasted_iota(jnp.int32, (tk, LANES), 1)
        sel_lane = lax.broadcasted_iota(jnp.int32, (8, LANES), 1)
        sel_row = lax.broadcasted_iota(jnp.int32, (8, LANES), 0)
        sel = jnp.where((sel_lane < B_VDIM) == (sel_row == 0), 1.0, 0.0)
        sel = jnp.where(sel_row < 2, sel, 0.0).astype(BF16)

        def rows(blk):
            return pl.ds(pl.multiple_of(blk * tq, tq), tq)

        def dsum_step(qb, carry):
            prod = do_ref[rows(qb), :].astype(F32) * o_ref[rows(qb), :]
            hi = prod.astype(BF16)
            lo = (prod - hi.astype(F32)).astype(BF16)
            dsum_scr[:, rows(qb)] = _dot_nt(sel, hi) + _dot_nt(sel, lo)
            return carry

        lax.fori_loop(0, nq, dsum_step, 0)
        dqt_scr[...] = jnp.zeros_like(dqt_scr)

        def kv_step(kb, carry):
            ks = [k_ref[rows(kb), hh * LANES:(hh + 1) * LANES] for hh in range(2)]
            vb = v_ref[rows(kb), :]
            zero = jnp.zeros_like(vb)
            vs = [jnp.where(lane < B_VDIM, vb, zero), jnp.where(lane < B_VDIM, zero, vb)]

            def make_step(masked):
                def step(qb, acc):
                    qs = [q_ref[rows(qb), hh * LANES:(hh + 1) * LANES] for hh in range(2)]
                    do_b = do_ref[rows(qb), :]
                    ss = [_dot_nt(ks[hh], qs[hh]) for hh in range(2)]
                    dps = [_dot_nt(vs[hh], do_b) for hh in range(2)]
                    pbs, dss = [], []
                    for hh in range(2):
                        s = ss[hh] * B_C2
                        if masked:
                            s = jnp.where(_key_le_query(kb, qb, tk, tq), s, NEG)
                        p = jnp.exp2(s - lse_ref[hh:hh + 1, rows(qb)])
                        dss.append((p * (dps[hh] - dsum_scr[hh:hh + 1, rows(qb)]) * B_SCALE).astype(BF16))
                        pbs.append(p.astype(BF16))
                    for hh in range(2):
                        dqt_scr[hh, :, rows(qb)] += _dot_tn(ks[hh], dss[hh])
                    return (acc[0] + _dot(dss[0], qs[0]), acc[1] + _dot(dss[1], qs[1]),
                            acc[2] + _dot(pbs[0], do_b), acc[3] + _dot(pbs[1], do_b))
                return step

            acc = make_step(True)(kb, (jnp.zeros((tk, LANES), F32),) * 4)
            acc = lax.fori_loop(kb + 1, nq, make_step(False), acc)
            dk_ref[rows(kb), :LANES] = acc[0].astype(BF16)
            dk_ref[rows(kb), LANES:] = acc[1].astype(BF16)
            dv_ref[rows(kb), :] = jnp.where(lane < B_VDIM, acc[2], acc[3]).astype(BF16)
            return carry

        lax.fori_loop(0, nq, kv_step, 0)

        def dq_step(qb, carry):
            c, sp, sm = c_ref[rows(qb), :], sp_ref[rows(qb), :], sm_ref[rows(qb), :]
            for hh in range(2):
                dq_ref[rows(qb), hh * LANES:(hh + 1) * LANES] = _rope_t(dqt_scr[hh, :, rows(qb)].T, c, sp, sm).astype(BF16)
            return carry

        lax.fori_loop(0, nq, dq_step, 0)

    pair_full = pl.BlockSpec((None, t, 2 * LANES), lambda b, j: (b, 0, j))
    one_full = pl.BlockSpec((None, t, LANES), lambda b, j: (b, 0, j))
    row_full = pl.BlockSpec((None, None, 8, t), lambda b, j: (b, j, 0, 0))
    tab_full = pl.BlockSpec((None, t, LANES), lambda b, j: (b, 0, 0))
    return pl.pallas_call(
        body, name="attn_b_bwd", grid=(bsz, B_HEADS // 2),
        in_specs=[pair_full, pair_full, one_full, one_full, one_full, row_full, tab_full, tab_full, tab_full],
        out_specs=[pair_full, pair_full, one_full],
        out_shape=[jax.ShapeDtypeStruct((bsz, t, B_KPAD), BF16), jax.ShapeDtypeStruct((bsz, t, B_KPAD), BF16),
                   jax.ShapeDtypeStruct((bsz, t, B_HEADS * B_VDIM), BF16)],
        scratch_shapes=[pltpu.VMEM((2, LANES, t), F32), pltpu.VMEM((8, t), F32)],
        compiler_params=_cparams(("parallel", "parallel")))(q, kpad, v, do, o, lse2, *tabs)


def _attn_b_dq(q, kpad, v, do, o, lse2, tabs):
    bsz, t, _ = q.shape
    tq = tk = B_TQ
    nq = t // tq

    def body(q_ref, k_ref, v_ref, do_ref, o_ref, lse_ref, c_ref, sp_ref, sm_ref, dq_ref, dsum_ref):
        qblk = pl.program_id(2)
        lane = lax.broadcasted_iota(jnp.int32, (tq, LANES), 1)
        do_b = do_ref[...]
        prod = do_b.astype(F32) * o_ref[...]
        hi = prod.astype(BF16)
        lo = (prod - hi.astype(F32)).astype(BF16)
        sel_lane = lax.broadcasted_iota(jnp.int32, (8, LANES), 1)
        sel_row = lax.broadcasted_iota(jnp.int32, (8, LANES), 0)
        sel = jnp.where((sel_lane < B_VDIM) == (sel_row == 0), 1.0, 0.0)
        sel = jnp.where(sel_row < 2, sel, 0.0).astype(BF16)
        dsum = _dot_nt(sel, hi) + _dot_nt(sel, lo)
        dsum_ref[...] = dsum
        lse = lse_ref[...]
        qs = [q_ref[:, hh * LANES:(hh + 1) * LANES] for hh in range(2)]
        zero = jnp.zeros_like(do_b)
        dos = [jnp.where(lane < B_VDIM, do_b, zero), jnp.where(lane < B_VDIM, zero, do_b)]

        def kblock(kb, hh):
            start = pl.multiple_of(kb * tk, tk)
            return k_ref[pl.ds(start, tk), hh * LANES:(hh + 1) * LANES]

        def dots(kb):
            start = pl.multiple_of(kb * tk, tk)
            vb = v_ref[pl.ds(start, tk), :]
            return ([_dot_nt(kblock(kb, hh), qs[hh]) for hh in range(2)], [_dot_nt(vb, dos[hh]) for hh in range(2)])

        def ds_of(ss, dps, kb, masked):
            out = []
            for hh in range(2):
                s = ss[hh] * B_C2
                if masked:
                    s = jnp.where(_key_le_query(kb, qblk, tk, tq), s, NEG)
                p = jnp.exp2(s - lse[hh:hh + 1])
                out.append((p * (dps[hh] - dsum[hh:hh + 1]) * B_SCALE).astype(BF16))
            return out

        def accum(acc, kb, dss):
            return [acc[hh] + _dot_tn(kblock(kb, hh), dss[hh]) for hh in range(2)]

        def step(kb, carry):
            dss, acc = carry
            ss, dps = dots(kb)
            acc = accum(acc, jnp.maximum(kb - 1, 0), dss)
            return (ds_of(ss, dps, kb, False), acc)

        init = ([jnp.zeros((tk, tq), BF16)] * 2, [jnp.zeros((LANES, tq), F32)] * 2)
        dss, acc = lax.fori_loop(0, qblk, step, init)
        ss, dps = dots(qblk)
        acc = accum(acc, jnp.maximum(qblk - 1, 0), dss)
        acc = accum(acc, qblk, ds_of(ss, dps, qblk, True))
        c, sp, sm = c_ref[...], sp_ref[...], sm_ref[...]
        for hh in range(2):
            dq_ref[:, hh * LANES:(hh + 1) * LANES] = _rope_t(acc[hh].T, c, sp, sm).astype(BF16)

    pair_tile, pair_full, one_tile, one_full, tab_tile = _b_specs(t, tq, tk)
    row_tile = _b_row_tile(tq)
    return pl.pallas_call(
        body, name="attn_b_dq", grid=(bsz, B_HEADS // 2, nq),
        in_specs=[pair_tile, pair_full, one_full, one_tile, one_tile, row_tile, tab_tile, tab_tile, tab_tile],
        out_specs=[pair_tile, row_tile],
        out_shape=[jax.ShapeDtypeStruct((bsz, t, B_KPAD), BF16),
                   jax.ShapeDtypeStruct((bsz, B_HEADS // 2, 8, t), F32)],
        compiler_params=_cparams(("parallel", "parallel", "arbitrary")))(q, kpad, v, do, o, lse2, *tabs)


def _attn_b_dkv(q, kpad, v, do, lse2, dsum):
    bsz, t, _ = q.shape
    tq = tk = B_TQ
    nq = t // tq

    def body(q_ref, k_ref, v_ref, do_ref, lse_ref, dsum_ref, dk_ref, dv_ref):
        kblk = pl.program_id(2)
        lane = lax.broadcasted_iota(jnp.int32, (tk, LANES), 1)
        ks = [k_ref[:, hh * LANES:(hh + 1) * LANES] for hh in range(2)]
        vb = v_ref[...]
        zero = jnp.zeros_like(vb)
        vs = [jnp.where(lane < B_VDIM, vb, zero), jnp.where(lane < B_VDIM, zero, vb)]

        def make_step(masked):
            def step(qb, carry):
                start = pl.multiple_of(qb * tq, tq)
                qs = [q_ref[pl.ds(start, tq), hh * LANES:(hh + 1) * LANES] for hh in range(2)]
                do_b = do_ref[pl.ds(start, tq), :]
                ss = [_dot_nt(ks[hh], qs[hh]) for hh in range(2)]
                dps = [_dot_nt(vs[hh], do_b) for hh in range(2)]
                pbs, dss = [], []
                for hh in range(2):
                    s = ss[hh] * B_C2
                    if masked:
                        s = jnp.where(_key_le_query(kblk, qb, tk, tq), s, NEG)
                    p = jnp.exp2(s - lse_ref[hh:hh + 1, pl.ds(start, tq)])
                    dss.append((p * (dps[hh] - dsum_ref[hh:hh + 1, pl.ds(start, tq)]) * B_SCALE).astype(BF16))
                    pbs.append(p.astype(BF16))
                return (carry[0] + _dot(dss[0], qs[0]), carry[1] + _dot(dss[1], qs[1]),
                        carry[2] + _dot(pbs[0], do_b), carry[3] + _dot(pbs[1], do_b))
            return step

        init = (jnp.zeros((tk, LANES), F32),) * 4
        carry = make_step(True)(kblk, init)
        carry = lax.fori_loop(kblk + 1, nq, make_step(False), carry)
        dk_ref[:, :LANES] = carry[0].astype(BF16)
        dk_ref[:, LANES:] = carry[1].astype(BF16)
        dv_ref[...] = jnp.where(lane < B_VDIM, carry[2], carry[3]).astype(BF16)

    pair_tile, pair_full, one_tile, one_full, _ = _b_specs(t, tq, tk)
    row_full = pl.BlockSpec((None, None, 8, t), lambda b, j, i: (b, j, 0, 0))
    return pl.pallas_call(
        body, name="attn_b_dkv", grid=(bsz, B_HEADS // 2, nq),
        in_specs=[pair_full, pair_tile, one_tile, one_full, row_full, row_full],
        out_specs=[pair_tile, one_tile],
        out_shape=[jax.ShapeDtypeStruct((bsz, t, B_KPAD), BF16),
                   jax.ShapeDtypeStruct((bsz, t, B_HEADS * B_VDIM), BF16)],
        compiler_params=_cparams(("parallel", "parallel", "arbitrary")))(q, kpad, v, do, lse2, dsum)


ANY = pl.BlockSpec(memory_space=pl.ANY)


def _all_gather(shard, name):
    def body(x_ref, out_ref, send_sems, recv_sems, local_sem):
        x, y, c = lax.axis_index("x"), lax.axis_index("y"), lax.axis_index("c")
        me, sibling = (x, y, c), (x, y, 1 - c)
        chips = [(1 - x, y), (x, 1 - y), (1 - x, 1 - y)]

        def rows(px, py, pc):
            return out_ref.at[4 * px + 2 * py + pc]

        def copy(k, block, to, src=None):
            return pltpu.make_async_remote_copy(
                src_ref=rows(*block) if src is None else src, dst_ref=rows(*block),
                send_sem=send_sems.at[k], recv_sem=recv_sems.at[k], device_id=to, device_id_type=MESH)

        mine = pltpu.make_async_copy(x_ref, rows(*me), local_sem)
        mine.start()
        first = [copy(0, me, sibling, src=x_ref)]
        first += [copy(1 + j, me, (*chip, c), src=x_ref) for j, chip in enumerate(chips)]
        for cp in first:
            cp.start()
        passed = [copy(4 + j, (*chip, c), sibling) for j, chip in enumerate(chips)]
        for j, chip in enumerate(chips):
            copy(1 + j, (*chip, c), me).wait_recv()
            passed[j].start()
        copy(0, sibling, me).wait_recv()
        for j, chip in enumerate(chips):
            copy(4 + j, (*chip, 1 - c), me).wait_recv()
        for cp in first + passed:
            cp.wait_send()
        mine.wait()

    return pl.pallas_call(
        body, name=name, in_specs=[ANY], out_specs=ANY,
        out_shape=jax.ShapeDtypeStruct((N_DEV,) + shard.shape, shard.dtype),
        scratch_shapes=[pltpu.SemaphoreType.DMA((7,)), pltpu.SemaphoreType.DMA((7,)), pltpu.SemaphoreType.DMA])(shard)


N_CHIPS = 4


def _exchange_sibling(blocks, name):
    def body(g_ref, got_ref, send_sems, recv_sems):
        x, y, c = lax.axis_index("x"), lax.axis_index("y"), lax.axis_index("c")
        sends = [pltpu.make_async_remote_copy(
            src_ref=g_ref.at[2 * q + (1 - c)], dst_ref=got_ref.at[q], send_sem=send_sems.at[q],
            recv_sem=recv_sems.at[q], device_id=(x, y, 1 - c), device_id_type=MESH) for q in range(N_CHIPS)]
        for cp in sends:
            cp.start()
        for cp in sends:
            cp.wait_recv()
        for cp in sends:
            cp.wait_send()

    return pl.pallas_call(
        body, name=name, in_specs=[ANY], out_specs=ANY,
        out_shape=jax.ShapeDtypeStruct((N_CHIPS,) + blocks.shape[1:], blocks.dtype),
        scratch_shapes=[pltpu.SemaphoreType.DMA((N_CHIPS,)), pltpu.SemaphoreType.DMA((N_CHIPS,))])(blocks)


def _exchange_chips(parts, name):
    def body(p_ref, out_ref, send_sems, recv_sems, local_sem):
        x, y, c = lax.axis_index("x"), lax.axis_index("y"), lax.axis_index("c")
        me = 2 * x + y

        def peer(k):
            return (1 - x if k & 2 else x, 1 - y if k & 1 else y)

        def copy(k):
            px, py = peer(k)
            return pltpu.make_async_remote_copy(
                src_ref=p_ref.at[2 * px + py], dst_ref=out_ref.at[me], send_sem=send_sems.at[k - 1],
                recv_sem=recv_sems.at[k - 1], device_id=(px, py, c), device_id_type=MESH)

        def arrival(k):
            px, py = peer(k)
            slot = out_ref.at[2 * px + py]
            return pltpu.make_async_remote_copy(
                src_ref=slot, dst_ref=slot, send_sem=send_sems.at[k - 1], recv_sem=recv_sems.at[k - 1],
                device_id=(px, py, c), device_id_type=MESH)

        mine = pltpu.make_async_copy(p_ref.at[me], out_ref.at[me], local_sem)
        mine.start()
        sends = [copy(k) for k in range(1, N_CHIPS)]
        for cp in sends:
            cp.start()
        for k in range(1, N_CHIPS):
            arrival(k).wait_recv()
        for cp in sends:
            cp.wait_send()
        mine.wait()

    return pl.pallas_call(
        body, name=name, in_specs=[ANY], out_specs=ANY,
        out_shape=jax.ShapeDtypeStruct(parts.shape, parts.dtype),
        scratch_shapes=[pltpu.SemaphoreType.DMA((N_CHIPS - 1,)), pltpu.SemaphoreType.DMA((N_CHIPS - 1,)),
                        pltpu.SemaphoreType.DMA])(parts)


def _add_pairs(blocks, got, core, *, tr, name):
    q, r, c = got.shape

    def body(core_ref, a_ref, b_ref, o_ref):
        o_ref[...] = (a_ref[...].astype(F32) + b_ref[...].astype(F32)).astype(o_ref.dtype)

    spec = pl.BlockSpec((q, tr, c), lambda i, core_ref: (0, i, 0))
    mine = pl.BlockSpec((q, None, tr, c), lambda i, core_ref: (0, core_ref[0], i, 0))
    return pl.pallas_call(
        body, name=name,
        grid_spec=pltpu.PrefetchScalarGridSpec(num_scalar_prefetch=1, grid=(r // tr,), in_specs=[mine, spec],
                                               out_specs=spec),
        out_shape=jax.ShapeDtypeStruct(got.shape, BF16), compiler_params=_cparams(("parallel",)))(
            core, blocks.reshape(q, 2, r, c), got)


def _sum_slots(slots, *, tr, name):
    n_slots, r, c = slots.shape

    def body(s_ref, o_ref):
        acc = s_ref[0].astype(F32)
        for s in range(1, n_slots):
            acc = acc + s_ref[s].astype(F32)
        o_ref[...] = acc

    return pl.pallas_call(
        body, name=name, grid=(r // tr,),
        in_specs=[pl.BlockSpec((n_slots, tr, c), lambda i: (0, i, 0))],
        out_specs=pl.BlockSpec((tr, c), lambda i: (i, 0)),
        out_shape=jax.ShapeDtypeStruct((r, c), F32),
        compiler_params=_cparams(("parallel",)))(slots)


def _adamw_math(w_t, g_t, m_t, v_t):
    m_n = ADAM_B1 * m_t + (1.0 - ADAM_B1) * g_t
    v_n = ADAM_B2 * v_t + (1.0 - ADAM_B2) * (g_t * g_t)
    m_hat = m_n / (1.0 - ADAM_B1 ** ADAM_STEP)
    v_hat = v_n / (1.0 - ADAM_B2 ** ADAM_STEP)
    delta = -ADAM_LR * (m_hat / (jnp.sqrt(v_hat) + ADAM_EPS) + ADAM_WD * w_t)
    return delta, m_n, v_n


def _adamw(w, g, m, v, name):
    shape = w.shape
    cols = shape[-1]
    args = [a.reshape(-1, cols) for a in (w, g, m, v)]
    rows = args[0].shape[0]
    tm = 256 if rows % 256 == 0 else rows
    delta, m_n, v_n = _rowwise(_adamw_math, args, [], [(cols, F32)] * 3, tm=tm, name=name)
    return delta.reshape(shape), m_n.reshape(shape), v_n.reshape(shape)


def _adamw_small(ws, gs, ms, vs, name):
    k = len(ws)
    args = [a.reshape(-1, a.shape[-1]) for group in (ws, gs, ms, vs) for a in group]

    def body(*refs):
        ins, outs = refs[:4 * k], refs[4 * k:]
        for i in range(k):
            res = _adamw_math(*[ins[j * k + i][...] for j in range(4)])
            for j in range(3):
                outs[j * k + i][...] = res[j]

    res = pl.pallas_call(
        body, name=name, out_shape=[jax.ShapeDtypeStruct(a.shape, F32) for a in args[:k]] * 3,
        compiler_params=pltpu.CompilerParams(vmem_limit_bytes=VMEM_LIMIT))(*args)
    return [[res[j * k + i].reshape(ws[i].shape) for i in range(k)] for j in range(3)]


def _local_step(x, positions, target, w):
    bsz, t, _ = x.shape
    n = bsz * t
    tm = 256
    mm = 512
    x2 = x.reshape(n, D_MODEL)
    tgt2 = target.reshape(n, D_MODEL)
    pos = positions.reshape(n)
    tb2 = _rope_tables(pos, B_ROPE_THETA, B_ROPE, B_NOPE, "rope_tables_b")
    tabs_b = [a.reshape(bsz, t, LANES) for a in tb2]

    w_a_in = w["a_w_in"]
    w_qkv, w_z = w_a_in[:, :A_QKV], w_a_in[:, A_QKV:]
    w_a_out = w["a_w_out"]
    w_down = w["kv_w_down"]
    w_down_p = jnp.zeros((D_MODEL, 3 * LANES), BF16).at[:, :B_KV_LORA].set(w_down[:, :B_KV_LORA])
    w_down_p = w_down_p.at[:, B_KV_LORA + B_NOPE:B_KV_LORA + B_QK_DIM].set(w_down[:, B_KV_LORA:])
    wu = w["kv_w_up"].reshape(B_KV_LORA, B_HEADS, B_NOPE + B_VDIM)
    w_upk = jnp.pad(wu[:, :, :B_NOPE], ((0, 0), (0, 0), (0, LANES - B_NOPE))).reshape(B_KV_LORA, B_KPAD)
    w_upv = wu[:, :, B_NOPE:].reshape(B_KV_LORA, B_HEADS * B_VDIM)
    w_b_in = w["b_w_in"]
    w_q_p = jnp.pad(w["b_w_q_up"].reshape(B_Q_LORA, B_HEADS, B_QK_DIM),
                    ((0, 0), (0, 0), (0, LANES - B_QK_DIM))).reshape(B_Q_LORA, B_KPAD)
    w_b_out = w["b_w_out"]

    def tab_extras(tabs2, rows):
        return [(a, (rows, LANES), lambda j, i, kk: (i, 0)) for a in tabs2]

    (hn_a,) = _rowwise(lambda xt, g: (_rms(xt, g),), [x2], [w["a_pre_norm"]], [(D_MODEL, BF16)],
                       tm=tm, name="a_pre_norm")

    def rope_epilogue(acc, o_ref, rows, c_ref, sp_ref, sm_ref):
        c, sp, sm = c_ref[rows, :], sp_ref[rows, :], sm_ref[rows, :]
        for h in range(acc.shape[1] // LANES):
            hs = slice(h * LANES, (h + 1) * LANES)
            o_ref[rows, hs] = _rope(acc[:, hs], c, sp, sm).astype(BF16)

    def to_group(a, d):
        if d == 1:
            return a
        return a.reshape(bsz, t // d, d, a.shape[-1]).transpose(0, 2, 1, 3).reshape(n, a.shape[-1])

    def from_group(a, d):
        if d == 1:
            return a
        return a.reshape(bsz, d, t // d, a.shape[-1]).transpose(0, 2, 1, 3).reshape(n, a.shape[-1])

    def rows_to_cols(r, d):
        return r.reshape(bsz, d, A_HEADS, t // d).transpose(0, 3, 1, 2).reshape(n, A_HEADS)

    def cols_to_rows(cc, d):
        return cc.reshape(bsz, t // d, d, A_HEADS).transpose(0, 2, 3, 1).reshape(bsz * d, A_HEADS, t // d)

    z_a = _matmul(hn_a, w_z, out_dtype=F32, tm=mm, tn=A_WIDTH, name="a_gate")
    hn_g, tabs_g, qk_g, v_g, o_g, lse_g = [], [], [], [], [], []
    for g, d in enumerate(A_DILATIONS):
        hn_g.append(to_group(hn_a, d))
        tabs_g.append(_rope_tables(to_group(pos.reshape(n, 1), d).reshape(n), A_ROPE_THETA, A_ROT_DIM, 0,
                                   f"rope_tables_a_g{g}"))
        w_g = w_qkv[:, g * 3 * A_WIDTH:(g + 1) * 3 * A_WIDTH]
        qk = _matmul(hn_g[g], w_g[:, :2 * A_WIDTH], out_dtype=BF16, tm=mm, tn=A_WIDTH, name=f"a_qk_g{g}",
                     epilogue=rope_epilogue, epilogue_rows=EPILOGUE_ROWS, extras=tab_extras(tabs_g[g], mm))
        v = _matmul(hn_g[g], w_g[:, 2 * A_WIDTH:], out_dtype=BF16, tm=mm, tn=A_WIDTH, name=f"a_v_g{g}")
        qk_g.append(qk.reshape(bsz * d, t // d, 2 * A_WIDTH))
        v_g.append(v.reshape(bsz * d, t // d, A_WIDTH))
        o, lse = _attn_a_fwd(qk_g[g], v_g[g], f"attn_a_fwd_g{g}")
        o_g.append(from_group(o.reshape(n, A_WIDTH), d))
        lse_g.append(rows_to_cols(lse, d))

    def merge_fn(o0, o1, o2, l0, l1, l2, z):
        lmax = jnp.maximum(jnp.maximum(l0, l1), l2)
        e0, e1, e2 = jnp.exp2(l0 - lmax), jnp.exp2(l1 - lmax), jnp.exp2(l2 - lmax)
        den = e0 + e1 + e2
        w0, w1, w2 = e0 / den, e1 / den, e2 / den
        parts = []
        for h in range(A_HEADS):
            hs = slice(h * A_HEAD_DIM, (h + 1) * A_HEAD_DIM)
            parts.append(w0[:, h:h + 1] * o0[:, hs] + w1[:, h:h + 1] * o1[:, hs] + w2[:, h:h + 1] * o2[:, hs])
        o = jnp.concatenate(parts, axis=1)
        return o * _silu(z), o, lmax + jnp.log2(den)

    y_a, o_a2, lse_a = _rowwise(merge_fn, [*o_g, *lse_g, z_a], [],
                                [(A_WIDTH, BF16), (A_WIDTH, F32), (A_HEADS, F32)], tm=tm, name="a_merge_gate")
    out_a = _matmul(y_a, w_a_out, out_dtype=F32, tm=mm, tn=D_MODEL, name="a_out")

    def mid_fn(xt, out, g_post, g_kv, g_b):
        h1 = xt + _rms(out, g_post)
        return h1, _rms(h1, g_kv), _rms(h1, g_b)

    h1, hn_kv, hn_b = _rowwise(mid_fn, [x2, out_a], [w["a_post_norm"], w["kv_norm"], w["b_pre_norm"]],
                               [(D_MODEL, F32), (D_MODEL, BF16), (D_MODEL, BF16)], tm=tm, name="a_post_norm")

    ckr = _matmul(hn_kv, w_down_p, out_dtype=F32, tm=mm, tn=3 * LANES, name="kv_down")

    def latent_fn(ck, c, sp, sm, g):
        return _rms(ck[:, :B_KV_LORA], g), _rope(ck[:, B_KV_LORA:], c, sp, sm)

    c_kv, k_rope = _rowwise(latent_fn, [ckr, *tb2], [w["kv_latent_norm"]], [(B_KV_LORA, BF16), (LANES, F32)],
                            tm=tm, name="kv_latent_norm")

    def kpad_epilogue(acc, o_ref, rows, kr_ref):
        kr = kr_ref[rows, :]
        for h in range(acc.shape[1] // LANES):
            hs = slice(h * LANES, (h + 1) * LANES)
            o_ref[rows, hs] = (acc[:, hs] + kr).astype(BF16)

    kpad = _matmul(c_kv, w_upk, out_dtype=BF16, tm=mm, tn=1024, name="kv_up_k", epilogue=kpad_epilogue,
                   epilogue_rows=EPILOGUE_ROWS,
                   extras=[(k_rope, (mm, LANES), lambda j, i, kk: (i, 0))])
    v_b = _matmul(c_kv, w_upv, out_dtype=BF16, tm=mm, tn=1024, name="kv_up_v")

    proj_b = _matmul(hn_b, w_b_in, out_dtype=F32, tm=mm, tn=w_b_in.shape[1], name="b_in")
    (c_q,) = _rowwise(lambda p, g: (_rms(p[:, :B_Q_LORA], g),), [proj_b], [w["b_q_norm"]], [(B_Q_LORA, BF16)],
                      tm=tm, name="b_q_norm")

    q_b = _matmul(c_q, w_q_p, out_dtype=BF16, tm=mm, tn=1024, name="b_q_up", epilogue=rope_epilogue,
                  epilogue_rows=EPILOGUE_ROWS,
                  extras=tab_extras(tb2, mm))
    q_b3, kpad3, v_b3 = q_b.reshape(bsz, t, B_KPAD), kpad.reshape(bsz, t, B_KPAD), v_b.reshape(bsz, t, -1)
    w_vt = jnp.pad(wu[:, :, B_NOPE:], ((0, 0), (0, 0), (0, LANES - B_VDIM))).reshape(B_KV_LORA, B_KPAD)
    vt1 = _value_heads_t(c_kv, w_vt, bsz, t)
    o_b, lse_b = _attn_b_fwd(q_b3, kpad3, vt1)
    o_b2 = o_b.reshape(n, -1)
    (y_b,) = _rowwise(lambda o, p: (o * _silu(p[:, B_Q_LORA:]),), [o_b2, proj_b], [], [(D_MODEL, BF16)],
                      tm=tm, name="b_gate_mul")
    out_b = _matmul(y_b, w_b_out, out_dtype=F32, tm=mm, tn=D_MODEL, name="b_out")

    def head_fn(h1t, out, tgt, g):
        e = h1t + _rms(out, g) - tgt
        loss_row = 0.5 * jnp.mean(e * e, axis=-1, keepdims=True)
        dh2 = e * (1.0 / D_MODEL)
        d_out, dg = _rms_bwd(out, g, dh2)
        return dh2, d_out, dg, jnp.broadcast_to(loss_row * (1.0 / LANES), (loss_row.shape[0], LANES))

    dh2, d_out_b, dg_b_post, loss_acc = _rowwise(
        head_fn, [h1, out_b, tgt2], [w["b_post_norm"]], [(D_MODEL, F32), (D_MODEL, BF16)], [D_MODEL, LANES],
        tm=tm, name="loss_head")

    dy_b = _matmul(d_out_b, w_b_out, tb=True, out_dtype=BF16, tm=mm, tn=D_MODEL, name="b_out_dx")
    gw_b_out = _matmul(y_b, d_out_b, ta=True, out_dtype=F32, tm=mm, tn=D_MODEL, tk=2048, name="b_out_dw")

    def gate_b_bwd(dy, o, p):
        z = p[:, B_Q_LORA:]
        return dy * _silu(z), dy * o * _silu_grad(z)

    do_b, dz_b = _rowwise(gate_b_bwd, [dy_b, o_b2, proj_b], [], [(D_MODEL, BF16), (D_MODEL, F32)],
                          tm=tm, name="b_gate_bwd")
    do_b3 = do_b.reshape(bsz, t, -1)
    dq_b, dk_b, dv_b = _attn_b_bwd(q_b3, kpad3, v_b3, do_b3, o_b, lse_b, tabs_b)
    dq_b2, dk_b2, dv_b2 = dq_b.reshape(n, B_KPAD), dk_b.reshape(n, B_KPAD), dv_b.reshape(n, -1)

    dc_kv = _matmul(dk_b2, w_upk, tb=True, out_dtype=F32, tm=mm, tn=B_KV_LORA, name="kv_up_k_dx")
    dc_kv = _matmul(dv_b2, w_upv, tb=True, out_dtype=BF16, tm=mm, tn=B_KV_LORA, name="kv_up_v_dx",
                    epilogue=_add_epilogue, extras=[(dc_kv, (mm, B_KV_LORA), lambda j, i, kk: (i, j))])
    gw_upk = _matmul(c_kv, dk_b2, ta=True, out_dtype=F32, tm=B_KV_LORA, tn=1024, tk=2048, name="kv_up_k_dw")
    gw_upv = _matmul(c_kv, dv_b2, ta=True, out_dtype=F32, tm=B_KV_LORA, tn=1024, tk=2048, name="kv_up_v_dw")

    def latent_bwd(ck, dck, dk, c, sp, sm, g):
        d1, dg = _rms_bwd(ck[:, :B_KV_LORA], g, dck)
        ksum = dk[:, :LANES].astype(F32)
        for h in range(1, B_HEADS):
            ksum = ksum + dk[:, h * LANES:(h + 1) * LANES].astype(F32)
        lane = lax.broadcasted_iota(jnp.int32, ksum.shape, 1)
        ksum = jnp.where((lane >= B_NOPE) & (lane < B_QK_DIM), ksum, 0.0)
        return jnp.concatenate([d1, _rope_t(ksum, c, sp, sm)], axis=1), dg

    dckr, dg_latent = _rowwise(latent_bwd, [ckr, dc_kv, dk_b2, *tb2], [w["kv_latent_norm"]],
                               [(3 * LANES, BF16)], [B_KV_LORA], tm=tm, name="kv_latent_bwd")
    dhn_kv = _matmul(dckr, w_down_p, tb=True, out_dtype=BF16, tm=mm, tn=D_MODEL, name="kv_down_dx")
    gw_down_p = _matmul(hn_kv, dckr, ta=True, out_dtype=F32, tm=mm, tn=3 * LANES, tk=2048, name="kv_down_dw")

    dc_q = _matmul(dq_b2, w_q_p, tb=True, out_dtype=BF16, tm=mm, tn=B_Q_LORA, name="b_q_up_dx")
    gw_q_p = _matmul(c_q, dq_b2, ta=True, out_dtype=F32, tm=B_Q_LORA, tn=1024, tk=2048, name="b_q_up_dw")

    def q_norm_bwd(p, dcq, dz, g):
        d1, dg = _rms_bwd(p[:, :B_Q_LORA], g, dcq)
        return jnp.concatenate([d1, dz], axis=1), dg

    dproj_b, dg_q_norm = _rowwise(q_norm_bwd, [proj_b, dc_q, dz_b], [w["b_q_norm"]],
                                  [(w_b_in.shape[1], BF16)], [B_Q_LORA], tm=tm, name="b_q_norm_bwd")
    dhn_b = _matmul(dproj_b, w_b_in, tb=True, out_dtype=BF16, tm=mm, tn=D_MODEL, name="b_in_dx")
    gw_b_in = _matmul(hn_b, dproj_b, ta=True, out_dtype=F32, tm=mm, tn=w_b_in.shape[1], tk=2048, name="b_in_dw")

    def mid_bwd(h1t, dh2t, dkv, db, g_kv, g_b, g_post, out):
        dxa, ra = _rms_bwd(h1t, g_kv, dkv)
        dxb, rb = _rms_bwd(h1t, g_b, db)
        dh1 = dh2t + dxa + dxb
        d_out, rp = _rms_bwd(out, g_post, dh1)
        return dh1, d_out, ra, rb, rp

    def mid_bwd_fn(h1t, dh2t, dkv, db, out, g_kv, g_b, g_post):
        return mid_bwd(h1t, dh2t, dkv, db, g_kv, g_b, g_post, out)

    dh1, d_out_a, dg_kv, dg_b_pre, dg_a_post = _rowwise(
        mid_bwd_fn, [h1, dh2, dhn_kv, dhn_b, out_a], [w["kv_norm"], w["b_pre_norm"], w["a_post_norm"]],
        [(D_MODEL, F32), (D_MODEL, BF16)], [D_MODEL] * 3, tm=tm, name="mid_bwd")

    dy_a = _matmul(d_out_a, w_a_out, tb=True, out_dtype=BF16, tm=mm, tn=A_WIDTH, name="a_out_dx")
    gw_a_out = _matmul(y_a, d_out_a, ta=True, out_dtype=F32, tm=mm, tn=D_MODEL, tk=2048, name="a_out_dw")

    def gate_a_bwd(dy, o, z):
        do = dy * _silu(z)
        prod = do * o
        lane = lax.broadcasted_iota(jnp.int32, (prod.shape[0], A_HEADS), 1)
        dsum = jnp.zeros((prod.shape[0], A_HEADS), F32)
        for h in range(A_HEADS):
            col = jnp.sum(prod[:, h * A_HEAD_DIM:(h + 1) * A_HEAD_DIM], axis=1, keepdims=True)
            dsum = jnp.where(lane == h, col, dsum)
        return do, dy * o * _silu_grad(z), dsum

    do_a, dz_a, dsum_a = _rowwise(gate_a_bwd, [dy_a, o_a2, z_a], [],
                                  [(A_WIDTH, BF16), (A_WIDTH, BF16), (A_HEADS, F32)], tm=tm, name="a_gate_bwd")
    dhn_a = _matmul(dz_a, w_z, tb=True, out_dtype=F32, tm=mm, tn=D_MODEL, name="a_gate_dx")
    gw_parts = []
    for g, d in enumerate(A_DILATIONS):
        s_n, ln = bsz * d, t // d
        dqkv = _attn_a_bwd(qk_g[g], v_g[g], to_group(do_a, d).reshape(s_n, ln, A_WIDTH), cols_to_rows(lse_a, d),
                           cols_to_rows(dsum_a, d), [a.reshape(s_n, ln, LANES) for a in tabs_g[g]],
                           f"attn_a_bwd_g{g}").reshape(n, 3 * A_WIDTH)
        w_g = w_qkv[:, g * 3 * A_WIDTH:(g + 1) * 3 * A_WIDTH]
        if d == 1:
            dhn_a = _matmul(dqkv, w_g, tb=True, out_dtype=F32, tm=mm, tn=D_MODEL, name=f"a_qkv_dx_g{g}",
                            epilogue=_add_epilogue, extras=[(dhn_a, (mm, D_MODEL), lambda j, i, kk: (i, j))])
        else:
            dhn_a = dhn_a + from_group(_matmul(dqkv, w_g, tb=True, out_dtype=BF16, tm=mm, tn=D_MODEL,
                                               name=f"a_qkv_dx_g{g}"), d)
        gw_parts.append(_matmul(hn_g[g], dqkv, ta=True, out_dtype=F32, tm=mm, tn=1024, tk=2048,
                                name=f"a_qkv_dw_g{g}"))
    gw_parts.append(_matmul(hn_a, dz_a, ta=True, out_dtype=F32, tm=mm, tn=1024, tk=2048, name="a_gate_dw"))
    gw_a_in = jnp.concatenate(gw_parts, axis=1)

    def first_bwd(xt, dhn, dh1t, g):
        dx, dg = _rms_bwd(xt, g, dhn)
        return dh1t + dx, dg

    grad_x, dg_a_pre = _rowwise(first_bwd, [x2, dhn_a, dh1], [w["a_pre_norm"]], [(D_MODEL, F32)], [D_MODEL],
                                tm=tm, name="a_pre_norm_bwd")

    gw_down = jnp.concatenate([gw_down_p[:, :B_KV_LORA], gw_down_p[:, B_KV_LORA + B_NOPE:B_KV_LORA + B_QK_DIM]], axis=1)
    gw_up = jnp.concatenate([gw_upk.reshape(B_KV_LORA, B_HEADS, LANES)[:, :, :B_NOPE],
                             gw_upv.reshape(B_KV_LORA, B_HEADS, B_VDIM)], axis=2).reshape(B_KV_LORA, -1)
    gw_q_up = gw_q_p.reshape(B_Q_LORA, B_HEADS, LANES)[:, :, :B_QK_DIM].reshape(B_Q_LORA, -1)
    grads = {"a_w_in": gw_a_in, "a_w_out": gw_a_out, "kv_w_down": gw_down, "kv_w_up": gw_up,
             "b_w_in": gw_b_in, "b_w_q_up": gw_q_up, "b_w_out": gw_b_out}
    gains = {"a_pre_norm": dg_a_pre, "a_post_norm": dg_a_post, "kv_norm": dg_kv, "kv_latent_norm": dg_latent,
             "b_pre_norm": dg_b_pre, "b_q_norm": dg_q_norm, "b_post_norm": dg_b_post}
    gains = {k: jnp.sum(a, axis=0) for k, a in gains.items()}
    return jnp.sum(loss_acc), grad_x.reshape(bsz, t, D_MODEL), grads, gains


WEIGHT_ORDER = ("a_pre_norm", "a_w_in", "a_w_out", "a_post_norm", "kv_norm", "kv_w_down", "kv_latent_norm",
                "kv_w_up", "b_pre_norm", "b_w_in", "b_q_norm", "b_w_q_up", "b_w_out", "b_post_norm")
MATRICES = (("a_w_in", 1024, 10240, 1), ("a_w_out", 1024, 1024, 0), ("kv_w_down", 1024, 288, 0),
            ("kv_w_up", 256, 2048, 1), ("b_w_in", 1024, 1408, 1), ("b_w_q_up", 384, 1536, 1),
            ("b_w_out", 1024, 1024, 0))
SHARDED_GAINS = ("a_pre_norm", "a_post_norm")
GAIN_WIDTHS = (("a_pre_norm", 1024), ("a_post_norm", 1024), ("kv_norm", 1024), ("kv_latent_norm", 256),
               ("b_pre_norm", 1024), ("b_q_norm", 384), ("b_post_norm", 1024))
GAIN_ROWS = 48


def _shard_rows(rows, cols):
    return rows * cols // (N_DEV * LANES)


def _whole_from_blocks(blocks, rows, cols, axis):
    if axis == 1:
        return blocks.reshape(N_DEV, rows, cols // N_DEV).transpose(1, 0, 2).reshape(rows, cols)
    return blocks.reshape(rows, cols)


def _blocks_from_whole(whole, rows, cols, axis):
    if axis == 1:
        whole = whole.reshape(rows, N_DEV, cols // N_DEV).transpose(1, 0, 2)
    return whole.reshape(N_DEV, -1, LANES)


def kernel(x, positions, a_pre_norm, a_w_in, a_w_out, a_post_norm, kv_norm, kv_w_down, kv_latent_norm, kv_w_up, b_pre_norm, b_w_in, b_q_norm, b_w_q_up, b_w_out, b_post_norm, loss_target, m_a_pre_norm, m_a_w_in, m_a_w_out, m_a_post_norm, m_kv_norm, m_kv_w_down, m_kv_latent_norm, m_kv_w_up, m_b_pre_norm, m_b_w_in, m_b_q_norm, m_b_w_q_up, m_b_w_out, m_b_post_norm, v_a_pre_norm, v_a_w_in, v_a_w_out, v_a_post_norm, v_kv_norm, v_kv_w_down, v_kv_latent_norm, v_kv_w_up, v_b_pre_norm, v_b_w_in, v_b_q_norm, v_b_w_q_up, v_b_w_out, v_b_post_norm):
    weights = dict(a_pre_norm=a_pre_norm, a_w_in=a_w_in, a_w_out=a_w_out, a_post_norm=a_post_norm, kv_norm=kv_norm,
                   kv_w_down=kv_w_down, kv_latent_norm=kv_latent_norm, kv_w_up=kv_w_up, b_pre_norm=b_pre_norm,
                   b_w_in=b_w_in, b_q_norm=b_q_norm, b_w_q_up=b_w_q_up, b_w_out=b_w_out, b_post_norm=b_post_norm)
    m_in = dict(a_pre_norm=m_a_pre_norm, a_w_in=m_a_w_in, a_w_out=m_a_w_out, a_post_norm=m_a_post_norm,
                kv_norm=m_kv_norm, kv_w_down=m_kv_w_down, kv_latent_norm=m_kv_latent_norm, kv_w_up=m_kv_w_up,
                b_pre_norm=m_b_pre_norm, b_w_in=m_b_w_in, b_q_norm=m_b_q_norm, b_w_q_up=m_b_w_q_up,
                b_w_out=m_b_w_out, b_post_norm=m_b_post_norm)
    v_in = dict(a_pre_norm=v_a_pre_norm, a_w_in=v_a_w_in, a_w_out=v_a_w_out, a_post_norm=v_a_post_norm,
                kv_norm=v_kv_norm, kv_w_down=v_kv_w_down, kv_latent_norm=v_kv_latent_norm, kv_w_up=v_kv_w_up,
                b_pre_norm=v_b_pre_norm, b_w_in=v_b_w_in, b_q_norm=v_b_q_norm, b_w_q_up=v_b_w_q_up,
                b_w_out=v_b_w_out, b_post_norm=v_b_post_norm)
    me = 4 * lax.axis_index("x") + 2 * lax.axis_index("y") + lax.axis_index("c")

    flat = jnp.concatenate([weights[name].astype(BF16).reshape(-1, LANES) for name, _, _, _ in MATRICES], axis=0)
    gathered = _all_gather(flat, "gather_weights")
    whole = {}
    off = 0
    for name, rows, cols, axis in MATRICES:
        nr = _shard_rows(rows, cols)
        whole[name] = _whole_from_blocks(gathered[:, off:off + nr], rows, cols, axis)
        off += nr
    gain_shard = jnp.concatenate([weights[name].reshape(1, LANES) for name in SHARDED_GAINS]
                                 + [jnp.zeros((8 - len(SHARDED_GAINS), LANES), F32)], axis=0)
    gain_blocks = _all_gather(gain_shard, "gather_gains")
    for i, name in enumerate(SHARDED_GAINS):
        whole[name] = gain_blocks[:, i, :].reshape(1, D_MODEL)
    for name in ("kv_norm", "kv_latent_norm", "b_pre_norm", "b_q_norm", "b_post_norm"):
        whole[name] = weights[name].reshape(1, -1)

    loss_part, grad_x, grads, gains = _local_step(x, positions, loss_target, whole)

    blocks = jnp.concatenate([_blocks_from_whole(grads[name], rows, cols, axis).astype(BF16)
                              for name, rows, cols, axis in MATRICES], axis=1)
    got = _exchange_sibling(blocks, "scatter_grads_core")
    core = lax.axis_index("c").astype(jnp.int32).reshape(1)
    landed = _exchange_chips(_add_pairs(blocks, got, core, tr=2512, name="add_core_grads"), "scatter_grads_chip")
    summed = _sum_slots(landed, tr=2512, name="sum_grads")
    grad_out = {}
    off = 0
    for name, rows, cols, axis in MATRICES:
        nr = _shard_rows(rows, cols)
        grad_out[name] = summed[off:off + nr].reshape(weights[name].shape)
        off += nr

    vec = jnp.concatenate([gains[name] for name, _ in GAIN_WIDTHS] + [jnp.full((LANES,), loss_part, F32)])
    vec = jnp.pad(vec, (0, GAIN_ROWS * LANES - vec.shape[0])).reshape(GAIN_ROWS, LANES)
    total = _sum_slots(_all_gather(vec, "gather_gain_grads"), tr=GAIN_ROWS, name="sum_gain_grads").reshape(-1)
    off = 0
    for name, width in GAIN_WIDTHS:
        g = total[off:off + width]
        if name in SHARDED_GAINS:
            g = lax.dynamic_slice(g, (me * LANES,), (LANES,))
        grad_out[name] = g.reshape(weights[name].shape)
        off += width
    loss = total[off]

    deltas, new_m, new_v = {}, {}, {}
    big = "a_w_in"
    deltas[big], new_m[big], new_v[big] = _adamw(weights[big], grad_out[big], m_in[big], v_in[big], "adamw_" + big)
    small = [name for name in WEIGHT_ORDER if name != big]
    res = _adamw_small(*[[d[name] for name in small] for d in (weights, grad_out, m_in, v_in)], "adamw_small")
    for out, vals in zip((deltas, new_m, new_v), res):
        out.update(zip(small, vals))
    return (loss, grad_x, *[grad_out[k] for k in WEIGHT_ORDER], *[deltas[k] for k in WEIGHT_ORDER],
            *[new_m[k] for k in WEIGHT_ORDER], *[new_v[k] for k in WEIGHT_ORDER])
```

```python
import jax
import jax.numpy as jnp
from jax import lax
from jax.experimental import pallas as pl
from jax.experimental.pallas import tpu as pltpu

F32 = jnp.float32
BF16 = jnp.bfloat16

N_DEV = 8
D_MODEL = 1024
NORM_EPS = 1e-6
A_GROUPS = 3
A_DILATIONS = (1, 4, 16)
A_HEADS = 8
A_HEAD_DIM = 128
A_WIDTH = 1024
A_ROT_DIM = 32
A_ROPE_THETA = 500000.0
A_QKV = A_GROUPS * 3 * A_WIDTH
B_HEADS = 16
B_NOPE = 64
B_ROPE = 32
B_QK_DIM = 96
B_VDIM = 64
B_Q_LORA = 384
B_KV_LORA = 256
B_ROPE_THETA = 10000.0
B_KPAD = B_HEADS * 128
ADAM_LR = 0.001
ADAM_B1 = 0.9
ADAM_B2 = 0.999
ADAM_EPS = 1e-08
ADAM_WD = 0.01
ADAM_STEP = 10

LANES = 128
BAND = 128
EPILOGUE_ROWS = 128
NEG = -1e30
VMEM_LIMIT = 56 * 1024 * 1024
MESH = pl.DeviceIdType.MESH


def _cparams(sem):
    return pltpu.CompilerParams(dimension_semantics=sem, vmem_limit_bytes=VMEM_LIMIT)


def _rowwise(fn, rows, bcast, outs, accs=(), *, tm, name):
    n = rows[0].shape[0]
    nr, nb, no = len(rows), len(bcast), len(outs)

    def body(*refs):
        res = fn(*[r[...] for r in refs[:nr + nb]])
        out_refs = refs[nr + nb:nr + nb + no]
        acc_refs = refs[nr + nb + no:]
        for r, v in zip(out_refs, res[:no]):
            r[...] = v.astype(r.dtype)
        if acc_refs:
            @pl.when(pl.program_id(0) == 0)
            def _():
                for r in acc_refs:
                    r[...] = jnp.zeros_like(r)
            for r, v in zip(acc_refs, res[no:]):
                r[...] += v.reshape(tm // 8, 8, v.shape[-1]).sum(axis=0)

    in_specs = [pl.BlockSpec((tm, a.shape[1]), lambda i: (i, 0)) for a in rows]
    in_specs += [pl.BlockSpec(a.shape, lambda i: (0, 0)) for a in bcast]
    out_specs = [pl.BlockSpec((tm, c), lambda i: (i, 0)) for c, _ in outs]
    out_specs += [pl.BlockSpec((8, c), lambda i: (0, 0)) for c in accs]
    out_shape = [jax.ShapeDtypeStruct((n, c), dt) for c, dt in outs]
    out_shape += [jax.ShapeDtypeStruct((8, c), F32) for c in accs]
    return pl.pallas_call(
        body, name=name, grid=(n // tm,), in_specs=in_specs, out_specs=out_specs, out_shape=out_shape,
        compiler_params=_cparams(("arbitrary",)))(*rows, *bcast)


def _matmul(a, b, *, out_dtype, tm, tn, tk=None, name, epilogue=None, extras=(), ta=False, tb=False,
            epilogue_rows=None):
    epilogue_rows = epilogue_rows or tm
    k, m = a.shape[::-1] if not ta else a.shape
    n = b.shape[0] if tb else b.shape[1]
    tk = tk or k
    nk = k // tk
    ne = len(extras)
    dot = _dot_tn if ta else (_dot_nt if tb else _dot)
    assert epilogue is None or (nk == 1 and not ta)

    def body(*refs):
        a_ref, b_ref = refs[:2]
        ex = refs[2:2 + ne]
        o_ref = refs[2 + ne]
        if epilogue is not None:
            b_tile = b_ref[...].astype(BF16)
            for r0 in range(0, tm, epilogue_rows):
                rows = slice(r0, r0 + epilogue_rows)
                epilogue(dot(a_ref[rows, :].astype(BF16), b_tile), o_ref, rows, *ex)
            return
        part = dot(a_ref[...].astype(BF16), b_ref[...].astype(BF16))
        if nk == 1:
            o_ref[...] = part.astype(o_ref.dtype)
        else:
            acc_ref = refs[-1]
            kk = pl.program_id(2)

            @pl.when(kk == 0)
            def _():
                acc_ref[...] = part

            @pl.when(kk > 0)
            def _():
                acc_ref[...] += part

            @pl.when(kk == nk - 1)
            def _():
                o_ref[...] = acc_ref[...].astype(o_ref.dtype)

    a_spec = pl.BlockSpec((tk, tm), lambda j, i, kk: (kk, i)) if ta else pl.BlockSpec((tm, tk), lambda j, i, kk: (i, kk))
    b_spec = pl.BlockSpec((tn, tk), lambda j, i, kk: (j, kk)) if tb else pl.BlockSpec((tk, tn), lambda j, i, kk: (kk, j))
    in_specs = [a_spec, b_spec] + [pl.BlockSpec(bs, im) for _, bs, im in extras]
    return pl.pallas_call(
        body, name=name, grid=(n // tn, m // tm, nk), in_specs=in_specs,
        out_specs=pl.BlockSpec((tm, tn), lambda j, i, kk: (i, j)),
        out_shape=jax.ShapeDtypeStruct((m, n), out_dtype),
        scratch_shapes=[pltpu.VMEM((tm, tn), F32)] if nk > 1 else [],
        compiler_params=_cparams(("parallel", "parallel", "arbitrary")))(a, b, *[e[0] for e in extras])


def _add_epilogue(acc, o_ref, rows, prev_ref):
    o_ref[rows, :] = (acc + prev_ref[rows, :]).astype(o_ref.dtype)


def _rope(x, c, sp, sm):
    return x * c + pltpu.roll(x, 16, 1) * sp + pltpu.roll(x, LANES - 16, 1) * sm


def _rope_t(dy, c, sp, sm):
    return dy * c + pltpu.roll(dy * sp, LANES - 16, 1) + pltpu.roll(dy * sm, 16, 1)


def _rope_tables(positions, theta, rot_dim, lane0, name):
    n = positions.shape[0]
    half = rot_dim // 2
    inv_freq = 1.0 / (theta ** (jnp.arange(half, dtype=F32) * (2.0 / rot_dim)))
    freq = jnp.concatenate([jnp.zeros((lane0,), F32), inv_freq, inv_freq,
                            jnp.zeros((LANES - lane0 - rot_dim,), F32)]).reshape(1, LANES)

    def fn(p, f):
        ang = p * f
        cos, sin = jnp.cos(ang), jnp.sin(ang)
        lane = lax.broadcasted_iota(jnp.int32, ang.shape, 1) - lane0
        first, second = (lane >= 0) & (lane < half), (lane >= half) & (lane < rot_dim)
        return jnp.where(first | second, cos, 1.0), jnp.where(second, sin, 0.0), jnp.where(first, -sin, 0.0)

    return _rowwise(fn, [positions.astype(F32).reshape(n, 1)], [freq], [(LANES, F32)] * 3, tm=512, name=name)


def _value_heads_t(c_kv, w_vt, bsz, t):
    n, k = c_kv.shape
    tm, tn = 512, 1024

    def body(a_ref, b_ref, o_ref):
        acc = _dot(a_ref[...], b_ref[...])
        lane = lax.broadcasted_iota(jnp.int32, (1, tn), 1)
        acc = acc + jnp.where(lax.rem(lane, LANES) == B_VDIM, 1.0, 0.0)
        o_ref[...] = acc.T.astype(BF16)

    per_seq = t // tm
    return pl.pallas_call(
        body, name="kv_up_v_t", grid=(w_vt.shape[1] // tn, n // tm),
        in_specs=[pl.BlockSpec((tm, k), lambda j, i: (i, 0)), pl.BlockSpec((k, tn), lambda j, i: (0, j))],
        out_specs=pl.BlockSpec((None, tn, tm), lambda j, i: (i // per_seq, j, lax.rem(i, per_seq))),
        out_shape=jax.ShapeDtypeStruct((bsz, w_vt.shape[1], t), BF16),
        compiler_params=_cparams(("parallel", "parallel")))(c_kv, w_vt)


def _rms(x, g):
    xf = x.astype(F32)
    return xf * lax.rsqrt(jnp.mean(xf * xf, axis=-1, keepdims=True) + NORM_EPS) * g


def _rms_bwd(x, g, dy):
    xf = x.astype(F32)
    rstd = lax.rsqrt(jnp.mean(xf * xf, axis=-1, keepdims=True) + NORM_EPS)
    xhat = xf * rstd
    dxhat = dy * g
    dx = rstd * (dxhat - xhat * jnp.mean(dxhat * xhat, axis=-1, keepdims=True))
    return dx, dy * xhat


def _silu(z):
    return z * jax.nn.sigmoid(z)


def _silu_grad(z):
    s = jax.nn.sigmoid(z)
    return s * (1.0 + z * (1.0 - s))


def _dot_nt(a, b):
    return lax.dot_general(a, b, (((1,), (1,)), ((), ())), preferred_element_type=F32)


def _dot_tn(a, b):
    return lax.dot_general(a, b, (((0,), (0,)), ((), ())), preferred_element_type=F32)


def _dot(a, b):
    return jnp.dot(a, b, preferred_element_type=F32)


A_SCALE = A_HEAD_DIM ** -0.5
LOG2E = 1.4426950408889634
A_C2 = A_SCALE * LOG2E
A_HEAD_GROUP = 4


def _attn_a_fwd(qk, v, name):
    s_n, ln, _ = qk.shape
    nb = ln // BAND
    blk = (None, BAND, A_WIDTH)

    def body(q_ref, kc_ref, kp_ref, vc_ref, vp_ref, o_ref, lse_ref):
        kpos = lax.broadcasted_iota(jnp.int32, (2 * BAND, BAND), 0)
        qpos = lax.broadcasted_iota(jnp.int32, (2 * BAND, BAND), 1) + BAND
        first_key = jnp.where(pl.program_id(1) > 0, 0, BAND)
        mask = (kpos <= qpos) & (kpos >= qpos - BAND) & (kpos >= first_key)
        rows = []
        for h0 in range(0, A_HEADS, A_HEAD_GROUP):
            hss = [slice(h * A_HEAD_DIM, (h + 1) * A_HEAD_DIM) for h in range(h0, h0 + A_HEAD_GROUP)]
            sts = [_dot_nt(jnp.concatenate([kp_ref[:, hs], kc_ref[:, hs]], axis=0), q_ref[:, hs]) for hs in hss]
            ps, ls = [], []
            for st in sts:
                st = jnp.where(mask, st * A_C2, NEG)
                m = jnp.max(st, axis=0, keepdims=True)
                p = jnp.exp2(st - m)
                l_row = jnp.sum(p, axis=0, keepdims=True)
                ps.append(p.astype(BF16))
                ls.append(l_row)
                rows.append(m + jnp.log2(l_row))
            ots = [_dot_tn(jnp.concatenate([vp_ref[:, hs], vc_ref[:, hs]], axis=0), p) for hs, p in zip(hss, ps)]
            for hs, o_t, l_row in zip(hss, ots, ls):
                o_ref[:, hs] = (o_t / l_row).T.astype(BF16)
        lse_ref[...] = jnp.concatenate(rows, axis=0)

    def col(c, off):
        return lambda s, l: (s, jnp.maximum(l + off, 0), c)

    return pl.pallas_call(
        body, name=name, grid=(s_n, nb),
        in_specs=[pl.BlockSpec(blk, col(0, 0)), pl.BlockSpec(blk, col(1, 0)), pl.BlockSpec(blk, col(1, -1)),
                  pl.BlockSpec(blk, col(0, 0)), pl.BlockSpec(blk, col(0, -1))],
        out_specs=[pl.BlockSpec(blk, lambda s, l: (s, l, 0)),
                   pl.BlockSpec((None, A_HEADS, BAND), lambda s, l: (s, 0, l))],
        out_shape=[jax.ShapeDtypeStruct((s_n, ln, A_WIDTH), BF16), jax.ShapeDtypeStruct((s_n, A_HEADS, ln), F32)],
        compiler_params=_cparams(("parallel", "arbitrary")))(qk, qk, qk, v, v)


def _attn_a_bwd(qk, v, do, lse2, dsum, tabs, name):
    s_n, ln, _ = qk.shape
    nb = ln // BAND
    blk = (None, BAND, A_WIDTH)

    def body(q_ref, qn_ref, kc_ref, kp_ref, vc_ref, vp_ref, do_ref, don_ref, lse_ref, lsen_ref, ds_ref, dsn_ref,
             c_ref, sp_ref, sm_ref, out_ref):
        l_idx = pl.program_id(1)
        kpos = lax.broadcasted_iota(jnp.int32, (2 * BAND, BAND), 0)
        qpos = lax.broadcasted_iota(jnp.int32, (2 * BAND, BAND), 1) + BAND
        first_key = jnp.where(l_idx > 0, 0, BAND)
        mask_q = (kpos <= qpos) & (kpos >= qpos - BAND) & (kpos >= first_key)
        kpos2 = lax.broadcasted_iota(jnp.int32, (BAND, 2 * BAND), 0)
        qpos2 = lax.broadcasted_iota(jnp.int32, (BAND, 2 * BAND), 1)
        last_query = jnp.where(l_idx < nb - 1, 2 * BAND, BAND)
        mask_k = (kpos2 <= qpos2) & (kpos2 >= qpos2 - BAND) & (qpos2 < last_query)
        c, sp, sm = c_ref[...], sp_ref[...], sm_ref[...]
        lse_q, ds_q = lse_ref[...], ds_ref[...]
        lse_k = jnp.concatenate([lse_q, lsen_ref[...]], axis=1)
        ds_k = jnp.concatenate([ds_q, dsn_ref[...]], axis=1)
        for h0 in range(0, A_HEADS, A_HEAD_GROUP):
            hl = list(range(h0, h0 + A_HEAD_GROUP))
            hss = [slice(h * A_HEAD_DIM, (h + 1) * A_HEAD_DIM) for h in hl]
            k2s = [jnp.concatenate([kp_ref[:, hs], kc_ref[:, hs]], axis=0) for hs in hss]
            v2s = [jnp.concatenate([vp_ref[:, hs], vc_ref[:, hs]], axis=0) for hs in hss]
            q2s = [jnp.concatenate([q_ref[:, hs], qn_ref[:, hs]], axis=0) for hs in hss]
            do2s = [jnp.concatenate([do_ref[:, hs], don_ref[:, hs]], axis=0) for hs in hss]
            sts = [_dot_nt(k2, q_ref[:, hs]) for k2, hs in zip(k2s, hss)]
            dpts = [_dot_nt(v2, do_ref[:, hs]) for v2, hs in zip(v2s, hss)]
            st2s = [_dot_nt(kc_ref[:, hs], q2) for q2, hs in zip(q2s, hss)]
            dpt2s = [_dot_nt(vc_ref[:, hs], do2) for do2, hs in zip(do2s, hss)]
            dsts, dst2s, p2s = [], [], []
            for i, h in enumerate(hl):
                p = jnp.exp2(jnp.where(mask_q, sts[i] * A_C2, NEG) - lse_q[h:h + 1])
                dsts.append((p * (dpts[i] - ds_q[h:h + 1]) * A_SCALE).astype(BF16))
                p2 = jnp.exp2(jnp.where(mask_k, st2s[i] * A_C2, NEG) - lse_k[h:h + 1])
                dst2s.append((p2 * (dpt2s[i] - ds_k[h:h + 1]) * A_SCALE).astype(BF16))
                p2s.append(p2.astype(BF16))
            dqs = [_dot_tn(dsts[i], k2s[i]) for i in range(A_HEAD_GROUP)]
            dks = [_dot(dst2s[i], q2s[i]) for i in range(A_HEAD_GROUP)]
            dvs = [_dot(p2s[i], do2s[i]) for i in range(A_HEAD_GROUP)]
            for i, h in enumerate(hl):
                out_ref[:, hss[i]] = _rope_t(dqs[i], c, sp, sm).astype(BF16)
                out_ref[:, A_WIDTH + h * A_HEAD_DIM:A_WIDTH + (h + 1) * A_HEAD_DIM] = _rope_t(dks[i], c, sp, sm).astype(BF16)
                out_ref[:, 2 * A_WIDTH + h * A_HEAD_DIM:2 * A_WIDTH + (h + 1) * A_HEAD_DIM] = dvs[i].astype(BF16)

    def col(c, off):
        return lambda s, l: (s, jnp.clip(l + off, 0, nb - 1), c)

    def row(off):
        return pl.BlockSpec((None, A_HEADS, BAND), lambda s, l: (s, 0, jnp.clip(l + off, 0, nb - 1)))

    tspec = pl.BlockSpec((None, BAND, LANES), lambda s, l: (s, l, 0))
    in_specs = [pl.BlockSpec(blk, col(0, 0)), pl.BlockSpec(blk, col(0, 1)),
                pl.BlockSpec(blk, col(1, 0)), pl.BlockSpec(blk, col(1, -1)),
                pl.BlockSpec(blk, col(0, 0)), pl.BlockSpec(blk, col(0, -1)),
                pl.BlockSpec(blk, col(0, 0)), pl.BlockSpec(blk, col(0, 1)),
                row(0), row(1), row(0), row(1), tspec, tspec, tspec]
    return pl.pallas_call(
        body, name=name, grid=(s_n, nb), in_specs=in_specs,
        out_specs=pl.BlockSpec((None, BAND, 3 * A_WIDTH), lambda s, l: (s, l, 0)),
        out_shape=jax.ShapeDtypeStruct((s_n, ln, 3 * A_WIDTH), BF16),
        compiler_params=_cparams(("parallel", "arbitrary")))(
            qk, qk, qk, qk, v, v, do, do, lse2, lse2, dsum, dsum, *tabs)


B_TQ = 256
B_SCALE = B_QK_DIM ** -0.5
B_C2 = B_SCALE * LOG2E


def _b_row_tile(tq):
    return pl.BlockSpec((None, None, 8, tq), lambda b, j, i: (b, j, 0, i))


def _b_specs(t, tq, tk):
    pair_tile = pl.BlockSpec((None, tq, 2 * LANES), lambda b, j, i: (b, i, j))
    pair_full = pl.BlockSpec((None, t, 2 * LANES), lambda b, j, i: (b, 0, j))
    one_tile = pl.BlockSpec((None, tk, LANES), lambda b, j, i: (b, i, j))
    one_full = pl.BlockSpec((None, t, LANES), lambda b, j, i: (b, 0, j))
    tab_tile = pl.BlockSpec((None, tq, LANES), lambda b, j, i: (b, i, 0))
    return pair_tile, pair_full, one_tile, one_full, tab_tile


def _key_le_query(kb, qb, tk, tq):
    kpos = kb * tk + lax.broadcasted_iota(jnp.int32, (tk, tq), 0)
    qpos = qb * tq + lax.broadcasted_iota(jnp.int32, (tk, tq), 1)
    return kpos <= qpos


def _attn_b_fwd(q, kpad, vt1):
    bsz, t, _ = q.shape
    tq = tk = B_TQ
    nq = t // tq

    def body(q_ref, k_ref, vt_ref, o_ref, lse_ref):
        qblk = pl.program_id(2)
        qs = [q_ref[:, hh * LANES:(hh + 1) * LANES] for hh in range(2)]

        def scores(kb):
            start = pl.multiple_of(kb * tk, tk)
            return [_dot_nt(k_ref[pl.ds(start, tk), hh * LANES:(hh + 1) * LANES], qs[hh]) for hh in range(2)]

        def pv(kb, ps):
            start = pl.multiple_of(kb * tk, tk)
            return [_dot(vt_ref[hh * LANES:(hh + 1) * LANES, pl.ds(start, tk)], ps[hh]) for hh in range(2)]

        def softmax(ss, ms, accs, kb, masked):
            out_m, out_acc, out_p = [], [], []
            for hh in range(2):
                s = ss[hh] * B_C2
                if masked:
                    s = jnp.where(_key_le_query(kb, qblk, tk, tq), s, NEG)
                m_new = jnp.maximum(ms[hh], jnp.max(s, axis=0, keepdims=True))
                out_acc.append(jnp.exp2(ms[hh] - m_new) * accs[hh])
                out_p.append(jnp.exp2(s - m_new).astype(BF16))
                out_m.append(m_new)
            return out_m, out_acc, out_p

        def step(kb, carry):
            ss, ps, ms, accs = carry
            pvs = pv(jnp.maximum(kb - 1, 0), ps)
            ss_next = scores(kb + 1)
            accs = [accs[hh] + pvs[hh] for hh in range(2)]
            ms, accs, ps = softmax(ss, ms, accs, kb, False)
            return (ss_next, ps, ms, accs)

        init = (scores(0), [jnp.zeros((tk, tq), BF16)] * 2, [jnp.full((1, tq), NEG, F32)] * 2,
                [jnp.zeros((LANES, tq), F32)] * 2)
        ss, ps, ms, accs = lax.fori_loop(0, qblk, step, init)
        pvs = pv(jnp.maximum(qblk - 1, 0), ps)
        accs = [accs[hh] + pvs[hh] for hh in range(2)]
        ms, accs, ps = softmax(ss, ms, accs, qblk, True)
        pvs = pv(qblk, ps)
        accs = [accs[hh] + pvs[hh] for hh in range(2)]
        ls = [accs[hh][B_VDIM:B_VDIM + 1] for hh in range(2)]
        o_t = jnp.concatenate([accs[0][:B_VDIM] / ls[0], accs[1][:B_VDIM] / ls[1]], axis=0)
        o_ref[...] = o_t.T
        lse_ref[...] = jnp.concatenate([ms[0] + jnp.log2(ls[0]), ms[1] + jnp.log2(ls[1]),
                                        jnp.zeros((6, tq), F32)], axis=0)

    pair_tile, pair_full, one_tile, _, _ = _b_specs(t, tq, tk)
    vt_spec = pl.BlockSpec((None, 2 * LANES, t), lambda b, j, i: (b, j, 0))
    return pl.pallas_call(
        body, name="attn_b_fwd", grid=(bsz, B_HEADS // 2, nq),
        in_specs=[pair_tile, pair_full, vt_spec],
        out_specs=[one_tile, _b_row_tile(tq)],
        out_shape=[jax.ShapeDtypeStruct((bsz, t, B_HEADS * B_VDIM), F32),
                   jax.ShapeDtypeStruct((bsz, B_HEADS // 2, 8, t), F32)],
        compiler_params=_cparams(("parallel", "parallel", "arbitrary")))(q, kpad, vt1)


def _attn_b_bwd(q, kpad, v, do, o, lse2, tabs):
    bsz, t, _ = q.shape
    tq = tk = B_TQ
    nq = t // tq

    def body(q_ref, k_ref, v_ref, do_ref, o_ref, lse_ref, c_ref, sp_ref, sm_ref, dq_ref, dk_ref, dv_ref,
             dqt_scr, dsum_scr):
        lane = lax.broadcasted_iota(jnp.int32, (tk, LANES), 1)
        sel_lane = lax.broadcasted_iota(jnp.int32, (8, LANES), 1)
        sel_row = lax.broadcasted_iota(jnp.int32, (8, LANES), 0)
        sel = jnp.where((sel_lane < B_VDIM) == (sel_row == 0), 1.0, 0.0)
        sel = jnp.where(sel_row < 2, sel, 0.0).astype(BF16)

        def rows(blk):
            return pl.ds(pl.multiple_of(blk * tq, tq), tq)

        def dsum_step(qb, carry):
            prod = do_ref[rows(qb), :].astype(F32) * o_ref[rows(qb), :]
            hi = prod.astype(BF16)
            lo = (prod - hi.astype(F32)).astype(BF16)
            dsum_scr[:, rows(qb)] = _dot_nt(sel, hi) + _dot_nt(sel, lo)
            return carry

        lax.fori_loop(0, nq, dsum_step, 0)
        dqt_scr[...] = jnp.zeros_like(dqt_scr)

        def kv_step(kb, carry):
            ks = [k_ref[rows(kb), hh * LANES:(hh + 1) * LANES] for hh in range(2)]
            vb = v_ref[rows(kb), :]
            zero = jnp.zeros_like(vb)
            vs = [jnp.where(lane < B_VDIM, vb, zero), jnp.where(lane < B_VDIM, zero, vb)]

            def make_step(masked):
                def step(qb, acc):
                    qs = [q_ref[rows(qb), hh * LANES:(hh + 1) * LANES] for hh in range(2)]
                    do_b = do_ref[rows(qb), :]
                    ss = [_dot_nt(ks[hh], qs[hh]) for hh in range(2)]
                    dps = [_dot_nt(vs[hh], do_b) for hh in range(2)]
                    pbs, dss = [], []
                    for hh in range(2):
                        s = ss[hh] * B_C2
                        if masked:
                            s = jnp.where(_key_le_query(kb, qb, tk, tq), s, NEG)
                        p = jnp.exp2(s - lse_ref[hh:hh + 1, rows(qb)])
                        dss.append((p * (dps[hh] - dsum_scr[hh:hh + 1, rows(qb)]) * B_SCALE).astype(BF16))
                        pbs.append(p.astype(BF16))
                    for hh in range(2):
                        dqt_scr[hh, :, rows(qb)] += _dot_tn(ks[hh], dss[hh])
                    return (acc[0] + _dot(dss[0], qs[0]), acc[1] + _dot(dss[1], qs[1]),
                            acc[2] + _dot(pbs[0], do_b), acc[3] + _dot(pbs[1], do_b))
                return step

            acc = make_step(True)(kb, (jnp.zeros((tk, LANES), F32),) * 4)
            acc = lax.fori_loop(kb + 1, nq, make_step(False), acc)
            dk_ref[rows(kb), :LANES] = acc[0].astype(BF16)
            dk_ref[rows(kb), LANES:] = acc[1].astype(BF16)
            dv_ref[rows(kb), :] = jnp.where(lane < B_VDIM, acc[2], acc[3]).astype(BF16)
            return carry

        lax.fori_loop(0, nq, kv_step, 0)

        def dq_step(qb, carry):
            c, sp, sm = c_ref[rows(qb), :], sp_ref[rows(qb), :], sm_ref[rows(qb), :]
            for hh in range(2):
                dq_ref[rows(qb), hh * LANES:(hh + 1) * LANES] = _rope_t(dqt_scr[hh, :, rows(qb)].T, c, sp, sm).astype(BF16)
            return carry

        lax.fori_loop(0, nq, dq_step, 0)

    pair_full = pl.BlockSpec((None, t, 2 * LANES), lambda b, j: (b, 0, j))
    one_full = pl.BlockSpec((None, t, LANES), lambda b, j: (b, 0, j))
    row_full = pl.BlockSpec((None, None, 8, t), lambda b, j: (b, j, 0, 0))
    tab_full = pl.BlockSpec((None, t, LANES), lambda b, j: (b, 0, 0))
    return pl.pallas_call(
        body, name="attn_b_bwd", grid=(bsz, B_HEADS // 2),
        in_specs=[pair_full, pair_full, one_full, one_full, one_full, row_full, tab_full, tab_full, tab_full],
        out_specs=[pair_full, pair_full, one_full],
        out_shape=[jax.ShapeDtypeStruct((bsz, t, B_KPAD), BF16), jax.ShapeDtypeStruct((bsz, t, B_KPAD), BF16),
                   jax.ShapeDtypeStruct((bsz, t, B_HEADS * B_VDIM), BF16)],
        scratch_shapes=[pltpu.VMEM((2, LANES, t), F32), pltpu.VMEM((8, t), F32)],
        compiler_params=_cparams(("parallel", "parallel")))(q, kpad, v, do, o, lse2, *tabs)


def _attn_b_dq(q, kpad, v, do, o, lse2, tabs):
    bsz, t, _ = q.shape
    tq = tk = B_TQ
    nq = t // tq

    def body(q_ref, k_ref, v_ref, do_ref, o_ref, lse_ref, c_ref, sp_ref, sm_ref, dq_ref, dsum_ref):
        qblk = pl.program_id(2)
        lane = lax.broadcasted_iota(jnp.int32, (tq, LANES), 1)
        do_b = do_ref[...]
        prod = do_b.astype(F32) * o_ref[...]
        hi = prod.astype(BF16)
        lo = (prod - hi.astype(F32)).astype(BF16)
        sel_lane = lax.broadcasted_iota(jnp.int32, (8, LANES), 1)
        sel_row = lax.broadcasted_iota(jnp.int32, (8, LANES), 0)
        sel = jnp.where((sel_lane < B_VDIM) == (sel_row == 0), 1.0, 0.0)
        sel = jnp.where(sel_row < 2, sel, 0.0).astype(BF16)
        dsum = _dot_nt(sel, hi) + _dot_nt(sel, lo)
        dsum_ref[...] = dsum
        lse = lse_ref[...]
        qs = [q_ref[:, hh * LANES:(hh + 1) * LANES] for hh in range(2)]
        zero = jnp.zeros_like(do_b)
        dos = [jnp.where(lane < B_VDIM, do_b, zero), jnp.where(lane < B_VDIM, zero, do_b)]

        def kblock(kb, hh):
            start = pl.multiple_of(kb * tk, tk)
            return k_ref[pl.ds(start, tk), hh * LANES:(hh + 1) * LANES]

        def dots(kb):
            start = pl.multiple_of(kb * tk, tk)
            vb = v_ref[pl.ds(start, tk), :]
            return ([_dot_nt(kblock(kb, hh), qs[hh]) for hh in range(2)], [_dot_nt(vb, dos[hh]) for hh in range(2)])

        def ds_of(ss, dps, kb, masked):
            out = []
            for hh in range(2):
                s = ss[hh] * B_C2
                if masked:
                    s = jnp.where(_key_le_query(kb, qblk, tk, tq), s, NEG)
                p = jnp.exp2(s - lse[hh:hh + 1])
                out.append((p * (dps[hh] - dsum[hh:hh + 1]) * B_SCALE).astype(BF16))
            return out

        def accum(acc, kb, dss):
            return [acc[hh] + _dot_tn(kblock(kb, hh), dss[hh]) for hh in range(2)]

        def step(kb, carry):
            dss, acc = carry
            ss, dps = dots(kb)
            acc = accum(acc, jnp.maximum(kb - 1, 0), dss)
            return (ds_of(ss, dps, kb, False), acc)

        init = ([jnp.zeros((tk, tq), BF16)] * 2, [jnp.zeros((LANES, tq), F32)] * 2)
        dss, acc = lax.fori_loop(0, qblk, step, init)
        ss, dps = dots(qblk)
        acc = accum(acc, jnp.maximum(qblk - 1, 0), dss)
        acc = accum(acc, qblk, ds_of(ss, dps, qblk, True))
        c, sp, sm = c_ref[...], sp_ref[...], sm_ref[...]
        for hh in range(2):
            dq_ref[:, hh * LANES:(hh + 1) * LANES] = _rope_t(acc[hh].T, c, sp, sm).astype(BF16)

    pair_tile, pair_full, one_tile, one_full, tab_tile = _b_specs(t, tq, tk)
    row_tile = _b_row_tile(tq)
    return pl.pallas_call(
        body, name="attn_b_dq", grid=(bsz, B_HEADS // 2, nq),
        in_specs=[pair_tile, pair_full, one_full, one_tile, one_tile, row_tile, tab_tile, tab_tile, tab_tile],
        out_specs=[pair_tile, row_tile],
        out_shape=[jax.ShapeDtypeStruct((bsz, t, B_KPAD), BF16),
                   jax.ShapeDtypeStruct((bsz, B_HEADS // 2, 8, t), F32)],
        compiler_params=_cparams(("parallel", "parallel", "arbitrary")))(q, kpad, v, do, o, lse2, *tabs)


def _attn_b_dkv(q, kpad, v, do, lse2, dsum):
    bsz, t, _ = q.shape
    tq = tk = B_TQ
    nq = t // tq

    def body(q_ref, k_ref, v_ref, do_ref, lse_ref, dsum_ref, dk_ref, dv_ref):
        kblk = pl.program_id(2)
        lane = lax.broadcasted_iota(jnp.int32, (tk, LANES), 1)
        ks = [k_ref[:, hh * LANES:(hh + 1) * LANES] for hh in range(2)]
        vb = v_ref[...]
        zero = jnp.zeros_like(vb)
        vs = [jnp.where(lane < B_VDIM, vb, zero), jnp.where(lane < B_VDIM, zero, vb)]

        def make_step(masked):
            def step(qb, carry):
                start = pl.multiple_of(qb * tq, tq)
                qs = [q_ref[pl.ds(start, tq), hh * LANES:(hh + 1) * LANES] for hh in range(2)]
                do_b = do_ref[pl.ds(start, tq), :]
                ss = [_dot_nt(ks[hh], qs[hh]) for hh in range(2)]
                dps = [_dot_nt(vs[hh], do_b) for hh in range(2)]
                pbs, dss = [], []
                for hh in range(2):
                    s = ss[hh] * B_C2
                    if masked:
                        s = jnp.where(_key_le_query(kblk, qb, tk, tq), s, NEG)
                    p = jnp.exp2(s - lse_ref[hh:hh + 1, pl.ds(start, tq)])
                    dss.append((p * (dps[hh] - dsum_ref[hh:hh + 1, pl.ds(start, tq)]) * B_SCALE).astype(BF16))
                    pbs.append(p.astype(BF16))
                return (carry[0] + _dot(dss[0], qs[0]), carry[1] + _dot(dss[1], qs[1]),
                        carry[2] + _dot(pbs[0], do_b), carry[3] + _dot(pbs[1], do_b))
            return step

        init = (jnp.zeros((tk, LANES), F32),) * 4
        carry = make_step(True)(kblk, init)
        carry = lax.fori_loop(kblk + 1, nq, make_step(False), carry)
        dk_ref[:, :LANES] = carry[0].astype(BF16)
        dk_ref[:, LANES:] = carry[1].astype(BF16)
        dv_ref[...] = jnp.where(lane < B_VDIM, carry[2], carry[3]).astype(BF16)

    pair_tile, pair_full, one_tile, one_full, _ = _b_specs(t, tq, tk)
    row_full = pl.BlockSpec((None, None, 8, t), lambda b, j, i: (b, j, 0, 0))
    return pl.pallas_call(
        body, name="attn_b_dkv", grid=(bsz, B_HEADS // 2, nq),
        in_specs=[pair_full, pair_tile, one_tile, one_full, row_full, row_full],
        out_specs=[pair_tile, one_tile],
        out_shape=[jax.ShapeDtypeStruct((bsz, t, B_KPAD), BF16),
                   jax.ShapeDtypeStruct((bsz, t, B_HEADS * B_VDIM), BF16)],
        compiler_params=_cparams(("parallel", "parallel", "arbitrary")))(q, kpad, v, do, lse2, dsum)


ANY = pl.BlockSpec(memory_space=pl.ANY)


def _all_gather(shard, name):
    def body(x_ref, out_ref, send_sems, recv_sems, local_sem):
        x, y, c = lax.axis_index("x"), lax.axis_index("y"), lax.axis_index("c")
        me, sibling = (x, y, c), (x, y, 1 - c)
        chips = [(1 - x, y), (x, 1 - y), (1 - x, 1 - y)]

        def rows(px, py, pc):
            return out_ref.at[4 * px + 2 * py + pc]

        def copy(k, block, to, src=None):
            return pltpu.make_async_remote_copy(
                src_ref=rows(*block) if src is None else src, dst_ref=rows(*block),
                send_sem=send_sems.at[k], recv_sem=recv_sems.at[k], device_id=to, device_id_type=MESH)

        mine = pltpu.make_async_copy(x_ref, rows(*me), local_sem)
        mine.start()
        first = [copy(0, me, sibling, src=x_ref)]
        first += [copy(1 + j, me, (*chip, c), src=x_ref) for j, chip in enumerate(chips)]
        for cp in first:
            cp.start()
        passed = [copy(4 + j, (*chip, c), sibling) for j, chip in enumerate(chips)]
        for j, chip in enumerate(chips):
            copy(1 + j, (*chip, c), me).wait_recv()
            passed[j].start()
        copy(0, sibling, me).wait_recv()
        for j, chip in enumerate(chips):
            copy(4 + j, (*chip, 1 - c), me).wait_recv()
        for cp in first + passed:
            cp.wait_send()
        mine.wait()

    return pl.pallas_call(
        body, name=name, in_specs=[ANY], out_specs=ANY,
        out_shape=jax.ShapeDtypeStruct((N_DEV,) + shard.shape, shard.dtype),
        scratch_shapes=[pltpu.SemaphoreType.DMA((7,)), pltpu.SemaphoreType.DMA((7,)), pltpu.SemaphoreType.DMA])(shard)


def _all_gather_weights(flat, wide, name):
    ns = wide.shape[1]

    def body(f_ref, w_ref, fo_ref, wo_ref, send_sems, recv_sems, local_sems):
        x, y, c = lax.axis_index("x"), lax.axis_index("y"), lax.axis_index("c")
        me, sibling = (x, y, c), (x, y, 1 - c)
        chips = [(1 - x, y), (x, 1 - y), (1 - x, 1 - y)]

        def place(a, px, py, pc):
            idx = 4 * px + 2 * py + pc
            if a == 0:
                return fo_ref.at[idx]
            return wo_ref.at[:, pl.ds(pl.multiple_of(idx * ns, LANES), ns)]

        def copy(a, k, block, to, src=None):
            return pltpu.make_async_remote_copy(
                src_ref=place(a, *block) if src is None else src, dst_ref=place(a, *block),
                send_sem=send_sems.at[a, k], recv_sem=recv_sems.at[a, k], device_id=to, device_id_type=MESH)

        own = (f_ref, w_ref)
        mine = [pltpu.make_async_copy(own[a], place(a, *me), local_sems.at[a]) for a in range(2)]
        first = []
        for a in range(2):
            mine[a].start()
            first.append(copy(a, 0, me, sibling, src=own[a]))
            first += [copy(a, 1 + j, me, (*chip, c), src=own[a]) for j, chip in enumerate(chips)]
        for cp in first:
            cp.start()
        passed = [[copy(a, 4 + j, (*chip, c), sibling) for j, chip in enumerate(chips)] for a in range(2)]
        for a in range(2):
            for j, chip in enumerate(chips):
                copy(a, 1 + j, (*chip, c), me).wait_recv()
                passed[a][j].start()
        for a in range(2):
            copy(a, 0, sibling, me).wait_recv()
            for j, chip in enumerate(chips):
                copy(a, 4 + j, (*chip, 1 - c), me).wait_recv()
        for cp in first + passed[0] + passed[1]:
            cp.wait_send()
        for cp in mine:
            cp.wait()

    return pl.pallas_call(
        body, name=name, in_specs=[ANY, ANY], out_specs=[ANY, ANY],
        out_shape=[jax.ShapeDtypeStruct((N_DEV,) + flat.shape, flat.dtype),
                   jax.ShapeDtypeStruct((wide.shape[0], N_DEV * ns), wide.dtype)],
        scratch_shapes=[pltpu.SemaphoreType.DMA((2, 7)), pltpu.SemaphoreType.DMA((2, 7)),
                        pltpu.SemaphoreType.DMA((2,))])(flat, wide)


N_CHIPS = 4


def _exchange_sibling(blocks, name):
    def body(g_ref, got_ref, send_sems, recv_sems):
        x, y, c = lax.axis_index("x"), lax.axis_index("y"), lax.axis_index("c")
        sends = [pltpu.make_async_remote_copy(
            src_ref=g_ref.at[q, 1 - c], dst_ref=got_ref.at[q], send_sem=send_sems.at[q],
            recv_sem=recv_sems.at[q], device_id=(x, y, 1 - c), device_id_type=MESH) for q in range(N_CHIPS)]
        for cp in sends:
            cp.start()
        for cp in sends:
            cp.wait_recv()
        for cp in sends:
            cp.wait_send()

    return pl.pallas_call(
        body, name=name, in_specs=[ANY], out_specs=ANY,
        out_shape=jax.ShapeDtypeStruct((N_CHIPS,) + blocks.shape[2:], blocks.dtype),
        scratch_shapes=[pltpu.SemaphoreType.DMA((N_CHIPS,)), pltpu.SemaphoreType.DMA((N_CHIPS,))])(blocks)


def _exchange_chips(parts, name):
    def body(p_ref, out_ref, send_sems, recv_sems, local_sem):
        x, y, c = lax.axis_index("x"), lax.axis_index("y"), lax.axis_index("c")
        me = 2 * x + y

        def peer(k):
            return (1 - x if k & 2 else x, 1 - y if k & 1 else y)

        def copy(k):
            px, py = peer(k)
            return pltpu.make_async_remote_copy(
                src_ref=p_ref.at[2 * px + py], dst_ref=out_ref.at[me], send_sem=send_sems.at[k - 1],
                recv_sem=recv_sems.at[k - 1], device_id=(px, py, c), device_id_type=MESH)

        def arrival(k):
            px, py = peer(k)
            slot = out_ref.at[2 * px + py]
            return pltpu.make_async_remote_copy(
                src_ref=slot, dst_ref=slot, send_sem=send_sems.at[k - 1], recv_sem=recv_sems.at[k - 1],
                device_id=(px, py, c), device_id_type=MESH)

        mine = pltpu.make_async_copy(p_ref.at[me], out_ref.at[me], local_sem)
        mine.start()
        sends = [copy(k) for k in range(1, N_CHIPS)]
        for cp in sends:
            cp.start()
        for k in range(1, N_CHIPS):
            arrival(k).wait_recv()
        for cp in sends:
            cp.wait_send()
        mine.wait()

    return pl.pallas_call(
        body, name=name, in_specs=[ANY], out_specs=ANY,
        out_shape=jax.ShapeDtypeStruct(parts.shape, parts.dtype),
        scratch_shapes=[pltpu.SemaphoreType.DMA((N_CHIPS - 1,)), pltpu.SemaphoreType.DMA((N_CHIPS - 1,)),
                        pltpu.SemaphoreType.DMA])(parts)


def _add_pairs(blocks, got, core, *, tr, name):
    q, r, c = got.shape

    def body(core_ref, a_ref, b_ref, o_ref):
        o_ref[...] = (a_ref[...].astype(F32) + b_ref[...].astype(F32)).astype(o_ref.dtype)

    spec = pl.BlockSpec((q, tr, c), lambda i, core_ref: (0, i, 0))
    mine = pl.BlockSpec((q, None, tr, c), lambda i, core_ref: (0, core_ref[0], i, 0))
    return pl.pallas_call(
        body, name=name,
        grid_spec=pltpu.PrefetchScalarGridSpec(num_scalar_prefetch=1, grid=(r // tr,), in_specs=[mine, spec],
                                               out_specs=spec),
        out_shape=jax.ShapeDtypeStruct(got.shape, BF16), compiler_params=_cparams(("parallel",)))(
            core, blocks, got)


def _sum_slots(slots, *, tr, name):
    n_slots, r, c = slots.shape

    def body(s_ref, o_ref):
        acc = s_ref[0].astype(F32)
        for s in range(1, n_slots):
            acc = acc + s_ref[s].astype(F32)
        o_ref[...] = acc

    return pl.pallas_call(
        body, name=name, grid=(r // tr,),
        in_specs=[pl.BlockSpec((n_slots, tr, c), lambda i: (0, i, 0))],
        out_specs=pl.BlockSpec((tr, c), lambda i: (i, 0)),
        out_shape=jax.ShapeDtypeStruct((r, c), F32),
        compiler_params=_cparams(("parallel",)))(slots)


def _adamw_math(w_t, g_t, m_t, v_t):
    m_n = ADAM_B1 * m_t + (1.0 - ADAM_B1) * g_t
    v_n = ADAM_B2 * v_t + (1.0 - ADAM_B2) * (g_t * g_t)
    m_hat = m_n / (1.0 - ADAM_B1 ** ADAM_STEP)
    v_hat = v_n / (1.0 - ADAM_B2 ** ADAM_STEP)
    delta = -ADAM_LR * (m_hat / (jnp.sqrt(v_hat) + ADAM_EPS) + ADAM_WD * w_t)
    return delta, m_n, v_n


def _adamw(w, g, m, v, name):
    shape = w.shape
    cols = shape[-1]
    args = [a.reshape(-1, cols) for a in (w, g, m, v)]
    rows = args[0].shape[0]
    tm = 256 if rows % 256 == 0 else rows
    delta, m_n, v_n = _rowwise(_adamw_math, args, [], [(cols, F32)] * 3, tm=tm, name=name)
    return delta.reshape(shape), m_n.reshape(shape), v_n.reshape(shape)


def _adamw_small(ws, gs, ms, vs, name):
    k = len(ws)
    args = [a.reshape(-1, a.shape[-1]) for group in (ws, gs, ms, vs) for a in group]

    def body(*refs):
        ins, outs = refs[:4 * k], refs[4 * k:]
        for i in range(k):
            res = _adamw_math(*[ins[j * k + i][...] for j in range(4)])
            for j in range(3):
                outs[j * k + i][...] = res[j]

    res = pl.pallas_call(
        body, name=name, out_shape=[jax.ShapeDtypeStruct(a.shape, F32) for a in args[:k]] * 3,
        compiler_params=pltpu.CompilerParams(vmem_limit_bytes=VMEM_LIMIT))(*args)
    return [[res[j * k + i].reshape(ws[i].shape) for i in range(k)] for j in range(3)]


def _local_step(x, positions, target, w):
    bsz, t, _ = x.shape
    n = bsz * t
    tm = 256
    mm = 512
    x2 = x.reshape(n, D_MODEL)
    tgt2 = target.reshape(n, D_MODEL)
    pos = positions.reshape(n)
    tb2 = _rope_tables(pos, B_ROPE_THETA, B_ROPE, B_NOPE, "rope_tables_b")
    tabs_b = [a.reshape(bsz, t, LANES) for a in tb2]

    w_a_in = w["a_w_in"]
    w_qkv, w_z = w_a_in[:, :A_QKV], w_a_in[:, A_QKV:]
    w_a_out = w["a_w_out"]
    w_down = w["kv_w_down"]
    w_down_p = jnp.zeros((D_MODEL, 3 * LANES), BF16).at[:, :B_KV_LORA].set(w_down[:, :B_KV_LORA])
    w_down_p = w_down_p.at[:, B_KV_LORA + B_NOPE:B_KV_LORA + B_QK_DIM].set(w_down[:, B_KV_LORA:])
    wu = w["kv_w_up"].reshape(B_KV_LORA, B_HEADS, B_NOPE + B_VDIM)
    w_upk = jnp.pad(wu[:, :, :B_NOPE], ((0, 0), (0, 0), (0, LANES - B_NOPE))).reshape(B_KV_LORA, B_KPAD)
    w_upv = wu[:, :, B_NOPE:].reshape(B_KV_LORA, B_HEADS * B_VDIM)
    w_b_in = w["b_w_in"]
    w_q_p = jnp.pad(w["b_w_q_up"].reshape(B_Q_LORA, B_HEADS, B_QK_DIM),
                    ((0, 0), (0, 0), (0, LANES - B_QK_DIM))).reshape(B_Q_LORA, B_KPAD)
    w_b_out = w["b_w_out"]

    def tab_extras(tabs2, rows):
        return [(a, (rows, LANES), lambda j, i, kk: (i, 0)) for a in tabs2]

    (hn_a,) = _rowwise(lambda xt, g: (_rms(xt, g),), [x2], [w["a_pre_norm"]], [(D_MODEL, BF16)],
                       tm=tm, name="a_pre_norm")

    def rope_epilogue(acc, o_ref, rows, c_ref, sp_ref, sm_ref):
        c, sp, sm = c_ref[rows, :], sp_ref[rows, :], sm_ref[rows, :]
        for h in range(acc.shape[1] // LANES):
            hs = slice(h * LANES, (h + 1) * LANES)
            o_ref[rows, hs] = _rope(acc[:, hs], c, sp, sm).astype(BF16)

    def to_group(a, d):
        if d == 1:
            return a
        return a.reshape(bsz, t // d, d, a.shape[-1]).transpose(0, 2, 1, 3).reshape(n, a.shape[-1])

    def from_group(a, d):
        if d == 1:
            return a
        return a.reshape(bsz, d, t // d, a.shape[-1]).transpose(0, 2, 1, 3).reshape(n, a.shape[-1])

    def rows_to_cols(r, d):
        return r.reshape(bsz, d, A_HEADS, t // d).transpose(0, 3, 1, 2).reshape(n, A_HEADS)

    def cols_to_rows(cc, d):
        return cc.reshape(bsz, t // d, d, A_HEADS).transpose(0, 2, 3, 1).reshape(bsz * d, A_HEADS, t // d)

    z_a = _matmul(hn_a, w_z, out_dtype=F32, tm=mm, tn=A_WIDTH, name="a_gate")
    hn_g, tabs_g, qk_g, v_g, o_g, lse_g = [], [], [], [], [], []
    for g, d in enumerate(A_DILATIONS):
        hn_g.append(to_group(hn_a, d))
        tabs_g.append(_rope_tables(to_group(pos.reshape(n, 1), d).reshape(n), A_ROPE_THETA, A_ROT_DIM, 0,
                                   f"rope_tables_a_g{g}"))
        w_g = w_qkv[:, g * 3 * A_WIDTH:(g + 1) * 3 * A_WIDTH]
        qk = _matmul(hn_g[g], w_g[:, :2 * A_WIDTH], out_dtype=BF16, tm=mm, tn=A_WIDTH, name=f"a_qk_g{g}",
                     epilogue=rope_epilogue, epilogue_rows=EPILOGUE_ROWS, extras=tab_extras(tabs_g[g], mm))
        v = _matmul(hn_g[g], w_g[:, 2 * A_WIDTH:], out_dtype=BF16, tm=mm, tn=A_WIDTH, name=f"a_v_g{g}")
        qk_g.append(qk.reshape(bsz * d, t // d, 2 * A_WIDTH))
        v_g.append(v.reshape(bsz * d, t // d, A_WIDTH))
        o, lse = _attn_a_fwd(qk_g[g], v_g[g], f"attn_a_fwd_g{g}")
        o_g.append(from_group(o.reshape(n, A_WIDTH), d))
        lse_g.append(rows_to_cols(lse, d))

    def merge_fn(o0, o1, o2, l0, l1, l2, z):
        lmax = jnp.maximum(jnp.maximum(l0, l1), l2)
        e0, e1, e2 = jnp.exp2(l0 - lmax), jnp.exp2(l1 - lmax), jnp.exp2(l2 - lmax)
        den = e0 + e1 + e2
        w0, w1, w2 = e0 / den, e1 / den, e2 / den
        parts = []
        for h in range(A_HEADS):
            hs = slice(h * A_HEAD_DIM, (h + 1) * A_HEAD_DIM)
            parts.append(w0[:, h:h + 1] * o0[:, hs] + w1[:, h:h + 1] * o1[:, hs] + w2[:, h:h + 1] * o2[:, hs])
        o = jnp.concatenate(parts, axis=1)
        return o * _silu(z), o, lmax + jnp.log2(den)

    y_a, o_a2, lse_a = _rowwise(merge_fn, [*o_g, *lse_g, z_a], [],
                                [(A_WIDTH, BF16), (A_WIDTH, F32), (A_HEADS, F32)], tm=tm, name="a_merge_gate")
    out_a = _matmul(y_a, w_a_out, out_dtype=F32, tm=mm, tn=D_MODEL, name="a_out")

    def mid_fn(xt, out, g_post, g_kv, g_b):
        h1 = xt + _rms(out, g_post)
        return h1, _rms(h1, g_kv), _rms(h1, g_b)

    h1, hn_kv, hn_b = _rowwise(mid_fn, [x2, out_a], [w["a_post_norm"], w["kv_norm"], w["b_pre_norm"]],
                               [(D_MODEL, F32), (D_MODEL, BF16), (D_MODEL, BF16)], tm=tm, name="a_post_norm")

    ckr = _matmul(hn_kv, w_down_p, out_dtype=F32, tm=mm, tn=3 * LANES, name="kv_down")

    def latent_fn(ck, c, sp, sm, g):
        return _rms(ck[:, :B_KV_LORA], g), _rope(ck[:, B_KV_LORA:], c, sp, sm)

    c_kv, k_rope = _rowwise(latent_fn, [ckr, *tb2], [w["kv_latent_norm"]], [(B_KV_LORA, BF16), (LANES, F32)],
                            tm=tm, name="kv_latent_norm")

    def kpad_epilogue(acc, o_ref, rows, kr_ref):
        kr = kr_ref[rows, :]
        for h in range(acc.shape[1] // LANES):
            hs = slice(h * LANES, (h + 1) * LANES)
            o_ref[rows, hs] = (acc[:, hs] + kr).astype(BF16)

    kpad = _matmul(c_kv, w_upk, out_dtype=BF16, tm=mm, tn=1024, name="kv_up_k", epilogue=kpad_epilogue,
                   epilogue_rows=EPILOGUE_ROWS,
                   extras=[(k_rope, (mm, LANES), lambda j, i, kk: (i, 0))])
    v_b = _matmul(c_kv, w_upv, out_dtype=BF16, tm=mm, tn=1024, name="kv_up_v")

    proj_b = _matmul(hn_b, w_b_in, out_dtype=F32, tm=mm, tn=w_b_in.shape[1], name="b_in")
    (c_q,) = _rowwise(lambda p, g: (_rms(p[:, :B_Q_LORA], g),), [proj_b], [w["b_q_norm"]], [(B_Q_LORA, BF16)],
                      tm=tm, name="b_q_norm")

    q_b = _matmul(c_q, w_q_p, out_dtype=BF16, tm=mm, tn=1024, name="b_q_up", epilogue=rope_epilogue,
                  epilogue_rows=EPILOGUE_ROWS,
                  extras=tab_extras(tb2, mm))
    q_b3, kpad3, v_b3 = q_b.reshape(bsz, t, B_KPAD), kpad.reshape(bsz, t, B_KPAD), v_b.reshape(bsz, t, -1)
    w_vt = jnp.pad(wu[:, :, B_NOPE:], ((0, 0), (0, 0), (0, LANES - B_VDIM))).reshape(B_KV_LORA, B_KPAD)
    vt1 = _value_heads_t(c_kv, w_vt, bsz, t)
    o_b, lse_b = _attn_b_fwd(q_b3, kpad3, vt1)
    o_b2 = o_b.reshape(n, -1)
    (y_b,) = _rowwise(lambda o, p: (o * _silu(p[:, B_Q_LORA:]),), [o_b2, proj_b], [], [(D_MODEL, BF16)],
                      tm=tm, name="b_gate_mul")
    out_b = _matmul(y_b, w_b_out, out_dtype=F32, tm=mm, tn=D_MODEL, name="b_out")

    def head_fn(h1t, out, tgt, g):
        e = h1t + _rms(out, g) - tgt
        loss_row = 0.5 * jnp.mean(e * e, axis=-1, keepdims=True)
        dh2 = e * (1.0 / D_MODEL)
        d_out, dg = _rms_bwd(out, g, dh2)
        return dh2, d_out, dg, jnp.broadcast_to(loss_row * (1.0 / LANES), (loss_row.shape[0], LANES))

    dh2, d_out_b, dg_b_post, loss_acc = _rowwise(
        head_fn, [h1, out_b, tgt2], [w["b_post_norm"]], [(D_MODEL, F32), (D_MODEL, BF16)], [D_MODEL, LANES],
        tm=tm, name="loss_head")

    dy_b = _matmul(d_out_b, w_b_out, tb=True, out_dtype=BF16, tm=mm, tn=D_MODEL, name="b_out_dx")
    gw_b_out = _matmul(y_b, d_out_b, ta=True, out_dtype=F32, tm=mm, tn=D_MODEL, tk=2048, name="b_out_dw")

    def gate_b_bwd(dy, o, p):
        z = p[:, B_Q_LORA:]
        return dy * _silu(z), dy * o * _silu_grad(z)

    do_b, dz_b = _rowwise(gate_b_bwd, [dy_b, o_b2, proj_b], [], [(D_MODEL, BF16), (D_MODEL, F32)],
                          tm=tm, name="b_gate_bwd")
    do_b3 = do_b.reshape(bsz, t, -1)
    dq_b, dk_b, dv_b = _attn_b_bwd(q_b3, kpad3, v_b3, do_b3, o_b, lse_b, tabs_b)
    dq_b2, dk_b2, dv_b2 = dq_b.reshape(n, B_KPAD), dk_b.reshape(n, B_KPAD), dv_b.reshape(n, -1)

    dc_kv = _matmul(dk_b2, w_upk, tb=True, out_dtype=F32, tm=mm, tn=B_KV_LORA, name="kv_up_k_dx")
    dc_kv = _matmul(dv_b2, w_upv, tb=True, out_dtype=BF16, tm=mm, tn=B_KV_LORA, name="kv_up_v_dx",
                    epilogue=_add_epilogue, extras=[(dc_kv, (mm, B_KV_LORA), lambda j, i, kk: (i, j))])
    gw_upk = _matmul(c_kv, dk_b2, ta=True, out_dtype=F32, tm=B_KV_LORA, tn=1024, tk=2048, name="kv_up_k_dw")
    gw_upv = _matmul(c_kv, dv_b2, ta=True, out_dtype=F32, tm=B_KV_LORA, tn=1024, tk=2048, name="kv_up_v_dw")

    def latent_bwd(ck, dck, dk, c, sp, sm, g):
        d1, dg = _rms_bwd(ck[:, :B_KV_LORA], g, dck)
        ksum = dk[:, :LANES].astype(F32)
        for h in range(1, B_HEADS):
            ksum = ksum + dk[:, h * LANES:(h + 1) * LANES].astype(F32)
        lane = lax.broadcasted_iota(jnp.int32, ksum.shape, 1)
        ksum = jnp.where((lane >= B_NOPE) & (lane < B_QK_DIM), ksum, 0.0)
        return jnp.concatenate([d1, _rope_t(ksum, c, sp, sm)], axis=1), dg

    dckr, dg_latent = _rowwise(latent_bwd, [ckr, dc_kv, dk_b2, *tb2], [w["kv_latent_norm"]],
                               [(3 * LANES, BF16)], [B_KV_LORA], tm=tm, name="kv_latent_bwd")
    dhn_kv = _matmul(dckr, w_down_p, tb=True, out_dtype=BF16, tm=mm, tn=D_MODEL, name="kv_down_dx")
    gw_down_p = _matmul(hn_kv, dckr, ta=True, out_dtype=F32, tm=mm, tn=3 * LANES, tk=2048, name="kv_down_dw")

    dc_q = _matmul(dq_b2, w_q_p, tb=True, out_dtype=BF16, tm=mm, tn=B_Q_LORA, name="b_q_up_dx")
    gw_q_p = _matmul(c_q, dq_b2, ta=True, out_dtype=F32, tm=B_Q_LORA, tn=1024, tk=2048, name="b_q_up_dw")

    def q_norm_bwd(p, dcq, dz, g):
        d1, dg = _rms_bwd(p[:, :B_Q_LORA], g, dcq)
        return jnp.concatenate([d1, dz], axis=1), dg

    dproj_b, dg_q_norm = _rowwise(q_norm_bwd, [proj_b, dc_q, dz_b], [w["b_q_norm"]],
                                  [(w_b_in.shape[1], BF16)], [B_Q_LORA], tm=tm, name="b_q_norm_bwd")
    dhn_b = _matmul(dproj_b, w_b_in, tb=True, out_dtype=BF16, tm=mm, tn=D_MODEL, name="b_in_dx")
    gw_b_in = _matmul(hn_b, dproj_b, ta=True, out_dtype=F32, tm=mm, tn=w_b_in.shape[1], tk=2048, name="b_in_dw")

    def mid_bwd(h1t, dh2t, dkv, db, g_kv, g_b, g_post, out):
        dxa, ra = _rms_bwd(h1t, g_kv, dkv)
        dxb, rb = _rms_bwd(h1t, g_b, db)
        dh1 = dh2t + dxa + dxb
        d_out, rp = _rms_bwd(out, g_post, dh1)
        return dh1, d_out, ra, rb, rp

    def mid_bwd_fn(h1t, dh2t, dkv, db, out, g_kv, g_b, g_post):
        return mid_bwd(h1t, dh2t, dkv, db, g_kv, g_b, g_post, out)

    dh1, d_out_a, dg_kv, dg_b_pre, dg_a_post = _rowwise(
        mid_bwd_fn, [h1, dh2, dhn_kv, dhn_b, out_a], [w["kv_norm"], w["b_pre_norm"], w["a_post_norm"]],
        [(D_MODEL, F32), (D_MODEL, BF16)], [D_MODEL] * 3, tm=tm, name="mid_bwd")

    dy_a = _matmul(d_out_a, w_a_out, tb=True, out_dtype=BF16, tm=mm, tn=A_WIDTH, name="a_out_dx")
    gw_a_out = _matmul(y_a, d_out_a, ta=True, out_dtype=F32, tm=mm, tn=D_MODEL, tk=2048, name="a_out_dw")

    def gate_a_bwd(dy, o, z):
        do = dy * _silu(z)
        prod = do * o
        lane = lax.broadcasted_iota(jnp.int32, (prod.shape[0], A_HEADS), 1)
        dsum = jnp.zeros((prod.shape[0], A_HEADS), F32)
        for h in range(A_HEADS):
            col = jnp.sum(prod[:, h * A_HEAD_DIM:(h + 1) * A_HEAD_DIM], axis=1, keepdims=True)
            dsum = jnp.where(lane == h, col, dsum)
        return do, dy * o * _silu_grad(z), dsum

    do_a, dz_a, dsum_a = _rowwise(gate_a_bwd, [dy_a, o_a2, z_a], [],
                                  [(A_WIDTH, BF16), (A_WIDTH, BF16), (A_HEADS, F32)], tm=tm, name="a_gate_bwd")
    dhn_a = _matmul(dz_a, w_z, tb=True, out_dtype=F32, tm=mm, tn=D_MODEL, name="a_gate_dx")
    gw_parts, dhn_groups = [], []
    for g, d in enumerate(A_DILATIONS):
        s_n, ln = bsz * d, t // d
        dqkv = _attn_a_bwd(qk_g[g], v_g[g], to_group(do_a, d).reshape(s_n, ln, A_WIDTH), cols_to_rows(lse_a, d),
                           cols_to_rows(dsum_a, d), [a.reshape(s_n, ln, LANES) for a in tabs_g[g]],
                           f"attn_a_bwd_g{g}").reshape(n, 3 * A_WIDTH)
        w_g = w_qkv[:, g * 3 * A_WIDTH:(g + 1) * 3 * A_WIDTH]
        if d == 1:
            dhn_a = _matmul(dqkv, w_g, tb=True, out_dtype=F32, tm=mm, tn=D_MODEL, name=f"a_qkv_dx_g{g}",
                            epilogue=_add_epilogue, extras=[(dhn_a, (mm, D_MODEL), lambda j, i, kk: (i, j))])
        else:
            dhn_groups.append(from_group(_matmul(dqkv, w_g, tb=True, out_dtype=BF16, tm=mm, tn=D_MODEL,
                                                 name=f"a_qkv_dx_g{g}"), d))
        gw_parts.append(_matmul(hn_g[g], dqkv, ta=True, out_dtype=F32, tm=mm, tn=1024, tk=2048,
                                name=f"a_qkv_dw_g{g}"))
    gw_parts.append(_matmul(hn_a, dz_a, ta=True, out_dtype=F32, tm=mm, tn=1024, tk=2048, name="a_gate_dw"))
    gw_a_in = jnp.concatenate(gw_parts, axis=1)

    def first_bwd(xt, dhn, dhn_1, dhn_2, dh1t, g):
        dx, dg = _rms_bwd(xt, g, dhn + dhn_1 + dhn_2)
        return dh1t + dx, dg

    grad_x, dg_a_pre = _rowwise(first_bwd, [x2, dhn_a, *dhn_groups, dh1], [w["a_pre_norm"]], [(D_MODEL, F32)],
                                [D_MODEL], tm=tm, name="a_pre_norm_bwd")

    gw_down = jnp.concatenate([gw_down_p[:, :B_KV_LORA], gw_down_p[:, B_KV_LORA + B_NOPE:B_KV_LORA + B_QK_DIM]], axis=1)
    gw_up = jnp.concatenate([gw_upk.reshape(B_KV_LORA, B_HEADS, LANES)[:, :, :B_NOPE],
                             gw_upv.reshape(B_KV_LORA, B_HEADS, B_VDIM)], axis=2).reshape(B_KV_LORA, -1)
    gw_q_up = gw_q_p.reshape(B_Q_LORA, B_HEADS, LANES)[:, :, :B_QK_DIM].reshape(B_Q_LORA, -1)
    grads = {"a_w_in": gw_a_in, "a_w_out": gw_a_out, "kv_w_down": gw_down, "kv_w_up": gw_up,
             "b_w_in": gw_b_in, "b_w_q_up": gw_q_up, "b_w_out": gw_b_out}
    gains = {"a_pre_norm": dg_a_pre, "a_post_norm": dg_a_post, "kv_norm": dg_kv, "kv_latent_norm": dg_latent,
             "b_pre_norm": dg_b_pre, "b_q_norm": dg_q_norm, "b_post_norm": dg_b_post}
    gains = {k: jnp.sum(a, axis=0) for k, a in gains.items()}
    return jnp.sum(loss_acc), grad_x.reshape(bsz, t, D_MODEL), grads, gains


WEIGHT_ORDER = ("a_pre_norm", "a_w_in", "a_w_out", "a_post_norm", "kv_norm", "kv_w_down", "kv_latent_norm",
                "kv_w_up", "b_pre_norm", "b_w_in", "b_q_norm", "b_w_q_up", "b_w_out", "b_post_norm")
MATRICES = (("a_w_in", 1024, 10240, 1), ("a_w_out", 1024, 1024, 0), ("kv_w_down", 1024, 288, 0),
            ("kv_w_up", 256, 2048, 1), ("b_w_in", 1024, 1408, 1), ("b_w_q_up", 384, 1536, 1),
            ("b_w_out", 1024, 1024, 0))
SHARDED_GAINS = ("a_pre_norm", "a_post_norm")
GAIN_WIDTHS = (("a_pre_norm", 1024), ("a_post_norm", 1024), ("kv_norm", 1024), ("kv_latent_norm", 256),
               ("b_pre_norm", 1024), ("b_q_norm", 384), ("b_post_norm", 1024))
GAIN_ROWS = 48


def _shard_rows(rows, cols):
    return rows * cols // (N_DEV * LANES)


def _whole_from_blocks(blocks, rows, cols, axis):
    if axis == 1:
        return blocks.reshape(N_DEV, rows, cols // N_DEV).transpose(1, 0, 2).reshape(rows, cols)
    return blocks.reshape(rows, cols)


def _blocks_from_whole(whole, rows, cols, axis):
    if axis == 1:
        whole = whole.reshape(rows, N_DEV, cols // N_DEV).transpose(1, 0, 2)
    return whole.reshape(N_DEV, -1, LANES)


def kernel(x, positions, a_pre_norm, a_w_in, a_w_out, a_post_norm, kv_norm, kv_w_down, kv_latent_norm, kv_w_up, b_pre_norm, b_w_in, b_q_norm, b_w_q_up, b_w_out, b_post_norm, loss_target, m_a_pre_norm, m_a_w_in, m_a_w_out, m_a_post_norm, m_kv_norm, m_kv_w_down, m_kv_latent_norm, m_kv_w_up, m_b_pre_norm, m_b_w_in, m_b_q_norm, m_b_w_q_up, m_b_w_out, m_b_post_norm, v_a_pre_norm, v_a_w_in, v_a_w_out, v_a_post_norm, v_kv_norm, v_kv_w_down, v_kv_latent_norm, v_kv_w_up, v_b_pre_norm, v_b_w_in, v_b_q_norm, v_b_w_q_up, v_b_w_out, v_b_post_norm):
    weights = dict(a_pre_norm=a_pre_norm, a_w_in=a_w_in, a_w_out=a_w_out, a_post_norm=a_post_norm, kv_norm=kv_norm,
                   kv_w_down=kv_w_down, kv_latent_norm=kv_latent_norm, kv_w_up=kv_w_up, b_pre_norm=b_pre_norm,
                   b_w_in=b_w_in, b_q_norm=b_q_norm, b_w_q_up=b_w_q_up, b_w_out=b_w_out, b_post_norm=b_post_norm)
    m_in = dict(a_pre_norm=m_a_pre_norm, a_w_in=m_a_w_in, a_w_out=m_a_w_out, a_post_norm=m_a_post_norm,
                kv_norm=m_kv_norm, kv_w_down=m_kv_w_down, kv_latent_norm=m_kv_latent_norm, kv_w_up=m_kv_w_up,
                b_pre_norm=m_b_pre_norm, b_w_in=m_b_w_in, b_q_norm=m_b_q_norm, b_w_q_up=m_b_w_q_up,
                b_w_out=m_b_w_out, b_post_norm=m_b_post_norm)
    v_in = dict(a_pre_norm=v_a_pre_norm, a_w_in=v_a_w_in, a_w_out=v_a_w_out, a_post_norm=v_a_post_norm,
                kv_norm=v_kv_norm, kv_w_down=v_kv_w_down, kv_latent_norm=v_kv_latent_norm, kv_w_up=v_kv_w_up,
                b_pre_norm=v_b_pre_norm, b_w_in=v_b_w_in, b_q_norm=v_b_q_norm, b_w_q_up=v_b_w_q_up,
                b_w_out=v_b_w_out, b_post_norm=v_b_post_norm)
    me = 4 * lax.axis_index("x") + 2 * lax.axis_index("y") + lax.axis_index("c")

    wide = MATRICES[0][0]
    flat = jnp.concatenate([weights[name].astype(BF16).reshape(-1, LANES) for name, _, _, _ in MATRICES[1:]], axis=0)
    gathered, w_wide = _all_gather_weights(flat, weights[wide][0].astype(BF16), "gather_weights")
    whole = {wide: w_wide}
    off = 0
    for name, rows, cols, axis in MATRICES[1:]:
        nr = _shard_rows(rows, cols)
        whole[name] = _whole_from_blocks(gathered[:, off:off + nr], rows, cols, axis)
        off += nr
    gain_shard = jnp.concatenate([weights[name].reshape(1, LANES) for name in SHARDED_GAINS]
                                 + [jnp.zeros((8 - len(SHARDED_GAINS), LANES), F32)], axis=0)
    gain_blocks = _all_gather(gain_shard, "gather_gains")
    for i, name in enumerate(SHARDED_GAINS):
        whole[name] = gain_blocks[:, i, :].reshape(1, D_MODEL)
    for name in ("kv_norm", "kv_latent_norm", "b_pre_norm", "b_q_norm", "b_post_norm"):
        whole[name] = weights[name].reshape(1, -1)

    loss_part, grad_x, grads, gains = _local_step(x, positions, loss_target, whole)

    blocks = jnp.concatenate([_blocks_from_whole(grads[name], rows, cols, axis).astype(BF16)
                              for name, rows, cols, axis in MATRICES], axis=1)
    blocks = blocks.reshape(N_CHIPS, 2, blocks.shape[1], LANES)
    got = _exchange_sibling(blocks, "scatter_grads_core")
    core = lax.axis_index("c").astype(jnp.int32).reshape(1)
    landed = _exchange_chips(_add_pairs(blocks, got, core, tr=2512, name="add_core_grads"), "scatter_grads_chip")
    summed = _sum_slots(landed, tr=2512, name="sum_grads")
    grad_out = {}
    off = 0
    for name, rows, cols, axis in MATRICES:
        nr = _shard_rows(rows, cols)
        grad_out[name] = summed[off:off + nr].reshape(weights[name].shape)
        off += nr

    vec = jnp.concatenate([gains[name] for name, _ in GAIN_WIDTHS] + [jnp.full((LANES,), loss_part, F32)])
    vec = jnp.pad(vec, (0, GAIN_ROWS * LANES - vec.shape[0])).reshape(GAIN_ROWS, LANES)
    total = _sum_slots(_all_gather(vec, "gather_gain_grads"), tr=GAIN_ROWS, name="sum_gain_grads").reshape(-1)
    off = 0
    for name, width in GAIN_WIDTHS:
        g = total[off:off + width]
        if name in SHARDED_GAINS:
            g = lax.dynamic_slice(g, (me * LANES,), (LANES,))
        grad_out[name] = g.reshape(weights[name].shape)
        off += width
    loss = total[off]

    deltas, new_m, new_v = {}, {}, {}
    big = "a_w_in"
    deltas[big], new_m[big], new_v[big] = _adamw(weights[big], grad_out[big], m_in[big], v_in[big], "adamw_" + big)
    small = [name for name in WEIGHT_ORDER if name != big]
    res = _adamw_small(*[[d[name] for name in small] for d in (weights, grad_out, m_in, v_in)], "adamw_small")
    for out, vals in zip((deltas, new_m, new_v), res):
        out.update(zip(small, vals))
    return (loss, grad_x, *[grad_out[k] for k in WEIGHT_ORDER], *[deltas[k] for k in WEIGHT_ORDER],
            *[new_m[k] for k in WEIGHT_ORDER], *[new_v[k] for k in WEIGHT_ORDER])
```

```python
import jax
import jax.numpy as jnp
from jax import lax
from jax.experimental import pallas as pl
from jax.experimental.pallas import tpu as pltpu

F32 = jnp.float32
BF16 = jnp.bfloat16

N_DEV = 8
D_MODEL = 1024
NORM_EPS = 1e-6
A_GROUPS = 3
A_DILATIONS = (1, 4, 16)
A_HEADS = 8
A_HEAD_DIM = 128
A_WIDTH = 1024
A_ROT_DIM = 32
A_ROPE_THETA = 500000.0
A_QKV = A_GROUPS * 3 * A_WIDTH
B_HEADS = 16
B_NOPE = 64
B_ROPE = 32
B_QK_DIM = 96
B_VDIM = 64
B_Q_LORA = 384
B_KV_LORA = 256
B_ROPE_THETA = 10000.0
B_KPAD = B_HEADS * 128
ADAM_LR = 0.001
ADAM_B1 = 0.9
ADAM_B2 = 0.999
ADAM_EPS = 1e-08
ADAM_WD = 0.01
ADAM_STEP = 10

LANES = 128
BAND = 128
EPILOGUE_ROWS = 128
NEG = -1e30
VMEM_LIMIT = 56 * 1024 * 1024
MESH = pl.DeviceIdType.MESH


def _cparams(sem):
    return pltpu.CompilerParams(dimension_semantics=sem, vmem_limit_bytes=VMEM_LIMIT)


def _rowwise(fn, rows, bcast, outs, accs=(), *, tm, name):
    n = rows[0].shape[0]
    nr, nb, no = len(rows), len(bcast), len(outs)

    def body(*refs):
        res = fn(*[r[...] for r in refs[:nr + nb]])
        out_refs = refs[nr + nb:nr + nb + no]
        acc_refs = refs[nr + nb + no:]
        for r, v in zip(out_refs, res[:no]):
            r[...] = v.astype(r.dtype)
        if acc_refs:
            @pl.when(pl.program_id(0) == 0)
            def _():
                for r in acc_refs:
                    r[...] = jnp.zeros_like(r)
            for r, v in zip(acc_refs, res[no:]):
                r[...] += v.reshape(tm // 8, 8, v.shape[-1]).sum(axis=0)

    in_specs = [pl.BlockSpec((tm, a.shape[1]), lambda i: (i, 0)) for a in rows]
    in_specs += [pl.BlockSpec(a.shape, lambda i: (0, 0)) for a in bcast]
    out_specs = [pl.BlockSpec((tm, c), lambda i: (i, 0)) for c, _ in outs]
    out_specs += [pl.BlockSpec((8, c), lambda i: (0, 0)) for c in accs]
    out_shape = [jax.ShapeDtypeStruct((n, c), dt) for c, dt in outs]
    out_shape += [jax.ShapeDtypeStruct((8, c), F32) for c in accs]
    return pl.pallas_call(
        body, name=name, grid=(n // tm,), in_specs=in_specs, out_specs=out_specs, out_shape=out_shape,
        compiler_params=_cparams(("arbitrary",)))(*rows, *bcast)


def _matmul(a, b, *, out_dtype, tm, tn, tk=None, name, epilogue=None, extras=(), ta=False, tb=False,
            epilogue_rows=None, b_cols=None):
    epilogue_rows = epilogue_rows or tm
    k, m = a.shape[::-1] if not ta else a.shape
    col0, width = b_cols or (0, b.shape[1])
    n = b.shape[0] if tb else width
    tk = tk or k
    nk = k // tk
    b_off = col0 // (tk if tb else tn)
    assert col0 % (tk if tb else tn) == 0 and (k == width if tb else True)
    ne = len(extras)
    dot = _dot_tn if ta else (_dot_nt if tb else _dot)
    assert epilogue is None or (nk == 1 and not ta)

    def body(*refs):
        a_ref, b_ref = refs[:2]
        ex = refs[2:2 + ne]
        o_ref = refs[2 + ne]
        if epilogue is not None:
            b_tile = b_ref[...].astype(BF16)
            for r0 in range(0, tm, epilogue_rows):
                rows = slice(r0, r0 + epilogue_rows)
                epilogue(dot(a_ref[rows, :].astype(BF16), b_tile), o_ref, rows, *ex)
            return
        part = dot(a_ref[...].astype(BF16), b_ref[...].astype(BF16))
        if nk == 1:
            o_ref[...] = part.astype(o_ref.dtype)
        else:
            acc_ref = refs[-1]
            kk = pl.program_id(2)

            @pl.when(kk == 0)
            def _():
                acc_ref[...] = part

            @pl.when(kk > 0)
            def _():
                acc_ref[...] += part

            @pl.when(kk == nk - 1)
            def _():
                o_ref[...] = acc_ref[...].astype(o_ref.dtype)

    a_spec = pl.BlockSpec((tk, tm), lambda j, i, kk: (kk, i)) if ta else pl.BlockSpec((tm, tk), lambda j, i, kk: (i, kk))
    b_spec = (pl.BlockSpec((tn, tk), lambda j, i, kk: (j, kk + b_off)) if tb
              else pl.BlockSpec((tk, tn), lambda j, i, kk: (kk, j + b_off)))
    in_specs = [a_spec, b_spec] + [pl.BlockSpec(bs, im) for _, bs, im in extras]
    return pl.pallas_call(
        body, name=name, grid=(n // tn, m // tm, nk), in_specs=in_specs,
        out_specs=pl.BlockSpec((tm, tn), lambda j, i, kk: (i, j)),
        out_shape=jax.ShapeDtypeStruct((m, n), out_dtype),
        scratch_shapes=[pltpu.VMEM((tm, tn), F32)] if nk > 1 else [],
        compiler_params=_cparams(("parallel", "parallel", "arbitrary")))(a, b, *[e[0] for e in extras])


def _add_epilogue(acc, o_ref, rows, prev_ref):
    o_ref[rows, :] = (acc + prev_ref[rows, :]).astype(o_ref.dtype)


def _rope(x, c, sp, sm):
    return x * c + pltpu.roll(x, 16, 1) * sp + pltpu.roll(x, LANES - 16, 1) * sm


def _rope_t(dy, c, sp, sm):
    return dy * c + pltpu.roll(dy * sp, LANES - 16, 1) + pltpu.roll(dy * sm, 16, 1)


def _rope_tables(positions, theta, rot_dim, lane0, name):
    n = positions.shape[0]
    half = rot_dim // 2
    inv_freq = 1.0 / (theta ** (jnp.arange(half, dtype=F32) * (2.0 / rot_dim)))
    freq = jnp.concatenate([jnp.zeros((lane0,), F32), inv_freq, inv_freq,
                            jnp.zeros((LANES - lane0 - rot_dim,), F32)]).reshape(1, LANES)

    def fn(p, f):
        ang = p * f
        cos, sin = jnp.cos(ang), jnp.sin(ang)
        lane = lax.broadcasted_iota(jnp.int32, ang.shape, 1) - lane0
        first, second = (lane >= 0) & (lane < half), (lane >= half) & (lane < rot_dim)
        return jnp.where(first | second, cos, 1.0), jnp.where(second, sin, 0.0), jnp.where(first, -sin, 0.0)

    return _rowwise(fn, [positions.astype(F32).reshape(n, 1)], [freq], [(LANES, F32)] * 3, tm=512, name=name)


def _value_heads_t(c_kv, w_vt, bsz, t):
    n, k = c_kv.shape
    tm, tn = 512, 1024

    def body(a_ref, b_ref, o_ref):
        acc = _dot(a_ref[...], b_ref[...])
        lane = lax.broadcasted_iota(jnp.int32, (1, tn), 1)
        acc = acc + jnp.where(lax.rem(lane, LANES) == B_VDIM, 1.0, 0.0)
        o_ref[...] = acc.T.astype(BF16)

    per_seq = t // tm
    return pl.pallas_call(
        body, name="kv_up_v_t", grid=(w_vt.shape[1] // tn, n // tm),
        in_specs=[pl.BlockSpec((tm, k), lambda j, i: (i, 0)), pl.BlockSpec((k, tn), lambda j, i: (0, j))],
        out_specs=pl.BlockSpec((None, tn, tm), lambda j, i: (i // per_seq, j, lax.rem(i, per_seq))),
        out_shape=jax.ShapeDtypeStruct((bsz, w_vt.shape[1], t), BF16),
        compiler_params=_cparams(("parallel", "parallel")))(c_kv, w_vt)


def _rms(x, g):
    xf = x.astype(F32)
    return xf * lax.rsqrt(jnp.mean(xf * xf, axis=-1, keepdims=True) + NORM_EPS) * g


def _rms_bwd(x, g, dy):
    xf = x.astype(F32)
    rstd = lax.rsqrt(jnp.mean(xf * xf, axis=-1, keepdims=True) + NORM_EPS)
    xhat = xf * rstd
    dxhat = dy * g
    dx = rstd * (dxhat - xhat * jnp.mean(dxhat * xhat, axis=-1, keepdims=True))
    return dx, dy * xhat


def _silu(z):
    return z * jax.nn.sigmoid(z)


def _silu_grad(z):
    s = jax.nn.sigmoid(z)
    return s * (1.0 + z * (1.0 - s))


def _dot_nt(a, b):
    return lax.dot_general(a, b, (((1,), (1,)), ((), ())), preferred_element_type=F32)


def _dot_tn(a, b):
    return lax.dot_general(a, b, (((0,), (0,)), ((), ())), preferred_element_type=F32)


def _dot(a, b):
    return jnp.dot(a, b, preferred_element_type=F32)


A_SCALE = A_HEAD_DIM ** -0.5
LOG2E = 1.4426950408889634
A_C2 = A_SCALE * LOG2E
A_HEAD_GROUP = 4


def _attn_a_fwd(qk, v, name):
    s_n, ln, _ = qk.shape
    nb = ln // BAND
    blk = (None, BAND, A_WIDTH)

    def body(q_ref, kc_ref, kp_ref, vc_ref, vp_ref, o_ref, lse_ref):
        kpos = lax.broadcasted_iota(jnp.int32, (2 * BAND, BAND), 0)
        qpos = lax.broadcasted_iota(jnp.int32, (2 * BAND, BAND), 1) + BAND
        first_key = jnp.where(pl.program_id(1) > 0, 0, BAND)
        mask = (kpos <= qpos) & (kpos >= qpos - BAND) & (kpos >= first_key)
        rows = []
        for h0 in range(0, A_HEADS, A_HEAD_GROUP):
            hss = [slice(h * A_HEAD_DIM, (h + 1) * A_HEAD_DIM) for h in range(h0, h0 + A_HEAD_GROUP)]
            sts = [_dot_nt(jnp.concatenate([kp_ref[:, hs], kc_ref[:, hs]], axis=0), q_ref[:, hs]) for hs in hss]
            ps, ls = [], []
            for st in sts:
                st = jnp.where(mask, st * A_C2, NEG)
                m = jnp.max(st, axis=0, keepdims=True)
                p = jnp.exp2(st - m)
                l_row = jnp.sum(p, axis=0, keepdims=True)
                ps.append(p.astype(BF16))
                ls.append(l_row)
                rows.append(m + jnp.log2(l_row))
            ots = [_dot_tn(jnp.concatenate([vp_ref[:, hs], vc_ref[:, hs]], axis=0), p) for hs, p in zip(hss, ps)]
            for hs, o_t, l_row in zip(hss, ots, ls):
                o_ref[:, hs] = (o_t / l_row).T.astype(BF16)
        lse_ref[...] = jnp.concatenate(rows, axis=0)

    def col(c, off):
        return lambda s, l: (s, jnp.maximum(l + off, 0), c)

    return pl.pallas_call(
        body, name=name, grid=(s_n, nb),
        in_specs=[pl.BlockSpec(blk, col(0, 0)), pl.BlockSpec(blk, col(1, 0)), pl.BlockSpec(blk, col(1, -1)),
                  pl.BlockSpec(blk, col(0, 0)), pl.BlockSpec(blk, col(0, -1))],
        out_specs=[pl.BlockSpec(blk, lambda s, l: (s, l, 0)),
                   pl.BlockSpec((None, A_HEADS, BAND), lambda s, l: (s, 0, l))],
        out_shape=[jax.ShapeDtypeStruct((s_n, ln, A_WIDTH), BF16), jax.ShapeDtypeStruct((s_n, A_HEADS, ln), F32)],
        compiler_params=_cparams(("parallel", "arbitrary")))(qk, qk, qk, v, v)


def _attn_a_bwd(qk, v, do, lse2, dsum, tabs, name):
    s_n, ln, _ = qk.shape
    nb = ln // BAND
    blk = (None, BAND, A_WIDTH)

    def body(q_ref, qn_ref, kc_ref, kp_ref, vc_ref, vp_ref, do_ref, don_ref, lse_ref, lsen_ref, ds_ref, dsn_ref,
             c_ref, sp_ref, sm_ref, out_ref):
        l_idx = pl.program_id(1)
        kpos = lax.broadcasted_iota(jnp.int32, (2 * BAND, BAND), 0)
        qpos = lax.broadcasted_iota(jnp.int32, (2 * BAND, BAND), 1) + BAND
        first_key = jnp.where(l_idx > 0, 0, BAND)
        mask_q = (kpos <= qpos) & (kpos >= qpos - BAND) & (kpos >= first_key)
        kpos2 = lax.broadcasted_iota(jnp.int32, (BAND, 2 * BAND), 0)
        qpos2 = lax.broadcasted_iota(jnp.int32, (BAND, 2 * BAND), 1)
        last_query = jnp.where(l_idx < nb - 1, 2 * BAND, BAND)
        mask_k = (kpos2 <= qpos2) & (kpos2 >= qpos2 - BAND) & (qpos2 < last_query)
        c, sp, sm = c_ref[...], sp_ref[...], sm_ref[...]
        lse_q, ds_q = lse_ref[...], ds_ref[...]
        lse_k = jnp.concatenate([lse_q, lsen_ref[...]], axis=1)
        ds_k = jnp.concatenate([ds_q, dsn_ref[...]], axis=1)
        for h0 in range(0, A_HEADS, A_HEAD_GROUP):
            hl = list(range(h0, h0 + A_HEAD_GROUP))
            hss = [slice(h * A_HEAD_DIM, (h + 1) * A_HEAD_DIM) for h in hl]
            k2s = [jnp.concatenate([kp_ref[:, hs], kc_ref[:, hs]], axis=0) for hs in hss]
            v2s = [jnp.concatenate([vp_ref[:, hs], vc_ref[:, hs]], axis=0) for hs in hss]
            q2s = [jnp.concatenate([q_ref[:, hs], qn_ref[:, hs]], axis=0) for hs in hss]
            do2s = [jnp.concatenate([do_ref[:, hs], don_ref[:, hs]], axis=0) for hs in hss]
            sts = [_dot_nt(k2, q_ref[:, hs]) for k2, hs in zip(k2s, hss)]
            dpts = [_dot_nt(v2, do_ref[:, hs]) for v2, hs in zip(v2s, hss)]
            st2s = [_dot_nt(kc_ref[:, hs], q2) for q2, hs in zip(q2s, hss)]
            dpt2s = [_dot_nt(vc_ref[:, hs], do2) for do2, hs in zip(do2s, hss)]
            dsts, dst2s, p2s = [], [], []
            for i, h in enumerate(hl):
                p = jnp.exp2(jnp.where(mask_q, sts[i] * A_C2, NEG) - lse_q[h:h + 1])
                dsts.append((p * (dpts[i] - ds_q[h:h + 1]) * A_SCALE).astype(BF16))
                p2 = jnp.exp2(jnp.where(mask_k, st2s[i] * A_C2, NEG) - lse_k[h:h + 1])
                dst2s.append((p2 * (dpt2s[i] - ds_k[h:h + 1]) * A_SCALE).astype(BF16))
                p2s.append(p2.astype(BF16))
            dqs = [_dot_tn(dsts[i], k2s[i]) for i in range(A_HEAD_GROUP)]
            dks = [_dot(dst2s[i], q2s[i]) for i in range(A_HEAD_GROUP)]
            dvs = [_dot(p2s[i], do2s[i]) for i in range(A_HEAD_GROUP)]
            for i, h in enumerate(hl):
                out_ref[:, hss[i]] = _rope_t(dqs[i], c, sp, sm).astype(BF16)
                out_ref[:, A_WIDTH + h * A_HEAD_DIM:A_WIDTH + (h + 1) * A_HEAD_DIM] = _rope_t(dks[i], c, sp, sm).astype(BF16)
                out_ref[:, 2 * A_WIDTH + h * A_HEAD_DIM:2 * A_WIDTH + (h + 1) * A_HEAD_DIM] = dvs[i].astype(BF16)

    def col(c, off):
        return lambda s, l: (s, jnp.clip(l + off, 0, nb - 1), c)

    def row(off):
        return pl.BlockSpec((None, A_HEADS, BAND), lambda s, l: (s, 0, jnp.clip(l + off, 0, nb - 1)))

    tspec = pl.BlockSpec((None, BAND, LANES), lambda s, l: (s, l, 0))
    in_specs = [pl.BlockSpec(blk, col(0, 0)), pl.BlockSpec(blk, col(0, 1)),
                pl.BlockSpec(blk, col(1, 0)), pl.BlockSpec(blk, col(1, -1)),
                pl.BlockSpec(blk, col(0, 0)), pl.BlockSpec(blk, col(0, -1)),
                pl.BlockSpec(blk, col(0, 0)), pl.BlockSpec(blk, col(0, 1)),
                row(0), row(1), row(0), row(1), tspec, tspec, tspec]
    return pl.pallas_call(
        body, name=name, grid=(s_n, nb), in_specs=in_specs,
        out_specs=pl.BlockSpec((None, BAND, 3 * A_WIDTH), lambda s, l: (s, l, 0)),
        out_shape=jax.ShapeDtypeStruct((s_n, ln, 3 * A_WIDTH), BF16),
        compiler_params=_cparams(("parallel", "arbitrary")))(
            qk, qk, qk, qk, v, v, do, do, lse2, lse2, dsum, dsum, *tabs)


B_TQ = 256
B_SCALE = B_QK_DIM ** -0.5
B_C2 = B_SCALE * LOG2E


def _b_row_tile(tq):
    return pl.BlockSpec((None, None, 8, tq), lambda b, j, i: (b, j, 0, i))


def _b_specs(t, tq, tk):
    pair_tile = pl.BlockSpec((None, tq, 2 * LANES), lambda b, j, i: (b, i, j))
    pair_full = pl.BlockSpec((None, t, 2 * LANES), lambda b, j, i: (b, 0, j))
    one_tile = pl.BlockSpec((None, tk, LANES), lambda b, j, i: (b, i, j))
    one_full = pl.BlockSpec((None, t, LANES), lambda b, j, i: (b, 0, j))
    tab_tile = pl.BlockSpec((None, tq, LANES), lambda b, j, i: (b, i, 0))
    return pair_tile, pair_full, one_tile, one_full, tab_tile


def _key_le_query(kb, qb, tk, tq):
    kpos = kb * tk + lax.broadcasted_iota(jnp.int32, (tk, tq), 0)
    qpos = qb * tq + lax.broadcasted_iota(jnp.int32, (tk, tq), 1)
    return kpos <= qpos


def _attn_b_fwd(q, kpad, vt1):
    bsz, t, _ = q.shape
    tq = tk = B_TQ
    nq = t // tq

    def body(q_ref, k_ref, vt_ref, o_ref, lse_ref):
        qblk = pl.program_id(2)
        qs = [q_ref[:, hh * LANES:(hh + 1) * LANES] for hh in range(2)]

        def scores(kb):
            start = pl.multiple_of(kb * tk, tk)
            return [_dot_nt(k_ref[pl.ds(start, tk), hh * LANES:(hh + 1) * LANES], qs[hh]) for hh in range(2)]

        def pv(kb, ps):
            start = pl.multiple_of(kb * tk, tk)
            return [_dot(vt_ref[hh * LANES:(hh + 1) * LANES, pl.ds(start, tk)], ps[hh]) for hh in range(2)]

        def softmax(ss, ms, accs, kb, masked):
            out_m, out_acc, out_p = [], [], []
            for hh in range(2):
                s = ss[hh] * B_C2
                if masked:
                    s = jnp.where(_key_le_query(kb, qblk, tk, tq), s, NEG)
                m_new = jnp.maximum(ms[hh], jnp.max(s, axis=0, keepdims=True))
                out_acc.append(jnp.exp2(ms[hh] - m_new) * accs[hh])
                out_p.append(jnp.exp2(s - m_new).astype(BF16))
                out_m.append(m_new)
            return out_m, out_acc, out_p

        def step(kb, carry):
            ss, ps, ms, accs = carry
            pvs = pv(jnp.maximum(kb - 1, 0), ps)
            ss_next = scores(kb + 1)
            accs = [accs[hh] + pvs[hh] for hh in range(2)]
            ms, accs, ps = softmax(ss, ms, accs, kb, False)
            return (ss_next, ps, ms, accs)

        init = (scores(0), [jnp.zeros((tk, tq), BF16)] * 2, [jnp.full((1, tq), NEG, F32)] * 2,
                [jnp.zeros((LANES, tq), F32)] * 2)
        ss, ps, ms, accs = lax.fori_loop(0, qblk, step, init)
        pvs = pv(jnp.maximum(qblk - 1, 0), ps)
        accs = [accs[hh] + pvs[hh] for hh in range(2)]
        ms, accs, ps = softmax(ss, ms, accs, qblk, True)
        pvs = pv(qblk, ps)
        accs = [accs[hh] + pvs[hh] for hh in range(2)]
        ls = [accs[hh][B_VDIM:B_VDIM + 1] for hh in range(2)]
        o_t = jnp.concatenate([accs[0][:B_VDIM] / ls[0], accs[1][:B_VDIM] / ls[1]], axis=0)
        o_ref[...] = o_t.T
        lse_ref[...] = jnp.concatenate([ms[0] + jnp.log2(ls[0]), ms[1] + jnp.log2(ls[1]),
                                        jnp.zeros((6, tq), F32)], axis=0)

    pair_tile, pair_full, one_tile, _, _ = _b_specs(t, tq, tk)
    vt_spec = pl.BlockSpec((None, 2 * LANES, t), lambda b, j, i: (b, j, 0))
    return pl.pallas_call(
        body, name="attn_b_fwd", grid=(bsz, B_HEADS // 2, nq),
        in_specs=[pair_tile, pair_full, vt_spec],
        out_specs=[one_tile, _b_row_tile(tq)],
        out_shape=[jax.ShapeDtypeStruct((bsz, t, B_HEADS * B_VDIM), F32),
                   jax.ShapeDtypeStruct((bsz, B_HEADS // 2, 8, t), F32)],
        compiler_params=_cparams(("parallel", "parallel", "arbitrary")))(q, kpad, vt1)


def _attn_b_bwd(q, kpad, v, do, o, lse2, tabs):
    bsz, t, _ = q.shape
    tq = tk = B_TQ
    nq = t // tq

    def body(q_ref, k_ref, v_ref, do_ref, o_ref, lse_ref, c_ref, sp_ref, sm_ref, dq_ref, dk_ref, dv_ref,
             dqt_scr, dsum_scr):
        lane = lax.broadcasted_iota(jnp.int32, (tk, LANES), 1)
        sel_lane = lax.broadcasted_iota(jnp.int32, (8, LANES), 1)
        sel_row = lax.broadcasted_iota(jnp.int32, (8, LANES), 0)
        sel = jnp.where((sel_lane < B_VDIM) == (sel_row == 0), 1.0, 0.0)
        sel = jnp.where(sel_row < 2, sel, 0.0).astype(BF16)

        def rows(blk):
            return pl.ds(pl.multiple_of(blk * tq, tq), tq)

        def dsum_step(qb, carry):
            prod = do_ref[rows(qb), :].astype(F32) * o_ref[rows(qb), :]
            hi = prod.astype(BF16)
            lo = (prod - hi.astype(F32)).astype(BF16)
            dsum_scr[:, rows(qb)] = _dot_nt(sel, hi) + _dot_nt(sel, lo)
            return carry

        lax.fori_loop(0, nq, dsum_step, 0)
        dqt_scr[...] = jnp.zeros_like(dqt_scr)

        def kv_step(kb, carry):
            ks = [k_ref[rows(kb), hh * LANES:(hh + 1) * LANES] for hh in range(2)]
            vb = v_ref[rows(kb), :]
            zero = jnp.zeros_like(vb)
            vs = [jnp.where(lane < B_VDIM, vb, zero), jnp.where(lane < B_VDIM, zero, vb)]

            def make_step(masked):
                def step(qb, acc):
                    qs = [q_ref[rows(qb), hh * LANES:(hh + 1) * LANES] for hh in range(2)]
                    do_b = do_ref[rows(qb), :]
                    ss = [_dot_nt(ks[hh], qs[hh]) for hh in range(2)]
                    dps = [_dot_nt(vs[hh], do_b) for hh in range(2)]
                    pbs, dss = [], []
                    for hh in range(2):
                        s = ss[hh] * B_C2
                        if masked:
                            s = jnp.where(_key_le_query(kb, qb, tk, tq), s, NEG)
                        p = jnp.exp2(s - lse_ref[hh:hh + 1, rows(qb)])
                        dss.append((p * (dps[hh] - dsum_scr[hh:hh + 1, rows(qb)]) * B_SCALE).astype(BF16))
                        pbs.append(p.astype(BF16))
                    for hh in range(2):
                        dqt_scr[hh, :, rows(qb)] += _dot_tn(ks[hh], dss[hh])
                    return (acc[0] + _dot(dss[0], qs[0]), acc[1] + _dot(dss[1], qs[1]),
                            acc[2] + _dot(pbs[0], do_b), acc[3] + _dot(pbs[1], do_b))
                return step

            acc = make_step(True)(kb, (jnp.zeros((tk, LANES), F32),) * 4)
            acc = lax.fori_loop(kb + 1, nq, make_step(False), acc)
            dk_ref[rows(kb), :LANES] = acc[0].astype(BF16)
            dk_ref[rows(kb), LANES:] = acc[1].astype(BF16)
            dv_ref[rows(kb), :] = jnp.where(lane < B_VDIM, acc[2], acc[3]).astype(BF16)
            return carry

        lax.fori_loop(0, nq, kv_step, 0)

        def dq_step(qb, carry):
            c, sp, sm = c_ref[rows(qb), :], sp_ref[rows(qb), :], sm_ref[rows(qb), :]
            for hh in range(2):
                dq_ref[rows(qb), hh * LANES:(hh + 1) * LANES] = _rope_t(dqt_scr[hh, :, rows(qb)].T, c, sp, sm).astype(BF16)
            return carry

        lax.fori_loop(0, nq, dq_step, 0)

    pair_full = pl.BlockSpec((None, t, 2 * LANES), lambda b, j: (b, 0, j))
    one_full = pl.BlockSpec((None, t, LANES), lambda b, j: (b, 0, j))
    row_full = pl.BlockSpec((None, None, 8, t), lambda b, j: (b, j, 0, 0))
    tab_full = pl.BlockSpec((None, t, LANES), lambda b, j: (b, 0, 0))
    return pl.pallas_call(
        body, name="attn_b_bwd", grid=(bsz, B_HEADS // 2),
        in_specs=[pair_full, pair_full, one_full, one_full, one_full, row_full, tab_full, tab_full, tab_full],
        out_specs=[pair_full, pair_full, one_full],
        out_shape=[jax.ShapeDtypeStruct((bsz, t, B_KPAD), BF16), jax.ShapeDtypeStruct((bsz, t, B_KPAD), BF16),
                   jax.ShapeDtypeStruct((bsz, t, B_HEADS * B_VDIM), BF16)],
        scratch_shapes=[pltpu.VMEM((2, LANES, t), F32), pltpu.VMEM((8, t), F32)],
        compiler_params=_cparams(("parallel", "parallel")))(q, kpad, v, do, o, lse2, *tabs)


ANY = pl.BlockSpec(memory_space=pl.ANY)


def _all_gather(shard, name):
    def body(x_ref, out_ref, send_sems, recv_sems, local_sem):
        x, y, c = lax.axis_index("x"), lax.axis_index("y"), lax.axis_index("c")
        me, sibling = (x, y, c), (x, y, 1 - c)
        chips = [(1 - x, y), (x, 1 - y), (1 - x, 1 - y)]

        def rows(px, py, pc):
            return out_ref.at[4 * px + 2 * py + pc]

        def copy(k, block, to, src=None):
            return pltpu.make_async_remote_copy(
                src_ref=rows(*block) if src is None else src, dst_ref=rows(*block),
                send_sem=send_sems.at[k], recv_sem=recv_sems.at[k], device_id=to, device_id_type=MESH)

        mine = pltpu.make_async_copy(x_ref, rows(*me), local_sem)
        mine.start()
        first = [copy(0, me, sibling, src=x_ref)]
        first += [copy(1 + j, me, (*chip, c), src=x_ref) for j, chip in enumerate(chips)]
        for cp in first:
            cp.start()
        passed = [copy(4 + j, (*chip, c), sibling) for j, chip in enumerate(chips)]
        for j, chip in enumerate(chips):
            copy(1 + j, (*chip, c), me).wait_recv()
            passed[j].start()
        copy(0, sibling, me).wait_recv()
        for j, chip in enumerate(chips):
            copy(4 + j, (*chip, 1 - c), me).wait_recv()
        for cp in first + passed:
            cp.wait_send()
        mine.wait()

    return pl.pallas_call(
        body, name=name, in_specs=[ANY], out_specs=ANY,
        out_shape=jax.ShapeDtypeStruct((N_DEV,) + shard.shape, shard.dtype),
        scratch_shapes=[pltpu.SemaphoreType.DMA((7,)), pltpu.SemaphoreType.DMA((7,)), pltpu.SemaphoreType.DMA])(shard)


def _all_gather_weights(flat, wide, name):
    ns = wide.shape[1]

    def body(f_ref, w_ref, fo_ref, wo_ref, send_sems, recv_sems, local_sems):
        x, y, c = lax.axis_index("x"), lax.axis_index("y"), lax.axis_index("c")
        me, sibling = (x, y, c), (x, y, 1 - c)
        chips = [(1 - x, y), (x, 1 - y), (1 - x, 1 - y)]

        def place(a, px, py, pc):
            idx = 4 * px + 2 * py + pc
            if a == 0:
                return fo_ref.at[idx]
            return wo_ref.at[:, pl.ds(pl.multiple_of(idx * ns, LANES), ns)]

        def copy(a, k, block, to, src=None):
            return pltpu.make_async_remote_copy(
                src_ref=place(a, *block) if src is None else src, dst_ref=place(a, *block),
                send_sem=send_sems.at[a, k], recv_sem=recv_sems.at[a, k], device_id=to, device_id_type=MESH)

        own = (f_ref, w_ref)
        mine = [pltpu.make_async_copy(own[a], place(a, *me), local_sems.at[a]) for a in range(2)]
        first = []
        for a in range(2):
            mine[a].start()
            first.append(copy(a, 0, me, sibling, src=own[a]))
            first += [copy(a, 1 + j, me, (*chip, c), src=own[a]) for j, chip in enumerate(chips)]
        for cp in first:
            cp.start()
        passed = [[copy(a, 4 + j, (*chip, c), sibling) for j, chip in enumerate(chips)] for a in range(2)]
        for a in range(2):
            for j, chip in enumerate(chips):
                copy(a, 1 + j, (*chip, c), me).wait_recv()
                passed[a][j].start()
        for a in range(2):
            copy(a, 0, sibling, me).wait_recv()
            for j, chip in enumerate(chips):
                copy(a, 4 + j, (*chip, 1 - c), me).wait_recv()
        for cp in first + passed[0] + passed[1]:
            cp.wait_send()
        for cp in mine:
            cp.wait()

    return pl.pallas_call(
        body, name=name, in_specs=[ANY, ANY], out_specs=[ANY, ANY],
        out_shape=[jax.ShapeDtypeStruct((N_DEV,) + flat.shape, flat.dtype),
                   jax.ShapeDtypeStruct((wide.shape[0], N_DEV * ns), wide.dtype)],
        scratch_shapes=[pltpu.SemaphoreType.DMA((2, 7)), pltpu.SemaphoreType.DMA((2, 7)),
                        pltpu.SemaphoreType.DMA((2,))])(flat, wide)


N_CHIPS = 4


def _exchange_sibling(blocks, name):
    def body(g_ref, got_ref, send_sems, recv_sems):
        x, y, c = lax.axis_index("x"), lax.axis_index("y"), lax.axis_index("c")
        sends = [pltpu.make_async_remote_copy(
            src_ref=g_ref.at[q, 1 - c], dst_ref=got_ref.at[q], send_sem=send_sems.at[q],
            recv_sem=recv_sems.at[q], device_id=(x, y, 1 - c), device_id_type=MESH) for q in range(N_CHIPS)]
        for cp in sends:
            cp.start()
        for cp in sends:
            cp.wait_recv()
        for cp in sends:
            cp.wait_send()

    return pl.pallas_call(
        body, name=name, in_specs=[ANY], out_specs=ANY,
        out_shape=jax.ShapeDtypeStruct((N_CHIPS,) + blocks.shape[2:], blocks.dtype),
        scratch_shapes=[pltpu.SemaphoreType.DMA((N_CHIPS,)), pltpu.SemaphoreType.DMA((N_CHIPS,))])(blocks)


def _exchange_chips(parts, name):
    def body(p_ref, out_ref, send_sems, recv_sems, local_sem):
        x, y, c = lax.axis_index("x"), lax.axis_index("y"), lax.axis_index("c")
        me = 2 * x + y

        def peer(k):
            return (1 - x if k & 2 else x, 1 - y if k & 1 else y)

        def copy(k):
            px, py = peer(k)
            return pltpu.make_async_remote_copy(
                src_ref=p_ref.at[2 * px + py], dst_ref=out_ref.at[me], send_sem=send_sems.at[k - 1],
                recv_sem=recv_sems.at[k - 1], device_id=(px, py, c), device_id_type=MESH)

        def arrival(k):
            px, py = peer(k)
            slot = out_ref.at[2 * px + py]
            return pltpu.make_async_remote_copy(
                src_ref=slot, dst_ref=slot, send_sem=send_sems.at[k - 1], recv_sem=recv_sems.at[k - 1],
                device_id=(px, py, c), device_id_type=MESH)

        mine = pltpu.make_async_copy(p_ref.at[me], out_ref.at[me], local_sem)
        mine.start()
        sends = [copy(k) for k in range(1, N_CHIPS)]
        for cp in sends:
            cp.start()
        for k in range(1, N_CHIPS):
            arrival(k).wait_recv()
        for cp in sends:
            cp.wait_send()
        mine.wait()

    return pl.pallas_call(
        body, name=name, in_specs=[ANY], out_specs=ANY,
        out_shape=jax.ShapeDtypeStruct(parts.shape, parts.dtype),
        scratch_shapes=[pltpu.SemaphoreType.DMA((N_CHIPS - 1,)), pltpu.SemaphoreType.DMA((N_CHIPS - 1,)),
                        pltpu.SemaphoreType.DMA])(parts)


def _add_pairs(blocks, got, core, *, tr, name):
    q, r, c = got.shape

    def body(core_ref, a_ref, b_ref, o_ref):
        o_ref[...] = (a_ref[...].astype(F32) + b_ref[...].astype(F32)).astype(o_ref.dtype)

    spec = pl.BlockSpec((q, tr, c), lambda i, core_ref: (0, i, 0))
    mine = pl.BlockSpec((q, None, tr, c), lambda i, core_ref: (0, core_ref[0], i, 0))
    return pl.pallas_call(
        body, name=name,
        grid_spec=pltpu.PrefetchScalarGridSpec(num_scalar_prefetch=1, grid=(r // tr,), in_specs=[mine, spec],
                                               out_specs=spec),
        out_shape=jax.ShapeDtypeStruct(got.shape, BF16), compiler_params=_cparams(("parallel",)))(
            core, blocks, got)


def _sum_slots(slots, *, tr, name):
    n_slots, r, c = slots.shape

    def body(s_ref, o_ref):
        acc = s_ref[0].astype(F32)
        for s in range(1, n_slots):
            acc = acc + s_ref[s].astype(F32)
        o_ref[...] = acc

    return pl.pallas_call(
        body, name=name, grid=(r // tr,),
        in_specs=[pl.BlockSpec((n_slots, tr, c), lambda i: (0, i, 0))],
        out_specs=pl.BlockSpec((tr, c), lambda i: (i, 0)),
        out_shape=jax.ShapeDtypeStruct((r, c), F32),
        compiler_params=_cparams(("parallel",)))(slots)


def _adamw_math(w_t, g_t, m_t, v_t):
    m_n = ADAM_B1 * m_t + (1.0 - ADAM_B1) * g_t
    v_n = ADAM_B2 * v_t + (1.0 - ADAM_B2) * (g_t * g_t)
    m_hat = m_n / (1.0 - ADAM_B1 ** ADAM_STEP)
    v_hat = v_n / (1.0 - ADAM_B2 ** ADAM_STEP)
    delta = -ADAM_LR * (m_hat / (jnp.sqrt(v_hat) + ADAM_EPS) + ADAM_WD * w_t)
    return delta, m_n, v_n


def _adamw(w, g, m, v, name):
    shape = w.shape
    cols = shape[-1]
    args = [a.reshape(-1, cols) for a in (w, g, m, v)]
    rows = args[0].shape[0]
    tm = 256 if rows % 256 == 0 else rows
    delta, m_n, v_n = _rowwise(_adamw_math, args, [], [(cols, F32)] * 3, tm=tm, name=name)
    return delta.reshape(shape), m_n.reshape(shape), v_n.reshape(shape)


def _adamw_small(ws, gs, ms, vs, name):
    k = len(ws)
    args = [a.reshape(-1, a.shape[-1]) for group in (ws, gs, ms, vs) for a in group]

    def body(*refs):
        ins, outs = refs[:4 * k], refs[4 * k:]
        for i in range(k):
            res = _adamw_math(*[ins[j * k + i][...] for j in range(4)])
            for j in range(3):
                outs[j * k + i][...] = res[j]

    res = pl.pallas_call(
        body, name=name, out_shape=[jax.ShapeDtypeStruct(a.shape, F32) for a in args[:k]] * 3,
        compiler_params=pltpu.CompilerParams(vmem_limit_bytes=VMEM_LIMIT))(*args)
    return [[res[j * k + i].reshape(ws[i].shape) for i in range(k)] for j in range(3)]


def _local_step(x, positions, target, w):
    bsz, t, _ = x.shape
    n = bsz * t
    tm = 256
    mm = 512
    x2 = x.reshape(n, D_MODEL)
    tgt2 = target.reshape(n, D_MODEL)
    pos = positions.reshape(n)
    tb2 = _rope_tables(pos, B_ROPE_THETA, B_ROPE, B_NOPE, "rope_tables_b")
    tabs_b = [a.reshape(bsz, t, LANES) for a in tb2]

    w_a_in = w["a_w_in"]
    w_a_out = w["a_w_out"]
    w_down = w["kv_w_down"]
    w_down_p = jnp.zeros((D_MODEL, 3 * LANES), BF16).at[:, :B_KV_LORA].set(w_down[:, :B_KV_LORA])
    w_down_p = w_down_p.at[:, B_KV_LORA + B_NOPE:B_KV_LORA + B_QK_DIM].set(w_down[:, B_KV_LORA:])
    wu = w["kv_w_up"].reshape(B_KV_LORA, B_HEADS, B_NOPE + B_VDIM)
    w_upk = jnp.pad(wu[:, :, :B_NOPE], ((0, 0), (0, 0), (0, LANES - B_NOPE))).reshape(B_KV_LORA, B_KPAD)
    w_upv = wu[:, :, B_NOPE:].reshape(B_KV_LORA, B_HEADS * B_VDIM)
    w_b_in = w["b_w_in"]
    w_q_p = jnp.pad(w["b_w_q_up"].reshape(B_Q_LORA, B_HEADS, B_QK_DIM),
                    ((0, 0), (0, 0), (0, LANES - B_QK_DIM))).reshape(B_Q_LORA, B_KPAD)
    w_b_out = w["b_w_out"]

    def tab_extras(tabs2, rows):
        return [(a, (rows, LANES), lambda j, i, kk: (i, 0)) for a in tabs2]

    (hn_a,) = _rowwise(lambda xt, g: (_rms(xt, g),), [x2], [w["a_pre_norm"]], [(D_MODEL, BF16)],
                       tm=tm, name="a_pre_norm")

    def rope_epilogue(acc, o_ref, rows, c_ref, sp_ref, sm_ref):
        c, sp, sm = c_ref[rows, :], sp_ref[rows, :], sm_ref[rows, :]
        for h in range(acc.shape[1] // LANES):
            hs = slice(h * LANES, (h + 1) * LANES)
            o_ref[rows, hs] = _rope(acc[:, hs], c, sp, sm).astype(BF16)

    def to_group(a, d):
        if d == 1:
            return a
        return a.reshape(bsz, t // d, d, a.shape[-1]).transpose(0, 2, 1, 3).reshape(n, a.shape[-1])

    def from_group(a, d):
        if d == 1:
            return a
        return a.reshape(bsz, d, t // d, a.shape[-1]).transpose(0, 2, 1, 3).reshape(n, a.shape[-1])

    def rows_to_cols(r, d):
        return r.reshape(bsz, d, A_HEADS, t // d).transpose(0, 3, 1, 2).reshape(n, A_HEADS)

    def cols_to_rows(cc, d):
        return cc.reshape(bsz, t // d, d, A_HEADS).transpose(0, 2, 3, 1).reshape(bsz * d, A_HEADS, t // d)

    z_a = _matmul(hn_a, w_a_in, b_cols=(A_QKV, A_WIDTH), out_dtype=F32, tm=mm, tn=A_WIDTH, name="a_gate")
    hn_g, tabs_g, qk_g, v_g, o_g, lse_g = [], [], [], [], [], []
    for g, d in enumerate(A_DILATIONS):
        hn_g.append(to_group(hn_a, d))
        tabs_g.append(_rope_tables(to_group(pos.reshape(n, 1), d).reshape(n), A_ROPE_THETA, A_ROT_DIM, 0,
                                   f"rope_tables_a_g{g}"))
        col0 = g * 3 * A_WIDTH
        qk = _matmul(hn_g[g], w_a_in, b_cols=(col0, 2 * A_WIDTH), out_dtype=BF16, tm=mm, tn=A_WIDTH,
                     name=f"a_qk_g{g}", epilogue=rope_epilogue, epilogue_rows=EPILOGUE_ROWS,
                     extras=tab_extras(tabs_g[g], mm))
        v = _matmul(hn_g[g], w_a_in, b_cols=(col0 + 2 * A_WIDTH, A_WIDTH), out_dtype=BF16, tm=mm, tn=A_WIDTH,
                    name=f"a_v_g{g}")
        qk_g.append(qk.reshape(bsz * d, t // d, 2 * A_WIDTH))
        v_g.append(v.reshape(bsz * d, t // d, A_WIDTH))
        o, lse = _attn_a_fwd(qk_g[g], v_g[g], f"attn_a_fwd_g{g}")
        o_g.append(from_group(o.reshape(n, A_WIDTH), d))
        lse_g.append(rows_to_cols(lse, d))

    def merge_fn(o0, o1, o2, l0, l1, l2, z):
        lmax = jnp.maximum(jnp.maximum(l0, l1), l2)
        e0, e1, e2 = jnp.exp2(l0 - lmax), jnp.exp2(l1 - lmax), jnp.exp2(l2 - lmax)
        den = e0 + e1 + e2
        w0, w1, w2 = e0 / den, e1 / den, e2 / den
        parts = []
        for h in range(A_HEADS):
            hs = slice(h * A_HEAD_DIM, (h + 1) * A_HEAD_DIM)
            parts.append(w0[:, h:h + 1] * o0[:, hs] + w1[:, h:h + 1] * o1[:, hs] + w2[:, h:h + 1] * o2[:, hs])
        o = jnp.concatenate(parts, axis=1)
        return o * _silu(z), o, lmax + jnp.log2(den)

    y_a, o_a2, lse_a = _rowwise(merge_fn, [*o_g, *lse_g, z_a], [],
                                [(A_WIDTH, BF16), (A_WIDTH, F32), (A_HEADS, F32)], tm=tm, name="a_merge_gate")
    out_a = _matmul(y_a, w_a_out, out_dtype=F32, tm=mm, tn=D_MODEL, name="a_out")

    def mid_fn(xt, out, g_post, g_kv, g_b):
        h1 = xt + _rms(out, g_post)
        return h1, _rms(h1, g_kv), _rms(h1, g_b)

    h1, hn_kv, hn_b = _rowwise(mid_fn, [x2, out_a], [w["a_post_norm"], w["kv_norm"], w["b_pre_norm"]],
                               [(D_MODEL, F32), (D_MODEL, BF16), (D_MODEL, BF16)], tm=tm, name="a_post_norm")

    ckr = _matmul(hn_kv, w_down_p, out_dtype=F32, tm=mm, tn=3 * LANES, name="kv_down")

    def latent_fn(ck, c, sp, sm, g):
        return _rms(ck[:, :B_KV_LORA], g), _rope(ck[:, B_KV_LORA:], c, sp, sm)

    c_kv, k_rope = _rowwise(latent_fn, [ckr, *tb2], [w["kv_latent_norm"]], [(B_KV_LORA, BF16), (LANES, F32)],
                            tm=tm, name="kv_latent_norm")

    def kpad_epilogue(acc, o_ref, rows, kr_ref):
        kr = kr_ref[rows, :]
        for h in range(acc.shape[1] // LANES):
            hs = slice(h * LANES, (h + 1) * LANES)
            o_ref[rows, hs] = (acc[:, hs] + kr).astype(BF16)

    kpad = _matmul(c_kv, w_upk, out_dtype=BF16, tm=mm, tn=1024, name="kv_up_k", epilogue=kpad_epilogue,
                   epilogue_rows=EPILOGUE_ROWS,
                   extras=[(k_rope, (mm, LANES), lambda j, i, kk: (i, 0))])
    v_b = _matmul(c_kv, w_upv, out_dtype=BF16, tm=mm, tn=1024, name="kv_up_v")

    proj_b = _matmul(hn_b, w_b_in, out_dtype=F32, tm=mm, tn=w_b_in.shape[1], name="b_in")
    (c_q,) = _rowwise(lambda p, g: (_rms(p[:, :B_Q_LORA], g),), [proj_b], [w["b_q_norm"]], [(B_Q_LORA, BF16)],
                      tm=tm, name="b_q_norm")

    q_b = _matmul(c_q, w_q_p, out_dtype=BF16, tm=mm, tn=1024, name="b_q_up", epilogue=rope_epilogue,
                  epilogue_rows=EPILOGUE_ROWS,
                  extras=tab_extras(tb2, mm))
    q_b3, kpad3, v_b3 = q_b.reshape(bsz, t, B_KPAD), kpad.reshape(bsz, t, B_KPAD), v_b.reshape(bsz, t, -1)
    w_vt = jnp.pad(wu[:, :, B_NOPE:], ((0, 0), (0, 0), (0, LANES - B_VDIM))).reshape(B_KV_LORA, B_KPAD)
    vt1 = _value_heads_t(c_kv, w_vt, bsz, t)
    o_b, lse_b = _attn_b_fwd(q_b3, kpad3, vt1)
    o_b2 = o_b.reshape(n, -1)
    (y_b,) = _rowwise(lambda o, p: (o * _silu(p[:, B_Q_LORA:]),), [o_b2, proj_b], [], [(D_MODEL, BF16)],
                      tm=tm, name="b_gate_mul")
    out_b = _matmul(y_b, w_b_out, out_dtype=F32, tm=mm, tn=D_MODEL, name="b_out")

    def head_fn(h1t, out, tgt, g):
        e = h1t + _rms(out, g) - tgt
        loss_row = 0.5 * jnp.mean(e * e, axis=-1, keepdims=True)
        dh2 = e * (1.0 / D_MODEL)
        d_out, dg = _rms_bwd(out, g, dh2)
        return dh2, d_out, dg, jnp.broadcast_to(loss_row * (1.0 / LANES), (loss_row.shape[0], LANES))

    dh2, d_out_b, dg_b_post, loss_acc = _rowwise(
        head_fn, [h1, out_b, tgt2], [w["b_post_norm"]], [(D_MODEL, F32), (D_MODEL, BF16)], [D_MODEL, LANES],
        tm=tm, name="loss_head")

    dy_b = _matmul(d_out_b, w_b_out, tb=True, out_dtype=BF16, tm=mm, tn=D_MODEL, name="b_out_dx")
    gw_b_out = _matmul(y_b, d_out_b, ta=True, out_dtype=BF16, tm=mm, tn=D_MODEL, tk=2048, name="b_out_dw")

    def gate_b_bwd(dy, o, p):
        z = p[:, B_Q_LORA:]
        return dy * _silu(z), dy * o * _silu_grad(z)

    do_b, dz_b = _rowwise(gate_b_bwd, [dy_b, o_b2, proj_b], [], [(D_MODEL, BF16), (D_MODEL, F32)],
                          tm=tm, name="b_gate_bwd")
    do_b3 = do_b.reshape(bsz, t, -1)
    dq_b, dk_b, dv_b = _attn_b_bwd(q_b3, kpad3, v_b3, do_b3, o_b, lse_b, tabs_b)
    dq_b2, dk_b2, dv_b2 = dq_b.reshape(n, B_KPAD), dk_b.reshape(n, B_KPAD), dv_b.reshape(n, -1)

    dc_kv = _matmul(dk_b2, w_upk, tb=True, out_dtype=F32, tm=mm, tn=B_KV_LORA, name="kv_up_k_dx")
    dc_kv = _matmul(dv_b2, w_upv, tb=True, out_dtype=BF16, tm=mm, tn=B_KV_LORA, name="kv_up_v_dx",
                    epilogue=_add_epilogue, extras=[(dc_kv, (mm, B_KV_LORA), lambda j, i, kk: (i, j))])
    gw_upk = _matmul(c_kv, dk_b2, ta=True, out_dtype=BF16, tm=B_KV_LORA, tn=1024, tk=2048, name="kv_up_k_dw")
    gw_upv = _matmul(c_kv, dv_b2, ta=True, out_dtype=BF16, tm=B_KV_LORA, tn=1024, tk=2048, name="kv_up_v_dw")

    def latent_bwd(ck, dck, dk, c, sp, sm, g):
        d1, dg = _rms_bwd(ck[:, :B_KV_LORA], g, dck)
        ksum = dk[:, :LANES].astype(F32)
        for h in range(1, B_HEADS):
            ksum = ksum + dk[:, h * LANES:(h + 1) * LANES].astype(F32)
        lane = lax.broadcasted_iota(jnp.int32, ksum.shape, 1)
        ksum = jnp.where((lane >= B_NOPE) & (lane < B_QK_DIM), ksum, 0.0)
        return jnp.concatenate([d1, _rope_t(ksum, c, sp, sm)], axis=1), dg

    dckr, dg_latent = _rowwise(latent_bwd, [ckr, dc_kv, dk_b2, *tb2], [w["kv_latent_norm"]],
                               [(3 * LANES, BF16)], [B_KV_LORA], tm=tm, name="kv_latent_bwd")
    dhn_kv = _matmul(dckr, w_down_p, tb=True, out_dtype=BF16, tm=mm, tn=D_MODEL, name="kv_down_dx")
    gw_down_p = _matmul(hn_kv, dckr, ta=True, out_dtype=BF16, tm=mm, tn=3 * LANES, tk=2048, name="kv_down_dw")

    dc_q = _matmul(dq_b2, w_q_p, tb=True, out_dtype=BF16, tm=mm, tn=B_Q_LORA, name="b_q_up_dx")
    gw_q_p = _matmul(c_q, dq_b2, ta=True, out_dtype=BF16, tm=B_Q_LORA, tn=1024, tk=2048, name="b_q_up_dw")

    def q_norm_bwd(p, dcq, dz, g):
        d1, dg = _rms_bwd(p[:, :B_Q_LORA], g, dcq)
        return jnp.concatenate([d1, dz], axis=1), dg

    dproj_b, dg_q_norm = _rowwise(q_norm_bwd, [proj_b, dc_q, dz_b], [w["b_q_norm"]],
                                  [(w_b_in.shape[1], BF16)], [B_Q_LORA], tm=tm, name="b_q_norm_bwd")
    dhn_b = _matmul(dproj_b, w_b_in, tb=True, out_dtype=BF16, tm=mm, tn=D_MODEL, name="b_in_dx")
    gw_b_in = _matmul(hn_b, dproj_b, ta=True, out_dtype=BF16, tm=mm, tn=w_b_in.shape[1], tk=2048, name="b_in_dw")

    def mid_bwd(h1t, dh2t, dkv, db, g_kv, g_b, g_post, out):
        dxa, ra = _rms_bwd(h1t, g_kv, dkv)
        dxb, rb = _rms_bwd(h1t, g_b, db)
        dh1 = dh2t + dxa + dxb
        d_out, rp = _rms_bwd(out, g_post, dh1)
        return dh1, d_out, ra, rb, rp

    def mid_bwd_fn(h1t, dh2t, dkv, db, out, g_kv, g_b, g_post):
        return mid_bwd(h1t, dh2t, dkv, db, g_kv, g_b, g_post, out)

    dh1, d_out_a, dg_kv, dg_b_pre, dg_a_post = _rowwise(
        mid_bwd_fn, [h1, dh2, dhn_kv, dhn_b, out_a], [w["kv_norm"], w["b_pre_norm"], w["a_post_norm"]],
        [(D_MODEL, F32), (D_MODEL, BF16)], [D_MODEL] * 3, tm=tm, name="mid_bwd")

    dy_a = _matmul(d_out_a, w_a_out, tb=True, out_dtype=BF16, tm=mm, tn=A_WIDTH, name="a_out_dx")
    gw_a_out = _matmul(y_a, d_out_a, ta=True, out_dtype=BF16, tm=mm, tn=D_MODEL, tk=2048, name="a_out_dw")

    def gate_a_bwd(dy, o, z):
        do = dy * _silu(z)
        prod = do * o
        lane = lax.broadcasted_iota(jnp.int32, (prod.shape[0], A_HEADS), 1)
        dsum = jnp.zeros((prod.shape[0], A_HEADS), F32)
        for h in range(A_HEADS):
            col = jnp.sum(prod[:, h * A_HEAD_DIM:(h + 1) * A_HEAD_DIM], axis=1, keepdims=True)
            dsum = jnp.where(lane == h, col, dsum)
        return do, dy * o * _silu_grad(z), dsum

    do_a, dz_a, dsum_a = _rowwise(gate_a_bwd, [dy_a, o_a2, z_a], [],
                                  [(A_WIDTH, BF16), (A_WIDTH, BF16), (A_HEADS, F32)], tm=tm, name="a_gate_bwd")
    dhn_a = _matmul(dz_a, w_a_in, b_cols=(A_QKV, A_WIDTH), tb=True, out_dtype=F32, tm=mm, tn=D_MODEL,
                    name="a_gate_dx")
    gw_parts, dhn_groups = [], []
    for g, d in enumerate(A_DILATIONS):
        s_n, ln = bsz * d, t // d
        dqkv = _attn_a_bwd(qk_g[g], v_g[g], to_group(do_a, d).reshape(s_n, ln, A_WIDTH), cols_to_rows(lse_a, d),
                           cols_to_rows(dsum_a, d), [a.reshape(s_n, ln, LANES) for a in tabs_g[g]],
                           f"attn_a_bwd_g{g}").reshape(n, 3 * A_WIDTH)
        w_cols = (g * 3 * A_WIDTH, 3 * A_WIDTH)
        if d == 1:
            dhn_a = _matmul(dqkv, w_a_in, b_cols=w_cols, tb=True, out_dtype=F32, tm=mm, tn=D_MODEL,
                            name=f"a_qkv_dx_g{g}", epilogue=_add_epilogue,
                            extras=[(dhn_a, (mm, D_MODEL), lambda j, i, kk: (i, j))])
        else:
            dhn_groups.append(from_group(_matmul(dqkv, w_a_in, b_cols=w_cols, tb=True, out_dtype=BF16, tm=mm,
                                                 tn=D_MODEL, name=f"a_qkv_dx_g{g}"), d))
        gw_parts.append(_matmul(hn_g[g], dqkv, ta=True, out_dtype=BF16, tm=mm, tn=1024, tk=2048,
                                name=f"a_qkv_dw_g{g}"))
    gw_parts.append(_matmul(hn_a, dz_a, ta=True, out_dtype=BF16, tm=mm, tn=1024, tk=2048, name="a_gate_dw"))
    gw_a_in = jnp.concatenate(gw_parts, axis=1)

    def first_bwd(xt, dhn, dhn_1, dhn_2, dh1t, g):
        dx, dg = _rms_bwd(xt, g, dhn + dhn_1 + dhn_2)
        return dh1t + dx, dg

    grad_x, dg_a_pre = _rowwise(first_bwd, [x2, dhn_a, *dhn_groups, dh1], [w["a_pre_norm"]], [(D_MODEL, F32)],
                                [D_MODEL], tm=tm, name="a_pre_norm_bwd")

    gw_down = jnp.concatenate([gw_down_p[:, :B_KV_LORA], gw_down_p[:, B_KV_LORA + B_NOPE:B_KV_LORA + B_QK_DIM]], axis=1)
    gw_up = jnp.concatenate([gw_upk.reshape(B_KV_LORA, B_HEADS, LANES)[:, :, :B_NOPE],
                             gw_upv.reshape(B_KV_LORA, B_HEADS, B_VDIM)], axis=2).reshape(B_KV_LORA, -1)
    gw_q_up = gw_q_p.reshape(B_Q_LORA, B_HEADS, LANES)[:, :, :B_QK_DIM].reshape(B_Q_LORA, -1)
    grads = {"a_w_in": gw_a_in, "a_w_out": gw_a_out, "kv_w_down": gw_down, "kv_w_up": gw_up,
             "b_w_in": gw_b_in, "b_w_q_up": gw_q_up, "b_w_out": gw_b_out}
    gains = {"a_pre_norm": dg_a_pre, "a_post_norm": dg_a_post, "kv_norm": dg_kv, "kv_latent_norm": dg_latent,
             "b_pre_norm": dg_b_pre, "b_q_norm": dg_q_norm, "b_post_norm": dg_b_post}
    gains = {k: jnp.sum(a, axis=0) for k, a in gains.items()}
    return jnp.sum(loss_acc), grad_x.reshape(bsz, t, D_MODEL), grads, gains


WEIGHT_ORDER = ("a_pre_norm", "a_w_in", "a_w_out", "a_post_norm", "kv_norm", "kv_w_down", "kv_latent_norm",
                "kv_w_up", "b_pre_norm", "b_w_in", "b_q_norm", "b_w_q_up", "b_w_out", "b_post_norm")
MATRICES = (("a_w_in", 1024, 10240, 1), ("a_w_out", 1024, 1024, 0), ("kv_w_down", 1024, 288, 0),
            ("kv_w_up", 256, 2048, 1), ("b_w_in", 1024, 1408, 1), ("b_w_q_up", 384, 1536, 1),
            ("b_w_out", 1024, 1024, 0))
SHARDED_GAINS = ("a_pre_norm", "a_post_norm")
GAIN_WIDTHS = (("a_pre_norm", 1024), ("a_post_norm", 1024), ("kv_norm", 1024), ("kv_latent_norm", 256),
               ("b_pre_norm", 1024), ("b_q_norm", 384), ("b_post_norm", 1024))
GAIN_ROWS = 48


def _shard_rows(rows, cols):
    return rows * cols // (N_DEV * LANES)


def _whole_from_blocks(blocks, rows, cols, axis):
    if axis == 1:
        return blocks.reshape(N_DEV, rows, cols // N_DEV).transpose(1, 0, 2).reshape(rows, cols)
    return blocks.reshape(rows, cols)


def _blocks_from_whole(whole, rows, cols, axis):
    if axis == 1:
        whole = whole.reshape(rows, N_DEV, cols // N_DEV).transpose(1, 0, 2)
    return whole.reshape(N_DEV, -1, LANES)


def kernel(x, positions, a_pre_norm, a_w_in, a_w_out, a_post_norm, kv_norm, kv_w_down, kv_latent_norm, kv_w_up, b_pre_norm, b_w_in, b_q_norm, b_w_q_up, b_w_out, b_post_norm, loss_target, m_a_pre_norm, m_a_w_in, m_a_w_out, m_a_post_norm, m_kv_norm, m_kv_w_down, m_kv_latent_norm, m_kv_w_up, m_b_pre_norm, m_b_w_in, m_b_q_norm, m_b_w_q_up, m_b_w_out, m_b_post_norm, v_a_pre_norm, v_a_w_in, v_a_w_out, v_a_post_norm, v_kv_norm, v_kv_w_down, v_kv_latent_norm, v_kv_w_up, v_b_pre_norm, v_b_w_in, v_b_q_norm, v_b_w_q_up, v_b_w_out, v_b_post_norm):
    weights = dict(a_pre_norm=a_pre_norm, a_w_in=a_w_in, a_w_out=a_w_out, a_post_norm=a_post_norm, kv_norm=kv_norm,
                   kv_w_down=kv_w_down, kv_latent_norm=kv_latent_norm, kv_w_up=kv_w_up, b_pre_norm=b_pre_norm,
                   b_w_in=b_w_in, b_q_norm=b_q_norm, b_w_q_up=b_w_q_up, b_w_out=b_w_out, b_post_norm=b_post_norm)
    m_in = dict(a_pre_norm=m_a_pre_norm, a_w_in=m_a_w_in, a_w_out=m_a_w_out, a_post_norm=m_a_post_norm,
                kv_norm=m_kv_norm, kv_w_down=m_kv_w_down, kv_latent_norm=m_kv_latent_norm, kv_w_up=m_kv_w_up,
                b_pre_norm=m_b_pre_norm, b_w_in=m_b_w_in, b_q_norm=m_b_q_norm, b_w_q_up=m_b_w_q_up,
                b_w_out=m_b_w_out, b_post_norm=m_b_post_norm)
    v_in = dict(a_pre_norm=v_a_pre_norm, a_w_in=v_a_w_in, a_w_out=v_a_w_out, a_post_norm=v_a_post_norm,
                kv_norm=v_kv_norm, kv_w_down=v_kv_w_down, kv_latent_norm=v_kv_latent_norm, kv_w_up=v_kv_w_up,
                b_pre_norm=v_b_pre_norm, b_w_in=v_b_w_in, b_q_norm=v_b_q_norm, b_w_q_up=v_b_w_q_up,
                b_w_out=v_b_w_out, b_post_norm=v_b_post_norm)
    me = 4 * lax.axis_index("x") + 2 * lax.axis_index("y") + lax.axis_index("c")

    wide = MATRICES[0][0]
    flat = jnp.concatenate([weights[name].astype(BF16).reshape(-1, LANES) for name, _, _, _ in MATRICES[1:]], axis=0)
    gathered, w_wide = _all_gather_weights(flat, weights[wide][0].astype(BF16), "gather_weights")
    whole = {wide: w_wide}
    off = 0
    for name, rows, cols, axis in MATRICES[1:]:
        nr = _shard_rows(rows, cols)
        whole[name] = _whole_from_blocks(gathered[:, off:off + nr], rows, cols, axis)
        off += nr
    gain_shard = jnp.concatenate([weights[name].reshape(1, LANES) for name in SHARDED_GAINS]
                                 + [jnp.zeros((8 - len(SHARDED_GAINS), LANES), F32)], axis=0)
    gain_blocks = _all_gather(gain_shard, "gather_gains")
    for i, name in enumerate(SHARDED_GAINS):
        whole[name] = gain_blocks[:, i, :].reshape(1, D_MODEL)
    for name in ("kv_norm", "kv_latent_norm", "b_pre_norm", "b_q_norm", "b_post_norm"):
        whole[name] = weights[name].reshape(1, -1)

    loss_part, grad_x, grads, gains = _local_step(x, positions, loss_target, whole)

    blocks = jnp.concatenate([_blocks_from_whole(grads[name], rows, cols, axis).astype(BF16)
                              for name, rows, cols, axis in MATRICES], axis=1)
    blocks = blocks.reshape(N_CHIPS, 2, blocks.shape[1], LANES)
    got = _exchange_sibling(blocks, "scatter_grads_core")
    core = lax.axis_index("c").astype(jnp.int32).reshape(1)
    landed = _exchange_chips(_add_pairs(blocks, got, core, tr=2512, name="add_core_grads"), "scatter_grads_chip")
    summed = _sum_slots(landed, tr=2512, name="sum_grads")
    grad_out = {}
    off = 0
    for name, rows, cols, axis in MATRICES:
        nr = _shard_rows(rows, cols)
        grad_out[name] = summed[off:off + nr].reshape(weights[name].shape)
        off += nr

    vec = jnp.concatenate([gains[name] for name, _ in GAIN_WIDTHS] + [jnp.full((LANES,), loss_part, F32)])
    vec = jnp.pad(vec, (0, GAIN_ROWS * LANES - vec.shape[0])).reshape(GAIN_ROWS, LANES)
    total = _sum_slots(_all_gather(vec, "gather_gain_grads"), tr=GAIN_ROWS, name="sum_gain_grads").reshape(-1)
    off = 0
    for name, width in GAIN_WIDTHS:
        g = total[off:off + width]
        if name in SHARDED_GAINS:
            g = lax.dynamic_slice(g, (me * LANES,), (LANES,))
        grad_out[name] = g.reshape(weights[name].shape)
        off += width
    loss = total[off]

    deltas, new_m, new_v = {}, {}, {}
    big = "a_w_in"
    deltas[big], new_m[big], new_v[big] = _adamw(weights[big], grad_out[big], m_in[big], v_in[big], "adamw_" + big)
    small = [name for name in WEIGHT_ORDER if name != big]
    res = _adamw_small(*[[d[name] for name in small] for d in (weights, grad_out, m_in, v_in)], "adamw_small")
    for out, vals in zip((deltas, new_m, new_v), res):
        out.update(zip(small, vals))
    return (loss, grad_x, *[grad_out[k] for k in WEIGHT_ORDER], *[deltas[k] for k in WEIGHT_ORDER],
            *[new_m[k] for k in WEIGHT_ORDER], *[new_v[k] for k in WEIGHT_ORDER])
```

```python
import jax
import jax.numpy as jnp
from jax import lax
from jax.experimental import pallas as pl
from jax.experimental.pallas import tpu as pltpu

F32 = jnp.float32
BF16 = jnp.bfloat16

N_DEV = 8
D_MODEL = 1024
NORM_EPS = 1e-6
A_GROUPS = 3
A_DILATIONS = (1, 4, 16)
A_HEADS = 8
A_HEAD_DIM = 128
A_WIDTH = 1024
A_ROT_DIM = 32
A_ROPE_THETA = 500000.0
A_QKV = A_GROUPS * 3 * A_WIDTH
B_HEADS = 16
B_NOPE = 64
B_ROPE = 32
B_QK_DIM = 96
B_VDIM = 64
B_Q_LORA = 384
B_KV_LORA = 256
B_ROPE_THETA = 10000.0
B_KPAD = B_HEADS * 128
ADAM_LR = 0.001
ADAM_B1 = 0.9
ADAM_B2 = 0.999
ADAM_EPS = 1e-08
ADAM_WD = 0.01
ADAM_STEP = 10

LANES = 128
BAND = 128
EPILOGUE_ROWS = 128
NEG = -1e30
VMEM_LIMIT = 56 * 1024 * 1024
MESH = pl.DeviceIdType.MESH


def _cparams(sem):
    return pltpu.CompilerParams(dimension_semantics=sem, vmem_limit_bytes=VMEM_LIMIT)


def _rowwise(fn, rows, bcast, outs, accs=(), *, tm, name):
    n = rows[0].shape[0]
    nr, nb, no = len(rows), len(bcast), len(outs)

    def body(*refs):
        res = fn(*[r[...] for r in refs[:nr + nb]])
        out_refs = refs[nr + nb:nr + nb + no]
        acc_refs = refs[nr + nb + no:]
        for r, v in zip(out_refs, res[:no]):
            r[...] = v.astype(r.dtype)
        if acc_refs:
            @pl.when(pl.program_id(0) == 0)
            def _():
                for r in acc_refs:
                    r[...] = jnp.zeros_like(r)
            for r, v in zip(acc_refs, res[no:]):
                r[...] += v.reshape(tm // 8, 8, v.shape[-1]).sum(axis=0)

    in_specs = [pl.BlockSpec((tm, a.shape[1]), lambda i: (i, 0)) for a in rows]
    in_specs += [pl.BlockSpec(a.shape, lambda i: (0, 0)) for a in bcast]
    out_specs = [pl.BlockSpec((tm, c), lambda i: (i, 0)) for c, _ in outs]
    out_specs += [pl.BlockSpec((8, c), lambda i: (0, 0)) for c in accs]
    out_shape = [jax.ShapeDtypeStruct((n, c), dt) for c, dt in outs]
    out_shape += [jax.ShapeDtypeStruct((8, c), F32) for c in accs]
    return pl.pallas_call(
        body, name=name, grid=(n // tm,), in_specs=in_specs, out_specs=out_specs, out_shape=out_shape,
        compiler_params=_cparams(("arbitrary",)))(*rows, *bcast)


def _matmul(a, b, *, out_dtype, tm, tn, tk=None, name, epilogue=None, extras=(), ta=False, tb=False,
            epilogue_rows=None, b_cols=None, out_into=None):
    epilogue_rows = epilogue_rows or tm
    k, m = a.shape[::-1] if not ta else a.shape
    col0, width = b_cols or (0, b.shape[1])
    n = b.shape[0] if tb else width
    tk = tk or k
    nk = k // tk
    b_off = col0 // (tk if tb else tn)
    assert col0 % (tk if tb else tn) == 0 and (k == width if tb else True)
    ne = len(extras)
    dot = _dot_tn if ta else (_dot_nt if tb else _dot)
    assert epilogue is None or (nk == 1 and not ta)

    def body(*refs):
        a_ref, b_ref = refs[:2]
        ex = refs[2:2 + ne]
        o_ref = refs[2 + ne + (1 if out_into is not None and out_into[0] is not None else 0)]
        if epilogue is not None:
            b_tile = b_ref[...].astype(BF16)
            for r0 in range(0, tm, epilogue_rows):
                rows = slice(r0, r0 + epilogue_rows)
                epilogue(dot(a_ref[rows, :].astype(BF16), b_tile), o_ref, rows, *ex)
            return
        part = dot(a_ref[...].astype(BF16), b_ref[...].astype(BF16))
        if nk == 1:
            o_ref[...] = part.astype(o_ref.dtype)
        else:
            acc_ref = refs[-1]
            kk = pl.program_id(2)

            @pl.when(kk == 0)
            def _():
                acc_ref[...] = part

            @pl.when(kk > 0)
            def _():
                acc_ref[...] += part

            @pl.when(kk == nk - 1)
            def _():
                o_ref[...] = acc_ref[...].astype(o_ref.dtype)

    a_spec = pl.BlockSpec((tk, tm), lambda j, i, kk: (kk, i)) if ta else pl.BlockSpec((tm, tk), lambda j, i, kk: (i, kk))
    b_spec = (pl.BlockSpec((tn, tk), lambda j, i, kk: (j, kk + b_off)) if tb
              else pl.BlockSpec((tk, tn), lambda j, i, kk: (kk, j + b_off)))
    in_specs = [a_spec, b_spec] + [pl.BlockSpec(bs, im) for _, bs, im in extras]
    operands = [a, b] + [e[0] for e in extras]
    aliases = {}
    if out_into is not None:
        prev, out0, n_total = out_into
        o_off = out0 // tn
        assert out0 % tn == 0
        if prev is not None:
            in_specs.append(ANY)
            operands.append(prev)
            aliases = {len(operands) - 1: 0}
    else:
        o_off, n_total = 0, n
    return pl.pallas_call(
        body, name=name, grid=(n // tn, m // tm, nk), in_specs=in_specs,
        out_specs=pl.BlockSpec((tm, tn), lambda j, i, kk: (i, j + o_off)),
        out_shape=jax.ShapeDtypeStruct((m, n_total), out_dtype), input_output_aliases=aliases,
        scratch_shapes=[pltpu.VMEM((tm, tn), F32)] if nk > 1 else [],
        compiler_params=_cparams(("parallel", "parallel", "arbitrary")))(*operands)


def _add_epilogue(acc, o_ref, rows, prev_ref):
    o_ref[rows, :] = (acc + prev_ref[rows, :]).astype(o_ref.dtype)


def _rope(x, c, sp, sm):
    return x * c + pltpu.roll(x, 16, 1) * sp + pltpu.roll(x, LANES - 16, 1) * sm


def _rope_t(dy, c, sp, sm):
    return dy * c + pltpu.roll(dy * sp, LANES - 16, 1) + pltpu.roll(dy * sm, 16, 1)


def _rope_tables(positions, theta, rot_dim, lane0, name):
    n = positions.shape[0]
    half = rot_dim // 2
    inv_freq = 1.0 / (theta ** (jnp.arange(half, dtype=F32) * (2.0 / rot_dim)))
    freq = jnp.concatenate([jnp.zeros((lane0,), F32), inv_freq, inv_freq,
                            jnp.zeros((LANES - lane0 - rot_dim,), F32)]).reshape(1, LANES)

    def fn(p, f):
        ang = p * f
        cos, sin = jnp.cos(ang), jnp.sin(ang)
        lane = lax.broadcasted_iota(jnp.int32, ang.shape, 1) - lane0
        first, second = (lane >= 0) & (lane < half), (lane >= half) & (lane < rot_dim)
        return jnp.where(first | second, cos, 1.0), jnp.where(second, sin, 0.0), jnp.where(first, -sin, 0.0)

    return _rowwise(fn, [positions.astype(F32).reshape(n, 1)], [freq], [(LANES, F32)] * 3, tm=512, name=name)


def _value_heads_t(c_kv, w_vt, bsz, t):
    n, k = c_kv.shape
    tm, tn = 512, 1024

    def body(a_ref, b_ref, o_ref):
        acc = _dot(a_ref[...], b_ref[...])
        lane = lax.broadcasted_iota(jnp.int32, (1, tn), 1)
        acc = acc + jnp.where(lax.rem(lane, LANES) == B_VDIM, 1.0, 0.0)
        o_ref[...] = acc.T.astype(BF16)

    per_seq = t // tm
    return pl.pallas_call(
        body, name="kv_up_v_t", grid=(w_vt.shape[1] // tn, n // tm),
        in_specs=[pl.BlockSpec((tm, k), lambda j, i: (i, 0)), pl.BlockSpec((k, tn), lambda j, i: (0, j))],
        out_specs=pl.BlockSpec((None, tn, tm), lambda j, i: (i // per_seq, j, lax.rem(i, per_seq))),
        out_shape=jax.ShapeDtypeStruct((bsz, w_vt.shape[1], t), BF16),
        compiler_params=_cparams(("parallel", "parallel")))(c_kv, w_vt)


def _rms(x, g):
    xf = x.astype(F32)
    return xf * lax.rsqrt(jnp.mean(xf * xf, axis=-1, keepdims=True) + NORM_EPS) * g


def _rms_bwd(x, g, dy):
    xf = x.astype(F32)
    rstd = lax.rsqrt(jnp.mean(xf * xf, axis=-1, keepdims=True) + NORM_EPS)
    xhat = xf * rstd
    dxhat = dy * g
    dx = rstd * (dxhat - xhat * jnp.mean(dxhat * xhat, axis=-1, keepdims=True))
    return dx, dy * xhat


def _silu(z):
    return z * jax.nn.sigmoid(z)


def _silu_grad(z):
    s = jax.nn.sigmoid(z)
    return s * (1.0 + z * (1.0 - s))


def _dot_nt(a, b):
    return lax.dot_general(a, b, (((1,), (1,)), ((), ())), preferred_element_type=F32)


def _dot_tn(a, b):
    return lax.dot_general(a, b, (((0,), (0,)), ((), ())), preferred_element_type=F32)


def _dot(a, b):
    return jnp.dot(a, b, preferred_element_type=F32)


A_SCALE = A_HEAD_DIM ** -0.5
LOG2E = 1.4426950408889634
A_C2 = A_SCALE * LOG2E
A_HEAD_GROUP = 4


def _attn_a_fwd(qk, v, name):
    s_n, ln, _ = qk.shape
    nb = ln // BAND
    blk = (None, BAND, A_WIDTH)

    def body(q_ref, kc_ref, kp_ref, vc_ref, vp_ref, o_ref, lse_ref):
        kpos = lax.broadcasted_iota(jnp.int32, (2 * BAND, BAND), 0)
        qpos = lax.broadcasted_iota(jnp.int32, (2 * BAND, BAND), 1) + BAND
        first_key = jnp.where(pl.program_id(1) > 0, 0, BAND)
        mask = (kpos <= qpos) & (kpos >= qpos - BAND) & (kpos >= first_key)
        rows = []
        for h0 in range(0, A_HEADS, A_HEAD_GROUP):
            hss = [slice(h * A_HEAD_DIM, (h + 1) * A_HEAD_DIM) for h in range(h0, h0 + A_HEAD_GROUP)]
            sts = [_dot_nt(jnp.concatenate([kp_ref[:, hs], kc_ref[:, hs]], axis=0), q_ref[:, hs]) for hs in hss]
            ps, ls = [], []
            for st in sts:
                st = jnp.where(mask, st * A_C2, NEG)
                m = jnp.max(st, axis=0, keepdims=True)
                p = jnp.exp2(st - m)
                l_row = jnp.sum(p, axis=0, keepdims=True)
                ps.append(p.astype(BF16))
                ls.append(l_row)
                rows.append(m + jnp.log2(l_row))
            ots = [_dot_tn(jnp.concatenate([vp_ref[:, hs], vc_ref[:, hs]], axis=0), p) for hs, p in zip(hss, ps)]
            for hs, o_t, l_row in zip(hss, ots, ls):
                o_ref[:, hs] = (o_t / l_row).T.astype(BF16)
        lse_ref[...] = jnp.concatenate(rows, axis=0)

    def col(c, off):
        return lambda s, l: (s, jnp.maximum(l + off, 0), c)

    return pl.pallas_call(
        body, name=name, grid=(s_n, nb),
        in_specs=[pl.BlockSpec(blk, col(0, 0)), pl.BlockSpec(blk, col(1, 0)), pl.BlockSpec(blk, col(1, -1)),
                  pl.BlockSpec(blk, col(0, 0)), pl.BlockSpec(blk, col(0, -1))],
        out_specs=[pl.BlockSpec(blk, lambda s, l: (s, l, 0)),
                   pl.BlockSpec((None, A_HEADS, BAND), lambda s, l: (s, 0, l))],
        out_shape=[jax.ShapeDtypeStruct((s_n, ln, A_WIDTH), BF16), jax.ShapeDtypeStruct((s_n, A_HEADS, ln), F32)],
        compiler_params=_cparams(("parallel", "arbitrary")))(qk, qk, qk, v, v)


def _attn_a_bwd(qk, v, do, lse2, dsum, tabs, name):
    s_n, ln, _ = qk.shape
    nb = ln // BAND
    blk = (None, BAND, A_WIDTH)

    def body(q_ref, qn_ref, kc_ref, kp_ref, vc_ref, vp_ref, do_ref, don_ref, lse_ref, lsen_ref, ds_ref, dsn_ref,
             c_ref, sp_ref, sm_ref, out_ref):
        l_idx = pl.program_id(1)
        kpos = lax.broadcasted_iota(jnp.int32, (2 * BAND, BAND), 0)
        qpos = lax.broadcasted_iota(jnp.int32, (2 * BAND, BAND), 1) + BAND
        first_key = jnp.where(l_idx > 0, 0, BAND)
        mask_q = (kpos <= qpos) & (kpos >= qpos - BAND) & (kpos >= first_key)
        kpos2 = lax.broadcasted_iota(jnp.int32, (BAND, 2 * BAND), 0)
        qpos2 = lax.broadcasted_iota(jnp.int32, (BAND, 2 * BAND), 1)
        last_query = jnp.where(l_idx < nb - 1, 2 * BAND, BAND)
        mask_k = (kpos2 <= qpos2) & (kpos2 >= qpos2 - BAND) & (qpos2 < last_query)
        c, sp, sm = c_ref[...], sp_ref[...], sm_ref[...]
        lse_q, ds_q = lse_ref[...], ds_ref[...]
        lse_k = jnp.concatenate([lse_q, lsen_ref[...]], axis=1)
        ds_k = jnp.concatenate([ds_q, dsn_ref[...]], axis=1)
        for h0 in range(0, A_HEADS, A_HEAD_GROUP):
            hl = list(range(h0, h0 + A_HEAD_GROUP))
            hss = [slice(h * A_HEAD_DIM, (h + 1) * A_HEAD_DIM) for h in hl]
            k2s = [jnp.concatenate([kp_ref[:, hs], kc_ref[:, hs]], axis=0) for hs in hss]
            v2s = [jnp.concatenate([vp_ref[:, hs], vc_ref[:, hs]], axis=0) for hs in hss]
            q2s = [jnp.concatenate([q_ref[:, hs], qn_ref[:, hs]], axis=0) for hs in hss]
            do2s = [jnp.concatenate([do_ref[:, hs], don_ref[:, hs]], axis=0) for hs in hss]
            sts = [_dot_nt(k2, q_ref[:, hs]) for k2, hs in zip(k2s, hss)]
            dpts = [_dot_nt(v2, do_ref[:, hs]) for v2, hs in zip(v2s, hss)]
            st2s = [_dot_nt(kc_ref[:, hs], q2) for q2, hs in zip(q2s, hss)]
            dpt2s = [_dot_nt(vc_ref[:, hs], do2) for do2, hs in zip(do2s, hss)]
            dsts, dst2s, p2s = [], [], []
            for i, h in enumerate(hl):
                p = jnp.exp2(jnp.where(mask_q, sts[i] * A_C2, NEG) - lse_q[h:h + 1])
                dsts.append((p * (dpts[i] - ds_q[h:h + 1]) * A_SCALE).astype(BF16))
                p2 = jnp.exp2(jnp.where(mask_k, st2s[i] * A_C2, NEG) - lse_k[h:h + 1])
                dst2s.append((p2 * (dpt2s[i] - ds_k[h:h + 1]) * A_SCALE).astype(BF16))
                p2s.append(p2.astype(BF16))
            dqs = [_dot_tn(dsts[i], k2s[i]) for i in range(A_HEAD_GROUP)]
            dks = [_dot(dst2s[i], q2s[i]) for i in range(A_HEAD_GROUP)]
            dvs = [_dot(p2s[i], do2s[i]) for i in range(A_HEAD_GROUP)]
            for i, h in enumerate(hl):
                out_ref[:, hss[i]] = _rope_t(dqs[i], c, sp, sm).astype(BF16)
                out_ref[:, A_WIDTH + h * A_HEAD_DIM:A_WIDTH + (h + 1) * A_HEAD_DIM] = _rope_t(dks[i], c, sp, sm).astype(BF16)
                out_ref[:, 2 * A_WIDTH + h * A_HEAD_DIM:2 * A_WIDTH + (h + 1) * A_HEAD_DIM] = dvs[i].astype(BF16)

    def col(c, off):
        return lambda s, l: (s, jnp.clip(l + off, 0, nb - 1), c)

    def row(off):
        return pl.BlockSpec((None, A_HEADS, BAND), lambda s, l: (s, 0, jnp.clip(l + off, 0, nb - 1)))

    tspec = pl.BlockSpec((None, BAND, LANES), lambda s, l: (s, l, 0))
    in_specs = [pl.BlockSpec(blk, col(0, 0)), pl.BlockSpec(blk, col(0, 1)),
                pl.BlockSpec(blk, col(1, 0)), pl.BlockSpec(blk, col(1, -1)),
                pl.BlockSpec(blk, col(0, 0)), pl.BlockSpec(blk, col(0, -1)),
                pl.BlockSpec(blk, col(0, 0)), pl.BlockSpec(blk, col(0, 1)),
                row(0), row(1), row(0), row(1), tspec, tspec, tspec]
    return pl.pallas_call(
        body, name=name, grid=(s_n, nb), in_specs=in_specs,
        out_specs=pl.BlockSpec((None, BAND, 3 * A_WIDTH), lambda s, l: (s, l, 0)),
        out_shape=jax.ShapeDtypeStruct((s_n, ln, 3 * A_WIDTH), BF16),
        compiler_params=_cparams(("parallel", "arbitrary")))(
            qk, qk, qk, qk, v, v, do, do, lse2, lse2, dsum, dsum, *tabs)


B_TQ = 256
B_FWD_HEADS = 4
B_SCALE = B_QK_DIM ** -0.5
B_C2 = B_SCALE * LOG2E


def _key_le_query(kb, qb, tk, tq):
    kpos = kb * tk + lax.broadcasted_iota(jnp.int32, (tk, tq), 0)
    qpos = qb * tq + lax.broadcasted_iota(jnp.int32, (tk, tq), 1)
    return kpos <= qpos


def _attn_b_fwd(q, kpad, vt1):
    bsz, t, _ = q.shape
    tq = tk = B_TQ
    nq = t // tq
    nh = B_FWD_HEADS

    def body(q_ref, k_ref, vt_ref, o_ref, lse_ref):
        qblk = pl.program_id(2)
        qs = [q_ref[:, hh * LANES:(hh + 1) * LANES] for hh in range(nh)]

        def scores(kb):
            start = pl.multiple_of(kb * tk, tk)
            return [_dot_nt(k_ref[pl.ds(start, tk), hh * LANES:(hh + 1) * LANES], qs[hh]) for hh in range(nh)]

        def pv(kb, ps):
            start = pl.multiple_of(kb * tk, tk)
            return [_dot(vt_ref[hh * LANES:(hh + 1) * LANES, pl.ds(start, tk)], ps[hh]) for hh in range(nh)]

        def softmax(ss, ms, accs, kb, masked):
            out_m, out_acc, out_p = [], [], []
            for hh in range(nh):
                s = ss[hh] * B_C2
                if masked:
                    s = jnp.where(_key_le_query(kb, qblk, tk, tq), s, NEG)
                m_new = jnp.maximum(ms[hh], jnp.max(s, axis=0, keepdims=True))
                out_acc.append(jnp.exp2(ms[hh] - m_new) * accs[hh])
                out_p.append(jnp.exp2(s - m_new).astype(BF16))
                out_m.append(m_new)
            return out_m, out_acc, out_p

        def step(kb, carry):
            ss, ps, ms, accs = carry
            pvs = pv(jnp.maximum(kb - 1, 0), ps)
            ss_next = scores(kb + 1)
            accs = [accs[hh] + pvs[hh] for hh in range(nh)]
            ms, accs, ps = softmax(ss, ms, accs, kb, False)
            return (ss_next, ps, ms, accs)

        init = (scores(0), [jnp.zeros((tk, tq), BF16)] * nh, [jnp.full((1, tq), NEG, F32)] * nh,
                [jnp.zeros((LANES, tq), F32)] * nh)
        ss, ps, ms, accs = lax.fori_loop(0, qblk, step, init)
        pvs = pv(jnp.maximum(qblk - 1, 0), ps)
        accs = [accs[hh] + pvs[hh] for hh in range(nh)]
        ms, accs, ps = softmax(ss, ms, accs, qblk, True)
        pvs = pv(qblk, ps)
        accs = [accs[hh] + pvs[hh] for hh in range(nh)]
        ls = [accs[hh][B_VDIM:B_VDIM + 1] for hh in range(nh)]
        o_t = jnp.concatenate([accs[hh][:B_VDIM] / ls[hh] for hh in range(nh)], axis=0)
        o_ref[...] = o_t.T
        for pair in range(nh // 2):
            lse_ref[pair] = jnp.concatenate([ms[2 * pair + hh] + jnp.log2(ls[2 * pair + hh]) for hh in range(2)]
                                            + [jnp.zeros((6, tq), F32)], axis=0)

    return pl.pallas_call(
        body, name="attn_b_fwd", grid=(bsz, B_HEADS // nh, nq),
        in_specs=[pl.BlockSpec((None, tq, nh * LANES), lambda b, j, i: (b, i, j)),
                  pl.BlockSpec((None, t, nh * LANES), lambda b, j, i: (b, 0, j)),
                  pl.BlockSpec((None, nh * LANES, t), lambda b, j, i: (b, j, 0))],
        out_specs=[pl.BlockSpec((None, tq, nh * B_VDIM), lambda b, j, i: (b, i, j)),
                   pl.BlockSpec((None, nh // 2, 8, tq), lambda b, j, i: (b, j, 0, i))],
        out_shape=[jax.ShapeDtypeStruct((bsz, t, B_HEADS * B_VDIM), F32),
                   jax.ShapeDtypeStruct((bsz, B_HEADS // 2, 8, t), F32)],
        compiler_params=_cparams(("parallel", "parallel", "arbitrary")))(q, kpad, vt1)


def _attn_b_bwd(q, kpad, v, do, o, lse2, tabs):
    bsz, t, _ = q.shape
    tq = tk = B_TQ
    nq = t // tq

    def body(q_ref, k_ref, v_ref, do_ref, o_ref, lse_ref, c_ref, sp_ref, sm_ref, dq_ref, dk_ref, dv_ref,
             dqt_scr, dsum_scr):
        lane = lax.broadcasted_iota(jnp.int32, (tk, LANES), 1)
        sel_lane = lax.broadcasted_iota(jnp.int32, (8, LANES), 1)
        sel_row = lax.broadcasted_iota(jnp.int32, (8, LANES), 0)
        sel = jnp.where((sel_lane < B_VDIM) == (sel_row == 0), 1.0, 0.0)
        sel = jnp.where(sel_row < 2, sel, 0.0).astype(BF16)

        def rows(blk):
            return pl.ds(pl.multiple_of(blk * tq, tq), tq)

        def dsum_step(qb, carry):
            prod = do_ref[rows(qb), :].astype(F32) * o_ref[rows(qb), :]
            hi = prod.astype(BF16)
            lo = (prod - hi.astype(F32)).astype(BF16)
            dsum_scr[:, rows(qb)] = _dot_nt(sel, hi) + _dot_nt(sel, lo)
            return carry

        lax.fori_loop(0, nq, dsum_step, 0)
        dqt_scr[...] = jnp.zeros_like(dqt_scr)

        def kv_step(kb, carry):
            ks = [k_ref[rows(kb), hh * LANES:(hh + 1) * LANES] for hh in range(2)]
            vb = v_ref[rows(kb), :]
            zero = jnp.zeros_like(vb)
            vs = [jnp.where(lane < B_VDIM, vb, zero), jnp.where(lane < B_VDIM, zero, vb)]

            def make_step(masked):
                def step(qb, acc):
                    qs = [q_ref[rows(qb), hh * LANES:(hh + 1) * LANES] for hh in range(2)]
                    do_b = do_ref[rows(qb), :]
                    ss = [_dot_nt(ks[hh], qs[hh]) for hh in range(2)]
                    dps = [_dot_nt(vs[hh], do_b) for hh in range(2)]
                    pbs, dss = [], []
                    for hh in range(2):
                        s = ss[hh] * B_C2
                        if masked:
                            s = jnp.where(_key_le_query(kb, qb, tk, tq), s, NEG)
                        p = jnp.exp2(s - lse_ref[hh:hh + 1, rows(qb)])
                        dss.append((p * (dps[hh] - dsum_scr[hh:hh + 1, rows(qb)]) * B_SCALE).astype(BF16))
                        pbs.append(p.astype(BF16))
                    for hh in range(2):
                        dqt_scr[hh, :, rows(qb)] += _dot_tn(ks[hh], dss[hh])
                    return (acc[0] + _dot(dss[0], qs[0]), acc[1] + _dot(dss[1], qs[1]),
                            acc[2] + _dot(pbs[0], do_b), acc[3] + _dot(pbs[1], do_b))
                return step

            acc = make_step(True)(kb, (jnp.zeros((tk, LANES), F32),) * 4)
            acc = lax.fori_loop(kb + 1, nq, make_step(False), acc)
            dk_ref[rows(kb), :LANES] = acc[0].astype(BF16)
            dk_ref[rows(kb), LANES:] = acc[1].astype(BF16)
            dv_ref[rows(kb), :] = jnp.where(lane < B_VDIM, acc[2], acc[3]).astype(BF16)
            return carry

        lax.fori_loop(0, nq, kv_step, 0)

        def dq_step(qb, carry):
            c, sp, sm = c_ref[rows(qb), :], sp_ref[rows(qb), :], sm_ref[rows(qb), :]
            for hh in range(2):
                dq_ref[rows(qb), hh * LANES:(hh + 1) * LANES] = _rope_t(dqt_scr[hh, :, rows(qb)].T, c, sp, sm).astype(BF16)
            return carry

        lax.fori_loop(0, nq, dq_step, 0)

    pair_full = pl.BlockSpec((None, t, 2 * LANES), lambda b, j: (b, 0, j))
    one_full = pl.BlockSpec((None, t, LANES), lambda b, j: (b, 0, j))
    row_full = pl.BlockSpec((None, None, 8, t), lambda b, j: (b, j, 0, 0))
    tab_full = pl.BlockSpec((None, t, LANES), lambda b, j: (b, 0, 0))
    return pl.pallas_call(
        body, name="attn_b_bwd", grid=(bsz, B_HEADS // 2),
        in_specs=[pair_full, pair_full, one_full, one_full, one_full, row_full, tab_full, tab_full, tab_full],
        out_specs=[pair_full, pair_full, one_full],
        out_shape=[jax.ShapeDtypeStruct((bsz, t, B_KPAD), BF16), jax.ShapeDtypeStruct((bsz, t, B_KPAD), BF16),
                   jax.ShapeDtypeStruct((bsz, t, B_HEADS * B_VDIM), BF16)],
        scratch_shapes=[pltpu.VMEM((2, LANES, t), F32), pltpu.VMEM((8, t), F32)],
        compiler_params=_cparams(("parallel", "parallel")))(q, kpad, v, do, o, lse2, *tabs)


ANY = pl.BlockSpec(memory_space=pl.ANY)


def _all_gather(shard, name):
    def body(x_ref, out_ref, send_sems, recv_sems, local_sem):
        x, y, c = lax.axis_index("x"), lax.axis_index("y"), lax.axis_index("c")
        me, sibling = (x, y, c), (x, y, 1 - c)
        chips = [(1 - x, y), (x, 1 - y), (1 - x, 1 - y)]

        def rows(px, py, pc):
            return out_ref.at[4 * px + 2 * py + pc]

        def copy(k, block, to, src=None):
            return pltpu.make_async_remote_copy(
                src_ref=rows(*block) if src is None else src, dst_ref=rows(*block),
                send_sem=send_sems.at[k], recv_sem=recv_sems.at[k], device_id=to, device_id_type=MESH)

        mine = pltpu.make_async_copy(x_ref, rows(*me), local_sem)
        mine.start()
        first = [copy(0, me, sibling, src=x_ref)]
        first += [copy(1 + j, me, (*chip, c), src=x_ref) for j, chip in enumerate(chips)]
        for cp in first:
            cp.start()
        passed = [copy(4 + j, (*chip, c), sibling) for j, chip in enumerate(chips)]
        for j, chip in enumerate(chips):
            copy(1 + j, (*chip, c), me).wait_recv()
            passed[j].start()
        copy(0, sibling, me).wait_recv()
        for j, chip in enumerate(chips):
            copy(4 + j, (*chip, 1 - c), me).wait_recv()
        for cp in first + passed:
            cp.wait_send()
        mine.wait()

    return pl.pallas_call(
        body, name=name, in_specs=[ANY], out_specs=ANY,
        out_shape=jax.ShapeDtypeStruct((N_DEV,) + shard.shape, shard.dtype),
        scratch_shapes=[pltpu.SemaphoreType.DMA((7,)), pltpu.SemaphoreType.DMA((7,)), pltpu.SemaphoreType.DMA])(shard)


def _all_gather_weights(flat, wide, name):
    ns = wide.shape[1]

    def body(f_ref, w_ref, fo_ref, wo_ref, send_sems, recv_sems, local_sems):
        x, y, c = lax.axis_index("x"), lax.axis_index("y"), lax.axis_index("c")
        me, sibling = (x, y, c), (x, y, 1 - c)
        chips = [(1 - x, y), (x, 1 - y), (1 - x, 1 - y)]

        def place(a, px, py, pc):
            idx = 4 * px + 2 * py + pc
            if a == 0:
                return fo_ref.at[idx]
            return wo_ref.at[:, pl.ds(pl.multiple_of(idx * ns, LANES), ns)]

        def copy(a, k, block, to, src=None):
            return pltpu.make_async_remote_copy(
                src_ref=place(a, *block) if src is None else src, dst_ref=place(a, *block),
                send_sem=send_sems.at[a, k], recv_sem=recv_sems.at[a, k], device_id=to, device_id_type=MESH)

        own = (f_ref, w_ref)
        mine = [pltpu.make_async_copy(own[a], place(a, *me), local_sems.at[a]) for a in range(2)]
        first = []
        for a in range(2):
            mine[a].start()
            first.append(copy(a, 0, me, sibling, src=own[a]))
            first += [copy(a, 1 + j, me, (*chip, c), src=own[a]) for j, chip in enumerate(chips)]
        for cp in first:
            cp.start()
        passed = [[copy(a, 4 + j, (*chip, c), sibling) for j, chip in enumerate(chips)] for a in range(2)]
        for a in range(2):
            for j, chip in enumerate(chips):
                copy(a, 1 + j, (*chip, c), me).wait_recv()
                passed[a][j].start()
        for a in range(2):
            copy(a, 0, sibling, me).wait_recv()
            for j, chip in enumerate(chips):
                copy(a, 4 + j, (*chip, 1 - c), me).wait_recv()
        for cp in first + passed[0] + passed[1]:
            cp.wait_send()
        for cp in mine:
            cp.wait()

    return pl.pallas_call(
        body, name=name, in_specs=[ANY, ANY], out_specs=[ANY, ANY],
        out_shape=[jax.ShapeDtypeStruct((N_DEV,) + flat.shape, flat.dtype),
                   jax.ShapeDtypeStruct((wide.shape[0], N_DEV * ns), wide.dtype)],
        scratch_shapes=[pltpu.SemaphoreType.DMA((2, 7)), pltpu.SemaphoreType.DMA((2, 7)),
                        pltpu.SemaphoreType.DMA((2,))])(flat, wide)


N_CHIPS = 4


def _exchange_sibling(blocks, name):
    def body(g_ref, got_ref, send_sems, recv_sems):
        x, y, c = lax.axis_index("x"), lax.axis_index("y"), lax.axis_index("c")
        sends = [pltpu.make_async_remote_copy(
            src_ref=g_ref.at[q, 1 - c], dst_ref=got_ref.at[q], send_sem=send_sems.at[q],
            recv_sem=recv_sems.at[q], device_id=(x, y, 1 - c), device_id_type=MESH) for q in range(N_CHIPS)]
        for cp in sends:
            cp.start()
        for cp in sends:
            cp.wait_recv()
        for cp in sends:
            cp.wait_send()

    return pl.pallas_call(
        body, name=name, in_specs=[ANY], out_specs=ANY,
        out_shape=jax.ShapeDtypeStruct((N_CHIPS,) + blocks.shape[2:], blocks.dtype),
        scratch_shapes=[pltpu.SemaphoreType.DMA((N_CHIPS,)), pltpu.SemaphoreType.DMA((N_CHIPS,))])(blocks)


def _exchange_chips(parts, name):
    def body(p_ref, out_ref, send_sems, recv_sems, local_sem):
        x, y, c = lax.axis_index("x"), lax.axis_index("y"), lax.axis_index("c")
        me = 2 * x + y

        def peer(k):
            return (1 - x if k & 2 else x, 1 - y if k & 1 else y)

        def copy(k):
            px, py = peer(k)
            return pltpu.make_async_remote_copy(
                src_ref=p_ref.at[2 * px + py], dst_ref=out_ref.at[me], send_sem=send_sems.at[k - 1],
                recv_sem=recv_sems.at[k - 1], device_id=(px, py, c), device_id_type=MESH)

        def arrival(k):
            px, py = peer(k)
            slot = out_ref.at[2 * px + py]
            return pltpu.make_async_remote_copy(
                src_ref=slot, dst_ref=slot, send_sem=send_sems.at[k - 1], recv_sem=recv_sems.at[k - 1],
                device_id=(px, py, c), device_id_type=MESH)

        mine = pltpu.make_async_copy(p_ref.at[me], out_ref.at[me], local_sem)
        mine.start()
        sends = [copy(k) for k in range(1, N_CHIPS)]
        for cp in sends:
            cp.start()
        for k in range(1, N_CHIPS):
            arrival(k).wait_recv()
        for cp in sends:
            cp.wait_send()
        mine.wait()

    return pl.pallas_call(
        body, name=name, in_specs=[ANY], out_specs=ANY,
        out_shape=jax.ShapeDtypeStruct(parts.shape, parts.dtype),
        scratch_shapes=[pltpu.SemaphoreType.DMA((N_CHIPS - 1,)), pltpu.SemaphoreType.DMA((N_CHIPS - 1,)),
                        pltpu.SemaphoreType.DMA])(parts)


def _add_pairs(blocks, got, core, *, tr, name):
    q, r, c = got.shape

    def body(core_ref, a_ref, b_ref, o_ref):
        o_ref[...] = (a_ref[...].astype(F32) + b_ref[...].astype(F32)).astype(o_ref.dtype)

    spec = pl.BlockSpec((q, tr, c), lambda i, core_ref: (0, i, 0))
    mine = pl.BlockSpec((q, None, tr, c), lambda i, core_ref: (0, core_ref[0], i, 0))
    return pl.pallas_call(
        body, name=name,
        grid_spec=pltpu.PrefetchScalarGridSpec(num_scalar_prefetch=1, grid=(r // tr,), in_specs=[mine, spec],
                                               out_specs=spec),
        out_shape=jax.ShapeDtypeStruct(got.shape, BF16), compiler_params=_cparams(("parallel",)))(
            core, blocks, got)


def _sum_slots(slots, *, tr, name):
    n_slots, r, c = slots.shape

    def body(s_ref, o_ref):
        acc = s_ref[0].astype(F32)
        for s in range(1, n_slots):
            acc = acc + s_ref[s].astype(F32)
        o_ref[...] = acc

    return pl.pallas_call(
        body, name=name, grid=(r // tr,),
        in_specs=[pl.BlockSpec((n_slots, tr, c), lambda i: (0, i, 0))],
        out_specs=pl.BlockSpec((tr, c), lambda i: (i, 0)),
        out_shape=jax.ShapeDtypeStruct((r, c), F32),
        compiler_params=_cparams(("parallel",)))(slots)


def _adamw_math(w_t, g_t, m_t, v_t):
    m_n = ADAM_B1 * m_t + (1.0 - ADAM_B1) * g_t
    v_n = ADAM_B2 * v_t + (1.0 - ADAM_B2) * (g_t * g_t)
    m_hat = m_n / (1.0 - ADAM_B1 ** ADAM_STEP)
    v_hat = v_n / (1.0 - ADAM_B2 ** ADAM_STEP)
    delta = -ADAM_LR * (m_hat / (jnp.sqrt(v_hat) + ADAM_EPS) + ADAM_WD * w_t)
    return delta, m_n, v_n


def _adamw(w, g, m, v, name):
    shape = w.shape
    cols = shape[-1]
    args = [a.reshape(-1, cols) for a in (w, g, m, v)]
    rows = args[0].shape[0]
    tm = 256 if rows % 256 == 0 else rows
    delta, m_n, v_n = _rowwise(_adamw_math, args, [], [(cols, F32)] * 3, tm=tm, name=name)
    return delta.reshape(shape), m_n.reshape(shape), v_n.reshape(shape)


def _adamw_small(ws, gs, ms, vs, name):
    k = len(ws)
    args = [a.reshape(-1, a.shape[-1]) for group in (ws, gs, ms, vs) for a in group]

    def body(*refs):
        ins, outs = refs[:4 * k], refs[4 * k:]
        for i in range(k):
            res = _adamw_math(*[ins[j * k + i][...] for j in range(4)])
            for j in range(3):
                outs[j * k + i][...] = res[j]

    res = pl.pallas_call(
        body, name=name, out_shape=[jax.ShapeDtypeStruct(a.shape, F32) for a in args[:k]] * 3,
        compiler_params=pltpu.CompilerParams(vmem_limit_bytes=VMEM_LIMIT))(*args)
    return [[res[j * k + i].reshape(ws[i].shape) for i in range(k)] for j in range(3)]


def _local_step(x, positions, target, w):
    bsz, t, _ = x.shape
    n = bsz * t
    tm = 256
    mm = 512
    x2 = x.reshape(n, D_MODEL)
    tgt2 = target.reshape(n, D_MODEL)
    pos = positions.reshape(n)
    tb2 = _rope_tables(pos, B_ROPE_THETA, B_ROPE, B_NOPE, "rope_tables_b")
    tabs_b = [a.reshape(bsz, t, LANES) for a in tb2]

    w_a_in = w["a_w_in"]
    w_a_out = w["a_w_out"]
    w_down = w["kv_w_down"]
    w_down_p = jnp.zeros((D_MODEL, 3 * LANES), BF16).at[:, :B_KV_LORA].set(w_down[:, :B_KV_LORA])
    w_down_p = w_down_p.at[:, B_KV_LORA + B_NOPE:B_KV_LORA + B_QK_DIM].set(w_down[:, B_KV_LORA:])
    wu = w["kv_w_up"].reshape(B_KV_LORA, B_HEADS, B_NOPE + B_VDIM)
    w_upk = jnp.pad(wu[:, :, :B_NOPE], ((0, 0), (0, 0), (0, LANES - B_NOPE))).reshape(B_KV_LORA, B_KPAD)
    w_upv = wu[:, :, B_NOPE:].reshape(B_KV_LORA, B_HEADS * B_VDIM)
    w_b_in = w["b_w_in"]
    w_q_p = jnp.pad(w["b_w_q_up"].reshape(B_Q_LORA, B_HEADS, B_QK_DIM),
                    ((0, 0), (0, 0), (0, LANES - B_QK_DIM))).reshape(B_Q_LORA, B_KPAD)
    w_b_out = w["b_w_out"]

    def tab_extras(tabs2, rows):
        return [(a, (rows, LANES), lambda j, i, kk: (i, 0)) for a in tabs2]

    (hn_a,) = _rowwise(lambda xt, g: (_rms(xt, g),), [x2], [w["a_pre_norm"]], [(D_MODEL, BF16)],
                       tm=tm, name="a_pre_norm")

    def rope_epilogue(acc, o_ref, rows, c_ref, sp_ref, sm_ref):
        c, sp, sm = c_ref[rows, :], sp_ref[rows, :], sm_ref[rows, :]
        for h in range(acc.shape[1] // LANES):
            hs = slice(h * LANES, (h + 1) * LANES)
            o_ref[rows, hs] = _rope(acc[:, hs], c, sp, sm).astype(BF16)

    def to_group(a, d):
        if d == 1:
            return a
        return a.reshape(bsz, t // d, d, a.shape[-1]).transpose(0, 2, 1, 3).reshape(n, a.shape[-1])

    def from_group(a, d):
        if d == 1:
            return a
        return a.reshape(bsz, d, t // d, a.shape[-1]).transpose(0, 2, 1, 3).reshape(n, a.shape[-1])

    def rows_to_cols(r, d):
        return r.reshape(bsz, d, A_HEADS, t // d).transpose(0, 3, 1, 2).reshape(n, A_HEADS)

    def cols_to_rows(cc, d):
        return cc.reshape(bsz, t // d, d, A_HEADS).transpose(0, 2, 3, 1).reshape(bsz * d, A_HEADS, t // d)

    z_a = _matmul(hn_a, w_a_in, b_cols=(A_QKV, A_WIDTH), out_dtype=F32, tm=mm, tn=A_WIDTH, name="a_gate")
    hn_g, tabs_g, qk_g, v_g, o_g, lse_g = [], [], [], [], [], []
    for g, d in enumerate(A_DILATIONS):
        hn_g.append(to_group(hn_a, d))
        tabs_g.append(_rope_tables(to_group(pos.reshape(n, 1), d).reshape(n), A_ROPE_THETA, A_ROT_DIM, 0,
                                   f"rope_tables_a_g{g}"))
        col0 = g * 3 * A_WIDTH
        qk = _matmul(hn_g[g], w_a_in, b_cols=(col0, 2 * A_WIDTH), out_dtype=BF16, tm=mm, tn=A_WIDTH,
                     name=f"a_qk_g{g}", epilogue=rope_epilogue, epilogue_rows=EPILOGUE_ROWS,
                     extras=tab_extras(tabs_g[g], mm))
        v = _matmul(hn_g[g], w_a_in, b_cols=(col0 + 2 * A_WIDTH, A_WIDTH), out_dtype=BF16, tm=mm, tn=A_WIDTH,
                    name=f"a_v_g{g}")
        qk_g.append(qk.reshape(bsz * d, t // d, 2 * A_WIDTH))
        v_g.append(v.reshape(bsz * d, t // d, A_WIDTH))
        o, lse = _attn_a_fwd(qk_g[g], v_g[g], f"attn_a_fwd_g{g}")
        o_g.append(from_group(o.reshape(n, A_WIDTH), d))
        lse_g.append(rows_to_cols(lse, d))

    def merge_fn(o0, o1, o2, l0, l1, l2, z):
        lmax = jnp.maximum(jnp.maximum(l0, l1), l2)
        e0, e1, e2 = jnp.exp2(l0 - lmax), jnp.exp2(l1 - lmax), jnp.exp2(l2 - lmax)
        den = e0 + e1 + e2
        w0, w1, w2 = e0 / den, e1 / den, e2 / den
        parts = []
        for h in range(A_HEADS):
            hs = slice(h * A_HEAD_DIM, (h + 1) * A_HEAD_DIM)
            parts.append(w0[:, h:h + 1] * o0[:, hs] + w1[:, h:h + 1] * o1[:, hs] + w2[:, h:h + 1] * o2[:, hs])
        o = jnp.concatenate(parts, axis=1)
        return o * _silu(z), o, lmax + jnp.log2(den)

    y_a, o_a2, lse_a = _rowwise(merge_fn, [*o_g, *lse_g, z_a], [],
                                [(A_WIDTH, BF16), (A_WIDTH, F32), (A_HEADS, F32)], tm=tm, name="a_merge_gate")
    out_a = _matmul(y_a, w_a_out, out_dtype=F32, tm=mm, tn=D_MODEL, name="a_out")

    def mid_fn(xt, out, g_post, g_kv, g_b):
        h1 = xt + _rms(out, g_post)
        return h1, _rms(h1, g_kv), _rms(h1, g_b)

    h1, hn_kv, hn_b = _rowwise(mid_fn, [x2, out_a], [w["a_post_norm"], w["kv_norm"], w["b_pre_norm"]],
                               [(D_MODEL, F32), (D_MODEL, BF16), (D_MODEL, BF16)], tm=tm, name="a_post_norm")

    ckr = _matmul(hn_kv, w_down_p, out_dtype=F32, tm=mm, tn=3 * LANES, name="kv_down")

    def latent_fn(ck, c, sp, sm, g):
        return _rms(ck[:, :B_KV_LORA], g), _rope(ck[:, B_KV_LORA:], c, sp, sm)

    c_kv, k_rope = _rowwise(latent_fn, [ckr, *tb2], [w["kv_latent_norm"]], [(B_KV_LORA, BF16), (LANES, F32)],
                            tm=tm, name="kv_latent_norm")

    def kpad_epilogue(acc, o_ref, rows, kr_ref):
        kr = kr_ref[rows, :]
        for h in range(acc.shape[1] // LANES):
            hs = slice(h * LANES, (h + 1) * LANES)
            o_ref[rows, hs] = (acc[:, hs] + kr).astype(BF16)

    kpad = _matmul(c_kv, w_upk, out_dtype=BF16, tm=mm, tn=1024, name="kv_up_k", epilogue=kpad_epilogue,
                   epilogue_rows=EPILOGUE_ROWS,
                   extras=[(k_rope, (mm, LANES), lambda j, i, kk: (i, 0))])
    v_b = _matmul(c_kv, w_upv, out_dtype=BF16, tm=mm, tn=1024, name="kv_up_v")

    proj_b = _matmul(hn_b, w_b_in, out_dtype=F32, tm=mm, tn=w_b_in.shape[1], name="b_in")
    (c_q,) = _rowwise(lambda p, g: (_rms(p[:, :B_Q_LORA], g),), [proj_b], [w["b_q_norm"]], [(B_Q_LORA, BF16)],
                      tm=tm, name="b_q_norm")

    q_b = _matmul(c_q, w_q_p, out_dtype=BF16, tm=mm, tn=1024, name="b_q_up", epilogue=rope_epilogue,
                  epilogue_rows=EPILOGUE_ROWS,
                  extras=tab_extras(tb2, mm))
    q_b3, kpad3, v_b3 = q_b.reshape(bsz, t, B_KPAD), kpad.reshape(bsz, t, B_KPAD), v_b.reshape(bsz, t, -1)
    w_vt = jnp.pad(wu[:, :, B_NOPE:], ((0, 0), (0, 0), (0, LANES - B_VDIM))).reshape(B_KV_LORA, B_KPAD)
    vt1 = _value_heads_t(c_kv, w_vt, bsz, t)
    o_b, lse_b = _attn_b_fwd(q_b3, kpad3, vt1)
    o_b2 = o_b.reshape(n, -1)
    (y_b,) = _rowwise(lambda o, p: (o * _silu(p[:, B_Q_LORA:]),), [o_b2, proj_b], [], [(D_MODEL, BF16)],
                      tm=tm, name="b_gate_mul")
    out_b = _matmul(y_b, w_b_out, out_dtype=F32, tm=mm, tn=D_MODEL, name="b_out")

    def head_fn(h1t, out, tgt, g):
        e = h1t + _rms(out, g) - tgt
        loss_row = 0.5 * jnp.mean(e * e, axis=-1, keepdims=True)
        dh2 = e * (1.0 / D_MODEL)
        d_out, dg = _rms_bwd(out, g, dh2)
        return dh2, d_out, dg, jnp.broadcast_to(loss_row * (1.0 / LANES), (loss_row.shape[0], LANES))

    dh2, d_out_b, dg_b_post, loss_acc = _rowwise(
        head_fn, [h1, out_b, tgt2], [w["b_post_norm"]], [(D_MODEL, F32), (D_MODEL, BF16)], [D_MODEL, LANES],
        tm=tm, name="loss_head")

    dy_b = _matmul(d_out_b, w_b_out, tb=True, out_dtype=BF16, tm=mm, tn=D_MODEL, name="b_out_dx")
    gw_b_out = _matmul(y_b, d_out_b, ta=True, out_dtype=BF16, tm=mm, tn=D_MODEL, tk=2048, name="b_out_dw")

    def gate_b_bwd(dy, o, p):
        z = p[:, B_Q_LORA:]
        return dy * _silu(z), dy * o * _silu_grad(z)

    do_b, dz_b = _rowwise(gate_b_bwd, [dy_b, o_b2, proj_b], [], [(D_MODEL, BF16), (D_MODEL, F32)],
                          tm=tm, name="b_gate_bwd")
    do_b3 = do_b.reshape(bsz, t, -1)
    dq_b, dk_b, dv_b = _attn_b_bwd(q_b3, kpad3, v_b3, do_b3, o_b, lse_b, tabs_b)
    dq_b2, dk_b2, dv_b2 = dq_b.reshape(n, B_KPAD), dk_b.reshape(n, B_KPAD), dv_b.reshape(n, -1)

    dc_kv = _matmul(dk_b2, w_upk, tb=True, out_dtype=F32, tm=mm, tn=B_KV_LORA, name="kv_up_k_dx")
    dc_kv = _matmul(dv_b2, w_upv, tb=True, out_dtype=BF16, tm=mm, tn=B_KV_LORA, name="kv_up_v_dx",
                    epilogue=_add_epilogue, extras=[(dc_kv, (mm, B_KV_LORA), lambda j, i, kk: (i, j))])
    gw_upk = _matmul(c_kv, dk_b2, ta=True, out_dtype=BF16, tm=B_KV_LORA, tn=1024, tk=2048, name="kv_up_k_dw")
    gw_upv = _matmul(c_kv, dv_b2, ta=True, out_dtype=BF16, tm=B_KV_LORA, tn=1024, tk=2048, name="kv_up_v_dw")

    def latent_bwd(ck, dck, dk, c, sp, sm, g):
        d1, dg = _rms_bwd(ck[:, :B_KV_LORA], g, dck)
        ksum = dk[:, :LANES].astype(F32)
        for h in range(1, B_HEADS):
            ksum = ksum + dk[:, h * LANES:(h + 1) * LANES].astype(F32)
        lane = lax.broadcasted_iota(jnp.int32, ksum.shape, 1)
        ksum = jnp.where((lane >= B_NOPE) & (lane < B_QK_DIM), ksum, 0.0)
        return jnp.concatenate([d1, _rope_t(ksum, c, sp, sm)], axis=1), dg

    dckr, dg_latent = _rowwise(latent_bwd, [ckr, dc_kv, dk_b2, *tb2], [w["kv_latent_norm"]],
                               [(3 * LANES, BF16)], [B_KV_LORA], tm=tm, name="kv_latent_bwd")
    dhn_kv = _matmul(dckr, w_down_p, tb=True, out_dtype=BF16, tm=mm, tn=D_MODEL, name="kv_down_dx")
    gw_down_p = _matmul(hn_kv, dckr, ta=True, out_dtype=BF16, tm=mm, tn=3 * LANES, tk=2048, name="kv_down_dw")

    dc_q = _matmul(dq_b2, w_q_p, tb=True, out_dtype=BF16, tm=mm, tn=B_Q_LORA, name="b_q_up_dx")
    gw_q_p = _matmul(c_q, dq_b2, ta=True, out_dtype=BF16, tm=B_Q_LORA, tn=1024, tk=2048, name="b_q_up_dw")

    def q_norm_bwd(p, dcq, dz, g):
        d1, dg = _rms_bwd(p[:, :B_Q_LORA], g, dcq)
        return jnp.concatenate([d1, dz], axis=1), dg

    dproj_b, dg_q_norm = _rowwise(q_norm_bwd, [proj_b, dc_q, dz_b], [w["b_q_norm"]],
                                  [(w_b_in.shape[1], BF16)], [B_Q_LORA], tm=tm, name="b_q_norm_bwd")
    dhn_b = _matmul(dproj_b, w_b_in, tb=True, out_dtype=BF16, tm=mm, tn=D_MODEL, name="b_in_dx")
    gw_b_in = _matmul(hn_b, dproj_b, ta=True, out_dtype=BF16, tm=mm, tn=w_b_in.shape[1], tk=2048, name="b_in_dw")

    def mid_bwd(h1t, dh2t, dkv, db, g_kv, g_b, g_post, out):
        dxa, ra = _rms_bwd(h1t, g_kv, dkv)
        dxb, rb = _rms_bwd(h1t, g_b, db)
        dh1 = dh2t + dxa + dxb
        d_out, rp = _rms_bwd(out, g_post, dh1)
        return dh1, d_out, ra, rb, rp

    def mid_bwd_fn(h1t, dh2t, dkv, db, out, g_kv, g_b, g_post):
        return mid_bwd(h1t, dh2t, dkv, db, g_kv, g_b, g_post, out)

    dh1, d_out_a, dg_kv, dg_b_pre, dg_a_post = _rowwise(
        mid_bwd_fn, [h1, dh2, dhn_kv, dhn_b, out_a], [w["kv_norm"], w["b_pre_norm"], w["a_post_norm"]],
        [(D_MODEL, F32), (D_MODEL, BF16)], [D_MODEL] * 3, tm=tm, name="mid_bwd")

    dy_a = _matmul(d_out_a, w_a_out, tb=True, out_dtype=BF16, tm=mm, tn=A_WIDTH, name="a_out_dx")
    gw_a_out = _matmul(y_a, d_out_a, ta=True, out_dtype=BF16, tm=mm, tn=D_MODEL, tk=2048, name="a_out_dw")

    def gate_a_bwd(dy, o, z):
        do = dy * _silu(z)
        prod = do * o
        lane = lax.broadcasted_iota(jnp.int32, (prod.shape[0], A_HEADS), 1)
        dsum = jnp.zeros((prod.shape[0], A_HEADS), F32)
        for h in range(A_HEADS):
            col = jnp.sum(prod[:, h * A_HEAD_DIM:(h + 1) * A_HEAD_DIM], axis=1, keepdims=True)
            dsum = jnp.where(lane == h, col, dsum)
        return do, dy * o * _silu_grad(z), dsum

    do_a, dz_a, dsum_a = _rowwise(gate_a_bwd, [dy_a, o_a2, z_a], [],
                                  [(A_WIDTH, BF16), (A_WIDTH, BF16), (A_HEADS, F32)], tm=tm, name="a_gate_bwd")
    dhn_a = _matmul(dz_a, w_a_in, b_cols=(A_QKV, A_WIDTH), tb=True, out_dtype=F32, tm=mm, tn=D_MODEL,
                    name="a_gate_dx")
    gw_a_in, dhn_groups = None, []
    for g, d in enumerate(A_DILATIONS):
        s_n, ln = bsz * d, t // d
        dqkv = _attn_a_bwd(qk_g[g], v_g[g], to_group(do_a, d).reshape(s_n, ln, A_WIDTH), cols_to_rows(lse_a, d),
                           cols_to_rows(dsum_a, d), [a.reshape(s_n, ln, LANES) for a in tabs_g[g]],
                           f"attn_a_bwd_g{g}").reshape(n, 3 * A_WIDTH)
        w_cols = (g * 3 * A_WIDTH, 3 * A_WIDTH)
        if d == 1:
            dhn_a = _matmul(dqkv, w_a_in, b_cols=w_cols, tb=True, out_dtype=F32, tm=mm, tn=D_MODEL,
                            name=f"a_qkv_dx_g{g}", epilogue=_add_epilogue,
                            extras=[(dhn_a, (mm, D_MODEL), lambda j, i, kk: (i, j))])
        else:
            dhn_groups.append(from_group(_matmul(dqkv, w_a_in, b_cols=w_cols, tb=True, out_dtype=BF16, tm=mm,
                                                 tn=D_MODEL, name=f"a_qkv_dx_g{g}"), d))
        gw_a_in = _matmul(hn_g[g], dqkv, ta=True, out_dtype=BF16, tm=mm, tn=1024, tk=2048, name=f"a_qkv_dw_g{g}",
                          out_into=(gw_a_in, g * 3 * A_WIDTH, A_QKV + A_WIDTH))
    gw_a_in = _matmul(hn_a, dz_a, ta=True, out_dtype=BF16, tm=mm, tn=1024, tk=2048, name="a_gate_dw",
                      out_into=(gw_a_in, A_QKV, A_QKV + A_WIDTH))

    def first_bwd(xt, dhn, dhn_1, dhn_2, dh1t, g):
        dx, dg = _rms_bwd(xt, g, dhn + dhn_1 + dhn_2)
        return dh1t + dx, dg

    grad_x, dg_a_pre = _rowwise(first_bwd, [x2, dhn_a, *dhn_groups, dh1], [w["a_pre_norm"]], [(D_MODEL, F32)],
                                [D_MODEL], tm=tm, name="a_pre_norm_bwd")

    gw_down = jnp.concatenate([gw_down_p[:, :B_KV_LORA], gw_down_p[:, B_KV_LORA + B_NOPE:B_KV_LORA + B_QK_DIM]], axis=1)
    gw_up = jnp.concatenate([gw_upk.reshape(B_KV_LORA, B_HEADS, LANES)[:, :, :B_NOPE],
                             gw_upv.reshape(B_KV_LORA, B_HEADS, B_VDIM)], axis=2).reshape(B_KV_LORA, -1)
    gw_q_up = gw_q_p.reshape(B_Q_LORA, B_HEADS, LANES)[:, :, :B_QK_DIM].reshape(B_Q_LORA, -1)
    grads = {"a_w_in": gw_a_in, "a_w_out": gw_a_out, "kv_w_down": gw_down, "kv_w_up": gw_up,
             "b_w_in": gw_b_in, "b_w_q_up": gw_q_up, "b_w_out": gw_b_out}
    gains = {"a_pre_norm": dg_a_pre, "a_post_norm": dg_a_post, "kv_norm": dg_kv, "kv_latent_norm": dg_latent,
             "b_pre_norm": dg_b_pre, "b_q_norm": dg_q_norm, "b_post_norm": dg_b_post}
    gains = {k: jnp.sum(a, axis=0) for k, a in gains.items()}
    return jnp.sum(loss_acc), grad_x.reshape(bsz, t, D_MODEL), grads, gains


WEIGHT_ORDER = ("a_pre_norm", "a_w_in", "a_w_out", "a_post_norm", "kv_norm", "kv_w_down", "kv_latent_norm",
                "kv_w_up", "b_pre_norm", "b_w_in", "b_q_norm", "b_w_q_up", "b_w_out", "b_post_norm")
MATRICES = (("a_w_in", 1024, 10240, 1), ("a_w_out", 1024, 1024, 0), ("kv_w_down", 1024, 288, 0),
            ("kv_w_up", 256, 2048, 1), ("b_w_in", 1024, 1408, 1), ("b_w_q_up", 384, 1536, 1),
            ("b_w_out", 1024, 1024, 0))
SHARDED_GAINS = ("a_pre_norm", "a_post_norm")
GAIN_WIDTHS = (("a_pre_norm", 1024), ("a_post_norm", 1024), ("kv_norm", 1024), ("kv_latent_norm", 256),
               ("b_pre_norm", 1024), ("b_q_norm", 384), ("b_post_norm", 1024))
GAIN_ROWS = 48


def _shard_rows(rows, cols):
    return rows * cols // (N_DEV * LANES)


def _whole_from_blocks(blocks, rows, cols, axis):
    if axis == 1:
        return blocks.reshape(N_DEV, rows, cols // N_DEV).transpose(1, 0, 2).reshape(rows, cols)
    return blocks.reshape(rows, cols)


def _blocks_from_whole(whole, rows, cols, axis):
    if axis == 1:
        whole = whole.reshape(rows, N_DEV, cols // N_DEV).transpose(1, 0, 2)
    return whole.reshape(N_DEV, -1, LANES)


def kernel(x, positions, a_pre_norm, a_w_in, a_w_out, a_post_norm, kv_norm, kv_w_down, kv_latent_norm, kv_w_up, b_pre_norm, b_w_in, b_q_norm, b_w_q_up, b_w_out, b_post_norm, loss_target, m_a_pre_norm, m_a_w_in, m_a_w_out, m_a_post_norm, m_kv_norm, m_kv_w_down, m_kv_latent_norm, m_kv_w_up, m_b_pre_norm, m_b_w_in, m_b_q_norm, m_b_w_q_up, m_b_w_out, m_b_post_norm, v_a_pre_norm, v_a_w_in, v_a_w_out, v_a_post_norm, v_kv_norm, v_kv_w_down, v_kv_latent_norm, v_kv_w_up, v_b_pre_norm, v_b_w_in, v_b_q_norm, v_b_w_q_up, v_b_w_out, v_b_post_norm):
    weights = dict(a_pre_norm=a_pre_norm, a_w_in=a_w_in, a_w_out=a_w_out, a_post_norm=a_post_norm, kv_norm=kv_norm,
                   kv_w_down=kv_w_down, kv_latent_norm=kv_latent_norm, kv_w_up=kv_w_up, b_pre_norm=b_pre_norm,
                   b_w_in=b_w_in, b_q_norm=b_q_norm, b_w_q_up=b_w_q_up, b_w_out=b_w_out, b_post_norm=b_post_norm)
    m_in = dict(a_pre_norm=m_a_pre_norm, a_w_in=m_a_w_in, a_w_out=m_a_w_out, a_post_norm=m_a_post_norm,
                kv_norm=m_kv_norm, kv_w_down=m_kv_w_down, kv_latent_norm=m_kv_latent_norm, kv_w_up=m_kv_w_up,
                b_pre_norm=m_b_pre_norm, b_w_in=m_b_w_in, b_q_norm=m_b_q_norm, b_w_q_up=m_b_w_q_up,
                b_w_out=m_b_w_out, b_post_norm=m_b_post_norm)
    v_in = dict(a_pre_norm=v_a_pre_norm, a_w_in=v_a_w_in, a_w_out=v_a_w_out, a_post_norm=v_a_post_norm,
                kv_norm=v_kv_norm, kv_w_down=v_kv_w_down, kv_latent_norm=v_kv_latent_norm, kv_w_up=v_kv_w_up,
                b_pre_norm=v_b_pre_norm, b_w_in=v_b_w_in, b_q_norm=v_b_q_norm, b_w_q_up=v_b_w_q_up,
                b_w_out=v_b_w_out, b_post_norm=v_b_post_norm)
    me = 4 * lax.axis_index("x") + 2 * lax.axis_index("y") + lax.axis_index("c")

    wide = MATRICES[0][0]
    flat = jnp.concatenate([weights[name].astype(BF16).reshape(-1, LANES) for name, _, _, _ in MATRICES[1:]], axis=0)
    gathered, w_wide = _all_gather_weights(flat, weights[wide][0].astype(BF16), "gather_weights")
    whole = {wide: w_wide}
    off = 0
    for name, rows, cols, axis in MATRICES[1:]:
        nr = _shard_rows(rows, cols)
        whole[name] = _whole_from_blocks(gathered[:, off:off + nr], rows, cols, axis)
        off += nr
    gain_shard = jnp.concatenate([weights[name].reshape(1, LANES) for name in SHARDED_GAINS]
                                 + [jnp.zeros((8 - len(SHARDED_GAINS), LANES), F32)], axis=0)
    gain_blocks = _all_gather(gain_shard, "gather_gains")
    for i, name in enumerate(SHARDED_GAINS):
        whole[name] = gain_blocks[:, i, :].reshape(1, D_MODEL)
    for name in ("kv_norm", "kv_latent_norm", "b_pre_norm", "b_q_norm", "b_post_norm"):
        whole[name] = weights[name].reshape(1, -1)

    loss_part, grad_x, grads, gains = _local_step(x, positions, loss_target, whole)

    blocks = jnp.concatenate([_blocks_from_whole(grads[name], rows, cols, axis).astype(BF16)
                              for name, rows, cols, axis in MATRICES], axis=1)
    blocks = blocks.reshape(N_CHIPS, 2, blocks.shape[1], LANES)
    got = _exchange_sibling(blocks, "scatter_grads_core")
    core = lax.axis_index("c").astype(jnp.int32).reshape(1)
    landed = _exchange_chips(_add_pairs(blocks, got, core, tr=2512, name="add_core_grads"), "scatter_grads_chip")
    summed = _sum_slots(landed, tr=2512, name="sum_grads")
    grad_out = {}
    off = 0
    for name, rows, cols, axis in MATRICES:
        nr = _shard_rows(rows, cols)
        grad_out[name] = summed[off:off + nr].reshape(weights[name].shape)
        off += nr

    vec = jnp.concatenate([gains[name] for name, _ in GAIN_WIDTHS] + [jnp.full((LANES,), loss_part, F32)])
    vec = jnp.pad(vec, (0, GAIN_ROWS * LANES - vec.shape[0])).reshape(GAIN_ROWS, LANES)
    total = _sum_slots(_all_gather(vec, "gather_gain_grads"), tr=GAIN_ROWS, name="sum_gain_grads").reshape(-1)
    off = 0
    for name, width in GAIN_WIDTHS:
        g = total[off:off + width]
        if name in SHARDED_GAINS:
            g = lax.dynamic_slice(g, (me * LANES,), (LANES,))
        grad_out[name] = g.reshape(weights[name].shape)
        off += width
    loss = total[off]

    deltas, new_m, new_v = {}, {}, {}
    big = "a_w_in"
    deltas[big], new_m[big], new_v[big] = _adamw(weights[big], grad_out[big], m_in[big], v_in[big], "adamw_" + big)
    small = [name for name in WEIGHT_ORDER if name != big]
    res = _adamw_small(*[[d[name] for name in small] for d in (weights, grad_out, m_in, v_in)], "adamw_small")
    for out, vals in zip((deltas, new_m, new_v), res):
        out.update(zip(small, vals))
    return (loss, grad_x, *[grad_out[k] for k in WEIGHT_ORDER], *[deltas[k] for k in WEIGHT_ORDER],
            *[new_m[k] for k in WEIGHT_ORDER], *[new_v[k] for k in WEIGHT_ORDER])
```

```python
import jax
import jax.numpy as jnp
from jax import lax
from jax.experimental import pallas as pl
from jax.experimental.pallas import tpu as pltpu

F32 = jnp.float32
BF16 = jnp.bfloat16

N_DEV = 8
D_MODEL = 1024
NORM_EPS = 1e-6
A_GROUPS = 3
A_DILATIONS = (1, 4, 16)
A_HEADS = 8
A_HEAD_DIM = 128
A_WIDTH = 1024
A_ROT_DIM = 32
A_ROPE_THETA = 500000.0
A_QKV = A_GROUPS * 3 * A_WIDTH
B_HEADS = 16
B_NOPE = 64
B_ROPE = 32
B_QK_DIM = 96
B_VDIM = 64
B_Q_LORA = 384
B_KV_LORA = 256
B_ROPE_THETA = 10000.0
B_KPAD = B_HEADS * 128
ADAM_LR = 0.001
ADAM_B1 = 0.9
ADAM_B2 = 0.999
ADAM_EPS = 1e-08
ADAM_WD = 0.01
ADAM_STEP = 10

LANES = 128
BAND = 128
EPILOGUE_ROWS = 128
NEG = -1e30
VMEM_LIMIT = 56 * 1024 * 1024
MESH = pl.DeviceIdType.MESH


def _cparams(sem):
    return pltpu.CompilerParams(dimension_semantics=sem, vmem_limit_bytes=VMEM_LIMIT)


def _rowwise(fn, rows, bcast, outs, accs=(), *, tm, name):
    n = rows[0].shape[0]
    nr, nb, no = len(rows), len(bcast), len(outs)

    def body(*refs):
        res = fn(*[r[...] for r in refs[:nr + nb]])
        out_refs = refs[nr + nb:nr + nb + no]
        acc_refs = refs[nr + nb + no:]
        for r, v in zip(out_refs, res[:no]):
            r[...] = v.astype(r.dtype)
        if acc_refs:
            @pl.when(pl.program_id(0) == 0)
            def _():
                for r in acc_refs:
                    r[...] = jnp.zeros_like(r)
            for r, v in zip(acc_refs, res[no:]):
                r[...] += v.reshape(tm // 8, 8, v.shape[-1]).sum(axis=0)

    in_specs = [pl.BlockSpec((tm, a.shape[1]), lambda i: (i, 0)) for a in rows]
    in_specs += [pl.BlockSpec(a.shape, lambda i: (0, 0)) for a in bcast]
    out_specs = [pl.BlockSpec((tm, c), lambda i: (i, 0)) for c, _ in outs]
    out_specs += [pl.BlockSpec((8, c), lambda i: (0, 0)) for c in accs]
    out_shape = [jax.ShapeDtypeStruct((n, c), dt) for c, dt in outs]
    out_shape += [jax.ShapeDtypeStruct((8, c), F32) for c in accs]
    return pl.pallas_call(
        body, name=name, grid=(n // tm,), in_specs=in_specs, out_specs=out_specs, out_shape=out_shape,
        compiler_params=_cparams(("arbitrary",)))(*rows, *bcast)


def _matmul(a, b, *, out_dtype, tm, tn, tk=None, name, epilogue=None, extras=(), ta=False, tb=False,
            epilogue_rows=None, b_cols=None, out_into=None):
    epilogue_rows = epilogue_rows or tm
    k, m = a.shape[::-1] if not ta else a.shape
    col0, width = b_cols or (0, b.shape[1])
    n = b.shape[0] if tb else width
    tk = tk or k
    nk = k // tk
    b_off = col0 // (tk if tb else tn)
    assert col0 % (tk if tb else tn) == 0 and (k == width if tb else True)
    ne = len(extras)
    dot = _dot_tn if ta else (_dot_nt if tb else _dot)
    assert epilogue is None or (nk == 1 and not ta)

    def body(*refs):
        a_ref, b_ref = refs[:2]
        ex = refs[2:2 + ne]
        o_ref = refs[2 + ne + (1 if out_into is not None and out_into[0] is not None else 0)]
        if epilogue is not None:
            b_tile = b_ref[...].astype(BF16)
            for r0 in range(0, tm, epilogue_rows):
                rows = slice(r0, r0 + epilogue_rows)
                epilogue(dot(a_ref[rows, :].astype(BF16), b_tile), o_ref, rows, *ex)
            return
        part = dot(a_ref[...].astype(BF16), b_ref[...].astype(BF16))
        if nk == 1:
            o_ref[...] = part.astype(o_ref.dtype)
        else:
            acc_ref = refs[-1]
            kk = pl.program_id(2)

            @pl.when(kk == 0)
            def _():
                acc_ref[...] = part

            @pl.when(kk > 0)
            def _():
                acc_ref[...] += part

            @pl.when(kk == nk - 1)
            def _():
                o_ref[...] = acc_ref[...].astype(o_ref.dtype)

    a_spec = pl.BlockSpec((tk, tm), lambda j, i, kk: (kk, i)) if ta else pl.BlockSpec((tm, tk), lambda j, i, kk: (i, kk))
    b_spec = (pl.BlockSpec((tn, tk), lambda j, i, kk: (j, kk + b_off)) if tb
              else pl.BlockSpec((tk, tn), lambda j, i, kk: (kk, j + b_off)))
    in_specs = [a_spec, b_spec] + [pl.BlockSpec(bs, im) for _, bs, im in extras]
    operands = [a, b] + [e[0] for e in extras]
    aliases = {}
    if out_into is not None:
        prev, out0, n_total = out_into
        o_off = out0 // tn
        assert out0 % tn == 0
        if prev is not None:
            in_specs.append(ANY)
            operands.append(prev)
            aliases = {len(operands) - 1: 0}
    else:
        o_off, n_total = 0, n
    return pl.pallas_call(
        body, name=name, grid=(n // tn, m // tm, nk), in_specs=in_specs,
        out_specs=pl.BlockSpec((tm, tn), lambda j, i, kk: (i, j + o_off)),
        out_shape=jax.ShapeDtypeStruct((m, n_total), out_dtype), input_output_aliases=aliases,
        scratch_shapes=[pltpu.VMEM((tm, tn), F32)] if nk > 1 else [],
        compiler_params=_cparams(("parallel", "parallel", "arbitrary")))(*operands)


def _add_epilogue(acc, o_ref, rows, prev_ref):
    o_ref[rows, :] = (acc + prev_ref[rows, :]).astype(o_ref.dtype)


def _rope(x, c, sp, sm):
    return x * c + pltpu.roll(x, 16, 1) * sp + pltpu.roll(x, LANES - 16, 1) * sm


def _rope_t(dy, c, sp, sm):
    return dy * c + pltpu.roll(dy * sp, LANES - 16, 1) + pltpu.roll(dy * sm, 16, 1)


def _rope_tables(positions, theta, rot_dim, lane0, name):
    n = positions.shape[0]
    half = rot_dim // 2
    inv_freq = 1.0 / (theta ** (jnp.arange(half, dtype=F32) * (2.0 / rot_dim)))
    freq = jnp.concatenate([jnp.zeros((lane0,), F32), inv_freq, inv_freq,
                            jnp.zeros((LANES - lane0 - rot_dim,), F32)]).reshape(1, LANES)

    def fn(p, f):
        ang = p * f
        cos, sin = jnp.cos(ang), jnp.sin(ang)
        lane = lax.broadcasted_iota(jnp.int32, ang.shape, 1) - lane0
        first, second = (lane >= 0) & (lane < half), (lane >= half) & (lane < rot_dim)
        return jnp.where(first | second, cos, 1.0), jnp.where(second, sin, 0.0), jnp.where(first, -sin, 0.0)

    return _rowwise(fn, [positions.astype(F32).reshape(n, 1)], [freq], [(LANES, F32)] * 3, tm=512, name=name)


def _value_heads_t(c_kv, w_vt, bsz, t):
    n, k = c_kv.shape
    tm, tn = 512, 1024

    def body(a_ref, b_ref, o_ref):
        acc = _dot(a_ref[...], b_ref[...])
        lane = lax.broadcasted_iota(jnp.int32, (1, tn), 1)
        acc = acc + jnp.where(lax.rem(lane, LANES) == B_VDIM, 1.0, 0.0)
        o_ref[...] = acc.T.astype(BF16)

    per_seq = t // tm
    return pl.pallas_call(
        body, name="kv_up_v_t", grid=(w_vt.shape[1] // tn, n // tm),
        in_specs=[pl.BlockSpec((tm, k), lambda j, i: (i, 0)), pl.BlockSpec((k, tn), lambda j, i: (0, j))],
        out_specs=pl.BlockSpec((None, tn, tm), lambda j, i: (i // per_seq, j, lax.rem(i, per_seq))),
        out_shape=jax.ShapeDtypeStruct((bsz, w_vt.shape[1], t), BF16),
        compiler_params=_cparams(("parallel", "parallel")))(c_kv, w_vt)


def _rms(x, g):
    xf = x.astype(F32)
    return xf * lax.rsqrt(jnp.mean(xf * xf, axis=-1, keepdims=True) + NORM_EPS) * g


def _rms_bwd(x, g, dy):
    xf = x.astype(F32)
    rstd = lax.rsqrt(jnp.mean(xf * xf, axis=-1, keepdims=True) + NORM_EPS)
    xhat = xf * rstd
    dxhat = dy * g
    dx = rstd * (dxhat - xhat * jnp.mean(dxhat * xhat, axis=-1, keepdims=True))
    return dx, dy * xhat


def _silu(z):
    return z * jax.nn.sigmoid(z)


def _silu_grad(z):
    s = jax.nn.sigmoid(z)
    return s * (1.0 + z * (1.0 - s))


def _dot_nt(a, b):
    return lax.dot_general(a, b, (((1,), (1,)), ((), ())), preferred_element_type=F32)


def _dot_tn(a, b):
    return lax.dot_general(a, b, (((0,), (0,)), ((), ())), preferred_element_type=F32)


def _dot(a, b):
    return jnp.dot(a, b, preferred_element_type=F32)


A_SCALE = A_HEAD_DIM ** -0.5
LOG2E = 1.4426950408889634
A_C2 = A_SCALE * LOG2E
A_HEAD_GROUP = 4
A_FWD_HEAD_GROUP = 8


def _attn_a_fwd(qk, v, name):
    s_n, ln, _ = qk.shape
    nb = ln // BAND
    blk = (None, BAND, A_WIDTH)

    def body(q_ref, kc_ref, kp_ref, vc_ref, vp_ref, o_ref, lse_ref):
        kpos = lax.broadcasted_iota(jnp.int32, (2 * BAND, BAND), 0)
        qpos = lax.broadcasted_iota(jnp.int32, (2 * BAND, BAND), 1) + BAND
        first_key = jnp.where(pl.program_id(1) > 0, 0, BAND)
        mask = (kpos <= qpos) & (kpos >= qpos - BAND) & (kpos >= first_key)
        rows = []
        for h0 in range(0, A_HEADS, A_FWD_HEAD_GROUP):
            hss = [slice(h * A_HEAD_DIM, (h + 1) * A_HEAD_DIM) for h in range(h0, h0 + A_FWD_HEAD_GROUP)]
            sts = [_dot_nt(jnp.concatenate([kp_ref[:, hs], kc_ref[:, hs]], axis=0), q_ref[:, hs]) for hs in hss]
            ps, ls = [], []
            for st in sts:
                st = jnp.where(mask, st * A_C2, NEG)
                m = jnp.max(st, axis=0, keepdims=True)
                p = jnp.exp2(st - m)
                l_row = jnp.sum(p, axis=0, keepdims=True)
                ps.append(p.astype(BF16))
                ls.append(l_row)
                rows.append(m + jnp.log2(l_row))
            ots = [_dot_tn(jnp.concatenate([vp_ref[:, hs], vc_ref[:, hs]], axis=0), p) for hs, p in zip(hss, ps)]
            for hs, o_t, l_row in zip(hss, ots, ls):
                o_ref[:, hs] = (o_t / l_row).T.astype(BF16)
        lse_ref[...] = jnp.concatenate(rows, axis=0)

    def col(c, off):
        return lambda s, l: (s, jnp.maximum(l + off, 0), c)

    return pl.pallas_call(
        body, name=name, grid=(s_n, nb),
        in_specs=[pl.BlockSpec(blk, col(0, 0)), pl.BlockSpec(blk, col(1, 0)), pl.BlockSpec(blk, col(1, -1)),
                  pl.BlockSpec(blk, col(0, 0)), pl.BlockSpec(blk, col(0, -1))],
        out_specs=[pl.BlockSpec(blk, lambda s, l: (s, l, 0)),
                   pl.BlockSpec((None, A_HEADS, BAND), lambda s, l: (s, 0, l))],
        out_shape=[jax.ShapeDtypeStruct((s_n, ln, A_WIDTH), BF16), jax.ShapeDtypeStruct((s_n, A_HEADS, ln), F32)],
        compiler_params=_cparams(("parallel", "arbitrary")))(qk, qk, qk, v, v)


def _attn_a_bwd(qk, v, do, lse2, dsum, tabs, name):
    s_n, ln, _ = qk.shape
    nb = ln // BAND
    blk = (None, BAND, A_WIDTH)

    def body(q_ref, qn_ref, kc_ref, kp_ref, vc_ref, vp_ref, do_ref, don_ref, lse_ref, lsen_ref, ds_ref, dsn_ref,
             c_ref, sp_ref, sm_ref, out_ref):
        l_idx = pl.program_id(1)
        kpos = lax.broadcasted_iota(jnp.int32, (2 * BAND, BAND), 0)
        qpos = lax.broadcasted_iota(jnp.int32, (2 * BAND, BAND), 1) + BAND
        first_key = jnp.where(l_idx > 0, 0, BAND)
        mask_q = (kpos <= qpos) & (kpos >= qpos - BAND) & (kpos >= first_key)
        kpos2 = lax.broadcasted_iota(jnp.int32, (BAND, 2 * BAND), 0)
        qpos2 = lax.broadcasted_iota(jnp.int32, (BAND, 2 * BAND), 1)
        last_query = jnp.where(l_idx < nb - 1, 2 * BAND, BAND)
        mask_k = (kpos2 <= qpos2) & (kpos2 >= qpos2 - BAND) & (qpos2 < last_query)
        c, sp, sm = c_ref[...], sp_ref[...], sm_ref[...]
        lse_q, ds_q = lse_ref[...], ds_ref[...]
        lse_k = jnp.concatenate([lse_q, lsen_ref[...]], axis=1)
        ds_k = jnp.concatenate([ds_q, dsn_ref[...]], axis=1)
        for h0 in range(0, A_HEADS, A_HEAD_GROUP):
            hl = list(range(h0, h0 + A_HEAD_GROUP))
            hss = [slice(h * A_HEAD_DIM, (h + 1) * A_HEAD_DIM) for h in hl]
            k2s = [jnp.concatenate([kp_ref[:, hs], kc_ref[:, hs]], axis=0) for hs in hss]
            v2s = [jnp.concatenate([vp_ref[:, hs], vc_ref[:, hs]], axis=0) for hs in hss]
            q2s = [jnp.concatenate([q_ref[:, hs], qn_ref[:, hs]], axis=0) for hs in hss]
            do2s = [jnp.concatenate([do_ref[:, hs], don_ref[:, hs]], axis=0) for hs in hss]
            sts = [_dot_nt(k2, q_ref[:, hs]) for k2, hs in zip(k2s, hss)]
            dpts = [_dot_nt(v2, do_ref[:, hs]) for v2, hs in zip(v2s, hss)]
            st2s = [_dot_nt(kc_ref[:, hs], q2) for q2, hs in zip(q2s, hss)]
            dpt2s = [_dot_nt(vc_ref[:, hs], do2) for do2, hs in zip(do2s, hss)]
            dsts, dst2s, p2s = [], [], []
            for i, h in enumerate(hl):
                p = jnp.exp2(jnp.where(mask_q, sts[i] * A_C2, NEG) - lse_q[h:h + 1])
                dsts.append((p * (dpts[i] - ds_q[h:h + 1]) * A_SCALE).astype(BF16))
                p2 = jnp.exp2(jnp.where(mask_k, st2s[i] * A_C2, NEG) - lse_k[h:h + 1])
                dst2s.append((p2 * (dpt2s[i] - ds_k[h:h + 1]) * A_SCALE).astype(BF16))
                p2s.append(p2.astype(BF16))
            dqs = [_dot_tn(dsts[i], k2s[i]) for i in range(A_HEAD_GROUP)]
            dks = [_dot(dst2s[i], q2s[i]) for i in range(A_HEAD_GROUP)]
            dvs = [_dot(p2s[i], do2s[i]) for i in range(A_HEAD_GROUP)]
            for i, h in enumerate(hl):
                out_ref[:, hss[i]] = _rope_t(dqs[i], c, sp, sm).astype(BF16)
                out_ref[:, A_WIDTH + h * A_HEAD_DIM:A_WIDTH + (h + 1) * A_HEAD_DIM] = _rope_t(dks[i], c, sp, sm).astype(BF16)
                out_ref[:, 2 * A_WIDTH + h * A_HEAD_DIM:2 * A_WIDTH + (h + 1) * A_HEAD_DIM] = dvs[i].astype(BF16)

    def col(c, off):
        return lambda s, l: (s, jnp.clip(l + off, 0, nb - 1), c)

    def row(off):
        return pl.BlockSpec((None, A_HEADS, BAND), lambda s, l: (s, 0, jnp.clip(l + off, 0, nb - 1)))

    tspec = pl.BlockSpec((None, BAND, LANES), lambda s, l: (s, l, 0))
    in_specs = [pl.BlockSpec(blk, col(0, 0)), pl.BlockSpec(blk, col(0, 1)),
                pl.BlockSpec(blk, col(1, 0)), pl.BlockSpec(blk, col(1, -1)),
                pl.BlockSpec(blk, col(0, 0)), pl.BlockSpec(blk, col(0, -1)),
                pl.BlockSpec(blk, col(0, 0)), pl.BlockSpec(blk, col(0, 1)),
                row(0), row(1), row(0), row(1), tspec, tspec, tspec]
    return pl.pallas_call(
        body, name=name, grid=(s_n, nb), in_specs=in_specs,
        out_specs=pl.BlockSpec((None, BAND, 3 * A_WIDTH), lambda s, l: (s, l, 0)),
        out_shape=jax.ShapeDtypeStruct((s_n, ln, 3 * A_WIDTH), BF16),
        compiler_params=_cparams(("parallel", "arbitrary")))(
            qk, qk, qk, qk, v, v, do, do, lse2, lse2, dsum, dsum, *tabs)


B_TQ = 256
B_FWD_HEADS = 4
B_BWD_PAIRS = 2
B_SCALE = B_QK_DIM ** -0.5
B_C2 = B_SCALE * LOG2E


def _key_le_query(kb, qb, tk, tq):
    kpos = kb * tk + lax.broadcasted_iota(jnp.int32, (tk, tq), 0)
    qpos = qb * tq + lax.broadcasted_iota(jnp.int32, (tk, tq), 1)
    return kpos <= qpos


def _attn_b_fwd(q, kpad, vt1):
    bsz, t, _ = q.shape
    tq = tk = B_TQ
    nq = t // tq
    nh = B_FWD_HEADS

    def body(q_ref, k_ref, vt_ref, o_ref, lse_ref):
        qblk = pl.program_id(2)
        qs = [q_ref[:, hh * LANES:(hh + 1) * LANES] for hh in range(nh)]

        def scores(kb):
            start = pl.multiple_of(kb * tk, tk)
            return [_dot_nt(k_ref[pl.ds(start, tk), hh * LANES:(hh + 1) * LANES], qs[hh]) for hh in range(nh)]

        def pv(kb, ps):
            start = pl.multiple_of(kb * tk, tk)
            return [_dot(vt_ref[hh * LANES:(hh + 1) * LANES, pl.ds(start, tk)], ps[hh]) for hh in range(nh)]

        def softmax(ss, ms, accs, kb, masked):
            out_m, out_acc, out_p = [], [], []
            for hh in range(nh):
                s = ss[hh] * B_C2
                if masked:
                    s = jnp.where(_key_le_query(kb, qblk, tk, tq), s, NEG)
                m_new = jnp.maximum(ms[hh], jnp.max(s, axis=0, keepdims=True))
                out_acc.append(jnp.exp2(ms[hh] - m_new) * accs[hh])
                out_p.append(jnp.exp2(s - m_new).astype(BF16))
                out_m.append(m_new)
            return out_m, out_acc, out_p

        def step(kb, carry):
            ss, ps, ms, accs = carry
            pvs = pv(jnp.maximum(kb - 1, 0), ps)
            ss_next = scores(kb + 1)
            accs = [accs[hh] + pvs[hh] for hh in range(nh)]
            ms, accs, ps = softmax(ss, ms, accs, kb, False)
            return (ss_next, ps, ms, accs)

        init = (scores(0), [jnp.zeros((tk, tq), BF16)] * nh, [jnp.full((1, tq), NEG, F32)] * nh,
                [jnp.zeros((LANES, tq), F32)] * nh)
        ss, ps, ms, accs = lax.fori_loop(0, qblk, step, init)
        pvs = pv(jnp.maximum(qblk - 1, 0), ps)
        accs = [accs[hh] + pvs[hh] for hh in range(nh)]
        ms, accs, ps = softmax(ss, ms, accs, qblk, True)
        pvs = pv(qblk, ps)
        accs = [accs[hh] + pvs[hh] for hh in range(nh)]
        ls = [accs[hh][B_VDIM:B_VDIM + 1] for hh in range(nh)]
        o_t = jnp.concatenate([accs[hh][:B_VDIM] / ls[hh] for hh in range(nh)], axis=0)
        o_ref[...] = o_t.T
        for pair in range(nh // 2):
            lse_ref[pair] = jnp.concatenate([ms[2 * pair + hh] + jnp.log2(ls[2 * pair + hh]) for hh in range(2)]
                                            + [jnp.zeros((6, tq), F32)], axis=0)

    return pl.pallas_call(
        body, name="attn_b_fwd", grid=(bsz, B_HEADS // nh, nq),
        in_specs=[pl.BlockSpec((None, tq, nh * LANES), lambda b, j, i: (b, i, j)),
                  pl.BlockSpec((None, t, nh * LANES), lambda b, j, i: (b, 0, j)),
                  pl.BlockSpec((None, nh * LANES, t), lambda b, j, i: (b, j, 0))],
        out_specs=[pl.BlockSpec((None, tq, nh * B_VDIM), lambda b, j, i: (b, i, j)),
                   pl.BlockSpec((None, nh // 2, 8, tq), lambda b, j, i: (b, j, 0, i))],
        out_shape=[jax.ShapeDtypeStruct((bsz, t, B_HEADS * B_VDIM), F32),
                   jax.ShapeDtypeStruct((bsz, B_HEADS // 2, 8, t), F32)],
        compiler_params=_cparams(("parallel", "parallel", "arbitrary")))(q, kpad, vt1)


def _attn_b_bwd(q, kpad, v, do, o, lse2, tabs):
    bsz, t, _ = q.shape
    tq = tk = B_TQ
    nq = t // tq
    n_pairs = B_BWD_PAIRS
    n_heads = 2 * n_pairs

    def body(q_ref, k_ref, v_ref, do_ref, o_ref, lse_ref, c_ref, sp_ref, sm_ref, dq_ref, dk_ref, dv_ref,
             dqt_scr, dsum_scr):
        lane = lax.broadcasted_iota(jnp.int32, (tk, LANES), 1)
        sel_lane = lax.broadcasted_iota(jnp.int32, (8, LANES), 1)
        sel_row = lax.broadcasted_iota(jnp.int32, (8, LANES), 0)
        sel = jnp.where((sel_lane < B_VDIM) == (sel_row == 0), 1.0, 0.0)
        sel = jnp.where(sel_row < 2, sel, 0.0).astype(BF16)

        def rows(blk):
            return pl.ds(pl.multiple_of(blk * tq, tq), tq)

        def dsum_step(qb, carry):
            for pp in range(n_pairs):
                pls = slice(pp * LANES, (pp + 1) * LANES)
                prod = do_ref[rows(qb), pls].astype(F32) * o_ref[rows(qb), pls]
                hi = prod.astype(BF16)
                lo = (prod - hi.astype(F32)).astype(BF16)
                dsum_scr[pp, :, rows(qb)] = _dot_nt(sel, hi) + _dot_nt(sel, lo)
            return carry

        lax.fori_loop(0, nq, dsum_step, 0)
        dqt_scr[...] = jnp.zeros_like(dqt_scr)

        def kv_step(kb, carry):
            ks = [k_ref[rows(kb), hh * LANES:(hh + 1) * LANES] for hh in range(n_heads)]
            vs = []
            for pp in range(n_pairs):
                vb = v_ref[rows(kb), pp * LANES:(pp + 1) * LANES]
                zero = jnp.zeros_like(vb)
                vs += [jnp.where(lane < B_VDIM, vb, zero), jnp.where(lane < B_VDIM, zero, vb)]

            def make_step(masked):
                def step(qb, acc):
                    qs = [q_ref[rows(qb), hh * LANES:(hh + 1) * LANES] for hh in range(n_heads)]
                    dob = [do_ref[rows(qb), pp * LANES:(pp + 1) * LANES] for pp in range(n_pairs)]
                    ss = [_dot_nt(ks[hh], qs[hh]) for hh in range(n_heads)]
                    dps = [_dot_nt(vs[hh], dob[hh // 2]) for hh in range(n_heads)]
                    pbs, dss = [], []
                    for hh in range(n_heads):
                        stat = (hh // 2, slice(hh % 2, hh % 2 + 1), rows(qb))
                        s = ss[hh] * B_C2
                        if masked:
                            s = jnp.where(_key_le_query(kb, qb, tk, tq), s, NEG)
                        p = jnp.exp2(s - lse_ref[stat])
                        dss.append((p * (dps[hh] - dsum_scr[stat]) * B_SCALE).astype(BF16))
                        pbs.append(p.astype(BF16))
                    for hh in range(n_heads):
                        dqt_scr[hh, :, rows(qb)] += _dot_tn(ks[hh], dss[hh])
                    return tuple([acc[hh] + _dot(dss[hh], qs[hh]) for hh in range(n_heads)]
                                 + [acc[n_heads + hh] + _dot(pbs[hh], dob[hh // 2]) for hh in range(n_heads)])
                return step

            acc = make_step(True)(kb, (jnp.zeros((tk, LANES), F32),) * (2 * n_heads))
            acc = lax.fori_loop(kb + 1, nq, make_step(False), acc)
            for hh in range(n_heads):
                dk_ref[rows(kb), hh * LANES:(hh + 1) * LANES] = acc[hh].astype(BF16)
            for pp in range(n_pairs):
                dv_pair = jnp.where(lane < B_VDIM, acc[n_heads + 2 * pp], acc[n_heads + 2 * pp + 1])
                dv_ref[rows(kb), pp * LANES:(pp + 1) * LANES] = dv_pair.astype(BF16)
            return carry

        lax.fori_loop(0, nq, kv_step, 0)

        def dq_step(qb, carry):
            c, sp, sm = c_ref[rows(qb), :], sp_ref[rows(qb), :], sm_ref[rows(qb), :]
            for hh in range(n_heads):
                dq_ref[rows(qb), hh * LANES:(hh + 1) * LANES] = _rope_t(dqt_scr[hh, :, rows(qb)].T, c, sp, sm).astype(BF16)
            return carry

        lax.fori_loop(0, nq, dq_step, 0)

    pair_full = pl.BlockSpec((None, t, n_heads * LANES), lambda b, j: (b, 0, j))
    one_full = pl.BlockSpec((None, t, n_pairs * LANES), lambda b, j: (b, 0, j))
    row_full = pl.BlockSpec((None, n_pairs, 8, t), lambda b, j: (b, j, 0, 0))
    tab_full = pl.BlockSpec((None, t, LANES), lambda b, j: (b, 0, 0))
    return pl.pallas_call(
        body, name="attn_b_bwd", grid=(bsz, B_HEADS // n_heads),
        in_specs=[pair_full, pair_full, one_full, one_full, one_full, row_full, tab_full, tab_full, tab_full],
        out_specs=[pair_full, pair_full, one_full],
        out_shape=[jax.ShapeDtypeStruct((bsz, t, B_KPAD), BF16), jax.ShapeDtypeStruct((bsz, t, B_KPAD), BF16),
                   jax.ShapeDtypeStruct((bsz, t, B_HEADS * B_VDIM), BF16)],
        scratch_shapes=[pltpu.VMEM((n_heads, LANES, t), F32), pltpu.VMEM((n_pairs, 8, t), F32)],
        compiler_params=_cparams(("parallel", "parallel")))(q, kpad, v, do, o, lse2, *tabs)


ANY = pl.BlockSpec(memory_space=pl.ANY)


def _all_gather(shard, name):
    def body(x_ref, out_ref, send_sems, recv_sems, local_sem):
        x, y, c = lax.axis_index("x"), lax.axis_index("y"), lax.axis_index("c")
        me, sibling = (x, y, c), (x, y, 1 - c)
        chips = [(1 - x, y), (x, 1 - y), (1 - x, 1 - y)]

        def rows(px, py, pc):
            return out_ref.at[4 * px + 2 * py + pc]

        def copy(k, block, to, src=None):
            return pltpu.make_async_remote_copy(
                src_ref=rows(*block) if src is None else src, dst_ref=rows(*block),
                send_sem=send_sems.at[k], recv_sem=recv_sems.at[k], device_id=to, device_id_type=MESH)

        mine = pltpu.make_async_copy(x_ref, rows(*me), local_sem)
        mine.start()
        first = [copy(0, me, sibling, src=x_ref)]
        first += [copy(1 + j, me, (*chip, c), src=x_ref) for j, chip in enumerate(chips)]
        for cp in first:
            cp.start()
        passed = [copy(4 + j, (*chip, c), sibling) for j, chip in enumerate(chips)]
        for j, chip in enumerate(chips):
            copy(1 + j, (*chip, c), me).wait_recv()
            passed[j].start()
        copy(0, sibling, me).wait_recv()
        for j, chip in enumerate(chips):
            copy(4 + j, (*chip, 1 - c), me).wait_recv()
        for cp in first + passed:
            cp.wait_send()
        mine.wait()

    return pl.pallas_call(
        body, name=name, in_specs=[ANY], out_specs=ANY,
        out_shape=jax.ShapeDtypeStruct((N_DEV,) + shard.shape, shard.dtype),
        scratch_shapes=[pltpu.SemaphoreType.DMA((7,)), pltpu.SemaphoreType.DMA((7,)), pltpu.SemaphoreType.DMA])(shard)


def _all_gather_weights(flat, wide, name):
    ns = wide.shape[1]

    def body(f_ref, w_ref, fo_ref, wo_ref, send_sems, recv_sems, local_sems):
        x, y, c = lax.axis_index("x"), lax.axis_index("y"), lax.axis_index("c")
        me, sibling = (x, y, c), (x, y, 1 - c)
        chips = [(1 - x, y), (x, 1 - y), (1 - x, 1 - y)]

        def place(a, px, py, pc):
            idx = 4 * px + 2 * py + pc
            if a == 0:
                return fo_ref.at[idx]
            return wo_ref.at[:, pl.ds(pl.multiple_of(idx * ns, LANES), ns)]

        def copy(a, k, block, to, src=None):
            return pltpu.make_async_remote_copy(
                src_ref=place(a, *block) if src is None else src, dst_ref=place(a, *block),
                send_sem=send_sems.at[a, k], recv_sem=recv_sems.at[a, k], device_id=to, device_id_type=MESH)

        own = (f_ref, w_ref)
        mine = [pltpu.make_async_copy(own[a], place(a, *me), local_sems.at[a]) for a in range(2)]
        first = []
        for a in range(2):
            mine[a].start()
            first.append(copy(a, 0, me, sibling, src=own[a]))
            first += [copy(a, 1 + j, me, (*chip, c), src=own[a]) for j, chip in enumerate(chips)]
        for cp in first:
            cp.start()
        passed = [[copy(a, 4 + j, (*chip, c), sibling) for j, chip in enumerate(chips)] for a in range(2)]
        for a in range(2):
            for j, chip in enumerate(chips):
                copy(a, 1 + j, (*chip, c), me).wait_recv()
                passed[a][j].start()
        for a in range(2):
            copy(a, 0, sibling, me).wait_recv()
            for j, chip in enumerate(chips):
                copy(a, 4 + j, (*chip, 1 - c), me).wait_recv()
        for cp in first + passed[0] + passed[1]:
            cp.wait_send()
        for cp in mine:
            cp.wait()

    return pl.pallas_call(
        body, name=name, in_specs=[ANY, ANY], out_specs=[ANY, ANY],
        out_shape=[jax.ShapeDtypeStruct((N_DEV,) + flat.shape, flat.dtype),
                   jax.ShapeDtypeStruct((wide.shape[0], N_DEV * ns), wide.dtype)],
        scratch_shapes=[pltpu.SemaphoreType.DMA((2, 7)), pltpu.SemaphoreType.DMA((2, 7)),
                        pltpu.SemaphoreType.DMA((2,))])(flat, wide)


N_CHIPS = 4


def _exchange_sibling(blocks, name):
    def body(g_ref, got_ref, send_sems, recv_sems):
        x, y, c = lax.axis_index("x"), lax.axis_index("y"), lax.axis_index("c")
        sends = [pltpu.make_async_remote_copy(
            src_ref=g_ref.at[q, 1 - c], dst_ref=got_ref.at[q], send_sem=send_sems.at[q],
            recv_sem=recv_sems.at[q], device_id=(x, y, 1 - c), device_id_type=MESH) for q in range(N_CHIPS)]
        for cp in sends:
            cp.start()
        for cp in sends:
            cp.wait_recv()
        for cp in sends:
            cp.wait_send()

    return pl.pallas_call(
        body, name=name, in_specs=[ANY], out_specs=ANY,
        out_shape=jax.ShapeDtypeStruct((N_CHIPS,) + blocks.shape[2:], blocks.dtype),
        scratch_shapes=[pltpu.SemaphoreType.DMA((N_CHIPS,)), pltpu.SemaphoreType.DMA((N_CHIPS,))])(blocks)


def _exchange_chips(parts, name):
    def body(p_ref, out_ref, send_sems, recv_sems, local_sem):
        x, y, c = lax.axis_index("x"), lax.axis_index("y"), lax.axis_index("c")
        me = 2 * x + y

        def peer(k):
            return (1 - x if k & 2 else x, 1 - y if k & 1 else y)

        def copy(k):
            px, py = peer(k)
            return pltpu.make_async_remote_copy(
                src_ref=p_ref.at[2 * px + py], dst_ref=out_ref.at[me], send_sem=send_sems.at[k - 1],
                recv_sem=recv_sems.at[k - 1], device_id=(px, py, c), device_id_type=MESH)

        def arrival(k):
            px, py = peer(k)
            slot = out_ref.at[2 * px + py]
            return pltpu.make_async_remote_copy(
                src_ref=slot, dst_ref=slot, send_sem=send_sems.at[k - 1], recv_sem=recv_sems.at[k - 1],
                device_id=(px, py, c), device_id_type=MESH)

        mine = pltpu.make_async_copy(p_ref.at[me], out_ref.at[me], local_sem)
        mine.start()
        sends = [copy(k) for k in range(1, N_CHIPS)]
        for cp in sends:
            cp.start()
        for k in range(1, N_CHIPS):
            arrival(k).wait_recv()
        for cp in sends:
            cp.wait_send()
        mine.wait()

    return pl.pallas_call(
        body, name=name, in_specs=[ANY], out_specs=ANY,
        out_shape=jax.ShapeDtypeStruct(parts.shape, parts.dtype),
        scratch_shapes=[pltpu.SemaphoreType.DMA((N_CHIPS - 1,)), pltpu.SemaphoreType.DMA((N_CHIPS - 1,)),
                        pltpu.SemaphoreType.DMA])(parts)


def _add_pairs(blocks, got, core, *, tr, name):
    q, r, c = got.shape

    def body(core_ref, a_ref, b_ref, o_ref):
        o_ref[...] = (a_ref[...].astype(F32) + b_ref[...].astype(F32)).astype(o_ref.dtype)

    spec = pl.BlockSpec((q, tr, c), lambda i, core_ref: (0, i, 0))
    mine = pl.BlockSpec((q, None, tr, c), lambda i, core_ref: (0, core_ref[0], i, 0))
    return pl.pallas_call(
        body, name=name,
        grid_spec=pltpu.PrefetchScalarGridSpec(num_scalar_prefetch=1, grid=(r // tr,), in_specs=[mine, spec],
                                               out_specs=spec),
        out_shape=jax.ShapeDtypeStruct(got.shape, BF16), compiler_params=_cparams(("parallel",)))(
            core, blocks, got)


def _sum_slots(slots, *, tr, name):
    n_slots, r, c = slots.shape

    def body(s_ref, o_ref):
        acc = s_ref[0].astype(F32)
        for s in range(1, n_slots):
            acc = acc + s_ref[s].astype(F32)
        o_ref[...] = acc

    return pl.pallas_call(
        body, name=name, grid=(r // tr,),
        in_specs=[pl.BlockSpec((n_slots, tr, c), lambda i: (0, i, 0))],
        out_specs=pl.BlockSpec((tr, c), lambda i: (i, 0)),
        out_shape=jax.ShapeDtypeStruct((r, c), F32),
        compiler_params=_cparams(("parallel",)))(slots)


def _adamw_math(w_t, g_t, m_t, v_t):
    m_n = ADAM_B1 * m_t + (1.0 - ADAM_B1) * g_t
    v_n = ADAM_B2 * v_t + (1.0 - ADAM_B2) * (g_t * g_t)
    m_hat = m_n / (1.0 - ADAM_B1 ** ADAM_STEP)
    v_hat = v_n / (1.0 - ADAM_B2 ** ADAM_STEP)
    delta = -ADAM_LR * (m_hat / (jnp.sqrt(v_hat) + ADAM_EPS) + ADAM_WD * w_t)
    return delta, m_n, v_n


def _adamw(w, g, m, v, name):
    shape = w.shape
    cols = shape[-1]
    args = [a.reshape(-1, cols) for a in (w, g, m, v)]
    rows = args[0].shape[0]
    tm = 256 if rows % 256 == 0 else rows
    delta, m_n, v_n = _rowwise(_adamw_math, args, [], [(cols, F32)] * 3, tm=tm, name=name)
    return delta.reshape(shape), m_n.reshape(shape), v_n.reshape(shape)


def _adamw_small(ws, gs, ms, vs, name):
    k = len(ws)
    args = [a.reshape(-1, a.shape[-1]) for group in (ws, gs, ms, vs) for a in group]

    def body(*refs):
        ins, outs = refs[:4 * k], refs[4 * k:]
        for i in range(k):
            res = _adamw_math(*[ins[j * k + i][...] for j in range(4)])
            for j in range(3):
                outs[j * k + i][...] = res[j]

    res = pl.pallas_call(
        body, name=name, out_shape=[jax.ShapeDtypeStruct(a.shape, F32) for a in args[:k]] * 3,
        compiler_params=pltpu.CompilerParams(vmem_limit_bytes=VMEM_LIMIT))(*args)
    return [[res[j * k + i].reshape(ws[i].shape) for i in range(k)] for j in range(3)]


def _local_step(x, positions, target, w):
    bsz, t, _ = x.shape
    n = bsz * t
    tm = 256
    mm = 512
    x2 = x.reshape(n, D_MODEL)
    tgt2 = target.reshape(n, D_MODEL)
    pos = positions.reshape(n)
    tb2 = _rope_tables(pos, B_ROPE_THETA, B_ROPE, B_NOPE, "rope_tables_b")
    tabs_b = [a.reshape(bsz, t, LANES) for a in tb2]

    w_a_in = w["a_w_in"]
    w_a_out = w["a_w_out"]
    w_down = w["kv_w_down"]
    w_down_p = jnp.zeros((D_MODEL, 3 * LANES), BF16).at[:, :B_KV_LORA].set(w_down[:, :B_KV_LORA])
    w_down_p = w_down_p.at[:, B_KV_LORA + B_NOPE:B_KV_LORA + B_QK_DIM].set(w_down[:, B_KV_LORA:])
    wu = w["kv_w_up"].reshape(B_KV_LORA, B_HEADS, B_NOPE + B_VDIM)
    w_upk = jnp.pad(wu[:, :, :B_NOPE], ((0, 0), (0, 0), (0, LANES - B_NOPE))).reshape(B_KV_LORA, B_KPAD)
    w_upv = wu[:, :, B_NOPE:].reshape(B_KV_LORA, B_HEADS * B_VDIM)
    w_b_in = w["b_w_in"]
    w_q_p = jnp.pad(w["b_w_q_up"].reshape(B_Q_LORA, B_HEADS, B_QK_DIM),
                    ((0, 0), (0, 0), (0, LANES - B_QK_DIM))).reshape(B_Q_LORA, B_KPAD)
    w_b_out = w["b_w_out"]

    def tab_extras(tabs2, rows):
        return [(a, (rows, LANES), lambda j, i, kk: (i, 0)) for a in tabs2]

    (hn_a,) = _rowwise(lambda xt, g: (_rms(xt, g),), [x2], [w["a_pre_norm"]], [(D_MODEL, BF16)],
                       tm=tm, name="a_pre_norm")

    def rope_epilogue(acc, o_ref, rows, c_ref, sp_ref, sm_ref):
        c, sp, sm = c_ref[rows, :], sp_ref[rows, :], sm_ref[rows, :]
        for h in range(acc.shape[1] // LANES):
            hs = slice(h * LANES, (h + 1) * LANES)
            o_ref[rows, hs] = _rope(acc[:, hs], c, sp, sm).astype(BF16)

    def to_group(a, d):
        if d == 1:
            return a
        return a.reshape(bsz, t // d, d, a.shape[-1]).transpose(0, 2, 1, 3).reshape(n, a.shape[-1])

    def from_group(a, d):
        if d == 1:
            return a
        return a.reshape(bsz, d, t // d, a.shape[-1]).transpose(0, 2, 1, 3).reshape(n, a.shape[-1])

    def rows_to_cols(r, d):
        return r.reshape(bsz, d, A_HEADS, t // d).transpose(0, 3, 1, 2).reshape(n, A_HEADS)

    def cols_to_rows(cc, d):
        return cc.reshape(bsz, t // d, d, A_HEADS).transpose(0, 2, 3, 1).reshape(bsz * d, A_HEADS, t // d)

    z_a = _matmul(hn_a, w_a_in, b_cols=(A_QKV, A_WIDTH), out_dtype=F32, tm=mm, tn=A_WIDTH, name="a_gate")
    hn_g, tabs_g, qk_g, v_g, o_g, lse_g = [], [], [], [], [], []
    for g, d in enumerate(A_DILATIONS):
        hn_g.append(to_group(hn_a, d))
        tabs_g.append(_rope_tables(to_group(pos.reshape(n, 1), d).reshape(n), A_ROPE_THETA, A_ROT_DIM, 0,
                                   f"rope_tables_a_g{g}"))
        col0 = g * 3 * A_WIDTH
        qk = _matmul(hn_g[g], w_a_in, b_cols=(col0, 2 * A_WIDTH), out_dtype=BF16, tm=mm, tn=A_WIDTH,
                     name=f"a_qk_g{g}", epilogue=rope_epilogue, epilogue_rows=EPILOGUE_ROWS,
                     extras=tab_extras(tabs_g[g], mm))
        v = _matmul(hn_g[g], w_a_in, b_cols=(col0 + 2 * A_WIDTH, A_WIDTH), out_dtype=BF16, tm=mm, tn=A_WIDTH,
                    name=f"a_v_g{g}")
        qk_g.append(qk.reshape(bsz * d, t // d, 2 * A_WIDTH))
        v_g.append(v.reshape(bsz * d, t // d, A_WIDTH))
        o, lse = _attn_a_fwd(qk_g[g], v_g[g], f"attn_a_fwd_g{g}")
        o_g.append(from_group(o.reshape(n, A_WIDTH), d))
        lse_g.append(rows_to_cols(lse, d))

    def merge_fn(o0, o1, o2, l0, l1, l2, z):
        lmax = jnp.maximum(jnp.maximum(l0, l1), l2)
        e0, e1, e2 = jnp.exp2(l0 - lmax), jnp.exp2(l1 - lmax), jnp.exp2(l2 - lmax)
        den = e0 + e1 + e2
        w0, w1, w2 = e0 / den, e1 / den, e2 / den
        parts = []
        for h in range(A_HEADS):
            hs = slice(h * A_HEAD_DIM, (h + 1) * A_HEAD_DIM)
            parts.append(w0[:, h:h + 1] * o0[:, hs] + w1[:, h:h + 1] * o1[:, hs] + w2[:, h:h + 1] * o2[:, hs])
        o = jnp.concatenate(parts, axis=1)
        return o * _silu(z), o, lmax + jnp.log2(den)

    y_a, o_a2, lse_a = _rowwise(merge_fn, [*o_g, *lse_g, z_a], [],
                                [(A_WIDTH, BF16), (A_WIDTH, F32), (A_HEADS, F32)], tm=tm, name="a_merge_gate")
    out_a = _matmul(y_a, w_a_out, out_dtype=F32, tm=mm, tn=D_MODEL, name="a_out")

    def mid_fn(xt, out, g_post, g_kv, g_b):
        h1 = xt + _rms(out, g_post)
        return h1, _rms(h1, g_kv), _rms(h1, g_b)

    h1, hn_kv, hn_b = _rowwise(mid_fn, [x2, out_a], [w["a_post_norm"], w["kv_norm"], w["b_pre_norm"]],
                               [(D_MODEL, F32), (D_MODEL, BF16), (D_MODEL, BF16)], tm=tm, name="a_post_norm")

    ckr = _matmul(hn_kv, w_down_p, out_dtype=F32, tm=mm, tn=3 * LANES, name="kv_down")

    def latent_fn(ck, c, sp, sm, g):
        return _rms(ck[:, :B_KV_LORA], g), _rope(ck[:, B_KV_LORA:], c, sp, sm)

    c_kv, k_rope = _rowwise(latent_fn, [ckr, *tb2], [w["kv_latent_norm"]], [(B_KV_LORA, BF16), (LANES, F32)],
                            tm=tm, name="kv_latent_norm")

    def kpad_epilogue(acc, o_ref, rows, kr_ref):
        kr = kr_ref[rows, :]
        for h in range(acc.shape[1] // LANES):
            hs = slice(h * LANES, (h + 1) * LANES)
            o_ref[rows, hs] = (acc[:, hs] + kr).astype(BF16)

    kpad = _matmul(c_kv, w_upk, out_dtype=BF16, tm=mm, tn=1024, name="kv_up_k", epilogue=kpad_epilogue,
                   epilogue_rows=EPILOGUE_ROWS,
                   extras=[(k_rope, (mm, LANES), lambda j, i, kk: (i, 0))])
    v_b = _matmul(c_kv, w_upv, out_dtype=BF16, tm=mm, tn=1024, name="kv_up_v")

    proj_b = _matmul(hn_b, w_b_in, out_dtype=F32, tm=mm, tn=w_b_in.shape[1], name="b_in")
    (c_q,) = _rowwise(lambda p, g: (_rms(p[:, :B_Q_LORA], g),), [proj_b], [w["b_q_norm"]], [(B_Q_LORA, BF16)],
                      tm=tm, name="b_q_norm")

    q_b = _matmul(c_q, w_q_p, out_dtype=BF16, tm=mm, tn=1024, name="b_q_up", epilogue=rope_epilogue,
                  epilogue_rows=EPILOGUE_ROWS,
                  extras=tab_extras(tb2, mm))
    q_b3, kpad3, v_b3 = q_b.reshape(bsz, t, B_KPAD), kpad.reshape(bsz, t, B_KPAD), v_b.reshape(bsz, t, -1)
    w_vt = jnp.pad(wu[:, :, B_NOPE:], ((0, 0), (0, 0), (0, LANES - B_VDIM))).reshape(B_KV_LORA, B_KPAD)
    vt1 = _value_heads_t(c_kv, w_vt, bsz, t)
    o_b, lse_b = _attn_b_fwd(q_b3, kpad3, vt1)
    o_b2 = o_b.reshape(n, -1)
    (y_b,) = _rowwise(lambda o, p: (o * _silu(p[:, B_Q_LORA:]),), [o_b2, proj_b], [], [(D_MODEL, BF16)],
                      tm=tm, name="b_gate_mul")
    out_b = _matmul(y_b, w_b_out, out_dtype=F32, tm=mm, tn=D_MODEL, name="b_out")

    def head_fn(h1t, out, tgt, g):
        e = h1t + _rms(out, g) - tgt
        loss_row = 0.5 * jnp.mean(e * e, axis=-1, keepdims=True)
        dh2 = e * (1.0 / D_MODEL)
        d_out, dg = _rms_bwd(out, g, dh2)
        return dh2, d_out, dg, jnp.broadcast_to(loss_row * (1.0 / LANES), (loss_row.shape[0], LANES))

    dh2, d_out_b, dg_b_post, loss_acc = _rowwise(
        head_fn, [h1, out_b, tgt2], [w["b_post_norm"]], [(D_MODEL, F32), (D_MODEL, BF16)], [D_MODEL, LANES],
        tm=tm, name="loss_head")

    dy_b = _matmul(d_out_b, w_b_out, tb=True, out_dtype=BF16, tm=mm, tn=D_MODEL, name="b_out_dx")
    gw_b_out = _matmul(y_b, d_out_b, ta=True, out_dtype=BF16, tm=mm, tn=D_MODEL, tk=2048, name="b_out_dw")

    def gate_b_bwd(dy, o, p):
        z = p[:, B_Q_LORA:]
        return dy * _silu(z), dy * o * _silu_grad(z)

    do_b, dz_b = _rowwise(gate_b_bwd, [dy_b, o_b2, proj_b], [], [(D_MODEL, BF16), (D_MODEL, F32)],
                          tm=tm, name="b_gate_bwd")
    do_b3 = do_b.reshape(bsz, t, -1)
    dq_b, dk_b, dv_b = _attn_b_bwd(q_b3, kpad3, v_b3, do_b3, o_b, lse_b, tabs_b)
    dq_b2, dk_b2, dv_b2 = dq_b.reshape(n, B_KPAD), dk_b.reshape(n, B_KPAD), dv_b.reshape(n, -1)

    dc_kv = _matmul(dk_b2, w_upk, tb=True, out_dtype=F32, tm=mm, tn=B_KV_LORA, name="kv_up_k_dx")
    dc_kv = _matmul(dv_b2, w_upv, tb=True, out_dtype=BF16, tm=mm, tn=B_KV_LORA, name="kv_up_v_dx",
                    epilogue=_add_epilogue, extras=[(dc_kv, (mm, B_KV_LORA), lambda j, i, kk: (i, j))])
    gw_upk = _matmul(c_kv, dk_b2, ta=True, out_dtype=BF16, tm=B_KV_LORA, tn=1024, tk=2048, name="kv_up_k_dw")
    gw_upv = _matmul(c_kv, dv_b2, ta=True, out_dtype=BF16, tm=B_KV_LORA, tn=1024, tk=2048, name="kv_up_v_dw")

    def latent_bwd(ck, dck, dk, c, sp, sm, g):
        d1, dg = _rms_bwd(ck[:, :B_KV_LORA], g, dck)
        ksum = dk[:, :LANES].astype(F32)
        for h in range(1, B_HEADS):
            ksum = ksum + dk[:, h * LANES:(h + 1) * LANES].astype(F32)
        lane = lax.broadcasted_iota(jnp.int32, ksum.shape, 1)
        ksum = jnp.where((lane >= B_NOPE) & (lane < B_QK_DIM), ksum, 0.0)
        return jnp.concatenate([d1, _rope_t(ksum, c, sp, sm)], axis=1), dg

    dckr, dg_latent = _rowwise(latent_bwd, [ckr, dc_kv, dk_b2, *tb2], [w["kv_latent_norm"]],
                               [(3 * LANES, BF16)], [B_KV_LORA], tm=tm, name="kv_latent_bwd")
    dhn_kv = _matmul(dckr, w_down_p, tb=True, out_dtype=BF16, tm=mm, tn=D_MODEL, name="kv_down_dx")
    gw_down_p = _matmul(hn_kv, dckr, ta=True, out_dtype=BF16, tm=mm, tn=3 * LANES, tk=2048, name="kv_down_dw")

    dc_q = _matmul(dq_b2, w_q_p, tb=True, out_dtype=BF16, tm=mm, tn=B_Q_LORA, name="b_q_up_dx")
    gw_q_p = _matmul(c_q, dq_b2, ta=True, out_dtype=BF16, tm=B_Q_LORA, tn=1024, tk=2048, name="b_q_up_dw")

    def q_norm_bwd(p, dcq, dz, g):
        d1, dg = _rms_bwd(p[:, :B_Q_LORA], g, dcq)
        return jnp.concatenate([d1, dz], axis=1), dg

    dproj_b, dg_q_norm = _rowwise(q_norm_bwd, [proj_b, dc_q, dz_b], [w["b_q_norm"]],
                                  [(w_b_in.shape[1], BF16)], [B_Q_LORA], tm=tm, name="b_q_norm_bwd")
    dhn_b = _matmul(dproj_b, w_b_in, tb=True, out_dtype=BF16, tm=mm, tn=D_MODEL, name="b_in_dx")
    gw_b_in = _matmul(hn_b, dproj_b, ta=True, out_dtype=BF16, tm=mm, tn=w_b_in.shape[1], tk=2048, name="b_in_dw")

    def mid_bwd(h1t, dh2t, dkv, db, g_kv, g_b, g_post, out):
        dxa, ra = _rms_bwd(h1t, g_kv, dkv)
        dxb, rb = _rms_bwd(h1t, g_b, db)
        dh1 = dh2t + dxa + dxb
        d_out, rp = _rms_bwd(out, g_post, dh1)
        return dh1, d_out, ra, rb, rp

    def mid_bwd_fn(h1t, dh2t, dkv, db, out, g_kv, g_b, g_post):
        return mid_bwd(h1t, dh2t, dkv, db, g_kv, g_b, g_post, out)

    dh1, d_out_a, dg_kv, dg_b_pre, dg_a_post = _rowwise(
        mid_bwd_fn, [h1, dh2, dhn_kv, dhn_b, out_a], [w["kv_norm"], w["b_pre_norm"], w["a_post_norm"]],
        [(D_MODEL, F32), (D_MODEL, BF16)], [D_MODEL] * 3, tm=tm, name="mid_bwd")

    dy_a = _matmul(d_out_a, w_a_out, tb=True, out_dtype=BF16, tm=mm, tn=A_WIDTH, name="a_out_dx")
    gw_a_out = _matmul(y_a, d_out_a, ta=True, out_dtype=BF16, tm=mm, tn=D_MODEL, tk=2048, name="a_out_dw")

    def gate_a_bwd(dy, o, z):
        do = dy * _silu(z)
        prod = do * o
        lane = lax.broadcasted_iota(jnp.int32, (prod.shape[0], A_HEADS), 1)
        dsum = jnp.zeros((prod.shape[0], A_HEADS), F32)
        for h in range(A_HEADS):
            col = jnp.sum(prod[:, h * A_HEAD_DIM:(h + 1) * A_HEAD_DIM], axis=1, keepdims=True)
            dsum = jnp.where(lane == h, col, dsum)
        return do, dy * o * _silu_grad(z), dsum

    do_a, dz_a, dsum_a = _rowwise(gate_a_bwd, [dy_a, o_a2, z_a], [],
                                  [(A_WIDTH, BF16), (A_WIDTH, BF16), (A_HEADS, F32)], tm=tm, name="a_gate_bwd")
    dhn_a = _matmul(dz_a, w_a_in, b_cols=(A_QKV, A_WIDTH), tb=True, out_dtype=F32, tm=mm, tn=D_MODEL,
                    name="a_gate_dx")
    gw_a_in, dhn_groups = None, []
    for g, d in enumerate(A_DILATIONS):
        s_n, ln = bsz * d, t // d
        dqkv = _attn_a_bwd(qk_g[g], v_g[g], to_group(do_a, d).reshape(s_n, ln, A_WIDTH), cols_to_rows(lse_a, d),
                           cols_to_rows(dsum_a, d), [a.reshape(s_n, ln, LANES) for a in tabs_g[g]],
                           f"attn_a_bwd_g{g}").reshape(n, 3 * A_WIDTH)
        w_cols = (g * 3 * A_WIDTH, 3 * A_WIDTH)
        if d == 1:
            dhn_a = _matmul(dqkv, w_a_in, b_cols=w_cols, tb=True, out_dtype=F32, tm=mm, tn=D_MODEL,
                            name=f"a_qkv_dx_g{g}", epilogue=_add_epilogue,
                            extras=[(dhn_a, (mm, D_MODEL), lambda j, i, kk: (i, j))])
        else:
            dhn_groups.append(from_group(_matmul(dqkv, w_a_in, b_cols=w_cols, tb=True, out_dtype=BF16, tm=mm,
                                                 tn=D_MODEL, name=f"a_qkv_dx_g{g}"), d))
        gw_a_in = _matmul(hn_g[g], dqkv, ta=True, out_dtype=BF16, tm=mm, tn=1024, tk=2048, name=f"a_qkv_dw_g{g}",
                          out_into=(gw_a_in, g * 3 * A_WIDTH, A_QKV + A_WIDTH))
    gw_a_in = _matmul(hn_a, dz_a, ta=True, out_dtype=BF16, tm=mm, tn=1024, tk=2048, name="a_gate_dw",
                      out_into=(gw_a_in, A_QKV, A_QKV + A_WIDTH))

    def first_bwd(xt, dhn, dhn_1, dhn_2, dh1t, g):
        dx, dg = _rms_bwd(xt, g, dhn + dhn_1 + dhn_2)
        return dh1t + dx, dg

    grad_x, dg_a_pre = _rowwise(first_bwd, [x2, dhn_a, *dhn_groups, dh1], [w["a_pre_norm"]], [(D_MODEL, F32)],
                                [D_MODEL], tm=tm, name="a_pre_norm_bwd")

    gw_down = jnp.concatenate([gw_down_p[:, :B_KV_LORA], gw_down_p[:, B_KV_LORA + B_NOPE:B_KV_LORA + B_QK_DIM]], axis=1)
    gw_up = jnp.concatenate([gw_upk.reshape(B_KV_LORA, B_HEADS, LANES)[:, :, :B_NOPE],
                             gw_upv.reshape(B_KV_LORA, B_HEADS, B_VDIM)], axis=2).reshape(B_KV_LORA, -1)
    gw_q_up = gw_q_p.reshape(B_Q_LORA, B_HEADS, LANES)[:, :, :B_QK_DIM].reshape(B_Q_LORA, -1)
    grads = {"a_w_in": gw_a_in, "a_w_out": gw_a_out, "kv_w_down": gw_down, "kv_w_up": gw_up,
             "b_w_in": gw_b_in, "b_w_q_up": gw_q_up, "b_w_out": gw_b_out}
    gains = {"a_pre_norm": dg_a_pre, "a_post_norm": dg_a_post, "kv_norm": dg_kv, "kv_latent_norm": dg_latent,
             "b_pre_norm": dg_b_pre, "b_q_norm": dg_q_norm, "b_post_norm": dg_b_post}
    gains = {k: jnp.sum(a, axis=0) for k, a in gains.items()}
    return jnp.sum(loss_acc), grad_x.reshape(bsz, t, D_MODEL), grads, gains


WEIGHT_ORDER = ("a_pre_norm", "a_w_in", "a_w_out", "a_post_norm", "kv_norm", "kv_w_down", "kv_latent_norm",
                "kv_w_up", "b_pre_norm", "b_w_in", "b_q_norm", "b_w_q_up", "b_w_out", "b_post_norm")
MATRICES = (("a_w_in", 1024, 10240, 1), ("a_w_out", 1024, 1024, 0), ("kv_w_down", 1024, 288, 0),
            ("kv_w_up", 256, 2048, 1), ("b_w_in", 1024, 1408, 1), ("b_w_q_up", 384, 1536, 1),
            ("b_w_out", 1024, 1024, 0))
SHARDED_GAINS = ("a_pre_norm", "a_post_norm")
GAIN_WIDTHS = (("a_pre_norm", 1024), ("a_post_norm", 1024), ("kv_norm", 1024), ("kv_latent_norm", 256),
               ("b_pre_norm", 1024), ("b_q_norm", 384), ("b_post_norm", 1024))
GAIN_ROWS = 48


def _shard_rows(rows, cols):
    return rows * cols // (N_DEV * LANES)


def _whole_from_blocks(blocks, rows, cols, axis):
    if axis == 1:
        return blocks.reshape(N_DEV, rows, cols // N_DEV).transpose(1, 0, 2).reshape(rows, cols)
    return blocks.reshape(rows, cols)


def _blocks_from_whole(whole, rows, cols, axis):
    if axis == 1:
        whole = whole.reshape(rows, N_DEV, cols // N_DEV).transpose(1, 0, 2)
    return whole.reshape(N_DEV, -1, LANES)


def kernel(x, positions, a_pre_norm, a_w_in, a_w_out, a_post_norm, kv_norm, kv_w_down, kv_latent_norm, kv_w_up, b_pre_norm, b_w_in, b_q_norm, b_w_q_up, b_w_out, b_post_norm, loss_target, m_a_pre_norm, m_a_w_in, m_a_w_out, m_a_post_norm, m_kv_norm, m_kv_w_down, m_kv_latent_norm, m_kv_w_up, m_b_pre_norm, m_b_w_in, m_b_q_norm, m_b_w_q_up, m_b_w_out, m_b_post_norm, v_a_pre_norm, v_a_w_in, v_a_w_out, v_a_post_norm, v_kv_norm, v_kv_w_down, v_kv_latent_norm, v_kv_w_up, v_b_pre_norm, v_b_w_in, v_b_q_norm, v_b_w_q_up, v_b_w_out, v_b_post_norm):
    weights = dict(a_pre_norm=a_pre_norm, a_w_in=a_w_in, a_w_out=a_w_out, a_post_norm=a_post_norm, kv_norm=kv_norm,
                   kv_w_down=kv_w_down, kv_latent_norm=kv_latent_norm, kv_w_up=kv_w_up, b_pre_norm=b_pre_norm,
                   b_w_in=b_w_in, b_q_norm=b_q_norm, b_w_q_up=b_w_q_up, b_w_out=b_w_out, b_post_norm=b_post_norm)
    m_in = dict(a_pre_norm=m_a_pre_norm, a_w_in=m_a_w_in, a_w_out=m_a_w_out, a_post_norm=m_a_post_norm,
                kv_norm=m_kv_norm, kv_w_down=m_kv_w_down, kv_latent_norm=m_kv_latent_norm, kv_w_up=m_kv_w_up,
                b_pre_norm=m_b_pre_norm, b_w_in=m_b_w_in, b_q_norm=m_b_q_norm, b_w_q_up=m_b_w_q_up,
                b_w_out=m_b_w_out, b_post_norm=m_b_post_norm)
    v_in = dict(a_pre_norm=v_a_pre_norm, a_w_in=v_a_w_in, a_w_out=v_a_w_out, a_post_norm=v_a_post_norm,
                kv_norm=v_kv_norm, kv_w_down=v_kv_w_down, kv_latent_norm=v_kv_latent_norm, kv_w_up=v_kv_w_up,
                b_pre_norm=v_b_pre_norm, b_w_in=v_b_w_in, b_q_norm=v_b_q_norm, b_w_q_up=v_b_w_q_up,
                b_w_out=v_b_w_out, b_post_norm=v_b_post_norm)
    me = 4 * lax.axis_index("x") + 2 * lax.axis_index("y") + lax.axis_index("c")

    wide = MATRICES[0][0]
    flat = jnp.concatenate([weights[name].astype(BF16).reshape(-1, LANES) for name, _, _, _ in MATRICES[1:]], axis=0)
    gathered, w_wide = _all_gather_weights(flat, weights[wide][0].astype(BF16), "gather_weights")
    whole = {wide: w_wide}
    off = 0
    for name, rows, cols, axis in MATRICES[1:]:
        nr = _shard_rows(rows, cols)
        whole[name] = _whole_from_blocks(gathered[:, off:off + nr], rows, cols, axis)
        off += nr
    gain_shard = jnp.concatenate([weights[name].reshape(1, LANES) for name in SHARDED_GAINS]
                                 + [jnp.zeros((8 - len(SHARDED_GAINS), LANES), F32)], axis=0)
    gain_blocks = _all_gather(gain_shard, "gather_gains")
    for i, name in enumerate(SHARDED_GAINS):
        whole[name] = gain_blocks[:, i, :].reshape(1, D_MODEL)
    for name in ("kv_norm", "kv_latent_norm", "b_pre_norm", "b_q_norm", "b_post_norm"):
        whole[name] = weights[name].reshape(1, -1)

    loss_part, grad_x, grads, gains = _local_step(x, positions, loss_target, whole)

    blocks = jnp.concatenate([_blocks_from_whole(grads[name], rows, cols, axis).astype(BF16)
                              for name, rows, cols, axis in MATRICES], axis=1)
    blocks = blocks.reshape(N_CHIPS, 2, blocks.shape[1], LANES)
    got = _exchange_sibling(blocks, "scatter_grads_core")
    core = lax.axis_index("c").astype(jnp.int32).reshape(1)
    landed = _exchange_chips(_add_pairs(blocks, got, core, tr=2512, name="add_core_grads"), "scatter_grads_chip")
    summed = _sum_slots(landed, tr=2512, name="sum_grads")
    grad_out = {}
    off = 0
    for name, rows, cols, axis in MATRICES:
        nr = _shard_rows(rows, cols)
        grad_out[name] = summed[off:off + nr].reshape(weights[name].shape)
        off += nr

    vec = jnp.concatenate([gains[name] for name, _ in GAIN_WIDTHS] + [jnp.full((LANES,), loss_part, F32)])
    vec = jnp.pad(vec, (0, GAIN_ROWS * LANES - vec.shape[0])).reshape(GAIN_ROWS, LANES)
    total = _sum_slots(_all_gather(vec, "gather_gain_grads"), tr=GAIN_ROWS, name="sum_gain_grads").reshape(-1)
    off = 0
    for name, width in GAIN_WIDTHS:
        g = total[off:off + width]
        if name in SHARDED_GAINS:
            g = lax.dynamic_slice(g, (me * LANES,), (LANES,))
        grad_out[name] = g.reshape(weights[name].shape)
        off += width
    loss = total[off]

    deltas, new_m, new_v = {}, {}, {}
    big = "a_w_in"
    deltas[big], new_m[big], new_v[big] = _adamw(weights[big], grad_out[big], m_in[big], v_in[big], "adamw_" + big)
    small = [name for name in WEIGHT_ORDER if name != big]
    res = _adamw_small(*[[d[name] for name in small] for d in (weights, grad_out, m_in, v_in)], "adamw_small")
    for out, vals in zip((deltas, new_m, new_v), res):
        out.update(zip(small, vals))
    return (loss, grad_x, *[grad_out[k] for k in WEIGHT_ORDER], *[deltas[k] for k in WEIGHT_ORDER],
            *[new_m[k] for k in WEIGHT_ORDER], *[new_v[k] for k in WEIGHT_ORDER])
```

```python
import jax
import jax.numpy as jnp
from jax import lax
from jax.experimental import pallas as pl
from jax.experimental.pallas import tpu as pltpu

F32 = jnp.float32
BF16 = jnp.bfloat16

N_DEV = 8
D_MODEL = 1024
NORM_EPS = 1e-6
A_GROUPS = 3
A_DILATIONS = (1, 4, 16)
A_HEADS = 8
A_HEAD_DIM = 128
A_WIDTH = 1024
A_ROT_DIM = 32
A_ROPE_THETA = 500000.0
A_QKV = A_GROUPS * 3 * A_WIDTH
B_HEADS = 16
B_NOPE = 64
B_ROPE = 32
B_QK_DIM = 96
B_VDIM = 64
B_Q_LORA = 384
B_KV_LORA = 256
B_ROPE_THETA = 10000.0
B_KPAD = B_HEADS * 128
ADAM_LR = 0.001
ADAM_B1 = 0.9
ADAM_B2 = 0.999
ADAM_EPS = 1e-08
ADAM_WD = 0.01
ADAM_STEP = 10

LANES = 128
BAND = 128
EPILOGUE_ROWS = 128
NEG = -1e30
VMEM_LIMIT = 56 * 1024 * 1024
MESH = pl.DeviceIdType.MESH


def _cparams(sem):
    return pltpu.CompilerParams(dimension_semantics=sem, vmem_limit_bytes=VMEM_LIMIT)


def _rowwise(fn, rows, bcast, outs, accs=(), *, tm, name):
    n = rows[0].shape[0]
    nr, nb, no = len(rows), len(bcast), len(outs)

    def body(*refs):
        res = fn(*[r[...] for r in refs[:nr + nb]])
        out_refs = refs[nr + nb:nr + nb + no]
        acc_refs = refs[nr + nb + no:]
        for r, v in zip(out_refs, res[:no]):
            r[...] = v.astype(r.dtype)
        if acc_refs:
            @pl.when(pl.program_id(0) == 0)
            def _():
                for r in acc_refs:
                    r[...] = jnp.zeros_like(r)
            for r, v in zip(acc_refs, res[no:]):
                r[...] += v.reshape(tm // 8, 8, v.shape[-1]).sum(axis=0)

    in_specs = [pl.BlockSpec((tm, a.shape[1]), lambda i: (i, 0)) for a in rows]
    in_specs += [pl.BlockSpec(a.shape, lambda i: (0, 0)) for a in bcast]
    out_specs = [pl.BlockSpec((tm, c), lambda i: (i, 0)) for c, _ in outs]
    out_specs += [pl.BlockSpec((8, c), lambda i: (0, 0)) for c in accs]
    out_shape = [jax.ShapeDtypeStruct((n, c), dt) for c, dt in outs]
    out_shape += [jax.ShapeDtypeStruct((8, c), F32) for c in accs]
    return pl.pallas_call(
        body, name=name, grid=(n // tm,), in_specs=in_specs, out_specs=out_specs, out_shape=out_shape,
        compiler_params=_cparams(("arbitrary",)))(*rows, *bcast)


def _matmul(a, b, *, out_dtype, tm, tn, tk=None, name, epilogue=None, extras=(), ta=False, tb=False,
            epilogue_rows=None, b_cols=None, out_into=None):
    epilogue_rows = epilogue_rows or tm
    k, m = a.shape[::-1] if not ta else a.shape
    col0, width = b_cols or (0, b.shape[1])
    n = b.shape[0] if tb else width
    tk = tk or k
    nk = k // tk
    b_off = col0 // (tk if tb else tn)
    assert col0 % (tk if tb else tn) == 0 and (k == width if tb else True)
    ne = len(extras)
    dot = _dot_tn if ta else (_dot_nt if tb else _dot)
    assert epilogue is None or (nk == 1 and not ta)

    def body(*refs):
        a_ref, b_ref = refs[:2]
        ex = refs[2:2 + ne]
        o_ref = refs[2 + ne + (1 if out_into is not None and out_into[0] is not None else 0)]
        if epilogue is not None:
            b_tile = b_ref[...].astype(BF16)
            for r0 in range(0, tm, epilogue_rows):
                rows = slice(r0, r0 + epilogue_rows)
                epilogue(dot(a_ref[rows, :].astype(BF16), b_tile), o_ref, rows, *ex)
            return
        part = dot(a_ref[...].astype(BF16), b_ref[...].astype(BF16))
        if nk == 1:
            o_ref[...] = part.astype(o_ref.dtype)
        else:
            acc_ref = refs[-1]
            kk = pl.program_id(2)

            @pl.when(kk == 0)
            def _():
                acc_ref[...] = part

            @pl.when(kk > 0)
            def _():
                acc_ref[...] += part

            @pl.when(kk == nk - 1)
            def _():
                o_ref[...] = acc_ref[...].astype(o_ref.dtype)

    a_spec = pl.BlockSpec((tk, tm), lambda j, i, kk: (kk, i)) if ta else pl.BlockSpec((tm, tk), lambda j, i, kk: (i, kk))
    b_spec = (pl.BlockSpec((tn, tk), lambda j, i, kk: (j, kk + b_off)) if tb
              else pl.BlockSpec((tk, tn), lambda j, i, kk: (kk, j + b_off)))
    in_specs = [a_spec, b_spec] + [pl.BlockSpec(bs, im) for _, bs, im in extras]
    operands = [a, b] + [e[0] for e in extras]
    aliases = {}
    if out_into is not None:
        prev, out0, n_total = out_into
        o_off = out0 // tn
        assert out0 % tn == 0
        if prev is not None:
            in_specs.append(ANY)
            operands.append(prev)
            aliases = {len(operands) - 1: 0}
    else:
        o_off, n_total = 0, n
    return pl.pallas_call(
        body, name=name, grid=(n // tn, m // tm, nk), in_specs=in_specs,
        out_specs=pl.BlockSpec((tm, tn), lambda j, i, kk: (i, j + o_off)),
        out_shape=jax.ShapeDtypeStruct((m, n_total), out_dtype), input_output_aliases=aliases,
        scratch_shapes=[pltpu.VMEM((tm, tn), F32)] if nk > 1 else [],
        compiler_params=_cparams(("parallel", "parallel", "arbitrary")))(*operands)


def _add_epilogue(acc, o_ref, rows, prev_ref):
    o_ref[rows, :] = (acc + prev_ref[rows, :]).astype(o_ref.dtype)


def _rope(x, c, sp, sm):
    return x * c + pltpu.roll(x, 16, 1) * sp + pltpu.roll(x, LANES - 16, 1) * sm


def _rope_t(dy, c, sp, sm):
    return dy * c + pltpu.roll(dy * sp, LANES - 16, 1) + pltpu.roll(dy * sm, 16, 1)


def _rope_tables(positions, theta, rot_dim, lane0, name):
    n = positions.shape[0]
    half = rot_dim // 2
    inv_freq = 1.0 / (theta ** (jnp.arange(half, dtype=F32) * (2.0 / rot_dim)))
    freq = jnp.concatenate([jnp.zeros((lane0,), F32), inv_freq, inv_freq,
                            jnp.zeros((LANES - lane0 - rot_dim,), F32)]).reshape(1, LANES)

    def fn(p, f):
        ang = p * f
        cos, sin = jnp.cos(ang), jnp.sin(ang)
        lane = lax.broadcasted_iota(jnp.int32, ang.shape, 1) - lane0
        first, second = (lane >= 0) & (lane < half), (lane >= half) & (lane < rot_dim)
        return jnp.where(first | second, cos, 1.0), jnp.where(second, sin, 0.0), jnp.where(first, -sin, 0.0)

    return _rowwise(fn, [positions.astype(F32).reshape(n, 1)], [freq], [(LANES, F32)] * 3, tm=512, name=name)


def _value_heads_t(c_kv, w_vt, bsz, t):
    n, k = c_kv.shape
    tm, tn = 512, 1024

    def body(a_ref, b_ref, o_ref):
        acc = _dot(a_ref[...], b_ref[...])
        lane = lax.broadcasted_iota(jnp.int32, (1, tn), 1)
        acc = acc + jnp.where(lax.rem(lane, LANES) == B_VDIM, 1.0, 0.0)
        o_ref[...] = acc.T.astype(BF16)

    per_seq = t // tm
    return pl.pallas_call(
        body, name="kv_up_v_t", grid=(w_vt.shape[1] // tn, n // tm),
        in_specs=[pl.BlockSpec((tm, k), lambda j, i: (i, 0)), pl.BlockSpec((k, tn), lambda j, i: (0, j))],
        out_specs=pl.BlockSpec((None, tn, tm), lambda j, i: (i // per_seq, j, lax.rem(i, per_seq))),
        out_shape=jax.ShapeDtypeStruct((bsz, w_vt.shape[1], t), BF16),
        compiler_params=_cparams(("parallel", "parallel")))(c_kv, w_vt)


def _rms(x, g):
    xf = x.astype(F32)
    return xf * lax.rsqrt(jnp.mean(xf * xf, axis=-1, keepdims=True) + NORM_EPS) * g


def _rms_bwd(x, g, dy):
    xf = x.astype(F32)
    rstd = lax.rsqrt(jnp.mean(xf * xf, axis=-1, keepdims=True) + NORM_EPS)
    xhat = xf * rstd
    dxhat = dy * g
    dx = rstd * (dxhat - xhat * jnp.mean(dxhat * xhat, axis=-1, keepdims=True))
    return dx, dy * xhat


def _silu(z):
    return z * jax.nn.sigmoid(z)


def _silu_grad(z):
    s = jax.nn.sigmoid(z)
    return s * (1.0 + z * (1.0 - s))


def _dot_nt(a, b):
    return lax.dot_general(a, b, (((1,), (1,)), ((), ())), preferred_element_type=F32)


def _dot_tn(a, b):
    return lax.dot_general(a, b, (((0,), (0,)), ((), ())), preferred_element_type=F32)


def _dot(a, b):
    return jnp.dot(a, b, preferred_element_type=F32)


A_SCALE = A_HEAD_DIM ** -0.5
LOG2E = 1.4426950408889634
A_C2 = A_SCALE * LOG2E
A_HEAD_GROUP = 4
A_FWD_HEAD_GROUP = 8


def _attn_a_fwd(qk, v, name):
    s_n, ln, _ = qk.shape
    nb = ln // BAND
    blk = (None, BAND, A_WIDTH)

    single = nb == 1

    def with_prev(prev_ref, cur_ref, hs):
        return cur_ref[:, hs] if single else jnp.concatenate([prev_ref[:, hs], cur_ref[:, hs]], axis=0)

    def body(q_ref, kc_ref, kp_ref, vc_ref, vp_ref, o_ref, lse_ref):
        n_keys = BAND if single else 2 * BAND
        kpos = lax.broadcasted_iota(jnp.int32, (n_keys, BAND), 0)
        qpos = lax.broadcasted_iota(jnp.int32, (n_keys, BAND), 1) + (n_keys - BAND)
        first_key = jnp.where(pl.program_id(1) > 0, 0, n_keys - BAND)
        mask = (kpos <= qpos) & (kpos >= qpos - BAND) & (kpos >= first_key)
        rows = []
        for h0 in range(0, A_HEADS, A_FWD_HEAD_GROUP):
            hss = [slice(h * A_HEAD_DIM, (h + 1) * A_HEAD_DIM) for h in range(h0, h0 + A_FWD_HEAD_GROUP)]
            sts = [_dot_nt(with_prev(kp_ref, kc_ref, hs), q_ref[:, hs]) for hs in hss]
            ps, ls = [], []
            for st in sts:
                st = jnp.where(mask, st * A_C2, NEG)
                m = jnp.max(st, axis=0, keepdims=True)
                p = jnp.exp2(st - m)
                l_row = jnp.sum(p, axis=0, keepdims=True)
                ps.append(p.astype(BF16))
                ls.append(l_row)
                rows.append(m + jnp.log2(l_row))
            ots = [_dot_tn(with_prev(vp_ref, vc_ref, hs), p) for hs, p in zip(hss, ps)]
            for hs, o_t, l_row in zip(hss, ots, ls):
                o_ref[:, hs] = (o_t / l_row).T.astype(BF16)
        lse_ref[...] = jnp.concatenate(rows, axis=0)

    def col(c, off):
        return lambda s, l: (s, jnp.maximum(l + off, 0), c)

    return pl.pallas_call(
        body, name=name, grid=(s_n, nb),
        in_specs=[pl.BlockSpec(blk, col(0, 0)), pl.BlockSpec(blk, col(1, 0)), pl.BlockSpec(blk, col(1, -1)),
                  pl.BlockSpec(blk, col(0, 0)), pl.BlockSpec(blk, col(0, -1))],
        out_specs=[pl.BlockSpec(blk, lambda s, l: (s, l, 0)),
                   pl.BlockSpec((None, A_HEADS, BAND), lambda s, l: (s, 0, l))],
        out_shape=[jax.ShapeDtypeStruct((s_n, ln, A_WIDTH), BF16), jax.ShapeDtypeStruct((s_n, A_HEADS, ln), F32)],
        compiler_params=_cparams(("parallel", "arbitrary")))(qk, qk, qk, v, v)


def _attn_a_bwd(qk, v, do, lse2, dsum, tabs, name):
    s_n, ln, _ = qk.shape
    nb = ln // BAND
    blk = (None, BAND, A_WIDTH)

    def body(q_ref, qn_ref, kc_ref, kp_ref, vc_ref, vp_ref, do_ref, don_ref, lse_ref, lsen_ref, ds_ref, dsn_ref,
             c_ref, sp_ref, sm_ref, out_ref):
        l_idx = pl.program_id(1)
        kpos = lax.broadcasted_iota(jnp.int32, (2 * BAND, BAND), 0)
        qpos = lax.broadcasted_iota(jnp.int32, (2 * BAND, BAND), 1) + BAND
        first_key = jnp.where(l_idx > 0, 0, BAND)
        mask_q = (kpos <= qpos) & (kpos >= qpos - BAND) & (kpos >= first_key)
        kpos2 = lax.broadcasted_iota(jnp.int32, (BAND, 2 * BAND), 0)
        qpos2 = lax.broadcasted_iota(jnp.int32, (BAND, 2 * BAND), 1)
        last_query = jnp.where(l_idx < nb - 1, 2 * BAND, BAND)
        mask_k = (kpos2 <= qpos2) & (kpos2 >= qpos2 - BAND) & (qpos2 < last_query)
        c, sp, sm = c_ref[...], sp_ref[...], sm_ref[...]
        lse_q, ds_q = lse_ref[...], ds_ref[...]

        def store(hl, hss, dqs, dks, dvs):
            for i, h in enumerate(hl):
                out_ref[:, hss[i]] = _rope_t(dqs[i], c, sp, sm).astype(BF16)
                out_ref[:, A_WIDTH + h * A_HEAD_DIM:A_WIDTH + (h + 1) * A_HEAD_DIM] = _rope_t(dks[i], c, sp, sm).astype(BF16)
                out_ref[:, 2 * A_WIDTH + h * A_HEAD_DIM:2 * A_WIDTH + (h + 1) * A_HEAD_DIM] = dvs[i].astype(BF16)

        if nb == 1:
            causal = kpos[:BAND] <= qpos[:BAND] - BAND
            for h0 in range(0, A_HEADS, A_HEAD_GROUP):
                hl = list(range(h0, h0 + A_HEAD_GROUP))
                hss = [slice(h * A_HEAD_DIM, (h + 1) * A_HEAD_DIM) for h in hl]
                sts = [_dot_nt(kc_ref[:, hs], q_ref[:, hs]) for hs in hss]
                dpts = [_dot_nt(vc_ref[:, hs], do_ref[:, hs]) for hs in hss]
                ps, dsts = [], []
                for i, h in enumerate(hl):
                    p = jnp.exp2(jnp.where(causal, sts[i] * A_C2, NEG) - lse_q[h:h + 1])
                    dsts.append((p * (dpts[i] - ds_q[h:h + 1]) * A_SCALE).astype(BF16))
                    ps.append(p.astype(BF16))
                store(hl, hss, [_dot_tn(dst, kc_ref[:, hs]) for dst, hs in zip(dsts, hss)],
                      [_dot(dst, q_ref[:, hs]) for dst, hs in zip(dsts, hss)],
                      [_dot(p, do_ref[:, hs]) for p, hs in zip(ps, hss)])
            return
        lse_k = jnp.concatenate([lse_q, lsen_ref[...]], axis=1)
        ds_k = jnp.concatenate([ds_q, dsn_ref[...]], axis=1)
        for h0 in range(0, A_HEADS, A_HEAD_GROUP):
            hl = list(range(h0, h0 + A_HEAD_GROUP))
            hss = [slice(h * A_HEAD_DIM, (h + 1) * A_HEAD_DIM) for h in hl]
            k2s = [jnp.concatenate([kp_ref[:, hs], kc_ref[:, hs]], axis=0) for hs in hss]
            v2s = [jnp.concatenate([vp_ref[:, hs], vc_ref[:, hs]], axis=0) for hs in hss]
            q2s = [jnp.concatenate([q_ref[:, hs], qn_ref[:, hs]], axis=0) for hs in hss]
            do2s = [jnp.concatenate([do_ref[:, hs], don_ref[:, hs]], axis=0) for hs in hss]
            sts = [_dot_nt(k2, q_ref[:, hs]) for k2, hs in zip(k2s, hss)]
            dpts = [_dot_nt(v2, do_ref[:, hs]) for v2, hs in zip(v2s, hss)]
            st2s = [_dot_nt(kc_ref[:, hs], q2) for q2, hs in zip(q2s, hss)]
            dpt2s = [_dot_nt(vc_ref[:, hs], do2) for do2, hs in zip(do2s, hss)]
            dsts, dst2s, p2s = [], [], []
            for i, h in enumerate(hl):
                p = jnp.exp2(jnp.where(mask_q, sts[i] * A_C2, NEG) - lse_q[h:h + 1])
                dsts.append((p * (dpts[i] - ds_q[h:h + 1]) * A_SCALE).astype(BF16))
                p2 = jnp.exp2(jnp.where(mask_k, st2s[i] * A_C2, NEG) - lse_k[h:h + 1])
                dst2s.append((p2 * (dpt2s[i] - ds_k[h:h + 1]) * A_SCALE).astype(BF16))
                p2s.append(p2.astype(BF16))
            dqs = [_dot_tn(dsts[i], k2s[i]) for i in range(A_HEAD_GROUP)]
            dks = [_dot(dst2s[i], q2s[i]) for i in range(A_HEAD_GROUP)]
            dvs = [_dot(p2s[i], do2s[i]) for i in range(A_HEAD_GROUP)]
            store(hl, hss, dqs, dks, dvs)

    def col(c, off):
        return lambda s, l: (s, jnp.clip(l + off, 0, nb - 1), c)

    def row(off):
        return pl.BlockSpec((None, A_HEADS, BAND), lambda s, l: (s, 0, jnp.clip(l + off, 0, nb - 1)))

    tspec = pl.BlockSpec((None, BAND, LANES), lambda s, l: (s, l, 0))
    in_specs = [pl.BlockSpec(blk, col(0, 0)), pl.BlockSpec(blk, col(0, 1)),
                pl.BlockSpec(blk, col(1, 0)), pl.BlockSpec(blk, col(1, -1)),
                pl.BlockSpec(blk, col(0, 0)), pl.BlockSpec(blk, col(0, -1)),
                pl.BlockSpec(blk, col(0, 0)), pl.BlockSpec(blk, col(0, 1)),
                row(0), row(1), row(0), row(1), tspec, tspec, tspec]
    return pl.pallas_call(
        body, name=name, grid=(s_n, nb), in_specs=in_specs,
        out_specs=pl.BlockSpec((None, BAND, 3 * A_WIDTH), lambda s, l: (s, l, 0)),
        out_shape=jax.ShapeDtypeStruct((s_n, ln, 3 * A_WIDTH), BF16),
        compiler_params=_cparams(("parallel", "arbitrary")))(
            qk, qk, qk, qk, v, v, do, do, lse2, lse2, dsum, dsum, *tabs)


B_TQ = 256
B_FWD_HEADS = 4
B_BWD_PAIRS = 2
B_SCALE = B_QK_DIM ** -0.5
B_C2 = B_SCALE * LOG2E


def _key_le_query(kb, qb, tk, tq):
    kpos = kb * tk + lax.broadcasted_iota(jnp.int32, (tk, tq), 0)
    qpos = qb * tq + lax.broadcasted_iota(jnp.int32, (tk, tq), 1)
    return kpos <= qpos


def _attn_b_fwd(q, kpad, vt1):
    bsz, t, _ = q.shape
    tq = tk = B_TQ
    nq = t // tq
    nh = B_FWD_HEADS

    def body(q_ref, k_ref, vt_ref, o_ref, lse_ref):
        qblk = pl.program_id(2)
        qs = [q_ref[:, hh * LANES:(hh + 1) * LANES] for hh in range(nh)]

        def scores(kb):
            start = pl.multiple_of(kb * tk, tk)
            return [_dot_nt(k_ref[pl.ds(start, tk), hh * LANES:(hh + 1) * LANES], qs[hh]) for hh in range(nh)]

        def pv(kb, ps):
            start = pl.multiple_of(kb * tk, tk)
            return [_dot(vt_ref[hh * LANES:(hh + 1) * LANES, pl.ds(start, tk)], ps[hh]) for hh in range(nh)]

        def softmax(ss, ms, accs, kb, masked):
            out_m, out_acc, out_p = [], [], []
            for hh in range(nh):
                s = ss[hh] * B_C2
                if masked:
                    s = jnp.where(_key_le_query(kb, qblk, tk, tq), s, NEG)
                m_new = jnp.maximum(ms[hh], jnp.max(s, axis=0, keepdims=True))
                out_acc.append(jnp.exp2(ms[hh] - m_new) * accs[hh])
                out_p.append(jnp.exp2(s - m_new).astype(BF16))
                out_m.append(m_new)
            return out_m, out_acc, out_p

        def step(kb, carry):
            ss, ps, ms, accs = carry
            pvs = pv(jnp.maximum(kb - 1, 0), ps)
            ss_next = scores(kb + 1)
            accs = [accs[hh] + pvs[hh] for hh in range(nh)]
            ms, accs, ps = softmax(ss, ms, accs, kb, False)
            return (ss_next, ps, ms, accs)

        init = (scores(0), [jnp.zeros((tk, tq), BF16)] * nh, [jnp.full((1, tq), NEG, F32)] * nh,
                [jnp.zeros((LANES, tq), F32)] * nh)
        ss, ps, ms, accs = lax.fori_loop(0, qblk, step, init)
        pvs = pv(jnp.maximum(qblk - 1, 0), ps)
        accs = [accs[hh] + pvs[hh] for hh in range(nh)]
        ms, accs, ps = softmax(ss, ms, accs, qblk, True)
        pvs = pv(qblk, ps)
        accs = [accs[hh] + pvs[hh] for hh in range(nh)]
        ls = [accs[hh][B_VDIM:B_VDIM + 1] for hh in range(nh)]
        o_t = jnp.concatenate([accs[hh][:B_VDIM] / ls[hh] for hh in range(nh)], axis=0)
        o_ref[...] = o_t.T
        for pair in range(nh // 2):
            lse_ref[pair] = jnp.concatenate([ms[2 * pair + hh] + jnp.log2(ls[2 * pair + hh]) for hh in range(2)]
                                            + [jnp.zeros((6, tq), F32)], axis=0)

    return pl.pallas_call(
        body, name="attn_b_fwd", grid=(bsz, B_HEADS // nh, nq),
        in_specs=[pl.BlockSpec((None, tq, nh * LANES), lambda b, j, i: (b, i, j)),
                  pl.BlockSpec((None, t, nh * LANES), lambda b, j, i: (b, 0, j)),
                  pl.BlockSpec((None, nh * LANES, t), lambda b, j, i: (b, j, 0))],
        out_specs=[pl.BlockSpec((None, tq, nh * B_VDIM), lambda b, j, i: (b, i, j)),
                   pl.BlockSpec((None, nh // 2, 8, tq), lambda b, j, i: (b, j, 0, i))],
        out_shape=[jax.ShapeDtypeStruct((bsz, t, B_HEADS * B_VDIM), F32),
                   jax.ShapeDtypeStruct((bsz, B_HEADS // 2, 8, t), F32)],
        compiler_params=_cparams(("parallel", "parallel", "arbitrary")))(q, kpad, vt1)


def _attn_b_bwd(q, kpad, v, do, o, lse2, tabs):
    bsz, t, _ = q.shape
    tq = tk = B_TQ
    nq = t // tq
    n_pairs = B_BWD_PAIRS
    n_heads = 2 * n_pairs

    def body(q_ref, k_ref, v_ref, do_ref, o_ref, lse_ref, c_ref, sp_ref, sm_ref, dq_ref, dk_ref, dv_ref,
             dqt_scr, dsum_scr):
        lane = lax.broadcasted_iota(jnp.int32, (tk, LANES), 1)
        sel_lane = lax.broadcasted_iota(jnp.int32, (8, LANES), 1)
        sel_row = lax.broadcasted_iota(jnp.int32, (8, LANES), 0)
        sel = jnp.where((sel_lane < B_VDIM) == (sel_row == 0), 1.0, 0.0)
        sel = jnp.where(sel_row < 2, sel, 0.0).astype(BF16)

        def rows(blk):
            return pl.ds(pl.multiple_of(blk * tq, tq), tq)

        def dsum_step(qb, carry):
            for pp in range(n_pairs):
                pls = slice(pp * LANES, (pp + 1) * LANES)
                prod = do_ref[rows(qb), pls].astype(F32) * o_ref[rows(qb), pls]
                hi = prod.astype(BF16)
                lo = (prod - hi.astype(F32)).astype(BF16)
                dsum_scr[pp, :, rows(qb)] = _dot_nt(sel, hi) + _dot_nt(sel, lo)
            return carry

        lax.fori_loop(0, nq, dsum_step, 0)
        dqt_scr[...] = jnp.zeros_like(dqt_scr)

        def kv_step(kb, carry):
            ks = [k_ref[rows(kb), hh * LANES:(hh + 1) * LANES] for hh in range(n_heads)]
            vs = []
            for pp in range(n_pairs):
                vb = v_ref[rows(kb), pp * LANES:(pp + 1) * LANES]
                zero = jnp.zeros_like(vb)
                vs += [jnp.where(lane < B_VDIM, vb, zero), jnp.where(lane < B_VDIM, zero, vb)]

            def make_step(masked):
                def step(qb, acc):
                    qs = [q_ref[rows(qb), hh * LANES:(hh + 1) * LANES] for hh in range(n_heads)]
                    dob = [do_ref[rows(qb), pp * LANES:(pp + 1) * LANES] for pp in range(n_pairs)]
                    ss = [_dot_nt(ks[hh], qs[hh]) for hh in range(n_heads)]
                    dps = [_dot_nt(vs[hh], dob[hh // 2]) for hh in range(n_heads)]
                    pbs, dss = [], []
                    for hh in range(n_heads):
                        stat = (hh // 2, slice(hh % 2, hh % 2 + 1), rows(qb))
                        s = ss[hh] * B_C2
                        if masked:
                            s = jnp.where(_key_le_query(kb, qb, tk, tq), s, NEG)
                        p = jnp.exp2(s - lse_ref[stat])
                        dss.append((p * (dps[hh] - dsum_scr[stat]) * B_SCALE).astype(BF16))
                        pbs.append(p.astype(BF16))
                    for hh in range(n_heads):
                        dqt_scr[hh, :, rows(qb)] += _dot_tn(ks[hh], dss[hh])
                    return tuple([acc[hh] + _dot(dss[hh], qs[hh]) for hh in range(n_heads)]
                                 + [acc[n_heads + hh] + _dot(pbs[hh], dob[hh // 2]) for hh in range(n_heads)])
                return step

            acc = make_step(True)(kb, (jnp.zeros((tk, LANES), F32),) * (2 * n_heads))
            acc = lax.fori_loop(kb + 1, nq, make_step(False), acc)
            for hh in range(n_heads):
                dk_ref[rows(kb), hh * LANES:(hh + 1) * LANES] = acc[hh].astype(BF16)
            for pp in range(n_pairs):
                dv_pair = jnp.where(lane < B_VDIM, acc[n_heads + 2 * pp], acc[n_heads + 2 * pp + 1])
                dv_ref[rows(kb), pp * LANES:(pp + 1) * LANES] = dv_pair.astype(BF16)
            return carry

        lax.fori_loop(0, nq, kv_step, 0)

        def dq_step(qb, carry):
            c, sp, sm = c_ref[rows(qb), :], sp_ref[rows(qb), :], sm_ref[rows(qb), :]
            for hh in range(n_heads):
                dq_ref[rows(qb), hh * LANES:(hh + 1) * LANES] = _rope_t(dqt_scr[hh, :, rows(qb)].T, c, sp, sm).astype(BF16)
            return carry

        lax.fori_loop(0, nq, dq_step, 0)

    pair_full = pl.BlockSpec((None, t, n_heads * LANES), lambda b, j: (b, 0, j))
    one_full = pl.BlockSpec((None, t, n_pairs * LANES), lambda b, j: (b, 0, j))
    row_full = pl.BlockSpec((None, n_pairs, 8, t), lambda b, j: (b, j, 0, 0))
    tab_full = pl.BlockSpec((None, t, LANES), lambda b, j: (b, 0, 0))
    return pl.pallas_call(
        body, name="attn_b_bwd", grid=(bsz, B_HEADS // n_heads),
        in_specs=[pair_full, pair_full, one_full, one_full, one_full, row_full, tab_full, tab_full, tab_full],
        out_specs=[pair_full, pair_full, one_full],
        out_shape=[jax.ShapeDtypeStruct((bsz, t, B_KPAD), BF16), jax.ShapeDtypeStruct((bsz, t, B_KPAD), BF16),
                   jax.ShapeDtypeStruct((bsz, t, B_HEADS * B_VDIM), BF16)],
        scratch_shapes=[pltpu.VMEM((n_heads, LANES, t), F32), pltpu.VMEM((n_pairs, 8, t), F32)],
        compiler_params=_cparams(("parallel", "parallel")))(q, kpad, v, do, o, lse2, *tabs)


ANY = pl.BlockSpec(memory_space=pl.ANY)


def _all_gather(shard, name):
    def body(x_ref, out_ref, send_sems, recv_sems, local_sem):
        x, y, c = lax.axis_index("x"), lax.axis_index("y"), lax.axis_index("c")
        me, sibling = (x, y, c), (x, y, 1 - c)
        chips = [(1 - x, y), (x, 1 - y), (1 - x, 1 - y)]

        def rows(px, py, pc):
            return out_ref.at[4 * px + 2 * py + pc]

        def copy(k, block, to, src=None):
            return pltpu.make_async_remote_copy(
                src_ref=rows(*block) if src is None else src, dst_ref=rows(*block),
                send_sem=send_sems.at[k], recv_sem=recv_sems.at[k], device_id=to, device_id_type=MESH)

        mine = pltpu.make_async_copy(x_ref, rows(*me), local_sem)
        mine.start()
        first = [copy(0, me, sibling, src=x_ref)]
        first += [copy(1 + j, me, (*chip, c), src=x_ref) for j, chip in enumerate(chips)]
        for cp in first:
            cp.start()
        passed = [copy(4 + j, (*chip, c), sibling) for j, chip in enumerate(chips)]
        for j, chip in enumerate(chips):
            copy(1 + j, (*chip, c), me).wait_recv()
            passed[j].start()
        copy(0, sibling, me).wait_recv()
        for j, chip in enumerate(chips):
            copy(4 + j, (*chip, 1 - c), me).wait_recv()
        for cp in first + passed:
            cp.wait_send()
        mine.wait()

    return pl.pallas_call(
        body, name=name, in_specs=[ANY], out_specs=ANY,
        out_shape=jax.ShapeDtypeStruct((N_DEV,) + shard.shape, shard.dtype),
        scratch_shapes=[pltpu.SemaphoreType.DMA((7,)), pltpu.SemaphoreType.DMA((7,)), pltpu.SemaphoreType.DMA])(shard)


def _all_gather_weights(flat, wide, name):
    ns = wide.shape[1]

    def body(f_ref, w_ref, fo_ref, wo_ref, send_sems, recv_sems, local_sems):
        x, y, c = lax.axis_index("x"), lax.axis_index("y"), lax.axis_index("c")
        me, sibling = (x, y, c), (x, y, 1 - c)
        chips = [(1 - x, y), (x, 1 - y), (1 - x, 1 - y)]

        def place(a, px, py, pc):
            idx = 4 * px + 2 * py + pc
            if a == 0:
                return fo_ref.at[idx]
            return wo_ref.at[:, pl.ds(pl.multiple_of(idx * ns, LANES), ns)]

        def copy(a, k, block, to, src=None):
            return pltpu.make_async_remote_copy(
                src_ref=place(a, *block) if src is None else src, dst_ref=place(a, *block),
                send_sem=send_sems.at[a, k], recv_sem=recv_sems.at[a, k], device_id=to, device_id_type=MESH)

        own = (f_ref, w_ref)
        mine = [pltpu.make_async_copy(own[a], place(a, *me), local_sems.at[a]) for a in range(2)]
        first = []
        for a in range(2):
            mine[a].start()
            first.append(copy(a, 0, me, sibling, src=own[a]))
            first += [copy(a, 1 + j, me, (*chip, c), src=own[a]) for j, chip in enumerate(chips)]
        for cp in first:
            cp.start()
        passed = [[copy(a, 4 + j, (*chip, c), sibling) for j, chip in enumerate(chips)] for a in range(2)]
        for a in range(2):
            for j, chip in enumerate(chips):
                copy(a, 1 + j, (*chip, c), me).wait_recv()
                passed[a][j].start()
        for a in range(2):
            copy(a, 0, sibling, me).wait_recv()
            for j, chip in enumerate(chips):
                copy(a, 4 + j, (*chip, 1 - c), me).wait_recv()
        for cp in first + passed[0] + passed[1]:
            cp.wait_send()
        for cp in mine:
            cp.wait()

    return pl.pallas_call(
        body, name=name, in_specs=[ANY, ANY], out_specs=[ANY, ANY],
        out_shape=[jax.ShapeDtypeStruct((N_DEV,) + flat.shape, flat.dtype),
                   jax.ShapeDtypeStruct((wide.shape[0], N_DEV * ns), wide.dtype)],
        scratch_shapes=[pltpu.SemaphoreType.DMA((2, 7)), pltpu.SemaphoreType.DMA((2, 7)),
                        pltpu.SemaphoreType.DMA((2,))])(flat, wide)


N_CHIPS = 4


def _exchange_sibling(blocks, name):
    def body(g_ref, got_ref, send_sems, recv_sems):
        x, y, c = lax.axis_index("x"), lax.axis_index("y"), lax.axis_index("c")
        sends = [pltpu.make_async_remote_copy(
            src_ref=g_ref.at[q, 1 - c], dst_ref=got_ref.at[q], send_sem=send_sems.at[q],
            recv_sem=recv_sems.at[q], device_id=(x, y, 1 - c), device_id_type=MESH) for q in range(N_CHIPS)]
        for cp in sends:
            cp.start()
        for cp in sends:
            cp.wait_recv()
        for cp in sends:
            cp.wait_send()

    return pl.pallas_call(
        body, name=name, in_specs=[ANY], out_specs=ANY,
        out_shape=jax.ShapeDtypeStruct((N_CHIPS,) + blocks.shape[2:], blocks.dtype),
        scratch_shapes=[pltpu.SemaphoreType.DMA((N_CHIPS,)), pltpu.SemaphoreType.DMA((N_CHIPS,))])(blocks)


def _exchange_chips(parts, name):
    def body(p_ref, out_ref, send_sems, recv_sems, local_sem):
        x, y, c = lax.axis_index("x"), lax.axis_index("y"), lax.axis_index("c")
        me = 2 * x + y

        def peer(k):
            return (1 - x if k & 2 else x, 1 - y if k & 1 else y)

        def copy(k):
            px, py = peer(k)
            return pltpu.make_async_remote_copy(
                src_ref=p_ref.at[2 * px + py], dst_ref=out_ref.at[me], send_sem=send_sems.at[k - 1],
                recv_sem=recv_sems.at[k - 1], device_id=(px, py, c), device_id_type=MESH)

        def arrival(k):
            px, py = peer(k)
            slot = out_ref.at[2 * px + py]
            return pltpu.make_async_remote_copy(
                src_ref=slot, dst_ref=slot, send_sem=send_sems.at[k - 1], recv_sem=recv_sems.at[k - 1],
                device_id=(px, py, c), device_id_type=MESH)

        mine = pltpu.make_async_copy(p_ref.at[me], out_ref.at[me], local_sem)
        mine.start()
        sends = [copy(k) for k in range(1, N_CHIPS)]
        for cp in sends:
            cp.start()
        for k in range(1, N_CHIPS):
            arrival(k).wait_recv()
        for cp in sends:
            cp.wait_send()
        mine.wait()

    return pl.pallas_call(
        body, name=name, in_specs=[ANY], out_specs=ANY,
        out_shape=jax.ShapeDtypeStruct(parts.shape, parts.dtype),
        scratch_shapes=[pltpu.SemaphoreType.DMA((N_CHIPS - 1,)), pltpu.SemaphoreType.DMA((N_CHIPS - 1,)),
                        pltpu.SemaphoreType.DMA])(parts)


def _add_pairs(blocks, got, core, *, tr, name):
    q, r, c = got.shape

    def body(core_ref, a_ref, b_ref, o_ref):
        o_ref[...] = (a_ref[...].astype(F32) + b_ref[...].astype(F32)).astype(o_ref.dtype)

    spec = pl.BlockSpec((q, tr, c), lambda i, core_ref: (0, i, 0))
    mine = pl.BlockSpec((q, None, tr, c), lambda i, core_ref: (0, core_ref[0], i, 0))
    return pl.pallas_call(
        body, name=name,
        grid_spec=pltpu.PrefetchScalarGridSpec(num_scalar_prefetch=1, grid=(r // tr,), in_specs=[mine, spec],
                                               out_specs=spec),
        out_shape=jax.ShapeDtypeStruct(got.shape, BF16), compiler_params=_cparams(("parallel",)))(
            core, blocks, got)


def _sum_slots(slots, *, tr, name):
    n_slots, r, c = slots.shape

    def body(s_ref, o_ref):
        acc = s_ref[0].astype(F32)
        for s in range(1, n_slots):
            acc = acc + s_ref[s].astype(F32)
        o_ref[...] = acc

    return pl.pallas_call(
        body, name=name, grid=(r // tr,),
        in_specs=[pl.BlockSpec((n_slots, tr, c), lambda i: (0, i, 0))],
        out_specs=pl.BlockSpec((tr, c), lambda i: (i, 0)),
        out_shape=jax.ShapeDtypeStruct((r, c), F32),
        compiler_params=_cparams(("parallel",)))(slots)


def _adamw_math(w_t, g_t, m_t, v_t):
    m_n = ADAM_B1 * m_t + (1.0 - ADAM_B1) * g_t
    v_n = ADAM_B2 * v_t + (1.0 - ADAM_B2) * (g_t * g_t)
    m_hat = m_n / (1.0 - ADAM_B1 ** ADAM_STEP)
    v_hat = v_n / (1.0 - ADAM_B2 ** ADAM_STEP)
    delta = -ADAM_LR * (m_hat / (jnp.sqrt(v_hat) + ADAM_EPS) + ADAM_WD * w_t)
    return delta, m_n, v_n


def _adamw(w, g, m, v, name):
    shape = w.shape
    cols = shape[-1]
    args = [a.reshape(-1, cols) for a in (w, g, m, v)]
    rows = args[0].shape[0]
    tm = 256 if rows % 256 == 0 else rows
    delta, m_n, v_n = _rowwise(_adamw_math, args, [], [(cols, F32)] * 3, tm=tm, name=name)
    return delta.reshape(shape), m_n.reshape(shape), v_n.reshape(shape)


def _adamw_small(ws, gs, ms, vs, name):
    k = len(ws)
    args = [a.reshape(-1, a.shape[-1]) for group in (ws, gs, ms, vs) for a in group]

    def body(*refs):
        ins, outs = refs[:4 * k], refs[4 * k:]
        for i in range(k):
            res = _adamw_math(*[ins[j * k + i][...] for j in range(4)])
            for j in range(3):
                outs[j * k + i][...] = res[j]

    res = pl.pallas_call(
        body, name=name, out_shape=[jax.ShapeDtypeStruct(a.shape, F32) for a in args[:k]] * 3,
        compiler_params=pltpu.CompilerParams(vmem_limit_bytes=VMEM_LIMIT))(*args)
    return [[res[j * k + i].reshape(ws[i].shape) for i in range(k)] for j in range(3)]


def _local_step(x, positions, target, w):
    bsz, t, _ = x.shape
    n = bsz * t
    tm = 256
    mm = 512
    x2 = x.reshape(n, D_MODEL)
    tgt2 = target.reshape(n, D_MODEL)
    pos = positions.reshape(n)
    tb2 = _rope_tables(pos, B_ROPE_THETA, B_ROPE, B_NOPE, "rope_tables_b")
    tabs_b = [a.reshape(bsz, t, LANES) for a in tb2]

    w_a_in = w["a_w_in"]
    w_a_out = w["a_w_out"]
    w_down = w["kv_w_down"]
    w_down_p = jnp.zeros((D_MODEL, 3 * LANES), BF16).at[:, :B_KV_LORA].set(w_down[:, :B_KV_LORA])
    w_down_p = w_down_p.at[:, B_KV_LORA + B_NOPE:B_KV_LORA + B_QK_DIM].set(w_down[:, B_KV_LORA:])
    wu = w["kv_w_up"].reshape(B_KV_LORA, B_HEADS, B_NOPE + B_VDIM)
    w_upk = jnp.pad(wu[:, :, :B_NOPE], ((0, 0), (0, 0), (0, LANES - B_NOPE))).reshape(B_KV_LORA, B_KPAD)
    w_upv = wu[:, :, B_NOPE:].reshape(B_KV_LORA, B_HEADS * B_VDIM)
    w_b_in = w["b_w_in"]
    w_q_p = jnp.pad(w["b_w_q_up"].reshape(B_Q_LORA, B_HEADS, B_QK_DIM),
                    ((0, 0), (0, 0), (0, LANES - B_QK_DIM))).reshape(B_Q_LORA, B_KPAD)
    w_b_out = w["b_w_out"]

    def tab_extras(tabs2, rows):
        return [(a, (rows, LANES), lambda j, i, kk: (i, 0)) for a in tabs2]

    (hn_a,) = _rowwise(lambda xt, g: (_rms(xt, g),), [x2], [w["a_pre_norm"]], [(D_MODEL, BF16)],
                       tm=tm, name="a_pre_norm")

    def rope_epilogue(acc, o_ref, rows, c_ref, sp_ref, sm_ref):
        c, sp, sm = c_ref[rows, :], sp_ref[rows, :], sm_ref[rows, :]
        for h in range(acc.shape[1] // LANES):
            hs = slice(h * LANES, (h + 1) * LANES)
            o_ref[rows, hs] = _rope(acc[:, hs], c, sp, sm).astype(BF16)

    def to_group(a, d):
        if d == 1:
            return a
        return a.reshape(bsz, t // d, d, a.shape[-1]).transpose(0, 2, 1, 3).reshape(n, a.shape[-1])

    def from_group(a, d):
        if d == 1:
            return a
        return a.reshape(bsz, d, t // d, a.shape[-1]).transpose(0, 2, 1, 3).reshape(n, a.shape[-1])

    def rows_to_cols(r, d):
        return r.reshape(bsz, d, A_HEADS, t // d).transpose(0, 3, 1, 2).reshape(n, A_HEADS)

    def cols_to_rows(cc, d):
        return cc.reshape(bsz, t // d, d, A_HEADS).transpose(0, 2, 3, 1).reshape(bsz * d, A_HEADS, t // d)

    z_a = _matmul(hn_a, w_a_in, b_cols=(A_QKV, A_WIDTH), out_dtype=F32, tm=mm, tn=A_WIDTH, name="a_gate")
    hn_g, tabs_g, qk_g, v_g, o_g, lse_g = [], [], [], [], [], []
    for g, d in enumerate(A_DILATIONS):
        hn_g.append(to_group(hn_a, d))
        tabs_g.append(_rope_tables(to_group(pos.reshape(n, 1), d).reshape(n), A_ROPE_THETA, A_ROT_DIM, 0,
                                   f"rope_tables_a_g{g}"))
        col0 = g * 3 * A_WIDTH
        qk = _matmul(hn_g[g], w_a_in, b_cols=(col0, 2 * A_WIDTH), out_dtype=BF16, tm=mm, tn=A_WIDTH,
                     name=f"a_qk_g{g}", epilogue=rope_epilogue, epilogue_rows=EPILOGUE_ROWS,
                     extras=tab_extras(tabs_g[g], mm))
        v = _matmul(hn_g[g], w_a_in, b_cols=(col0 + 2 * A_WIDTH, A_WIDTH), out_dtype=BF16, tm=mm, tn=A_WIDTH,
                    name=f"a_v_g{g}")
        qk_g.append(qk.reshape(bsz * d, t // d, 2 * A_WIDTH))
        v_g.append(v.reshape(bsz * d, t // d, A_WIDTH))
        o, lse = _attn_a_fwd(qk_g[g], v_g[g], f"attn_a_fwd_g{g}")
        o_g.append(from_group(o.reshape(n, A_WIDTH), d))
        lse_g.append(rows_to_cols(lse, d))

    def merge_fn(o0, o1, o2, l0, l1, l2, z):
        lmax = jnp.maximum(jnp.maximum(l0, l1), l2)
        e0, e1, e2 = jnp.exp2(l0 - lmax), jnp.exp2(l1 - lmax), jnp.exp2(l2 - lmax)
        den = e0 + e1 + e2
        w0, w1, w2 = e0 / den, e1 / den, e2 / den
        parts = []
        for h in range(A_HEADS):
            hs = slice(h * A_HEAD_DIM, (h + 1) * A_HEAD_DIM)
            parts.append(w0[:, h:h + 1] * o0[:, hs] + w1[:, h:h + 1] * o1[:, hs] + w2[:, h:h + 1] * o2[:, hs])
        o = jnp.concatenate(parts, axis=1)
        return o * _silu(z), o, lmax + jnp.log2(den)

    y_a, o_a2, lse_a = _rowwise(merge_fn, [*o_g, *lse_g, z_a], [],
                                [(A_WIDTH, BF16), (A_WIDTH, F32), (A_HEADS, F32)], tm=tm, name="a_merge_gate")
    out_a = _matmul(y_a, w_a_out, out_dtype=F32, tm=mm, tn=D_MODEL, name="a_out")

    def mid_fn(xt, out, g_post, g_kv, g_b):
        h1 = xt + _rms(out, g_post)
        return h1, _rms(h1, g_kv), _rms(h1, g_b)

    h1, hn_kv, hn_b = _rowwise(mid_fn, [x2, out_a], [w["a_post_norm"], w["kv_norm"], w["b_pre_norm"]],
                               [(D_MODEL, F32), (D_MODEL, BF16), (D_MODEL, BF16)], tm=tm, name="a_post_norm")

    ckr = _matmul(hn_kv, w_down_p, out_dtype=F32, tm=mm, tn=3 * LANES, name="kv_down")

    def latent_fn(ck, c, sp, sm, g):
        return _rms(ck[:, :B_KV_LORA], g), _rope(ck[:, B_KV_LORA:], c, sp, sm)

    c_kv, k_rope = _rowwise(latent_fn, [ckr, *tb2], [w["kv_latent_norm"]], [(B_KV_LORA, BF16), (LANES, F32)],
                            tm=tm, name="kv_latent_norm")

    def kpad_epilogue(acc, o_ref, rows, kr_ref):
        kr = kr_ref[rows, :]
        for h in range(acc.shape[1] // LANES):
            hs = slice(h * LANES, (h + 1) * LANES)
            o_ref[rows, hs] = (acc[:, hs] + kr).astype(BF16)

    kpad = _matmul(c_kv, w_upk, out_dtype=BF16, tm=mm, tn=1024, name="kv_up_k", epilogue=kpad_epilogue,
                   epilogue_rows=EPILOGUE_ROWS,
                   extras=[(k_rope, (mm, LANES), lambda j, i, kk: (i, 0))])
    v_b = _matmul(c_kv, w_upv, out_dtype=BF16, tm=mm, tn=1024, name="kv_up_v")

    proj_b = _matmul(hn_b, w_b_in, out_dtype=F32, tm=mm, tn=w_b_in.shape[1], name="b_in")
    (c_q,) = _rowwise(lambda p, g: (_rms(p[:, :B_Q_LORA], g),), [proj_b], [w["b_q_norm"]], [(B_Q_LORA, BF16)],
                      tm=tm, name="b_q_norm")

    q_b = _matmul(c_q, w_q_p, out_dtype=BF16, tm=mm, tn=1024, name="b_q_up", epilogue=rope_epilogue,
                  epilogue_rows=EPILOGUE_ROWS,
                  extras=tab_extras(tb2, mm))
    q_b3, kpad3, v_b3 = q_b.reshape(bsz, t, B_KPAD), kpad.reshape(bsz, t, B_KPAD), v_b.reshape(bsz, t, -1)
    w_vt = jnp.pad(wu[:, :, B_NOPE:], ((0, 0), (0, 0), (0, LANES - B_VDIM))).reshape(B_KV_LORA, B_KPAD)
    vt1 = _value_heads_t(c_kv, w_vt, bsz, t)
    o_b, lse_b = _attn_b_fwd(q_b3, kpad3, vt1)
    o_b2 = o_b.reshape(n, -1)
    (y_b,) = _rowwise(lambda o, p: (o * _silu(p[:, B_Q_LORA:]),), [o_b2, proj_b], [], [(D_MODEL, BF16)],
                      tm=tm, name="b_gate_mul")
    out_b = _matmul(y_b, w_b_out, out_dtype=F32, tm=mm, tn=D_MODEL, name="b_out")

    def head_fn(h1t, out, tgt, g):
        e = h1t + _rms(out, g) - tgt
        loss_row = 0.5 * jnp.mean(e * e, axis=-1, keepdims=True)
        dh2 = e * (1.0 / D_MODEL)
        d_out, dg = _rms_bwd(out, g, dh2)
        return dh2, d_out, dg, jnp.broadcast_to(loss_row * (1.0 / LANES), (loss_row.shape[0], LANES))

    dh2, d_out_b, dg_b_post, loss_acc = _rowwise(
        head_fn, [h1, out_b, tgt2], [w["b_post_norm"]], [(D_MODEL, F32), (D_MODEL, BF16)], [D_MODEL, LANES],
        tm=tm, name="loss_head")

    dy_b = _matmul(d_out_b, w_b_out, tb=True, out_dtype=BF16, tm=mm, tn=D_MODEL, name="b_out_dx")
    gw_b_out = _matmul(y_b, d_out_b, ta=True, out_dtype=BF16, tm=mm, tn=D_MODEL, tk=2048, name="b_out_dw")

    def gate_b_bwd(dy, o, p):
        z = p[:, B_Q_LORA:]
        return dy * _silu(z), dy * o * _silu_grad(z)

    do_b, dz_b = _rowwise(gate_b_bwd, [dy_b, o_b2, proj_b], [], [(D_MODEL, BF16), (D_MODEL, F32)],
                          tm=tm, name="b_gate_bwd")
    do_b3 = do_b.reshape(bsz, t, -1)
    dq_b, dk_b, dv_b = _attn_b_bwd(q_b3, kpad3, v_b3, do_b3, o_b, lse_b, tabs_b)
    dq_b2, dk_b2, dv_b2 = dq_b.reshape(n, B_KPAD), dk_b.reshape(n, B_KPAD), dv_b.reshape(n, -1)

    dc_kv = _matmul(dk_b2, w_upk, tb=True, out_dtype=F32, tm=mm, tn=B_KV_LORA, name="kv_up_k_dx")
    dc_kv = _matmul(dv_b2, w_upv, tb=True, out_dtype=BF16, tm=mm, tn=B_KV_LORA, name="kv_up_v_dx",
                    epilogue=_add_epilogue, extras=[(dc_kv, (mm, B_KV_LORA), lambda j, i, kk: (i, j))])
    gw_upk = _matmul(c_kv, dk_b2, ta=True, out_dtype=BF16, tm=B_KV_LORA, tn=1024, tk=2048, name="kv_up_k_dw")
    gw_upv = _matmul(c_kv, dv_b2, ta=True, out_dtype=BF16, tm=B_KV_LORA, tn=1024, tk=2048, name="kv_up_v_dw")

    def latent_bwd(ck, dck, dk, c, sp, sm, g):
        d1, dg = _rms_bwd(ck[:, :B_KV_LORA], g, dck)
        ksum = dk[:, :LANES].astype(F32)
        for h in range(1, B_HEADS):
            ksum = ksum + dk[:, h * LANES:(h + 1) * LANES].astype(F32)
        lane = lax.broadcasted_iota(jnp.int32, ksum.shape, 1)
        ksum = jnp.where((lane >= B_NOPE) & (lane < B_QK_DIM), ksum, 0.0)
        return jnp.concatenate([d1, _rope_t(ksum, c, sp, sm)], axis=1), dg

    dckr, dg_latent = _rowwise(latent_bwd, [ckr, dc_kv, dk_b2, *tb2], [w["kv_latent_norm"]],
                               [(3 * LANES, BF16)], [B_KV_LORA], tm=tm, name="kv_latent_bwd")
    dhn_kv = _matmul(dckr, w_down_p, tb=True, out_dtype=BF16, tm=mm, tn=D_MODEL, name="kv_down_dx")
    gw_down_p = _matmul(hn_kv, dckr, ta=True, out_dtype=BF16, tm=mm, tn=3 * LANES, tk=2048, name="kv_down_dw")

    dc_q = _matmul(dq_b2, w_q_p, tb=True, out_dtype=BF16, tm=mm, tn=B_Q_LORA, name="b_q_up_dx")
    gw_q_p = _matmul(c_q, dq_b2, ta=True, out_dtype=BF16, tm=B_Q_LORA, tn=1024, tk=2048, name="b_q_up_dw")

    def q_norm_bwd(p, dcq, dz, g):
        d1, dg = _rms_bwd(p[:, :B_Q_LORA], g, dcq)
        return jnp.concatenate([d1, dz], axis=1), dg

    dproj_b, dg_q_norm = _rowwise(q_norm_bwd, [proj_b, dc_q, dz_b], [w["b_q_norm"]],
                                  [(w_b_in.shape[1], BF16)], [B_Q_LORA], tm=tm, name="b_q_norm_bwd")
    dhn_b = _matmul(dproj_b, w_b_in, tb=True, out_dtype=BF16, tm=mm, tn=D_MODEL, name="b_in_dx")
    gw_b_in = _matmul(hn_b, dproj_b, ta=True, out_dtype=BF16, tm=mm, tn=w_b_in.shape[1], tk=2048, name="b_in_dw")

    def mid_bwd(h1t, dh2t, dkv, db, g_kv, g_b, g_post, out):
        dxa, ra = _rms_bwd(h1t, g_kv, dkv)
        dxb, rb = _rms_bwd(h1t, g_b, db)
        dh1 = dh2t + dxa + dxb
        d_out, rp = _rms_bwd(out, g_post, dh1)
        return dh1, d_out, ra, rb, rp

    def mid_bwd_fn(h1t, dh2t, dkv, db, out, g_kv, g_b, g_post):
        return mid_bwd(h1t, dh2t, dkv, db, g_kv, g_b, g_post, out)

    dh1, d_out_a, dg_kv, dg_b_pre, dg_a_post = _rowwise(
        mid_bwd_fn, [h1, dh2, dhn_kv, dhn_b, out_a], [w["kv_norm"], w["b_pre_norm"], w["a_post_norm"]],
        [(D_MODEL, F32), (D_MODEL, BF16)], [D_MODEL] * 3, tm=tm, name="mid_bwd")

    dy_a = _matmul(d_out_a, w_a_out, tb=True, out_dtype=BF16, tm=mm, tn=A_WIDTH, name="a_out_dx")
    gw_a_out = _matmul(y_a, d_out_a, ta=True, out_dtype=BF16, tm=mm, tn=D_MODEL, tk=2048, name="a_out_dw")

    def gate_a_bwd(dy, o, z):
        do = dy * _silu(z)
        prod = do * o
        lane = lax.broadcasted_iota(jnp.int32, (prod.shape[0], A_HEADS), 1)
        dsum = jnp.zeros((prod.shape[0], A_HEADS), F32)
        for h in range(A_HEADS):
            col = jnp.sum(prod[:, h * A_HEAD_DIM:(h + 1) * A_HEAD_DIM], axis=1, keepdims=True)
            dsum = jnp.where(lane == h, col, dsum)
        return do, dy * o * _silu_grad(z), dsum

    do_a, dz_a, dsum_a = _rowwise(gate_a_bwd, [dy_a, o_a2, z_a], [],
                                  [(A_WIDTH, BF16), (A_WIDTH, BF16), (A_HEADS, F32)], tm=tm, name="a_gate_bwd")
    dhn_a = _matmul(dz_a, w_a_in, b_cols=(A_QKV, A_WIDTH), tb=True, out_dtype=F32, tm=mm, tn=D_MODEL,
                    name="a_gate_dx")
    gw_a_in, dhn_groups = None, []
    for g, d in enumerate(A_DILATIONS):
        s_n, ln = bsz * d, t // d
        dqkv = _attn_a_bwd(qk_g[g], v_g[g], to_group(do_a, d).reshape(s_n, ln, A_WIDTH), cols_to_rows(lse_a, d),
                           cols_to_rows(dsum_a, d), [a.reshape(s_n, ln, LANES) for a in tabs_g[g]],
                           f"attn_a_bwd_g{g}").reshape(n, 3 * A_WIDTH)
        w_cols = (g * 3 * A_WIDTH, 3 * A_WIDTH)
        if d == 1:
            dhn_a = _matmul(dqkv, w_a_in, b_cols=w_cols, tb=True, out_dtype=F32, tm=mm, tn=D_MODEL,
                            name=f"a_qkv_dx_g{g}", epilogue=_add_epilogue,
                            extras=[(dhn_a, (mm, D_MODEL), lambda j, i, kk: (i, j))])
        else:
            dhn_groups.append(from_group(_matmul(dqkv, w_a_in, b_cols=w_cols, tb=True, out_dtype=BF16, tm=mm,
                                                 tn=D_MODEL, name=f"a_qkv_dx_g{g}"), d))
        gw_a_in = _matmul(hn_g[g], dqkv, ta=True, out_dtype=BF16, tm=mm, tn=1024, tk=2048, name=f"a_qkv_dw_g{g}",
                          out_into=(gw_a_in, g * 3 * A_WIDTH, A_QKV + A_WIDTH))
    gw_a_in = _matmul(hn_a, dz_a, ta=True, out_dtype=BF16, tm=mm, tn=1024, tk=2048, name="a_gate_dw",
                      out_into=(gw_a_in, A_QKV, A_QKV + A_WIDTH))

    def first_bwd(xt, dhn, dhn_1, dhn_2, dh1t, g):
        dx, dg = _rms_bwd(xt, g, dhn + dhn_1 + dhn_2)
        return dh1t + dx, dg

    grad_x, dg_a_pre = _rowwise(first_bwd, [x2, dhn_a, *dhn_groups, dh1], [w["a_pre_norm"]], [(D_MODEL, F32)],
                                [D_MODEL], tm=tm, name="a_pre_norm_bwd")

    gw_down = jnp.concatenate([gw_down_p[:, :B_KV_LORA], gw_down_p[:, B_KV_LORA + B_NOPE:B_KV_LORA + B_QK_DIM]], axis=1)
    gw_up = jnp.concatenate([gw_upk.reshape(B_KV_LORA, B_HEADS, LANES)[:, :, :B_NOPE],
                             gw_upv.reshape(B_KV_LORA, B_HEADS, B_VDIM)], axis=2).reshape(B_KV_LORA, -1)
    gw_q_up = gw_q_p.reshape(B_Q_LORA, B_HEADS, LANES)[:, :, :B_QK_DIM].reshape(B_Q_LORA, -1)
    grads = {"a_w_in": gw_a_in, "a_w_out": gw_a_out, "kv_w_down": gw_down, "kv_w_up": gw_up,
             "b_w_in": gw_b_in, "b_w_q_up": gw_q_up, "b_w_out": gw_b_out}
    gains = {"a_pre_norm": dg_a_pre, "a_post_norm": dg_a_post, "kv_norm": dg_kv, "kv_latent_norm": dg_latent,
             "b_pre_norm": dg_b_pre, "b_q_norm": dg_q_norm, "b_post_norm": dg_b_post}
    gains = {k: jnp.sum(a, axis=0) for k, a in gains.items()}
    return jnp.sum(loss_acc), grad_x.reshape(bsz, t, D_MODEL), grads, gains


WEIGHT_ORDER = ("a_pre_norm", "a_w_in", "a_w_out", "a_post_norm", "kv_norm", "kv_w_down", "kv_latent_norm",
                "kv_w_up", "b_pre_norm", "b_w_in", "b_q_norm", "b_w_q_up", "b_w_out", "b_post_norm")
MATRICES = (("a_w_in", 1024, 10240, 1), ("a_w_out", 1024, 1024, 0), ("kv_w_down", 1024, 288, 0),
            ("kv_w_up", 256, 2048, 1), ("b_w_in", 1024, 1408, 1), ("b_w_q_up", 384, 1536, 1),
            ("b_w_out", 1024, 1024, 0))
SHARDED_GAINS = ("a_pre_norm", "a_post_norm")
GAIN_WIDTHS = (("a_pre_norm", 1024), ("a_post_norm", 1024), ("kv_norm", 1024), ("kv_latent_norm", 256),
               ("b_pre_norm", 1024), ("b_q_norm", 384), ("b_post_norm", 1024))
GAIN_ROWS = 48


def _shard_rows(rows, cols):
    return rows * cols // (N_DEV * LANES)


def _whole_from_blocks(blocks, rows, cols, axis):
    if axis == 1:
        return blocks.reshape(N_DEV, rows, cols // N_DEV).transpose(1, 0, 2).reshape(rows, cols)
    return blocks.reshape(rows, cols)


def _blocks_from_whole(whole, rows, cols, axis):
    if axis == 1:
        whole = whole.reshape(rows, N_DEV, cols // N_DEV).transpose(1, 0, 2)
    return whole.reshape(N_DEV, -1, LANES)


def kernel(x, positions, a_pre_norm, a_w_in, a_w_out, a_post_norm, kv_norm, kv_w_down, kv_latent_norm, kv_w_up, b_pre_norm, b_w_in, b_q_norm, b_w_q_up, b_w_out, b_post_norm, loss_target, m_a_pre_norm, m_a_w_in, m_a_w_out, m_a_post_norm, m_kv_norm, m_kv_w_down, m_kv_latent_norm, m_kv_w_up, m_b_pre_norm, m_b_w_in, m_b_q_norm, m_b_w_q_up, m_b_w_out, m_b_post_norm, v_a_pre_norm, v_a_w_in, v_a_w_out, v_a_post_norm, v_kv_norm, v_kv_w_down, v_kv_latent_norm, v_kv_w_up, v_b_pre_norm, v_b_w_in, v_b_q_norm, v_b_w_q_up, v_b_w_out, v_b_post_norm):
    weights = dict(a_pre_norm=a_pre_norm, a_w_in=a_w_in, a_w_out=a_w_out, a_post_norm=a_post_norm, kv_norm=kv_norm,
                   kv_w_down=kv_w_down, kv_latent_norm=kv_latent_norm, kv_w_up=kv_w_up, b_pre_norm=b_pre_norm,
                   b_w_in=b_w_in, b_q_norm=b_q_norm, b_w_q_up=b_w_q_up, b_w_out=b_w_out, b_post_norm=b_post_norm)
    m_in = dict(a_pre_norm=m_a_pre_norm, a_w_in=m_a_w_in, a_w_out=m_a_w_out, a_post_norm=m_a_post_norm,
                kv_norm=m_kv_norm, kv_w_down=m_kv_w_down, kv_latent_norm=m_kv_latent_norm, kv_w_up=m_kv_w_up,
                b_pre_norm=m_b_pre_norm, b_w_in=m_b_w_in, b_q_norm=m_b_q_norm, b_w_q_up=m_b_w_q_up,
                b_w_out=m_b_w_out, b_post_norm=m_b_post_norm)
    v_in = dict(a_pre_norm=v_a_pre_norm, a_w_in=v_a_w_in, a_w_out=v_a_w_out, a_post_norm=v_a_post_norm,
                kv_norm=v_kv_norm, kv_w_down=v_kv_w_down, kv_latent_norm=v_kv_latent_norm, kv_w_up=v_kv_w_up,
                b_pre_norm=v_b_pre_norm, b_w_in=v_b_w_in, b_q_norm=v_b_q_norm, b_w_q_up=v_b_w_q_up,
                b_w_out=v_b_w_out, b_post_norm=v_b_post_norm)
    me = 4 * lax.axis_index("x") + 2 * lax.axis_index("y") + lax.axis_index("c")

    wide = MATRICES[0][0]
    flat = jnp.concatenate([weights[name].astype(BF16).reshape(-1, LANES) for name, _, _, _ in MATRICES[1:]], axis=0)
    gathered, w_wide = _all_gather_weights(flat, weights[wide][0].astype(BF16), "gather_weights")
    whole = {wide: w_wide}
    off = 0
    for name, rows, cols, axis in MATRICES[1:]:
        nr = _shard_rows(rows, cols)
        whole[name] = _whole_from_blocks(gathered[:, off:off + nr], rows, cols, axis)
        off += nr
    gain_shard = jnp.concatenate([weights[name].reshape(1, LANES) for name in SHARDED_GAINS]
                                 + [jnp.zeros((8 - len(SHARDED_GAINS), LANES), F32)], axis=0)
    gain_blocks = _all_gather(gain_shard, "gather_gains")
    for i, name in enumerate(SHARDED_GAINS):
        whole[name] = gain_blocks[:, i, :].reshape(1, D_MODEL)
    for name in ("kv_norm", "kv_latent_norm", "b_pre_norm", "b_q_norm", "b_post_norm"):
        whole[name] = weights[name].reshape(1, -1)

    loss_part, grad_x, grads, gains = _local_step(x, positions, loss_target, whole)

    blocks = jnp.concatenate([_blocks_from_whole(grads[name], rows, cols, axis).astype(BF16)
                              for name, rows, cols, axis in MATRICES], axis=1)
    blocks = blocks.reshape(N_CHIPS, 2, blocks.shape[1], LANES)
    got = _exchange_sibling(blocks, "scatter_grads_core")
    core = lax.axis_index("c").astype(jnp.int32).reshape(1)
    landed = _exchange_chips(_add_pairs(blocks, got, core, tr=2512, name="add_core_grads"), "scatter_grads_chip")
    summed = _sum_slots(landed, tr=2512, name="sum_grads")
    grad_out = {}
    off = 0
    for name, rows, cols, axis in MATRICES:
        nr = _shard_rows(rows, cols)
        grad_out[name] = summed[off:off + nr].reshape(weights[name].shape)
        off += nr

    vec = jnp.concatenate([gains[name] for name, _ in GAIN_WIDTHS] + [jnp.full((LANES,), loss_part, F32)])
    vec = jnp.pad(vec, (0, GAIN_ROWS * LANES - vec.shape[0])).reshape(GAIN_ROWS, LANES)
    total = _sum_slots(_all_gather(vec, "gather_gain_grads"), tr=GAIN_ROWS, name="sum_gain_grads").reshape(-1)
    off = 0
    for name, width in GAIN_WIDTHS:
        g = total[off:off + width]
        if name in SHARDED_GAINS:
            g = lax.dynamic_slice(g, (me * LANES,), (LANES,))
        grad_out[name] = g.reshape(weights[name].shape)
        off += width
    loss = total[off]

    deltas, new_m, new_v = {}, {}, {}
    big = "a_w_in"
    deltas[big], new_m[big], new_v[big] = _adamw(weights[big], grad_out[big], m_in[big], v_in[big], "adamw_" + big)
    small = [name for name in WEIGHT_ORDER if name != big]
    res = _adamw_small(*[[d[name] for name in small] for d in (weights, grad_out, m_in, v_in)], "adamw_small")
    for out, vals in zip((deltas, new_m, new_v), res):
        out.update(zip(small, vals))
    return (loss, grad_x, *[grad_out[k] for k in WEIGHT_ORDER], *[deltas[k] for k in WEIGHT_ORDER],
            *[new_m[k] for k in WEIGHT_ORDER], *[new_v[k] for k in WEIGHT_ORDER])
```

```python
import jax
import jax.numpy as jnp
from jax import lax
from jax.experimental import pallas as pl
from jax.experimental.pallas import tpu as pltpu

F32 = jnp.float32
BF16 = jnp.bfloat16

N_DEV = 8
D_MODEL = 1024
NORM_EPS = 1e-6
A_GROUPS = 3
A_DILATIONS = (1, 4, 16)
A_HEADS = 8
A_HEAD_DIM = 128
A_WIDTH = 1024
A_ROT_DIM = 32
A_ROPE_THETA = 500000.0
A_QKV = A_GROUPS * 3 * A_WIDTH
B_HEADS = 16
B_NOPE = 64
B_ROPE = 32
B_QK_DIM = 96
B_VDIM = 64
B_Q_LORA = 384
B_KV_LORA = 256
B_ROPE_THETA = 10000.0
B_KPAD = B_HEADS * 128
ADAM_LR = 0.001
ADAM_B1 = 0.9
ADAM_B2 = 0.999
ADAM_EPS = 1e-08
ADAM_WD = 0.01
ADAM_STEP = 10

LANES = 128
BAND = 128
EPILOGUE_ROWS = 128
NEG = -1e30
VMEM_LIMIT = 56 * 1024 * 1024
MESH = pl.DeviceIdType.MESH


def _cparams(sem):
    return pltpu.CompilerParams(dimension_semantics=sem, vmem_limit_bytes=VMEM_LIMIT)


def _rowwise(fn, rows, bcast, outs, accs=(), *, tm, name):
    n = rows[0].shape[0]
    nr, nb, no = len(rows), len(bcast), len(outs)

    def body(*refs):
        res = fn(*[r[...] for r in refs[:nr + nb]])
        out_refs = refs[nr + nb:nr + nb + no]
        acc_refs = refs[nr + nb + no:]
        for r, v in zip(out_refs, res[:no]):
            r[...] = v.astype(r.dtype)
        if acc_refs:
            @pl.when(pl.program_id(0) == 0)
            def _():
                for r in acc_refs:
                    r[...] = jnp.zeros_like(r)
            for r, v in zip(acc_refs, res[no:]):
                r[...] += v.reshape(tm // 8, 8, v.shape[-1]).sum(axis=0)

    in_specs = [pl.BlockSpec((tm, a.shape[1]), lambda i: (i, 0)) for a in rows]
    in_specs += [pl.BlockSpec(a.shape, lambda i: (0, 0)) for a in bcast]
    out_specs = [pl.BlockSpec((tm, c), lambda i: (i, 0)) for c, _ in outs]
    out_specs += [pl.BlockSpec((8, c), lambda i: (0, 0)) for c in accs]
    out_shape = [jax.ShapeDtypeStruct((n, c), dt) for c, dt in outs]
    out_shape += [jax.ShapeDtypeStruct((8, c), F32) for c in accs]
    return pl.pallas_call(
        body, name=name, grid=(n // tm,), in_specs=in_specs, out_specs=out_specs, out_shape=out_shape,
        compiler_params=_cparams(("arbitrary",)))(*rows, *bcast)


def _matmul(a, b, *, out_dtype, tm, tn, tk=None, name, epilogue=None, extras=(), ta=False, tb=False,
            epilogue_rows=None, b_cols=None, out_into=None):
    epilogue_rows = epilogue_rows or tm
    k, m = a.shape[::-1] if not ta else a.shape
    col0, width = b_cols or (0, b.shape[1])
    n = b.shape[0] if tb else width
    tk = tk or k
    nk = k // tk
    b_off = col0 // (tk if tb else tn)
    assert col0 % (tk if tb else tn) == 0 and (k == width if tb else True)
    ne = len(extras)
    dot = _dot_tn if ta else (_dot_nt if tb else _dot)
    assert epilogue is None or (nk == 1 and not ta)

    def body(*refs):
        a_ref, b_ref = refs[:2]
        ex = refs[2:2 + ne]
        o_ref = refs[2 + ne + (1 if out_into is not None and out_into[0] is not None else 0)]
        if epilogue is not None:
            b_tile = b_ref[...].astype(BF16)
            for r0 in range(0, tm, epilogue_rows):
                rows = slice(r0, r0 + epilogue_rows)
                epilogue(dot(a_ref[rows, :].astype(BF16), b_tile), o_ref, rows, *ex)
            return
        part = dot(a_ref[...].astype(BF16), b_ref[...].astype(BF16))
        if nk == 1:
            o_ref[...] = part.astype(o_ref.dtype)
        else:
            acc_ref = refs[-1]
            kk = pl.program_id(2)

            @pl.when(kk == 0)
            def _():
                acc_ref[...] = part

            @pl.when(kk > 0)
            def _():
                acc_ref[...] += part

            @pl.when(kk == nk - 1)
            def _():
                o_ref[...] = acc_ref[...].astype(o_ref.dtype)

    a_spec = pl.BlockSpec((tk, tm), lambda j, i, kk: (kk, i)) if ta else pl.BlockSpec((tm, tk), lambda j, i, kk: (i, kk))
    b_spec = (pl.BlockSpec((tn, tk), lambda j, i, kk: (j, kk + b_off)) if tb
              else pl.BlockSpec((tk, tn), lambda j, i, kk: (kk, j + b_off)))
    in_specs = [a_spec, b_spec] + [pl.BlockSpec(bs, im) for _, bs, im in extras]
    operands = [a, b] + [e[0] for e in extras]
    aliases = {}
    if out_into is not None:
        prev, out0, n_total = out_into
        o_off = out0 // tn
        assert out0 % tn == 0
        if prev is not None:
            in_specs.append(ANY)
            operands.append(prev)
            aliases = {len(operands) - 1: 0}
    else:
        o_off, n_total = 0, n
    return pl.pallas_call(
        body, name=name, grid=(n // tn, m // tm, nk), in_specs=in_specs,
        out_specs=pl.BlockSpec((tm, tn), lambda j, i, kk: (i, j + o_off)),
        out_shape=jax.ShapeDtypeStruct((m, n_total), out_dtype), input_output_aliases=aliases,
        scratch_shapes=[pltpu.VMEM((tm, tn), F32)] if nk > 1 else [],
        compiler_params=_cparams(("parallel", "parallel", "arbitrary")))(*operands)


def _add_epilogue(acc, o_ref, rows, prev_ref):
    o_ref[rows, :] = (acc + prev_ref[rows, :]).astype(o_ref.dtype)


def _rope(x, c, sp, sm):
    return x * c + pltpu.roll(x, 16, 1) * sp + pltpu.roll(x, LANES - 16, 1) * sm


def _rope_t(dy, c, sp, sm):
    return dy * c + pltpu.roll(dy * sp, LANES - 16, 1) + pltpu.roll(dy * sm, 16, 1)


def _rope_tables(positions, theta, rot_dim, lane0, name):
    n = positions.shape[0]
    half = rot_dim // 2
    inv_freq = 1.0 / (theta ** (jnp.arange(half, dtype=F32) * (2.0 / rot_dim)))
    freq = jnp.concatenate([jnp.zeros((lane0,), F32), inv_freq, inv_freq,
                            jnp.zeros((LANES - lane0 - rot_dim,), F32)]).reshape(1, LANES)

    def fn(p, f):
        ang = p * f
        cos, sin = jnp.cos(ang), jnp.sin(ang)
        lane = lax.broadcasted_iota(jnp.int32, ang.shape, 1) - lane0
        first, second = (lane >= 0) & (lane < half), (lane >= half) & (lane < rot_dim)
        return jnp.where(first | second, cos, 1.0), jnp.where(second, sin, 0.0), jnp.where(first, -sin, 0.0)

    return _rowwise(fn, [positions.astype(F32).reshape(n, 1)], [freq], [(LANES, F32)] * 3, tm=512, name=name)


def _value_heads_t(c_kv, w_vt, bsz, t):
    n, k = c_kv.shape
    tm, tn = 512, 1024

    def body(a_ref, b_ref, o_ref):
        acc = _dot(a_ref[...], b_ref[...])
        lane = lax.broadcasted_iota(jnp.int32, (1, tn), 1)
        acc = acc + jnp.where(lax.rem(lane, LANES) == B_VDIM, 1.0, 0.0)
        o_ref[...] = acc.T.astype(BF16)

    per_seq = t // tm
    return pl.pallas_call(
        body, name="kv_up_v_t", grid=(w_vt.shape[1] // tn, n // tm),
        in_specs=[pl.BlockSpec((tm, k), lambda j, i: (i, 0)), pl.BlockSpec((k, tn), lambda j, i: (0, j))],
        out_specs=pl.BlockSpec((None, tn, tm), lambda j, i: (i // per_seq, j, lax.rem(i, per_seq))),
        out_shape=jax.ShapeDtypeStruct((bsz, w_vt.shape[1], t), BF16),
        compiler_params=_cparams(("parallel", "parallel")))(c_kv, w_vt)


def _rms(x, g):
    xf = x.astype(F32)
    return xf * lax.rsqrt(jnp.mean(xf * xf, axis=-1, keepdims=True) + NORM_EPS) * g


def _rms_bwd(x, g, dy):
    xf = x.astype(F32)
    rstd = lax.rsqrt(jnp.mean(xf * xf, axis=-1, keepdims=True) + NORM_EPS)
    xhat = xf * rstd
    dxhat = dy * g
    dx = rstd * (dxhat - xhat * jnp.mean(dxhat * xhat, axis=-1, keepdims=True))
    return dx, dy * xhat


def _silu(z):
    return z * jax.nn.sigmoid(z)


def _silu_grad(z):
    s = jax.nn.sigmoid(z)
    return s * (1.0 + z * (1.0 - s))


def _dot_nt(a, b):
    return lax.dot_general(a, b, (((1,), (1,)), ((), ())), preferred_element_type=F32)


def _dot_tn(a, b):
    return lax.dot_general(a, b, (((0,), (0,)), ((), ())), preferred_element_type=F32)


def _dot(a, b):
    return jnp.dot(a, b, preferred_element_type=F32)


A_SCALE = A_HEAD_DIM ** -0.5
LOG2E = 1.4426950408889634
A_C2 = A_SCALE * LOG2E
A_HEAD_GROUP = 4
A_FWD_HEAD_GROUP = 8
A_BLOCKS_PER_STEP = 2
A_SEQS_PER_STEP = 2


def _a_steps(nb):
    return (min(nb, A_BLOCKS_PER_STEP), 1) if nb > 1 else (1, A_SEQS_PER_STEP)


def _attn_a_fwd(qk, v, name):
    s_n, ln, _ = qk.shape
    nb = ln // BAND
    lb, sb = _a_steps(nb)
    single = nb == 1
    rows = lb * BAND
    n_keys = BAND if single else 2 * BAND

    def body(q_ref, kc_ref, kp_ref, vc_ref, vp_ref, o_ref, lse_ref):
        kpos = lax.broadcasted_iota(jnp.int32, (n_keys, BAND), 0)
        qpos = lax.broadcasted_iota(jnp.int32, (n_keys, BAND), 1) + (n_keys - BAND)
        band = (kpos <= qpos) & (kpos >= qpos - BAND)
        at_start = band & (kpos >= jnp.where(pl.program_id(1) > 0, 0, n_keys - BAND))
        for s in range(sb):
            for u in range(lb):
                r = slice(u * BAND, (u + 1) * BAND)
                mask = at_start if u == 0 else band

                def keys(prev_ref, cur_ref, hs):
                    cur = cur_ref[s, r, hs]
                    if single:
                        return cur
                    prev = prev_ref[s, :, hs] if u == 0 else cur_ref[s, (u - 1) * BAND:u * BAND, hs]
                    return jnp.concatenate([prev, cur], axis=0)

                stats = []
                for h0 in range(0, A_HEADS, A_FWD_HEAD_GROUP):
                    hss = [slice(h * A_HEAD_DIM, (h + 1) * A_HEAD_DIM) for h in range(h0, h0 + A_FWD_HEAD_GROUP)]
                    sts = [_dot_nt(keys(kp_ref, kc_ref, hs), q_ref[s, r, hs]) for hs in hss]
                    ps, ls = [], []
                    for st in sts:
                        st = jnp.where(mask, st * A_C2, NEG)
                        m = jnp.max(st, axis=0, keepdims=True)
                        p = jnp.exp2(st - m)
                        l_row = jnp.sum(p, axis=0, keepdims=True)
                        ps.append(p.astype(BF16))
                        ls.append(l_row)
                        stats.append(m + jnp.log2(l_row))
                    ots = [_dot_tn(keys(vp_ref, vc_ref, hs), p) for hs, p in zip(hss, ps)]
                    for hs, o_t, l_row in zip(hss, ots, ls):
                        o_ref[s, r, hs] = (o_t / l_row).T.astype(BF16)
                lse_ref[s, :, r] = jnp.concatenate(stats, axis=0)

    def cur(c):
        return pl.BlockSpec((sb, rows, A_WIDTH), lambda s, l: (s, l, c))

    def prev(c):
        return pl.BlockSpec((sb, BAND, A_WIDTH), lambda s, l: (s, jnp.maximum(l * lb - 1, 0), c))

    return pl.pallas_call(
        body, name=name, grid=(s_n // sb, nb // lb),
        in_specs=[cur(0), cur(1), prev(1), cur(0), prev(0)],
        out_specs=[cur(0), pl.BlockSpec((sb, A_HEADS, rows), lambda s, l: (s, 0, l))],
        out_shape=[jax.ShapeDtypeStruct((s_n, ln, A_WIDTH), BF16), jax.ShapeDtypeStruct((s_n, A_HEADS, ln), F32)],
        compiler_params=_cparams(("parallel", "arbitrary")))(qk, qk, qk, v, v)


def _attn_a_bwd(qk, v, do, lse2, dsum, tabs, name):
    s_n, ln, _ = qk.shape
    nb = ln // BAND
    lb, sb = _a_steps(nb)
    single = nb == 1
    rows = lb * BAND
    n_steps = nb // lb

    def body(q_ref, qn_ref, kc_ref, kp_ref, vc_ref, vp_ref, do_ref, don_ref, lse_ref, lsen_ref, ds_ref, dsn_ref,
             c_ref, sp_ref, sm_ref, out_ref):
        l_idx = pl.program_id(1)
        kpos = lax.broadcasted_iota(jnp.int32, (2 * BAND, BAND), 0)
        qpos = lax.broadcasted_iota(jnp.int32, (2 * BAND, BAND), 1) + BAND
        band_q = (kpos <= qpos) & (kpos >= qpos - BAND)
        start_q = band_q & (kpos >= jnp.where(l_idx > 0, 0, BAND))
        kpos2 = lax.broadcasted_iota(jnp.int32, (BAND, 2 * BAND), 0)
        qpos2 = lax.broadcasted_iota(jnp.int32, (BAND, 2 * BAND), 1)
        band_k = (kpos2 <= qpos2) & (kpos2 >= qpos2 - BAND)
        end_k = band_k & (qpos2 < jnp.where(l_idx < n_steps - 1, 2 * BAND, BAND))
        causal = kpos[:BAND] <= qpos[:BAND] - BAND
        for s in range(sb):
            for u in range(lb):
                r = slice(u * BAND, (u + 1) * BAND)
                c, sp, sm = c_ref[s, r, :], sp_ref[s, r, :], sm_ref[s, r, :]
                lse_q, ds_q = lse_ref[s, :, r], ds_ref[s, :, r]

                def store(hl, hss, dqs, dks, dvs):
                    for i, h in enumerate(hl):
                        out_ref[s, r, hss[i]] = _rope_t(dqs[i], c, sp, sm).astype(BF16)
                        out_ref[s, r, A_WIDTH + h * A_HEAD_DIM:A_WIDTH + (h + 1) * A_HEAD_DIM] = \
                            _rope_t(dks[i], c, sp, sm).astype(BF16)
                        out_ref[s, r, 2 * A_WIDTH + h * A_HEAD_DIM:2 * A_WIDTH + (h + 1) * A_HEAD_DIM] = \
                            dvs[i].astype(BF16)

                if single:
                    for h0 in range(0, A_HEADS, A_HEAD_GROUP):
                        hl = list(range(h0, h0 + A_HEAD_GROUP))
                        hss = [slice(h * A_HEAD_DIM, (h + 1) * A_HEAD_DIM) for h in hl]
                        sts = [_dot_nt(kc_ref[s, r, hs], q_ref[s, r, hs]) for hs in hss]
                        dpts = [_dot_nt(vc_ref[s, r, hs], do_ref[s, r, hs]) for hs in hss]
                        ps, dsts = [], []
                        for i, h in enumerate(hl):
                            p = jnp.exp2(jnp.where(causal, sts[i] * A_C2, NEG) - lse_q[h:h + 1])
                            dsts.append((p * (dpts[i] - ds_q[h:h + 1]) * A_SCALE).astype(BF16))
                            ps.append(p.astype(BF16))
                        store(hl, hss, [_dot_tn(dst, kc_ref[s, r, hs]) for dst, hs in zip(dsts, hss)],
                              [_dot(dst, q_ref[s, r, hs]) for dst, hs in zip(dsts, hss)],
                              [_dot(p, do_ref[s, r, hs]) for p, hs in zip(ps, hss)])
                    continue

                r_prev = slice((u - 1) * BAND, u * BAND)
                r_next = slice((u + 1) * BAND, (u + 2) * BAND)

                def before(prev_ref, cur_ref, hs):
                    return prev_ref[s, :, hs] if u == 0 else cur_ref[s, r_prev, hs]

                def after(next_ref, cur_ref, hs):
                    return next_ref[s, :, hs] if u == lb - 1 else cur_ref[s, r_next, hs]

                mask_q = start_q if u == 0 else band_q
                mask_k = end_k if u == lb - 1 else band_k
                lse_n = lsen_ref[s] if u == lb - 1 else lse_ref[s, :, r_next]
                ds_n = dsn_ref[s] if u == lb - 1 else ds_ref[s, :, r_next]
                lse_k = jnp.concatenate([lse_q, lse_n], axis=1)
                ds_k = jnp.concatenate([ds_q, ds_n], axis=1)
                for h0 in range(0, A_HEADS, A_HEAD_GROUP):
                    hl = list(range(h0, h0 + A_HEAD_GROUP))
                    hss = [slice(h * A_HEAD_DIM, (h + 1) * A_HEAD_DIM) for h in hl]
                    k2s = [jnp.concatenate([before(kp_ref, kc_ref, hs), kc_ref[s, r, hs]], axis=0) for hs in hss]
                    v2s = [jnp.concatenate([before(vp_ref, vc_ref, hs), vc_ref[s, r, hs]], axis=0) for hs in hss]
                    q2s = [jnp.concatenate([q_ref[s, r, hs], after(qn_ref, q_ref, hs)], axis=0) for hs in hss]
                    do2s = [jnp.concatenate([do_ref[s, r, hs], after(don_ref, do_ref, hs)], axis=0) for hs in hss]
                    sts = [_dot_nt(k2, q_ref[s, r, hs]) for k2, hs in zip(k2s, hss)]
                    dpts = [_dot_nt(v2, do_ref[s, r, hs]) for v2, hs in zip(v2s, hss)]
                    st2s = [_dot_nt(kc_ref[s, r, hs], q2) for q2, hs in zip(q2s, hss)]
                    dpt2s = [_dot_nt(vc_ref[s, r, hs], do2) for do2, hs in zip(do2s, hss)]
                    dsts, dst2s, p2s = [], [], []
                    for i, h in enumerate(hl):
                        p = jnp.exp2(jnp.where(mask_q, sts[i] * A_C2, NEG) - lse_q[h:h + 1])
                        dsts.append((p * (dpts[i] - ds_q[h:h + 1]) * A_SCALE).astype(BF16))
                        p2 = jnp.exp2(jnp.where(mask_k, st2s[i] * A_C2, NEG) - lse_k[h:h + 1])
                        dst2s.append((p2 * (dpt2s[i] - ds_k[h:h + 1]) * A_SCALE).astype(BF16))
                        p2s.append(p2.astype(BF16))
                    store(hl, hss, [_dot_tn(dsts[i], k2s[i]) for i in range(A_HEAD_GROUP)],
                          [_dot(dst2s[i], q2s[i]) for i in range(A_HEAD_GROUP)],
                          [_dot(p2s[i], do2s[i]) for i in range(A_HEAD_GROUP)])

    def cur(c, width=A_WIDTH):
        return pl.BlockSpec((sb, rows, width), lambda s, l: (s, l, c))

    def prev(c):
        return pl.BlockSpec((sb, BAND, A_WIDTH), lambda s, l: (s, jnp.maximum(l * lb - 1, 0), c))

    def nxt(c):
        return pl.BlockSpec((sb, BAND, A_WIDTH), lambda s, l: (s, jnp.minimum((l + 1) * lb, nb - 1), c))

    stat = pl.BlockSpec((sb, A_HEADS, rows), lambda s, l: (s, 0, l))
    stat_next = pl.BlockSpec((sb, A_HEADS, BAND), lambda s, l: (s, 0, jnp.minimum((l + 1) * lb, nb - 1)))
    tspec = cur(0, LANES)
    in_specs = [cur(0), nxt(0), cur(1), prev(1), cur(0), prev(0), cur(0), nxt(0),
                stat, stat_next, stat, stat_next, tspec, tspec, tspec]
    return pl.pallas_call(
        body, name=name, grid=(s_n // sb, n_steps), in_specs=in_specs,
        out_specs=cur(0, 3 * A_WIDTH),
        out_shape=jax.ShapeDtypeStruct((s_n, ln, 3 * A_WIDTH), BF16),
        compiler_params=_cparams(("parallel", "arbitrary")))(
            qk, qk, qk, qk, v, v, do, do, lse2, lse2, dsum, dsum, *tabs)


B_TQ = 256
B_FWD_HEADS = 4
B_BWD_PAIRS = 2
B_SCALE = B_QK_DIM ** -0.5
B_C2 = B_SCALE * LOG2E


def _key_le_query(kb, qb, tk, tq):
    kpos = kb * tk + lax.broadcasted_iota(jnp.int32, (tk, tq), 0)
    qpos = qb * tq + lax.broadcasted_iota(jnp.int32, (tk, tq), 1)
    return kpos <= qpos


def _attn_b_fwd(q, kpad, vt1):
    bsz, t, _ = q.shape
    tq = tk = B_TQ
    nq = t // tq
    nh = B_FWD_HEADS

    def body(q_ref, k_ref, vt_ref, o_ref, lse_ref):
        qblk = pl.program_id(2)
        qs = [q_ref[:, hh * LANES:(hh + 1) * LANES] for hh in range(nh)]

        def scores(kb):
            start = pl.multiple_of(kb * tk, tk)
            return [_dot_nt(k_ref[pl.ds(start, tk), hh * LANES:(hh + 1) * LANES], qs[hh]) for hh in range(nh)]

        def pv(kb, ps):
            start = pl.multiple_of(kb * tk, tk)
            return [_dot(vt_ref[hh * LANES:(hh + 1) * LANES, pl.ds(start, tk)], ps[hh]) for hh in range(nh)]

        def softmax(ss, ms, accs, kb, masked):
            out_m, out_acc, out_p = [], [], []
            for hh in range(nh):
                s = ss[hh] * B_C2
                if masked:
                    s = jnp.where(_key_le_query(kb, qblk, tk, tq), s, NEG)
                m_new = jnp.maximum(ms[hh], jnp.max(s, axis=0, keepdims=True))
                out_acc.append(jnp.exp2(ms[hh] - m_new) * accs[hh])
                out_p.append(jnp.exp2(s - m_new).astype(BF16))
                out_m.append(m_new)
            return out_m, out_acc, out_p

        def step(kb, carry):
            ss, ps, ms, accs = carry
            pvs = pv(jnp.maximum(kb - 1, 0), ps)
            ss_next = scores(kb + 1)
            accs = [accs[hh] + pvs[hh] for hh in range(nh)]
            ms, accs, ps = softmax(ss, ms, accs, kb, False)
            return (ss_next, ps, ms, accs)

        init = (scores(0), [jnp.zeros((tk, tq), BF16)] * nh, [jnp.full((1, tq), NEG, F32)] * nh,
                [jnp.zeros((LANES, tq), F32)] * nh)
        ss, ps, ms, accs = lax.fori_loop(0, qblk, step, init)
        pvs = pv(jnp.maximum(qblk - 1, 0), ps)
        accs = [accs[hh] + pvs[hh] for hh in range(nh)]
        ms, accs, ps = softmax(ss, ms, accs, qblk, True)
        pvs = pv(qblk, ps)
        accs = [accs[hh] + pvs[hh] for hh in range(nh)]
        ls = [accs[hh][B_VDIM:B_VDIM + 1] for hh in range(nh)]
        o_t = jnp.concatenate([accs[hh][:B_VDIM] / ls[hh] for hh in range(nh)], axis=0)
        o_ref[...] = o_t.T
        for pair in range(nh // 2):
            lse_ref[pair] = jnp.concatenate([ms[2 * pair + hh] + jnp.log2(ls[2 * pair + hh]) for hh in range(2)]
                                            + [jnp.zeros((6, tq), F32)], axis=0)

    return pl.pallas_call(
        body, name="attn_b_fwd", grid=(bsz, B_HEADS // nh, nq),
        in_specs=[pl.BlockSpec((None, tq, nh * LANES), lambda b, j, i: (b, i, j)),
                  pl.BlockSpec((None, t, nh * LANES), lambda b, j, i: (b, 0, j)),
                  pl.BlockSpec((None, nh * LANES, t), lambda b, j, i: (b, j, 0))],
        out_specs=[pl.BlockSpec((None, tq, nh * B_VDIM), lambda b, j, i: (b, i, j)),
                   pl.BlockSpec((None, nh // 2, 8, tq), lambda b, j, i: (b, j, 0, i))],
        out_shape=[jax.ShapeDtypeStruct((bsz, t, B_HEADS * B_VDIM), F32),
                   jax.ShapeDtypeStruct((bsz, B_HEADS // 2, 8, t), F32)],
        compiler_params=_cparams(("parallel", "parallel", "arbitrary")))(q, kpad, vt1)


def _attn_b_bwd(q, kpad, v, do, o, lse2, tabs):
    bsz, t, _ = q.shape
    tq = tk = B_TQ
    nq = t // tq
    n_pairs = B_BWD_PAIRS
    n_heads = 2 * n_pairs

    def body(q_ref, k_ref, v_ref, do_ref, o_ref, lse_ref, c_ref, sp_ref, sm_ref, dq_ref, dk_ref, dv_ref,
             dqt_scr, dsum_scr):
        lane = lax.broadcasted_iota(jnp.int32, (tk, LANES), 1)
        sel_lane = lax.broadcasted_iota(jnp.int32, (8, LANES), 1)
        sel_row = lax.broadcasted_iota(jnp.int32, (8, LANES), 0)
        sel = jnp.where((sel_lane < B_VDIM) == (sel_row == 0), 1.0, 0.0)
        sel = jnp.where(sel_row < 2, sel, 0.0).astype(BF16)

        def rows(blk):
            return pl.ds(pl.multiple_of(blk * tq, tq), tq)

        def dsum_step(qb, carry):
            for pp in range(n_pairs):
                pls = slice(pp * LANES, (pp + 1) * LANES)
                prod = do_ref[rows(qb), pls].astype(F32) * o_ref[rows(qb), pls]
                hi = prod.astype(BF16)
                lo = (prod - hi.astype(F32)).astype(BF16)
                dsum_scr[pp, :, rows(qb)] = _dot_nt(sel, hi) + _dot_nt(sel, lo)
            return carry

        lax.fori_loop(0, nq, dsum_step, 0)
        dqt_scr[...] = jnp.zeros_like(dqt_scr)

        def kv_step(kb, carry):
            ks = [k_ref[rows(kb), hh * LANES:(hh + 1) * LANES] for hh in range(n_heads)]
            vs = []
            for pp in range(n_pairs):
                vb = v_ref[rows(kb), pp * LANES:(pp + 1) * LANES]
                zero = jnp.zeros_like(vb)
                vs += [jnp.where(lane < B_VDIM, vb, zero), jnp.where(lane < B_VDIM, zero, vb)]

            def make_step(masked):
                def step(qb, acc):
                    qs = [q_ref[rows(qb), hh * LANES:(hh + 1) * LANES] for hh in range(n_heads)]
                    dob = [do_ref[rows(qb), pp * LANES:(pp + 1) * LANES] for pp in range(n_pairs)]
                    ss = [_dot_nt(ks[hh], qs[hh]) for hh in range(n_heads)]
                    dps = [_dot_nt(vs[hh], dob[hh // 2]) for hh in range(n_heads)]
                    pbs, dss = [], []
                    for hh in range(n_heads):
                        stat = (hh // 2, slice(hh % 2, hh % 2 + 1), rows(qb))
                        s = ss[hh] * B_C2
                        if masked:
                            s = jnp.where(_key_le_query(kb, qb, tk, tq), s, NEG)
                        p = jnp.exp2(s - lse_ref[stat])
                        dss.append((p * (dps[hh] - dsum_scr[stat]) * B_SCALE).astype(BF16))
                        pbs.append(p.astype(BF16))
                    for hh in range(n_heads):
                        dqt_scr[hh, :, rows(qb)] += _dot_tn(ks[hh], dss[hh])
                    return tuple([acc[hh] + _dot(dss[hh], qs[hh]) for hh in range(n_heads)]
                                 + [acc[n_heads + hh] + _dot(pbs[hh], dob[hh // 2]) for hh in range(n_heads)])
                return step

            acc = make_step(True)(kb, (jnp.zeros((tk, LANES), F32),) * (2 * n_heads))
            acc = lax.fori_loop(kb + 1, nq, make_step(False), acc)
            for hh in range(n_heads):
                dk_ref[rows(kb), hh * LANES:(hh + 1) * LANES] = acc[hh].astype(BF16)
            for pp in range(n_pairs):
                dv_pair = jnp.where(lane < B_VDIM, acc[n_heads + 2 * pp], acc[n_heads + 2 * pp + 1])
                dv_ref[rows(kb), pp * LANES:(pp + 1) * LANES] = dv_pair.astype(BF16)
            return carry

        lax.fori_loop(0, nq, kv_step, 0)

        def dq_step(qb, carry):
            c, sp, sm = c_ref[rows(qb), :], sp_ref[rows(qb), :], sm_ref[rows(qb), :]
            for hh in range(n_heads):
                dq_ref[rows(qb), hh * LANES:(hh + 1) * LANES] = _rope_t(dqt_scr[hh, :, rows(qb)].T, c, sp, sm).astype(BF16)
            return carry

        lax.fori_loop(0, nq, dq_step, 0)

    pair_full = pl.BlockSpec((None, t, n_heads * LANES), lambda b, j: (b, 0, j))
    one_full = pl.BlockSpec((None, t, n_pairs * LANES), lambda b, j: (b, 0, j))
    row_full = pl.BlockSpec((None, n_pairs, 8, t), lambda b, j: (b, j, 0, 0))
    tab_full = pl.BlockSpec((None, t, LANES), lambda b, j: (b, 0, 0))
    return pl.pallas_call(
        body, name="attn_b_bwd", grid=(bsz, B_HEADS // n_heads),
        in_specs=[pair_full, pair_full, one_full, one_full, one_full, row_full, tab_full, tab_full, tab_full],
        out_specs=[pair_full, pair_full, one_full],
        out_shape=[jax.ShapeDtypeStruct((bsz, t, B_KPAD), BF16), jax.ShapeDtypeStruct((bsz, t, B_KPAD), BF16),
                   jax.ShapeDtypeStruct((bsz, t, B_HEADS * B_VDIM), BF16)],
        scratch_shapes=[pltpu.VMEM((n_heads, LANES, t), F32), pltpu.VMEM((n_pairs, 8, t), F32)],
        compiler_params=_cparams(("parallel", "parallel")))(q, kpad, v, do, o, lse2, *tabs)


ANY = pl.BlockSpec(memory_space=pl.ANY)


def _all_gather(shard, name):
    def body(x_ref, out_ref, send_sems, recv_sems, local_sem):
        x, y, c = lax.axis_index("x"), lax.axis_index("y"), lax.axis_index("c")
        me, sibling = (x, y, c), (x, y, 1 - c)
        chips = [(1 - x, y), (x, 1 - y), (1 - x, 1 - y)]

        def rows(px, py, pc):
            return out_ref.at[4 * px + 2 * py + pc]

        def copy(k, block, to, src=None):
            return pltpu.make_async_remote_copy(
                src_ref=rows(*block) if src is None else src, dst_ref=rows(*block),
                send_sem=send_sems.at[k], recv_sem=recv_sems.at[k], device_id=to, device_id_type=MESH)

        mine = pltpu.make_async_copy(x_ref, rows(*me), local_sem)
        mine.start()
        first = [copy(0, me, sibling, src=x_ref)]
        first += [copy(1 + j, me, (*chip, c), src=x_ref) for j, chip in enumerate(chips)]
        for cp in first:
            cp.start()
        passed = [copy(4 + j, (*chip, c), sibling) for j, chip in enumerate(chips)]
        for j, chip in enumerate(chips):
            copy(1 + j, (*chip, c), me).wait_recv()
            passed[j].start()
        copy(0, sibling, me).wait_recv()
        for j, chip in enumerate(chips):
            copy(4 + j, (*chip, 1 - c), me).wait_recv()
        for cp in first + passed:
            cp.wait_send()
        mine.wait()

    return pl.pallas_call(
        body, name=name, in_specs=[ANY], out_specs=ANY,
        out_shape=jax.ShapeDtypeStruct((N_DEV,) + shard.shape, shard.dtype),
        scratch_shapes=[pltpu.SemaphoreType.DMA((7,)), pltpu.SemaphoreType.DMA((7,)), pltpu.SemaphoreType.DMA])(shard)


def _all_gather_weights(flat, wide, name):
    ns = wide.shape[1]

    def body(f_ref, w_ref, fo_ref, wo_ref, send_sems, recv_sems, local_sems):
        x, y, c = lax.axis_index("x"), lax.axis_index("y"), lax.axis_index("c")
        me, sibling = (x, y, c), (x, y, 1 - c)
        chips = [(1 - x, y), (x, 1 - y), (1 - x, 1 - y)]

        def place(a, px, py, pc):
            idx = 4 * px + 2 * py + pc
            if a == 0:
                return fo_ref.at[idx]
            return wo_ref.at[:, pl.ds(pl.multiple_of(idx * ns, LANES), ns)]

        def copy(a, k, block, to, src=None):
            return pltpu.make_async_remote_copy(
                src_ref=place(a, *block) if src is None else src, dst_ref=place(a, *block),
                send_sem=send_sems.at[a, k], recv_sem=recv_sems.at[a, k], device_id=to, device_id_type=MESH)

        own = (f_ref, w_ref)
        mine = [pltpu.make_async_copy(own[a], place(a, *me), local_sems.at[a]) for a in range(2)]
        first = []
        for a in range(2):
            mine[a].start()
            first.append(copy(a, 0, me, sibling, src=own[a]))
            first += [copy(a, 1 + j, me, (*chip, c), src=own[a]) for j, chip in enumerate(chips)]
        for cp in first:
            cp.start()
        passed = [[copy(a, 4 + j, (*chip, c), sibling) for j, chip in enumerate(chips)] for a in range(2)]
        for a in range(2):
            for j, chip in enumerate(chips):
                copy(a, 1 + j, (*chip, c), me).wait_recv()
                passed[a][j].start()
        for a in range(2):
            copy(a, 0, sibling, me).wait_recv()
            for j, chip in enumerate(chips):
                copy(a, 4 + j, (*chip, 1 - c), me).wait_recv()
        for cp in first + passed[0] + passed[1]:
            cp.wait_send()
        for cp in mine:
            cp.wait()

    return pl.pallas_call(
        body, name=name, in_specs=[ANY, ANY], out_specs=[ANY, ANY],
        out_shape=[jax.ShapeDtypeStruct((N_DEV,) + flat.shape, flat.dtype),
                   jax.ShapeDtypeStruct((wide.shape[0], N_DEV * ns), wide.dtype)],
        scratch_shapes=[pltpu.SemaphoreType.DMA((2, 7)), pltpu.SemaphoreType.DMA((2, 7)),
                        pltpu.SemaphoreType.DMA((2,))])(flat, wide)


N_CHIPS = 4


def _exchange_sibling(blocks, name):
    def body(g_ref, got_ref, send_sems, recv_sems):
        x, y, c = lax.axis_index("x"), lax.axis_index("y"), lax.axis_index("c")
        sends = [pltpu.make_async_remote_copy(
            src_ref=g_ref.at[q, 1 - c], dst_ref=got_ref.at[q], send_sem=send_sems.at[q],
            recv_sem=recv_sems.at[q], device_id=(x, y, 1 - c), device_id_type=MESH) for q in range(N_CHIPS)]
        for cp in sends:
            cp.start()
        for cp in sends:
            cp.wait_recv()
        for cp in sends:
            cp.wait_send()

    return pl.pallas_call(
        body, name=name, in_specs=[ANY], out_specs=ANY,
        out_shape=jax.ShapeDtypeStruct((N_CHIPS,) + blocks.shape[2:], blocks.dtype),
        scratch_shapes=[pltpu.SemaphoreType.DMA((N_CHIPS,)), pltpu.SemaphoreType.DMA((N_CHIPS,))])(blocks)


def _exchange_chips(parts, name):
    def body(p_ref, out_ref, send_sems, recv_sems, local_sem):
        x, y, c = lax.axis_index("x"), lax.axis_index("y"), lax.axis_index("c")
        me = 2 * x + y

        def peer(k):
            return (1 - x if k & 2 else x, 1 - y if k & 1 else y)

        def copy(k):
            px, py = peer(k)
            return pltpu.make_async_remote_copy(
                src_ref=p_ref.at[2 * px + py], dst_ref=out_ref.at[me], send_sem=send_sems.at[k - 1],
                recv_sem=recv_sems.at[k - 1], device_id=(px, py, c), device_id_type=MESH)

        def arrival(k):
            px, py = peer(k)
            slot = out_ref.at[2 * px + py]
            return pltpu.make_async_remote_copy(
                src_ref=slot, dst_ref=slot, send_sem=send_sems.at[k - 1], recv_sem=recv_sems.at[k - 1],
                device_id=(px, py, c), device_id_type=MESH)

        mine = pltpu.make_async_copy(p_ref.at[me], out_ref.at[me], local_sem)
        mine.start()
        sends = [copy(k) for k in range(1, N_CHIPS)]
        for cp in sends:
            cp.start()
        for k in range(1, N_CHIPS):
            arrival(k).wait_recv()
        for cp in sends:
            cp.wait_send()
        mine.wait()

    return pl.pallas_call(
        body, name=name, in_specs=[ANY], out_specs=ANY,
        out_shape=jax.ShapeDtypeStruct(parts.shape, parts.dtype),
        scratch_shapes=[pltpu.SemaphoreType.DMA((N_CHIPS - 1,)), pltpu.SemaphoreType.DMA((N_CHIPS - 1,)),
                        pltpu.SemaphoreType.DMA])(parts)


def _add_pairs(blocks, got, core, *, tr, name):
    q, r, c = got.shape

    def body(core_ref, a_ref, b_ref, o_ref):
        o_ref[...] = (a_ref[...].astype(F32) + b_ref[...].astype(F32)).astype(o_ref.dtype)

    spec = pl.BlockSpec((q, tr, c), lambda i, core_ref: (0, i, 0))
    mine = pl.BlockSpec((q, None, tr, c), lambda i, core_ref: (0, core_ref[0], i, 0))
    return pl.pallas_call(
        body, name=name,
        grid_spec=pltpu.PrefetchScalarGridSpec(num_scalar_prefetch=1, grid=(r // tr,), in_specs=[mine, spec],
                                               out_specs=spec),
        out_shape=jax.ShapeDtypeStruct(got.shape, BF16), compiler_params=_cparams(("parallel",)))(
            core, blocks, got)


def _sum_slots(slots, *, tr, name):
    n_slots, r, c = slots.shape

    def body(s_ref, o_ref):
        acc = s_ref[0].astype(F32)
        for s in range(1, n_slots):
            acc = acc + s_ref[s].astype(F32)
        o_ref[...] = acc

    return pl.pallas_call(
        body, name=name, grid=(r // tr,),
        in_specs=[pl.BlockSpec((n_slots, tr, c), lambda i: (0, i, 0))],
        out_specs=pl.BlockSpec((tr, c), lambda i: (i, 0)),
        out_shape=jax.ShapeDtypeStruct((r, c), F32),
        compiler_params=_cparams(("parallel",)))(slots)


def _adamw_math(w_t, g_t, m_t, v_t):
    m_n = ADAM_B1 * m_t + (1.0 - ADAM_B1) * g_t
    v_n = ADAM_B2 * v_t + (1.0 - ADAM_B2) * (g_t * g_t)
    m_hat = m_n / (1.0 - ADAM_B1 ** ADAM_STEP)
    v_hat = v_n / (1.0 - ADAM_B2 ** ADAM_STEP)
    delta = -ADAM_LR * (m_hat / (jnp.sqrt(v_hat) + ADAM_EPS) + ADAM_WD * w_t)
    return delta, m_n, v_n


def _adamw(w, g, m, v, name):
    shape = w.shape
    cols = shape[-1]
    args = [a.reshape(-1, cols) for a in (w, g, m, v)]
    rows = args[0].shape[0]
    tm = 256 if rows % 256 == 0 else rows
    delta, m_n, v_n = _rowwise(_adamw_math, args, [], [(cols, F32)] * 3, tm=tm, name=name)
    return delta.reshape(shape), m_n.reshape(shape), v_n.reshape(shape)


def _adamw_small(ws, gs, ms, vs, name):
    k = len(ws)
    args = [a.reshape(-1, a.shape[-1]) for group in (ws, gs, ms, vs) for a in group]

    def body(*refs):
        ins, outs = refs[:4 * k], refs[4 * k:]
        for i in range(k):
            res = _adamw_math(*[ins[j * k + i][...] for j in range(4)])
            for j in range(3):
                outs[j * k + i][...] = res[j]

    res = pl.pallas_call(
        body, name=name, out_shape=[jax.ShapeDtypeStruct(a.shape, F32) for a in args[:k]] * 3,
        compiler_params=pltpu.CompilerParams(vmem_limit_bytes=VMEM_LIMIT))(*args)
    return [[res[j * k + i].reshape(ws[i].shape) for i in range(k)] for j in range(3)]


def _local_step(x, positions, target, w):
    bsz, t, _ = x.shape
    n = bsz * t
    tm = 256
    mm = 512
    x2 = x.reshape(n, D_MODEL)
    tgt2 = target.reshape(n, D_MODEL)
    pos = positions.reshape(n)
    tb2 = _rope_tables(pos, B_ROPE_THETA, B_ROPE, B_NOPE, "rope_tables_b")
    tabs_b = [a.reshape(bsz, t, LANES) for a in tb2]

    w_a_in = w["a_w_in"]
    w_a_out = w["a_w_out"]
    w_down = w["kv_w_down"]
    w_down_p = jnp.zeros((D_MODEL, 3 * LANES), BF16).at[:, :B_KV_LORA].set(w_down[:, :B_KV_LORA])
    w_down_p = w_down_p.at[:, B_KV_LORA + B_NOPE:B_KV_LORA + B_QK_DIM].set(w_down[:, B_KV_LORA:])
    wu = w["kv_w_up"].reshape(B_KV_LORA, B_HEADS, B_NOPE + B_VDIM)
    w_upk = jnp.pad(wu[:, :, :B_NOPE], ((0, 0), (0, 0), (0, LANES - B_NOPE))).reshape(B_KV_LORA, B_KPAD)
    w_upv = wu[:, :, B_NOPE:].reshape(B_KV_LORA, B_HEADS * B_VDIM)
    w_b_in = w["b_w_in"]
    w_q_p = jnp.pad(w["b_w_q_up"].reshape(B_Q_LORA, B_HEADS, B_QK_DIM),
                    ((0, 0), (0, 0), (0, LANES - B_QK_DIM))).reshape(B_Q_LORA, B_KPAD)
    w_b_out = w["b_w_out"]

    def tab_extras(tabs2, rows):
        return [(a, (rows, LANES), lambda j, i, kk: (i, 0)) for a in tabs2]

    (hn_a,) = _rowwise(lambda xt, g: (_rms(xt, g),), [x2], [w["a_pre_norm"]], [(D_MODEL, BF16)],
                       tm=tm, name="a_pre_norm")

    def rope_epilogue(acc, o_ref, rows, c_ref, sp_ref, sm_ref):
        c, sp, sm = c_ref[rows, :], sp_ref[rows, :], sm_ref[rows, :]
        for h in range(acc.shape[1] // LANES):
            hs = slice(h * LANES, (h + 1) * LANES)
            o_ref[rows, hs] = _rope(acc[:, hs], c, sp, sm).astype(BF16)

    def to_group(a, d):
        if d == 1:
            return a
        return a.reshape(bsz, t // d, d, a.shape[-1]).transpose(0, 2, 1, 3).reshape(n, a.shape[-1])

    def from_group(a, d):
        if d == 1:
            return a
        return a.reshape(bsz, d, t // d, a.shape[-1]).transpose(0, 2, 1, 3).reshape(n, a.shape[-1])

    def rows_to_cols(r, d):
        return r.reshape(bsz, d, A_HEADS, t // d).transpose(0, 3, 1, 2).reshape(n, A_HEADS)

    def cols_to_rows(cc, d):
        return cc.reshape(bsz, t // d, d, A_HEADS).transpose(0, 2, 3, 1).reshape(bsz * d, A_HEADS, t // d)

    z_a = _matmul(hn_a, w_a_in, b_cols=(A_QKV, A_WIDTH), out_dtype=F32, tm=mm, tn=A_WIDTH, name="a_gate")
    hn_g, tabs_g, qk_g, v_g, o_g, lse_g = [], [], [], [], [], []
    for g, d in enumerate(A_DILATIONS):
        hn_g.append(to_group(hn_a, d))
        tabs_g.append(_rope_tables(to_group(pos.reshape(n, 1), d).reshape(n), A_ROPE_THETA, A_ROT_DIM, 0,
                                   f"rope_tables_a_g{g}"))
        col0 = g * 3 * A_WIDTH
        qk = _matmul(hn_g[g], w_a_in, b_cols=(col0, 2 * A_WIDTH), out_dtype=BF16, tm=mm, tn=A_WIDTH,
                     name=f"a_qk_g{g}", epilogue=rope_epilogue, epilogue_rows=EPILOGUE_ROWS,
                     extras=tab_extras(tabs_g[g], mm))
        v = _matmul(hn_g[g], w_a_in, b_cols=(col0 + 2 * A_WIDTH, A_WIDTH), out_dtype=BF16, tm=mm, tn=A_WIDTH,
                    name=f"a_v_g{g}")
        qk_g.append(qk.reshape(bsz * d, t // d, 2 * A_WIDTH))
        v_g.append(v.reshape(bsz * d, t // d, A_WIDTH))
        o, lse = _attn_a_fwd(qk_g[g], v_g[g], f"attn_a_fwd_g{g}")
        o_g.append(from_group(o.reshape(n, A_WIDTH), d))
        lse_g.append(rows_to_cols(lse, d))

    def merge_fn(o0, o1, o2, l0, l1, l2, z):
        lmax = jnp.maximum(jnp.maximum(l0, l1), l2)
        e0, e1, e2 = jnp.exp2(l0 - lmax), jnp.exp2(l1 - lmax), jnp.exp2(l2 - lmax)
        den = e0 + e1 + e2
        w0, w1, w2 = e0 / den, e1 / den, e2 / den
        parts = []
        for h in range(A_HEADS):
            hs = slice(h * A_HEAD_DIM, (h + 1) * A_HEAD_DIM)
            parts.append(w0[:, h:h + 1] * o0[:, hs] + w1[:, h:h + 1] * o1[:, hs] + w2[:, h:h + 1] * o2[:, hs])
        o = jnp.concatenate(parts, axis=1)
        return o * _silu(z), o, lmax + jnp.log2(den)

    y_a, o_a2, lse_a = _rowwise(merge_fn, [*o_g, *lse_g, z_a], [],
                                [(A_WIDTH, BF16), (A_WIDTH, F32), (A_HEADS, F32)], tm=tm, name="a_merge_gate")
    out_a = _matmul(y_a, w_a_out, out_dtype=F32, tm=mm, tn=D_MODEL, name="a_out")

    def mid_fn(xt, out, g_post, g_kv, g_b):
        h1 = xt + _rms(out, g_post)
        return h1, _rms(h1, g_kv), _rms(h1, g_b)

    h1, hn_kv, hn_b = _rowwise(mid_fn, [x2, out_a], [w["a_post_norm"], w["kv_norm"], w["b_pre_norm"]],
                               [(D_MODEL, F32), (D_MODEL, BF16), (D_MODEL, BF16)], tm=tm, name="a_post_norm")

    ckr = _matmul(hn_kv, w_down_p, out_dtype=F32, tm=mm, tn=3 * LANES, name="kv_down")

    def latent_fn(ck, c, sp, sm, g):
        return _rms(ck[:, :B_KV_LORA], g), _rope(ck[:, B_KV_LORA:], c, sp, sm)

    c_kv, k_rope = _rowwise(latent_fn, [ckr, *tb2], [w["kv_latent_norm"]], [(B_KV_LORA, BF16), (LANES, F32)],
                            tm=tm, name="kv_latent_norm")

    def kpad_epilogue(acc, o_ref, rows, kr_ref):
        kr = kr_ref[rows, :]
        for h in range(acc.shape[1] // LANES):
            hs = slice(h * LANES, (h + 1) * LANES)
            o_ref[rows, hs] = (acc[:, hs] + kr).astype(BF16)

    kpad = _matmul(c_kv, w_upk, out_dtype=BF16, tm=mm, tn=1024, name="kv_up_k", epilogue=kpad_epilogue,
                   epilogue_rows=EPILOGUE_ROWS,
                   extras=[(k_rope, (mm, LANES), lambda j, i, kk: (i, 0))])
    v_b = _matmul(c_kv, w_upv, out_dtype=BF16, tm=mm, tn=1024, name="kv_up_v")

    proj_b = _matmul(hn_b, w_b_in, out_dtype=F32, tm=mm, tn=w_b_in.shape[1], name="b_in")
    (c_q,) = _rowwise(lambda p, g: (_rms(p[:, :B_Q_LORA], g),), [proj_b], [w["b_q_norm"]], [(B_Q_LORA, BF16)],
                      tm=tm, name="b_q_norm")

    q_b = _matmul(c_q, w_q_p, out_dtype=BF16, tm=mm, tn=1024, name="b_q_up", epilogue=rope_epilogue,
                  epilogue_rows=EPILOGUE_ROWS,
                  extras=tab_extras(tb2, mm))
    q_b3, kpad3, v_b3 = q_b.reshape(bsz, t, B_KPAD), kpad.reshape(bsz, t, B_KPAD), v_b.reshape(bsz, t, -1)
    w_vt = jnp.pad(wu[:, :, B_NOPE:], ((0, 0), (0, 0), (0, LANES - B_VDIM))).reshape(B_KV_LORA, B_KPAD)
    vt1 = _value_heads_t(c_kv, w_vt, bsz, t)
    o_b, lse_b = _attn_b_fwd(q_b3, kpad3, vt1)
    o_b2 = o_b.reshape(n, -1)
    (y_b,) = _rowwise(lambda o, p: (o * _silu(p[:, B_Q_LORA:]),), [o_b2, proj_b], [], [(D_MODEL, BF16)],
                      tm=tm, name="b_gate_mul")
    out_b = _matmul(y_b, w_b_out, out_dtype=F32, tm=mm, tn=D_MODEL, name="b_out")

    def head_fn(h1t, out, tgt, g):
        e = h1t + _rms(out, g) - tgt
        loss_row = 0.5 * jnp.mean(e * e, axis=-1, keepdims=True)
        dh2 = e * (1.0 / D_MODEL)
        d_out, dg = _rms_bwd(out, g, dh2)
        return dh2, d_out, dg, jnp.broadcast_to(loss_row * (1.0 / LANES), (loss_row.shape[0], LANES))

    dh2, d_out_b, dg_b_post, loss_acc = _rowwise(
        head_fn, [h1, out_b, tgt2], [w["b_post_norm"]], [(D_MODEL, F32), (D_MODEL, BF16)], [D_MODEL, LANES],
        tm=tm, name="loss_head")

    dy_b = _matmul(d_out_b, w_b_out, tb=True, out_dtype=BF16, tm=mm, tn=D_MODEL, name="b_out_dx")
    gw_b_out = _matmul(y_b, d_out_b, ta=True, out_dtype=BF16, tm=mm, tn=D_MODEL, tk=2048, name="b_out_dw")

    def gate_b_bwd(dy, o, p):
        z = p[:, B_Q_LORA:]
        return dy * _silu(z), dy * o * _silu_grad(z)

    do_b, dz_b = _rowwise(gate_b_bwd, [dy_b, o_b2, proj_b], [], [(D_MODEL, BF16), (D_MODEL, F32)],
                          tm=tm, name="b_gate_bwd")
    do_b3 = do_b.reshape(bsz, t, -1)
    dq_b, dk_b, dv_b = _attn_b_bwd(q_b3, kpad3, v_b3, do_b3, o_b, lse_b, tabs_b)
    dq_b2, dk_b2, dv_b2 = dq_b.reshape(n, B_KPAD), dk_b.reshape(n, B_KPAD), dv_b.reshape(n, -1)

    dc_kv = _matmul(dk_b2, w_upk, tb=True, out_dtype=F32, tm=mm, tn=B_KV_LORA, name="kv_up_k_dx")
    dc_kv = _matmul(dv_b2, w_upv, tb=True, out_dtype=BF16, tm=mm, tn=B_KV_LORA, name="kv_up_v_dx",
                    epilogue=_add_epilogue, extras=[(dc_kv, (mm, B_KV_LORA), lambda j, i, kk: (i, j))])
    gw_upk = _matmul(c_kv, dk_b2, ta=True, out_dtype=BF16, tm=B_KV_LORA, tn=1024, tk=2048, name="kv_up_k_dw")
    gw_upv = _matmul(c_kv, dv_b2, ta=True, out_dtype=BF16, tm=B_KV_LORA, tn=1024, tk=2048, name="kv_up_v_dw")

    def latent_bwd(ck, dck, dk, c, sp, sm, g):
        d1, dg = _rms_bwd(ck[:, :B_KV_LORA], g, dck)
        ksum = dk[:, :LANES].astype(F32)
        for h in range(1, B_HEADS):
            ksum = ksum + dk[:, h * LANES:(h + 1) * LANES].astype(F32)
        lane = lax.broadcasted_iota(jnp.int32, ksum.shape, 1)
        ksum = jnp.where((lane >= B_NOPE) & (lane < B_QK_DIM), ksum, 0.0)
        return jnp.concatenate([d1, _rope_t(ksum, c, sp, sm)], axis=1), dg

    dckr, dg_latent = _rowwise(latent_bwd, [ckr, dc_kv, dk_b2, *tb2], [w["kv_latent_norm"]],
                               [(3 * LANES, BF16)], [B_KV_LORA], tm=tm, name="kv_latent_bwd")
    dhn_kv = _matmul(dckr, w_down_p, tb=True, out_dtype=BF16, tm=mm, tn=D_MODEL, name="kv_down_dx")
    gw_down_p = _matmul(hn_kv, dckr, ta=True, out_dtype=BF16, tm=mm, tn=3 * LANES, tk=2048, name="kv_down_dw")

    dc_q = _matmul(dq_b2, w_q_p, tb=True, out_dtype=BF16, tm=mm, tn=B_Q_LORA, name="b_q_up_dx")
    gw_q_p = _matmul(c_q, dq_b2, ta=True, out_dtype=BF16, tm=B_Q_LORA, tn=1024, tk=2048, name="b_q_up_dw")

    def q_norm_bwd(p, dcq, dz, g):
        d1, dg = _rms_bwd(p[:, :B_Q_LORA], g, dcq)
        return jnp.concatenate([d1, dz], axis=1), dg

    dproj_b, dg_q_norm = _rowwise(q_norm_bwd, [proj_b, dc_q, dz_b], [w["b_q_norm"]],
                                  [(w_b_in.shape[1], BF16)], [B_Q_LORA], tm=tm, name="b_q_norm_bwd")
    dhn_b = _matmul(dproj_b, w_b_in, tb=True, out_dtype=BF16, tm=mm, tn=D_MODEL, name="b_in_dx")
    gw_b_in = _matmul(hn_b, dproj_b, ta=True, out_dtype=BF16, tm=mm, tn=w_b_in.shape[1], tk=2048, name="b_in_dw")

    def mid_bwd(h1t, dh2t, dkv, db, g_kv, g_b, g_post, out):
        dxa, ra = _rms_bwd(h1t, g_kv, dkv)
        dxb, rb = _rms_bwd(h1t, g_b, db)
        dh1 = dh2t + dxa + dxb
        d_out, rp = _rms_bwd(out, g_post, dh1)
        return dh1, d_out, ra, rb, rp

    def mid_bwd_fn(h1t, dh2t, dkv, db, out, g_kv, g_b, g_post):
        return mid_bwd(h1t, dh2t, dkv, db, g_kv, g_b, g_post, out)

    dh1, d_out_a, dg_kv, dg_b_pre, dg_a_post = _rowwise(
        mid_bwd_fn, [h1, dh2, dhn_kv, dhn_b, out_a], [w["kv_norm"], w["b_pre_norm"], w["a_post_norm"]],
        [(D_MODEL, F32), (D_MODEL, BF16)], [D_MODEL] * 3, tm=tm, name="mid_bwd")

    dy_a = _matmul(d_out_a, w_a_out, tb=True, out_dtype=BF16, tm=mm, tn=A_WIDTH, name="a_out_dx")
    gw_a_out = _matmul(y_a, d_out_a, ta=True, out_dtype=BF16, tm=mm, tn=D_MODEL, tk=2048, name="a_out_dw")

    def gate_a_bwd(dy, o, z):
        do = dy * _silu(z)
        prod = do * o
        lane = lax.broadcasted_iota(jnp.int32, (prod.shape[0], A_HEADS), 1)
        dsum = jnp.zeros((prod.shape[0], A_HEADS), F32)
        for h in range(A_HEADS):
            col = jnp.sum(prod[:, h * A_HEAD_DIM:(h + 1) * A_HEAD_DIM], axis=1, keepdims=True)
            dsum = jnp.where(lane == h, col, dsum)
        return do, dy * o * _silu_grad(z), dsum

    do_a, dz_a, dsum_a = _rowwise(gate_a_bwd, [dy_a, o_a2, z_a], [],
                                  [(A_WIDTH, BF16), (A_WIDTH, BF16), (A_HEADS, F32)], tm=tm, name="a_gate_bwd")
    dhn_a = _matmul(dz_a, w_a_in, b_cols=(A_QKV, A_WIDTH), tb=True, out_dtype=F32, tm=mm, tn=D_MODEL,
                    name="a_gate_dx")
    gw_a_in, dhn_groups = None, []
    for g, d in enumerate(A_DILATIONS):
        s_n, ln = bsz * d, t // d
        dqkv = _attn_a_bwd(qk_g[g], v_g[g], to_group(do_a, d).reshape(s_n, ln, A_WIDTH), cols_to_rows(lse_a, d),
                           cols_to_rows(dsum_a, d), [a.reshape(s_n, ln, LANES) for a in tabs_g[g]],
                           f"attn_a_bwd_g{g}").reshape(n, 3 * A_WIDTH)
        w_cols = (g * 3 * A_WIDTH, 3 * A_WIDTH)
        if d == 1:
            dhn_a = _matmul(dqkv, w_a_in, b_cols=w_cols, tb=True, out_dtype=F32, tm=mm, tn=D_MODEL,
                            name=f"a_qkv_dx_g{g}", epilogue=_add_epilogue,
                            extras=[(dhn_a, (mm, D_MODEL), lambda j, i, kk: (i, j))])
        else:
            dhn_groups.append(from_group(_matmul(dqkv, w_a_in, b_cols=w_cols, tb=True, out_dtype=BF16, tm=mm,
                                                 tn=D_MODEL, name=f"a_qkv_dx_g{g}"), d))
        gw_a_in = _matmul(hn_g[g], dqkv, ta=True, out_dtype=BF16, tm=mm, tn=1024, tk=2048, name=f"a_qkv_dw_g{g}",
                          out_into=(gw_a_in, g * 3 * A_WIDTH, A_QKV + A_WIDTH))
    gw_a_in = _matmul(hn_a, dz_a, ta=True, out_dtype=BF16, tm=mm, tn=1024, tk=2048, name="a_gate_dw",
                      out_into=(gw_a_in, A_QKV, A_QKV + A_WIDTH))

    def first_bwd(xt, dhn, dhn_1, dhn_2, dh1t, g):
        dx, dg = _rms_bwd(xt, g, dhn + dhn_1 + dhn_2)
        return dh1t + dx, dg

    grad_x, dg_a_pre = _rowwise(first_bwd, [x2, dhn_a, *dhn_groups, dh1], [w["a_pre_norm"]], [(D_MODEL, F32)],
                                [D_MODEL], tm=tm, name="a_pre_norm_bwd")

    gw_down = jnp.concatenate([gw_down_p[:, :B_KV_LORA], gw_down_p[:, B_KV_LORA + B_NOPE:B_KV_LORA + B_QK_DIM]], axis=1)
    gw_up = jnp.concatenate([gw_upk.reshape(B_KV_LORA, B_HEADS, LANES)[:, :, :B_NOPE],
                             gw_upv.reshape(B_KV_LORA, B_HEADS, B_VDIM)], axis=2).reshape(B_KV_LORA, -1)
    gw_q_up = gw_q_p.reshape(B_Q_LORA, B_HEADS, LANES)[:, :, :B_QK_DIM].reshape(B_Q_LORA, -1)
    grads = {"a_w_in": gw_a_in, "a_w_out": gw_a_out, "kv_w_down": gw_down, "kv_w_up": gw_up,
             "b_w_in": gw_b_in, "b_w_q_up": gw_q_up, "b_w_out": gw_b_out}
    gains = {"a_pre_norm": dg_a_pre, "a_post_norm": dg_a_post, "kv_norm": dg_kv, "kv_latent_norm": dg_latent,
             "b_pre_norm": dg_b_pre, "b_q_norm": dg_q_norm, "b_post_norm": dg_b_post}
    gains = {k: jnp.sum(a, axis=0) for k, a in gains.items()}
    return jnp.sum(loss_acc), grad_x.reshape(bsz, t, D_MODEL), grads, gains


WEIGHT_ORDER = ("a_pre_norm", "a_w_in", "a_w_out", "a_post_norm", "kv_norm", "kv_w_down", "kv_latent_norm",
                "kv_w_up", "b_pre_norm", "b_w_in", "b_q_norm", "b_w_q_up", "b_w_out", "b_post_norm")
MATRICES = (("a_w_in", 1024, 10240, 1), ("a_w_out", 1024, 1024, 0), ("kv_w_down", 1024, 288, 0),
            ("kv_w_up", 256, 2048, 1), ("b_w_in", 1024, 1408, 1), ("b_w_q_up", 384, 1536, 1),
            ("b_w_out", 1024, 1024, 0))
SHARDED_GAINS = ("a_pre_norm", "a_post_norm")
GAIN_WIDTHS = (("a_pre_norm", 1024), ("a_post_norm", 1024), ("kv_norm", 1024), ("kv_latent_norm", 256),
               ("b_pre_norm", 1024), ("b_q_norm", 384), ("b_post_norm", 1024))
GAIN_ROWS = 48


def _shard_rows(rows, cols):
    return rows * cols // (N_DEV * LANES)


def _whole_from_blocks(blocks, rows, cols, axis):
    if axis == 1:
        return blocks.reshape(N_DEV, rows, cols // N_DEV).transpose(1, 0, 2).reshape(rows, cols)
    return blocks.reshape(rows, cols)


def _blocks_from_whole(whole, rows, cols, axis):
    if axis == 1:
        whole = whole.reshape(rows, N_DEV, cols // N_DEV).transpose(1, 0, 2)
    return whole.reshape(N_DEV, -1, LANES)


def kernel(x, positions, a_pre_norm, a_w_in, a_w_out, a_post_norm, kv_norm, kv_w_down, kv_latent_norm, kv_w_up, b_pre_norm, b_w_in, b_q_norm, b_w_q_up, b_w_out, b_post_norm, loss_target, m_a_pre_norm, m_a_w_in, m_a_w_out, m_a_post_norm, m_kv_norm, m_kv_w_down, m_kv_latent_norm, m_kv_w_up, m_b_pre_norm, m_b_w_in, m_b_q_norm, m_b_w_q_up, m_b_w_out, m_b_post_norm, v_a_pre_norm, v_a_w_in, v_a_w_out, v_a_post_norm, v_kv_norm, v_kv_w_down, v_kv_latent_norm, v_kv_w_up, v_b_pre_norm, v_b_w_in, v_b_q_norm, v_b_w_q_up, v_b_w_out, v_b_post_norm):
    weights = dict(a_pre_norm=a_pre_norm, a_w_in=a_w_in, a_w_out=a_w_out, a_post_norm=a_post_norm, kv_norm=kv_norm,
                   kv_w_down=kv_w_down, kv_latent_norm=kv_latent_norm, kv_w_up=kv_w_up, b_pre_norm=b_pre_norm,
                   b_w_in=b_w_in, b_q_norm=b_q_norm, b_w_q_up=b_w_q_up, b_w_out=b_w_out, b_post_norm=b_post_norm)
    m_in = dict(a_pre_norm=m_a_pre_norm, a_w_in=m_a_w_in, a_w_out=m_a_w_out, a_post_norm=m_a_post_norm,
                kv_norm=m_kv_norm, kv_w_down=m_kv_w_down, kv_latent_norm=m_kv_latent_norm, kv_w_up=m_kv_w_up,
                b_pre_norm=m_b_pre_norm, b_w_in=m_b_w_in, b_q_norm=m_b_q_norm, b_w_q_up=m_b_w_q_up,
                b_w_out=m_b_w_out, b_post_norm=m_b_post_norm)
    v_in = dict(a_pre_norm=v_a_pre_norm, a_w_in=v_a_w_in, a_w_out=v_a_w_out, a_post_norm=v_a_post_norm,
                kv_norm=v_kv_norm, kv_w_down=v_kv_w_down, kv_latent_norm=v_kv_latent_norm, kv_w_up=v_kv_w_up,
                b_pre_norm=v_b_pre_norm, b_w_in=v_b_w_in, b_q_norm=v_b_q_norm, b_w_q_up=v_b_w_q_up,
                b_w_out=v_b_w_out, b_post_norm=v_b_post_norm)
    me = 4 * lax.axis_index("x") + 2 * lax.axis_index("y") + lax.axis_index("c")

    wide = MATRICES[0][0]
    flat = jnp.concatenate([weights[name].astype(BF16).reshape(-1, LANES) for name, _, _, _ in MATRICES[1:]], axis=0)
    gathered, w_wide = _all_gather_weights(flat, weights[wide][0].astype(BF16), "gather_weights")
    whole = {wide: w_wide}
    off = 0
    for name, rows, cols, axis in MATRICES[1:]:
        nr = _shard_rows(rows, cols)
        whole[name] = _whole_from_blocks(gathered[:, off:off + nr], rows, cols, axis)
        off += nr
    gain_shard = jnp.concatenate([weights[name].reshape(1, LANES) for name in SHARDED_GAINS]
                                 + [jnp.zeros((8 - len(SHARDED_GAINS), LANES), F32)], axis=0)
    gain_blocks = _all_gather(gain_shard, "gather_gains")
    for i, name in enumerate(SHARDED_GAINS):
        whole[name] = gain_blocks[:, i, :].reshape(1, D_MODEL)
    for name in ("kv_norm", "kv_latent_norm", "b_pre_norm", "b_q_norm", "b_post_norm"):
        whole[name] = weights[name].reshape(1, -1)

    loss_part, grad_x, grads, gains = _local_step(x, positions, loss_target, whole)

    blocks = jnp.concatenate([_blocks_from_whole(grads[name], rows, cols, axis).astype(BF16)
                              for name, rows, cols, axis in MATRICES], axis=1)
    blocks = blocks.reshape(N_CHIPS, 2, blocks.shape[1], LANES)
    got = _exchange_sibling(blocks, "scatter_grads_core")
    core = lax.axis_index("c").astype(jnp.int32).reshape(1)
    landed = _exchange_chips(_add_pairs(blocks, got, core, tr=2512, name="add_core_grads"), "scatter_grads_chip")
    summed = _sum_slots(landed, tr=2512, name="sum_grads")
    grad_out = {}
    off = 0
    for name, rows, cols, axis in MATRICES:
        nr = _shard_rows(rows, cols)
        grad_out[name] = summed[off:off + nr].reshape(weights[name].shape)
        off += nr

    vec = jnp.concatenate([gains[name] for name, _ in GAIN_WIDTHS] + [jnp.full((LANES,), loss_part, F32)])
    vec = jnp.pad(vec, (0, GAIN_ROWS * LANES - vec.shape[0])).reshape(GAIN_ROWS, LANES)
    total = _sum_slots(_all_gather(vec, "gather_gain_grads"), tr=GAIN_ROWS, name="sum_gain_grads").reshape(-1)
    off = 0
    for name, width in GAIN_WIDTHS:
        g = total[off:off + width]
        if name in SHARDED_GAINS:
            g = lax.dynamic_slice(g, (me * LANES,), (LANES,))
        grad_out[name] = g.reshape(weights[name].shape)
        off += width
    loss = total[off]

    deltas, new_m, new_v = {}, {}, {}
    big = "a_w_in"
    deltas[big], new_m[big], new_v[big] = _adamw(weights[big], grad_out[big], m_in[big], v_in[big], "adamw_" + big)
    small = [name for name in WEIGHT_ORDER if name != big]
    res = _adamw_small(*[[d[name] for name in small] for d in (weights, grad_out, m_in, v_in)], "adamw_small")
    for out, vals in zip((deltas, new_m, new_v), res):
        out.update(zip(small, vals))
    return (loss, grad_x, *[grad_out[k] for k in WEIGHT_ORDER], *[deltas[k] for k in WEIGHT_ORDER],
            *[new_m[k] for k in WEIGHT_ORDER], *[new_v[k] for k in WEIGHT_ORDER])
```

```python
import jax
import jax.numpy as jnp
from jax import lax
from jax.experimental import pallas as pl
from jax.experimental.pallas import tpu as pltpu

F32 = jnp.float32
BF16 = jnp.bfloat16

N_DEV = 8
D_MODEL = 1024
NORM_EPS = 1e-6
A_GROUPS = 3
A_DILATIONS = (1, 4, 16)
A_HEADS = 8
A_HEAD_DIM = 128
A_WIDTH = 1024
A_ROT_DIM = 32
A_ROPE_THETA = 500000.0
A_QKV = A_GROUPS * 3 * A_WIDTH
B_HEADS = 16
B_NOPE = 64
B_ROPE = 32
B_QK_DIM = 96
B_VDIM = 64
B_Q_LORA = 384
B_KV_LORA = 256
B_ROPE_THETA = 10000.0
B_KPAD = B_HEADS * 128
ADAM_LR = 0.001
ADAM_B1 = 0.9
ADAM_B2 = 0.999
ADAM_EPS = 1e-08
ADAM_WD = 0.01
ADAM_STEP = 10

LANES = 128
BAND = 128
EPILOGUE_ROWS = 128
NEG = -1e30
VMEM_LIMIT = 56 * 1024 * 1024
MESH = pl.DeviceIdType.MESH


def _cparams(sem):
    return pltpu.CompilerParams(dimension_semantics=sem, vmem_limit_bytes=VMEM_LIMIT)


def _rowwise(fn, rows, bcast, outs, accs=(), *, tm, name):
    n = rows[0].shape[0]
    nr, nb, no = len(rows), len(bcast), len(outs)

    def body(*refs):
        res = fn(*[r[...] for r in refs[:nr + nb]])
        out_refs = refs[nr + nb:nr + nb + no]
        acc_refs = refs[nr + nb + no:]
        for r, v in zip(out_refs, res[:no]):
            r[...] = v.astype(r.dtype)
        if acc_refs:
            @pl.when(pl.program_id(0) == 0)
            def _():
                for r in acc_refs:
                    r[...] = jnp.zeros_like(r)
            for r, v in zip(acc_refs, res[no:]):
                r[...] += v.reshape(tm // 8, 8, v.shape[-1]).sum(axis=0)

    in_specs = [pl.BlockSpec((tm, a.shape[1]), lambda i: (i, 0)) for a in rows]
    in_specs += [pl.BlockSpec(a.shape, lambda i: (0, 0)) for a in bcast]
    out_specs = [pl.BlockSpec((tm, c), lambda i: (i, 0)) for c, _ in outs]
    out_specs += [pl.BlockSpec((8, c), lambda i: (0, 0)) for c in accs]
    out_shape = [jax.ShapeDtypeStruct((n, c), dt) for c, dt in outs]
    out_shape += [jax.ShapeDtypeStruct((8, c), F32) for c in accs]
    return pl.pallas_call(
        body, name=name, grid=(n // tm,), in_specs=in_specs, out_specs=out_specs, out_shape=out_shape,
        compiler_params=_cparams(("arbitrary",)))(*rows, *bcast)


def _matmul(a, b, *, out_dtype, tm, tn, tk=None, name, epilogue=None, extras=(), ta=False, tb=False,
            epilogue_rows=None, b_cols=None, out_into=None):
    epilogue_rows = epilogue_rows or tm
    k, m = a.shape[::-1] if not ta else a.shape
    col0, width = b_cols or (0, b.shape[1])
    n = b.shape[0] if tb else width
    tk = tk or k
    nk = k // tk
    b_off = col0 // (tk if tb else tn)
    assert col0 % (tk if tb else tn) == 0 and (k == width if tb else True)
    ne = len(extras)
    dot = _dot_tn if ta else (_dot_nt if tb else _dot)
    assert epilogue is None or (nk == 1 and not ta)

    def body(*refs):
        a_ref, b_ref = refs[:2]
        ex = refs[2:2 + ne]
        o_ref = refs[2 + ne + (1 if out_into is not None and out_into[0] is not None else 0)]
        if epilogue is not None:
            b_tile = b_ref[...].astype(BF16)
            for r0 in range(0, tm, epilogue_rows):
                rows = slice(r0, r0 + epilogue_rows)
                epilogue(dot(a_ref[rows, :].astype(BF16), b_tile), o_ref, rows, *ex)
            return
        part = dot(a_ref[...].astype(BF16), b_ref[...].astype(BF16))
        if nk == 1:
            o_ref[...] = part.astype(o_ref.dtype)
        else:
            acc_ref = refs[-1]
            kk = pl.program_id(2)

            @pl.when(kk == 0)
            def _():
                acc_ref[...] = part

            @pl.when(kk > 0)
            def _():
                acc_ref[...] += part

            @pl.when(kk == nk - 1)
            def _():
                o_ref[...] = acc_ref[...].astype(o_ref.dtype)

    a_spec = pl.BlockSpec((tk, tm), lambda j, i, kk: (kk, i)) if ta else pl.BlockSpec((tm, tk), lambda j, i, kk: (i, kk))
    b_spec = (pl.BlockSpec((tn, tk), lambda j, i, kk: (j, kk + b_off)) if tb
              else pl.BlockSpec((tk, tn), lambda j, i, kk: (kk, j + b_off)))
    in_specs = [a_spec, b_spec] + [pl.BlockSpec(bs, im) for _, bs, im in extras]
    operands = [a, b] + [e[0] for e in extras]
    aliases = {}
    if out_into is not None:
        prev, out0, n_total = out_into
        o_off = out0 // tn
        assert out0 % tn == 0
        if prev is not None:
            in_specs.append(ANY)
            operands.append(prev)
            aliases = {len(operands) - 1: 0}
    else:
        o_off, n_total = 0, n
    return pl.pallas_call(
        body, name=name, grid=(n // tn, m // tm, nk), in_specs=in_specs,
        out_specs=pl.BlockSpec((tm, tn), lambda j, i, kk: (i, j + o_off)),
        out_shape=jax.ShapeDtypeStruct((m, n_total), out_dtype), input_output_aliases=aliases,
        scratch_shapes=[pltpu.VMEM((tm, tn), F32)] if nk > 1 else [],
        compiler_params=_cparams(("parallel", "parallel", "arbitrary")))(*operands)


def _add_epilogue(acc, o_ref, rows, prev_ref):
    o_ref[rows, :] = (acc + prev_ref[rows, :]).astype(o_ref.dtype)


def _rope(x, c, sp, sm):
    return x * c + pltpu.roll(x, 16, 1) * sp + pltpu.roll(x, LANES - 16, 1) * sm


def _rope_t(dy, c, sp, sm):
    return dy * c + pltpu.roll(dy * sp, LANES - 16, 1) + pltpu.roll(dy * sm, 16, 1)


def _rope_tables(positions, theta, rot_dim, lane0, name):
    n = positions.shape[0]
    half = rot_dim // 2
    inv_freq = 1.0 / (theta ** (jnp.arange(half, dtype=F32) * (2.0 / rot_dim)))
    freq = jnp.concatenate([jnp.zeros((lane0,), F32), inv_freq, inv_freq,
                            jnp.zeros((LANES - lane0 - rot_dim,), F32)]).reshape(1, LANES)

    def fn(p, f):
        ang = p * f
        cos, sin = jnp.cos(ang), jnp.sin(ang)
        lane = lax.broadcasted_iota(jnp.int32, ang.shape, 1) - lane0
        first, second = (lane >= 0) & (lane < half), (lane >= half) & (lane < rot_dim)
        return jnp.where(first | second, cos, 1.0), jnp.where(second, sin, 0.0), jnp.where(first, -sin, 0.0)

    return _rowwise(fn, [positions.astype(F32).reshape(n, 1)], [freq], [(LANES, F32)] * 3, tm=512, name=name)


def _value_heads_t(c_kv, w_vt, bsz, t):
    n, k = c_kv.shape
    tm, tn = 512, 1024

    def body(a_ref, b_ref, o_ref):
        acc = _dot(a_ref[...], b_ref[...])
        lane = lax.broadcasted_iota(jnp.int32, (1, tn), 1)
        acc = acc + jnp.where(lax.rem(lane, LANES) == B_VDIM, 1.0, 0.0)
        o_ref[...] = acc.T.astype(BF16)

    per_seq = t // tm
    return pl.pallas_call(
        body, name="kv_up_v_t", grid=(w_vt.shape[1] // tn, n // tm),
        in_specs=[pl.BlockSpec((tm, k), lambda j, i: (i, 0)), pl.BlockSpec((k, tn), lambda j, i: (0, j))],
        out_specs=pl.BlockSpec((None, tn, tm), lambda j, i: (i // per_seq, j, lax.rem(i, per_seq))),
        out_shape=jax.ShapeDtypeStruct((bsz, w_vt.shape[1], t), BF16),
        compiler_params=_cparams(("parallel", "parallel")))(c_kv, w_vt)


def _rms(x, g):
    xf = x.astype(F32)
    return xf * lax.rsqrt(jnp.mean(xf * xf, axis=-1, keepdims=True) + NORM_EPS) * g


def _rms_bwd(x, g, dy):
    xf = x.astype(F32)
    rstd = lax.rsqrt(jnp.mean(xf * xf, axis=-1, keepdims=True) + NORM_EPS)
    xhat = xf * rstd
    dxhat = dy * g
    dx = rstd * (dxhat - xhat * jnp.mean(dxhat * xhat, axis=-1, keepdims=True))
    return dx, dy * xhat


def _silu(z):
    return z * jax.nn.sigmoid(z)


def _silu_grad(z):
    s = jax.nn.sigmoid(z)
    return s * (1.0 + z * (1.0 - s))


def _dot_nt(a, b):
    return lax.dot_general(a, b, (((1,), (1,)), ((), ())), preferred_element_type=F32)


def _dot_tn(a, b):
    return lax.dot_general(a, b, (((0,), (0,)), ((), ())), preferred_element_type=F32)


def _dot(a, b):
    return jnp.dot(a, b, preferred_element_type=F32)


A_SCALE = A_HEAD_DIM ** -0.5
LOG2E = 1.4426950408889634
A_C2 = A_SCALE * LOG2E
A_HEAD_GROUP = 4
A_FWD_HEAD_GROUP = 8
A_BLOCKS_PER_STEP = 4
A_SEQS_PER_STEP = 4


def _a_steps(nb):
    return (min(nb, A_BLOCKS_PER_STEP), 1) if nb > 1 else (1, A_SEQS_PER_STEP)


def _attn_a_fwd(qk, v, name):
    s_n, ln, _ = qk.shape
    nb = ln // BAND
    lb, sb = _a_steps(nb)
    single = nb == 1
    rows = lb * BAND
    n_keys = BAND if single else 2 * BAND

    def body(q_ref, kc_ref, kp_ref, vc_ref, vp_ref, o_ref, lse_ref):
        kpos = lax.broadcasted_iota(jnp.int32, (n_keys, BAND), 0)
        qpos = lax.broadcasted_iota(jnp.int32, (n_keys, BAND), 1) + (n_keys - BAND)
        band = (kpos <= qpos) & (kpos >= qpos - BAND)
        at_start = band & (kpos >= jnp.where(pl.program_id(1) > 0, 0, n_keys - BAND))
        for s in range(sb):
            for u in range(lb):
                r = slice(u * BAND, (u + 1) * BAND)
                mask = at_start if u == 0 else band

                def keys(prev_ref, cur_ref, hs):
                    cur = cur_ref[s, r, hs]
                    if single:
                        return cur
                    prev = prev_ref[s, :, hs] if u == 0 else cur_ref[s, (u - 1) * BAND:u * BAND, hs]
                    return jnp.concatenate([prev, cur], axis=0)

                stats = []
                for h0 in range(0, A_HEADS, A_FWD_HEAD_GROUP):
                    hss = [slice(h * A_HEAD_DIM, (h + 1) * A_HEAD_DIM) for h in range(h0, h0 + A_FWD_HEAD_GROUP)]
                    sts = [_dot_nt(keys(kp_ref, kc_ref, hs), q_ref[s, r, hs]) for hs in hss]
                    ps, ls = [], []
                    for st in sts:
                        st = jnp.where(mask, st * A_C2, NEG)
                        m = jnp.max(st, axis=0, keepdims=True)
                        p = jnp.exp2(st - m)
                        l_row = jnp.sum(p, axis=0, keepdims=True)
                        ps.append(p.astype(BF16))
                        ls.append(l_row)
                        stats.append(m + jnp.log2(l_row))
                    ots = [_dot_tn(keys(vp_ref, vc_ref, hs), p) for hs, p in zip(hss, ps)]
                    for hs, o_t, l_row in zip(hss, ots, ls):
                        o_ref[s, r, hs] = (o_t / l_row).T.astype(BF16)
                lse_ref[s, :, r] = jnp.concatenate(stats, axis=0)

    def cur(c):
        return pl.BlockSpec((sb, rows, A_WIDTH), lambda s, l: (s, l, c))

    def prev(c):
        return pl.BlockSpec((sb, BAND, A_WIDTH), lambda s, l: (s, jnp.maximum(l * lb - 1, 0), c))

    return pl.pallas_call(
        body, name=name, grid=(s_n // sb, nb // lb),
        in_specs=[cur(0), cur(1), prev(1), cur(0), prev(0)],
        out_specs=[cur(0), pl.BlockSpec((sb, A_HEADS, rows), lambda s, l: (s, 0, l))],
        out_shape=[jax.ShapeDtypeStruct((s_n, ln, A_WIDTH), BF16), jax.ShapeDtypeStruct((s_n, A_HEADS, ln), F32)],
        compiler_params=_cparams(("parallel", "arbitrary")))(qk, qk, qk, v, v)


def _attn_a_bwd(qk, v, do, lse2, dsum, tabs, name):
    s_n, ln, _ = qk.shape
    nb = ln // BAND
    lb, sb = _a_steps(nb)
    single = nb == 1
    rows = lb * BAND
    n_steps = nb // lb

    def body(q_ref, qn_ref, kc_ref, kp_ref, vc_ref, vp_ref, do_ref, don_ref, lse_ref, lsen_ref, ds_ref, dsn_ref,
             c_ref, sp_ref, sm_ref, out_ref):
        l_idx = pl.program_id(1)
        kpos = lax.broadcasted_iota(jnp.int32, (2 * BAND, BAND), 0)
        qpos = lax.broadcasted_iota(jnp.int32, (2 * BAND, BAND), 1) + BAND
        band_q = (kpos <= qpos) & (kpos >= qpos - BAND)
        start_q = band_q & (kpos >= jnp.where(l_idx > 0, 0, BAND))
        kpos2 = lax.broadcasted_iota(jnp.int32, (BAND, 2 * BAND), 0)
        qpos2 = lax.broadcasted_iota(jnp.int32, (BAND, 2 * BAND), 1)
        band_k = (kpos2 <= qpos2) & (kpos2 >= qpos2 - BAND)
        end_k = band_k & (qpos2 < jnp.where(l_idx < n_steps - 1, 2 * BAND, BAND))
        causal = kpos[:BAND] <= qpos[:BAND] - BAND
        for s in range(sb):
            for u in range(lb):
                r = slice(u * BAND, (u + 1) * BAND)
                c, sp, sm = c_ref[s, r, :], sp_ref[s, r, :], sm_ref[s, r, :]
                lse_q, ds_q = lse_ref[s, :, r], ds_ref[s, :, r]

                def store(hl, hss, dqs, dks, dvs):
                    for i, h in enumerate(hl):
                        out_ref[s, r, hss[i]] = _rope_t(dqs[i], c, sp, sm).astype(BF16)
                        out_ref[s, r, A_WIDTH + h * A_HEAD_DIM:A_WIDTH + (h + 1) * A_HEAD_DIM] = \
                            _rope_t(dks[i], c, sp, sm).astype(BF16)
                        out_ref[s, r, 2 * A_WIDTH + h * A_HEAD_DIM:2 * A_WIDTH + (h + 1) * A_HEAD_DIM] = \
                            dvs[i].astype(BF16)

                if single:
                    for h0 in range(0, A_HEADS, A_HEAD_GROUP):
                        hl = list(range(h0, h0 + A_HEAD_GROUP))
                        hss = [slice(h * A_HEAD_DIM, (h + 1) * A_HEAD_DIM) for h in hl]
                        sts = [_dot_nt(kc_ref[s, r, hs], q_ref[s, r, hs]) for hs in hss]
                        dpts = [_dot_nt(vc_ref[s, r, hs], do_ref[s, r, hs]) for hs in hss]
                        ps, dsts = [], []
                        for i, h in enumerate(hl):
                            p = jnp.exp2(jnp.where(causal, sts[i] * A_C2, NEG) - lse_q[h:h + 1])
                            dsts.append((p * (dpts[i] - ds_q[h:h + 1]) * A_SCALE).astype(BF16))
                            ps.append(p.astype(BF16))
                        store(hl, hss, [_dot_tn(dst, kc_ref[s, r, hs]) for dst, hs in zip(dsts, hss)],
                              [_dot(dst, q_ref[s, r, hs]) for dst, hs in zip(dsts, hss)],
                              [_dot(p, do_ref[s, r, hs]) for p, hs in zip(ps, hss)])
                    continue

                r_prev = slice((u - 1) * BAND, u * BAND)
                r_next = slice((u + 1) * BAND, (u + 2) * BAND)

                def before(prev_ref, cur_ref, hs):
                    return prev_ref[s, :, hs] if u == 0 else cur_ref[s, r_prev, hs]

                def after(next_ref, cur_ref, hs):
                    return next_ref[s, :, hs] if u == lb - 1 else cur_ref[s, r_next, hs]

                mask_q = start_q if u == 0 else band_q
                mask_k = end_k if u == lb - 1 else band_k
                lse_n = lsen_ref[s] if u == lb - 1 else lse_ref[s, :, r_next]
                ds_n = dsn_ref[s] if u == lb - 1 else ds_ref[s, :, r_next]
                lse_k = jnp.concatenate([lse_q, lse_n], axis=1)
                ds_k = jnp.concatenate([ds_q, ds_n], axis=1)
                for h0 in range(0, A_HEADS, A_HEAD_GROUP):
                    hl = list(range(h0, h0 + A_HEAD_GROUP))
                    hss = [slice(h * A_HEAD_DIM, (h + 1) * A_HEAD_DIM) for h in hl]
                    k2s = [jnp.concatenate([before(kp_ref, kc_ref, hs), kc_ref[s, r, hs]], axis=0) for hs in hss]
                    v2s = [jnp.concatenate([before(vp_ref, vc_ref, hs), vc_ref[s, r, hs]], axis=0) for hs in hss]
                    q2s = [jnp.concatenate([q_ref[s, r, hs], after(qn_ref, q_ref, hs)], axis=0) for hs in hss]
                    do2s = [jnp.concatenate([do_ref[s, r, hs], after(don_ref, do_ref, hs)], axis=0) for hs in hss]
                    sts = [_dot_nt(k2, q_ref[s, r, hs]) for k2, hs in zip(k2s, hss)]
                    dpts = [_dot_nt(v2, do_ref[s, r, hs]) for v2, hs in zip(v2s, hss)]
                    st2s = [_dot_nt(kc_ref[s, r, hs], q2) for q2, hs in zip(q2s, hss)]
                    dpt2s = [_dot_nt(vc_ref[s, r, hs], do2) for do2, hs in zip(do2s, hss)]
                    dsts, dst2s, p2s = [], [], []
                    for i, h in enumerate(hl):
                        p = jnp.exp2(jnp.where(mask_q, sts[i] * A_C2, NEG) - lse_q[h:h + 1])
                        dsts.append((p * (dpts[i] - ds_q[h:h + 1]) * A_SCALE).astype(BF16))
                        p2 = jnp.exp2(jnp.where(mask_k, st2s[i] * A_C2, NEG) - lse_k[h:h + 1])
                        dst2s.append((p2 * (dpt2s[i] - ds_k[h:h + 1]) * A_SCALE).astype(BF16))
                        p2s.append(p2.astype(BF16))
                    store(hl, hss, [_dot_tn(dsts[i], k2s[i]) for i in range(A_HEAD_GROUP)],
                          [_dot(dst2s[i], q2s[i]) for i in range(A_HEAD_GROUP)],
                          [_dot(p2s[i], do2s[i]) for i in range(A_HEAD_GROUP)])

    def cur(c, width=A_WIDTH):
        return pl.BlockSpec((sb, rows, width), lambda s, l: (s, l, c))

    def prev(c):
        return pl.BlockSpec((sb, BAND, A_WIDTH), lambda s, l: (s, jnp.maximum(l * lb - 1, 0), c))

    def nxt(c):
        return pl.BlockSpec((sb, BAND, A_WIDTH), lambda s, l: (s, jnp.minimum((l + 1) * lb, nb - 1), c))

    stat = pl.BlockSpec((sb, A_HEADS, rows), lambda s, l: (s, 0, l))
    stat_next = pl.BlockSpec((sb, A_HEADS, BAND), lambda s, l: (s, 0, jnp.minimum((l + 1) * lb, nb - 1)))
    tspec = cur(0, LANES)
    in_specs = [cur(0), nxt(0), cur(1), prev(1), cur(0), prev(0), cur(0), nxt(0),
                stat, stat_next, stat, stat_next, tspec, tspec, tspec]
    return pl.pallas_call(
        body, name=name, grid=(s_n // sb, n_steps), in_specs=in_specs,
        out_specs=cur(0, 3 * A_WIDTH),
        out_shape=jax.ShapeDtypeStruct((s_n, ln, 3 * A_WIDTH), BF16),
        compiler_params=_cparams(("parallel", "arbitrary")))(
            qk, qk, qk, qk, v, v, do, do, lse2, lse2, dsum, dsum, *tabs)


B_TQ = 256
B_FWD_HEADS = 4
B_BWD_PAIRS = 2
B_SCALE = B_QK_DIM ** -0.5
B_C2 = B_SCALE * LOG2E


def _key_le_query(kb, qb, tk, tq):
    kpos = kb * tk + lax.broadcasted_iota(jnp.int32, (tk, tq), 0)
    qpos = qb * tq + lax.broadcasted_iota(jnp.int32, (tk, tq), 1)
    return kpos <= qpos


def _attn_b_fwd(q, kpad, vt1):
    bsz, t, _ = q.shape
    tq = tk = B_TQ
    nq = t // tq
    nh = B_FWD_HEADS

    def body(q_ref, k_ref, vt_ref, o_ref, lse_ref):
        qblk = pl.program_id(2)
        qs = [q_ref[:, hh * LANES:(hh + 1) * LANES] for hh in range(nh)]

        def scores(kb):
            start = pl.multiple_of(kb * tk, tk)
            return [_dot_nt(k_ref[pl.ds(start, tk), hh * LANES:(hh + 1) * LANES], qs[hh]) for hh in range(nh)]

        def pv(kb, ps):
            start = pl.multiple_of(kb * tk, tk)
            return [_dot(vt_ref[hh * LANES:(hh + 1) * LANES, pl.ds(start, tk)], ps[hh]) for hh in range(nh)]

        def softmax(ss, ms, accs, kb, masked):
            out_m, out_acc, out_p = [], [], []
            for hh in range(nh):
                s = ss[hh] * B_C2
                if masked:
                    s = jnp.where(_key_le_query(kb, qblk, tk, tq), s, NEG)
                m_new = jnp.maximum(ms[hh], jnp.max(s, axis=0, keepdims=True))
                out_acc.append(jnp.exp2(ms[hh] - m_new) * accs[hh])
                out_p.append(jnp.exp2(s - m_new).astype(BF16))
                out_m.append(m_new)
            return out_m, out_acc, out_p

        def step(kb, carry):
            ss, ps, ms, accs = carry
            pvs = pv(jnp.maximum(kb - 1, 0), ps)
            ss_next = scores(kb + 1)
            accs = [accs[hh] + pvs[hh] for hh in range(nh)]
            ms, accs, ps = softmax(ss, ms, accs, kb, False)
            return (ss_next, ps, ms, accs)

        init = (scores(0), [jnp.zeros((tk, tq), BF16)] * nh, [jnp.full((1, tq), NEG, F32)] * nh,
                [jnp.zeros((LANES, tq), F32)] * nh)
        ss, ps, ms, accs = lax.fori_loop(0, qblk, step, init)
        pvs = pv(jnp.maximum(qblk - 1, 0), ps)
        accs = [accs[hh] + pvs[hh] for hh in range(nh)]
        ms, accs, ps = softmax(ss, ms, accs, qblk, True)
        pvs = pv(qblk, ps)
        accs = [accs[hh] + pvs[hh] for hh in range(nh)]
        ls = [accs[hh][B_VDIM:B_VDIM + 1] for hh in range(nh)]
        o_t = jnp.concatenate([accs[hh][:B_VDIM] / ls[hh] for hh in range(nh)], axis=0)
        o_ref[...] = o_t.T
        for pair in range(nh // 2):
            lse_ref[pair] = jnp.concatenate([ms[2 * pair + hh] + jnp.log2(ls[2 * pair + hh]) for hh in range(2)]
                                            + [jnp.zeros((6, tq), F32)], axis=0)

    return pl.pallas_call(
        body, name="attn_b_fwd", grid=(bsz, B_HEADS // nh, nq),
        in_specs=[pl.BlockSpec((None, tq, nh * LANES), lambda b, j, i: (b, i, j)),
                  pl.BlockSpec((None, t, nh * LANES), lambda b, j, i: (b, 0, j)),
                  pl.BlockSpec((None, nh * LANES, t), lambda b, j, i: (b, j, 0))],
        out_specs=[pl.BlockSpec((None, tq, nh * B_VDIM), lambda b, j, i: (b, i, j)),
                   pl.BlockSpec((None, nh // 2, 8, tq), lambda b, j, i: (b, j, 0, i))],
        out_shape=[jax.ShapeDtypeStruct((bsz, t, B_HEADS * B_VDIM), F32),
                   jax.ShapeDtypeStruct((bsz, B_HEADS // 2, 8, t), F32)],
        compiler_params=_cparams(("parallel", "parallel", "arbitrary")))(q, kpad, vt1)


def _attn_b_bwd(q, kpad, v, do, o, lse2, tabs):
    bsz, t, _ = q.shape
    tq = tk = B_TQ
    nq = t // tq
    n_pairs = B_BWD_PAIRS
    n_heads = 2 * n_pairs

    def body(q_ref, k_ref, v_ref, do_ref, o_ref, lse_ref, c_ref, sp_ref, sm_ref, dq_ref, dk_ref, dv_ref,
             dqt_scr, dsum_scr):
        lane = lax.broadcasted_iota(jnp.int32, (tk, LANES), 1)
        sel_lane = lax.broadcasted_iota(jnp.int32, (8, LANES), 1)
        sel_row = lax.broadcasted_iota(jnp.int32, (8, LANES), 0)
        sel = jnp.where((sel_lane < B_VDIM) == (sel_row == 0), 1.0, 0.0)
        sel = jnp.where(sel_row < 2, sel, 0.0).astype(BF16)

        def rows(blk):
            return pl.ds(pl.multiple_of(blk * tq, tq), tq)

        def dsum_step(qb, carry):
            for pp in range(n_pairs):
                pls = slice(pp * LANES, (pp + 1) * LANES)
                prod = do_ref[rows(qb), pls].astype(F32) * o_ref[rows(qb), pls]
                hi = prod.astype(BF16)
                lo = (prod - hi.astype(F32)).astype(BF16)
                dsum_scr[pp, :, rows(qb)] = _dot_nt(sel, hi) + _dot_nt(sel, lo)
            return carry

        lax.fori_loop(0, nq, dsum_step, 0)
        dqt_scr[...] = jnp.zeros_like(dqt_scr)

        def kv_step(kb, carry):
            ks = [k_ref[rows(kb), hh * LANES:(hh + 1) * LANES] for hh in range(n_heads)]
            vs = []
            for pp in range(n_pairs):
                vb = v_ref[rows(kb), pp * LANES:(pp + 1) * LANES]
                zero = jnp.zeros_like(vb)
                vs += [jnp.where(lane < B_VDIM, vb, zero), jnp.where(lane < B_VDIM, zero, vb)]

            def make_step(masked):
                def step(qb, acc):
                    qs = [q_ref[rows(qb), hh * LANES:(hh + 1) * LANES] for hh in range(n_heads)]
                    dob = [do_ref[rows(qb), pp * LANES:(pp + 1) * LANES] for pp in range(n_pairs)]
                    ss = [_dot_nt(ks[hh], qs[hh]) for hh in range(n_heads)]
                    dps = [_dot_nt(vs[hh], dob[hh // 2]) for hh in range(n_heads)]
                    pbs, dss = [], []
                    for hh in range(n_heads):
                        stat = (hh // 2, slice(hh % 2, hh % 2 + 1), rows(qb))
                        s = ss[hh] * B_C2
                        if masked:
                            s = jnp.where(_key_le_query(kb, qb, tk, tq), s, NEG)
                        p = jnp.exp2(s - lse_ref[stat])
                        dss.append((p * (dps[hh] - dsum_scr[stat]) * B_SCALE).astype(BF16))
                        pbs.append(p.astype(BF16))
                    for hh in range(n_heads):
                        dqt_scr[hh, :, rows(qb)] += _dot_tn(ks[hh], dss[hh])
                    return tuple([acc[hh] + _dot(dss[hh], qs[hh]) for hh in range(n_heads)]
                                 + [acc[n_heads + hh] + _dot(pbs[hh], dob[hh // 2]) for hh in range(n_heads)])
                return step

            acc = make_step(True)(kb, (jnp.zeros((tk, LANES), F32),) * (2 * n_heads))
            acc = lax.fori_loop(kb + 1, nq, make_step(False), acc)
            for hh in range(n_heads):
                dk_ref[rows(kb), hh * LANES:(hh + 1) * LANES] = acc[hh].astype(BF16)
            for pp in range(n_pairs):
                dv_pair = jnp.where(lane < B_VDIM, acc[n_heads + 2 * pp], acc[n_heads + 2 * pp + 1])
                dv_ref[rows(kb), pp * LANES:(pp + 1) * LANES] = dv_pair.astype(BF16)
            return carry

        lax.fori_loop(0, nq, kv_step, 0)

        def dq_step(qb, carry):
            c, sp, sm = c_ref[rows(qb), :], sp_ref[rows(qb), :], sm_ref[rows(qb), :]
            for hh in range(n_heads):
                dq_ref[rows(qb), hh * LANES:(hh + 1) * LANES] = _rope_t(dqt_scr[hh, :, rows(qb)].T, c, sp, sm).astype(BF16)
            return carry

        lax.fori_loop(0, nq, dq_step, 0)

    pair_full = pl.BlockSpec((None, t, n_heads * LANES), lambda b, j: (b, 0, j))
    one_full = pl.BlockSpec((None, t, n_pairs * LANES), lambda b, j: (b, 0, j))
    row_full = pl.BlockSpec((None, n_pairs, 8, t), lambda b, j: (b, j, 0, 0))
    tab_full = pl.BlockSpec((None, t, LANES), lambda b, j: (b, 0, 0))
    return pl.pallas_call(
        body, name="attn_b_bwd", grid=(bsz, B_HEADS // n_heads),
        in_specs=[pair_full, pair_full, one_full, one_full, one_full, row_full, tab_full, tab_full, tab_full],
        out_specs=[pair_full, pair_full, one_full],
        out_shape=[jax.ShapeDtypeStruct((bsz, t, B_KPAD), BF16), jax.ShapeDtypeStruct((bsz, t, B_KPAD), BF16),
                   jax.ShapeDtypeStruct((bsz, t, B_HEADS * B_VDIM), BF16)],
        scratch_shapes=[pltpu.VMEM((n_heads, LANES, t), F32), pltpu.VMEM((n_pairs, 8, t), F32)],
        compiler_params=_cparams(("parallel", "parallel")))(q, kpad, v, do, o, lse2, *tabs)


ANY = pl.BlockSpec(memory_space=pl.ANY)


def _all_gather(shard, name):
    def body(x_ref, out_ref, send_sems, recv_sems, local_sem):
        x, y, c = lax.axis_index("x"), lax.axis_index("y"), lax.axis_index("c")
        me, sibling = (x, y, c), (x, y, 1 - c)
        chips = [(1 - x, y), (x, 1 - y), (1 - x, 1 - y)]

        def rows(px, py, pc):
            return out_ref.at[4 * px + 2 * py + pc]

        def copy(k, block, to, src=None):
            return pltpu.make_async_remote_copy(
                src_ref=rows(*block) if src is None else src, dst_ref=rows(*block),
                send_sem=send_sems.at[k], recv_sem=recv_sems.at[k], device_id=to, device_id_type=MESH)

        mine = pltpu.make_async_copy(x_ref, rows(*me), local_sem)
        mine.start()
        first = [copy(0, me, sibling, src=x_ref)]
        first += [copy(1 + j, me, (*chip, c), src=x_ref) for j, chip in enumerate(chips)]
        for cp in first:
            cp.start()
        passed = [copy(4 + j, (*chip, c), sibling) for j, chip in enumerate(chips)]
        for j, chip in enumerate(chips):
            copy(1 + j, (*chip, c), me).wait_recv()
            passed[j].start()
        copy(0, sibling, me).wait_recv()
        for j, chip in enumerate(chips):
            copy(4 + j, (*chip, 1 - c), me).wait_recv()
        for cp in first + passed:
            cp.wait_send()
        mine.wait()

    return pl.pallas_call(
        body, name=name, in_specs=[ANY], out_specs=ANY,
        out_shape=jax.ShapeDtypeStruct((N_DEV,) + shard.shape, shard.dtype),
        scratch_shapes=[pltpu.SemaphoreType.DMA((7,)), pltpu.SemaphoreType.DMA((7,)), pltpu.SemaphoreType.DMA])(shard)


def _all_gather_weights(flat, wide, name):
    ns = wide.shape[1]

    def body(f_ref, w_ref, fo_ref, wo_ref, send_sems, recv_sems, local_sems):
        x, y, c = lax.axis_index("x"), lax.axis_index("y"), lax.axis_index("c")
        me, sibling = (x, y, c), (x, y, 1 - c)
        chips = [(1 - x, y), (x, 1 - y), (1 - x, 1 - y)]

        def place(a, px, py, pc):
            idx = 4 * px + 2 * py + pc
            if a == 0:
                return fo_ref.at[idx]
            return wo_ref.at[:, pl.ds(pl.multiple_of(idx * ns, LANES), ns)]

        def copy(a, k, block, to, src=None):
            return pltpu.make_async_remote_copy(
                src_ref=place(a, *block) if src is None else src, dst_ref=place(a, *block),
                send_sem=send_sems.at[a, k], recv_sem=recv_sems.at[a, k], device_id=to, device_id_type=MESH)

        own = (f_ref, w_ref)
        mine = [pltpu.make_async_copy(own[a], place(a, *me), local_sems.at[a]) for a in range(2)]
        first = []
        for a in range(2):
            mine[a].start()
            first.append(copy(a, 0, me, sibling, src=own[a]))
            first += [copy(a, 1 + j, me, (*chip, c), src=own[a]) for j, chip in enumerate(chips)]
        for cp in first:
            cp.start()
        passed = [[copy(a, 4 + j, (*chip, c), sibling) for j, chip in enumerate(chips)] for a in range(2)]
        for a in range(2):
            for j, chip in enumerate(chips):
                copy(a, 1 + j, (*chip, c), me).wait_recv()
                passed[a][j].start()
        for a in range(2):
            copy(a, 0, sibling, me).wait_recv()
            for j, chip in enumerate(chips):
                copy(a, 4 + j, (*chip, 1 - c), me).wait_recv()
        for cp in first + passed[0] + passed[1]:
            cp.wait_send()
        for cp in mine:
            cp.wait()

    return pl.pallas_call(
        body, name=name, in_specs=[ANY, ANY], out_specs=[ANY, ANY],
        out_shape=[jax.ShapeDtypeStruct((N_DEV,) + flat.shape, flat.dtype),
                   jax.ShapeDtypeStruct((wide.shape[0], N_DEV * ns), wide.dtype)],
        scratch_shapes=[pltpu.SemaphoreType.DMA((2, 7)), pltpu.SemaphoreType.DMA((2, 7)),
                        pltpu.SemaphoreType.DMA((2,))])(flat, wide)


N_CHIPS = 4


def _exchange_sibling(blocks, name):
    def body(g_ref, got_ref, send_sems, recv_sems):
        x, y, c = lax.axis_index("x"), lax.axis_index("y"), lax.axis_index("c")
        sends = [pltpu.make_async_remote_copy(
            src_ref=g_ref.at[q, 1 - c], dst_ref=got_ref.at[q], send_sem=send_sems.at[q],
            recv_sem=recv_sems.at[q], device_id=(x, y, 1 - c), device_id_type=MESH) for q in range(N_CHIPS)]
        for cp in sends:
            cp.start()
        for cp in sends:
            cp.wait_recv()
        for cp in sends:
            cp.wait_send()

    return pl.pallas_call(
        body, name=name, in_specs=[ANY], out_specs=ANY,
        out_shape=jax.ShapeDtypeStruct((N_CHIPS,) + blocks.shape[2:], blocks.dtype),
        scratch_shapes=[pltpu.SemaphoreType.DMA((N_CHIPS,)), pltpu.SemaphoreType.DMA((N_CHIPS,))])(blocks)


def _exchange_chips(parts, name):
    def body(p_ref, out_ref, send_sems, recv_sems, local_sem):
        x, y, c = lax.axis_index("x"), lax.axis_index("y"), lax.axis_index("c")
        me = 2 * x + y

        def peer(k):
            return (1 - x if k & 2 else x, 1 - y if k & 1 else y)

        def copy(k):
            px, py = peer(k)
            return pltpu.make_async_remote_copy(
                src_ref=p_ref.at[2 * px + py], dst_ref=out_ref.at[me], send_sem=send_sems.at[k - 1],
                recv_sem=recv_sems.at[k - 1], device_id=(px, py, c), device_id_type=MESH)

        def arrival(k):
            px, py = peer(k)
            slot = out_ref.at[2 * px + py]
            return pltpu.make_async_remote_copy(
                src_ref=slot, dst_ref=slot, send_sem=send_sems.at[k - 1], recv_sem=recv_sems.at[k - 1],
                device_id=(px, py, c), device_id_type=MESH)

        mine = pltpu.make_async_copy(p_ref.at[me], out_ref.at[me], local_sem)
        mine.start()
        sends = [copy(k) for k in range(1, N_CHIPS)]
        for cp in sends:
            cp.start()
        for k in range(1, N_CHIPS):
            arrival(k).wait_recv()
        for cp in sends:
            cp.wait_send()
        mine.wait()

    return pl.pallas_call(
        body, name=name, in_specs=[ANY], out_specs=ANY,
        out_shape=jax.ShapeDtypeStruct(parts.shape, parts.dtype),
        scratch_shapes=[pltpu.SemaphoreType.DMA((N_CHIPS - 1,)), pltpu.SemaphoreType.DMA((N_CHIPS - 1,)),
                        pltpu.SemaphoreType.DMA])(parts)


def _add_pairs(blocks, got, core, *, tr, name):
    q, r, c = got.shape

    def body(core_ref, a_ref, b_ref, o_ref):
        o_ref[...] = (a_ref[...].astype(F32) + b_ref[...].astype(F32)).astype(o_ref.dtype)

    spec = pl.BlockSpec((q, tr, c), lambda i, core_ref: (0, i, 0))
    mine = pl.BlockSpec((q, None, tr, c), lambda i, core_ref: (0, core_ref[0], i, 0))
    return pl.pallas_call(
        body, name=name,
        grid_spec=pltpu.PrefetchScalarGridSpec(num_scalar_prefetch=1, grid=(r // tr,), in_specs=[mine, spec],
                                               out_specs=spec),
        out_shape=jax.ShapeDtypeStruct(got.shape, BF16), compiler_params=_cparams(("parallel",)))(
            core, blocks, got)


def _sum_slots(slots, *, tr, name):
    n_slots, r, c = slots.shape

    def body(s_ref, o_ref):
        acc = s_ref[0].astype(F32)
        for s in range(1, n_slots):
            acc = acc + s_ref[s].astype(F32)
        o_ref[...] = acc

    return pl.pallas_call(
        body, name=name, grid=(r // tr,),
        in_specs=[pl.BlockSpec((n_slots, tr, c), lambda i: (0, i, 0))],
        out_specs=pl.BlockSpec((tr, c), lambda i: (i, 0)),
        out_shape=jax.ShapeDtypeStruct((r, c), F32),
        compiler_params=_cparams(("parallel",)))(slots)


def _adamw_math(w_t, g_t, m_t, v_t):
    m_n = ADAM_B1 * m_t + (1.0 - ADAM_B1) * g_t
    v_n = ADAM_B2 * v_t + (1.0 - ADAM_B2) * (g_t * g_t)
    m_hat = m_n / (1.0 - ADAM_B1 ** ADAM_STEP)
    v_hat = v_n / (1.0 - ADAM_B2 ** ADAM_STEP)
    delta = -ADAM_LR * (m_hat / (jnp.sqrt(v_hat) + ADAM_EPS) + ADAM_WD * w_t)
    return delta, m_n, v_n


def _adamw(w, g, m, v, name):
    shape = w.shape
    cols = shape[-1]
    args = [a.reshape(-1, cols) for a in (w, g, m, v)]
    rows = args[0].shape[0]
    tm = 256 if rows % 256 == 0 else rows
    delta, m_n, v_n = _rowwise(_adamw_math, args, [], [(cols, F32)] * 3, tm=tm, name=name)
    return delta.reshape(shape), m_n.reshape(shape), v_n.reshape(shape)


def _adamw_small(ws, gs, ms, vs, name):
    k = len(ws)
    args = [a.reshape(-1, a.shape[-1]) for group in (ws, gs, ms, vs) for a in group]

    def body(*refs):
        ins, outs = refs[:4 * k], refs[4 * k:]
        for i in range(k):
            res = _adamw_math(*[ins[j * k + i][...] for j in range(4)])
            for j in range(3):
                outs[j * k + i][...] = res[j]

    res = pl.pallas_call(
        body, name=name, out_shape=[jax.ShapeDtypeStruct(a.shape, F32) for a in args[:k]] * 3,
        compiler_params=pltpu.CompilerParams(vmem_limit_bytes=VMEM_LIMIT))(*args)
    return [[res[j * k + i].reshape(ws[i].shape) for i in range(k)] for j in range(3)]


def _local_step(x, positions, target, w):
    bsz, t, _ = x.shape
    n = bsz * t
    tm = 256
    mm = 512
    x2 = x.reshape(n, D_MODEL)
    tgt2 = target.reshape(n, D_MODEL)
    pos = positions.reshape(n)
    tb2 = _rope_tables(pos, B_ROPE_THETA, B_ROPE, B_NOPE, "rope_tables_b")
    tabs_b = [a.reshape(bsz, t, LANES) for a in tb2]

    w_a_in = w["a_w_in"]
    w_a_out = w["a_w_out"]
    w_down = w["kv_w_down"]
    w_down_p = jnp.zeros((D_MODEL, 3 * LANES), BF16).at[:, :B_KV_LORA].set(w_down[:, :B_KV_LORA])
    w_down_p = w_down_p.at[:, B_KV_LORA + B_NOPE:B_KV_LORA + B_QK_DIM].set(w_down[:, B_KV_LORA:])
    wu = w["kv_w_up"].reshape(B_KV_LORA, B_HEADS, B_NOPE + B_VDIM)
    w_upk = jnp.pad(wu[:, :, :B_NOPE], ((0, 0), (0, 0), (0, LANES - B_NOPE))).reshape(B_KV_LORA, B_KPAD)
    w_upv = wu[:, :, B_NOPE:].reshape(B_KV_LORA, B_HEADS * B_VDIM)
    w_b_in = w["b_w_in"]
    w_q_p = jnp.pad(w["b_w_q_up"].reshape(B_Q_LORA, B_HEADS, B_QK_DIM),
                    ((0, 0), (0, 0), (0, LANES - B_QK_DIM))).reshape(B_Q_LORA, B_KPAD)
    w_b_out = w["b_w_out"]

    def tab_extras(tabs2, rows):
        return [(a, (rows, LANES), lambda j, i, kk: (i, 0)) for a in tabs2]

    (hn_a,) = _rowwise(lambda xt, g: (_rms(xt, g),), [x2], [w["a_pre_norm"]], [(D_MODEL, BF16)],
                       tm=tm, name="a_pre_norm")

    def rope_epilogue(acc, o_ref, rows, c_ref, sp_ref, sm_ref):
        c, sp, sm = c_ref[rows, :], sp_ref[rows, :], sm_ref[rows, :]
        for h in range(acc.shape[1] // LANES):
            hs = slice(h * LANES, (h + 1) * LANES)
            o_ref[rows, hs] = _rope(acc[:, hs], c, sp, sm).astype(BF16)

    def to_group(a, d):
        if d == 1:
            return a
        return a.reshape(bsz, t // d, d, a.shape[-1]).transpose(0, 2, 1, 3).reshape(n, a.shape[-1])

    def from_group(a, d):
        if d == 1:
            return a
        return a.reshape(bsz, d, t // d, a.shape[-1]).transpose(0, 2, 1, 3).reshape(n, a.shape[-1])

    def rows_to_cols(r, d):
        return r.reshape(bsz, d, A_HEADS, t // d).transpose(0, 3, 1, 2).reshape(n, A_HEADS)

    def cols_to_rows(cc, d):
        return cc.reshape(bsz, t // d, d, A_HEADS).transpose(0, 2, 3, 1).reshape(bsz * d, A_HEADS, t // d)

    z_a = _matmul(hn_a, w_a_in, b_cols=(A_QKV, A_WIDTH), out_dtype=F32, tm=mm, tn=A_WIDTH, name="a_gate")
    hn_g, tabs_g, qk_g, v_g, o_g, lse_g = [], [], [], [], [], []
    for g, d in enumerate(A_DILATIONS):
        hn_g.append(to_group(hn_a, d))
        tabs_g.append(_rope_tables(to_group(pos.reshape(n, 1), d).reshape(n), A_ROPE_THETA, A_ROT_DIM, 0,
                                   f"rope_tables_a_g{g}"))
        col0 = g * 3 * A_WIDTH
        qk = _matmul(hn_g[g], w_a_in, b_cols=(col0, 2 * A_WIDTH), out_dtype=BF16, tm=mm, tn=A_WIDTH,
                     name=f"a_qk_g{g}", epilogue=rope_epilogue, epilogue_rows=EPILOGUE_ROWS,
                     extras=tab_extras(tabs_g[g], mm))
        v = _matmul(hn_g[g], w_a_in, b_cols=(col0 + 2 * A_WIDTH, A_WIDTH), out_dtype=BF16, tm=mm, tn=A_WIDTH,
                    name=f"a_v_g{g}")
        qk_g.append(qk.reshape(bsz * d, t // d, 2 * A_WIDTH))
        v_g.append(v.reshape(bsz * d, t // d, A_WIDTH))
        o, lse = _attn_a_fwd(qk_g[g], v_g[g], f"attn_a_fwd_g{g}")
        o_g.append(from_group(o.reshape(n, A_WIDTH), d))
        lse_g.append(rows_to_cols(lse, d))

    def merge_fn(o0, o1, o2, l0, l1, l2, z):
        lmax = jnp.maximum(jnp.maximum(l0, l1), l2)
        e0, e1, e2 = jnp.exp2(l0 - lmax), jnp.exp2(l1 - lmax), jnp.exp2(l2 - lmax)
        den = e0 + e1 + e2
        w0, w1, w2 = e0 / den, e1 / den, e2 / den
        parts = []
        for h in range(A_HEADS):
            hs = slice(h * A_HEAD_DIM, (h + 1) * A_HEAD_DIM)
            parts.append(w0[:, h:h + 1] * o0[:, hs] + w1[:, h:h + 1] * o1[:, hs] + w2[:, h:h + 1] * o2[:, hs])
        o = jnp.concatenate(parts, axis=1)
        return o * _silu(z), o, lmax + jnp.log2(den)

    y_a, o_a2, lse_a = _rowwise(merge_fn, [*o_g, *lse_g, z_a], [],
                                [(A_WIDTH, BF16), (A_WIDTH, F32), (A_HEADS, F32)], tm=tm, name="a_merge_gate")
    out_a = _matmul(y_a, w_a_out, out_dtype=F32, tm=mm, tn=D_MODEL, name="a_out")

    def mid_fn(xt, out, g_post, g_kv, g_b):
        h1 = xt + _rms(out, g_post)
        return h1, _rms(h1, g_kv), _rms(h1, g_b)

    h1, hn_kv, hn_b = _rowwise(mid_fn, [x2, out_a], [w["a_post_norm"], w["kv_norm"], w["b_pre_norm"]],
                               [(D_MODEL, F32), (D_MODEL, BF16), (D_MODEL, BF16)], tm=tm, name="a_post_norm")

    ckr = _matmul(hn_kv, w_down_p, out_dtype=F32, tm=mm, tn=3 * LANES, name="kv_down")

    def latent_fn(ck, c, sp, sm, g):
        return _rms(ck[:, :B_KV_LORA], g), _rope(ck[:, B_KV_LORA:], c, sp, sm)

    c_kv, k_rope = _rowwise(latent_fn, [ckr, *tb2], [w["kv_latent_norm"]], [(B_KV_LORA, BF16), (LANES, F32)],
                            tm=tm, name="kv_latent_norm")

    def kpad_epilogue(acc, o_ref, rows, kr_ref):
        kr = kr_ref[rows, :]
        for h in range(acc.shape[1] // LANES):
            hs = slice(h * LANES, (h + 1) * LANES)
            o_ref[rows, hs] = (acc[:, hs] + kr).astype(BF16)

    kpad = _matmul(c_kv, w_upk, out_dtype=BF16, tm=mm, tn=1024, name="kv_up_k", epilogue=kpad_epilogue,
                   epilogue_rows=EPILOGUE_ROWS,
                   extras=[(k_rope, (mm, LANES), lambda j, i, kk: (i, 0))])
    v_b = _matmul(c_kv, w_upv, out_dtype=BF16, tm=mm, tn=1024, name="kv_up_v")

    proj_b = _matmul(hn_b, w_b_in, out_dtype=F32, tm=mm, tn=w_b_in.shape[1], name="b_in")
    (c_q,) = _rowwise(lambda p, g: (_rms(p[:, :B_Q_LORA], g),), [proj_b], [w["b_q_norm"]], [(B_Q_LORA, BF16)],
                      tm=tm, name="b_q_norm")

    q_b = _matmul(c_q, w_q_p, out_dtype=BF16, tm=mm, tn=1024, name="b_q_up", epilogue=rope_epilogue,
                  epilogue_rows=EPILOGUE_ROWS,
                  extras=tab_extras(tb2, mm))
    q_b3, kpad3, v_b3 = q_b.reshape(bsz, t, B_KPAD), kpad.reshape(bsz, t, B_KPAD), v_b.reshape(bsz, t, -1)
    w_vt = jnp.pad(wu[:, :, B_NOPE:], ((0, 0), (0, 0), (0, LANES - B_VDIM))).reshape(B_KV_LORA, B_KPAD)
    vt1 = _value_heads_t(c_kv, w_vt, bsz, t)
    o_b, lse_b = _attn_b_fwd(q_b3, kpad3, vt1)
    o_b2 = o_b.reshape(n, -1)
    (y_b,) = _rowwise(lambda o, p: (o * _silu(p[:, B_Q_LORA:]),), [o_b2, proj_b], [], [(D_MODEL, BF16)],
                      tm=tm, name="b_gate_mul")
    out_b = _matmul(y_b, w_b_out, out_dtype=F32, tm=mm, tn=D_MODEL, name="b_out")

    def head_fn(h1t, out, tgt, g):
        e = h1t + _rms(out, g) - tgt
        loss_row = 0.5 * jnp.mean(e * e, axis=-1, keepdims=True)
        dh2 = e * (1.0 / D_MODEL)
        d_out, dg = _rms_bwd(out, g, dh2)
        return dh2, d_out, dg, jnp.broadcast_to(loss_row * (1.0 / LANES), (loss_row.shape[0], LANES))

    dh2, d_out_b, dg_b_post, loss_acc = _rowwise(
        head_fn, [h1, out_b, tgt2], [w["b_post_norm"]], [(D_MODEL, F32), (D_MODEL, BF16)], [D_MODEL, LANES],
        tm=tm, name="loss_head")

    dy_b = _matmul(d_out_b, w_b_out, tb=True, out_dtype=BF16, tm=mm, tn=D_MODEL, name="b_out_dx")
    gw_b_out = _matmul(y_b, d_out_b, ta=True, out_dtype=BF16, tm=mm, tn=D_MODEL, tk=2048, name="b_out_dw")

    def gate_b_bwd(dy, o, p):
        z = p[:, B_Q_LORA:]
        return dy * _silu(z), dy * o * _silu_grad(z)

    do_b, dz_b = _rowwise(gate_b_bwd, [dy_b, o_b2, proj_b], [], [(D_MODEL, BF16), (D_MODEL, F32)],
                          tm=tm, name="b_gate_bwd")
    do_b3 = do_b.reshape(bsz, t, -1)
    dq_b, dk_b, dv_b = _attn_b_bwd(q_b3, kpad3, v_b3, do_b3, o_b, lse_b, tabs_b)
    dq_b2, dk_b2, dv_b2 = dq_b.reshape(n, B_KPAD), dk_b.reshape(n, B_KPAD), dv_b.reshape(n, -1)

    dc_kv = _matmul(dk_b2, w_upk, tb=True, out_dtype=F32, tm=mm, tn=B_KV_LORA, name="kv_up_k_dx")
    dc_kv = _matmul(dv_b2, w_upv, tb=True, out_dtype=BF16, tm=mm, tn=B_KV_LORA, name="kv_up_v_dx",
                    epilogue=_add_epilogue, extras=[(dc_kv, (mm, B_KV_LORA), lambda j, i, kk: (i, j))])
    gw_upk = _matmul(c_kv, dk_b2, ta=True, out_dtype=BF16, tm=B_KV_LORA, tn=1024, tk=2048, name="kv_up_k_dw")
    gw_upv = _matmul(c_kv, dv_b2, ta=True, out_dtype=BF16, tm=B_KV_LORA, tn=1024, tk=2048, name="kv_up_v_dw")

    def latent_bwd(ck, dck, dk, c, sp, sm, g):
        d1, dg = _rms_bwd(ck[:, :B_KV_LORA], g, dck)
        ksum = dk[:, :LANES].astype(F32)
        for h in range(1, B_HEADS):
            ksum = ksum + dk[:, h * LANES:(h + 1) * LANES].astype(F32)
        lane = lax.broadcasted_iota(jnp.int32, ksum.shape, 1)
        ksum = jnp.where((lane >= B_NOPE) & (lane < B_QK_DIM), ksum, 0.0)
        return jnp.concatenate([d1, _rope_t(ksum, c, sp, sm)], axis=1), dg

    dckr, dg_latent = _rowwise(latent_bwd, [ckr, dc_kv, dk_b2, *tb2], [w["kv_latent_norm"]],
                               [(3 * LANES, BF16)], [B_KV_LORA], tm=tm, name="kv_latent_bwd")
    dhn_kv = _matmul(dckr, w_down_p, tb=True, out_dtype=BF16, tm=mm, tn=D_MODEL, name="kv_down_dx")
    gw_down_p = _matmul(hn_kv, dckr, ta=True, out_dtype=BF16, tm=mm, tn=3 * LANES, tk=2048, name="kv_down_dw")

    dc_q = _matmul(dq_b2, w_q_p, tb=True, out_dtype=BF16, tm=mm, tn=B_Q_LORA, name="b_q_up_dx")
    gw_q_p = _matmul(c_q, dq_b2, ta=True, out_dtype=BF16, tm=B_Q_LORA, tn=1024, tk=2048, name="b_q_up_dw")

    def q_norm_bwd(p, dcq, dz, g):
        d1, dg = _rms_bwd(p[:, :B_Q_LORA], g, dcq)
        return jnp.concatenate([d1, dz], axis=1), dg

    dproj_b, dg_q_norm = _rowwise(q_norm_bwd, [proj_b, dc_q, dz_b], [w["b_q_norm"]],
                                  [(w_b_in.shape[1], BF16)], [B_Q_LORA], tm=tm, name="b_q_norm_bwd")
    dhn_b = _matmul(dproj_b, w_b_in, tb=True, out_dtype=BF16, tm=mm, tn=D_MODEL, name="b_in_dx")
    gw_b_in = _matmul(hn_b, dproj_b, ta=True, out_dtype=BF16, tm=mm, tn=w_b_in.shape[1], tk=2048, name="b_in_dw")

    def mid_bwd(h1t, dh2t, dkv, db, g_kv, g_b, g_post, out):
        dxa, ra = _rms_bwd(h1t, g_kv, dkv)
        dxb, rb = _rms_bwd(h1t, g_b, db)
        dh1 = dh2t + dxa + dxb
        d_out, rp = _rms_bwd(out, g_post, dh1)
        return dh1, d_out, ra, rb, rp

    def mid_bwd_fn(h1t, dh2t, dkv, db, out, g_kv, g_b, g_post):
        return mid_bwd(h1t, dh2t, dkv, db, g_kv, g_b, g_post, out)

    dh1, d_out_a, dg_kv, dg_b_pre, dg_a_post = _rowwise(
        mid_bwd_fn, [h1, dh2, dhn_kv, dhn_b, out_a], [w["kv_norm"], w["b_pre_norm"], w["a_post_norm"]],
        [(D_MODEL, F32), (D_MODEL, BF16)], [D_MODEL] * 3, tm=tm, name="mid_bwd")

    dy_a = _matmul(d_out_a, w_a_out, tb=True, out_dtype=BF16, tm=mm, tn=A_WIDTH, name="a_out_dx")
    gw_a_out = _matmul(y_a, d_out_a, ta=True, out_dtype=BF16, tm=mm, tn=D_MODEL, tk=2048, name="a_out_dw")

    def gate_a_bwd(dy, o, z):
        do = dy * _silu(z)
        prod = do * o
        lane = lax.broadcasted_iota(jnp.int32, (prod.shape[0], A_HEADS), 1)
        dsum = jnp.zeros((prod.shape[0], A_HEADS), F32)
        for h in range(A_HEADS):
            col = jnp.sum(prod[:, h * A_HEAD_DIM:(h + 1) * A_HEAD_DIM], axis=1, keepdims=True)
            dsum = jnp.where(lane == h, col, dsum)
        return do, dy * o * _silu_grad(z), dsum

    do_a, dz_a, dsum_a = _rowwise(gate_a_bwd, [dy_a, o_a2, z_a], [],
                                  [(A_WIDTH, BF16), (A_WIDTH, BF16), (A_HEADS, F32)], tm=tm, name="a_gate_bwd")
    dhn_a = _matmul(dz_a, w_a_in, b_cols=(A_QKV, A_WIDTH), tb=True, out_dtype=F32, tm=mm, tn=D_MODEL,
                    name="a_gate_dx")
    gw_a_in, dhn_groups = None, []
    for g, d in enumerate(A_DILATIONS):
        s_n, ln = bsz * d, t // d
        dqkv = _attn_a_bwd(qk_g[g], v_g[g], to_group(do_a, d).reshape(s_n, ln, A_WIDTH), cols_to_rows(lse_a, d),
                           cols_to_rows(dsum_a, d), [a.reshape(s_n, ln, LANES) for a in tabs_g[g]],
                           f"attn_a_bwd_g{g}").reshape(n, 3 * A_WIDTH)
        w_cols = (g * 3 * A_WIDTH, 3 * A_WIDTH)
        if d == 1:
            dhn_a = _matmul(dqkv, w_a_in, b_cols=w_cols, tb=True, out_dtype=F32, tm=mm, tn=D_MODEL,
                            name=f"a_qkv_dx_g{g}", epilogue=_add_epilogue,
                            extras=[(dhn_a, (mm, D_MODEL), lambda j, i, kk: (i, j))])
        else:
            dhn_groups.append(from_group(_matmul(dqkv, w_a_in, b_cols=w_cols, tb=True, out_dtype=BF16, tm=mm,
                                                 tn=D_MODEL, name=f"a_qkv_dx_g{g}"), d))
        gw_a_in = _matmul(hn_g[g], dqkv, ta=True, out_dtype=BF16, tm=mm, tn=1024, tk=2048, name=f"a_qkv_dw_g{g}",
                          out_into=(gw_a_in, g * 3 * A_WIDTH, A_QKV + A_WIDTH))
    gw_a_in = _matmul(hn_a, dz_a, ta=True, out_dtype=BF16, tm=mm, tn=1024, tk=2048, name="a_gate_dw",
                      out_into=(gw_a_in, A_QKV, A_QKV + A_WIDTH))

    def first_bwd(xt, dhn, dhn_1, dhn_2, dh1t, g):
        dx, dg = _rms_bwd(xt, g, dhn + dhn_1 + dhn_2)
        return dh1t + dx, dg

    grad_x, dg_a_pre = _rowwise(first_bwd, [x2, dhn_a, *dhn_groups, dh1], [w["a_pre_norm"]], [(D_MODEL, F32)],
                                [D_MODEL], tm=tm, name="a_pre_norm_bwd")

    gw_down = jnp.concatenate([gw_down_p[:, :B_KV_LORA], gw_down_p[:, B_KV_LORA + B_NOPE:B_KV_LORA + B_QK_DIM]], axis=1)
    gw_up = jnp.concatenate([gw_upk.reshape(B_KV_LORA, B_HEADS, LANES)[:, :, :B_NOPE],
                             gw_upv.reshape(B_KV_LORA, B_HEADS, B_VDIM)], axis=2).reshape(B_KV_LORA, -1)
    gw_q_up = gw_q_p.reshape(B_Q_LORA, B_HEADS, LANES)[:, :, :B_QK_DIM].reshape(B_Q_LORA, -1)
    grads = {"a_w_in": gw_a_in, "a_w_out": gw_a_out, "kv_w_down": gw_down, "kv_w_up": gw_up,
             "b_w_in": gw_b_in, "b_w_q_up": gw_q_up, "b_w_out": gw_b_out}
    gains = {"a_pre_norm": dg_a_pre, "a_post_norm": dg_a_post, "kv_norm": dg_kv, "kv_latent_norm": dg_latent,
             "b_pre_norm": dg_b_pre, "b_q_norm": dg_q_norm, "b_post_norm": dg_b_post}
    gains = {k: jnp.sum(a, axis=0) for k, a in gains.items()}
    return jnp.sum(loss_acc), grad_x.reshape(bsz, t, D_MODEL), grads, gains


WEIGHT_ORDER = ("a_pre_norm", "a_w_in", "a_w_out", "a_post_norm", "kv_norm", "kv_w_down", "kv_latent_norm",
                "kv_w_up", "b_pre_norm", "b_w_in", "b_q_norm", "b_w_q_up", "b_w_out", "b_post_norm")
MATRICES = (("a_w_in", 1024, 10240, 1), ("a_w_out", 1024, 1024, 0), ("kv_w_down", 1024, 288, 0),
            ("kv_w_up", 256, 2048, 1), ("b_w_in", 1024, 1408, 1), ("b_w_q_up", 384, 1536, 1),
            ("b_w_out", 1024, 1024, 0))
SHARDED_GAINS = ("a_pre_norm", "a_post_norm")
GAIN_WIDTHS = (("a_pre_norm", 1024), ("a_post_norm", 1024), ("kv_norm", 1024), ("kv_latent_norm", 256),
               ("b_pre_norm", 1024), ("b_q_norm", 384), ("b_post_norm", 1024))
GAIN_ROWS = 48


def _shard_rows(rows, cols):
    return rows * cols // (N_DEV * LANES)


def _whole_from_blocks(blocks, rows, cols, axis):
    if axis == 1:
        return blocks.reshape(N_DEV, rows, cols // N_DEV).transpose(1, 0, 2).reshape(rows, cols)
    return blocks.reshape(rows, cols)


def _blocks_from_whole(whole, rows, cols, axis):
    if axis == 1:
        whole = whole.reshape(rows, N_DEV, cols // N_DEV).transpose(1, 0, 2)
    return whole.reshape(N_DEV, -1, LANES)


def kernel(x, positions, a_pre_norm, a_w_in, a_w_out, a_post_norm, kv_norm, kv_w_down, kv_latent_norm, kv_w_up, b_pre_norm, b_w_in, b_q_norm, b_w_q_up, b_w_out, b_post_norm, loss_target, m_a_pre_norm, m_a_w_in, m_a_w_out, m_a_post_norm, m_kv_norm, m_kv_w_down, m_kv_latent_norm, m_kv_w_up, m_b_pre_norm, m_b_w_in, m_b_q_norm, m_b_w_q_up, m_b_w_out, m_b_post_norm, v_a_pre_norm, v_a_w_in, v_a_w_out, v_a_post_norm, v_kv_norm, v_kv_w_down, v_kv_latent_norm, v_kv_w_up, v_b_pre_norm, v_b_w_in, v_b_q_norm, v_b_w_q_up, v_b_w_out, v_b_post_norm):
    weights = dict(a_pre_norm=a_pre_norm, a_w_in=a_w_in, a_w_out=a_w_out, a_post_norm=a_post_norm, kv_norm=kv_norm,
                   kv_w_down=kv_w_down, kv_latent_norm=kv_latent_norm, kv_w_up=kv_w_up, b_pre_norm=b_pre_norm,
                   b_w_in=b_w_in, b_q_norm=b_q_norm, b_w_q_up=b_w_q_up, b_w_out=b_w_out, b_post_norm=b_post_norm)
    m_in = dict(a_pre_norm=m_a_pre_norm, a_w_in=m_a_w_in, a_w_out=m_a_w_out, a_post_norm=m_a_post_norm,
                kv_norm=m_kv_norm, kv_w_down=m_kv_w_down, kv_latent_norm=m_kv_latent_norm, kv_w_up=m_kv_w_up,
                b_pre_norm=m_b_pre_norm, b_w_in=m_b_w_in, b_q_norm=m_b_q_norm, b_w_q_up=m_b_w_q_up,
                b_w_out=m_b_w_out, b_post_norm=m_b_post_norm)
    v_in = dict(a_pre_norm=v_a_pre_norm, a_w_in=v_a_w_in, a_w_out=v_a_w_out, a_post_norm=v_a_post_norm,
                kv_norm=v_kv_norm, kv_w_down=v_kv_w_down, kv_latent_norm=v_kv_latent_norm, kv_w_up=v_kv_w_up,
                b_pre_norm=v_b_pre_norm, b_w_in=v_b_w_in, b_q_norm=v_b_q_norm, b_w_q_up=v_b_w_q_up,
                b_w_out=v_b_w_out, b_post_norm=v_b_post_norm)
    me = 4 * lax.axis_index("x") + 2 * lax.axis_index("y") + lax.axis_index("c")

    wide = MATRICES[0][0]
    flat = jnp.concatenate([weights[name].astype(BF16).reshape(-1, LANES) for name, _, _, _ in MATRICES[1:]], axis=0)
    gathered, w_wide = _all_gather_weights(flat, weights[wide][0].astype(BF16), "gather_weights")
    whole = {wide: w_wide}
    off = 0
    for name, rows, cols, axis in MATRICES[1:]:
        nr = _shard_rows(rows, cols)
        whole[name] = _whole_from_blocks(gathered[:, off:off + nr], rows, cols, axis)
        off += nr
    gain_shard = jnp.concatenate([weights[name].reshape(1, LANES) for name in SHARDED_GAINS]
                                 + [jnp.zeros((8 - len(SHARDED_GAINS), LANES), F32)], axis=0)
    gain_blocks = _all_gather(gain_shard, "gather_gains")
    for i, name in enumerate(SHARDED_GAINS):
        whole[name] = gain_blocks[:, i, :].reshape(1, D_MODEL)
    for name in ("kv_norm", "kv_latent_norm", "b_pre_norm", "b_q_norm", "b_post_norm"):
        whole[name] = weights[name].reshape(1, -1)

    loss_part, grad_x, grads, gains = _local_step(x, positions, loss_target, whole)

    blocks = jnp.concatenate([_blocks_from_whole(grads[name], rows, cols, axis).astype(BF16)
                              for name, rows, cols, axis in MATRICES], axis=1)
    blocks = blocks.reshape(N_CHIPS, 2, blocks.shape[1], LANES)
    got = _exchange_sibling(blocks, "scatter_grads_core")
    core = lax.axis_index("c").astype(jnp.int32).reshape(1)
    landed = _exchange_chips(_add_pairs(blocks, got, core, tr=2512, name="add_core_grads"), "scatter_grads_chip")
    summed = _sum_slots(landed, tr=2512, name="sum_grads")
    grad_out = {}
    off = 0
    for name, rows, cols, axis in MATRICES:
        nr = _shard_rows(rows, cols)
        grad_out[name] = summed[off:off + nr].reshape(weights[name].shape)
        off += nr

    vec = jnp.concatenate([gains[name] for name, _ in GAIN_WIDTHS] + [jnp.full((LANES,), loss_part, F32)])
    vec = jnp.pad(vec, (0, GAIN_ROWS * LANES - vec.shape[0])).reshape(GAIN_ROWS, LANES)
    total = _sum_slots(_all_gather(vec, "gather_gain_grads"), tr=GAIN_ROWS, name="sum_gain_grads").reshape(-1)
    off = 0
    for name, width in GAIN_WIDTHS:
        g = total[off:off + width]
        if name in SHARDED_GAINS:
            g = lax.dynamic_slice(g, (me * LANES,), (LANES,))
        grad_out[name] = g.reshape(weights[name].shape)
        off += width
    loss = total[off]

    deltas, new_m, new_v = {}, {}, {}
    big = "a_w_in"
    deltas[big], new_m[big], new_v[big] = _adamw(weights[big], grad_out[big], m_in[big], v_in[big], "adamw_" + big)
    small = [name for name in WEIGHT_ORDER if name != big]
    res = _adamw_small(*[[d[name] for name in small] for d in (weights, grad_out, m_in, v_in)], "adamw_small")
    for out, vals in zip((deltas, new_m, new_v), res):
        out.update(zip(small, vals))
    return (loss, grad_x, *[grad_out[k] for k in WEIGHT_ORDER], *[deltas[k] for k in WEIGHT_ORDER],
            *[new_m[k] for k in WEIGHT_ORDER], *[new_v[k] for k in WEIGHT_ORDER])
```

```python
import jax
import jax.numpy as jnp
from jax import lax
from jax.experimental import pallas as pl
from jax.experimental.pallas import tpu as pltpu

F32 = jnp.float32
BF16 = jnp.bfloat16

N_DEV = 8
D_MODEL = 1024
NORM_EPS = 1e-6
A_GROUPS = 3
A_DILATIONS = (1, 4, 16)
A_HEADS = 8
A_HEAD_DIM = 128
A_WIDTH = 1024
A_ROT_DIM = 32
A_ROPE_THETA = 500000.0
A_QKV = A_GROUPS * 3 * A_WIDTH
B_HEADS = 16
B_NOPE = 64
B_ROPE = 32
B_QK_DIM = 96
B_VDIM = 64
B_Q_LORA = 384
B_KV_LORA = 256
B_ROPE_THETA = 10000.0
B_KPAD = B_HEADS * 128
ADAM_LR = 0.001
ADAM_B1 = 0.9
ADAM_B2 = 0.999
ADAM_EPS = 1e-08
ADAM_WD = 0.01
ADAM_STEP = 10

LANES = 128
BAND = 128
EPILOGUE_ROWS = 128
NEG = -1e30
VMEM_LIMIT = 56 * 1024 * 1024
MESH = pl.DeviceIdType.MESH


def _cparams(sem):
    return pltpu.CompilerParams(dimension_semantics=sem, vmem_limit_bytes=VMEM_LIMIT)


def _rowwise(fn, rows, bcast, outs, accs=(), *, tm, name):
    n = rows[0].shape[0]
    nr, nb, no = len(rows), len(bcast), len(outs)

    def body(*refs):
        res = fn(*[r[...] for r in refs[:nr + nb]])
        out_refs = refs[nr + nb:nr + nb + no]
        acc_refs = refs[nr + nb + no:]
        for r, v in zip(out_refs, res[:no]):
            r[...] = v.astype(r.dtype)
        if acc_refs:
            @pl.when(pl.program_id(0) == 0)
            def _():
                for r in acc_refs:
                    r[...] = jnp.zeros_like(r)
            for r, v in zip(acc_refs, res[no:]):
                r[...] += v.reshape(tm // 8, 8, v.shape[-1]).sum(axis=0)

    in_specs = [pl.BlockSpec((tm, a.shape[1]), lambda i: (i, 0)) for a in rows]
    in_specs += [pl.BlockSpec(a.shape, lambda i: (0, 0)) for a in bcast]
    out_specs = [pl.BlockSpec((tm, c), lambda i: (i, 0)) for c, _ in outs]
    out_specs += [pl.BlockSpec((8, c), lambda i: (0, 0)) for c in accs]
    out_shape = [jax.ShapeDtypeStruct((n, c), dt) for c, dt in outs]
    out_shape += [jax.ShapeDtypeStruct((8, c), F32) for c in accs]
    return pl.pallas_call(
        body, name=name, grid=(n // tm,), in_specs=in_specs, out_specs=out_specs, out_shape=out_shape,
        compiler_params=_cparams(("arbitrary",)))(*rows, *bcast)


def _matmul(a, b, *, out_dtype, tm, tn, tk=None, name, epilogue=None, extras=(), ta=False, tb=False,
            epilogue_rows=None, b_cols=None, out_into=None):
    epilogue_rows = epilogue_rows or tm
    k, m = a.shape[::-1] if not ta else a.shape
    col0, width = b_cols or (0, b.shape[1])
    n = b.shape[0] if tb else width
    tk = tk or k
    nk = k // tk
    b_off = col0 // (tk if tb else tn)
    assert col0 % (tk if tb else tn) == 0 and (k == width if tb else True)
    ne = len(extras)
    dot = _dot_tn if ta else (_dot_nt if tb else _dot)
    assert epilogue is None or (nk == 1 and not ta)

    def body(*refs):
        a_ref, b_ref = refs[:2]
        ex = refs[2:2 + ne]
        o_ref = refs[2 + ne + (1 if out_into is not None and out_into[0] is not None else 0)]
        if epilogue is not None:
            b_tile = b_ref[...].astype(BF16)
            for r0 in range(0, tm, epilogue_rows):
                rows = slice(r0, r0 + epilogue_rows)
                epilogue(dot(a_ref[rows, :].astype(BF16), b_tile), o_ref, rows, *ex)
            return
        part = dot(a_ref[...].astype(BF16), b_ref[...].astype(BF16))
        if nk == 1:
            o_ref[...] = part.astype(o_ref.dtype)
        else:
            acc_ref = refs[-1]
            kk = pl.program_id(2)

            @pl.when(kk == 0)
            def _():
                acc_ref[...] = part

            @pl.when(kk > 0)
            def _():
                acc_ref[...] += part

            @pl.when(kk == nk - 1)
            def _():
                o_ref[...] = acc_ref[...].astype(o_ref.dtype)

    a_spec = pl.BlockSpec((tk, tm), lambda j, i, kk: (kk, i)) if ta else pl.BlockSpec((tm, tk), lambda j, i, kk: (i, kk))
    b_spec = (pl.BlockSpec((tn, tk), lambda j, i, kk: (j, kk + b_off)) if tb
              else pl.BlockSpec((tk, tn), lambda j, i, kk: (kk, j + b_off)))
    in_specs = [a_spec, b_spec] + [pl.BlockSpec(bs, im) for _, bs, im in extras]
    operands = [a, b] + [e[0] for e in extras]
    aliases = {}
    if out_into is not None:
        prev, out0, n_total = out_into
        o_off = out0 // tn
        assert out0 % tn == 0
        if prev is not None:
            in_specs.append(ANY)
            operands.append(prev)
            aliases = {len(operands) - 1: 0}
    else:
        o_off, n_total = 0, n
    return pl.pallas_call(
        body, name=name, grid=(n // tn, m // tm, nk), in_specs=in_specs,
        out_specs=pl.BlockSpec((tm, tn), lambda j, i, kk: (i, j + o_off)),
        out_shape=jax.ShapeDtypeStruct((m, n_total), out_dtype), input_output_aliases=aliases,
        scratch_shapes=[pltpu.VMEM((tm, tn), F32)] if nk > 1 else [],
        compiler_params=_cparams(("parallel", "parallel", "arbitrary")))(*operands)


def _add_epilogue(acc, o_ref, rows, prev_ref):
    o_ref[rows, :] = (acc + prev_ref[rows, :]).astype(o_ref.dtype)


def _rope(x, c, sp, sm):
    return x * c + pltpu.roll(x, 16, 1) * sp + pltpu.roll(x, LANES - 16, 1) * sm


def _rope_t(dy, c, sp, sm):
    return dy * c + pltpu.roll(dy * sp, LANES - 16, 1) + pltpu.roll(dy * sm, 16, 1)


def _rope_tables(positions, theta, rot_dim, lane0, name):
    n = positions.shape[0]
    half = rot_dim // 2
    inv_freq = 1.0 / (theta ** (jnp.arange(half, dtype=F32) * (2.0 / rot_dim)))
    freq = jnp.concatenate([jnp.zeros((lane0,), F32), inv_freq, inv_freq,
                            jnp.zeros((LANES - lane0 - rot_dim,), F32)]).reshape(1, LANES)

    def fn(p, f):
        ang = p * f
        cos, sin = jnp.cos(ang), jnp.sin(ang)
        lane = lax.broadcasted_iota(jnp.int32, ang.shape, 1) - lane0
        first, second = (lane >= 0) & (lane < half), (lane >= half) & (lane < rot_dim)
        return jnp.where(first | second, cos, 1.0), jnp.where(second, sin, 0.0), jnp.where(first, -sin, 0.0)

    return _rowwise(fn, [positions.astype(F32).reshape(n, 1)], [freq], [(LANES, F32)] * 3, tm=512, name=name)


def _value_heads_t(c_kv, w_vt, bsz, t):
    n, k = c_kv.shape
    tm, tn = 512, 1024

    def body(a_ref, b_ref, o_ref):
        acc = _dot(a_ref[...], b_ref[...])
        lane = lax.broadcasted_iota(jnp.int32, (1, tn), 1)
        acc = acc + jnp.where(lax.rem(lane, LANES) == B_VDIM, 1.0, 0.0)
        o_ref[...] = acc.T.astype(BF16)

    per_seq = t // tm
    return pl.pallas_call(
        body, name="kv_up_v_t", grid=(w_vt.shape[1] // tn, n // tm),
        in_specs=[pl.BlockSpec((tm, k), lambda j, i: (i, 0)), pl.BlockSpec((k, tn), lambda j, i: (0, j))],
        out_specs=pl.BlockSpec((None, tn, tm), lambda j, i: (i // per_seq, j, lax.rem(i, per_seq))),
        out_shape=jax.ShapeDtypeStruct((bsz, w_vt.shape[1], t), BF16),
        compiler_params=_cparams(("parallel", "parallel")))(c_kv, w_vt)


def _rms(x, g):
    xf = x.astype(F32)
    return xf * lax.rsqrt(jnp.mean(xf * xf, axis=-1, keepdims=True) + NORM_EPS) * g


def _rms_bwd(x, g, dy):
    xf = x.astype(F32)
    rstd = lax.rsqrt(jnp.mean(xf * xf, axis=-1, keepdims=True) + NORM_EPS)
    xhat = xf * rstd
    dxhat = dy * g
    dx = rstd * (dxhat - xhat * jnp.mean(dxhat * xhat, axis=-1, keepdims=True))
    return dx, dy * xhat


def _silu(z):
    return z * jax.nn.sigmoid(z)


def _silu_grad(z):
    s = jax.nn.sigmoid(z)
    return s * (1.0 + z * (1.0 - s))


def _dot_nt(a, b):
    return lax.dot_general(a, b, (((1,), (1,)), ((), ())), preferred_element_type=F32)


def _dot_tn(a, b):
    return lax.dot_general(a, b, (((0,), (0,)), ((), ())), preferred_element_type=F32)


def _dot(a, b):
    return jnp.dot(a, b, preferred_element_type=F32)


A_SCALE = A_HEAD_DIM ** -0.5
LOG2E = 1.4426950408889634
A_C2 = A_SCALE * LOG2E
A_HEAD_GROUP = 4
A_FWD_HEAD_GROUP = 8
A_BLOCKS_PER_STEP = 4
A_SEQS_PER_STEP = 4


def _a_steps(nb):
    return (min(nb, A_BLOCKS_PER_STEP), 1) if nb > 1 else (1, A_SEQS_PER_STEP)


def _attn_a_fwd(qk, v, name):
    s_n, ln, _ = qk.shape
    nb = ln // BAND
    lb, sb = _a_steps(nb)
    single = nb == 1
    rows = lb * BAND
    n_keys = BAND if single else 2 * BAND

    def body(q_ref, kc_ref, kp_ref, vc_ref, vp_ref, o_ref, lse_ref):
        kpos = lax.broadcasted_iota(jnp.int32, (n_keys, BAND), 0)
        qpos = lax.broadcasted_iota(jnp.int32, (n_keys, BAND), 1) + (n_keys - BAND)
        band = (kpos <= qpos) & (kpos >= qpos - BAND)
        at_start = band & (kpos >= jnp.where(pl.program_id(1) > 0, 0, n_keys - BAND))
        for s in range(sb):
            for u in range(lb):
                r = slice(u * BAND, (u + 1) * BAND)
                mask = at_start if u == 0 else band

                def keys(prev_ref, cur_ref, hs):
                    cur = cur_ref[s, r, hs]
                    if single:
                        return cur
                    prev = prev_ref[s, :, hs] if u == 0 else cur_ref[s, (u - 1) * BAND:u * BAND, hs]
                    return jnp.concatenate([prev, cur], axis=0)

                stats = []
                for h0 in range(0, A_HEADS, A_FWD_HEAD_GROUP):
                    hss = [slice(h * A_HEAD_DIM, (h + 1) * A_HEAD_DIM) for h in range(h0, h0 + A_FWD_HEAD_GROUP)]
                    sts = [_dot_nt(keys(kp_ref, kc_ref, hs), q_ref[s, r, hs]) for hs in hss]
                    ps, ls = [], []
                    for st in sts:
                        st = jnp.where(mask, st * A_C2, NEG)
                        m = jnp.max(st, axis=0, keepdims=True)
                        p = jnp.exp2(st - m)
                        l_row = jnp.sum(p, axis=0, keepdims=True)
                        ps.append(p.astype(BF16))
                        ls.append(l_row)
                        stats.append(m + jnp.log2(l_row))
                    ots = [_dot_tn(keys(vp_ref, vc_ref, hs), p) for hs, p in zip(hss, ps)]
                    for hs, o_t, l_row in zip(hss, ots, ls):
                        o_ref[s, r, hs] = (o_t / l_row).T.astype(BF16)
                lse_ref[s, :, r] = jnp.concatenate(stats, axis=0)

    def cur(c):
        return pl.BlockSpec((sb, rows, A_WIDTH), lambda s, l: (s, l, c))

    def prev(c):
        return pl.BlockSpec((sb, BAND, A_WIDTH), lambda s, l: (s, jnp.maximum(l * lb - 1, 0), c))

    return pl.pallas_call(
        body, name=name, grid=(s_n // sb, nb // lb),
        in_specs=[cur(0), cur(1), prev(1), cur(0), prev(0)],
        out_specs=[cur(0), pl.BlockSpec((sb, A_HEADS, rows), lambda s, l: (s, 0, l))],
        out_shape=[jax.ShapeDtypeStruct((s_n, ln, A_WIDTH), BF16), jax.ShapeDtypeStruct((s_n, A_HEADS, ln), F32)],
        compiler_params=_cparams(("parallel", "arbitrary")))(qk, qk, qk, v, v)


def _attn_a_bwd(qk, v, do, lse2, dsum, tabs, name):
    s_n, ln, _ = qk.shape
    nb = ln // BAND
    lb, sb = _a_steps(nb)
    single = nb == 1
    rows = lb * BAND
    n_steps = nb // lb

    def body(q_ref, qn_ref, kc_ref, kp_ref, vc_ref, vp_ref, do_ref, don_ref, lse_ref, lsen_ref, ds_ref, dsn_ref,
             c_ref, sp_ref, sm_ref, out_ref):
        l_idx = pl.program_id(1)
        kpos = lax.broadcasted_iota(jnp.int32, (2 * BAND, BAND), 0)
        qpos = lax.broadcasted_iota(jnp.int32, (2 * BAND, BAND), 1) + BAND
        band_q = (kpos <= qpos) & (kpos >= qpos - BAND)
        start_q = band_q & (kpos >= jnp.where(l_idx > 0, 0, BAND))
        kpos2 = lax.broadcasted_iota(jnp.int32, (BAND, 2 * BAND), 0)
        qpos2 = lax.broadcasted_iota(jnp.int32, (BAND, 2 * BAND), 1)
        band_k = (kpos2 <= qpos2) & (kpos2 >= qpos2 - BAND)
        end_k = band_k & (qpos2 < jnp.where(l_idx < n_steps - 1, 2 * BAND, BAND))
        causal = kpos[:BAND] <= qpos[:BAND] - BAND
        for s in range(sb):
            for u in range(lb):
                r = slice(u * BAND, (u + 1) * BAND)
                c, sp, sm = c_ref[s, r, :], sp_ref[s, r, :], sm_ref[s, r, :]
                lse_q, ds_q = lse_ref[s, :, r], ds_ref[s, :, r]

                def store(hl, hss, dqs, dks, dvs):
                    for i, h in enumerate(hl):
                        out_ref[s, r, hss[i]] = _rope_t(dqs[i], c, sp, sm).astype(BF16)
                        out_ref[s, r, A_WIDTH + h * A_HEAD_DIM:A_WIDTH + (h + 1) * A_HEAD_DIM] = \
                            _rope_t(dks[i], c, sp, sm).astype(BF16)
                        out_ref[s, r, 2 * A_WIDTH + h * A_HEAD_DIM:2 * A_WIDTH + (h + 1) * A_HEAD_DIM] = \
                            dvs[i].astype(BF16)

                if single:
                    for h0 in range(0, A_HEADS, A_HEAD_GROUP):
                        hl = list(range(h0, h0 + A_HEAD_GROUP))
                        hss = [slice(h * A_HEAD_DIM, (h + 1) * A_HEAD_DIM) for h in hl]
                        sts = [_dot_nt(kc_ref[s, r, hs], q_ref[s, r, hs]) for hs in hss]
                        dpts = [_dot_nt(vc_ref[s, r, hs], do_ref[s, r, hs]) for hs in hss]
                        ps, dsts = [], []
                        for i, h in enumerate(hl):
                            p = jnp.exp2(jnp.where(causal, sts[i] * A_C2, NEG) - lse_q[h:h + 1])
                            dsts.append((p * (dpts[i] - ds_q[h:h + 1]) * A_SCALE).astype(BF16))
                            ps.append(p.astype(BF16))
                        store(hl, hss, [_dot_tn(dst, kc_ref[s, r, hs]) for dst, hs in zip(dsts, hss)],
                              [_dot(dst, q_ref[s, r, hs]) for dst, hs in zip(dsts, hss)],
                              [_dot(p, do_ref[s, r, hs]) for p, hs in zip(ps, hss)])
                    continue

                r_prev = slice((u - 1) * BAND, u * BAND)
                r_next = slice((u + 1) * BAND, (u + 2) * BAND)

                def before(prev_ref, cur_ref, hs):
                    return prev_ref[s, :, hs] if u == 0 else cur_ref[s, r_prev, hs]

                def after(next_ref, cur_ref, hs):
                    return next_ref[s, :, hs] if u == lb - 1 else cur_ref[s, r_next, hs]

                mask_q = start_q if u == 0 else band_q
                mask_k = end_k if u == lb - 1 else band_k
                lse_n = lsen_ref[s] if u == lb - 1 else lse_ref[s, :, r_next]
                ds_n = dsn_ref[s] if u == lb - 1 else ds_ref[s, :, r_next]
                lse_k = jnp.concatenate([lse_q, lse_n], axis=1)
                ds_k = jnp.concatenate([ds_q, ds_n], axis=1)
                for h0 in range(0, A_HEADS, A_HEAD_GROUP):
                    hl = list(range(h0, h0 + A_HEAD_GROUP))
                    hss = [slice(h * A_HEAD_DIM, (h + 1) * A_HEAD_DIM) for h in hl]
                    k2s = [jnp.concatenate([before(kp_ref, kc_ref, hs), kc_ref[s, r, hs]], axis=0) for hs in hss]
                    v2s = [jnp.concatenate([before(vp_ref, vc_ref, hs), vc_ref[s, r, hs]], axis=0) for hs in hss]
                    q2s = [jnp.concatenate([q_ref[s, r, hs], after(qn_ref, q_ref, hs)], axis=0) for hs in hss]
                    do2s = [jnp.concatenate([do_ref[s, r, hs], after(don_ref, do_ref, hs)], axis=0) for hs in hss]
                    sts = [_dot_nt(k2, q_ref[s, r, hs]) for k2, hs in zip(k2s, hss)]
                    dpts = [_dot_nt(v2, do_ref[s, r, hs]) for v2, hs in zip(v2s, hss)]
                    st2s = [_dot_nt(kc_ref[s, r, hs], q2) for q2, hs in zip(q2s, hss)]
                    dpt2s = [_dot_nt(vc_ref[s, r, hs], do2) for do2, hs in zip(do2s, hss)]
                    dsts, dst2s, p2s = [], [], []
                    for i, h in enumerate(hl):
                        p = jnp.exp2(jnp.where(mask_q, sts[i] * A_C2, NEG) - lse_q[h:h + 1])
                        dsts.append((p * (dpts[i] - ds_q[h:h + 1]) * A_SCALE).astype(BF16))
                        p2 = jnp.exp2(jnp.where(mask_k, st2s[i] * A_C2, NEG) - lse_k[h:h + 1])
                        dst2s.append((p2 * (dpt2s[i] - ds_k[h:h + 1]) * A_SCALE).astype(BF16))
                        p2s.append(p2.astype(BF16))
                    store(hl, hss, [_dot_tn(dsts[i], k2s[i]) for i in range(A_HEAD_GROUP)],
                          [_dot(dst2s[i], q2s[i]) for i in range(A_HEAD_GROUP)],
                          [_dot(p2s[i], do2s[i]) for i in range(A_HEAD_GROUP)])

    def cur(c, width=A_WIDTH):
        return pl.BlockSpec((sb, rows, width), lambda s, l: (s, l, c))

    def prev(c):
        return pl.BlockSpec((sb, BAND, A_WIDTH), lambda s, l: (s, jnp.maximum(l * lb - 1, 0), c))

    def nxt(c):
        return pl.BlockSpec((sb, BAND, A_WIDTH), lambda s, l: (s, jnp.minimum((l + 1) * lb, nb - 1), c))

    stat = pl.BlockSpec((sb, A_HEADS, rows), lambda s, l: (s, 0, l))
    stat_next = pl.BlockSpec((sb, A_HEADS, BAND), lambda s, l: (s, 0, jnp.minimum((l + 1) * lb, nb - 1)))
    tspec = cur(0, LANES)
    in_specs = [cur(0), nxt(0), cur(1), prev(1), cur(0), prev(0), cur(0), nxt(0),
                stat, stat_next, stat, stat_next, tspec, tspec, tspec]
    return pl.pallas_call(
        body, name=name, grid=(s_n // sb, n_steps), in_specs=in_specs,
        out_specs=cur(0, 3 * A_WIDTH),
        out_shape=jax.ShapeDtypeStruct((s_n, ln, 3 * A_WIDTH), BF16),
        compiler_params=_cparams(("parallel", "arbitrary")))(
            qk, qk, qk, qk, v, v, do, do, lse2, lse2, dsum, dsum, *tabs)


B_TQ = 256
B_FWD_HEADS = 4
B_BWD_PAIRS = 2
B_SCALE = B_QK_DIM ** -0.5
B_C2 = B_SCALE * LOG2E


def _key_le_query(kb, qb, tk, tq):
    kpos = kb * tk + lax.broadcasted_iota(jnp.int32, (tk, tq), 0)
    qpos = qb * tq + lax.broadcasted_iota(jnp.int32, (tk, tq), 1)
    return kpos <= qpos


def _attn_b_fwd(q, kpad, vt1):
    bsz, t, _ = q.shape
    tq = tk = B_TQ
    nq = t // tq
    nh = B_FWD_HEADS

    def body(q_ref, k_ref, vt_ref, o_ref, lse_ref):
        qblk = pl.program_id(2)
        qs = [q_ref[:, hh * LANES:(hh + 1) * LANES] for hh in range(nh)]

        def scores(kb):
            start = pl.multiple_of(kb * tk, tk)
            return [_dot_nt(k_ref[pl.ds(start, tk), hh * LANES:(hh + 1) * LANES], qs[hh]) for hh in range(nh)]

        def pv(kb, ps):
            start = pl.multiple_of(kb * tk, tk)
            return [_dot(vt_ref[hh * LANES:(hh + 1) * LANES, pl.ds(start, tk)], ps[hh]) for hh in range(nh)]

        def softmax(ss, ms, accs, kb, masked):
            out_m, out_acc, out_p = [], [], []
            for hh in range(nh):
                s = ss[hh] * B_C2
                if masked:
                    s = jnp.where(_key_le_query(kb, qblk, tk, tq), s, NEG)
                m_new = jnp.maximum(ms[hh], jnp.max(s, axis=0, keepdims=True))
                out_acc.append(jnp.exp2(ms[hh] - m_new) * accs[hh])
                out_p.append(jnp.exp2(s - m_new).astype(BF16))
                out_m.append(m_new)
            return out_m, out_acc, out_p

        def step(kb, carry):
            ss, ps, ms, accs = carry
            pvs = pv(jnp.maximum(kb - 1, 0), ps)
            ss_next = scores(kb + 1)
            accs = [accs[hh] + pvs[hh] for hh in range(nh)]
            ms, accs, ps = softmax(ss, ms, accs, kb, False)
            return (ss_next, ps, ms, accs)

        init = (scores(0), [jnp.zeros((tk, tq), BF16)] * nh, [jnp.full((1, tq), NEG, F32)] * nh,
                [jnp.zeros((LANES, tq), F32)] * nh)
        ss, ps, ms, accs = lax.fori_loop(0, qblk, step, init)
        pvs = pv(jnp.maximum(qblk - 1, 0), ps)
        accs = [accs[hh] + pvs[hh] for hh in range(nh)]
        ms, accs, ps = softmax(ss, ms, accs, qblk, True)
        pvs = pv(qblk, ps)
        accs = [accs[hh] + pvs[hh] for hh in range(nh)]
        ls = [accs[hh][B_VDIM:B_VDIM + 1] for hh in range(nh)]
        o_t = jnp.concatenate([accs[hh][:B_VDIM] / ls[hh] for hh in range(nh)], axis=0)
        o_ref[...] = o_t.T
        for pair in range(nh // 2):
            lse_ref[pair] = jnp.concatenate([ms[2 * pair + hh] + jnp.log2(ls[2 * pair + hh]) for hh in range(2)]
                                            + [jnp.zeros((6, tq), F32)], axis=0)

    return pl.pallas_call(
        body, name="attn_b_fwd", grid=(bsz, B_HEADS // nh, nq),
        in_specs=[pl.BlockSpec((None, tq, nh * LANES), lambda b, j, i: (b, i, j)),
                  pl.BlockSpec((None, t, nh * LANES), lambda b, j, i: (b, 0, j)),
                  pl.BlockSpec((None, nh * LANES, t), lambda b, j, i: (b, j, 0))],
        out_specs=[pl.BlockSpec((None, tq, nh * B_VDIM), lambda b, j, i: (b, i, j)),
                   pl.BlockSpec((None, nh // 2, 8, tq), lambda b, j, i: (b, j, 0, i))],
        out_shape=[jax.ShapeDtypeStruct((bsz, t, B_HEADS * B_VDIM), F32),
                   jax.ShapeDtypeStruct((bsz, B_HEADS // 2, 8, t), F32)],
        compiler_params=_cparams(("parallel", "parallel", "arbitrary")))(q, kpad, vt1)


def _attn_b_bwd(q, kpad, v, do, o, lse2, tabs):
    bsz, t, _ = q.shape
    tq = tk = B_TQ
    nq = t // tq
    n_pairs = B_BWD_PAIRS
    n_heads = 2 * n_pairs

    def body(q_ref, k_ref, v_ref, do_ref, o_ref, lse_ref, c_ref, sp_ref, sm_ref, dq_ref, dk_ref, dv_ref,
             dqt_scr, dsum_scr):
        lane = lax.broadcasted_iota(jnp.int32, (tk, LANES), 1)
        sel_lane = lax.broadcasted_iota(jnp.int32, (8, LANES), 1)
        sel_row = lax.broadcasted_iota(jnp.int32, (8, LANES), 0)
        sel = jnp.where((sel_lane < B_VDIM) == (sel_row == 0), 1.0, 0.0)
        sel = jnp.where(sel_row < 2, sel, 0.0).astype(BF16)

        def rows(blk):
            return pl.ds(pl.multiple_of(blk * tq, tq), tq)

        def dsum_step(qb, carry):
            for pp in range(n_pairs):
                pls = slice(pp * LANES, (pp + 1) * LANES)
                prod = do_ref[rows(qb), pls].astype(F32) * o_ref[rows(qb), pls]
                hi = prod.astype(BF16)
                lo = (prod - hi.astype(F32)).astype(BF16)
                dsum_scr[pp, :, rows(qb)] = _dot_nt(sel, hi) + _dot_nt(sel, lo)
            return carry

        lax.fori_loop(0, nq, dsum_step, 0)
        dqt_scr[...] = jnp.zeros_like(dqt_scr)

        def kv_step(kb, carry):
            ks = [k_ref[rows(kb), hh * LANES:(hh + 1) * LANES] for hh in range(n_heads)]
            vs = []
            for pp in range(n_pairs):
                vb = v_ref[rows(kb), pp * LANES:(pp + 1) * LANES]
                zero = jnp.zeros_like(vb)
                vs += [jnp.where(lane < B_VDIM, vb, zero), jnp.where(lane < B_VDIM, zero, vb)]

            def make_step(masked):
                def step(qb, acc):
                    qs = [q_ref[rows(qb), hh * LANES:(hh + 1) * LANES] for hh in range(n_heads)]
                    dob = [do_ref[rows(qb), pp * LANES:(pp + 1) * LANES] for pp in range(n_pairs)]
                    ss = [_dot_nt(ks[hh], qs[hh]) for hh in range(n_heads)]
                    dps = [_dot_nt(vs[hh], dob[hh // 2]) for hh in range(n_heads)]
                    pbs, dss = [], []
                    for hh in range(n_heads):
                        stat = (hh // 2, slice(hh % 2, hh % 2 + 1), rows(qb))
                        s = ss[hh] * B_C2
                        if masked:
                            s = jnp.where(_key_le_query(kb, qb, tk, tq), s, NEG)
                        p = jnp.exp2(s - lse_ref[stat])
                        dss.append((p * (dps[hh] - dsum_scr[stat]) * B_SCALE).astype(BF16))
                        pbs.append(p.astype(BF16))
                    for hh in range(n_heads):
                        dqt_scr[hh, :, rows(qb)] += _dot_tn(ks[hh], dss[hh])
                    return tuple([acc[hh] + _dot(dss[hh], qs[hh]) for hh in range(n_heads)]
                                 + [acc[n_heads + hh] + _dot(pbs[hh], dob[hh // 2]) for hh in range(n_heads)])
                return step

            acc = make_step(True)(kb, (jnp.zeros((tk, LANES), F32),) * (2 * n_heads))
            acc = lax.fori_loop(kb + 1, nq, make_step(False), acc)
            for hh in range(n_heads):
                dk_ref[rows(kb), hh * LANES:(hh + 1) * LANES] = acc[hh].astype(BF16)
            for pp in range(n_pairs):
                dv_pair = jnp.where(lane < B_VDIM, acc[n_heads + 2 * pp], acc[n_heads + 2 * pp + 1])
                dv_ref[rows(kb), pp * LANES:(pp + 1) * LANES] = dv_pair.astype(BF16)
            return carry

        lax.fori_loop(0, nq, kv_step, 0)

        def dq_step(qb, carry):
            c, sp, sm = c_ref[rows(qb), :], sp_ref[rows(qb), :], sm_ref[rows(qb), :]
            for hh in range(n_heads):
                dq_ref[rows(qb), hh * LANES:(hh + 1) * LANES] = _rope_t(dqt_scr[hh, :, rows(qb)].T, c, sp, sm).astype(BF16)
            return carry

        lax.fori_loop(0, nq, dq_step, 0)

    pair_full = pl.BlockSpec((None, t, n_heads * LANES), lambda b, j: (b, 0, j))
    one_full = pl.BlockSpec((None, t, n_pairs * LANES), lambda b, j: (b, 0, j))
    row_full = pl.BlockSpec((None, n_pairs, 8, t), lambda b, j: (b, j, 0, 0))
    tab_full = pl.BlockSpec((None, t, LANES), lambda b, j: (b, 0, 0))
    return pl.pallas_call(
        body, name="attn_b_bwd", grid=(bsz, B_HEADS // n_heads),
        in_specs=[pair_full, pair_full, one_full, one_full, one_full, row_full, tab_full, tab_full, tab_full],
        out_specs=[pair_full, pair_full, one_full],
        out_shape=[jax.ShapeDtypeStruct((bsz, t, B_KPAD), BF16), jax.ShapeDtypeStruct((bsz, t, B_KPAD), BF16),
                   jax.ShapeDtypeStruct((bsz, t, B_HEADS * B_VDIM), BF16)],
        scratch_shapes=[pltpu.VMEM((n_heads, LANES, t), F32), pltpu.VMEM((n_pairs, 8, t), F32)],
        compiler_params=_cparams(("parallel", "parallel")))(q, kpad, v, do, o, lse2, *tabs)


ANY = pl.BlockSpec(memory_space=pl.ANY)


def _all_gather(shard, name):
    def body(x_ref, out_ref, send_sems, recv_sems, local_sem):
        x, y, c = lax.axis_index("x"), lax.axis_index("y"), lax.axis_index("c")
        me, sibling = (x, y, c), (x, y, 1 - c)
        chips = [(1 - x, y), (x, 1 - y), (1 - x, 1 - y)]

        def rows(px, py, pc):
            return out_ref.at[4 * px + 2 * py + pc]

        def copy(k, block, to, src=None):
            return pltpu.make_async_remote_copy(
                src_ref=rows(*block) if src is None else src, dst_ref=rows(*block),
                send_sem=send_sems.at[k], recv_sem=recv_sems.at[k], device_id=to, device_id_type=MESH)

        mine = pltpu.make_async_copy(x_ref, rows(*me), local_sem)
        mine.start()
        first = [copy(0, me, sibling, src=x_ref)]
        first += [copy(1 + j, me, (*chip, c), src=x_ref) for j, chip in enumerate(chips)]
        for cp in first:
            cp.start()
        passed = [copy(4 + j, (*chip, c), sibling) for j, chip in enumerate(chips)]
        for j, chip in enumerate(chips):
            copy(1 + j, (*chip, c), me).wait_recv()
            passed[j].start()
        copy(0, sibling, me).wait_recv()
        for j, chip in enumerate(chips):
            copy(4 + j, (*chip, 1 - c), me).wait_recv()
        for cp in first + passed:
            cp.wait_send()
        mine.wait()

    return pl.pallas_call(
        body, name=name, in_specs=[ANY], out_specs=ANY,
        out_shape=jax.ShapeDtypeStruct((N_DEV,) + shard.shape, shard.dtype),
        scratch_shapes=[pltpu.SemaphoreType.DMA((7,)), pltpu.SemaphoreType.DMA((7,)), pltpu.SemaphoreType.DMA])(shard)


def _all_gather_weights(flat, wide, name):
    ns = wide.shape[1]

    def body(f_ref, w_ref, fo_ref, wo_ref, send_sems, recv_sems, local_sems):
        x, y, c = lax.axis_index("x"), lax.axis_index("y"), lax.axis_index("c")
        me, sibling = (x, y, c), (x, y, 1 - c)
        chips = [(1 - x, y), (x, 1 - y), (1 - x, 1 - y)]

        def place(a, px, py, pc):
            idx = 4 * px + 2 * py + pc
            if a == 0:
                return fo_ref.at[idx]
            return wo_ref.at[:, pl.ds(pl.multiple_of(idx * ns, LANES), ns)]

        def copy(a, k, block, to, src=None):
            return pltpu.make_async_remote_copy(
                src_ref=place(a, *block) if src is None else src, dst_ref=place(a, *block),
                send_sem=send_sems.at[a, k], recv_sem=recv_sems.at[a, k], device_id=to, device_id_type=MESH)

        relay_in = (c * x + (1 - c) * (1 - x), c * (1 - y) + (1 - c) * y)
        relay_out = (c * (1 - x) + (1 - c) * x, c * y + (1 - c) * (1 - y))
        own = (f_ref, w_ref)
        mine = [pltpu.make_async_copy(own[a], place(a, *me), local_sems.at[a]) for a in range(2)]
        first = []
        for a in range(2):
            mine[a].start()
            first.append(copy(a, 0, me, sibling, src=own[a]))
            first += [copy(a, 1 + j, me, (*chip, c), src=own[a]) for j, chip in enumerate(chips[:2])]
        for cp in first:
            cp.start()
        relay = [copy(a, 3, (*relay_in, c), (*relay_out, c)) for a in range(2)]
        passed = [[copy(a, 4 + j, (*chip, c), sibling) for j, chip in enumerate(chips)] for a in range(2)]
        for a in range(2):
            for j, chip in enumerate(chips[:2]):
                copy(a, 1 + j, (*chip, c), me).wait_recv()
                passed[a][j].start()
            relay[a].start()
        for a in range(2):
            copy(a, 3, (*chips[2], c), me).wait_recv()
            passed[a][2].start()
        for a in range(2):
            copy(a, 0, sibling, me).wait_recv()
            for j, chip in enumerate(chips):
                copy(a, 4 + j, (*chip, 1 - c), me).wait_recv()
        for cp in first + relay + passed[0] + passed[1]:
            cp.wait_send()
        for cp in mine:
            cp.wait()

    return pl.pallas_call(
        body, name=name, in_specs=[ANY, ANY], out_specs=[ANY, ANY],
        out_shape=[jax.ShapeDtypeStruct((N_DEV,) + flat.shape, flat.dtype),
                   jax.ShapeDtypeStruct((wide.shape[0], N_DEV * ns), wide.dtype)],
        scratch_shapes=[pltpu.SemaphoreType.DMA((2, 7)), pltpu.SemaphoreType.DMA((2, 7)),
                        pltpu.SemaphoreType.DMA((2,))])(flat, wide)


N_CHIPS = 4


def _exchange_sibling(blocks, name):
    def body(g_ref, got_ref, send_sems, recv_sems):
        x, y, c = lax.axis_index("x"), lax.axis_index("y"), lax.axis_index("c")
        sends = [pltpu.make_async_remote_copy(
            src_ref=g_ref.at[q, 1 - c], dst_ref=got_ref.at[q], send_sem=send_sems.at[q],
            recv_sem=recv_sems.at[q], device_id=(x, y, 1 - c), device_id_type=MESH) for q in range(N_CHIPS)]
        for cp in sends:
            cp.start()
        for cp in sends:
            cp.wait_recv()
        for cp in sends:
            cp.wait_send()

    return pl.pallas_call(
        body, name=name, in_specs=[ANY], out_specs=ANY,
        out_shape=jax.ShapeDtypeStruct((N_CHIPS,) + blocks.shape[2:], blocks.dtype),
        scratch_shapes=[pltpu.SemaphoreType.DMA((N_CHIPS,)), pltpu.SemaphoreType.DMA((N_CHIPS,))])(blocks)


def _exchange_chips(parts, name):
    def body(p_ref, out_ref, send_sems, recv_sems, local_sem):
        x, y, c = lax.axis_index("x"), lax.axis_index("y"), lax.axis_index("c")
        me = 2 * x + y

        def peer(k):
            return (1 - x if k & 2 else x, 1 - y if k & 1 else y)

        def copy(k):
            px, py = peer(k)
            return pltpu.make_async_remote_copy(
                src_ref=p_ref.at[2 * px + py], dst_ref=out_ref.at[me], send_sem=send_sems.at[k - 1],
                recv_sem=recv_sems.at[k - 1], device_id=(px, py, c), device_id_type=MESH)

        def arrival(k):
            px, py = peer(k)
            slot = out_ref.at[2 * px + py]
            return pltpu.make_async_remote_copy(
                src_ref=slot, dst_ref=slot, send_sem=send_sems.at[k - 1], recv_sem=recv_sems.at[k - 1],
                device_id=(px, py, c), device_id_type=MESH)

        mine = pltpu.make_async_copy(p_ref.at[me], out_ref.at[me], local_sem)
        mine.start()
        sends = [copy(k) for k in range(1, N_CHIPS)]
        for cp in sends:
            cp.start()
        for k in range(1, N_CHIPS):
            arrival(k).wait_recv()
        for cp in sends:
            cp.wait_send()
        mine.wait()

    return pl.pallas_call(
        body, name=name, in_specs=[ANY], out_specs=ANY,
        out_shape=jax.ShapeDtypeStruct(parts.shape, parts.dtype),
        scratch_shapes=[pltpu.SemaphoreType.DMA((N_CHIPS - 1,)), pltpu.SemaphoreType.DMA((N_CHIPS - 1,)),
                        pltpu.SemaphoreType.DMA])(parts)


def _add_pairs(blocks, got, core, *, tr, name):
    q, r, c = got.shape

    def body(core_ref, a_ref, b_ref, o_ref):
        o_ref[...] = (a_ref[...].astype(F32) + b_ref[...].astype(F32)).astype(o_ref.dtype)

    spec = pl.BlockSpec((q, tr, c), lambda i, core_ref: (0, i, 0))
    mine = pl.BlockSpec((q, None, tr, c), lambda i, core_ref: (0, core_ref[0], i, 0))
    return pl.pallas_call(
        body, name=name,
        grid_spec=pltpu.PrefetchScalarGridSpec(num_scalar_prefetch=1, grid=(r // tr,), in_specs=[mine, spec],
                                               out_specs=spec),
        out_shape=jax.ShapeDtypeStruct(got.shape, BF16), compiler_params=_cparams(("parallel",)))(
            core, blocks, got)


def _sum_slots(slots, *, tr, name):
    n_slots, r, c = slots.shape

    def body(s_ref, o_ref):
        acc = s_ref[0].astype(F32)
        for s in range(1, n_slots):
            acc = acc + s_ref[s].astype(F32)
        o_ref[...] = acc

    return pl.pallas_call(
        body, name=name, grid=(r // tr,),
        in_specs=[pl.BlockSpec((n_slots, tr, c), lambda i: (0, i, 0))],
        out_specs=pl.BlockSpec((tr, c), lambda i: (i, 0)),
        out_shape=jax.ShapeDtypeStruct((r, c), F32),
        compiler_params=_cparams(("parallel",)))(slots)


def _adamw_math(w_t, g_t, m_t, v_t):
    m_n = ADAM_B1 * m_t + (1.0 - ADAM_B1) * g_t
    v_n = ADAM_B2 * v_t + (1.0 - ADAM_B2) * (g_t * g_t)
    m_hat = m_n / (1.0 - ADAM_B1 ** ADAM_STEP)
    v_hat = v_n / (1.0 - ADAM_B2 ** ADAM_STEP)
    delta = -ADAM_LR * (m_hat / (jnp.sqrt(v_hat) + ADAM_EPS) + ADAM_WD * w_t)
    return delta, m_n, v_n


def _adamw(w, g, m, v, name):
    shape = w.shape
    cols = shape[-1]
    args = [a.reshape(-1, cols) for a in (w, g, m, v)]
    rows = args[0].shape[0]
    tm = 256 if rows % 256 == 0 else rows
    delta, m_n, v_n = _rowwise(_adamw_math, args, [], [(cols, F32)] * 3, tm=tm, name=name)
    return delta.reshape(shape), m_n.reshape(shape), v_n.reshape(shape)


def _adamw_small(ws, gs, ms, vs, name):
    k = len(ws)
    args = [a.reshape(-1, a.shape[-1]) for group in (ws, gs, ms, vs) for a in group]

    def body(*refs):
        ins, outs = refs[:4 * k], refs[4 * k:]
        for i in range(k):
            res = _adamw_math(*[ins[j * k + i][...] for j in range(4)])
            for j in range(3):
                outs[j * k + i][...] = res[j]

    res = pl.pallas_call(
        body, name=name, out_shape=[jax.ShapeDtypeStruct(a.shape, F32) for a in args[:k]] * 3,
        compiler_params=pltpu.CompilerParams(vmem_limit_bytes=VMEM_LIMIT))(*args)
    return [[res[j * k + i].reshape(ws[i].shape) for i in range(k)] for j in range(3)]


def _local_step(x, positions, target, w):
    bsz, t, _ = x.shape
    n = bsz * t
    tm = 256
    mm = 512
    x2 = x.reshape(n, D_MODEL)
    tgt2 = target.reshape(n, D_MODEL)
    pos = positions.reshape(n)
    tb2 = _rope_tables(pos, B_ROPE_THETA, B_ROPE, B_NOPE, "rope_tables_b")
    tabs_b = [a.reshape(bsz, t, LANES) for a in tb2]

    w_a_in = w["a_w_in"]
    w_a_out = w["a_w_out"]
    w_down = w["kv_w_down"]
    w_down_p = jnp.zeros((D_MODEL, 3 * LANES), BF16).at[:, :B_KV_LORA].set(w_down[:, :B_KV_LORA])
    w_down_p = w_down_p.at[:, B_KV_LORA + B_NOPE:B_KV_LORA + B_QK_DIM].set(w_down[:, B_KV_LORA:])
    wu = w["kv_w_up"].reshape(B_KV_LORA, B_HEADS, B_NOPE + B_VDIM)
    w_upk = jnp.pad(wu[:, :, :B_NOPE], ((0, 0), (0, 0), (0, LANES - B_NOPE))).reshape(B_KV_LORA, B_KPAD)
    w_upv = wu[:, :, B_NOPE:].reshape(B_KV_LORA, B_HEADS * B_VDIM)
    w_b_in = w["b_w_in"]
    w_q_p = jnp.pad(w["b_w_q_up"].reshape(B_Q_LORA, B_HEADS, B_QK_DIM),
                    ((0, 0), (0, 0), (0, LANES - B_QK_DIM))).reshape(B_Q_LORA, B_KPAD)
    w_b_out = w["b_w_out"]

    def tab_extras(tabs2, rows):
        return [(a, (rows, LANES), lambda j, i, kk: (i, 0)) for a in tabs2]

    (hn_a,) = _rowwise(lambda xt, g: (_rms(xt, g),), [x2], [w["a_pre_norm"]], [(D_MODEL, BF16)],
                       tm=tm, name="a_pre_norm")

    def rope_epilogue(acc, o_ref, rows, c_ref, sp_ref, sm_ref):
        c, sp, sm = c_ref[rows, :], sp_ref[rows, :], sm_ref[rows, :]
        for h in range(acc.shape[1] // LANES):
            hs = slice(h * LANES, (h + 1) * LANES)
            o_ref[rows, hs] = _rope(acc[:, hs], c, sp, sm).astype(BF16)

    def to_group(a, d):
        if d == 1:
            return a
        return a.reshape(bsz, t // d, d, a.shape[-1]).transpose(0, 2, 1, 3).reshape(n, a.shape[-1])

    def from_group(a, d):
        if d == 1:
            return a
        return a.reshape(bsz, d, t // d, a.shape[-1]).transpose(0, 2, 1, 3).reshape(n, a.shape[-1])

    def rows_to_cols(r, d):
        return r.reshape(bsz, d, A_HEADS, t // d).transpose(0, 3, 1, 2).reshape(n, A_HEADS)

    def cols_to_rows(cc, d):
        return cc.reshape(bsz, t // d, d, A_HEADS).transpose(0, 2, 3, 1).reshape(bsz * d, A_HEADS, t // d)

    z_a = _matmul(hn_a, w_a_in, b_cols=(A_QKV, A_WIDTH), out_dtype=F32, tm=mm, tn=A_WIDTH, name="a_gate")
    hn_g, tabs_g, qk_g, v_g, o_g, lse_g = [], [], [], [], [], []
    for g, d in enumerate(A_DILATIONS):
        hn_g.append(to_group(hn_a, d))
        tabs_g.append(_rope_tables(to_group(pos.reshape(n, 1), d).reshape(n), A_ROPE_THETA, A_ROT_DIM, 0,
                                   f"rope_tables_a_g{g}"))
        col0 = g * 3 * A_WIDTH
        qk = _matmul(hn_g[g], w_a_in, b_cols=(col0, 2 * A_WIDTH), out_dtype=BF16, tm=mm, tn=A_WIDTH,
                     name=f"a_qk_g{g}", epilogue=rope_epilogue, epilogue_rows=EPILOGUE_ROWS,
                     extras=tab_extras(tabs_g[g], mm))
        v = _matmul(hn_g[g], w_a_in, b_cols=(col0 + 2 * A_WIDTH, A_WIDTH), out_dtype=BF16, tm=mm, tn=A_WIDTH,
                    name=f"a_v_g{g}")
        qk_g.append(qk.reshape(bsz * d, t // d, 2 * A_WIDTH))
        v_g.append(v.reshape(bsz * d, t // d, A_WIDTH))
        o, lse = _attn_a_fwd(qk_g[g], v_g[g], f"attn_a_fwd_g{g}")
        o_g.append(from_group(o.reshape(n, A_WIDTH), d))
        lse_g.append(rows_to_cols(lse, d))

    def merge_fn(o0, o1, o2, l0, l1, l2, z):
        lmax = jnp.maximum(jnp.maximum(l0, l1), l2)
        e0, e1, e2 = jnp.exp2(l0 - lmax), jnp.exp2(l1 - lmax), jnp.exp2(l2 - lmax)
        den = e0 + e1 + e2
        w0, w1, w2 = e0 / den, e1 / den, e2 / den
        parts = []
        for h in range(A_HEADS):
            hs = slice(h * A_HEAD_DIM, (h + 1) * A_HEAD_DIM)
            parts.append(w0[:, h:h + 1] * o0[:, hs] + w1[:, h:h + 1] * o1[:, hs] + w2[:, h:h + 1] * o2[:, hs])
        o = jnp.concatenate(parts, axis=1)
        return o * _silu(z), o, lmax + jnp.log2(den)

    y_a, o_a2, lse_a = _rowwise(merge_fn, [*o_g, *lse_g, z_a], [],
                                [(A_WIDTH, BF16), (A_WIDTH, F32), (A_HEADS, F32)], tm=tm, name="a_merge_gate")
    out_a = _matmul(y_a, w_a_out, out_dtype=F32, tm=mm, tn=D_MODEL, name="a_out")

    def mid_fn(xt, out, g_post, g_kv, g_b):
        h1 = xt + _rms(out, g_post)
        return h1, _rms(h1, g_kv), _rms(h1, g_b)

    h1, hn_kv, hn_b = _rowwise(mid_fn, [x2, out_a], [w["a_post_norm"], w["kv_norm"], w["b_pre_norm"]],
                               [(D_MODEL, F32), (D_MODEL, BF16), (D_MODEL, BF16)], tm=tm, name="a_post_norm")

    ckr = _matmul(hn_kv, w_down_p, out_dtype=F32, tm=mm, tn=3 * LANES, name="kv_down")

    def latent_fn(ck, c, sp, sm, g):
        return _rms(ck[:, :B_KV_LORA], g), _rope(ck[:, B_KV_LORA:], c, sp, sm)

    c_kv, k_rope = _rowwise(latent_fn, [ckr, *tb2], [w["kv_latent_norm"]], [(B_KV_LORA, BF16), (LANES, F32)],
                            tm=tm, name="kv_latent_norm")

    def kpad_epilogue(acc, o_ref, rows, kr_ref):
        kr = kr_ref[rows, :]
        for h in range(acc.shape[1] // LANES):
            hs = slice(h * LANES, (h + 1) * LANES)
            o_ref[rows, hs] = (acc[:, hs] + kr).astype(BF16)

    kpad = _matmul(c_kv, w_upk, out_dtype=BF16, tm=mm, tn=1024, name="kv_up_k", epilogue=kpad_epilogue,
                   epilogue_rows=EPILOGUE_ROWS,
                   extras=[(k_rope, (mm, LANES), lambda j, i, kk: (i, 0))])
    v_b = _matmul(c_kv, w_upv, out_dtype=BF16, tm=mm, tn=1024, name="kv_up_v")

    proj_b = _matmul(hn_b, w_b_in, out_dtype=F32, tm=mm, tn=w_b_in.shape[1], name="b_in")
    (c_q,) = _rowwise(lambda p, g: (_rms(p[:, :B_Q_LORA], g),), [proj_b], [w["b_q_norm"]], [(B_Q_LORA, BF16)],
                      tm=tm, name="b_q_norm")

    q_b = _matmul(c_q, w_q_p, out_dtype=BF16, tm=mm, tn=1024, name="b_q_up", epilogue=rope_epilogue,
                  epilogue_rows=EPILOGUE_ROWS,
                  extras=tab_extras(tb2, mm))
    q_b3, kpad3, v_b3 = q_b.reshape(bsz, t, B_KPAD), kpad.reshape(bsz, t, B_KPAD), v_b.reshape(bsz, t, -1)
    w_vt = jnp.pad(wu[:, :, B_NOPE:], ((0, 0), (0, 0), (0, LANES - B_VDIM))).reshape(B_KV_LORA, B_KPAD)
    vt1 = _value_heads_t(c_kv, w_vt, bsz, t)
    o_b, lse_b = _attn_b_fwd(q_b3, kpad3, vt1)
    o_b2 = o_b.reshape(n, -1)
    (y_b,) = _rowwise(lambda o, p: (o * _silu(p[:, B_Q_LORA:]),), [o_b2, proj_b], [], [(D_MODEL, BF16)],
                      tm=tm, name="b_gate_mul")
    out_b = _matmul(y_b, w_b_out, out_dtype=F32, tm=mm, tn=D_MODEL, name="b_out")

    def head_fn(h1t, out, tgt, g):
        e = h1t + _rms(out, g) - tgt
        loss_row = 0.5 * jnp.mean(e * e, axis=-1, keepdims=True)
        dh2 = e * (1.0 / D_MODEL)
        d_out, dg = _rms_bwd(out, g, dh2)
        return dh2, d_out, dg, jnp.broadcast_to(loss_row * (1.0 / LANES), (loss_row.shape[0], LANES))

    dh2, d_out_b, dg_b_post, loss_acc = _rowwise(
        head_fn, [h1, out_b, tgt2], [w["b_post_norm"]], [(D_MODEL, F32), (D_MODEL, BF16)], [D_MODEL, LANES],
        tm=tm, name="loss_head")

    dy_b = _matmul(d_out_b, w_b_out, tb=True, out_dtype=BF16, tm=mm, tn=D_MODEL, name="b_out_dx")
    gw_b_out = _matmul(y_b, d_out_b, ta=True, out_dtype=BF16, tm=mm, tn=D_MODEL, tk=2048, name="b_out_dw")

    def gate_b_bwd(dy, o, p):
        z = p[:, B_Q_LORA:]
        return dy * _silu(z), dy * o * _silu_grad(z)

    do_b, dz_b = _rowwise(gate_b_bwd, [dy_b, o_b2, proj_b], [], [(D_MODEL, BF16), (D_MODEL, F32)],
                          tm=tm, name="b_gate_bwd")
    do_b3 = do_b.reshape(bsz, t, -1)
    dq_b, dk_b, dv_b = _attn_b_bwd(q_b3, kpad3, v_b3, do_b3, o_b, lse_b, tabs_b)
    dq_b2, dk_b2, dv_b2 = dq_b.reshape(n, B_KPAD), dk_b.reshape(n, B_KPAD), dv_b.reshape(n, -1)

    dc_kv = _matmul(dk_b2, w_upk, tb=True, out_dtype=F32, tm=mm, tn=B_KV_LORA, name="kv_up_k_dx")
    dc_kv = _matmul(dv_b2, w_upv, tb=True, out_dtype=BF16, tm=mm, tn=B_KV_LORA, name="kv_up_v_dx",
                    epilogue=_add_epilogue, extras=[(dc_kv, (mm, B_KV_LORA), lambda j, i, kk: (i, j))])
    gw_upk = _matmul(c_kv, dk_b2, ta=True, out_dtype=BF16, tm=B_KV_LORA, tn=1024, tk=2048, name="kv_up_k_dw")
    gw_upv = _matmul(c_kv, dv_b2, ta=True, out_dtype=BF16, tm=B_KV_LORA, tn=1024, tk=2048, name="kv_up_v_dw")

    def latent_bwd(ck, dck, dk, c, sp, sm, g):
        d1, dg = _rms_bwd(ck[:, :B_KV_LORA], g, dck)
        ksum = dk[:, :LANES].astype(F32)
        for h in range(1, B_HEADS):
            ksum = ksum + dk[:, h * LANES:(h + 1) * LANES].astype(F32)
        lane = lax.broadcasted_iota(jnp.int32, ksum.shape, 1)
        ksum = jnp.where((lane >= B_NOPE) & (lane < B_QK_DIM), ksum, 0.0)
        return jnp.concatenate([d1, _rope_t(ksum, c, sp, sm)], axis=1), dg

    dckr, dg_latent = _rowwise(latent_bwd, [ckr, dc_kv, dk_b2, *tb2], [w["kv_latent_norm"]],
                               [(3 * LANES, BF16)], [B_KV_LORA], tm=tm, name="kv_latent_bwd")
    dhn_kv = _matmul(dckr, w_down_p, tb=True, out_dtype=BF16, tm=mm, tn=D_MODEL, name="kv_down_dx")
    gw_down_p = _matmul(hn_kv, dckr, ta=True, out_dtype=BF16, tm=mm, tn=3 * LANES, tk=2048, name="kv_down_dw")

    dc_q = _matmul(dq_b2, w_q_p, tb=True, out_dtype=BF16, tm=mm, tn=B_Q_LORA, name="b_q_up_dx")
    gw_q_p = _matmul(c_q, dq_b2, ta=True, out_dtype=BF16, tm=B_Q_LORA, tn=1024, tk=2048, name="b_q_up_dw")

    def q_norm_bwd(p, dcq, dz, g):
        d1, dg = _rms_bwd(p[:, :B_Q_LORA], g, dcq)
        return jnp.concatenate([d1, dz], axis=1), dg

    dproj_b, dg_q_norm = _rowwise(q_norm_bwd, [proj_b, dc_q, dz_b], [w["b_q_norm"]],
                                  [(w_b_in.shape[1], BF16)], [B_Q_LORA], tm=tm, name="b_q_norm_bwd")
    dhn_b = _matmul(dproj_b, w_b_in, tb=True, out_dtype=BF16, tm=mm, tn=D_MODEL, name="b_in_dx")
    gw_b_in = _matmul(hn_b, dproj_b, ta=True, out_dtype=BF16, tm=mm, tn=w_b_in.shape[1], tk=2048, name="b_in_dw")

    def mid_bwd(h1t, dh2t, dkv, db, g_kv, g_b, g_post, out):
        dxa, ra = _rms_bwd(h1t, g_kv, dkv)
        dxb, rb = _rms_bwd(h1t, g_b, db)
        dh1 = dh2t + dxa + dxb
        d_out, rp = _rms_bwd(out, g_post, dh1)
        return dh1, d_out, ra, rb, rp

    def mid_bwd_fn(h1t, dh2t, dkv, db, out, g_kv, g_b, g_post):
        return mid_bwd(h1t, dh2t, dkv, db, g_kv, g_b, g_post, out)

    dh1, d_out_a, dg_kv, dg_b_pre, dg_a_post = _rowwise(
        mid_bwd_fn, [h1, dh2, dhn_kv, dhn_b, out_a], [w["kv_norm"], w["b_pre_norm"], w["a_post_norm"]],
        [(D_MODEL, F32), (D_MODEL, BF16)], [D_MODEL] * 3, tm=tm, name="mid_bwd")

    dy_a = _matmul(d_out_a, w_a_out, tb=True, out_dtype=BF16, tm=mm, tn=A_WIDTH, name="a_out_dx")
    gw_a_out = _matmul(y_a, d_out_a, ta=True, out_dtype=BF16, tm=mm, tn=D_MODEL, tk=2048, name="a_out_dw")

    def gate_a_bwd(dy, o, z):
        do = dy * _silu(z)
        prod = do * o
        lane = lax.broadcasted_iota(jnp.int32, (prod.shape[0], A_HEADS), 1)
        dsum = jnp.zeros((prod.shape[0], A_HEADS), F32)
        for h in range(A_HEADS):
            col = jnp.sum(prod[:, h * A_HEAD_DIM:(h + 1) * A_HEAD_DIM], axis=1, keepdims=True)
            dsum = jnp.where(lane == h, col, dsum)
        return do, dy * o * _silu_grad(z), dsum

    do_a, dz_a, dsum_a = _rowwise(gate_a_bwd, [dy_a, o_a2, z_a], [],
                                  [(A_WIDTH, BF16), (A_WIDTH, BF16), (A_HEADS, F32)], tm=tm, name="a_gate_bwd")
    dhn_a = _matmul(dz_a, w_a_in, b_cols=(A_QKV, A_WIDTH), tb=True, out_dtype=F32, tm=mm, tn=D_MODEL,
                    name="a_gate_dx")
    gw_a_in, dhn_groups = None, []
    for g, d in enumerate(A_DILATIONS):
        s_n, ln = bsz * d, t // d
        dqkv = _attn_a_bwd(qk_g[g], v_g[g], to_group(do_a, d).reshape(s_n, ln, A_WIDTH), cols_to_rows(lse_a, d),
                           cols_to_rows(dsum_a, d), [a.reshape(s_n, ln, LANES) for a in tabs_g[g]],
                           f"attn_a_bwd_g{g}").reshape(n, 3 * A_WIDTH)
        w_cols = (g * 3 * A_WIDTH, 3 * A_WIDTH)
        if d == 1:
            dhn_a = _matmul(dqkv, w_a_in, b_cols=w_cols, tb=True, out_dtype=F32, tm=mm, tn=D_MODEL,
                            name=f"a_qkv_dx_g{g}", epilogue=_add_epilogue,
                            extras=[(dhn_a, (mm, D_MODEL), lambda j, i, kk: (i, j))])
        else:
            dhn_groups.append(from_group(_matmul(dqkv, w_a_in, b_cols=w_cols, tb=True, out_dtype=BF16, tm=mm,
                                                 tn=D_MODEL, name=f"a_qkv_dx_g{g}"), d))
        gw_a_in = _matmul(hn_g[g], dqkv, ta=True, out_dtype=BF16, tm=mm, tn=1024, tk=2048, name=f"a_qkv_dw_g{g}",
                          out_into=(gw_a_in, g * 3 * A_WIDTH, A_QKV + A_WIDTH))
    gw_a_in = _matmul(hn_a, dz_a, ta=True, out_dtype=BF16, tm=mm, tn=1024, tk=2048, name="a_gate_dw",
                      out_into=(gw_a_in, A_QKV, A_QKV + A_WIDTH))

    def first_bwd(xt, dhn, dhn_1, dhn_2, dh1t, g):
        dx, dg = _rms_bwd(xt, g, dhn + dhn_1 + dhn_2)
        return dh1t + dx, dg

    grad_x, dg_a_pre = _rowwise(first_bwd, [x2, dhn_a, *dhn_groups, dh1], [w["a_pre_norm"]], [(D_MODEL, F32)],
                                [D_MODEL], tm=tm, name="a_pre_norm_bwd")

    gw_down = jnp.concatenate([gw_down_p[:, :B_KV_LORA], gw_down_p[:, B_KV_LORA + B_NOPE:B_KV_LORA + B_QK_DIM]], axis=1)
    gw_up = jnp.concatenate([gw_upk.reshape(B_KV_LORA, B_HEADS, LANES)[:, :, :B_NOPE],
                             gw_upv.reshape(B_KV_LORA, B_HEADS, B_VDIM)], axis=2).reshape(B_KV_LORA, -1)
    gw_q_up = gw_q_p.reshape(B_Q_LORA, B_HEADS, LANES)[:, :, :B_QK_DIM].reshape(B_Q_LORA, -1)
    grads = {"a_w_in": gw_a_in, "a_w_out": gw_a_out, "kv_w_down": gw_down, "kv_w_up": gw_up,
             "b_w_in": gw_b_in, "b_w_q_up": gw_q_up, "b_w_out": gw_b_out}
    gains = {"a_pre_norm": dg_a_pre, "a_post_norm": dg_a_post, "kv_norm": dg_kv, "kv_latent_norm": dg_latent,
             "b_pre_norm": dg_b_pre, "b_q_norm": dg_q_norm, "b_post_norm": dg_b_post}
    gains = {k: jnp.sum(a, axis=0) for k, a in gains.items()}
    return jnp.sum(loss_acc), grad_x.reshape(bsz, t, D_MODEL), grads, gains


WEIGHT_ORDER = ("a_pre_norm", "a_w_in", "a_w_out", "a_post_norm", "kv_norm", "kv_w_down", "kv_latent_norm",
                "kv_w_up", "b_pre_norm", "b_w_in", "b_q_norm", "b_w_q_up", "b_w_out", "b_post_norm")
MATRICES = (("a_w_in", 1024, 10240, 1), ("a_w_out", 1024, 1024, 0), ("kv_w_down", 1024, 288, 0),
            ("kv_w_up", 256, 2048, 1), ("b_w_in", 1024, 1408, 1), ("b_w_q_up", 384, 1536, 1),
            ("b_w_out", 1024, 1024, 0))
SHARDED_GAINS = ("a_pre_norm", "a_post_norm")
GAIN_WIDTHS = (("a_pre_norm", 1024), ("a_post_norm", 1024), ("kv_norm", 1024), ("kv_latent_norm", 256),
               ("b_pre_norm", 1024), ("b_q_norm", 384), ("b_post_norm", 1024))
GAIN_ROWS = 48


def _shard_rows(rows, cols):
    return rows * cols // (N_DEV * LANES)


def _whole_from_blocks(blocks, rows, cols, axis):
    if axis == 1:
        return blocks.reshape(N_DEV, rows, cols // N_DEV).transpose(1, 0, 2).reshape(rows, cols)
    return blocks.reshape(rows, cols)


def _blocks_from_whole(whole, rows, cols, axis):
    if axis == 1:
        whole = whole.reshape(rows, N_DEV, cols // N_DEV).transpose(1, 0, 2)
    return whole.reshape(N_DEV, -1, LANES)


def kernel(x, positions, a_pre_norm, a_w_in, a_w_out, a_post_norm, kv_norm, kv_w_down, kv_latent_norm, kv_w_up, b_pre_norm, b_w_in, b_q_norm, b_w_q_up, b_w_out, b_post_norm, loss_target, m_a_pre_norm, m_a_w_in, m_a_w_out, m_a_post_norm, m_kv_norm, m_kv_w_down, m_kv_latent_norm, m_kv_w_up, m_b_pre_norm, m_b_w_in, m_b_q_norm, m_b_w_q_up, m_b_w_out, m_b_post_norm, v_a_pre_norm, v_a_w_in, v_a_w_out, v_a_post_norm, v_kv_norm, v_kv_w_down, v_kv_latent_norm, v_kv_w_up, v_b_pre_norm, v_b_w_in, v_b_q_norm, v_b_w_q_up, v_b_w_out, v_b_post_norm):
    weights = dict(a_pre_norm=a_pre_norm, a_w_in=a_w_in, a_w_out=a_w_out, a_post_norm=a_post_norm, kv_norm=kv_norm,
                   kv_w_down=kv_w_down, kv_latent_norm=kv_latent_norm, kv_w_up=kv_w_up, b_pre_norm=b_pre_norm,
                   b_w_in=b_w_in, b_q_norm=b_q_norm, b_w_q_up=b_w_q_up, b_w_out=b_w_out, b_post_norm=b_post_norm)
    m_in = dict(a_pre_norm=m_a_pre_norm, a_w_in=m_a_w_in, a_w_out=m_a_w_out, a_post_norm=m_a_post_norm,
                kv_norm=m_kv_norm, kv_w_down=m_kv_w_down, kv_latent_norm=m_kv_latent_norm, kv_w_up=m_kv_w_up,
                b_pre_norm=m_b_pre_norm, b_w_in=m_b_w_in, b_q_norm=m_b_q_norm, b_w_q_up=m_b_w_q_up,
                b_w_out=m_b_w_out, b_post_norm=m_b_post_norm)
    v_in = dict(a_pre_norm=v_a_pre_norm, a_w_in=v_a_w_in, a_w_out=v_a_w_out, a_post_norm=v_a_post_norm,
                kv_norm=v_kv_norm, kv_w_down=v_kv_w_down, kv_latent_norm=v_kv_latent_norm, kv_w_up=v_kv_w_up,
                b_pre_norm=v_b_pre_norm, b_w_in=v_b_w_in, b_q_norm=v_b_q_norm, b_w_q_up=v_b_w_q_up,
                b_w_out=v_b_w_out, b_post_norm=v_b_post_norm)
    me = 4 * lax.axis_index("x") + 2 * lax.axis_index("y") + lax.axis_index("c")

    wide = MATRICES[0][0]
    flat = jnp.concatenate([weights[name].astype(BF16).reshape(-1, LANES) for name, _, _, _ in MATRICES[1:]], axis=0)
    gathered, w_wide = _all_gather_weights(flat, weights[wide][0].astype(BF16), "gather_weights")
    whole = {wide: w_wide}
    off = 0
    for name, rows, cols, axis in MATRICES[1:]:
        nr = _shard_rows(rows, cols)
        whole[name] = _whole_from_blocks(gathered[:, off:off + nr], rows, cols, axis)
        off += nr
    gain_shard = jnp.concatenate([weights[name].reshape(1, LANES) for name in SHARDED_GAINS]
                                 + [jnp.zeros((8 - len(SHARDED_GAINS), LANES), F32)], axis=0)
    gain_blocks = _all_gather(gain_shard, "gather_gains")
    for i, name in enumerate(SHARDED_GAINS):
        whole[name] = gain_blocks[:, i, :].reshape(1, D_MODEL)
    for name in ("kv_norm", "kv_latent_norm", "b_pre_norm", "b_q_norm", "b_post_norm"):
        whole[name] = weights[name].reshape(1, -1)

    loss_part, grad_x, grads, gains = _local_step(x, positions, loss_target, whole)

    blocks = jnp.concatenate([_blocks_from_whole(grads[name], rows, cols, axis).astype(BF16)
                              for name, rows, cols, axis in MATRICES], axis=1)
    blocks = blocks.reshape(N_CHIPS, 2, blocks.shape[1], LANES)
    got = _exchange_sibling(blocks, "scatter_grads_core")
    core = lax.axis_index("c").astype(jnp.int32).reshape(1)
    landed = _exchange_chips(_add_pairs(blocks, got, core, tr=2512, name="add_core_grads"), "scatter_grads_chip")
    summed = _sum_slots(landed, tr=2512, name="sum_grads")
    grad_out = {}
    off = 0
    for name, rows, cols, axis in MATRICES:
        nr = _shard_rows(rows, cols)
        grad_out[name] = summed[off:off + nr].reshape(weights[name].shape)
        off += nr

    vec = jnp.concatenate([gains[name] for name, _ in GAIN_WIDTHS] + [jnp.full((LANES,), loss_part, F32)])
    vec = jnp.pad(vec, (0, GAIN_ROWS * LANES - vec.shape[0])).reshape(GAIN_ROWS, LANES)
    total = _sum_slots(_all_gather(vec, "gather_gain_grads"), tr=GAIN_ROWS, name="sum_gain_grads").reshape(-1)
    off = 0
    for name, width in GAIN_WIDTHS:
        g = total[off:off + width]
        if name in SHARDED_GAINS:
            g = lax.dynamic_slice(g, (me * LANES,), (LANES,))
        grad_out[name] = g.reshape(weights[name].shape)
        off += width
    loss = total[off]

    deltas, new_m, new_v = {}, {}, {}
    big = "a_w_in"
    deltas[big], new_m[big], new_v[big] = _adamw(weights[big], grad_out[big], m_in[big], v_in[big], "adamw_" + big)
    small = [name for name in WEIGHT_ORDER if name != big]
    res = _adamw_small(*[[d[name] for name in small] for d in (weights, grad_out, m_in, v_in)], "adamw_small")
    for out, vals in zip((deltas, new_m, new_v), res):
        out.update(zip(small, vals))
    return (loss, grad_x, *[grad_out[k] for k in WEIGHT_ORDER], *[deltas[k] for k in WEIGHT_ORDER],
            *[new_m[k] for k in WEIGHT_ORDER], *[new_v[k] for k in WEIGHT_ORDER])
```

```python
import jax
import jax.numpy as jnp
from jax import lax
from jax.experimental import pallas as pl
from jax.experimental.pallas import tpu as pltpu

F32 = jnp.float32
BF16 = jnp.bfloat16

N_DEV = 8
D_MODEL = 1024
NORM_EPS = 1e-6
A_GROUPS = 3
A_DILATIONS = (1, 4, 16)
A_HEADS = 8
A_HEAD_DIM = 128
A_WIDTH = 1024
A_ROT_DIM = 32
A_ROPE_THETA = 500000.0
A_QKV = A_GROUPS * 3 * A_WIDTH
B_HEADS = 16
B_NOPE = 64
B_ROPE = 32
B_QK_DIM = 96
B_VDIM = 64
B_Q_LORA = 384
B_KV_LORA = 256
B_ROPE_THETA = 10000.0
B_KPAD = B_HEADS * 128
ADAM_LR = 0.001
ADAM_B1 = 0.9
ADAM_B2 = 0.999
ADAM_EPS = 1e-08
ADAM_WD = 0.01
ADAM_STEP = 10

LANES = 128
BAND = 128
EPILOGUE_ROWS = 128
NEG = -1e30
VMEM_LIMIT = 56 * 1024 * 1024
MESH = pl.DeviceIdType.MESH


def _cparams(sem):
    return pltpu.CompilerParams(dimension_semantics=sem, vmem_limit_bytes=VMEM_LIMIT)


def _rowwise(fn, rows, bcast, outs, accs=(), *, tm, name):
    n = rows[0].shape[0]
    nr, nb, no = len(rows), len(bcast), len(outs)

    def body(*refs):
        res = fn(*[r[...] for r in refs[:nr + nb]])
        out_refs = refs[nr + nb:nr + nb + no]
        acc_refs = refs[nr + nb + no:]
        for r, v in zip(out_refs, res[:no]):
            r[...] = v.astype(r.dtype)
        if acc_refs:
            @pl.when(pl.program_id(0) == 0)
            def _():
                for r in acc_refs:
                    r[...] = jnp.zeros_like(r)
            for r, v in zip(acc_refs, res[no:]):
                r[...] += v.reshape(tm // 8, 8, v.shape[-1]).sum(axis=0)

    in_specs = [pl.BlockSpec((tm, a.shape[1]), lambda i: (i, 0)) for a in rows]
    in_specs += [pl.BlockSpec(a.shape, lambda i: (0, 0)) for a in bcast]
    out_specs = [pl.BlockSpec((tm, c), lambda i: (i, 0)) for c, _ in outs]
    out_specs += [pl.BlockSpec((8, c), lambda i: (0, 0)) for c in accs]
    out_shape = [jax.ShapeDtypeStruct((n, c), dt) for c, dt in outs]
    out_shape += [jax.ShapeDtypeStruct((8, c), F32) for c in accs]
    return pl.pallas_call(
        body, name=name, grid=(n // tm,), in_specs=in_specs, out_specs=out_specs, out_shape=out_shape,
        compiler_params=_cparams(("arbitrary",)))(*rows, *bcast)


def _matmul(a, b, *, out_dtype, tm, tn, tk=None, name, epilogue=None, extras=(), ta=False, tb=False,
            epilogue_rows=None, b_cols=None, out_into=None):
    epilogue_rows = epilogue_rows or tm
    k, m = a.shape[::-1] if not ta else a.shape
    col0, width = b_cols or (0, b.shape[1])
    n = b.shape[0] if tb else width
    tk = tk or k
    nk = k // tk
    b_off = col0 // (tk if tb else tn)
    assert col0 % (tk if tb else tn) == 0 and (k == width if tb else True)
    ne = len(extras)
    dot = _dot_tn if ta else (_dot_nt if tb else _dot)
    assert epilogue is None or (nk == 1 and not ta)

    def body(*refs):
        a_ref, b_ref = refs[:2]
        ex = refs[2:2 + ne]
        o_ref = refs[2 + ne + (1 if out_into is not None and out_into[0] is not None else 0)]
        if epilogue is not None:
            b_tile = b_ref[...].astype(BF16)
            for r0 in range(0, tm, epilogue_rows):
                rows = slice(r0, r0 + epilogue_rows)
                epilogue(dot(a_ref[rows, :].astype(BF16), b_tile), o_ref, rows, *ex)
            return
        part = dot(a_ref[...].astype(BF16), b_ref[...].astype(BF16))
        if nk == 1:
            o_ref[...] = part.astype(o_ref.dtype)
        else:
            acc_ref = refs[-1]
            kk = pl.program_id(2)

            @pl.when(kk == 0)
            def _():
                acc_ref[...] = part

            @pl.when(kk > 0)
            def _():
                acc_ref[...] += part

            @pl.when(kk == nk - 1)
            def _():
                o_ref[...] = acc_ref[...].astype(o_ref.dtype)

    a_spec = pl.BlockSpec((tk, tm), lambda j, i, kk: (kk, i)) if ta else pl.BlockSpec((tm, tk), lambda j, i, kk: (i, kk))
    b_spec = (pl.BlockSpec((tn, tk), lambda j, i, kk: (j, kk + b_off)) if tb
              else pl.BlockSpec((tk, tn), lambda j, i, kk: (kk, j + b_off)))
    in_specs = [a_spec, b_spec] + [pl.BlockSpec(bs, im) for _, bs, im in extras]
    operands = [a, b] + [e[0] for e in extras]
    aliases = {}
    if out_into is not None:
        prev, out0, n_total = out_into
        o_off = out0 // tn
        assert out0 % tn == 0
        if prev is not None:
            in_specs.append(ANY)
            operands.append(prev)
            aliases = {len(operands) - 1: 0}
    else:
        o_off, n_total = 0, n
    return pl.pallas_call(
        body, name=name, grid=(n // tn, m // tm, nk), in_specs=in_specs,
        out_specs=pl.BlockSpec((tm, tn), lambda j, i, kk: (i, j + o_off)),
        out_shape=jax.ShapeDtypeStruct((m, n_total), out_dtype), input_output_aliases=aliases,
        scratch_shapes=[pltpu.VMEM((tm, tn), F32)] if nk > 1 else [],
        compiler_params=_cparams(("parallel", "parallel", "arbitrary")))(*operands)


def _add_epilogue(acc, o_ref, rows, prev_ref):
    o_ref[rows, :] = (acc + prev_ref[rows, :]).astype(o_ref.dtype)


def _rope(x, c, sp, sm):
    return x * c + pltpu.roll(x, 16, 1) * sp + pltpu.roll(x, LANES - 16, 1) * sm


def _rope_t(dy, c, sp, sm):
    return dy * c + pltpu.roll(dy * sp, LANES - 16, 1) + pltpu.roll(dy * sm, 16, 1)


def _rope_tables(positions, theta, rot_dim, lane0, name):
    n = positions.shape[0]
    half = rot_dim // 2
    inv_freq = 1.0 / (theta ** (jnp.arange(half, dtype=F32) * (2.0 / rot_dim)))
    freq = jnp.concatenate([jnp.zeros((lane0,), F32), inv_freq, inv_freq,
                            jnp.zeros((LANES - lane0 - rot_dim,), F32)]).reshape(1, LANES)

    def fn(p, f):
        ang = p * f
        cos, sin = jnp.cos(ang), jnp.sin(ang)
        lane = lax.broadcasted_iota(jnp.int32, ang.shape, 1) - lane0
        first, second = (lane >= 0) & (lane < half), (lane >= half) & (lane < rot_dim)
        return jnp.where(first | second, cos, 1.0), jnp.where(second, sin, 0.0), jnp.where(first, -sin, 0.0)

    return _rowwise(fn, [positions.astype(F32).reshape(n, 1)], [freq], [(LANES, F32)] * 3, tm=512, name=name)


def _value_heads_t(c_kv, w_vt, bsz, t):
    n, k = c_kv.shape
    tm, tn = 512, 1024

    def body(a_ref, b_ref, o_ref):
        acc = _dot(a_ref[...], b_ref[...])
        lane = lax.broadcasted_iota(jnp.int32, (1, tn), 1)
        acc = acc + jnp.where(lax.rem(lane, LANES) == B_VDIM, 1.0, 0.0)
        o_ref[...] = acc.T.astype(BF16)

    per_seq = t // tm
    return pl.pallas_call(
        body, name="kv_up_v_t", grid=(w_vt.shape[1] // tn, n // tm),
        in_specs=[pl.BlockSpec((tm, k), lambda j, i: (i, 0)), pl.BlockSpec((k, tn), lambda j, i: (0, j))],
        out_specs=pl.BlockSpec((None, tn, tm), lambda j, i: (i // per_seq, j, lax.rem(i, per_seq))),
        out_shape=jax.ShapeDtypeStruct((bsz, w_vt.shape[1], t), BF16),
        compiler_params=_cparams(("parallel", "parallel")))(c_kv, w_vt)


def _rms(x, g):
    xf = x.astype(F32)
    return xf * lax.rsqrt(jnp.mean(xf * xf, axis=-1, keepdims=True) + NORM_EPS) * g


def _rms_bwd(x, g, dy):
    xf = x.astype(F32)
    rstd = lax.rsqrt(jnp.mean(xf * xf, axis=-1, keepdims=True) + NORM_EPS)
    xhat = xf * rstd
    dxhat = dy * g
    dx = rstd * (dxhat - xhat * jnp.mean(dxhat * xhat, axis=-1, keepdims=True))
    return dx, dy * xhat


def _silu(z):
    return z * jax.nn.sigmoid(z)


def _silu_grad(z):
    s = jax.nn.sigmoid(z)
    return s * (1.0 + z * (1.0 - s))


def _dot_nt(a, b):
    return lax.dot_general(a, b, (((1,), (1,)), ((), ())), preferred_element_type=F32)


def _dot_tn(a, b):
    return lax.dot_general(a, b, (((0,), (0,)), ((), ())), preferred_element_type=F32)


def _dot(a, b):
    return jnp.dot(a, b, preferred_element_type=F32)


A_SCALE = A_HEAD_DIM ** -0.5
LOG2E = 1.4426950408889634
A_C2 = A_SCALE * LOG2E
A_HEAD_GROUP = 4
A_FWD_HEAD_GROUP = 8
A_BLOCKS_PER_STEP = 4
A_SEQS_PER_STEP = 4


def _a_steps(nb):
    return (min(nb, A_BLOCKS_PER_STEP), 1) if nb > 1 else (1, A_SEQS_PER_STEP)


def _attn_a_fwd(qk, v, name):
    s_n, ln, _ = qk.shape
    nb = ln // BAND
    lb, sb = _a_steps(nb)
    single = nb == 1
    rows = lb * BAND
    n_keys = BAND if single else 2 * BAND

    def body(q_ref, kc_ref, kp_ref, vc_ref, vp_ref, o_ref, lse_ref):
        kpos = lax.broadcasted_iota(jnp.int32, (n_keys, BAND), 0)
        qpos = lax.broadcasted_iota(jnp.int32, (n_keys, BAND), 1) + (n_keys - BAND)
        band = (kpos <= qpos) & (kpos >= qpos - BAND)
        at_start = band & (kpos >= jnp.where(pl.program_id(1) > 0, 0, n_keys - BAND))
        for s in range(sb):
            for u in range(lb):
                r = slice(u * BAND, (u + 1) * BAND)
                mask = at_start if u == 0 else band

                def keys(prev_ref, cur_ref, hs):
                    cur = cur_ref[s, r, hs]
                    if single:
                        return cur
                    prev = prev_ref[s, :, hs] if u == 0 else cur_ref[s, (u - 1) * BAND:u * BAND, hs]
                    return jnp.concatenate([prev, cur], axis=0)

                stats = []
                for h0 in range(0, A_HEADS, A_FWD_HEAD_GROUP):
                    hss = [slice(h * A_HEAD_DIM, (h + 1) * A_HEAD_DIM) for h in range(h0, h0 + A_FWD_HEAD_GROUP)]
                    sts = [_dot_nt(keys(kp_ref, kc_ref, hs), q_ref[s, r, hs]) for hs in hss]
                    ps, ls = [], []
                    for st in sts:
                        st = jnp.where(mask, st * A_C2, NEG)
                        m = jnp.max(st, axis=0, keepdims=True)
                        p = jnp.exp2(st - m)
                        l_row = jnp.sum(p, axis=0, keepdims=True)
                        ps.append(p.astype(BF16))
                        ls.append(l_row)
                        stats.append(m + jnp.log2(l_row))
                    ots = [_dot_tn(keys(vp_ref, vc_ref, hs), p) for hs, p in zip(hss, ps)]
                    for hs, o_t, l_row in zip(hss, ots, ls):
                        o_ref[s, r, hs] = (o_t / l_row).T.astype(BF16)
                lse_ref[s, :, r] = jnp.concatenate(stats, axis=0)

    def cur(c):
        return pl.BlockSpec((sb, rows, A_WIDTH), lambda s, l: (s, l, c))

    def prev(c):
        return pl.BlockSpec((sb, BAND, A_WIDTH), lambda s, l: (s, jnp.maximum(l * lb - 1, 0), c))

    return pl.pallas_call(
        body, name=name, grid=(s_n // sb, nb // lb),
        in_specs=[cur(0), cur(1), prev(1), cur(0), prev(0)],
        out_specs=[cur(0), pl.BlockSpec((sb, A_HEADS, rows), lambda s, l: (s, 0, l))],
        out_shape=[jax.ShapeDtypeStruct((s_n, ln, A_WIDTH), BF16), jax.ShapeDtypeStruct((s_n, A_HEADS, ln), F32)],
        compiler_params=_cparams(("parallel", "arbitrary")))(qk, qk, qk, v, v)


def _attn_a_bwd(qk, v, do, lse2, dsum, tabs, name):
    s_n, ln, _ = qk.shape
    nb = ln // BAND
    lb, sb = _a_steps(nb)
    single = nb == 1
    rows = lb * BAND
    n_steps = nb // lb

    def body(q_ref, qn_ref, kc_ref, kp_ref, vc_ref, vp_ref, do_ref, don_ref, lse_ref, lsen_ref, ds_ref, dsn_ref,
             c_ref, sp_ref, sm_ref, out_ref):
        l_idx = pl.program_id(1)
        kpos = lax.broadcasted_iota(jnp.int32, (2 * BAND, BAND), 0)
        qpos = lax.broadcasted_iota(jnp.int32, (2 * BAND, BAND), 1) + BAND
        band_q = (kpos <= qpos) & (kpos >= qpos - BAND)
        start_q = band_q & (kpos >= jnp.where(l_idx > 0, 0, BAND))
        kpos2 = lax.broadcasted_iota(jnp.int32, (BAND, 2 * BAND), 0)
        qpos2 = lax.broadcasted_iota(jnp.int32, (BAND, 2 * BAND), 1)
        band_k = (kpos2 <= qpos2) & (kpos2 >= qpos2 - BAND)
        end_k = band_k & (qpos2 < jnp.where(l_idx < n_steps - 1, 2 * BAND, BAND))
        causal = kpos[:BAND] <= qpos[:BAND] - BAND
        for s in range(sb):
            for u in range(lb):
                r = slice(u * BAND, (u + 1) * BAND)
                c, sp, sm = c_ref[s, r, :], sp_ref[s, r, :], sm_ref[s, r, :]
                lse_q, ds_q = lse_ref[s, :, r], ds_ref[s, :, r]

                def store(hl, hss, dqs, dks, dvs):
                    for i, h in enumerate(hl):
                        out_ref[s, r, hss[i]] = _rope_t(dqs[i], c, sp, sm).astype(BF16)
                        out_ref[s, r, A_WIDTH + h * A_HEAD_DIM:A_WIDTH + (h + 1) * A_HEAD_DIM] = \
                            _rope_t(dks[i], c, sp, sm).astype(BF16)
                        out_ref[s, r, 2 * A_WIDTH + h * A_HEAD_DIM:2 * A_WIDTH + (h + 1) * A_HEAD_DIM] = \
                            dvs[i].astype(BF16)

                if single:
                    for h0 in range(0, A_HEADS, A_HEAD_GROUP):
                        hl = list(range(h0, h0 + A_HEAD_GROUP))
                        hss = [slice(h * A_HEAD_DIM, (h + 1) * A_HEAD_DIM) for h in hl]
                        sts = [_dot_nt(kc_ref[s, r, hs], q_ref[s, r, hs]) for hs in hss]
                        dpts = [_dot_nt(vc_ref[s, r, hs], do_ref[s, r, hs]) for hs in hss]
                        ps, dsts = [], []
                        for i, h in enumerate(hl):
                            p = jnp.exp2(jnp.where(causal, sts[i] * A_C2, NEG) - lse_q[h:h + 1])
                            dsts.append((p * (dpts[i] - ds_q[h:h + 1]) * A_SCALE).astype(BF16))
                            ps.append(p.astype(BF16))
                        store(hl, hss, [_dot_tn(dst, kc_ref[s, r, hs]) for dst, hs in zip(dsts, hss)],
                              [_dot(dst, q_ref[s, r, hs]) for dst, hs in zip(dsts, hss)],
                              [_dot(p, do_ref[s, r, hs]) for p, hs in zip(ps, hss)])
                    continue

                r_prev = slice((u - 1) * BAND, u * BAND)
                r_next = slice((u + 1) * BAND, (u + 2) * BAND)

                def before(prev_ref, cur_ref, hs):
                    return prev_ref[s, :, hs] if u == 0 else cur_ref[s, r_prev, hs]

                def after(next_ref, cur_ref, hs):
                    return next_ref[s, :, hs] if u == lb - 1 else cur_ref[s, r_next, hs]

                mask_q = start_q if u == 0 else band_q
                mask_k = end_k if u == lb - 1 else band_k
                lse_n = lsen_ref[s] if u == lb - 1 else lse_ref[s, :, r_next]
                ds_n = dsn_ref[s] if u == lb - 1 else ds_ref[s, :, r_next]
                lse_k = jnp.concatenate([lse_q, lse_n], axis=1)
                ds_k = jnp.concatenate([ds_q, ds_n], axis=1)
                for h0 in range(0, A_HEADS, A_HEAD_GROUP):
                    hl = list(range(h0, h0 + A_HEAD_GROUP))
                    hss = [slice(h * A_HEAD_DIM, (h + 1) * A_HEAD_DIM) for h in hl]
                    k2s = [jnp.concatenate([before(kp_ref, kc_ref, hs), kc_ref[s, r, hs]], axis=0) for hs in hss]
                    v2s = [jnp.concatenate([before(vp_ref, vc_ref, hs), vc_ref[s, r, hs]], axis=0) for hs in hss]
                    q2s = [jnp.concatenate([q_ref[s, r, hs], after(qn_ref, q_ref, hs)], axis=0) for hs in hss]
                    do2s = [jnp.concatenate([do_ref[s, r, hs], after(don_ref, do_ref, hs)], axis=0) for hs in hss]
                    sts = [_dot_nt(k2, q_ref[s, r, hs]) for k2, hs in zip(k2s, hss)]
                    dpts = [_dot_nt(v2, do_ref[s, r, hs]) for v2, hs in zip(v2s, hss)]
                    st2s = [_dot_nt(kc_ref[s, r, hs], q2) for q2, hs in zip(q2s, hss)]
                    dpt2s = [_dot_nt(vc_ref[s, r, hs], do2) for do2, hs in zip(do2s, hss)]
                    dsts, dst2s, p2s = [], [], []
                    for i, h in enumerate(hl):
                        p = jnp.exp2(jnp.where(mask_q, sts[i] * A_C2, NEG) - lse_q[h:h + 1])
                        dsts.append((p * (dpts[i] - ds_q[h:h + 1]) * A_SCALE).astype(BF16))
                        p2 = jnp.exp2(jnp.where(mask_k, st2s[i] * A_C2, NEG) - lse_k[h:h + 1])
                        dst2s.append((p2 * (dpt2s[i] - ds_k[h:h + 1]) * A_SCALE).astype(BF16))
                        p2s.append(p2.astype(BF16))
                    store(hl, hss, [_dot_tn(dsts[i], k2s[i]) for i in range(A_HEAD_GROUP)],
                          [_dot(dst2s[i], q2s[i]) for i in range(A_HEAD_GROUP)],
                          [_dot(p2s[i], do2s[i]) for i in range(A_HEAD_GROUP)])

    def cur(c, width=A_WIDTH):
        return pl.BlockSpec((sb, rows, width), lambda s, l: (s, l, c))

    def prev(c):
        return pl.BlockSpec((sb, BAND, A_WIDTH), lambda s, l: (s, jnp.maximum(l * lb - 1, 0), c))

    def nxt(c):
        return pl.BlockSpec((sb, BAND, A_WIDTH), lambda s, l: (s, jnp.minimum((l + 1) * lb, nb - 1), c))

    stat = pl.BlockSpec((sb, A_HEADS, rows), lambda s, l: (s, 0, l))
    stat_next = pl.BlockSpec((sb, A_HEADS, BAND), lambda s, l: (s, 0, jnp.minimum((l + 1) * lb, nb - 1)))
    tspec = cur(0, LANES)
    in_specs = [cur(0), nxt(0), cur(1), prev(1), cur(0), prev(0), cur(0), nxt(0),
                stat, stat_next, stat, stat_next, tspec, tspec, tspec]
    return pl.pallas_call(
        body, name=name, grid=(s_n // sb, n_steps), in_specs=in_specs,
        out_specs=cur(0, 3 * A_WIDTH),
        out_shape=jax.ShapeDtypeStruct((s_n, ln, 3 * A_WIDTH), BF16),
        compiler_params=_cparams(("parallel", "arbitrary")))(
            qk, qk, qk, qk, v, v, do, do, lse2, lse2, dsum, dsum, *tabs)


B_TQ = 256
B_FWD_HEADS = 4
B_BWD_PAIRS = 2
B_SCALE = B_QK_DIM ** -0.5
B_C2 = B_SCALE * LOG2E


def _key_le_query(kb, qb, tk, tq):
    kpos = kb * tk + lax.broadcasted_iota(jnp.int32, (tk, tq), 0)
    qpos = qb * tq + lax.broadcasted_iota(jnp.int32, (tk, tq), 1)
    return kpos <= qpos


def _attn_b_fwd(q, kpad, vt1):
    bsz, t, _ = q.shape
    tq = tk = B_TQ
    nq = t // tq
    nh = B_FWD_HEADS

    def body(q_ref, k_ref, vt_ref, o_ref, lse_ref):
        qblk = pl.program_id(2)
        qs = [q_ref[:, hh * LANES:(hh + 1) * LANES] for hh in range(nh)]

        def scores(kb):
            start = pl.multiple_of(kb * tk, tk)
            return [_dot_nt(k_ref[pl.ds(start, tk), hh * LANES:(hh + 1) * LANES], qs[hh]) for hh in range(nh)]

        def pv(kb, ps):
            start = pl.multiple_of(kb * tk, tk)
            return [_dot(vt_ref[hh * LANES:(hh + 1) * LANES, pl.ds(start, tk)], ps[hh]) for hh in range(nh)]

        def softmax(ss, ms, accs, kb, masked):
            out_m, out_acc, out_p = [], [], []
            for hh in range(nh):
                s = ss[hh] * B_C2
                if masked:
                    s = jnp.where(_key_le_query(kb, qblk, tk, tq), s, NEG)
                m_new = jnp.maximum(ms[hh], jnp.max(s, axis=0, keepdims=True))
                out_acc.append(jnp.exp2(ms[hh] - m_new) * accs[hh])
                out_p.append(jnp.exp2(s - m_new).astype(BF16))
                out_m.append(m_new)
            return out_m, out_acc, out_p

        def step(kb, carry):
            ss, ps, ms, accs = carry
            pvs = pv(jnp.maximum(kb - 1, 0), ps)
            ss_next = scores(kb + 1)
            accs = [accs[hh] + pvs[hh] for hh in range(nh)]
            ms, accs, ps = softmax(ss, ms, accs, kb, False)
            return (ss_next, ps, ms, accs)

        init = (scores(0), [jnp.zeros((tk, tq), BF16)] * nh, [jnp.full((1, tq), NEG, F32)] * nh,
                [jnp.zeros((LANES, tq), F32)] * nh)
        ss, ps, ms, accs = lax.fori_loop(0, qblk, step, init)
        pvs = pv(jnp.maximum(qblk - 1, 0), ps)
        accs = [accs[hh] + pvs[hh] for hh in range(nh)]
        ms, accs, ps = softmax(ss, ms, accs, qblk, True)
        pvs = pv(qblk, ps)
        accs = [accs[hh] + pvs[hh] for hh in range(nh)]
        ls = [accs[hh][B_VDIM:B_VDIM + 1] for hh in range(nh)]
        o_t = jnp.concatenate([accs[hh][:B_VDIM] / ls[hh] for hh in range(nh)], axis=0)
        o_ref[...] = o_t.T
        for pair in range(nh // 2):
            lse_ref[pair] = jnp.concatenate([ms[2 * pair + hh] + jnp.log2(ls[2 * pair + hh]) for hh in range(2)]
                                            + [jnp.zeros((6, tq), F32)], axis=0)

    return pl.pallas_call(
        body, name="attn_b_fwd", grid=(bsz, B_HEADS // nh, nq),
        in_specs=[pl.BlockSpec((None, tq, nh * LANES), lambda b, j, i: (b, i, j)),
                  pl.BlockSpec((None, t, nh * LANES), lambda b, j, i: (b, 0, j)),
                  pl.BlockSpec((None, nh * LANES, t), lambda b, j, i: (b, j, 0))],
        out_specs=[pl.BlockSpec((None, tq, nh * B_VDIM), lambda b, j, i: (b, i, j)),
                   pl.BlockSpec((None, nh // 2, 8, tq), lambda b, j, i: (b, j, 0, i))],
        out_shape=[jax.ShapeDtypeStruct((bsz, t, B_HEADS * B_VDIM), F32),
                   jax.ShapeDtypeStruct((bsz, B_HEADS // 2, 8, t), F32)],
        compiler_params=_cparams(("parallel", "parallel", "arbitrary")))(q, kpad, vt1)


def _attn_b_bwd(q, kpad, v, do, o, lse2, tabs):
    bsz, t, _ = q.shape
    tq = tk = B_TQ
    nq = t // tq
    n_pairs = B_BWD_PAIRS
    n_heads = 2 * n_pairs

    def body(q_ref, k_ref, v_ref, do_ref, o_ref, lse_ref, c_ref, sp_ref, sm_ref, dq_ref, dk_ref, dv_ref,
             dqt_scr, dsum_scr):
        lane = lax.broadcasted_iota(jnp.int32, (tk, LANES), 1)
        sel_lane = lax.broadcasted_iota(jnp.int32, (8, LANES), 1)
        sel_row = lax.broadcasted_iota(jnp.int32, (8, LANES), 0)
        sel = jnp.where((sel_lane < B_VDIM) == (sel_row == 0), 1.0, 0.0)
        sel = jnp.where(sel_row < 2, sel, 0.0).astype(BF16)

        def rows(blk):
            return pl.ds(pl.multiple_of(blk * tq, tq), tq)

        def dsum_step(qb, carry):
            for pp in range(n_pairs):
                pls = slice(pp * LANES, (pp + 1) * LANES)
                prod = do_ref[rows(qb), pls].astype(F32) * o_ref[rows(qb), pls]
                hi = prod.astype(BF16)
                lo = (prod - hi.astype(F32)).astype(BF16)
                dsum_scr[pp, :, rows(qb)] = _dot_nt(sel, hi) + _dot_nt(sel, lo)
            return carry

        lax.fori_loop(0, nq, dsum_step, 0)
        dqt_scr[...] = jnp.zeros_like(dqt_scr)

        def kv_step(kb, carry):
            ks = [k_ref[rows(kb), hh * LANES:(hh + 1) * LANES] for hh in range(n_heads)]
            vs = []
            for pp in range(n_pairs):
                vb = v_ref[rows(kb), pp * LANES:(pp + 1) * LANES]
                zero = jnp.zeros_like(vb)
                vs += [jnp.where(lane < B_VDIM, vb, zero), jnp.where(lane < B_VDIM, zero, vb)]

            def make_step(masked):
                def step(qb, acc):
                    qs = [q_ref[rows(qb), hh * LANES:(hh + 1) * LANES] for hh in range(n_heads)]
                    dob = [do_ref[rows(qb), pp * LANES:(pp + 1) * LANES] for pp in range(n_pairs)]
                    ss = [_dot_nt(ks[hh], qs[hh]) for hh in range(n_heads)]
                    dps = [_dot_nt(vs[hh], dob[hh // 2]) for hh in range(n_heads)]
                    pbs, dss = [], []
                    for hh in range(n_heads):
                        stat = (hh // 2, slice(hh % 2, hh % 2 + 1), rows(qb))
                        s = ss[hh] * B_C2
                        if masked:
                            s = jnp.where(_key_le_query(kb, qb, tk, tq), s, NEG)
                        p = jnp.exp2(s - lse_ref[stat])
                        dss.append((p * (dps[hh] - dsum_scr[stat]) * B_SCALE).astype(BF16))
                        pbs.append(p.astype(BF16))
                    for hh in range(n_heads):
                        dqt_scr[hh, :, rows(qb)] += _dot_tn(ks[hh], dss[hh])
                    return tuple([acc[hh] + _dot(dss[hh], qs[hh]) for hh in range(n_heads)]
                                 + [acc[n_heads + hh] + _dot(pbs[hh], dob[hh // 2]) for hh in range(n_heads)])
                return step

            acc = make_step(True)(kb, (jnp.zeros((tk, LANES), F32),) * (2 * n_heads))
            acc = lax.fori_loop(kb + 1, nq, make_step(False), acc)
            for hh in range(n_heads):
                dk_ref[rows(kb), hh * LANES:(hh + 1) * LANES] = acc[hh].astype(BF16)
            for pp in range(n_pairs):
                dv_pair = jnp.where(lane < B_VDIM, acc[n_heads + 2 * pp], acc[n_heads + 2 * pp + 1])
                dv_ref[rows(kb), pp * LANES:(pp + 1) * LANES] = dv_pair.astype(BF16)
            return carry

        lax.fori_loop(0, nq, kv_step, 0)

        def dq_step(qb, carry):
            c, sp, sm = c_ref[rows(qb), :], sp_ref[rows(qb), :], sm_ref[rows(qb), :]
            for hh in range(n_heads):
                dq_ref[rows(qb), hh * LANES:(hh + 1) * LANES] = _rope_t(dqt_scr[hh, :, rows(qb)].T, c, sp, sm).astype(BF16)
            return carry

        lax.fori_loop(0, nq, dq_step, 0)

    pair_full = pl.BlockSpec((None, t, n_heads * LANES), lambda b, j: (b, 0, j))
    one_full = pl.BlockSpec((None, t, n_pairs * LANES), lambda b, j: (b, 0, j))
    row_full = pl.BlockSpec((None, n_pairs, 8, t), lambda b, j: (b, j, 0, 0))
    tab_full = pl.BlockSpec((None, t, LANES), lambda b, j: (b, 0, 0))
    return pl.pallas_call(
        body, name="attn_b_bwd", grid=(bsz, B_HEADS // n_heads),
        in_specs=[pair_full, pair_full, one_full, one_full, one_full, row_full, tab_full, tab_full, tab_full],
        out_specs=[pair_full, pair_full, one_full],
        out_shape=[jax.ShapeDtypeStruct((bsz, t, B_KPAD), BF16), jax.ShapeDtypeStruct((bsz, t, B_KPAD), BF16),
                   jax.ShapeDtypeStruct((bsz, t, B_HEADS * B_VDIM), BF16)],
        scratch_shapes=[pltpu.VMEM((n_heads, LANES, t), F32), pltpu.VMEM((n_pairs, 8, t), F32)],
        compiler_params=_cparams(("parallel", "parallel")))(q, kpad, v, do, o, lse2, *tabs)


ANY = pl.BlockSpec(memory_space=pl.ANY)


def _all_gather(shard, name):
    def body(x_ref, out_ref, send_sems, recv_sems, local_sem):
        x, y, c = lax.axis_index("x"), lax.axis_index("y"), lax.axis_index("c")
        me, sibling = (x, y, c), (x, y, 1 - c)
        chips = [(1 - x, y), (x, 1 - y), (1 - x, 1 - y)]

        def rows(px, py, pc):
            return out_ref.at[4 * px + 2 * py + pc]

        def copy(k, block, to, src=None):
            return pltpu.make_async_remote_copy(
                src_ref=rows(*block) if src is None else src, dst_ref=rows(*block),
                send_sem=send_sems.at[k], recv_sem=recv_sems.at[k], device_id=to, device_id_type=MESH)

        mine = pltpu.make_async_copy(x_ref, rows(*me), local_sem)
        mine.start()
        first = [copy(0, me, sibling, src=x_ref)]
        first += [copy(1 + j, me, (*chip, c), src=x_ref) for j, chip in enumerate(chips)]
        for cp in first:
            cp.start()
        passed = [copy(4 + j, (*chip, c), sibling) for j, chip in enumerate(chips)]
        for j, chip in enumerate(chips):
            copy(1 + j, (*chip, c), me).wait_recv()
            passed[j].start()
        copy(0, sibling, me).wait_recv()
        for j, chip in enumerate(chips):
            copy(4 + j, (*chip, 1 - c), me).wait_recv()
        for cp in first + passed:
            cp.wait_send()
        mine.wait()

    return pl.pallas_call(
        body, name=name, in_specs=[ANY], out_specs=ANY,
        out_shape=jax.ShapeDtypeStruct((N_DEV,) + shard.shape, shard.dtype),
        scratch_shapes=[pltpu.SemaphoreType.DMA((7,)), pltpu.SemaphoreType.DMA((7,)), pltpu.SemaphoreType.DMA])(shard)


def _all_gather_weights(flat, wide, name):
    ns = wide.shape[1]

    def body(f_ref, w_ref, fo_ref, wo_ref, send_sems, recv_sems, local_sems):
        x, y, c = lax.axis_index("x"), lax.axis_index("y"), lax.axis_index("c")
        me, sibling = (x, y, c), (x, y, 1 - c)
        chips = [(1 - x, y), (x, 1 - y), (1 - x, 1 - y)]

        def place(a, px, py, pc):
            idx = 4 * px + 2 * py + pc
            if a == 0:
                return fo_ref.at[idx]
            return wo_ref.at[:, pl.ds(pl.multiple_of(idx * ns, LANES), ns)]

        def copy(a, k, block, to, src=None):
            return pltpu.make_async_remote_copy(
                src_ref=place(a, *block) if src is None else src, dst_ref=place(a, *block),
                send_sem=send_sems.at[a, k], recv_sem=recv_sems.at[a, k], device_id=to, device_id_type=MESH)

        relay_in = (c * x + (1 - c) * (1 - x), c * (1 - y) + (1 - c) * y)
        relay_out = (c * (1 - x) + (1 - c) * x, c * y + (1 - c) * (1 - y))
        own = (f_ref, w_ref)
        mine = [pltpu.make_async_copy(own[a], place(a, *me), local_sems.at[a]) for a in range(2)]
        first = []
        for a in range(2):
            mine[a].start()
            first.append(copy(a, 0, me, sibling, src=own[a]))
            first += [copy(a, 1 + j, me, (*chip, c), src=own[a]) for j, chip in enumerate(chips[:2])]
        for cp in first:
            cp.start()
        relay = [copy(a, 3, (*relay_in, c), (*relay_out, c)) for a in range(2)]
        passed = [[copy(a, 4 + j, (*chip, c), sibling) for j, chip in enumerate(chips)] for a in range(2)]
        for a in range(2):
            for j, chip in enumerate(chips[:2]):
                copy(a, 1 + j, (*chip, c), me).wait_recv()
                passed[a][j].start()
            relay[a].start()
        for a in range(2):
            copy(a, 3, (*chips[2], c), me).wait_recv()
            passed[a][2].start()
        for a in range(2):
            copy(a, 0, sibling, me).wait_recv()
            for j, chip in enumerate(chips):
                copy(a, 4 + j, (*chip, 1 - c), me).wait_recv()
        for cp in first + relay + passed[0] + passed[1]:
            cp.wait_send()
        for cp in mine:
            cp.wait()

    return pl.pallas_call(
        body, name=name, in_specs=[ANY, ANY], out_specs=[ANY, ANY],
        out_shape=[jax.ShapeDtypeStruct((N_DEV,) + flat.shape, flat.dtype),
                   jax.ShapeDtypeStruct((wide.shape[0], N_DEV * ns), wide.dtype)],
        scratch_shapes=[pltpu.SemaphoreType.DMA((2, 7)), pltpu.SemaphoreType.DMA((2, 7)),
                        pltpu.SemaphoreType.DMA((2,))])(flat, wide)


N_CHIPS = 4


def _exchange_sibling(blocks, name):
    def body(g_ref, got_ref, send_sems, recv_sems):
        x, y, c = lax.axis_index("x"), lax.axis_index("y"), lax.axis_index("c")
        sends = [pltpu.make_async_remote_copy(
            src_ref=g_ref.at[q, 1 - c], dst_ref=got_ref.at[q], send_sem=send_sems.at[q],
            recv_sem=recv_sems.at[q], device_id=(x, y, 1 - c), device_id_type=MESH) for q in range(N_CHIPS)]
        for cp in sends:
            cp.start()
        for cp in sends:
            cp.wait_recv()
        for cp in sends:
            cp.wait_send()

    return pl.pallas_call(
        body, name=name, in_specs=[ANY], out_specs=ANY,
        out_shape=jax.ShapeDtypeStruct((N_CHIPS,) + blocks.shape[2:], blocks.dtype),
        scratch_shapes=[pltpu.SemaphoreType.DMA((N_CHIPS,)), pltpu.SemaphoreType.DMA((N_CHIPS,))])(blocks)


def _exchange_chips(parts, name):
    def body(p_ref, out_ref, send_sems, recv_sems, local_sem):
        x, y, c = lax.axis_index("x"), lax.axis_index("y"), lax.axis_index("c")
        me = 2 * x + y

        def peer(k):
            return (1 - x if k & 2 else x, 1 - y if k & 1 else y)

        def copy(k):
            px, py = peer(k)
            return pltpu.make_async_remote_copy(
                src_ref=p_ref.at[2 * px + py], dst_ref=out_ref.at[me], send_sem=send_sems.at[k - 1],
                recv_sem=recv_sems.at[k - 1], device_id=(px, py, c), device_id_type=MESH)

        def arrival(k):
            px, py = peer(k)
            slot = out_ref.at[2 * px + py]
            return pltpu.make_async_remote_copy(
                src_ref=slot, dst_ref=slot, send_sem=send_sems.at[k - 1], recv_sem=recv_sems.at[k - 1],
                device_id=(px, py, c), device_id_type=MESH)

        mine = pltpu.make_async_copy(p_ref.at[me], out_ref.at[me], local_sem)
        mine.start()
        sends = [copy(k) for k in range(1, N_CHIPS)]
        for cp in sends:
            cp.start()
        for k in range(1, N_CHIPS):
            arrival(k).wait_recv()
        for cp in sends:
            cp.wait_send()
        mine.wait()

    return pl.pallas_call(
        body, name=name, in_specs=[ANY], out_specs=ANY,
        out_shape=jax.ShapeDtypeStruct(parts.shape, parts.dtype),
        scratch_shapes=[pltpu.SemaphoreType.DMA((N_CHIPS - 1,)), pltpu.SemaphoreType.DMA((N_CHIPS - 1,)),
                        pltpu.SemaphoreType.DMA])(parts)


def _add_pairs(blocks, got, core, *, tr, name):
    q, r, c = got.shape

    def body(core_ref, a_ref, b_ref, o_ref):
        o_ref[...] = (a_ref[...].astype(F32) + b_ref[...].astype(F32)).astype(o_ref.dtype)

    spec = pl.BlockSpec((q, tr, c), lambda i, core_ref: (0, i, 0))
    mine = pl.BlockSpec((q, None, tr, c), lambda i, core_ref: (0, core_ref[0], i, 0))
    return pl.pallas_call(
        body, name=name,
        grid_spec=pltpu.PrefetchScalarGridSpec(num_scalar_prefetch=1, grid=(r // tr,), in_specs=[mine, spec],
                                               out_specs=spec),
        out_shape=jax.ShapeDtypeStruct(got.shape, BF16), compiler_params=_cparams(("parallel",)))(
            core, blocks, got)


def _sum_slots(slots, *, tr, name):
    n_slots, r, c = slots.shape

    def body(s_ref, o_ref):
        acc = s_ref[0].astype(F32)
        for s in range(1, n_slots):
            acc = acc + s_ref[s].astype(F32)
        o_ref[...] = acc

    return pl.pallas_call(
        body, name=name, grid=(r // tr,),
        in_specs=[pl.BlockSpec((n_slots, tr, c), lambda i: (0, i, 0))],
        out_specs=pl.BlockSpec((tr, c), lambda i: (i, 0)),
        out_shape=jax.ShapeDtypeStruct((r, c), F32),
        compiler_params=_cparams(("parallel",)))(slots)


def _adamw_math(w_t, g_t, m_t, v_t):
    m_n = ADAM_B1 * m_t + (1.0 - ADAM_B1) * g_t
    v_n = ADAM_B2 * v_t + (1.0 - ADAM_B2) * (g_t * g_t)
    m_hat = m_n / (1.0 - ADAM_B1 ** ADAM_STEP)
    v_hat = v_n / (1.0 - ADAM_B2 ** ADAM_STEP)
    delta = -ADAM_LR * (m_hat / (jnp.sqrt(v_hat) + ADAM_EPS) + ADAM_WD * w_t)
    return delta, m_n, v_n


def _adamw(w, g, m, v, name):
    shape = w.shape
    cols = shape[-1]
    args = [a.reshape(-1, cols) for a in (w, g, m, v)]
    rows = args[0].shape[0]
    tm = 256 if rows % 256 == 0 else rows
    delta, m_n, v_n = _rowwise(_adamw_math, args, [], [(cols, F32)] * 3, tm=tm, name=name)
    return delta.reshape(shape), m_n.reshape(shape), v_n.reshape(shape)


def _adamw_small(ws, gs, ms, vs, name):
    k = len(ws)
    args = [a.reshape(-1, a.shape[-1]) for group in (ws, gs, ms, vs) for a in group]

    def body(*refs):
        ins, outs = refs[:4 * k], refs[4 * k:]
        for i in range(k):
            res = _adamw_math(*[ins[j * k + i][...] for j in range(4)])
            for j in range(3):
                outs[j * k + i][...] = res[j]

    res = pl.pallas_call(
        body, name=name, out_shape=[jax.ShapeDtypeStruct(a.shape, F32) for a in args[:k]] * 3,
        compiler_params=pltpu.CompilerParams(vmem_limit_bytes=VMEM_LIMIT))(*args)
    return [[res[j * k + i].reshape(ws[i].shape) for i in range(k)] for j in range(3)]


def _local_step(x, positions, target, w):
    bsz, t, _ = x.shape
    n = bsz * t
    tm = 256
    mm = 512
    x2 = x.reshape(n, D_MODEL)
    tgt2 = target.reshape(n, D_MODEL)
    pos = positions.reshape(n)
    tb2 = _rope_tables(pos, B_ROPE_THETA, B_ROPE, B_NOPE, "rope_tables_b")
    tabs_b = [a.reshape(bsz, t, LANES) for a in tb2]

    w_a_in = w["a_w_in"]
    w_a_out = w["a_w_out"]
    w_down = w["kv_w_down"]
    w_down_p = jnp.zeros((D_MODEL, 3 * LANES), BF16).at[:, :B_KV_LORA].set(w_down[:, :B_KV_LORA])
    w_down_p = w_down_p.at[:, B_KV_LORA + B_NOPE:B_KV_LORA + B_QK_DIM].set(w_down[:, B_KV_LORA:])
    wu = w["kv_w_up"].reshape(B_KV_LORA, B_HEADS, B_NOPE + B_VDIM)
    w_upk = jnp.pad(wu[:, :, :B_NOPE], ((0, 0), (0, 0), (0, LANES - B_NOPE))).reshape(B_KV_LORA, B_KPAD)
    w_upv = wu[:, :, B_NOPE:].reshape(B_KV_LORA, B_HEADS * B_VDIM)
    w_b_in = w["b_w_in"]
    w_q_p = jnp.pad(w["b_w_q_up"].reshape(B_Q_LORA, B_HEADS, B_QK_DIM),
                    ((0, 0), (0, 0), (0, LANES - B_QK_DIM))).reshape(B_Q_LORA, B_KPAD)
    w_b_out = w["b_w_out"]

    def tab_extras(tabs2, rows):
        return [(a, (rows, LANES), lambda j, i, kk: (i, 0)) for a in tabs2]

    (hn_a,) = _rowwise(lambda xt, g: (_rms(xt, g),), [x2], [w["a_pre_norm"]], [(D_MODEL, BF16)],
                       tm=tm, name="a_pre_norm")

    def rope_epilogue(acc, o_ref, rows, c_ref, sp_ref, sm_ref):
        c, sp, sm = c_ref[rows, :], sp_ref[rows, :], sm_ref[rows, :]
        for h in range(acc.shape[1] // LANES):
            hs = slice(h * LANES, (h + 1) * LANES)
            o_ref[rows, hs] = _rope(acc[:, hs], c, sp, sm).astype(BF16)

    def to_group(a, d):
        if d == 1:
            return a
        return a.reshape(bsz, t // d, d, a.shape[-1]).transpose(0, 2, 1, 3).reshape(n, a.shape[-1])

    def from_group(a, d):
        if d == 1:
            return a
        return a.reshape(bsz, d, t // d, a.shape[-1]).transpose(0, 2, 1, 3).reshape(n, a.shape[-1])

    def rows_to_cols(r, d):
        return r.reshape(bsz, d, A_HEADS, t // d).transpose(0, 3, 1, 2).reshape(n, A_HEADS)

    def cols_to_rows(cc, d):
        return cc.reshape(bsz, t // d, d, A_HEADS).transpose(0, 2, 3, 1).reshape(bsz * d, A_HEADS, t // d)

    z_a = _matmul(hn_a, w_a_in, b_cols=(A_QKV, A_WIDTH), out_dtype=F32, tm=mm, tn=A_WIDTH, name="a_gate")
    hn_g, tabs_g, qk_g, v_g, o_g, lse_g = [], [], [], [], [], []
    for g, d in enumerate(A_DILATIONS):
        hn_g.append(to_group(hn_a, d))
        tabs_g.append(_rope_tables(to_group(pos.reshape(n, 1), d).reshape(n), A_ROPE_THETA, A_ROT_DIM, 0,
                                   f"rope_tables_a_g{g}"))
        col0 = g * 3 * A_WIDTH
        qk = _matmul(hn_g[g], w_a_in, b_cols=(col0, 2 * A_WIDTH), out_dtype=BF16, tm=mm, tn=A_WIDTH,
                     name=f"a_qk_g{g}", epilogue=rope_epilogue, epilogue_rows=EPILOGUE_ROWS,
                     extras=tab_extras(tabs_g[g], mm))
        v = _matmul(hn_g[g], w_a_in, b_cols=(col0 + 2 * A_WIDTH, A_WIDTH), out_dtype=BF16, tm=mm, tn=A_WIDTH,
                    name=f"a_v_g{g}")
        qk_g.append(qk.reshape(bsz * d, t // d, 2 * A_WIDTH))
        v_g.append(v.reshape(bsz * d, t // d, A_WIDTH))
        o, lse = _attn_a_fwd(qk_g[g], v_g[g], f"attn_a_fwd_g{g}")
        o_g.append(from_group(o.reshape(n, A_WIDTH), d))
        lse_g.append(rows_to_cols(lse, d))

    def merge_fn(o0, o1, o2, l0, l1, l2, z):
        lmax = jnp.maximum(jnp.maximum(l0, l1), l2)
        e0, e1, e2 = jnp.exp2(l0 - lmax), jnp.exp2(l1 - lmax), jnp.exp2(l2 - lmax)
        den = e0 + e1 + e2
        w0, w1, w2 = e0 / den, e1 / den, e2 / den
        parts = []
        for h in range(A_HEADS):
            hs = slice(h * A_HEAD_DIM, (h + 1) * A_HEAD_DIM)
            parts.append(w0[:, h:h + 1] * o0[:, hs] + w1[:, h:h + 1] * o1[:, hs] + w2[:, h:h + 1] * o2[:, hs])
        o = jnp.concatenate(parts, axis=1)
        return o * _silu(z), o, lmax + jnp.log2(den)

    y_a, o_a2, lse_a = _rowwise(merge_fn, [*o_g, *lse_g, z_a], [],
                                [(A_WIDTH, BF16), (A_WIDTH, F32), (A_HEADS, F32)], tm=tm, name="a_merge_gate")
    out_a = _matmul(y_a, w_a_out, out_dtype=F32, tm=mm, tn=D_MODEL, name="a_out")

    def mid_fn(xt, out, g_post, g_kv, g_b):
        h1 = xt + _rms(out, g_post)
        return h1, _rms(h1, g_kv), _rms(h1, g_b)

    h1, hn_kv, hn_b = _rowwise(mid_fn, [x2, out_a], [w["a_post_norm"], w["kv_norm"], w["b_pre_norm"]],
                               [(D_MODEL, F32), (D_MODEL, BF16), (D_MODEL, BF16)], tm=tm, name="a_post_norm")

    ckr = _matmul(hn_kv, w_down_p, out_dtype=F32, tm=mm, tn=3 * LANES, name="kv_down")

    def latent_fn(ck, c, sp, sm, g):
        return _rms(ck[:, :B_KV_LORA], g), _rope(ck[:, B_KV_LORA:], c, sp, sm)

    c_kv, k_rope = _rowwise(latent_fn, [ckr, *tb2], [w["kv_latent_norm"]], [(B_KV_LORA, BF16), (LANES, F32)],
                            tm=tm, name="kv_latent_norm")

    def kpad_epilogue(acc, o_ref, rows, kr_ref):
        kr = kr_ref[rows, :]
        for h in range(acc.shape[1] // LANES):
            hs = slice(h * LANES, (h + 1) * LANES)
            o_ref[rows, hs] = (acc[:, hs] + kr).astype(BF16)

    kpad = _matmul(c_kv, w_upk, out_dtype=BF16, tm=mm, tn=1024, name="kv_up_k", epilogue=kpad_epilogue,
                   epilogue_rows=EPILOGUE_ROWS,
                   extras=[(k_rope, (mm, LANES), lambda j, i, kk: (i, 0))])
    v_b = _matmul(c_kv, w_upv, out_dtype=BF16, tm=mm, tn=1024, name="kv_up_v")

    proj_b = _matmul(hn_b, w_b_in, out_dtype=F32, tm=mm, tn=w_b_in.shape[1], name="b_in")
    (c_q,) = _rowwise(lambda p, g: (_rms(p[:, :B_Q_LORA], g),), [proj_b], [w["b_q_norm"]], [(B_Q_LORA, BF16)],
                      tm=tm, name="b_q_norm")

    q_b = _matmul(c_q, w_q_p, out_dtype=BF16, tm=mm, tn=1024, name="b_q_up", epilogue=rope_epilogue,
                  epilogue_rows=EPILOGUE_ROWS,
                  extras=tab_extras(tb2, mm))
    q_b3, kpad3, v_b3 = q_b.reshape(bsz, t, B_KPAD), kpad.reshape(bsz, t, B_KPAD), v_b.reshape(bsz, t, -1)
    w_vt = jnp.pad(wu[:, :, B_NOPE:], ((0, 0), (0, 0), (0, LANES - B_VDIM))).reshape(B_KV_LORA, B_KPAD)
    vt1 = _value_heads_t(c_kv, w_vt, bsz, t)
    o_b, lse_b = _attn_b_fwd(q_b3, kpad3, vt1)
    o_b2 = o_b.reshape(n, -1)
    (y_b,) = _rowwise(lambda o, p: (o * _silu(p[:, B_Q_LORA:]),), [o_b2, proj_b], [], [(D_MODEL, BF16)],
                      tm=tm, name="b_gate_mul")
    out_b = _matmul(y_b, w_b_out, out_dtype=F32, tm=mm, tn=D_MODEL, name="b_out")

    def head_fn(h1t, out, tgt, g):
        e = h1t + _rms(out, g) - tgt
        loss_row = 0.5 * jnp.mean(e * e, axis=-1, keepdims=True)
        dh2 = e * (1.0 / D_MODEL)
        d_out, dg = _rms_bwd(out, g, dh2)
        return dh2, d_out, dg, jnp.broadcast_to(loss_row * (1.0 / LANES), (loss_row.shape[0], LANES))

    dh2, d_out_b, dg_b_post, loss_acc = _rowwise(
        head_fn, [h1, out_b, tgt2], [w["b_post_norm"]], [(D_MODEL, F32), (D_MODEL, BF16)], [D_MODEL, LANES],
        tm=tm, name="loss_head")

    dy_b = _matmul(d_out_b, w_b_out, tb=True, out_dtype=BF16, tm=mm, tn=D_MODEL, name="b_out_dx")
    gw_b_out = _matmul(y_b, d_out_b, ta=True, out_dtype=BF16, tm=mm, tn=D_MODEL, tk=2048, name="b_out_dw")

    def gate_b_bwd(dy, o, p):
        z = p[:, B_Q_LORA:]
        return dy * _silu(z), dy * o * _silu_grad(z)

    do_b, dz_b = _rowwise(gate_b_bwd, [dy_b, o_b2, proj_b], [], [(D_MODEL, BF16), (D_MODEL, F32)],
                          tm=tm, name="b_gate_bwd")
    do_b3 = do_b.reshape(bsz, t, -1)
    dq_b, dk_b, dv_b = _attn_b_bwd(q_b3, kpad3, v_b3, do_b3, o_b, lse_b, tabs_b)
    dq_b2, dk_b2, dv_b2 = dq_b.reshape(n, B_KPAD), dk_b.reshape(n, B_KPAD), dv_b.reshape(n, -1)

    dc_kv = _matmul(dk_b2, w_upk, tb=True, out_dtype=F32, tm=mm, tn=B_KV_LORA, name="kv_up_k_dx")
    dc_kv = _matmul(dv_b2, w_upv, tb=True, out_dtype=BF16, tm=mm, tn=B_KV_LORA, name="kv_up_v_dx",
                    epilogue=_add_epilogue, extras=[(dc_kv, (mm, B_KV_LORA), lambda j, i, kk: (i, j))])
    gw_upk = _matmul(c_kv, dk_b2, ta=True, out_dtype=BF16, tm=B_KV_LORA, tn=1024, tk=2048, name="kv_up_k_dw")
    gw_upv = _matmul(c_kv, dv_b2, ta=True, out_dtype=BF16, tm=B_KV_LORA, tn=1024, tk=2048, name="kv_up_v_dw")

    def latent_bwd(ck, dck, dk, c, sp, sm, g):
        d1, dg = _rms_bwd(ck[:, :B_KV_LORA], g, dck)
        ksum = dk[:, :LANES].astype(F32)
        for h in range(1, B_HEADS):
            ksum = ksum + dk[:, h * LANES:(h + 1) * LANES].astype(F32)
        lane = lax.broadcasted_iota(jnp.int32, ksum.shape, 1)
        ksum = jnp.where((lane >= B_NOPE) & (lane < B_QK_DIM), ksum, 0.0)
        return jnp.concatenate([d1, _rope_t(ksum, c, sp, sm)], axis=1), dg

    dckr, dg_latent = _rowwise(latent_bwd, [ckr, dc_kv, dk_b2, *tb2], [w["kv_latent_norm"]],
                               [(3 * LANES, BF16)], [B_KV_LORA], tm=tm, name="kv_latent_bwd")
    dhn_kv = _matmul(dckr, w_down_p, tb=True, out_dtype=BF16, tm=mm, tn=D_MODEL, name="kv_down_dx")
    gw_down_p = _matmul(hn_kv, dckr, ta=True, out_dtype=BF16, tm=mm, tn=3 * LANES, tk=2048, name="kv_down_dw")

    dc_q = _matmul(dq_b2, w_q_p, tb=True, out_dtype=BF16, tm=mm, tn=B_Q_LORA, name="b_q_up_dx")
    gw_q_p = _matmul(c_q, dq_b2, ta=True, out_dtype=BF16, tm=B_Q_LORA, tn=1024, tk=2048, name="b_q_up_dw")

    def q_norm_bwd(p, dcq, dz, g):
        d1, dg = _rms_bwd(p[:, :B_Q_LORA], g, dcq)
        return jnp.concatenate([d1, dz], axis=1), dg

    dproj_b, dg_q_norm = _rowwise(q_norm_bwd, [proj_b, dc_q, dz_b], [w["b_q_norm"]],
                                  [(w_b_in.shape[1], BF16)], [B_Q_LORA], tm=tm, name="b_q_norm_bwd")
    dhn_b = _matmul(dproj_b, w_b_in, tb=True, out_dtype=BF16, tm=mm, tn=D_MODEL, name="b_in_dx")
    gw_b_in = _matmul(hn_b, dproj_b, ta=True, out_dtype=BF16, tm=mm, tn=w_b_in.shape[1], tk=2048, name="b_in_dw")

    def mid_bwd(h1t, dh2t, dkv, db, g_kv, g_b, g_post, out):
        rstd = lax.rsqrt(jnp.mean(h1t * h1t, axis=-1, keepdims=True) + NORM_EPS)
        xhat = h1t * rstd
        dxhat = dkv * g_kv + db * g_b
        dh1 = dh2t + rstd * (dxhat - xhat * jnp.mean(dxhat * xhat, axis=-1, keepdims=True))
        d_out, rp = _rms_bwd(out, g_post, dh1)
        return dh1, d_out, dkv * xhat, db * xhat, rp

    def mid_bwd_fn(h1t, dh2t, dkv, db, out, g_kv, g_b, g_post):
        return mid_bwd(h1t, dh2t, dkv, db, g_kv, g_b, g_post, out)

    dh1, d_out_a, dg_kv, dg_b_pre, dg_a_post = _rowwise(
        mid_bwd_fn, [h1, dh2, dhn_kv, dhn_b, out_a], [w["kv_norm"], w["b_pre_norm"], w["a_post_norm"]],
        [(D_MODEL, F32), (D_MODEL, BF16)], [D_MODEL] * 3, tm=tm, name="mid_bwd")

    dy_a = _matmul(d_out_a, w_a_out, tb=True, out_dtype=BF16, tm=mm, tn=A_WIDTH, name="a_out_dx")
    gw_a_out = _matmul(y_a, d_out_a, ta=True, out_dtype=BF16, tm=mm, tn=D_MODEL, tk=2048, name="a_out_dw")

    def gate_a_bwd(dy, o, z):
        do = dy * _silu(z)
        prod = do * o
        lane = lax.broadcasted_iota(jnp.int32, (prod.shape[0], A_HEADS), 1)
        dsum = jnp.zeros((prod.shape[0], A_HEADS), F32)
        for h in range(A_HEADS):
            col = jnp.sum(prod[:, h * A_HEAD_DIM:(h + 1) * A_HEAD_DIM], axis=1, keepdims=True)
            dsum = jnp.where(lane == h, col, dsum)
        return do, dy * o * _silu_grad(z), dsum

    do_a, dz_a, dsum_a = _rowwise(gate_a_bwd, [dy_a, o_a2, z_a], [],
                                  [(A_WIDTH, BF16), (A_WIDTH, BF16), (A_HEADS, F32)], tm=tm, name="a_gate_bwd")
    dhn_a = _matmul(dz_a, w_a_in, b_cols=(A_QKV, A_WIDTH), tb=True, out_dtype=F32, tm=mm, tn=D_MODEL,
                    name="a_gate_dx")
    gw_a_in, dhn_groups = None, []
    for g, d in enumerate(A_DILATIONS):
        s_n, ln = bsz * d, t // d
        dqkv = _attn_a_bwd(qk_g[g], v_g[g], to_group(do_a, d).reshape(s_n, ln, A_WIDTH), cols_to_rows(lse_a, d),
                           cols_to_rows(dsum_a, d), [a.reshape(s_n, ln, LANES) for a in tabs_g[g]],
                           f"attn_a_bwd_g{g}").reshape(n, 3 * A_WIDTH)
        w_cols = (g * 3 * A_WIDTH, 3 * A_WIDTH)
        if d == 1:
            dhn_a = _matmul(dqkv, w_a_in, b_cols=w_cols, tb=True, out_dtype=F32, tm=mm, tn=D_MODEL,
                            name=f"a_qkv_dx_g{g}", epilogue=_add_epilogue,
                            extras=[(dhn_a, (mm, D_MODEL), lambda j, i, kk: (i, j))])
        else:
            dhn_groups.append(from_group(_matmul(dqkv, w_a_in, b_cols=w_cols, tb=True, out_dtype=BF16, tm=mm,
                                                 tn=D_MODEL, name=f"a_qkv_dx_g{g}"), d))
        gw_a_in = _matmul(hn_g[g], dqkv, ta=True, out_dtype=BF16, tm=mm, tn=1024, tk=2048, name=f"a_qkv_dw_g{g}",
                          out_into=(gw_a_in, g * 3 * A_WIDTH, A_QKV + A_WIDTH))
    gw_a_in = _matmul(hn_a, dz_a, ta=True, out_dtype=BF16, tm=mm, tn=1024, tk=2048, name="a_gate_dw",
                      out_into=(gw_a_in, A_QKV, A_QKV + A_WIDTH))

    def first_bwd(xt, dhn, dhn_1, dhn_2, dh1t, g):
        dx, dg = _rms_bwd(xt, g, dhn + dhn_1 + dhn_2)
        return dh1t + dx, dg

    grad_x, dg_a_pre = _rowwise(first_bwd, [x2, dhn_a, *dhn_groups, dh1], [w["a_pre_norm"]], [(D_MODEL, F32)],
                                [D_MODEL], tm=tm, name="a_pre_norm_bwd")

    gw_down = jnp.concatenate([gw_down_p[:, :B_KV_LORA], gw_down_p[:, B_KV_LORA + B_NOPE:B_KV_LORA + B_QK_DIM]], axis=1)
    gw_up = jnp.concatenate([gw_upk.reshape(B_KV_LORA, B_HEADS, LANES)[:, :, :B_NOPE],
                             gw_upv.reshape(B_KV_LORA, B_HEADS, B_VDIM)], axis=2).reshape(B_KV_LORA, -1)
    gw_q_up = gw_q_p.reshape(B_Q_LORA, B_HEADS, LANES)[:, :, :B_QK_DIM].reshape(B_Q_LORA, -1)
    grads = {"a_w_in": gw_a_in, "a_w_out": gw_a_out, "kv_w_down": gw_down, "kv_w_up": gw_up,
             "b_w_in": gw_b_in, "b_w_q_up": gw_q_up, "b_w_out": gw_b_out}
    gains = {"a_pre_norm": dg_a_pre, "a_post_norm": dg_a_post, "kv_norm": dg_kv, "kv_latent_norm": dg_latent,
             "b_pre_norm": dg_b_pre, "b_q_norm": dg_q_norm, "b_post_norm": dg_b_post}
    gains = {k: jnp.sum(a, axis=0) for k, a in gains.items()}
    return jnp.sum(loss_acc), grad_x.reshape(bsz, t, D_MODEL), grads, gains


WEIGHT_ORDER = ("a_pre_norm", "a_w_in", "a_w_out", "a_post_norm", "kv_norm", "kv_w_down", "kv_latent_norm",
                "kv_w_up", "b_pre_norm", "b_w_in", "b_q_norm", "b_w_q_up", "b_w_out", "b_post_norm")
MATRICES = (("a_w_in", 1024, 10240, 1), ("a_w_out", 1024, 1024, 0), ("kv_w_down", 1024, 288, 0),
            ("kv_w_up", 256, 2048, 1), ("b_w_in", 1024, 1408, 1), ("b_w_q_up", 384, 1536, 1),
            ("b_w_out", 1024, 1024, 0))
SHARDED_GAINS = ("a_pre_norm", "a_post_norm")
GAIN_WIDTHS = (("a_pre_norm", 1024), ("a_post_norm", 1024), ("kv_norm", 1024), ("kv_latent_norm", 256),
               ("b_pre_norm", 1024), ("b_q_norm", 384), ("b_post_norm", 1024))
GAIN_ROWS = 48


def _shard_rows(rows, cols):
    return rows * cols // (N_DEV * LANES)


def _whole_from_blocks(blocks, rows, cols, axis):
    if axis == 1:
        return blocks.reshape(N_DEV, rows, cols // N_DEV).transpose(1, 0, 2).reshape(rows, cols)
    return blocks.reshape(rows, cols)


def _blocks_from_whole(whole, rows, cols, axis):
    if axis == 1:
        whole = whole.reshape(rows, N_DEV, cols // N_DEV).transpose(1, 0, 2)
    return whole.reshape(N_DEV, -1, LANES)


def kernel(x, positions, a_pre_norm, a_w_in, a_w_out, a_post_norm, kv_norm, kv_w_down, kv_latent_norm, kv_w_up, b_pre_norm, b_w_in, b_q_norm, b_w_q_up, b_w_out, b_post_norm, loss_target, m_a_pre_norm, m_a_w_in, m_a_w_out, m_a_post_norm, m_kv_norm, m_kv_w_down, m_kv_latent_norm, m_kv_w_up, m_b_pre_norm, m_b_w_in, m_b_q_norm, m_b_w_q_up, m_b_w_out, m_b_post_norm, v_a_pre_norm, v_a_w_in, v_a_w_out, v_a_post_norm, v_kv_norm, v_kv_w_down, v_kv_latent_norm, v_kv_w_up, v_b_pre_norm, v_b_w_in, v_b_q_norm, v_b_w_q_up, v_b_w_out, v_b_post_norm):
    weights = dict(a_pre_norm=a_pre_norm, a_w_in=a_w_in, a_w_out=a_w_out, a_post_norm=a_post_norm, kv_norm=kv_norm,
                   kv_w_down=kv_w_down, kv_latent_norm=kv_latent_norm, kv_w_up=kv_w_up, b_pre_norm=b_pre_norm,
                   b_w_in=b_w_in, b_q_norm=b_q_norm, b_w_q_up=b_w_q_up, b_w_out=b_w_out, b_post_norm=b_post_norm)
    m_in = dict(a_pre_norm=m_a_pre_norm, a_w_in=m_a_w_in, a_w_out=m_a_w_out, a_post_norm=m_a_post_norm,
                kv_norm=m_kv_norm, kv_w_down=m_kv_w_down, kv_latent_norm=m_kv_latent_norm, kv_w_up=m_kv_w_up,
                b_pre_norm=m_b_pre_norm, b_w_in=m_b_w_in, b_q_norm=m_b_q_norm, b_w_q_up=m_b_w_q_up,
                b_w_out=m_b_w_out, b_post_norm=m_b_post_norm)
    v_in = dict(a_pre_norm=v_a_pre_norm, a_w_in=v_a_w_in, a_w_out=v_a_w_out, a_post_norm=v_a_post_norm,
                kv_norm=v_kv_norm, kv_w_down=v_kv_w_down, kv_latent_norm=v_kv_latent_norm, kv_w_up=v_kv_w_up,
                b_pre_norm=v_b_pre_norm, b_w_in=v_b_w_in, b_q_norm=v_b_q_norm, b_w_q_up=v_b_w_q_up,
                b_w_out=v_b_w_out, b_post_norm=v_b_post_norm)
    me = 4 * lax.axis_index("x") + 2 * lax.axis_index("y") + lax.axis_index("c")

    wide = MATRICES[0][0]
    flat = jnp.concatenate([weights[name].astype(BF16).reshape(-1, LANES) for name, _, _, _ in MATRICES[1:]], axis=0)
    gathered, w_wide = _all_gather_weights(flat, weights[wide][0].astype(BF16), "gather_weights")
    whole = {wide: w_wide}
    off = 0
    for name, rows, cols, axis in MATRICES[1:]:
        nr = _shard_rows(rows, cols)
        whole[name] = _whole_from_blocks(gathered[:, off:off + nr], rows, cols, axis)
        off += nr
    gain_shard = jnp.concatenate([weights[name].reshape(1, LANES) for name in SHARDED_GAINS]
                                 + [jnp.zeros((8 - len(SHARDED_GAINS), LANES), F32)], axis=0)
    gain_blocks = _all_gather(gain_shard, "gather_gains")
    for i, name in enumerate(SHARDED_GAINS):
        whole[name] = gain_blocks[:, i, :].reshape(1, D_MODEL)
    for name in ("kv_norm", "kv_latent_norm", "b_pre_norm", "b_q_norm", "b_post_norm"):
        whole[name] = weights[name].reshape(1, -1)

    loss_part, grad_x, grads, gains = _local_step(x, positions, loss_target, whole)

    blocks = jnp.concatenate([_blocks_from_whole(grads[name], rows, cols, axis).astype(BF16)
                              for name, rows, cols, axis in MATRICES], axis=1)
    blocks = blocks.reshape(N_CHIPS, 2, blocks.shape[1], LANES)
    got = _exchange_sibling(blocks, "scatter_grads_core")
    core = lax.axis_index("c").astype(jnp.int32).reshape(1)
    landed = _exchange_chips(_add_pairs(blocks, got, core, tr=2512, name="add_core_grads"), "scatter_grads_chip")
    summed = _sum_slots(landed, tr=2512, name="sum_grads")
    grad_out = {}
    off = 0
    for name, rows, cols, axis in MATRICES:
        nr = _shard_rows(rows, cols)
        grad_out[name] = summed[off:off + nr].reshape(weights[name].shape)
        off += nr

    vec = jnp.concatenate([gains[name] for name, _ in GAIN_WIDTHS] + [jnp.full((LANES,), loss_part, F32)])
    vec = jnp.pad(vec, (0, GAIN_ROWS * LANES - vec.shape[0])).reshape(GAIN_ROWS, LANES)
    total = _sum_slots(_all_gather(vec, "gather_gain_grads"), tr=GAIN_ROWS, name="sum_gain_grads").reshape(-1)
    off = 0
    for name, width in GAIN_WIDTHS:
        g = total[off:off + width]
        if name in SHARDED_GAINS:
            g = lax.dynamic_slice(g, (me * LANES,), (LANES,))
        grad_out[name] = g.reshape(weights[name].shape)
        off += width
    loss = total[off]

    deltas, new_m, new_v = {}, {}, {}
    big = "a_w_in"
    deltas[big], new_m[big], new_v[big] = _adamw(weights[big], grad_out[big], m_in[big], v_in[big], "adamw_" + big)
    small = [name for name in WEIGHT_ORDER if name != big]
    res = _adamw_small(*[[d[name] for name in small] for d in (weights, grad_out, m_in, v_in)], "adamw_small")
    for out, vals in zip((deltas, new_m, new_v), res):
        out.update(zip(small, vals))
    return (loss, grad_x, *[grad_out[k] for k in WEIGHT_ORDER], *[deltas[k] for k in WEIGHT_ORDER],
            *[new_m[k] for k in WEIGHT_ORDER], *[new_v[k] for k in WEIGHT_ORDER])
```

```python
import jax
import jax.numpy as jnp
from jax import lax
from jax.experimental import pallas as pl
from jax.experimental.pallas import tpu as pltpu

F32 = jnp.float32
BF16 = jnp.bfloat16

N_DEV = 8
D_MODEL = 1024
NORM_EPS = 1e-6
A_GROUPS = 3
A_DILATIONS = (1, 4, 16)
A_HEADS = 8
A_HEAD_DIM = 128
A_WIDTH = 1024
A_ROT_DIM = 32
A_ROPE_THETA = 500000.0
A_QKV = A_GROUPS * 3 * A_WIDTH
B_HEADS = 16
B_NOPE = 64
B_ROPE = 32
B_QK_DIM = 96
B_VDIM = 64
B_Q_LORA = 384
B_KV_LORA = 256
B_ROPE_THETA = 10000.0
B_KPAD = B_HEADS * 128
ADAM_LR = 0.001
ADAM_B1 = 0.9
ADAM_B2 = 0.999
ADAM_EPS = 1e-08
ADAM_WD = 0.01
ADAM_STEP = 10

LANES = 128
BAND = 128
EPILOGUE_ROWS = 128
NEG = -1e30
VMEM_LIMIT = 56 * 1024 * 1024
MESH = pl.DeviceIdType.MESH


def _cparams(sem):
    return pltpu.CompilerParams(dimension_semantics=sem, vmem_limit_bytes=VMEM_LIMIT)


def _rowwise(fn, rows, bcast, outs, accs=(), *, tm, name):
    n = rows[0].shape[0]
    nr, nb, no = len(rows), len(bcast), len(outs)

    def body(*refs):
        res = fn(*[r[...] for r in refs[:nr + nb]])
        out_refs = refs[nr + nb:nr + nb + no]
        acc_refs = refs[nr + nb + no:]
        for r, v in zip(out_refs, res[:no]):
            r[...] = v.astype(r.dtype)
        if acc_refs:
            @pl.when(pl.program_id(0) == 0)
            def _():
                for r in acc_refs:
                    r[...] = jnp.zeros_like(r)
            for r, v in zip(acc_refs, res[no:]):
                r[...] += v.reshape(tm // 8, 8, v.shape[-1]).sum(axis=0)

    in_specs = [pl.BlockSpec((tm, a.shape[1]), lambda i: (i, 0)) for a in rows]
    in_specs += [pl.BlockSpec(a.shape, lambda i: (0, 0)) for a in bcast]
    out_specs = [pl.BlockSpec((tm, c), lambda i: (i, 0)) for c, _ in outs]
    out_specs += [pl.BlockSpec((8, c), lambda i: (0, 0)) for c in accs]
    out_shape = [jax.ShapeDtypeStruct((n, c), dt) for c, dt in outs]
    out_shape += [jax.ShapeDtypeStruct((8, c), F32) for c in accs]
    return pl.pallas_call(
        body, name=name, grid=(n // tm,), in_specs=in_specs, out_specs=out_specs, out_shape=out_shape,
        compiler_params=_cparams(("arbitrary",)))(*rows, *bcast)


def _matmul(a, b, *, out_dtype, tm, tn, tk=None, name, epilogue=None, extras=(), ta=False, tb=False,
            epilogue_rows=None, b_cols=None, out_into=None):
    epilogue_rows = epilogue_rows or tm
    k, m = a.shape[::-1] if not ta else a.shape
    col0, width = b_cols or (0, b.shape[1])
    n = b.shape[0] if tb else width
    tk = tk or k
    nk = k // tk
    b_off = col0 // (tk if tb else tn)
    assert col0 % (tk if tb else tn) == 0 and (k == width if tb else True)
    ne = len(extras)
    dot = _dot_tn if ta else (_dot_nt if tb else _dot)
    assert epilogue is None or (nk == 1 and not ta)

    def body(*refs):
        a_ref, b_ref = refs[:2]
        ex = refs[2:2 + ne]
        o_ref = refs[2 + ne + (1 if out_into is not None and out_into[0] is not None else 0)]
        if epilogue is not None:
            b_tile = b_ref[...].astype(BF16)
            for r0 in range(0, tm, epilogue_rows):
                rows = slice(r0, r0 + epilogue_rows)
                epilogue(dot(a_ref[rows, :].astype(BF16), b_tile), o_ref, rows, *ex)
            return
        part = dot(a_ref[...].astype(BF16), b_ref[...].astype(BF16))
        if nk == 1:
            o_ref[...] = part.astype(o_ref.dtype)
        else:
            acc_ref = refs[-1]
            kk = pl.program_id(2)

            @pl.when(kk == 0)
            def _():
                acc_ref[...] = part

            @pl.when(kk > 0)
            def _():
                acc_ref[...] += part

            @pl.when(kk == nk - 1)
            def _():
                o_ref[...] = acc_ref[...].astype(o_ref.dtype)

    a_spec = pl.BlockSpec((tk, tm), lambda j, i, kk: (kk, i)) if ta else pl.BlockSpec((tm, tk), lambda j, i, kk: (i, kk))
    b_spec = (pl.BlockSpec((tn, tk), lambda j, i, kk: (j, kk + b_off)) if tb
              else pl.BlockSpec((tk, tn), lambda j, i, kk: (kk, j + b_off)))
    in_specs = [a_spec, b_spec] + [pl.BlockSpec(bs, im) for _, bs, im in extras]
    operands = [a, b] + [e[0] for e in extras]
    aliases = {}
    if out_into is not None:
        prev, out0, n_total = out_into
        o_off = out0 // tn
        assert out0 % tn == 0
        if prev is not None:
            in_specs.append(ANY)
            operands.append(prev)
            aliases = {len(operands) - 1: 0}
    else:
        o_off, n_total = 0, n
    return pl.pallas_call(
        body, name=name, grid=(n // tn, m // tm, nk), in_specs=in_specs,
        out_specs=pl.BlockSpec((tm, tn), lambda j, i, kk: (i, j + o_off)),
        out_shape=jax.ShapeDtypeStruct((m, n_total), out_dtype), input_output_aliases=aliases,
        scratch_shapes=[pltpu.VMEM((tm, tn), F32)] if nk > 1 else [],
        compiler_params=_cparams(("parallel", "parallel", "arbitrary")))(*operands)


def _add_epilogue(acc, o_ref, rows, prev_ref):
    o_ref[rows, :] = (acc + prev_ref[rows, :]).astype(o_ref.dtype)


def _rope(x, c, sp, sm):
    return x * c + pltpu.roll(x, 16, 1) * sp + pltpu.roll(x, LANES - 16, 1) * sm


def _rope_t(dy, c, sp, sm):
    return dy * c + pltpu.roll(dy * sp, LANES - 16, 1) + pltpu.roll(dy * sm, 16, 1)


def _rope_tables(positions, theta, rot_dim, lane0, name):
    n = positions.shape[0]
    half = rot_dim // 2
    inv_freq = 1.0 / (theta ** (jnp.arange(half, dtype=F32) * (2.0 / rot_dim)))
    freq = jnp.concatenate([jnp.zeros((lane0,), F32), inv_freq, inv_freq,
                            jnp.zeros((LANES - lane0 - rot_dim,), F32)]).reshape(1, LANES)

    def fn(p, f):
        ang = p * f
        cos, sin = jnp.cos(ang), jnp.sin(ang)
        lane = lax.broadcasted_iota(jnp.int32, ang.shape, 1) - lane0
        first, second = (lane >= 0) & (lane < half), (lane >= half) & (lane < rot_dim)
        return jnp.where(first | second, cos, 1.0), jnp.where(second, sin, 0.0), jnp.where(first, -sin, 0.0)

    return _rowwise(fn, [positions.astype(F32).reshape(n, 1)], [freq], [(LANES, F32)] * 3, tm=512, name=name)


def _value_heads_t(c_kv, w_vt, bsz, t):
    n, k = c_kv.shape
    tm, tn = 512, 1024

    def body(a_ref, b_ref, o_ref):
        acc = _dot(a_ref[...], b_ref[...])
        lane = lax.broadcasted_iota(jnp.int32, (1, tn), 1)
        acc = acc + jnp.where(lax.rem(lane, LANES) == B_VDIM, 1.0, 0.0)
        o_ref[...] = acc.T.astype(BF16)

    per_seq = t // tm
    return pl.pallas_call(
        body, name="kv_up_v_t", grid=(w_vt.shape[1] // tn, n // tm),
        in_specs=[pl.BlockSpec((tm, k), lambda j, i: (i, 0)), pl.BlockSpec((k, tn), lambda j, i: (0, j))],
        out_specs=pl.BlockSpec((None, tn, tm), lambda j, i: (i // per_seq, j, lax.rem(i, per_seq))),
        out_shape=jax.ShapeDtypeStruct((bsz, w_vt.shape[1], t), BF16),
        compiler_params=_cparams(("parallel", "parallel")))(c_kv, w_vt)


def _rms(x, g):
    xf = x.astype(F32)
    return xf * lax.rsqrt(jnp.mean(xf * xf, axis=-1, keepdims=True) + NORM_EPS) * g


def _rms_bwd(x, g, dy):
    xf = x.astype(F32)
    rstd = lax.rsqrt(jnp.mean(xf * xf, axis=-1, keepdims=True) + NORM_EPS)
    xhat = xf * rstd
    dxhat = dy * g
    dx = rstd * (dxhat - xhat * jnp.mean(dxhat * xhat, axis=-1, keepdims=True))
    return dx, dy * xhat


def _silu(z):
    return z * jax.nn.sigmoid(z)


def _silu_grad(z):
    s = jax.nn.sigmoid(z)
    return s * (1.0 + z * (1.0 - s))


def _dot_nt(a, b):
    return lax.dot_general(a, b, (((1,), (1,)), ((), ())), preferred_element_type=F32)


def _dot_tn(a, b):
    return lax.dot_general(a, b, (((0,), (0,)), ((), ())), preferred_element_type=F32)


def _dot(a, b):
    return jnp.dot(a, b, preferred_element_type=F32)


A_SCALE = A_HEAD_DIM ** -0.5
LOG2E = 1.4426950408889634
A_C2 = A_SCALE * LOG2E
A_HEAD_GROUP = 4
A_FWD_HEAD_GROUP = 8
A_BLOCKS_PER_STEP = 4
A_SEQS_PER_STEP = 4


def _a_steps(nb):
    return (min(nb, A_BLOCKS_PER_STEP), 1) if nb > 1 else (1, A_SEQS_PER_STEP)


def _attn_a_fwd(qk, v, name):
    s_n, ln, _ = qk.shape
    nb = ln // BAND
    lb, sb = _a_steps(nb)
    single = nb == 1
    rows = lb * BAND
    n_keys = BAND if single else 2 * BAND

    def body(q_ref, kc_ref, kp_ref, vc_ref, vp_ref, o_ref, lse_ref):
        kpos = lax.broadcasted_iota(jnp.int32, (n_keys, BAND), 0)
        qpos = lax.broadcasted_iota(jnp.int32, (n_keys, BAND), 1) + (n_keys - BAND)
        band = (kpos <= qpos) & (kpos >= qpos - BAND)
        at_start = band & (kpos >= jnp.where(pl.program_id(1) > 0, 0, n_keys - BAND))
        for s in range(sb):
            for u in range(lb):
                r = slice(u * BAND, (u + 1) * BAND)
                mask = at_start if u == 0 else band

                def keys(prev_ref, cur_ref, hs):
                    cur = cur_ref[s, r, hs]
                    if single:
                        return cur
                    prev = prev_ref[s, :, hs] if u == 0 else cur_ref[s, (u - 1) * BAND:u * BAND, hs]
                    return jnp.concatenate([prev, cur], axis=0)

                stats = []
                for h0 in range(0, A_HEADS, A_FWD_HEAD_GROUP):
                    hss = [slice(h * A_HEAD_DIM, (h + 1) * A_HEAD_DIM) for h in range(h0, h0 + A_FWD_HEAD_GROUP)]
                    sts = [_dot_nt(keys(kp_ref, kc_ref, hs), q_ref[s, r, hs]) for hs in hss]
                    ps, ls = [], []
                    for st in sts:
                        st = jnp.where(mask, st * A_C2, NEG)
                        m = jnp.max(st, axis=0, keepdims=True)
                        p = jnp.exp2(st - m)
                        l_row = jnp.sum(p, axis=0, keepdims=True)
                        ps.append(p.astype(BF16))
                        ls.append(l_row)
                        stats.append(m + jnp.log2(l_row))
                    ots = [_dot_tn(keys(vp_ref, vc_ref, hs), p) for hs, p in zip(hss, ps)]
                    for hs, o_t, l_row in zip(hss, ots, ls):
                        o_ref[s, r, hs] = (o_t / l_row).T.astype(BF16)
                lse_ref[s, :, r] = jnp.concatenate(stats, axis=0)

    def cur(c):
        return pl.BlockSpec((sb, rows, A_WIDTH), lambda s, l: (s, l, c))

    def prev(c):
        return pl.BlockSpec((sb, BAND, A_WIDTH), lambda s, l: (s, jnp.maximum(l * lb - 1, 0), c))

    return pl.pallas_call(
        body, name=name, grid=(s_n // sb, nb // lb),
        in_specs=[cur(0), cur(1), prev(1), cur(0), prev(0)],
        out_specs=[cur(0), pl.BlockSpec((sb, A_HEADS, rows), lambda s, l: (s, 0, l))],
        out_shape=[jax.ShapeDtypeStruct((s_n, ln, A_WIDTH), BF16), jax.ShapeDtypeStruct((s_n, A_HEADS, ln), F32)],
        compiler_params=_cparams(("parallel", "arbitrary")))(qk, qk, qk, v, v)


def _attn_a_bwd(qk, v, do, lse2, dsum, tabs, name):
    s_n, ln, _ = qk.shape
    nb = ln // BAND
    lb, sb = _a_steps(nb)
    single = nb == 1
    rows = lb * BAND
    n_steps = nb // lb

    def body(q_ref, qn_ref, kc_ref, kp_ref, vc_ref, vp_ref, do_ref, don_ref, lse_ref, lsen_ref, ds_ref, dsn_ref,
             c_ref, sp_ref, sm_ref, out_ref):
        l_idx = pl.program_id(1)
        kpos = lax.broadcasted_iota(jnp.int32, (2 * BAND, BAND), 0)
        qpos = lax.broadcasted_iota(jnp.int32, (2 * BAND, BAND), 1) + BAND
        band_q = (kpos <= qpos) & (kpos >= qpos - BAND)
        start_q = band_q & (kpos >= jnp.where(l_idx > 0, 0, BAND))
        kpos2 = lax.broadcasted_iota(jnp.int32, (BAND, 2 * BAND), 0)
        qpos2 = lax.broadcasted_iota(jnp.int32, (BAND, 2 * BAND), 1)
        band_k = (kpos2 <= qpos2) & (kpos2 >= qpos2 - BAND)
        end_k = band_k & (qpos2 < jnp.where(l_idx < n_steps - 1, 2 * BAND, BAND))
        causal = kpos[:BAND] <= qpos[:BAND] - BAND
        for s in range(sb):
            for u in range(lb):
                r = slice(u * BAND, (u + 1) * BAND)
                c, sp, sm = c_ref[s, r, :], sp_ref[s, r, :], sm_ref[s, r, :]
                lse_q, ds_q = lse_ref[s, :, r], ds_ref[s, :, r]

                def store(hl, hss, dqs, dks, dvs):
                    for i, h in enumerate(hl):
                        out_ref[s, r, hss[i]] = _rope_t(dqs[i], c, sp, sm).astype(BF16)
                        out_ref[s, r, A_WIDTH + h * A_HEAD_DIM:A_WIDTH + (h + 1) * A_HEAD_DIM] = \
                            _rope_t(dks[i], c, sp, sm).astype(BF16)
                        out_ref[s, r, 2 * A_WIDTH + h * A_HEAD_DIM:2 * A_WIDTH + (h + 1) * A_HEAD_DIM] = \
                            dvs[i].astype(BF16)

                if single:
                    for h0 in range(0, A_HEADS, A_HEAD_GROUP):
                        hl = list(range(h0, h0 + A_HEAD_GROUP))
                        hss = [slice(h * A_HEAD_DIM, (h + 1) * A_HEAD_DIM) for h in hl]
                        sts = [_dot_nt(kc_ref[s, r, hs], q_ref[s, r, hs]) for hs in hss]
                        dpts = [_dot_nt(vc_ref[s, r, hs], do_ref[s, r, hs]) for hs in hss]
                        ps, dsts = [], []
                        for i, h in enumerate(hl):
                            p = jnp.exp2(jnp.where(causal, sts[i] * A_C2, NEG) - lse_q[h:h + 1])
                            dsts.append((p * (dpts[i] - ds_q[h:h + 1]) * A_SCALE).astype(BF16))
                            ps.append(p.astype(BF16))
                        store(hl, hss, [_dot_tn(dst, kc_ref[s, r, hs]) for dst, hs in zip(dsts, hss)],
                              [_dot(dst, q_ref[s, r, hs]) for dst, hs in zip(dsts, hss)],
                              [_dot(p, do_ref[s, r, hs]) for p, hs in zip(ps, hss)])
                    continue

                r_prev = slice((u - 1) * BAND, u * BAND)
                r_next = slice((u + 1) * BAND, (u + 2) * BAND)

                def before(prev_ref, cur_ref, hs):
                    return prev_ref[s, :, hs] if u == 0 else cur_ref[s, r_prev, hs]

                def after(next_ref, cur_ref, hs):
                    return next_ref[s, :, hs] if u == lb - 1 else cur_ref[s, r_next, hs]

                mask_q = start_q if u == 0 else band_q
                mask_k = end_k if u == lb - 1 else band_k
                lse_n = lsen_ref[s] if u == lb - 1 else lse_ref[s, :, r_next]
                ds_n = dsn_ref[s] if u == lb - 1 else ds_ref[s, :, r_next]
                lse_k = jnp.concatenate([lse_q, lse_n], axis=1)
                ds_k = jnp.concatenate([ds_q, ds_n], axis=1)
                for h0 in range(0, A_HEADS, A_HEAD_GROUP):
                    hl = list(range(h0, h0 + A_HEAD_GROUP))
                    hss = [slice(h * A_HEAD_DIM, (h + 1) * A_HEAD_DIM) for h in hl]
                    k2s = [jnp.concatenate([before(kp_ref, kc_ref, hs), kc_ref[s, r, hs]], axis=0) for hs in hss]
                    v2s = [jnp.concatenate([before(vp_ref, vc_ref, hs), vc_ref[s, r, hs]], axis=0) for hs in hss]
                    q2s = [jnp.concatenate([q_ref[s, r, hs], after(qn_ref, q_ref, hs)], axis=0) for hs in hss]
                    do2s = [jnp.concatenate([do_ref[s, r, hs], after(don_ref, do_ref, hs)], axis=0) for hs in hss]
                    sts = [_dot_nt(k2, q_ref[s, r, hs]) for k2, hs in zip(k2s, hss)]
                    dpts = [_dot_nt(v2, do_ref[s, r, hs]) for v2, hs in zip(v2s, hss)]
                    st2s = [_dot_nt(kc_ref[s, r, hs], q2) for q2, hs in zip(q2s, hss)]
                    dpt2s = [_dot_nt(vc_ref[s, r, hs], do2) for do2, hs in zip(do2s, hss)]
                    dsts, dst2s, p2s = [], [], []
                    for i, h in enumerate(hl):
                        p = jnp.exp2(jnp.where(mask_q, sts[i] * A_C2, NEG) - lse_q[h:h + 1])
                        dsts.append((p * (dpts[i] - ds_q[h:h + 1]) * A_SCALE).astype(BF16))
                        p2 = jnp.exp2(jnp.where(mask_k, st2s[i] * A_C2, NEG) - lse_k[h:h + 1])
                        dst2s.append((p2 * (dpt2s[i] - ds_k[h:h + 1]) * A_SCALE).astype(BF16))
                        p2s.append(p2.astype(BF16))
                    store(hl, hss, [_dot_tn(dsts[i], k2s[i]) for i in range(A_HEAD_GROUP)],
                          [_dot(dst2s[i], q2s[i]) for i in range(A_HEAD_GROUP)],
                          [_dot(p2s[i], do2s[i]) for i in range(A_HEAD_GROUP)])

    def cur(c, width=A_WIDTH):
        return pl.BlockSpec((sb, rows, width), lambda s, l: (s, l, c))

    def prev(c):
        return pl.BlockSpec((sb, BAND, A_WIDTH), lambda s, l: (s, jnp.maximum(l * lb - 1, 0), c))

    def nxt(c):
        return pl.BlockSpec((sb, BAND, A_WIDTH), lambda s, l: (s, jnp.minimum((l + 1) * lb, nb - 1), c))

    stat = pl.BlockSpec((sb, A_HEADS, rows), lambda s, l: (s, 0, l))
    stat_next = pl.BlockSpec((sb, A_HEADS, BAND), lambda s, l: (s, 0, jnp.minimum((l + 1) * lb, nb - 1)))
    tspec = cur(0, LANES)
    in_specs = [cur(0), nxt(0), cur(1), prev(1), cur(0), prev(0), cur(0), nxt(0),
                stat, stat_next, stat, stat_next, tspec, tspec, tspec]
    return pl.pallas_call(
        body, name=name, grid=(s_n // sb, n_steps), in_specs=in_specs,
        out_specs=cur(0, 3 * A_WIDTH),
        out_shape=jax.ShapeDtypeStruct((s_n, ln, 3 * A_WIDTH), BF16),
        compiler_params=_cparams(("parallel", "arbitrary")))(
            qk, qk, qk, qk, v, v, do, do, lse2, lse2, dsum, dsum, *tabs)


B_TQ = 256
B_FWD_HEADS = 4
B_BWD_PAIRS = 2
B_SCALE = B_QK_DIM ** -0.5
B_C2 = B_SCALE * LOG2E


def _key_le_query(kb, qb, tk, tq):
    kpos = kb * tk + lax.broadcasted_iota(jnp.int32, (tk, tq), 0)
    qpos = qb * tq + lax.broadcasted_iota(jnp.int32, (tk, tq), 1)
    return kpos <= qpos


def _attn_b_fwd(q, kpad, vt1):
    bsz, t, _ = q.shape
    tq = tk = B_TQ
    nq = t // tq
    nh = B_FWD_HEADS

    def body(q_ref, k_ref, vt_ref, o_ref, lse_ref):
        qblk = pl.program_id(2)
        qs = [q_ref[:, hh * LANES:(hh + 1) * LANES] for hh in range(nh)]

        def scores(kb):
            start = pl.multiple_of(kb * tk, tk)
            return [_dot_nt(k_ref[pl.ds(start, tk), hh * LANES:(hh + 1) * LANES], qs[hh]) for hh in range(nh)]

        def pv(kb, ps):
            start = pl.multiple_of(kb * tk, tk)
            return [_dot(vt_ref[hh * LANES:(hh + 1) * LANES, pl.ds(start, tk)], ps[hh]) for hh in range(nh)]

        def softmax(ss, ms, accs, kb, masked):
            out_m, out_acc, out_p = [], [], []
            for hh in range(nh):
                s = ss[hh] * B_C2
                if masked:
                    s = jnp.where(_key_le_query(kb, qblk, tk, tq), s, NEG)
                m_new = jnp.maximum(ms[hh], jnp.max(s, axis=0, keepdims=True))
                out_acc.append(jnp.exp2(ms[hh] - m_new) * accs[hh])
                out_p.append(jnp.exp2(s - m_new).astype(BF16))
                out_m.append(m_new)
            return out_m, out_acc, out_p

        def step(kb, carry):
            ss, ps, ms, accs = carry
            pvs = pv(jnp.maximum(kb - 1, 0), ps)
            ss_next = scores(kb + 1)
            accs = [accs[hh] + pvs[hh] for hh in range(nh)]
            ms, accs, ps = softmax(ss, ms, accs, kb, False)
            return (ss_next, ps, ms, accs)

        init = (scores(0), [jnp.zeros((tk, tq), BF16)] * nh, [jnp.full((1, tq), NEG, F32)] * nh,
                [jnp.zeros((LANES, tq), F32)] * nh)
        ss, ps, ms, accs = lax.fori_loop(0, qblk, step, init)
        pvs = pv(jnp.maximum(qblk - 1, 0), ps)
        accs = [accs[hh] + pvs[hh] for hh in range(nh)]
        ms, accs, ps = softmax(ss, ms, accs, qblk, True)
        pvs = pv(qblk, ps)
        accs = [accs[hh] + pvs[hh] for hh in range(nh)]
        ls = [accs[hh][B_VDIM:B_VDIM + 1] for hh in range(nh)]
        o_t = jnp.concatenate([accs[hh][:B_VDIM] / ls[hh] for hh in range(nh)], axis=0)
        o_ref[...] = o_t.T.astype(BF16)
        for pair in range(nh // 2):
            lse_ref[pair] = jnp.concatenate([ms[2 * pair + hh] + jnp.log2(ls[2 * pair + hh]) for hh in range(2)]
                                            + [jnp.zeros((6, tq), F32)], axis=0)

    return pl.pallas_call(
        body, name="attn_b_fwd", grid=(bsz, B_HEADS // nh, nq),
        in_specs=[pl.BlockSpec((None, tq, nh * LANES), lambda b, j, i: (b, i, j)),
                  pl.BlockSpec((None, t, nh * LANES), lambda b, j, i: (b, 0, j)),
                  pl.BlockSpec((None, nh * LANES, t), lambda b, j, i: (b, j, 0))],
        out_specs=[pl.BlockSpec((None, tq, nh * B_VDIM), lambda b, j, i: (b, i, j)),
                   pl.BlockSpec((None, nh // 2, 8, tq), lambda b, j, i: (b, j, 0, i))],
        out_shape=[jax.ShapeDtypeStruct((bsz, t, B_HEADS * B_VDIM), BF16),
                   jax.ShapeDtypeStruct((bsz, B_HEADS // 2, 8, t), F32)],
        compiler_params=_cparams(("parallel", "parallel", "arbitrary")))(q, kpad, vt1)


def _attn_b_bwd(q, kpad, v, do, o, lse2, tabs):
    bsz, t, _ = q.shape
    tq = tk = B_TQ
    nq = t // tq
    n_pairs = B_BWD_PAIRS
    n_heads = 2 * n_pairs

    def body(q_ref, k_ref, v_ref, do_ref, o_ref, lse_ref, c_ref, sp_ref, sm_ref, dq_ref, dk_ref, dv_ref,
             dqt_scr, dsum_scr):
        lane = lax.broadcasted_iota(jnp.int32, (tk, LANES), 1)
        sel_lane = lax.broadcasted_iota(jnp.int32, (8, LANES), 1)
        sel_row = lax.broadcasted_iota(jnp.int32, (8, LANES), 0)
        sel = jnp.where((sel_lane < B_VDIM) == (sel_row == 0), 1.0, 0.0)
        sel = jnp.where(sel_row < 2, sel, 0.0).astype(BF16)

        def rows(blk):
            return pl.ds(pl.multiple_of(blk * tq, tq), tq)

        def dsum_step(qb, carry):
            for pp in range(n_pairs):
                pls = slice(pp * LANES, (pp + 1) * LANES)
                prod = do_ref[rows(qb), pls].astype(F32) * o_ref[rows(qb), pls]
                hi = prod.astype(BF16)
                lo = (prod - hi.astype(F32)).astype(BF16)
                dsum_scr[pp, :, rows(qb)] = _dot_nt(sel, hi) + _dot_nt(sel, lo)
            return carry

        lax.fori_loop(0, nq, dsum_step, 0)
        dqt_scr[...] = jnp.zeros_like(dqt_scr)

        def kv_step(kb, carry):
            ks = [k_ref[rows(kb), hh * LANES:(hh + 1) * LANES] for hh in range(n_heads)]
            vs = []
            for pp in range(n_pairs):
                vb = v_ref[rows(kb), pp * LANES:(pp + 1) * LANES]
                zero = jnp.zeros_like(vb)
                vs += [jnp.where(lane < B_VDIM, vb, zero), jnp.where(lane < B_VDIM, zero, vb)]

            def make_step(masked):
                def step(qb, acc):
                    qs = [q_ref[rows(qb), hh * LANES:(hh + 1) * LANES] for hh in range(n_heads)]
                    dob = [do_ref[rows(qb), pp * LANES:(pp + 1) * LANES] for pp in range(n_pairs)]
                    ss = [_dot_nt(ks[hh], qs[hh]) for hh in range(n_heads)]
                    dps = [_dot_nt(vs[hh], dob[hh // 2]) for hh in range(n_heads)]
                    pbs, dss = [], []
                    for hh in range(n_heads):
                        stat = (hh // 2, slice(hh % 2, hh % 2 + 1), rows(qb))
                        s = ss[hh] * B_C2
                        if masked:
                            s = jnp.where(_key_le_query(kb, qb, tk, tq), s, NEG)
                        p = jnp.exp2(s - lse_ref[stat])
                        dss.append((p * (dps[hh] - dsum_scr[stat]) * B_SCALE).astype(BF16))
                        pbs.append(p.astype(BF16))
                    for hh in range(n_heads):
                        dqt_scr[hh, :, rows(qb)] += _dot_tn(ks[hh], dss[hh])
                    return tuple([acc[hh] + _dot(dss[hh], qs[hh]) for hh in range(n_heads)]
                                 + [acc[n_heads + hh] + _dot(pbs[hh], dob[hh // 2]) for hh in range(n_heads)])
                return step

            acc = make_step(True)(kb, (jnp.zeros((tk, LANES), F32),) * (2 * n_heads))
            acc = lax.fori_loop(kb + 1, nq, make_step(False), acc)
            for hh in range(n_heads):
                dk_ref[rows(kb), hh * LANES:(hh + 1) * LANES] = acc[hh].astype(BF16)
            for pp in range(n_pairs):
                dv_pair = jnp.where(lane < B_VDIM, acc[n_heads + 2 * pp], acc[n_heads + 2 * pp + 1])
                dv_ref[rows(kb), pp * LANES:(pp + 1) * LANES] = dv_pair.astype(BF16)
            return carry

        lax.fori_loop(0, nq, kv_step, 0)

        def dq_step(qb, carry):
            c, sp, sm = c_ref[rows(qb), :], sp_ref[rows(qb), :], sm_ref[rows(qb), :]
            for hh in range(n_heads):
                dq_ref[rows(qb), hh * LANES:(hh + 1) * LANES] = _rope_t(dqt_scr[hh, :, rows(qb)].T, c, sp, sm).astype(BF16)
            return carry

        lax.fori_loop(0, nq, dq_step, 0)

    pair_full = pl.BlockSpec((None, t, n_heads * LANES), lambda b, j: (b, 0, j))
    one_full = pl.BlockSpec((None, t, n_pairs * LANES), lambda b, j: (b, 0, j))
    row_full = pl.BlockSpec((None, n_pairs, 8, t), lambda b, j: (b, j, 0, 0))
    tab_full = pl.BlockSpec((None, t, LANES), lambda b, j: (b, 0, 0))
    return pl.pallas_call(
        body, name="attn_b_bwd", grid=(bsz, B_HEADS // n_heads),
        in_specs=[pair_full, pair_full, one_full, one_full, one_full, row_full, tab_full, tab_full, tab_full],
        out_specs=[pair_full, pair_full, one_full],
        out_shape=[jax.ShapeDtypeStruct((bsz, t, B_KPAD), BF16), jax.ShapeDtypeStruct((bsz, t, B_KPAD), BF16),
                   jax.ShapeDtypeStruct((bsz, t, B_HEADS * B_VDIM), BF16)],
        scratch_shapes=[pltpu.VMEM((n_heads, LANES, t), F32), pltpu.VMEM((n_pairs, 8, t), F32)],
        compiler_params=_cparams(("parallel", "parallel")))(q, kpad, v, do, o, lse2, *tabs)


ANY = pl.BlockSpec(memory_space=pl.ANY)


def _all_gather(shard, name):
    def body(x_ref, out_ref, send_sems, recv_sems, local_sem):
        x, y, c = lax.axis_index("x"), lax.axis_index("y"), lax.axis_index("c")
        me, sibling = (x, y, c), (x, y, 1 - c)
        chips = [(1 - x, y), (x, 1 - y), (1 - x, 1 - y)]

        def rows(px, py, pc):
            return out_ref.at[4 * px + 2 * py + pc]

        def copy(k, block, to, src=None):
            return pltpu.make_async_remote_copy(
                src_ref=rows(*block) if src is None else src, dst_ref=rows(*block),
                send_sem=send_sems.at[k], recv_sem=recv_sems.at[k], device_id=to, device_id_type=MESH)

        mine = pltpu.make_async_copy(x_ref, rows(*me), local_sem)
        mine.start()
        first = [copy(0, me, sibling, src=x_ref)]
        first += [copy(1 + j, me, (*chip, c), src=x_ref) for j, chip in enumerate(chips)]
        for cp in first:
            cp.start()
        passed = [copy(4 + j, (*chip, c), sibling) for j, chip in enumerate(chips)]
        for j, chip in enumerate(chips):
            copy(1 + j, (*chip, c), me).wait_recv()
            passed[j].start()
        copy(0, sibling, me).wait_recv()
        for j, chip in enumerate(chips):
            copy(4 + j, (*chip, 1 - c), me).wait_recv()
        for cp in first + passed:
            cp.wait_send()
        mine.wait()

    return pl.pallas_call(
        body, name=name, in_specs=[ANY], out_specs=ANY,
        out_shape=jax.ShapeDtypeStruct((N_DEV,) + shard.shape, shard.dtype),
        scratch_shapes=[pltpu.SemaphoreType.DMA((7,)), pltpu.SemaphoreType.DMA((7,)), pltpu.SemaphoreType.DMA])(shard)


def _all_gather_weights(flat, wide, name):
    ns = wide.shape[1]

    def body(f_ref, w_ref, fo_ref, wo_ref, send_sems, recv_sems, local_sems):
        x, y, c = lax.axis_index("x"), lax.axis_index("y"), lax.axis_index("c")
        me, sibling = (x, y, c), (x, y, 1 - c)
        chips = [(1 - x, y), (x, 1 - y), (1 - x, 1 - y)]

        def place(a, px, py, pc):
            idx = 4 * px + 2 * py + pc
            if a == 0:
                return fo_ref.at[idx]
            return wo_ref.at[:, pl.ds(pl.multiple_of(idx * ns, LANES), ns)]

        def copy(a, k, block, to, src=None):
            return pltpu.make_async_remote_copy(
                src_ref=place(a, *block) if src is None else src, dst_ref=place(a, *block),
                send_sem=send_sems.at[a, k], recv_sem=recv_sems.at[a, k], device_id=to, device_id_type=MESH)

        relay_in = (c * x + (1 - c) * (1 - x), c * (1 - y) + (1 - c) * y)
        relay_out = (c * (1 - x) + (1 - c) * x, c * y + (1 - c) * (1 - y))
        own = (f_ref, w_ref)
        mine = [pltpu.make_async_copy(own[a], place(a, *me), local_sems.at[a]) for a in range(2)]
        first = []
        for a in range(2):
            mine[a].start()
            first.append(copy(a, 0, me, sibling, src=own[a]))
            first += [copy(a, 1 + j, me, (*chip, c), src=own[a]) for j, chip in enumerate(chips[:2])]
        for cp in first:
            cp.start()
        relay = [copy(a, 3, (*relay_in, c), (*relay_out, c)) for a in range(2)]
        passed = [[copy(a, 4 + j, (*chip, c), sibling) for j, chip in enumerate(chips)] for a in range(2)]
        for a in range(2):
            for j, chip in enumerate(chips[:2]):
                copy(a, 1 + j, (*chip, c), me).wait_recv()
                passed[a][j].start()
            relay[a].start()
        for a in range(2):
            copy(a, 3, (*chips[2], c), me).wait_recv()
            passed[a][2].start()
        for a in range(2):
            copy(a, 0, sibling, me).wait_recv()
            for j, chip in enumerate(chips):
                copy(a, 4 + j, (*chip, 1 - c), me).wait_recv()
        for cp in first + relay + passed[0] + passed[1]:
            cp.wait_send()
        for cp in mine:
            cp.wait()

    return pl.pallas_call(
        body, name=name, in_specs=[ANY, ANY], out_specs=[ANY, ANY],
        out_shape=[jax.ShapeDtypeStruct((N_DEV,) + flat.shape, flat.dtype),
                   jax.ShapeDtypeStruct((wide.shape[0], N_DEV * ns), wide.dtype)],
        scratch_shapes=[pltpu.SemaphoreType.DMA((2, 7)), pltpu.SemaphoreType.DMA((2, 7)),
                        pltpu.SemaphoreType.DMA((2,))])(flat, wide)


N_CHIPS = 4


def _exchange_sibling(blocks, name):
    def body(g_ref, got_ref, send_sems, recv_sems):
        x, y, c = lax.axis_index("x"), lax.axis_index("y"), lax.axis_index("c")
        sends = [pltpu.make_async_remote_copy(
            src_ref=g_ref.at[q, 1 - c], dst_ref=got_ref.at[q], send_sem=send_sems.at[q],
            recv_sem=recv_sems.at[q], device_id=(x, y, 1 - c), device_id_type=MESH) for q in range(N_CHIPS)]
        for cp in sends:
            cp.start()
        for cp in sends:
            cp.wait_recv()
        for cp in sends:
            cp.wait_send()

    return pl.pallas_call(
        body, name=name, in_specs=[ANY], out_specs=ANY,
        out_shape=jax.ShapeDtypeStruct((N_CHIPS,) + blocks.shape[2:], blocks.dtype),
        scratch_shapes=[pltpu.SemaphoreType.DMA((N_CHIPS,)), pltpu.SemaphoreType.DMA((N_CHIPS,))])(blocks)


def _exchange_chips(parts, name):
    def body(p_ref, out_ref, send_sems, recv_sems, local_sem):
        x, y, c = lax.axis_index("x"), lax.axis_index("y"), lax.axis_index("c")
        me = 2 * x + y

        def peer(k):
            return (1 - x if k & 2 else x, 1 - y if k & 1 else y)

        def copy(k):
            px, py = peer(k)
            return pltpu.make_async_remote_copy(
                src_ref=p_ref.at[2 * px + py], dst_ref=out_ref.at[me], send_sem=send_sems.at[k - 1],
                recv_sem=recv_sems.at[k - 1], device_id=(px, py, c), device_id_type=MESH)

        def arrival(k):
            px, py = peer(k)
            slot = out_ref.at[2 * px + py]
            return pltpu.make_async_remote_copy(
                src_ref=slot, dst_ref=slot, send_sem=send_sems.at[k - 1], recv_sem=recv_sems.at[k - 1],
                device_id=(px, py, c), device_id_type=MESH)

        mine = pltpu.make_async_copy(p_ref.at[me], out_ref.at[me], local_sem)
        mine.start()
        sends = [copy(k) for k in range(1, N_CHIPS)]
        for cp in sends:
            cp.start()
        for k in range(1, N_CHIPS):
            arrival(k).wait_recv()
        for cp in sends:
            cp.wait_send()
        mine.wait()

    return pl.pallas_call(
        body, name=name, in_specs=[ANY], out_specs=ANY,
        out_shape=jax.ShapeDtypeStruct(parts.shape, parts.dtype),
        scratch_shapes=[pltpu.SemaphoreType.DMA((N_CHIPS - 1,)), pltpu.SemaphoreType.DMA((N_CHIPS - 1,)),
                        pltpu.SemaphoreType.DMA])(parts)


def _add_pairs(blocks, got, core, *, tr, name):
    q, r, c = got.shape

    def body(core_ref, a_ref, b_ref, o_ref):
        o_ref[...] = (a_ref[...].astype(F32) + b_ref[...].astype(F32)).astype(o_ref.dtype)

    spec = pl.BlockSpec((q, tr, c), lambda i, core_ref: (0, i, 0))
    mine = pl.BlockSpec((q, None, tr, c), lambda i, core_ref: (0, core_ref[0], i, 0))
    return pl.pallas_call(
        body, name=name,
        grid_spec=pltpu.PrefetchScalarGridSpec(num_scalar_prefetch=1, grid=(r // tr,), in_specs=[mine, spec],
                                               out_specs=spec),
        out_shape=jax.ShapeDtypeStruct(got.shape, BF16), compiler_params=_cparams(("parallel",)))(
            core, blocks, got)


def _sum_slots(slots, *, tr, name):
    n_slots, r, c = slots.shape

    def body(s_ref, o_ref):
        acc = s_ref[0].astype(F32)
        for s in range(1, n_slots):
            acc = acc + s_ref[s].astype(F32)
        o_ref[...] = acc

    return pl.pallas_call(
        body, name=name, grid=(r // tr,),
        in_specs=[pl.BlockSpec((n_slots, tr, c), lambda i: (0, i, 0))],
        out_specs=pl.BlockSpec((tr, c), lambda i: (i, 0)),
        out_shape=jax.ShapeDtypeStruct((r, c), F32),
        compiler_params=_cparams(("parallel",)))(slots)


def _adamw_math(w_t, g_t, m_t, v_t):
    m_n = ADAM_B1 * m_t + (1.0 - ADAM_B1) * g_t
    v_n = ADAM_B2 * v_t + (1.0 - ADAM_B2) * (g_t * g_t)
    m_hat = m_n / (1.0 - ADAM_B1 ** ADAM_STEP)
    v_hat = v_n / (1.0 - ADAM_B2 ** ADAM_STEP)
    delta = -ADAM_LR * (m_hat / (jnp.sqrt(v_hat) + ADAM_EPS) + ADAM_WD * w_t)
    return delta, m_n, v_n


def _adamw(w, g, m, v, name):
    shape = w.shape
    cols = shape[-1]
    args = [a.reshape(-1, cols) for a in (w, g, m, v)]
    rows = args[0].shape[0]
    tm = 256 if rows % 256 == 0 else rows
    delta, m_n, v_n = _rowwise(_adamw_math, args, [], [(cols, F32)] * 3, tm=tm, name=name)
    return delta.reshape(shape), m_n.reshape(shape), v_n.reshape(shape)


def _adamw_small(ws, gs, ms, vs, name):
    k = len(ws)
    args = [a.reshape(-1, a.shape[-1]) for group in (ws, gs, ms, vs) for a in group]

    def body(*refs):
        ins, outs = refs[:4 * k], refs[4 * k:]
        for i in range(k):
            res = _adamw_math(*[ins[j * k + i][...] for j in range(4)])
            for j in range(3):
                outs[j * k + i][...] = res[j]

    res = pl.pallas_call(
        body, name=name, out_shape=[jax.ShapeDtypeStruct(a.shape, F32) for a in args[:k]] * 3,
        compiler_params=pltpu.CompilerParams(vmem_limit_bytes=VMEM_LIMIT))(*args)
    return [[res[j * k + i].reshape(ws[i].shape) for i in range(k)] for j in range(3)]


def _local_step(x, positions, target, w):
    bsz, t, _ = x.shape
    n = bsz * t
    tm = 256
    mm = 512
    x2 = x.reshape(n, D_MODEL)
    tgt2 = target.reshape(n, D_MODEL)
    pos = positions.reshape(n)
    tb2 = _rope_tables(pos, B_ROPE_THETA, B_ROPE, B_NOPE, "rope_tables_b")
    tabs_b = [a.reshape(bsz, t, LANES) for a in tb2]

    w_a_in = w["a_w_in"]
    w_a_out = w["a_w_out"]
    w_down = w["kv_w_down"]
    w_down_p = jnp.zeros((D_MODEL, 3 * LANES), BF16).at[:, :B_KV_LORA].set(w_down[:, :B_KV_LORA])
    w_down_p = w_down_p.at[:, B_KV_LORA + B_NOPE:B_KV_LORA + B_QK_DIM].set(w_down[:, B_KV_LORA:])
    wu = w["kv_w_up"].reshape(B_KV_LORA, B_HEADS, B_NOPE + B_VDIM)
    w_upk = jnp.pad(wu[:, :, :B_NOPE], ((0, 0), (0, 0), (0, LANES - B_NOPE))).reshape(B_KV_LORA, B_KPAD)
    w_upv = wu[:, :, B_NOPE:].reshape(B_KV_LORA, B_HEADS * B_VDIM)
    w_b_in = w["b_w_in"]
    w_q_p = jnp.pad(w["b_w_q_up"].reshape(B_Q_LORA, B_HEADS, B_QK_DIM),
                    ((0, 0), (0, 0), (0, LANES - B_QK_DIM))).reshape(B_Q_LORA, B_KPAD)
    w_b_out = w["b_w_out"]

    def tab_extras(tabs2, rows):
        return [(a, (rows, LANES), lambda j, i, kk: (i, 0)) for a in tabs2]

    (hn_a,) = _rowwise(lambda xt, g: (_rms(xt, g),), [x2], [w["a_pre_norm"]], [(D_MODEL, BF16)],
                       tm=tm, name="a_pre_norm")

    def rope_epilogue(acc, o_ref, rows, c_ref, sp_ref, sm_ref):
        c, sp, sm = c_ref[rows, :], sp_ref[rows, :], sm_ref[rows, :]
        for h in range(acc.shape[1] // LANES):
            hs = slice(h * LANES, (h + 1) * LANES)
            o_ref[rows, hs] = _rope(acc[:, hs], c, sp, sm).astype(BF16)

    def to_group(a, d):
        if d == 1:
            return a
        return a.reshape(bsz, t // d, d, a.shape[-1]).transpose(0, 2, 1, 3).reshape(n, a.shape[-1])

    def from_group(a, d):
        if d == 1:
            return a
        return a.reshape(bsz, d, t // d, a.shape[-1]).transpose(0, 2, 1, 3).reshape(n, a.shape[-1])

    def rows_to_cols(r, d):
        return r.reshape(bsz, d, A_HEADS, t // d).transpose(0, 3, 1, 2).reshape(n, A_HEADS)

    def cols_to_rows(cc, d):
        return cc.reshape(bsz, t // d, d, A_HEADS).transpose(0, 2, 3, 1).reshape(bsz * d, A_HEADS, t // d)

    z_a = _matmul(hn_a, w_a_in, b_cols=(A_QKV, A_WIDTH), out_dtype=F32, tm=mm, tn=A_WIDTH, name="a_gate")
    hn_g, tabs_g, qk_g, v_g, o_g, lse_g = [], [], [], [], [], []
    for g, d in enumerate(A_DILATIONS):
        hn_g.append(to_group(hn_a, d))
        tabs_g.append(_rope_tables(to_group(pos.reshape(n, 1), d).reshape(n), A_ROPE_THETA, A_ROT_DIM, 0,
                                   f"rope_tables_a_g{g}"))
        col0 = g * 3 * A_WIDTH
        qk = _matmul(hn_g[g], w_a_in, b_cols=(col0, 2 * A_WIDTH), out_dtype=BF16, tm=mm, tn=A_WIDTH,
                     name=f"a_qk_g{g}", epilogue=rope_epilogue, epilogue_rows=EPILOGUE_ROWS,
                     extras=tab_extras(tabs_g[g], mm))
        v = _matmul(hn_g[g], w_a_in, b_cols=(col0 + 2 * A_WIDTH, A_WIDTH), out_dtype=BF16, tm=mm, tn=A_WIDTH,
                    name=f"a_v_g{g}")
        qk_g.append(qk.reshape(bsz * d, t // d, 2 * A_WIDTH))
        v_g.append(v.reshape(bsz * d, t // d, A_WIDTH))
        o, lse = _attn_a_fwd(qk_g[g], v_g[g], f"attn_a_fwd_g{g}")
        o_g.append(from_group(o.reshape(n, A_WIDTH), d))
        lse_g.append(rows_to_cols(lse, d))

    def merge_fn(o0, o1, o2, l0, l1, l2, z):
        lmax = jnp.maximum(jnp.maximum(l0, l1), l2)
        e0, e1, e2 = jnp.exp2(l0 - lmax), jnp.exp2(l1 - lmax), jnp.exp2(l2 - lmax)
        den = e0 + e1 + e2
        w0, w1, w2 = e0 / den, e1 / den, e2 / den
        parts = []
        for h in range(A_HEADS):
            hs = slice(h * A_HEAD_DIM, (h + 1) * A_HEAD_DIM)
            parts.append(w0[:, h:h + 1] * o0[:, hs] + w1[:, h:h + 1] * o1[:, hs] + w2[:, h:h + 1] * o2[:, hs])
        o = jnp.concatenate(parts, axis=1)
        return o * _silu(z), o, lmax + jnp.log2(den)

    y_a, o_a2, lse_a = _rowwise(merge_fn, [*o_g, *lse_g, z_a], [],
                                [(A_WIDTH, BF16), (A_WIDTH, BF16), (A_HEADS, F32)], tm=tm, name="a_merge_gate")
    out_a = _matmul(y_a, w_a_out, out_dtype=F32, tm=mm, tn=D_MODEL, name="a_out")

    def mid_fn(xt, out, g_post, g_kv, g_b):
        h1 = xt + _rms(out, g_post)
        return h1, _rms(h1, g_kv), _rms(h1, g_b)

    h1, hn_kv, hn_b = _rowwise(mid_fn, [x2, out_a], [w["a_post_norm"], w["kv_norm"], w["b_pre_norm"]],
                               [(D_MODEL, F32), (D_MODEL, BF16), (D_MODEL, BF16)], tm=tm, name="a_post_norm")

    ckr = _matmul(hn_kv, w_down_p, out_dtype=F32, tm=mm, tn=3 * LANES, name="kv_down")

    def latent_fn(ck, c, sp, sm, g):
        return _rms(ck[:, :B_KV_LORA], g), _rope(ck[:, B_KV_LORA:], c, sp, sm)

    c_kv, k_rope = _rowwise(latent_fn, [ckr, *tb2], [w["kv_latent_norm"]], [(B_KV_LORA, BF16), (LANES, F32)],
                            tm=tm, name="kv_latent_norm")

    def kpad_epilogue(acc, o_ref, rows, kr_ref):
        kr = kr_ref[rows, :]
        for h in range(acc.shape[1] // LANES):
            hs = slice(h * LANES, (h + 1) * LANES)
            o_ref[rows, hs] = (acc[:, hs] + kr).astype(BF16)

    kpad = _matmul(c_kv, w_upk, out_dtype=BF16, tm=mm, tn=1024, name="kv_up_k", epilogue=kpad_epilogue,
                   epilogue_rows=EPILOGUE_ROWS,
                   extras=[(k_rope, (mm, LANES), lambda j, i, kk: (i, 0))])
    v_b = _matmul(c_kv, w_upv, out_dtype=BF16, tm=mm, tn=1024, name="kv_up_v")

    proj_b = _matmul(hn_b, w_b_in, out_dtype=F32, tm=mm, tn=w_b_in.shape[1], name="b_in")
    (c_q,) = _rowwise(lambda p, g: (_rms(p[:, :B_Q_LORA], g),), [proj_b], [w["b_q_norm"]], [(B_Q_LORA, BF16)],
                      tm=tm, name="b_q_norm")

    q_b = _matmul(c_q, w_q_p, out_dtype=BF16, tm=mm, tn=1024, name="b_q_up", epilogue=rope_epilogue,
                  epilogue_rows=EPILOGUE_ROWS,
                  extras=tab_extras(tb2, mm))
    q_b3, kpad3, v_b3 = q_b.reshape(bsz, t, B_KPAD), kpad.reshape(bsz, t, B_KPAD), v_b.reshape(bsz, t, -1)
    w_vt = jnp.pad(wu[:, :, B_NOPE:], ((0, 0), (0, 0), (0, LANES - B_VDIM))).reshape(B_KV_LORA, B_KPAD)
    vt1 = _value_heads_t(c_kv, w_vt, bsz, t)
    o_b, lse_b = _attn_b_fwd(q_b3, kpad3, vt1)
    o_b2 = o_b.reshape(n, -1)
    (y_b,) = _rowwise(lambda o, p: (o * _silu(p[:, B_Q_LORA:]),), [o_b2, proj_b], [], [(D_MODEL, BF16)],
                      tm=tm, name="b_gate_mul")
    out_b = _matmul(y_b, w_b_out, out_dtype=F32, tm=mm, tn=D_MODEL, name="b_out")

    def head_fn(h1t, out, tgt, g):
        e = h1t + _rms(out, g) - tgt
        loss_row = 0.5 * jnp.mean(e * e, axis=-1, keepdims=True)
        dh2 = e * (1.0 / D_MODEL)
        d_out, dg = _rms_bwd(out, g, dh2)
        return dh2, d_out, dg, jnp.broadcast_to(loss_row * (1.0 / LANES), (loss_row.shape[0], LANES))

    dh2, d_out_b, dg_b_post, loss_acc = _rowwise(
        head_fn, [h1, out_b, tgt2], [w["b_post_norm"]], [(D_MODEL, F32), (D_MODEL, BF16)], [D_MODEL, LANES],
        tm=tm, name="loss_head")

    dy_b = _matmul(d_out_b, w_b_out, tb=True, out_dtype=BF16, tm=mm, tn=D_MODEL, name="b_out_dx")
    gw_b_out = _matmul(y_b, d_out_b, ta=True, out_dtype=BF16, tm=mm, tn=D_MODEL, tk=2048, name="b_out_dw")

    def gate_b_bwd(dy, o, p):
        z = p[:, B_Q_LORA:]
        return dy * _silu(z), dy * (o * _silu_grad(z))

    do_b, dz_b = _rowwise(gate_b_bwd, [dy_b, o_b2, proj_b], [], [(D_MODEL, BF16), (D_MODEL, F32)],
                          tm=tm, name="b_gate_bwd")
    do_b3 = do_b.reshape(bsz, t, -1)
    dq_b, dk_b, dv_b = _attn_b_bwd(q_b3, kpad3, v_b3, do_b3, o_b, lse_b, tabs_b)
    dq_b2, dk_b2, dv_b2 = dq_b.reshape(n, B_KPAD), dk_b.reshape(n, B_KPAD), dv_b.reshape(n, -1)

    dc_kv = _matmul(dk_b2, w_upk, tb=True, out_dtype=F32, tm=mm, tn=B_KV_LORA, name="kv_up_k_dx")
    dc_kv = _matmul(dv_b2, w_upv, tb=True, out_dtype=BF16, tm=mm, tn=B_KV_LORA, name="kv_up_v_dx",
                    epilogue=_add_epilogue, extras=[(dc_kv, (mm, B_KV_LORA), lambda j, i, kk: (i, j))])
    gw_upk = _matmul(c_kv, dk_b2, ta=True, out_dtype=BF16, tm=B_KV_LORA, tn=1024, tk=2048, name="kv_up_k_dw")
    gw_upv = _matmul(c_kv, dv_b2, ta=True, out_dtype=BF16, tm=B_KV_LORA, tn=1024, tk=2048, name="kv_up_v_dw")

    def latent_bwd(ck, dck, dk, c, sp, sm, g):
        d1, dg = _rms_bwd(ck[:, :B_KV_LORA], g, dck)
        ksum = dk[:, :LANES].astype(F32)
        for h in range(1, B_HEADS):
            ksum = ksum + dk[:, h * LANES:(h + 1) * LANES].astype(F32)
        lane = lax.broadcasted_iota(jnp.int32, ksum.shape, 1)
        ksum = jnp.where((lane >= B_NOPE) & (lane < B_QK_DIM), ksum, 0.0)
        return jnp.concatenate([d1, _rope_t(ksum, c, sp, sm)], axis=1), dg

    dckr, dg_latent = _rowwise(latent_bwd, [ckr, dc_kv, dk_b2, *tb2], [w["kv_latent_norm"]],
                               [(3 * LANES, BF16)], [B_KV_LORA], tm=tm, name="kv_latent_bwd")
    dhn_kv = _matmul(dckr, w_down_p, tb=True, out_dtype=BF16, tm=mm, tn=D_MODEL, name="kv_down_dx")
    gw_down_p = _matmul(hn_kv, dckr, ta=True, out_dtype=BF16, tm=mm, tn=3 * LANES, tk=2048, name="kv_down_dw")

    dc_q = _matmul(dq_b2, w_q_p, tb=True, out_dtype=BF16, tm=mm, tn=B_Q_LORA, name="b_q_up_dx")
    gw_q_p = _matmul(c_q, dq_b2, ta=True, out_dtype=BF16, tm=B_Q_LORA, tn=1024, tk=2048, name="b_q_up_dw")

    def q_norm_bwd(p, dcq, dz, g):
        d1, dg = _rms_bwd(p[:, :B_Q_LORA], g, dcq)
        return jnp.concatenate([d1, dz], axis=1), dg

    dproj_b, dg_q_norm = _rowwise(q_norm_bwd, [proj_b, dc_q, dz_b], [w["b_q_norm"]],
                                  [(w_b_in.shape[1], BF16)], [B_Q_LORA], tm=tm, name="b_q_norm_bwd")
    dhn_b = _matmul(dproj_b, w_b_in, tb=True, out_dtype=BF16, tm=mm, tn=D_MODEL, name="b_in_dx")
    gw_b_in = _matmul(hn_b, dproj_b, ta=True, out_dtype=BF16, tm=mm, tn=w_b_in.shape[1], tk=2048, name="b_in_dw")

    def mid_bwd(h1t, dh2t, dkv, db, g_kv, g_b, g_post, out):
        rstd = lax.rsqrt(jnp.mean(h1t * h1t, axis=-1, keepdims=True) + NORM_EPS)
        xhat = h1t * rstd
        dxhat = dkv * g_kv + db * g_b
        dh1 = dh2t + rstd * (dxhat - xhat * jnp.mean(dxhat * xhat, axis=-1, keepdims=True))
        d_out, rp = _rms_bwd(out, g_post, dh1)
        return dh1, d_out, dkv * xhat, db * xhat, rp

    def mid_bwd_fn(h1t, dh2t, dkv, db, out, g_kv, g_b, g_post):
        return mid_bwd(h1t, dh2t, dkv, db, g_kv, g_b, g_post, out)

    dh1, d_out_a, dg_kv, dg_b_pre, dg_a_post = _rowwise(
        mid_bwd_fn, [h1, dh2, dhn_kv, dhn_b, out_a], [w["kv_norm"], w["b_pre_norm"], w["a_post_norm"]],
        [(D_MODEL, F32), (D_MODEL, BF16)], [D_MODEL] * 3, tm=tm, name="mid_bwd")

    dy_a = _matmul(d_out_a, w_a_out, tb=True, out_dtype=BF16, tm=mm, tn=A_WIDTH, name="a_out_dx")
    gw_a_out = _matmul(y_a, d_out_a, ta=True, out_dtype=BF16, tm=mm, tn=D_MODEL, tk=2048, name="a_out_dw")

    def gate_a_bwd(dy, o, z):
        do = dy * _silu(z)
        prod = do * o
        lane = lax.broadcasted_iota(jnp.int32, (prod.shape[0], A_HEADS), 1)
        dsum = jnp.zeros((prod.shape[0], A_HEADS), F32)
        for h in range(A_HEADS):
            col = jnp.sum(prod[:, h * A_HEAD_DIM:(h + 1) * A_HEAD_DIM], axis=1, keepdims=True)
            dsum = jnp.where(lane == h, col, dsum)
        return do, dy * (o * _silu_grad(z)), dsum

    do_a, dz_a, dsum_a = _rowwise(gate_a_bwd, [dy_a, o_a2, z_a], [],
                                  [(A_WIDTH, BF16), (A_WIDTH, BF16), (A_HEADS, F32)], tm=tm, name="a_gate_bwd")
    dhn_a = _matmul(dz_a, w_a_in, b_cols=(A_QKV, A_WIDTH), tb=True, out_dtype=F32, tm=mm, tn=D_MODEL,
                    name="a_gate_dx")
    gw_a_in, dhn_groups = None, []
    for g, d in enumerate(A_DILATIONS):
        s_n, ln = bsz * d, t // d
        dqkv = _attn_a_bwd(qk_g[g], v_g[g], to_group(do_a, d).reshape(s_n, ln, A_WIDTH), cols_to_rows(lse_a, d),
                           cols_to_rows(dsum_a, d), [a.reshape(s_n, ln, LANES) for a in tabs_g[g]],
                           f"attn_a_bwd_g{g}").reshape(n, 3 * A_WIDTH)
        w_cols = (g * 3 * A_WIDTH, 3 * A_WIDTH)
        if d == 1:
            dhn_a = _matmul(dqkv, w_a_in, b_cols=w_cols, tb=True, out_dtype=F32, tm=mm, tn=D_MODEL,
                            name=f"a_qkv_dx_g{g}", epilogue=_add_epilogue,
                            extras=[(dhn_a, (mm, D_MODEL), lambda j, i, kk: (i, j))])
        else:
            dhn_groups.append(from_group(_matmul(dqkv, w_a_in, b_cols=w_cols, tb=True, out_dtype=BF16, tm=mm,
                                                 tn=D_MODEL, name=f"a_qkv_dx_g{g}"), d))
        gw_a_in = _matmul(hn_g[g], dqkv, ta=True, out_dtype=BF16, tm=mm, tn=1024, tk=2048, name=f"a_qkv_dw_g{g}",
                          out_into=(gw_a_in, g * 3 * A_WIDTH, A_QKV + A_WIDTH))
    gw_a_in = _matmul(hn_a, dz_a, ta=True, out_dtype=BF16, tm=mm, tn=1024, tk=2048, name="a_gate_dw",
                      out_into=(gw_a_in, A_QKV, A_QKV + A_WIDTH))

    def first_bwd(xt, dhn, dhn_1, dhn_2, dh1t, g):
        dx, dg = _rms_bwd(xt, g, dhn + dhn_1 + dhn_2)
        return dh1t + dx, dg

    grad_x, dg_a_pre = _rowwise(first_bwd, [x2, dhn_a, *dhn_groups, dh1], [w["a_pre_norm"]], [(D_MODEL, F32)],
                                [D_MODEL], tm=tm, name="a_pre_norm_bwd")

    gw_down = jnp.concatenate([gw_down_p[:, :B_KV_LORA], gw_down_p[:, B_KV_LORA + B_NOPE:B_KV_LORA + B_QK_DIM]], axis=1)
    gw_up = jnp.concatenate([gw_upk.reshape(B_KV_LORA, B_HEADS, LANES)[:, :, :B_NOPE],
                             gw_upv.reshape(B_KV_LORA, B_HEADS, B_VDIM)], axis=2).reshape(B_KV_LORA, -1)
    gw_q_up = gw_q_p.reshape(B_Q_LORA, B_HEADS, LANES)[:, :, :B_QK_DIM].reshape(B_Q_LORA, -1)
    grads = {"a_w_in": gw_a_in, "a_w_out": gw_a_out, "kv_w_down": gw_down, "kv_w_up": gw_up,
             "b_w_in": gw_b_in, "b_w_q_up": gw_q_up, "b_w_out": gw_b_out}
    gains = {"a_pre_norm": dg_a_pre, "a_post_norm": dg_a_post, "kv_norm": dg_kv, "kv_latent_norm": dg_latent,
             "b_pre_norm": dg_b_pre, "b_q_norm": dg_q_norm, "b_post_norm": dg_b_post}
    gains = {k: jnp.sum(a, axis=0) for k, a in gains.items()}
    return jnp.sum(loss_acc), grad_x.reshape(bsz, t, D_MODEL), grads, gains


WEIGHT_ORDER = ("a_pre_norm", "a_w_in", "a_w_out", "a_post_norm", "kv_norm", "kv_w_down", "kv_latent_norm",
                "kv_w_up", "b_pre_norm", "b_w_in", "b_q_norm", "b_w_q_up", "b_w_out", "b_post_norm")
MATRICES = (("a_w_in", 1024, 10240, 1), ("a_w_out", 1024, 1024, 0), ("kv_w_down", 1024, 288, 0),
            ("kv_w_up", 256, 2048, 1), ("b_w_in", 1024, 1408, 1), ("b_w_q_up", 384, 1536, 1),
            ("b_w_out", 1024, 1024, 0))
SHARDED_GAINS = ("a_pre_norm", "a_post_norm")
GAIN_WIDTHS = (("a_pre_norm", 1024), ("a_post_norm", 1024), ("kv_norm", 1024), ("kv_latent_norm", 256),
               ("b_pre_norm", 1024), ("b_q_norm", 384), ("b_post_norm", 1024))
GAIN_ROWS = 48


def _shard_rows(rows, cols):
    return rows * cols // (N_DEV * LANES)


def _whole_from_blocks(blocks, rows, cols, axis):
    if axis == 1:
        return blocks.reshape(N_DEV, rows, cols // N_DEV).transpose(1, 0, 2).reshape(rows, cols)
    return blocks.reshape(rows, cols)


def _blocks_from_whole(whole, rows, cols, axis):
    if axis == 1:
        whole = whole.reshape(rows, N_DEV, cols // N_DEV).transpose(1, 0, 2)
    return whole.reshape(N_DEV, -1, LANES)


def kernel(x, positions, a_pre_norm, a_w_in, a_w_out, a_post_norm, kv_norm, kv_w_down, kv_latent_norm, kv_w_up, b_pre_norm, b_w_in, b_q_norm, b_w_q_up, b_w_out, b_post_norm, loss_target, m_a_pre_norm, m_a_w_in, m_a_w_out, m_a_post_norm, m_kv_norm, m_kv_w_down, m_kv_latent_norm, m_kv_w_up, m_b_pre_norm, m_b_w_in, m_b_q_norm, m_b_w_q_up, m_b_w_out, m_b_post_norm, v_a_pre_norm, v_a_w_in, v_a_w_out, v_a_post_norm, v_kv_norm, v_kv_w_down, v_kv_latent_norm, v_kv_w_up, v_b_pre_norm, v_b_w_in, v_b_q_norm, v_b_w_q_up, v_b_w_out, v_b_post_norm):
    weights = dict(a_pre_norm=a_pre_norm, a_w_in=a_w_in, a_w_out=a_w_out, a_post_norm=a_post_norm, kv_norm=kv_norm,
                   kv_w_down=kv_w_down, kv_latent_norm=kv_latent_norm, kv_w_up=kv_w_up, b_pre_norm=b_pre_norm,
                   b_w_in=b_w_in, b_q_norm=b_q_norm, b_w_q_up=b_w_q_up, b_w_out=b_w_out, b_post_norm=b_post_norm)
    m_in = dict(a_pre_norm=m_a_pre_norm, a_w_in=m_a_w_in, a_w_out=m_a_w_out, a_post_norm=m_a_post_norm,
                kv_norm=m_kv_norm, kv_w_down=m_kv_w_down, kv_latent_norm=m_kv_latent_norm, kv_w_up=m_kv_w_up,
                b_pre_norm=m_b_pre_norm, b_w_in=m_b_w_in, b_q_norm=m_b_q_norm, b_w_q_up=m_b_w_q_up,
                b_w_out=m_b_w_out, b_post_norm=m_b_post_norm)
    v_in = dict(a_pre_norm=v_a_pre_norm, a_w_in=v_a_w_in, a_w_out=v_a_w_out, a_post_norm=v_a_post_norm,
                kv_norm=v_kv_norm, kv_w_down=v_kv_w_down, kv_latent_norm=v_kv_latent_norm, kv_w_up=v_kv_w_up,
                b_pre_norm=v_b_pre_norm, b_w_in=v_b_w_in, b_q_norm=v_b_q_norm, b_w_q_up=v_b_w_q_up,
                b_w_out=v_b_w_out, b_post_norm=v_b_post_norm)
    me = 4 * lax.axis_index("x") + 2 * lax.axis_index("y") + lax.axis_index("c")

    wide = MATRICES[0][0]
    flat = jnp.concatenate([weights[name].astype(BF16).reshape(-1, LANES) for name, _, _, _ in MATRICES[1:]], axis=0)
    gathered, w_wide = _all_gather_weights(flat, weights[wide][0].astype(BF16), "gather_weights")
    whole = {wide: w_wide}
    off = 0
    for name, rows, cols, axis in MATRICES[1:]:
        nr = _shard_rows(rows, cols)
        whole[name] = _whole_from_blocks(gathered[:, off:off + nr], rows, cols, axis)
        off += nr
    gain_shard = jnp.concatenate([weights[name].reshape(1, LANES) for name in SHARDED_GAINS]
                                 + [jnp.zeros((8 - len(SHARDED_GAINS), LANES), F32)], axis=0)
    gain_blocks = _all_gather(gain_shard, "gather_gains")
    for i, name in enumerate(SHARDED_GAINS):
        whole[name] = gain_blocks[:, i, :].reshape(1, D_MODEL)
    for name in ("kv_norm", "kv_latent_norm", "b_pre_norm", "b_q_norm", "b_post_norm"):
        whole[name] = weights[name].reshape(1, -1)

    loss_part, grad_x, grads, gains = _local_step(x, positions, loss_target, whole)

    blocks = jnp.concatenate([_blocks_from_whole(grads[name], rows, cols, axis).astype(BF16)
                              for name, rows, cols, axis in MATRICES], axis=1)
    blocks = blocks.reshape(N_CHIPS, 2, blocks.shape[1], LANES)
    got = _exchange_sibling(blocks, "scatter_grads_core")
    core = lax.axis_index("c").astype(jnp.int32).reshape(1)
    landed = _exchange_chips(_add_pairs(blocks, got, core, tr=2512, name="add_core_grads"), "scatter_grads_chip")
    summed = _sum_slots(landed, tr=2512, name="sum_grads")
    grad_out = {}
    off = 0
    for name, rows, cols, axis in MATRICES:
        nr = _shard_rows(rows, cols)
        grad_out[name] = summed[off:off + nr].reshape(weights[name].shape)
        off += nr

    vec = jnp.concatenate([gains[name] for name, _ in GAIN_WIDTHS] + [jnp.full((LANES,), loss_part, F32)])
    vec = jnp.pad(vec, (0, GAIN_ROWS * LANES - vec.shape[0])).reshape(GAIN_ROWS, LANES)
    total = _sum_slots(_all_gather(vec, "gather_gain_grads"), tr=GAIN_ROWS, name="sum_gain_grads").reshape(-1)
    off = 0
    for name, width in GAIN_WIDTHS:
        g = total[off:off + width]
        if name in SHARDED_GAINS:
            g = lax.dynamic_slice(g, (me * LANES,), (LANES,))
        grad_out[name] = g.reshape(weights[name].shape)
        off += width
    loss = total[off]

    deltas, new_m, new_v = {}, {}, {}
    big = "a_w_in"
    deltas[big], new_m[big], new_v[big] = _adamw(weights[big], grad_out[big], m_in[big], v_in[big], "adamw_" + big)
    small = [name for name in WEIGHT_ORDER if name != big]
    res = _adamw_small(*[[d[name] for name in small] for d in (weights, grad_out, m_in, v_in)], "adamw_small")
    for out, vals in zip((deltas, new_m, new_v), res):
        out.update(zip(small, vals))
    return (loss, grad_x, *[grad_out[k] for k in WEIGHT_ORDER], *[deltas[k] for k in WEIGHT_ORDER],
            *[new_m[k] for k in WEIGHT_ORDER], *[new_v[k] for k in WEIGHT_ORDER])
```

```python
import jax
import jax.numpy as jnp
from jax import lax
from jax.experimental import pallas as pl
from jax.experimental.pallas import tpu as pltpu

F32 = jnp.float32
BF16 = jnp.bfloat16

N_DEV = 8
D_MODEL = 1024
NORM_EPS = 1e-6
A_GROUPS = 3
A_DILATIONS = (1, 4, 16)
A_HEADS = 8
A_HEAD_DIM = 128
A_WIDTH = 1024
A_ROT_DIM = 32
A_ROPE_THETA = 500000.0
A_QKV = A_GROUPS * 3 * A_WIDTH
B_HEADS = 16
B_NOPE = 64
B_ROPE = 32
B_QK_DIM = 96
B_VDIM = 64
B_Q_LORA = 384
B_KV_LORA = 256
B_ROPE_THETA = 10000.0
B_KPAD = B_HEADS * 128
ADAM_LR = 0.001
ADAM_B1 = 0.9
ADAM_B2 = 0.999
ADAM_EPS = 1e-08
ADAM_WD = 0.01
ADAM_STEP = 10

LANES = 128
BAND = 128
EPILOGUE_ROWS = 128
NEG = -1e30
VMEM_LIMIT = 56 * 1024 * 1024
MESH = pl.DeviceIdType.MESH


def _cparams(sem):
    return pltpu.CompilerParams(dimension_semantics=sem, vmem_limit_bytes=VMEM_LIMIT)


def _rowwise(fn, rows, bcast, outs, accs=(), *, tm, name):
    n = rows[0].shape[0]
    nr, nb, no = len(rows), len(bcast), len(outs)

    def body(*refs):
        res = fn(*[r[...] for r in refs[:nr + nb]])
        out_refs = refs[nr + nb:nr + nb + no]
        acc_refs = refs[nr + nb + no:]
        for r, v in zip(out_refs, res[:no]):
            r[...] = v.astype(r.dtype)
        if acc_refs:
            @pl.when(pl.program_id(0) == 0)
            def _():
                for r in acc_refs:
                    r[...] = jnp.zeros_like(r)
            for r, v in zip(acc_refs, res[no:]):
                r[...] += v.reshape(tm // 8, 8, v.shape[-1]).sum(axis=0)

    in_specs = [pl.BlockSpec((tm, a.shape[1]), lambda i: (i, 0)) for a in rows]
    in_specs += [pl.BlockSpec(a.shape, lambda i: (0, 0)) for a in bcast]
    out_specs = [pl.BlockSpec((tm, c), lambda i: (i, 0)) for c, _ in outs]
    out_specs += [pl.BlockSpec((8, c), lambda i: (0, 0)) for c in accs]
    out_shape = [jax.ShapeDtypeStruct((n, c), dt) for c, dt in outs]
    out_shape += [jax.ShapeDtypeStruct((8, c), F32) for c in accs]
    return pl.pallas_call(
        body, name=name, grid=(n // tm,), in_specs=in_specs, out_specs=out_specs, out_shape=out_shape,
        compiler_params=_cparams(("arbitrary",)))(*rows, *bcast)


def _matmul(a, b, *, out_dtype, tm, tn, tk=None, name, epilogue=None, extras=(), ta=False, tb=False,
            epilogue_rows=None, b_cols=None, out_into=None):
    epilogue_rows = epilogue_rows or tm
    k, m = a.shape[::-1] if not ta else a.shape
    col0, width = b_cols or (0, b.shape[1])
    n = b.shape[0] if tb else width
    tk = tk or k
    nk = k // tk
    b_off = col0 // (tk if tb else tn)
    assert col0 % (tk if tb else tn) == 0 and (k == width if tb else True)
    ne = len(extras)
    dot = _dot_tn if ta else (_dot_nt if tb else _dot)
    assert epilogue is None or (nk == 1 and not ta)

    def body(*refs):
        a_ref, b_ref = refs[:2]
        ex = refs[2:2 + ne]
        o_ref = refs[2 + ne + (1 if out_into is not None and out_into[0] is not None else 0)]
        if epilogue is not None:
            b_tile = b_ref[...].astype(BF16)
            for r0 in range(0, tm, epilogue_rows):
                rows = slice(r0, r0 + epilogue_rows)
                epilogue(dot(a_ref[rows, :].astype(BF16), b_tile), o_ref, rows, *ex)
            return
        part = dot(a_ref[...].astype(BF16), b_ref[...].astype(BF16))
        if nk == 1:
            o_ref[...] = part.astype(o_ref.dtype)
        else:
            acc_ref = refs[-1]
            kk = pl.program_id(2)

            @pl.when(kk == 0)
            def _():
                acc_ref[...] = part

            @pl.when(kk > 0)
            def _():
                acc_ref[...] += part

            @pl.when(kk == nk - 1)
            def _():
                o_ref[...] = acc_ref[...].astype(o_ref.dtype)

    a_spec = pl.BlockSpec((tk, tm), lambda j, i, kk: (kk, i)) if ta else pl.BlockSpec((tm, tk), lambda j, i, kk: (i, kk))
    b_spec = (pl.BlockSpec((tn, tk), lambda j, i, kk: (j, kk + b_off)) if tb
              else pl.BlockSpec((tk, tn), lambda j, i, kk: (kk, j + b_off)))
    in_specs = [a_spec, b_spec] + [pl.BlockSpec(bs, im) for _, bs, im in extras]
    operands = [a, b] + [e[0] for e in extras]
    aliases = {}
    if out_into is not None:
        prev, out0, n_total = out_into
        o_off = out0 // tn
        assert out0 % tn == 0
        if prev is not None:
            in_specs.append(ANY)
            operands.append(prev)
            aliases = {len(operands) - 1: 0}
    else:
        o_off, n_total = 0, n
    return pl.pallas_call(
        body, name=name, grid=(n // tn, m // tm, nk), in_specs=in_specs,
        out_specs=pl.BlockSpec((tm, tn), lambda j, i, kk: (i, j + o_off)),
        out_shape=jax.ShapeDtypeStruct((m, n_total), out_dtype), input_output_aliases=aliases,
        scratch_shapes=[pltpu.VMEM((tm, tn), F32)] if nk > 1 else [],
        compiler_params=_cparams(("parallel", "parallel", "arbitrary")))(*operands)


def _add_epilogue(acc, o_ref, rows, prev_ref):
    o_ref[rows, :] = (acc + prev_ref[rows, :]).astype(o_ref.dtype)


def _rope(x, c, sp, sm):
    return x * c + pltpu.roll(x, 16, 1) * sp + pltpu.roll(x, LANES - 16, 1) * sm


def _rope_t(dy, c, sp, sm):
    return dy * c + pltpu.roll(dy * sp, LANES - 16, 1) + pltpu.roll(dy * sm, 16, 1)


def _rope_tables(positions, theta, rot_dim, lane0, name):
    n = positions.shape[0]
    half = rot_dim // 2
    inv_freq = 1.0 / (theta ** (jnp.arange(half, dtype=F32) * (2.0 / rot_dim)))
    freq = jnp.concatenate([jnp.zeros((lane0,), F32), inv_freq, inv_freq,
                            jnp.zeros((LANES - lane0 - rot_dim,), F32)]).reshape(1, LANES)

    def fn(p, f):
        ang = p * f
        cos, sin = jnp.cos(ang), jnp.sin(ang)
        lane = lax.broadcasted_iota(jnp.int32, ang.shape, 1) - lane0
        first, second = (lane >= 0) & (lane < half), (lane >= half) & (lane < rot_dim)
        return jnp.where(first | second, cos, 1.0), jnp.where(second, sin, 0.0), jnp.where(first, -sin, 0.0)

    return _rowwise(fn, [positions.astype(F32).reshape(n, 1)], [freq], [(LANES, F32)] * 3, tm=512, name=name)


def _value_heads_t(c_kv, w_vt, bsz, t):
    n, k = c_kv.shape
    tm, tn = 512, 1024

    def body(a_ref, b_ref, o_ref):
        acc = _dot(a_ref[...], b_ref[...])
        lane = lax.broadcasted_iota(jnp.int32, (1, tn), 1)
        acc = acc + jnp.where(lax.rem(lane, LANES) == B_VDIM, 1.0, 0.0)
        o_ref[...] = acc.T.astype(BF16)

    per_seq = t // tm
    return pl.pallas_call(
        body, name="kv_up_v_t", grid=(w_vt.shape[1] // tn, n // tm),
        in_specs=[pl.BlockSpec((tm, k), lambda j, i: (i, 0)), pl.BlockSpec((k, tn), lambda j, i: (0, j))],
        out_specs=pl.BlockSpec((None, tn, tm), lambda j, i: (i // per_seq, j, lax.rem(i, per_seq))),
        out_shape=jax.ShapeDtypeStruct((bsz, w_vt.shape[1], t), BF16),
        compiler_params=_cparams(("parallel", "parallel")))(c_kv, w_vt)


def _rms(x, g):
    xf = x.astype(F32)
    return xf * lax.rsqrt(jnp.mean(xf * xf, axis=-1, keepdims=True) + NORM_EPS) * g


def _rms_bwd(x, g, dy):
    xf = x.astype(F32)
    rstd = lax.rsqrt(jnp.mean(xf * xf, axis=-1, keepdims=True) + NORM_EPS)
    xhat = xf * rstd
    dxhat = dy * g
    dx = rstd * (dxhat - xhat * jnp.mean(dxhat * xhat, axis=-1, keepdims=True))
    return dx, dy * xhat


def _silu(z):
    return z * jax.nn.sigmoid(z)


def _silu_grad(z):
    s = jax.nn.sigmoid(z)
    return s * (1.0 + z * (1.0 - s))


def _dot_nt(a, b):
    return lax.dot_general(a, b, (((1,), (1,)), ((), ())), preferred_element_type=F32)


def _dot_tn(a, b):
    return lax.dot_general(a, b, (((0,), (0,)), ((), ())), preferred_element_type=F32)


def _dot(a, b):
    return jnp.dot(a, b, preferred_element_type=F32)


A_SCALE = A_HEAD_DIM ** -0.5
LOG2E = 1.4426950408889634
A_C2 = A_SCALE * LOG2E
A_HEAD_GROUP = 4
A_FWD_HEAD_GROUP = 8
A_BLOCKS_PER_STEP = 4
A_SEQS_PER_STEP = 4


def _a_steps(nb):
    return (min(nb, A_BLOCKS_PER_STEP), 1) if nb > 1 else (1, A_SEQS_PER_STEP)


def _attn_a_fwd(qk, v, name):
    s_n, ln, _ = qk.shape
    nb = ln // BAND
    lb, sb = _a_steps(nb)
    single = nb == 1
    rows = lb * BAND
    n_keys = BAND if single else 2 * BAND

    def body(q_ref, kc_ref, kp_ref, vc_ref, vp_ref, o_ref, lse_ref):
        kpos = lax.broadcasted_iota(jnp.int32, (n_keys, BAND), 0)
        qpos = lax.broadcasted_iota(jnp.int32, (n_keys, BAND), 1) + (n_keys - BAND)
        band = (kpos <= qpos) & (kpos >= qpos - BAND)
        at_start = band & (kpos >= jnp.where(pl.program_id(1) > 0, 0, n_keys - BAND))
        for s in range(sb):
            for u in range(lb):
                r = slice(u * BAND, (u + 1) * BAND)
                mask = at_start if u == 0 else band

                def keys(prev_ref, cur_ref, hs):
                    cur = cur_ref[s, r, hs]
                    if single:
                        return cur
                    prev = prev_ref[s, :, hs] if u == 0 else cur_ref[s, (u - 1) * BAND:u * BAND, hs]
                    return jnp.concatenate([prev, cur], axis=0)

                stats = []
                for h0 in range(0, A_HEADS, A_FWD_HEAD_GROUP):
                    hss = [slice(h * A_HEAD_DIM, (h + 1) * A_HEAD_DIM) for h in range(h0, h0 + A_FWD_HEAD_GROUP)]
                    sts = [_dot_nt(keys(kp_ref, kc_ref, hs), q_ref[s, r, hs]) for hs in hss]
                    ps, ls = [], []
                    for st in sts:
                        st = jnp.where(mask, st * A_C2, NEG)
                        m = jnp.max(st, axis=0, keepdims=True)
                        p = jnp.exp2(st - m)
                        l_row = jnp.sum(p, axis=0, keepdims=True)
                        ps.append(p.astype(BF16))
                        ls.append(l_row)
                        stats.append(m + jnp.log2(l_row))
                    ots = [_dot_tn(keys(vp_ref, vc_ref, hs), p) for hs, p in zip(hss, ps)]
                    for hs, o_t, l_row in zip(hss, ots, ls):
                        o_ref[s, r, hs] = (o_t / l_row).T.astype(BF16)
                lse_ref[s, :, r] = jnp.concatenate(stats, axis=0)

    def cur(c):
        return pl.BlockSpec((sb, rows, A_WIDTH), lambda s, l: (s, l, c))

    def prev(c):
        return pl.BlockSpec((sb, BAND, A_WIDTH), lambda s, l: (s, jnp.maximum(l * lb - 1, 0), c))

    return pl.pallas_call(
        body, name=name, grid=(s_n // sb, nb // lb),
        in_specs=[cur(0), cur(1), prev(1), cur(0), prev(0)],
        out_specs=[cur(0), pl.BlockSpec((sb, A_HEADS, rows), lambda s, l: (s, 0, l))],
        out_shape=[jax.ShapeDtypeStruct((s_n, ln, A_WIDTH), BF16), jax.ShapeDtypeStruct((s_n, A_HEADS, ln), F32)],
        compiler_params=_cparams(("parallel", "arbitrary")))(qk, qk, qk, v, v)


def _attn_a_bwd(qk, v, do, lse2, dsum, tabs, name):
    s_n, ln, _ = qk.shape
    nb = ln // BAND
    lb, sb = _a_steps(nb)
    single = nb == 1
    rows = lb * BAND
    n_steps = nb // lb

    def body(q_ref, qn_ref, kc_ref, kp_ref, vc_ref, vp_ref, do_ref, don_ref, lse_ref, lsen_ref, ds_ref, dsn_ref,
             c_ref, sp_ref, sm_ref, out_ref):
        l_idx = pl.program_id(1)
        kpos = lax.broadcasted_iota(jnp.int32, (2 * BAND, BAND), 0)
        qpos = lax.broadcasted_iota(jnp.int32, (2 * BAND, BAND), 1) + BAND
        band_q = (kpos <= qpos) & (kpos >= qpos - BAND)
        start_q = band_q & (kpos >= jnp.where(l_idx > 0, 0, BAND))
        kpos2 = lax.broadcasted_iota(jnp.int32, (BAND, 2 * BAND), 0)
        qpos2 = lax.broadcasted_iota(jnp.int32, (BAND, 2 * BAND), 1)
        band_k = (kpos2 <= qpos2) & (kpos2 >= qpos2 - BAND)
        end_k = band_k & (qpos2 < jnp.where(l_idx < n_steps - 1, 2 * BAND, BAND))
        causal = kpos[:BAND] <= qpos[:BAND] - BAND
        for s in range(sb):
            for u in range(lb):
                r = slice(u * BAND, (u + 1) * BAND)
                c, sp, sm = c_ref[s, r, :], sp_ref[s, r, :], sm_ref[s, r, :]
                lse_q, ds_q = lse_ref[s, :, r], ds_ref[s, :, r]

                def store(hl, hss, dqs, dks, dvs):
                    for i, h in enumerate(hl):
                        out_ref[s, r, hss[i]] = _rope_t(dqs[i], c, sp, sm).astype(BF16)
                        out_ref[s, r, A_WIDTH + h * A_HEAD_DIM:A_WIDTH + (h + 1) * A_HEAD_DIM] = \
                            _rope_t(dks[i], c, sp, sm).astype(BF16)
                        out_ref[s, r, 2 * A_WIDTH + h * A_HEAD_DIM:2 * A_WIDTH + (h + 1) * A_HEAD_DIM] = \
                            dvs[i].astype(BF16)

                if single:
                    for h0 in range(0, A_HEADS, A_HEAD_GROUP):
                        hl = list(range(h0, h0 + A_HEAD_GROUP))
                        hss = [slice(h * A_HEAD_DIM, (h + 1) * A_HEAD_DIM) for h in hl]
                        sts = [_dot_nt(kc_ref[s, r, hs], q_ref[s, r, hs]) for hs in hss]
                        dpts = [_dot_nt(vc_ref[s, r, hs], do_ref[s, r, hs]) for hs in hss]
                        ps, dsts = [], []
                        for i, h in enumerate(hl):
                            p = jnp.exp2(jnp.where(causal, sts[i] * A_C2, NEG) - lse_q[h:h + 1])
                            dsts.append((p * (dpts[i] - ds_q[h:h + 1]) * A_SCALE).astype(BF16))
                            ps.append(p.astype(BF16))
                        store(hl, hss, [_dot_tn(dst, kc_ref[s, r, hs]) for dst, hs in zip(dsts, hss)],
                              [_dot(dst, q_ref[s, r, hs]) for dst, hs in zip(dsts, hss)],
                              [_dot(p, do_ref[s, r, hs]) for p, hs in zip(ps, hss)])
                    continue

                r_prev = slice((u - 1) * BAND, u * BAND)
                r_next = slice((u + 1) * BAND, (u + 2) * BAND)

                def before(prev_ref, cur_ref, hs):
                    return prev_ref[s, :, hs] if u == 0 else cur_ref[s, r_prev, hs]

                def after(next_ref, cur_ref, hs):
                    return next_ref[s, :, hs] if u == lb - 1 else cur_ref[s, r_next, hs]

                mask_q = start_q if u == 0 else band_q
                mask_k = end_k if u == lb - 1 else band_k
                lse_n = lsen_ref[s] if u == lb - 1 else lse_ref[s, :, r_next]
                ds_n = dsn_ref[s] if u == lb - 1 else ds_ref[s, :, r_next]
                lse_k = jnp.concatenate([lse_q, lse_n], axis=1)
                ds_k = jnp.concatenate([ds_q, ds_n], axis=1)
                for h0 in range(0, A_HEADS, A_HEAD_GROUP):
                    hl = list(range(h0, h0 + A_HEAD_GROUP))
                    hss = [slice(h * A_HEAD_DIM, (h + 1) * A_HEAD_DIM) for h in hl]
                    k2s = [jnp.concatenate([before(kp_ref, kc_ref, hs), kc_ref[s, r, hs]], axis=0) for hs in hss]
                    v2s = [jnp.concatenate([before(vp_ref, vc_ref, hs), vc_ref[s, r, hs]], axis=0) for hs in hss]
                    q2s = [jnp.concatenate([q_ref[s, r, hs], after(qn_ref, q_ref, hs)], axis=0) for hs in hss]
                    do2s = [jnp.concatenate([do_ref[s, r, hs], after(don_ref, do_ref, hs)], axis=0) for hs in hss]
                    sts = [_dot_nt(k2, q_ref[s, r, hs]) for k2, hs in zip(k2s, hss)]
                    dpts = [_dot_nt(v2, do_ref[s, r, hs]) for v2, hs in zip(v2s, hss)]
                    st2s = [_dot_nt(kc_ref[s, r, hs], q2) for q2, hs in zip(q2s, hss)]
                    dpt2s = [_dot_nt(vc_ref[s, r, hs], do2) for do2, hs in zip(do2s, hss)]
                    dsts, dst2s, p2s = [], [], []
                    for i, h in enumerate(hl):
                        p = jnp.exp2(jnp.where(mask_q, sts[i] * A_C2, NEG) - lse_q[h:h + 1])
                        dsts.append((p * (dpts[i] - ds_q[h:h + 1]) * A_SCALE).astype(BF16))
                        p2 = jnp.exp2(jnp.where(mask_k, st2s[i] * A_C2, NEG) - lse_k[h:h + 1])
                        dst2s.append((p2 * (dpt2s[i] - ds_k[h:h + 1]) * A_SCALE).astype(BF16))
                        p2s.append(p2.astype(BF16))
                    store(hl, hss, [_dot_tn(dsts[i], k2s[i]) for i in range(A_HEAD_GROUP)],
                          [_dot(dst2s[i], q2s[i]) for i in range(A_HEAD_GROUP)],
                          [_dot(p2s[i], do2s[i]) for i in range(A_HEAD_GROUP)])

    def cur(c, width=A_WIDTH):
        return pl.BlockSpec((sb, rows, width), lambda s, l: (s, l, c))

    def prev(c):
        return pl.BlockSpec((sb, BAND, A_WIDTH), lambda s, l: (s, jnp.maximum(l * lb - 1, 0), c))

    def nxt(c):
        return pl.BlockSpec((sb, BAND, A_WIDTH), lambda s, l: (s, jnp.minimum((l + 1) * lb, nb - 1), c))

    stat = pl.BlockSpec((sb, A_HEADS, rows), lambda s, l: (s, 0, l))
    stat_next = pl.BlockSpec((sb, A_HEADS, BAND), lambda s, l: (s, 0, jnp.minimum((l + 1) * lb, nb - 1)))
    tspec = cur(0, LANES)
    in_specs = [cur(0), nxt(0), cur(1), prev(1), cur(0), prev(0), cur(0), nxt(0),
                stat, stat_next, stat, stat_next, tspec, tspec, tspec]
    return pl.pallas_call(
        body, name=name, grid=(s_n // sb, n_steps), in_specs=in_specs,
        out_specs=cur(0, 3 * A_WIDTH),
        out_shape=jax.ShapeDtypeStruct((s_n, ln, 3 * A_WIDTH), BF16),
        compiler_params=_cparams(("parallel", "arbitrary")))(
            qk, qk, qk, qk, v, v, do, do, lse2, lse2, dsum, dsum, *tabs)


B_TQ = 256
B_FWD_HEADS = 4
B_BWD_PAIRS = 2
B_SCALE = B_QK_DIM ** -0.5
B_C2 = B_SCALE * LOG2E


def _key_le_query(kb, qb, tk, tq):
    kpos = kb * tk + lax.broadcasted_iota(jnp.int32, (tk, tq), 0)
    qpos = qb * tq + lax.broadcasted_iota(jnp.int32, (tk, tq), 1)
    return kpos <= qpos


def _attn_b_fwd(q, kpad, vt1):
    bsz, t, _ = q.shape
    tq = tk = B_TQ
    nq = t // tq
    nh = B_FWD_HEADS

    def body(q_ref, k_ref, vt_ref, o_ref, lse_ref):
        qblk = pl.program_id(2)
        qs = [q_ref[:, hh * LANES:(hh + 1) * LANES] for hh in range(nh)]

        def scores(kb):
            start = pl.multiple_of(kb * tk, tk)
            return [_dot_nt(k_ref[pl.ds(start, tk), hh * LANES:(hh + 1) * LANES], qs[hh]) for hh in range(nh)]

        def pv(kb, ps):
            start = pl.multiple_of(kb * tk, tk)
            return [_dot(vt_ref[hh * LANES:(hh + 1) * LANES, pl.ds(start, tk)], ps[hh]) for hh in range(nh)]

        def softmax(ss, ms, accs, kb, masked):
            out_m, out_acc, out_p = [], [], []
            for hh in range(nh):
                s = ss[hh] * B_C2
                if masked:
                    s = jnp.where(_key_le_query(kb, qblk, tk, tq), s, NEG)
                m_new = jnp.maximum(ms[hh], jnp.max(s, axis=0, keepdims=True))
                out_acc.append(jnp.exp2(ms[hh] - m_new) * accs[hh])
                out_p.append(jnp.exp2(s - m_new).astype(BF16))
                out_m.append(m_new)
            return out_m, out_acc, out_p

        def step(kb, carry):
            ss, ps, ms, accs = carry
            pvs = pv(jnp.maximum(kb - 1, 0), ps)
            ss_next = scores(kb + 1)
            accs = [accs[hh] + pvs[hh] for hh in range(nh)]
            ms, accs, ps = softmax(ss, ms, accs, kb, False)
            return (ss_next, ps, ms, accs)

        init = (scores(0), [jnp.zeros((tk, tq), BF16)] * nh, [jnp.full((1, tq), NEG, F32)] * nh,
                [jnp.zeros((LANES, tq), F32)] * nh)
        ss, ps, ms, accs = lax.fori_loop(0, qblk, step, init)
        pvs = pv(jnp.maximum(qblk - 1, 0), ps)
        accs = [accs[hh] + pvs[hh] for hh in range(nh)]
        ms, accs, ps = softmax(ss, ms, accs, qblk, True)
        pvs = pv(qblk, ps)
        accs = [accs[hh] + pvs[hh] for hh in range(nh)]
        ls = [accs[hh][B_VDIM:B_VDIM + 1] for hh in range(nh)]
        o_t = jnp.concatenate([accs[hh][:B_VDIM] / ls[hh] for hh in range(nh)], axis=0)
        o_ref[...] = o_t.T
        for pair in range(nh // 2):
            lse_ref[pair] = jnp.concatenate([ms[2 * pair + hh] + jnp.log2(ls[2 * pair + hh]) for hh in range(2)]
                                            + [jnp.zeros((6, tq), F32)], axis=0)

    return pl.pallas_call(
        body, name="attn_b_fwd", grid=(bsz, B_HEADS // nh, nq),
        in_specs=[pl.BlockSpec((None, tq, nh * LANES), lambda b, j, i: (b, i, j)),
                  pl.BlockSpec((None, t, nh * LANES), lambda b, j, i: (b, 0, j)),
                  pl.BlockSpec((None, nh * LANES, t), lambda b, j, i: (b, j, 0))],
        out_specs=[pl.BlockSpec((None, tq, nh * B_VDIM), lambda b, j, i: (b, i, j)),
                   pl.BlockSpec((None, nh // 2, 8, tq), lambda b, j, i: (b, j, 0, i))],
        out_shape=[jax.ShapeDtypeStruct((bsz, t, B_HEADS * B_VDIM), F32),
                   jax.ShapeDtypeStruct((bsz, B_HEADS // 2, 8, t), F32)],
        compiler_params=_cparams(("parallel", "parallel", "arbitrary")))(q, kpad, vt1)


def _attn_b_bwd(q, kpad, v, do, o, lse2, tabs):
    bsz, t, _ = q.shape
    tq = tk = B_TQ
    nq = t // tq
    n_pairs = B_BWD_PAIRS
    n_heads = 2 * n_pairs

    def body(q_ref, k_ref, v_ref, do_ref, o_ref, lse_ref, c_ref, sp_ref, sm_ref, dq_ref, dk_ref, dv_ref,
             dqt_scr, dsum_scr):
        lane = lax.broadcasted_iota(jnp.int32, (tk, LANES), 1)
        sel_lane = lax.broadcasted_iota(jnp.int32, (8, LANES), 1)
        sel_row = lax.broadcasted_iota(jnp.int32, (8, LANES), 0)
        sel = jnp.where((sel_lane < B_VDIM) == (sel_row == 0), 1.0, 0.0)
        sel = jnp.where(sel_row < 2, sel, 0.0).astype(BF16)

        def rows(blk):
            return pl.ds(pl.multiple_of(blk * tq, tq), tq)

        def dsum_step(qb, carry):
            for pp in range(n_pairs):
                pls = slice(pp * LANES, (pp + 1) * LANES)
                prod = do_ref[rows(qb), pls].astype(F32) * o_ref[rows(qb), pls]
                hi = prod.astype(BF16)
                lo = (prod - hi.astype(F32)).astype(BF16)
                dsum_scr[pp, :, rows(qb)] = _dot_nt(sel, hi) + _dot_nt(sel, lo)
            return carry

        lax.fori_loop(0, nq, dsum_step, 0)
        dqt_scr[...] = jnp.zeros_like(dqt_scr)

        def kv_step(kb, carry):
            ks = [k_ref[rows(kb), hh * LANES:(hh + 1) * LANES] for hh in range(n_heads)]
            vs = []
            for pp in range(n_pairs):
                vb = v_ref[rows(kb), pp * LANES:(pp + 1) * LANES]
                zero = jnp.zeros_like(vb)
                vs += [jnp.where(lane < B_VDIM, vb, zero), jnp.where(lane < B_VDIM, zero, vb)]

            def make_step(masked):
                def step(qb, acc):
                    qs = [q_ref[rows(qb), hh * LANES:(hh + 1) * LANES] for hh in range(n_heads)]
                    dob = [do_ref[rows(qb), pp * LANES:(pp + 1) * LANES] for pp in range(n_pairs)]
                    ss = [_dot_nt(ks[hh], qs[hh]) for hh in range(n_heads)]
                    dps = [_dot_nt(vs[hh], dob[hh // 2]) for hh in range(n_heads)]
                    pbs, dss = [], []
                    for hh in range(n_heads):
                        stat = (hh // 2, slice(hh % 2, hh % 2 + 1), rows(qb))
                        s = ss[hh] * B_C2
                        if masked:
                            s = jnp.where(_key_le_query(kb, qb, tk, tq), s, NEG)
                        p = jnp.exp2(s - lse_ref[stat])
                        dss.append((p * (dps[hh] - dsum_scr[stat]) * B_SCALE).astype(BF16))
                        pbs.append(p.astype(BF16))
                    for hh in range(n_heads):
                        dqt_scr[hh, :, rows(qb)] += _dot_tn(ks[hh], dss[hh])
                    return tuple([acc[hh] + _dot(dss[hh], qs[hh]) for hh in range(n_heads)]
                                 + [acc[n_heads + hh] + _dot(pbs[hh], dob[hh // 2]) for hh in range(n_heads)])
                return step

            acc = make_step(True)(kb, (jnp.zeros((tk, LANES), F32),) * (2 * n_heads))
            acc = lax.fori_loop(kb + 1, nq, make_step(False), acc)
            for hh in range(n_heads):
                dk_ref[rows(kb), hh * LANES:(hh + 1) * LANES] = acc[hh].astype(BF16)
            for pp in range(n_pairs):
                dv_pair = jnp.where(lane < B_VDIM, acc[n_heads + 2 * pp], acc[n_heads + 2 * pp + 1])
                dv_ref[rows(kb), pp * LANES:(pp + 1) * LANES] = dv_pair.astype(BF16)
            return carry

        lax.fori_loop(0, nq, kv_step, 0)

        def dq_step(qb, carry):
            c, sp, sm = c_ref[rows(qb), :], sp_ref[rows(qb), :], sm_ref[rows(qb), :]
            for hh in range(n_heads):
                dq_ref[rows(qb), hh * LANES:(hh + 1) * LANES] = _rope_t(dqt_scr[hh, :, rows(qb)].T, c, sp, sm).astype(BF16)
            return carry

        lax.fori_loop(0, nq, dq_step, 0)

    pair_full = pl.BlockSpec((None, t, n_heads * LANES), lambda b, j: (b, 0, j))
    one_full = pl.BlockSpec((None, t, n_pairs * LANES), lambda b, j: (b, 0, j))
    row_full = pl.BlockSpec((None, n_pairs, 8, t), lambda b, j: (b, j, 0, 0))
    tab_full = pl.BlockSpec((None, t, LANES), lambda b, j: (b, 0, 0))
    return pl.pallas_call(
        body, name="attn_b_bwd", grid=(bsz, B_HEADS // n_heads),
        in_specs=[pair_full, pair_full, one_full, one_full, one_full, row_full, tab_full, tab_full, tab_full],
        out_specs=[pair_full, pair_full, one_full],
        out_shape=[jax.ShapeDtypeStruct((bsz, t, B_KPAD), BF16), jax.ShapeDtypeStruct((bsz, t, B_KPAD), BF16),
                   jax.ShapeDtypeStruct((bsz, t, B_HEADS * B_VDIM), BF16)],
        scratch_shapes=[pltpu.VMEM((n_heads, LANES, t), F32), pltpu.VMEM((n_pairs, 8, t), F32)],
        compiler_params=_cparams(("parallel", "parallel")))(q, kpad, v, do, o, lse2, *tabs)


ANY = pl.BlockSpec(memory_space=pl.ANY)


def _all_gather(shard, name):
    def body(x_ref, out_ref, send_sems, recv_sems, local_sem):
        x, y, c = lax.axis_index("x"), lax.axis_index("y"), lax.axis_index("c")
        me, sibling = (x, y, c), (x, y, 1 - c)
        chips = [(1 - x, y), (x, 1 - y), (1 - x, 1 - y)]

        def rows(px, py, pc):
            return out_ref.at[4 * px + 2 * py + pc]

        def copy(k, block, to, src=None):
            return pltpu.make_async_remote_copy(
                src_ref=rows(*block) if src is None else src, dst_ref=rows(*block),
                send_sem=send_sems.at[k], recv_sem=recv_sems.at[k], device_id=to, device_id_type=MESH)

        mine = pltpu.make_async_copy(x_ref, rows(*me), local_sem)
        mine.start()
        first = [copy(0, me, sibling, src=x_ref)]
        first += [copy(1 + j, me, (*chip, c), src=x_ref) for j, chip in enumerate(chips)]
        for cp in first:
            cp.start()
        passed = [copy(4 + j, (*chip, c), sibling) for j, chip in enumerate(chips)]
        for j, chip in enumerate(chips):
            copy(1 + j, (*chip, c), me).wait_recv()
            passed[j].start()
        copy(0, sibling, me).wait_recv()
        for j, chip in enumerate(chips):
            copy(4 + j, (*chip, 1 - c), me).wait_recv()
        for cp in first + passed:
            cp.wait_send()
        mine.wait()

    return pl.pallas_call(
        body, name=name, in_specs=[ANY], out_specs=ANY,
        out_shape=jax.ShapeDtypeStruct((N_DEV,) + shard.shape, shard.dtype),
        scratch_shapes=[pltpu.SemaphoreType.DMA((7,)), pltpu.SemaphoreType.DMA((7,)), pltpu.SemaphoreType.DMA])(shard)


def _all_gather_weights(flat, wide, name):
    ns = wide.shape[1]

    def body(f_ref, w_ref, fo_ref, wo_ref, send_sems, recv_sems, local_sems):
        x, y, c = lax.axis_index("x"), lax.axis_index("y"), lax.axis_index("c")
        me, sibling = (x, y, c), (x, y, 1 - c)
        chips = [(1 - x, y), (x, 1 - y), (1 - x, 1 - y)]

        def place(a, px, py, pc):
            idx = 4 * px + 2 * py + pc
            if a == 0:
                return fo_ref.at[idx]
            return wo_ref.at[:, pl.ds(pl.multiple_of(idx * ns, LANES), ns)]

        def copy(a, k, block, to, src=None):
            return pltpu.make_async_remote_copy(
                src_ref=place(a, *block) if src is None else src, dst_ref=place(a, *block),
                send_sem=send_sems.at[a, k], recv_sem=recv_sems.at[a, k], device_id=to, device_id_type=MESH)

        relay_in = (c * x + (1 - c) * (1 - x), c * (1 - y) + (1 - c) * y)
        relay_out = (c * (1 - x) + (1 - c) * x, c * y + (1 - c) * (1 - y))
        own = (f_ref, w_ref)
        mine = [pltpu.make_async_copy(own[a], place(a, *me), local_sems.at[a]) for a in range(2)]
        first = []
        for a in range(2):
            mine[a].start()
            first.append(copy(a, 0, me, sibling, src=own[a]))
            first += [copy(a, 1 + j, me, (*chip, c), src=own[a]) for j, chip in enumerate(chips[:2])]
        for cp in first:
            cp.start()
        relay = [copy(a, 3, (*relay_in, c), (*relay_out, c)) for a in range(2)]
        passed = [[copy(a, 4 + j, (*chip, c), sibling) for j, chip in enumerate(chips)] for a in range(2)]
        for a in range(2):
            for j, chip in enumerate(chips[:2]):
                copy(a, 1 + j, (*chip, c), me).wait_recv()
                passed[a][j].start()
            relay[a].start()
        for a in range(2):
            copy(a, 3, (*chips[2], c), me).wait_recv()
            passed[a][2].start()
        for a in range(2):
            copy(a, 0, sibling, me).wait_recv()
            for j, chip in enumerate(chips):
                copy(a, 4 + j, (*chip, 1 - c), me).wait_recv()
        for cp in first + relay + passed[0] + passed[1]:
            cp.wait_send()
        for cp in mine:
            cp.wait()

    return pl.pallas_call(
        body, name=name, in_specs=[ANY, ANY], out_specs=[ANY, ANY],
        out_shape=[jax.ShapeDtypeStruct((N_DEV,) + flat.shape, flat.dtype),
                   jax.ShapeDtypeStruct((wide.shape[0], N_DEV * ns), wide.dtype)],
        scratch_shapes=[pltpu.SemaphoreType.DMA((2, 7)), pltpu.SemaphoreType.DMA((2, 7)),
                        pltpu.SemaphoreType.DMA((2,))])(flat, wide)


N_CHIPS = 4


def _exchange_sibling(blocks, name):
    def body(g_ref, got_ref, send_sems, recv_sems):
        x, y, c = lax.axis_index("x"), lax.axis_index("y"), lax.axis_index("c")
        sends = [pltpu.make_async_remote_copy(
            src_ref=g_ref.at[q, 1 - c], dst_ref=got_ref.at[q], send_sem=send_sems.at[q],
            recv_sem=recv_sems.at[q], device_id=(x, y, 1 - c), device_id_type=MESH) for q in range(N_CHIPS)]
        for cp in sends:
            cp.start()
        for cp in sends:
            cp.wait_recv()
        for cp in sends:
            cp.wait_send()

    return pl.pallas_call(
        body, name=name, in_specs=[ANY], out_specs=ANY,
        out_shape=jax.ShapeDtypeStruct((N_CHIPS,) + blocks.shape[2:], blocks.dtype),
        scratch_shapes=[pltpu.SemaphoreType.DMA((N_CHIPS,)), pltpu.SemaphoreType.DMA((N_CHIPS,))])(blocks)


def _exchange_chips(parts, name):
    def body(p_ref, out_ref, send_sems, recv_sems, local_sem):
        x, y, c = lax.axis_index("x"), lax.axis_index("y"), lax.axis_index("c")
        me = 2 * x + y

        def peer(k):
            return (1 - x if k & 2 else x, 1 - y if k & 1 else y)

        def copy(k):
            px, py = peer(k)
            return pltpu.make_async_remote_copy(
                src_ref=p_ref.at[2 * px + py], dst_ref=out_ref.at[me], send_sem=send_sems.at[k - 1],
                recv_sem=recv_sems.at[k - 1], device_id=(px, py, c), device_id_type=MESH)

        def arrival(k):
            px, py = peer(k)
            slot = out_ref.at[2 * px + py]
            return pltpu.make_async_remote_copy(
                src_ref=slot, dst_ref=slot, send_sem=send_sems.at[k - 1], recv_sem=recv_sems.at[k - 1],
                device_id=(px, py, c), device_id_type=MESH)

        mine = pltpu.make_async_copy(p_ref.at[me], out_ref.at[me], local_sem)
        mine.start()
        sends = [copy(k) for k in range(1, N_CHIPS)]
        for cp in sends:
            cp.start()
        for k in range(1, N_CHIPS):
            arrival(k).wait_recv()
        for cp in sends:
            cp.wait_send()
        mine.wait()

    return pl.pallas_call(
        body, name=name, in_specs=[ANY], out_specs=ANY,
        out_shape=jax.ShapeDtypeStruct(parts.shape, parts.dtype),
        scratch_shapes=[pltpu.SemaphoreType.DMA((N_CHIPS - 1,)), pltpu.SemaphoreType.DMA((N_CHIPS - 1,)),
                        pltpu.SemaphoreType.DMA])(parts)


def _add_pairs(blocks, got, core, *, tr, name):
    q, r, c = got.shape

    def body(core_ref, a_ref, b_ref, o_ref):
        o_ref[...] = (a_ref[...].astype(F32) + b_ref[...].astype(F32)).astype(o_ref.dtype)

    spec = pl.BlockSpec((q, tr, c), lambda i, core_ref: (0, i, 0))
    mine = pl.BlockSpec((q, None, tr, c), lambda i, core_ref: (0, core_ref[0], i, 0))
    return pl.pallas_call(
        body, name=name,
        grid_spec=pltpu.PrefetchScalarGridSpec(num_scalar_prefetch=1, grid=(r // tr,), in_specs=[mine, spec],
                                               out_specs=spec),
        out_shape=jax.ShapeDtypeStruct(got.shape, BF16), compiler_params=_cparams(("parallel",)))(
            core, blocks, got)


def _sum_slots(slots, *, tr, name):
    n_slots, r, c = slots.shape

    def body(s_ref, o_ref):
        acc = s_ref[0].astype(F32)
        for s in range(1, n_slots):
            acc = acc + s_ref[s].astype(F32)
        o_ref[...] = acc

    return pl.pallas_call(
        body, name=name, grid=(r // tr,),
        in_specs=[pl.BlockSpec((n_slots, tr, c), lambda i: (0, i, 0))],
        out_specs=pl.BlockSpec((tr, c), lambda i: (i, 0)),
        out_shape=jax.ShapeDtypeStruct((r, c), F32),
        compiler_params=_cparams(("parallel",)))(slots)


def _adamw_math(w_t, g_t, m_t, v_t):
    m_n = ADAM_B1 * m_t + (1.0 - ADAM_B1) * g_t
    v_n = ADAM_B2 * v_t + (1.0 - ADAM_B2) * (g_t * g_t)
    m_hat = m_n / (1.0 - ADAM_B1 ** ADAM_STEP)
    v_hat = v_n / (1.0 - ADAM_B2 ** ADAM_STEP)
    delta = -ADAM_LR * (m_hat / (jnp.sqrt(v_hat) + ADAM_EPS) + ADAM_WD * w_t)
    return delta, m_n, v_n


def _adamw(w, g, m, v, name):
    shape = w.shape
    cols = shape[-1]
    args = [a.reshape(-1, cols) for a in (w, g, m, v)]
    rows = args[0].shape[0]
    tm = 256 if rows % 256 == 0 else rows
    delta, m_n, v_n = _rowwise(_adamw_math, args, [], [(cols, F32)] * 3, tm=tm, name=name)
    return delta.reshape(shape), m_n.reshape(shape), v_n.reshape(shape)


def _adamw_small(ws, gs, ms, vs, name):
    k = len(ws)
    args = [a.reshape(-1, a.shape[-1]) for group in (ws, gs, ms, vs) for a in group]

    def body(*refs):
        ins, outs = refs[:4 * k], refs[4 * k:]
        for i in range(k):
            res = _adamw_math(*[ins[j * k + i][...] for j in range(4)])
            for j in range(3):
                outs[j * k + i][...] = res[j]

    res = pl.pallas_call(
        body, name=name, out_shape=[jax.ShapeDtypeStruct(a.shape, F32) for a in args[:k]] * 3,
        compiler_params=pltpu.CompilerParams(vmem_limit_bytes=VMEM_LIMIT))(*args)
    return [[res[j * k + i].reshape(ws[i].shape) for i in range(k)] for j in range(3)]


def _local_step(x, positions, target, w):
    bsz, t, _ = x.shape
    n = bsz * t
    tm = 512
    mm = 512
    x2 = x.reshape(n, D_MODEL)
    tgt2 = target.reshape(n, D_MODEL)
    pos = positions.reshape(n)
    tb2 = _rope_tables(pos, B_ROPE_THETA, B_ROPE, B_NOPE, "rope_tables_b")
    tabs_b = [a.reshape(bsz, t, LANES) for a in tb2]

    w_a_in = w["a_w_in"]
    w_a_out = w["a_w_out"]
    w_down = w["kv_w_down"]
    w_down_p = jnp.zeros((D_MODEL, 3 * LANES), BF16).at[:, :B_KV_LORA].set(w_down[:, :B_KV_LORA])
    w_down_p = w_down_p.at[:, B_KV_LORA + B_NOPE:B_KV_LORA + B_QK_DIM].set(w_down[:, B_KV_LORA:])
    wu = w["kv_w_up"].reshape(B_KV_LORA, B_HEADS, B_NOPE + B_VDIM)
    w_upk = jnp.pad(wu[:, :, :B_NOPE], ((0, 0), (0, 0), (0, LANES - B_NOPE))).reshape(B_KV_LORA, B_KPAD)
    w_upv = wu[:, :, B_NOPE:].reshape(B_KV_LORA, B_HEADS * B_VDIM)
    w_b_in = w["b_w_in"]
    w_q_p = jnp.pad(w["b_w_q_up"].reshape(B_Q_LORA, B_HEADS, B_QK_DIM),
                    ((0, 0), (0, 0), (0, LANES - B_QK_DIM))).reshape(B_Q_LORA, B_KPAD)
    w_b_out = w["b_w_out"]

    def tab_extras(tabs2, rows):
        return [(a, (rows, LANES), lambda j, i, kk: (i, 0)) for a in tabs2]

    (hn_a,) = _rowwise(lambda xt, g: (_rms(xt, g),), [x2], [w["a_pre_norm"]], [(D_MODEL, BF16)],
                       tm=tm, name="a_pre_norm")

    def rope_epilogue(acc, o_ref, rows, c_ref, sp_ref, sm_ref):
        c, sp, sm = c_ref[rows, :], sp_ref[rows, :], sm_ref[rows, :]
        for h in range(acc.shape[1] // LANES):
            hs = slice(h * LANES, (h + 1) * LANES)
            o_ref[rows, hs] = _rope(acc[:, hs], c, sp, sm).astype(BF16)

    def to_group(a, d):
        if d == 1:
            return a
        return a.reshape(bsz, t // d, d, a.shape[-1]).transpose(0, 2, 1, 3).reshape(n, a.shape[-1])

    def from_group(a, d):
        if d == 1:
            return a
        return a.reshape(bsz, d, t // d, a.shape[-1]).transpose(0, 2, 1, 3).reshape(n, a.shape[-1])

    def rows_to_cols(r, d):
        return r.reshape(bsz, d, A_HEADS, t // d).transpose(0, 3, 1, 2).reshape(n, A_HEADS)

    def cols_to_rows(cc, d):
        return cc.reshape(bsz, t // d, d, A_HEADS).transpose(0, 2, 3, 1).reshape(bsz * d, A_HEADS, t // d)

    z_a = _matmul(hn_a, w_a_in, b_cols=(A_QKV, A_WIDTH), out_dtype=F32, tm=mm, tn=A_WIDTH, name="a_gate")
    hn_g, tabs_g, qk_g, v_g, o_g, lse_g = [], [], [], [], [], []
    for g, d in enumerate(A_DILATIONS):
        hn_g.append(to_group(hn_a, d))
        tabs_g.append(_rope_tables(to_group(pos.reshape(n, 1), d).reshape(n), A_ROPE_THETA, A_ROT_DIM, 0,
                                   f"rope_tables_a_g{g}"))
        col0 = g * 3 * A_WIDTH
        qk = _matmul(hn_g[g], w_a_in, b_cols=(col0, 2 * A_WIDTH), out_dtype=BF16, tm=mm, tn=A_WIDTH,
                     name=f"a_qk_g{g}", epilogue=rope_epilogue, epilogue_rows=EPILOGUE_ROWS,
                     extras=tab_extras(tabs_g[g], mm))
        v = _matmul(hn_g[g], w_a_in, b_cols=(col0 + 2 * A_WIDTH, A_WIDTH), out_dtype=BF16, tm=mm, tn=A_WIDTH,
                    name=f"a_v_g{g}")
        qk_g.append(qk.reshape(bsz * d, t // d, 2 * A_WIDTH))
        v_g.append(v.reshape(bsz * d, t // d, A_WIDTH))
        o, lse = _attn_a_fwd(qk_g[g], v_g[g], f"attn_a_fwd_g{g}")
        o_g.append(from_group(o.reshape(n, A_WIDTH), d))
        lse_g.append(rows_to_cols(lse, d))

    def merge_fn(o0, o1, o2, l0, l1, l2, z):
        lmax = jnp.maximum(jnp.maximum(l0, l1), l2)
        e0, e1, e2 = jnp.exp2(l0 - lmax), jnp.exp2(l1 - lmax), jnp.exp2(l2 - lmax)
        den = e0 + e1 + e2
        w0, w1, w2 = e0 / den, e1 / den, e2 / den
        parts = []
        for h in range(A_HEADS):
            hs = slice(h * A_HEAD_DIM, (h + 1) * A_HEAD_DIM)
            parts.append(w0[:, h:h + 1] * o0[:, hs] + w1[:, h:h + 1] * o1[:, hs] + w2[:, h:h + 1] * o2[:, hs])
        o = jnp.concatenate(parts, axis=1)
        return o * _silu(z), o, lmax + jnp.log2(den)

    y_a, o_a2, lse_a = _rowwise(merge_fn, [*o_g, *lse_g, z_a], [],
                                [(A_WIDTH, BF16), (A_WIDTH, F32), (A_HEADS, F32)], tm=tm, name="a_merge_gate")
    out_a = _matmul(y_a, w_a_out, out_dtype=F32, tm=mm, tn=D_MODEL, name="a_out")

    def mid_fn(xt, out, g_post, g_kv, g_b):
        h1 = xt + _rms(out, g_post)
        return h1, _rms(h1, g_kv), _rms(h1, g_b)

    h1, hn_kv, hn_b = _rowwise(mid_fn, [x2, out_a], [w["a_post_norm"], w["kv_norm"], w["b_pre_norm"]],
                               [(D_MODEL, F32), (D_MODEL, BF16), (D_MODEL, BF16)], tm=tm, name="a_post_norm")

    ckr = _matmul(hn_kv, w_down_p, out_dtype=F32, tm=mm, tn=3 * LANES, name="kv_down")

    def latent_fn(ck, c, sp, sm, g):
        return _rms(ck[:, :B_KV_LORA], g), _rope(ck[:, B_KV_LORA:], c, sp, sm)

    c_kv, k_rope = _rowwise(latent_fn, [ckr, *tb2], [w["kv_latent_norm"]], [(B_KV_LORA, BF16), (LANES, F32)],
                            tm=tm, name="kv_latent_norm")

    def kpad_epilogue(acc, o_ref, rows, kr_ref):
        kr = kr_ref[rows, :]
        for h in range(acc.shape[1] // LANES):
            hs = slice(h * LANES, (h + 1) * LANES)
            o_ref[rows, hs] = (acc[:, hs] + kr).astype(BF16)

    kpad = _matmul(c_kv, w_upk, out_dtype=BF16, tm=mm, tn=1024, name="kv_up_k", epilogue=kpad_epilogue,
                   epilogue_rows=EPILOGUE_ROWS,
                   extras=[(k_rope, (mm, LANES), lambda j, i, kk: (i, 0))])
    v_b = _matmul(c_kv, w_upv, out_dtype=BF16, tm=mm, tn=1024, name="kv_up_v")

    proj_b = _matmul(hn_b, w_b_in, out_dtype=F32, tm=mm, tn=w_b_in.shape[1], name="b_in")
    (c_q,) = _rowwise(lambda p, g: (_rms(p[:, :B_Q_LORA], g),), [proj_b], [w["b_q_norm"]], [(B_Q_LORA, BF16)],
                      tm=tm, name="b_q_norm")

    q_b = _matmul(c_q, w_q_p, out_dtype=BF16, tm=mm, tn=1024, name="b_q_up", epilogue=rope_epilogue,
                  epilogue_rows=EPILOGUE_ROWS,
                  extras=tab_extras(tb2, mm))
    q_b3, kpad3, v_b3 = q_b.reshape(bsz, t, B_KPAD), kpad.reshape(bsz, t, B_KPAD), v_b.reshape(bsz, t, -1)
    w_vt = jnp.pad(wu[:, :, B_NOPE:], ((0, 0), (0, 0), (0, LANES - B_VDIM))).reshape(B_KV_LORA, B_KPAD)
    vt1 = _value_heads_t(c_kv, w_vt, bsz, t)
    o_b, lse_b = _attn_b_fwd(q_b3, kpad3, vt1)
    o_b2 = o_b.reshape(n, -1)
    (y_b,) = _rowwise(lambda o, p: (o * _silu(p[:, B_Q_LORA:]),), [o_b2, proj_b], [], [(D_MODEL, BF16)],
                      tm=tm, name="b_gate_mul")
    out_b = _matmul(y_b, w_b_out, out_dtype=F32, tm=mm, tn=D_MODEL, name="b_out")

    def head_fn(h1t, out, tgt, g):
        e = h1t + _rms(out, g) - tgt
        loss_row = 0.5 * jnp.mean(e * e, axis=-1, keepdims=True)
        dh2 = e * (1.0 / D_MODEL)
        d_out, dg = _rms_bwd(out, g, dh2)
        return dh2, d_out, dg, jnp.broadcast_to(loss_row * (1.0 / LANES), (loss_row.shape[0], LANES))

    dh2, d_out_b, dg_b_post, loss_acc = _rowwise(
        head_fn, [h1, out_b, tgt2], [w["b_post_norm"]], [(D_MODEL, F32), (D_MODEL, BF16)], [D_MODEL, LANES],
        tm=tm, name="loss_head")

    dy_b = _matmul(d_out_b, w_b_out, tb=True, out_dtype=BF16, tm=mm, tn=D_MODEL, name="b_out_dx")
    gw_b_out = _matmul(y_b, d_out_b, ta=True, out_dtype=BF16, tm=mm, tn=D_MODEL, tk=2048, name="b_out_dw")

    def gate_b_bwd(dy, o, p):
        z = p[:, B_Q_LORA:]
        return dy * _silu(z), dy * o * _silu_grad(z)

    do_b, dz_b = _rowwise(gate_b_bwd, [dy_b, o_b2, proj_b], [], [(D_MODEL, BF16), (D_MODEL, F32)],
                          tm=tm, name="b_gate_bwd")
    do_b3 = do_b.reshape(bsz, t, -1)
    dq_b, dk_b, dv_b = _attn_b_bwd(q_b3, kpad3, v_b3, do_b3, o_b, lse_b, tabs_b)
    dq_b2, dk_b2, dv_b2 = dq_b.reshape(n, B_KPAD), dk_b.reshape(n, B_KPAD), dv_b.reshape(n, -1)

    dc_kv = _matmul(dk_b2, w_upk, tb=True, out_dtype=F32, tm=mm, tn=B_KV_LORA, name="kv_up_k_dx")
    dc_kv = _matmul(dv_b2, w_upv, tb=True, out_dtype=BF16, tm=mm, tn=B_KV_LORA, name="kv_up_v_dx",
                    epilogue=_add_epilogue, extras=[(dc_kv, (mm, B_KV_LORA), lambda j, i, kk: (i, j))])
    gw_upk = _matmul(c_kv, dk_b2, ta=True, out_dtype=BF16, tm=B_KV_LORA, tn=1024, tk=2048, name="kv_up_k_dw")
    gw_upv = _matmul(c_kv, dv_b2, ta=True, out_dtype=BF16, tm=B_KV_LORA, tn=1024, tk=2048, name="kv_up_v_dw")

    def latent_bwd(ck, dck, dk, c, sp, sm, g):
        d1, dg = _rms_bwd(ck[:, :B_KV_LORA], g, dck)
        ksum = dk[:, :LANES].astype(F32)
        for h in range(1, B_HEADS):
            ksum = ksum + dk[:, h * LANES:(h + 1) * LANES].astype(F32)
        lane = lax.broadcasted_iota(jnp.int32, ksum.shape, 1)
        ksum = jnp.where((lane >= B_NOPE) & (lane < B_QK_DIM), ksum, 0.0)
        return jnp.concatenate([d1, _rope_t(ksum, c, sp, sm)], axis=1), dg

    dckr, dg_latent = _rowwise(latent_bwd, [ckr, dc_kv, dk_b2, *tb2], [w["kv_latent_norm"]],
                               [(3 * LANES, BF16)], [B_KV_LORA], tm=tm, name="kv_latent_bwd")
    dhn_kv = _matmul(dckr, w_down_p, tb=True, out_dtype=BF16, tm=mm, tn=D_MODEL, name="kv_down_dx")
    gw_down_p = _matmul(hn_kv, dckr, ta=True, out_dtype=BF16, tm=mm, tn=3 * LANES, tk=2048, name="kv_down_dw")

    dc_q = _matmul(dq_b2, w_q_p, tb=True, out_dtype=BF16, tm=mm, tn=B_Q_LORA, name="b_q_up_dx")
    gw_q_p = _matmul(c_q, dq_b2, ta=True, out_dtype=BF16, tm=B_Q_LORA, tn=1024, tk=2048, name="b_q_up_dw")

    def q_norm_bwd(p, dcq, dz, g):
        d1, dg = _rms_bwd(p[:, :B_Q_LORA], g, dcq)
        return jnp.concatenate([d1, dz], axis=1), dg

    dproj_b, dg_q_norm = _rowwise(q_norm_bwd, [proj_b, dc_q, dz_b], [w["b_q_norm"]],
                                  [(w_b_in.shape[1], BF16)], [B_Q_LORA], tm=tm, name="b_q_norm_bwd")
    dhn_b = _matmul(dproj_b, w_b_in, tb=True, out_dtype=BF16, tm=mm, tn=D_MODEL, name="b_in_dx")
    gw_b_in = _matmul(hn_b, dproj_b, ta=True, out_dtype=BF16, tm=mm, tn=w_b_in.shape[1], tk=2048, name="b_in_dw")

    def mid_bwd(h1t, dh2t, dkv, db, g_kv, g_b, g_post, out):
        rstd = lax.rsqrt(jnp.mean(h1t * h1t, axis=-1, keepdims=True) + NORM_EPS)
        xhat = h1t * rstd
        dxhat = dkv * g_kv + db * g_b
        dh1 = dh2t + rstd * (dxhat - xhat * jnp.mean(dxhat * xhat, axis=-1, keepdims=True))
        d_out, rp = _rms_bwd(out, g_post, dh1)
        return dh1, d_out, dkv * xhat, db * xhat, rp

    def mid_bwd_fn(h1t, dh2t, dkv, db, out, g_kv, g_b, g_post):
        return mid_bwd(h1t, dh2t, dkv, db, g_kv, g_b, g_post, out)

    dh1, d_out_a, dg_kv, dg_b_pre, dg_a_post = _rowwise(
        mid_bwd_fn, [h1, dh2, dhn_kv, dhn_b, out_a], [w["kv_norm"], w["b_pre_norm"], w["a_post_norm"]],
        [(D_MODEL, F32), (D_MODEL, BF16)], [D_MODEL] * 3, tm=tm, name="mid_bwd")

    dy_a = _matmul(d_out_a, w_a_out, tb=True, out_dtype=BF16, tm=mm, tn=A_WIDTH, name="a_out_dx")
    gw_a_out = _matmul(y_a, d_out_a, ta=True, out_dtype=BF16, tm=mm, tn=D_MODEL, tk=2048, name="a_out_dw")

    def gate_a_bwd(dy, o, z):
        do = dy * _silu(z)
        prod = do * o
        lane = lax.broadcasted_iota(jnp.int32, (prod.shape[0], A_HEADS), 1)
        dsum = jnp.zeros((prod.shape[0], A_HEADS), F32)
        for h in range(A_HEADS):
            col = jnp.sum(prod[:, h * A_HEAD_DIM:(h + 1) * A_HEAD_DIM], axis=1, keepdims=True)
            dsum = jnp.where(lane == h, col, dsum)
        return do, dy * o * _silu_grad(z), dsum

    do_a, dz_a, dsum_a = _rowwise(gate_a_bwd, [dy_a, o_a2, z_a], [],
                                  [(A_WIDTH, BF16), (A_WIDTH, BF16), (A_HEADS, F32)], tm=tm, name="a_gate_bwd")
    dhn_a = _matmul(dz_a, w_a_in, b_cols=(A_QKV, A_WIDTH), tb=True, out_dtype=F32, tm=mm, tn=D_MODEL,
                    name="a_gate_dx")
    gw_a_in, dhn_groups = None, []
    for g, d in enumerate(A_DILATIONS):
        s_n, ln = bsz * d, t // d
        dqkv = _attn_a_bwd(qk_g[g], v_g[g], to_group(do_a, d).reshape(s_n, ln, A_WIDTH), cols_to_rows(lse_a, d),
                           cols_to_rows(dsum_a, d), [a.reshape(s_n, ln, LANES) for a in tabs_g[g]],
                           f"attn_a_bwd_g{g}").reshape(n, 3 * A_WIDTH)
        w_cols = (g * 3 * A_WIDTH, 3 * A_WIDTH)
        if d == 1:
            dhn_a = _matmul(dqkv, w_a_in, b_cols=w_cols, tb=True, out_dtype=F32, tm=mm, tn=D_MODEL,
                            name=f"a_qkv_dx_g{g}", epilogue=_add_epilogue,
                            extras=[(dhn_a, (mm, D_MODEL), lambda j, i, kk: (i, j))])
        else:
            dhn_groups.append(from_group(_matmul(dqkv, w_a_in, b_cols=w_cols, tb=True, out_dtype=BF16, tm=mm,
                                                 tn=D_MODEL, name=f"a_qkv_dx_g{g}"), d))
        gw_a_in = _matmul(hn_g[g], dqkv, ta=True, out_dtype=BF16, tm=mm, tn=1024, tk=2048, name=f"a_qkv_dw_g{g}",
                          out_into=(gw_a_in, g * 3 * A_WIDTH, A_QKV + A_WIDTH))
    gw_a_in = _matmul(hn_a, dz_a, ta=True, out_dtype=BF16, tm=mm, tn=1024, tk=2048, name="a_gate_dw",
                      out_into=(gw_a_in, A_QKV, A_QKV + A_WIDTH))

    def first_bwd(xt, dhn, dhn_1, dhn_2, dh1t, g):
        dx, dg = _rms_bwd(xt, g, dhn + dhn_1 + dhn_2)
        return dh1t + dx, dg

    grad_x, dg_a_pre = _rowwise(first_bwd, [x2, dhn_a, *dhn_groups, dh1], [w["a_pre_norm"]], [(D_MODEL, F32)],
                                [D_MODEL], tm=tm, name="a_pre_norm_bwd")

    gw_down = jnp.concatenate([gw_down_p[:, :B_KV_LORA], gw_down_p[:, B_KV_LORA + B_NOPE:B_KV_LORA + B_QK_DIM]], axis=1)
    gw_up = jnp.concatenate([gw_upk.reshape(B_KV_LORA, B_HEADS, LANES)[:, :, :B_NOPE],
                             gw_upv.reshape(B_KV_LORA, B_HEADS, B_VDIM)], axis=2).reshape(B_KV_LORA, -1)
    gw_q_up = gw_q_p.reshape(B_Q_LORA, B_HEADS, LANES)[:, :, :B_QK_DIM].reshape(B_Q_LORA, -1)
    grads = {"a_w_in": gw_a_in, "a_w_out": gw_a_out, "kv_w_down": gw_down, "kv_w_up": gw_up,
             "b_w_in": gw_b_in, "b_w_q_up": gw_q_up, "b_w_out": gw_b_out}
    gains = {"a_pre_norm": dg_a_pre, "a_post_norm": dg_a_post, "kv_norm": dg_kv, "kv_latent_norm": dg_latent,
             "b_pre_norm": dg_b_pre, "b_q_norm": dg_q_norm, "b_post_norm": dg_b_post}
    gains = {k: jnp.sum(a, axis=0) for k, a in gains.items()}
    return jnp.sum(loss_acc), grad_x.reshape(bsz, t, D_MODEL), grads, gains


WEIGHT_ORDER = ("a_pre_norm", "a_w_in", "a_w_out", "a_post_norm", "kv_norm", "kv_w_down", "kv_latent_norm",
                "kv_w_up", "b_pre_norm", "b_w_in", "b_q_norm", "b_w_q_up", "b_w_out", "b_post_norm")
MATRICES = (("a_w_in", 1024, 10240, 1), ("a_w_out", 1024, 1024, 0), ("kv_w_down", 1024, 288, 0),
            ("kv_w_up", 256, 2048, 1), ("b_w_in", 1024, 1408, 1), ("b_w_q_up", 384, 1536, 1),
            ("b_w_out", 1024, 1024, 0))
SHARDED_GAINS = ("a_pre_norm", "a_post_norm")
GAIN_WIDTHS = (("a_pre_norm", 1024), ("a_post_norm", 1024), ("kv_norm", 1024), ("kv_latent_norm", 256),
               ("b_pre_norm", 1024), ("b_q_norm", 384), ("b_post_norm", 1024))
GAIN_ROWS = 48


def _shard_rows(rows, cols):
    return rows * cols // (N_DEV * LANES)


def _whole_from_blocks(blocks, rows, cols, axis):
    if axis == 1:
        return blocks.reshape(N_DEV, rows, cols // N_DEV).transpose(1, 0, 2).reshape(rows, cols)
    return blocks.reshape(rows, cols)


def _blocks_from_whole(whole, rows, cols, axis):
    if axis == 1:
        whole = whole.reshape(rows, N_DEV, cols // N_DEV).transpose(1, 0, 2)
    return whole.reshape(N_DEV, -1, LANES)


def kernel(x, positions, a_pre_norm, a_w_in, a_w_out, a_post_norm, kv_norm, kv_w_down, kv_latent_norm, kv_w_up, b_pre_norm, b_w_in, b_q_norm, b_w_q_up, b_w_out, b_post_norm, loss_target, m_a_pre_norm, m_a_w_in, m_a_w_out, m_a_post_norm, m_kv_norm, m_kv_w_down, m_kv_latent_norm, m_kv_w_up, m_b_pre_norm, m_b_w_in, m_b_q_norm, m_b_w_q_up, m_b_w_out, m_b_post_norm, v_a_pre_norm, v_a_w_in, v_a_w_out, v_a_post_norm, v_kv_norm, v_kv_w_down, v_kv_latent_norm, v_kv_w_up, v_b_pre_norm, v_b_w_in, v_b_q_norm, v_b_w_q_up, v_b_w_out, v_b_post_norm):
    weights = dict(a_pre_norm=a_pre_norm, a_w_in=a_w_in, a_w_out=a_w_out, a_post_norm=a_post_norm, kv_norm=kv_norm,
                   kv_w_down=kv_w_down, kv_latent_norm=kv_latent_norm, kv_w_up=kv_w_up, b_pre_norm=b_pre_norm,
                   b_w_in=b_w_in, b_q_norm=b_q_norm, b_w_q_up=b_w_q_up, b_w_out=b_w_out, b_post_norm=b_post_norm)
    m_in = dict(a_pre_norm=m_a_pre_norm, a_w_in=m_a_w_in, a_w_out=m_a_w_out, a_post_norm=m_a_post_norm,
                kv_norm=m_kv_norm, kv_w_down=m_kv_w_down, kv_latent_norm=m_kv_latent_norm, kv_w_up=m_kv_w_up,
                b_pre_norm=m_b_pre_norm, b_w_in=m_b_w_in, b_q_norm=m_b_q_norm, b_w_q_up=m_b_w_q_up,
                b_w_out=m_b_w_out, b_post_norm=m_b_post_norm)
    v_in = dict(a_pre_norm=v_a_pre_norm, a_w_in=v_a_w_in, a_w_out=v_a_w_out, a_post_norm=v_a_post_norm,
                kv_norm=v_kv_norm, kv_w_down=v_kv_w_down, kv_latent_norm=v_kv_latent_norm, kv_w_up=v_kv_w_up,
                b_pre_norm=v_b_pre_norm, b_w_in=v_b_w_in, b_q_norm=v_b_q_norm, b_w_q_up=v_b_w_q_up,
                b_w_out=v_b_w_out, b_post_norm=v_b_post_norm)
    me = 4 * lax.axis_index("x") + 2 * lax.axis_index("y") + lax.axis_index("c")

    wide = MATRICES[0][0]
    flat = jnp.concatenate([weights[name].astype(BF16).reshape(-1, LANES) for name, _, _, _ in MATRICES[1:]], axis=0)
    gathered, w_wide = _all_gather_weights(flat, weights[wide][0].astype(BF16), "gather_weights")
    whole = {wide: w_wide}
    off = 0
    for name, rows, cols, axis in MATRICES[1:]:
        nr = _shard_rows(rows, cols)
        whole[name] = _whole_from_blocks(gathered[:, off:off + nr], rows, cols, axis)
        off += nr
    gain_shard = jnp.concatenate([weights[name].reshape(1, LANES) for name in SHARDED_GAINS]
                                 + [jnp.zeros((8 - len(SHARDED_GAINS), LANES), F32)], axis=0)
    gain_blocks = _all_gather(gain_shard, "gather_gains")
    for i, name in enumerate(SHARDED_GAINS):
        whole[name] = gain_blocks[:, i, :].reshape(1, D_MODEL)
    for name in ("kv_norm", "kv_latent_norm", "b_pre_norm", "b_q_norm", "b_post_norm"):
        whole[name] = weights[name].reshape(1, -1)

    loss_part, grad_x, grads, gains = _local_step(x, positions, loss_target, whole)

    blocks = jnp.concatenate([_blocks_from_whole(grads[name], rows, cols, axis).astype(BF16)
                              for name, rows, cols, axis in MATRICES], axis=1)
    blocks = blocks.reshape(N_CHIPS, 2, blocks.shape[1], LANES)
    got = _exchange_sibling(blocks, "scatter_grads_core")
    core = lax.axis_index("c").astype(jnp.int32).reshape(1)
    landed = _exchange_chips(_add_pairs(blocks, got, core, tr=2512, name="add_core_grads"), "scatter_grads_chip")
    summed = _sum_slots(landed, tr=2512, name="sum_grads")
    grad_out = {}
    off = 0
    for name, rows, cols, axis in MATRICES:
        nr = _shard_rows(rows, cols)
        grad_out[name] = summed[off:off + nr].reshape(weights[name].shape)
        off += nr

    vec = jnp.concatenate([gains[name] for name, _ in GAIN_WIDTHS] + [jnp.full((LANES,), loss_part, F32)])
    vec = jnp.pad(vec, (0, GAIN_ROWS * LANES - vec.shape[0])).reshape(GAIN_ROWS, LANES)
    total = _sum_slots(_all_gather(vec, "gather_gain_grads"), tr=GAIN_ROWS, name="sum_gain_grads").reshape(-1)
    off = 0
    for name, width in GAIN_WIDTHS:
        g = total[off:off + width]
        if name in SHARDED_GAINS:
            g = lax.dynamic_slice(g, (me * LANES,), (LANES,))
        grad_out[name] = g.reshape(weights[name].shape)
        off += width
    loss = total[off]

    deltas, new_m, new_v = {}, {}, {}
    big = "a_w_in"
    deltas[big], new_m[big], new_v[big] = _adamw(weights[big], grad_out[big], m_in[big], v_in[big], "adamw_" + big)
    small = [name for name in WEIGHT_ORDER if name != big]
    res = _adamw_small(*[[d[name] for name in small] for d in (weights, grad_out, m_in, v_in)], "adamw_small")
    for out, vals in zip((deltas, new_m, new_v), res):
        out.update(zip(small, vals))
    return (loss, grad_x, *[grad_out[k] for k in WEIGHT_ORDER], *[deltas[k] for k in WEIGHT_ORDER],
            *[new_m[k] for k in WEIGHT_ORDER], *[new_v[k] for k in WEIGHT_ORDER])
```
